```python
import math
import jax, jax.numpy as jnp
from jax import lax
import numpy as np

D_MODEL = 1024
BATCH = 8
SEQ = 8192
DEPTH = 1

EPS = 1e-6
Q_BLOCK = 128
MLA_HEADS = 8
MLA_NOPE_DIM = 64
MLA_ROPE_DIM = 32
MLA_V_DIM = 64
Q_LORA_RANK = 256
KV_LORA_RANK = 128
ROPE_THETA = 10000.0
MLA_QK_DIM = MLA_NOPE_DIM + MLA_ROPE_DIM
SB_HEADS = 8
SB_HEAD_DIM = 64
MLA_WIDTH = MLA_HEADS * MLA_V_DIM
SB_WIDTH = SB_HEADS * SB_HEAD_DIM
MIX_WIDTH = MLA_WIDTH + SB_WIDTH
IN_SPLITS = (Q_LORA_RANK, KV_LORA_RANK, MLA_ROPE_DIM, SB_WIDTH, SB_WIDTH, SB_WIDTH)
IN_PROJ_WIDTH = sum(IN_SPLITS)
IN_SPLIT_POINTS = tuple(int(v) for v in np.cumsum(IN_SPLITS)[:-1])
D_FF = ((8 * D_MODEL + 3 * 256 - 1) // (3 * 256)) * 256

kernel_name = "hymba_mla_stickbreaking_swiglu"


def rmsnorm(x, g):
    xf = x.astype(jnp.float32)
    y = xf * lax.rsqrt(jnp.mean(xf * xf, axis=-1, keepdims=True) + EPS)
    return (y * g.astype(jnp.float32)).astype(x.dtype)


def rope_tables(positions, dim):
    inv_freq = ROPE_THETA ** (-jnp.arange(0, dim, 2, dtype=jnp.float32) / dim)
    ang = positions.astype(jnp.float32)[:, :, None] * inv_freq[None, None, :]
    return jnp.cos(ang)[:, None], jnp.sin(ang)[:, None]


def apply_rope(x, cos, sin):
    xf = x.astype(jnp.float32)
    x1, x2 = jnp.split(xf, 2, axis=-1)
    out = jnp.concatenate([x1 * cos - x2 * sin, x2 * cos + x1 * sin], axis=-1)
    return out.astype(x.dtype)


def to_query_blocks(q):
    b, h, s, d = q.shape
    return q.reshape(b, h, s // Q_BLOCK, Q_BLOCK, d).transpose(2, 0, 1, 3, 4)


def from_query_blocks(o):
    nb, b, h, qb, d = o.shape
    return o.transpose(1, 2, 0, 3, 4).reshape(b, h, nb * qb, d)


def mla_causal_attention(q, k, v):
    seq = q.shape[2]
    scale = 1.0 / math.sqrt(q.shape[-1])
    kf = k.astype(jnp.float32)
    vf = v.astype(jnp.float32)
    k_pos = jnp.arange(seq)
    nb = seq // Q_BLOCK

    def block(args):
        i, qb = args
        s = jnp.einsum("bhqd,bhkd->bhqk", qb.astype(jnp.float32), kf) * scale
        q_pos = i * Q_BLOCK + jnp.arange(Q_BLOCK)
        causal = k_pos[None, :] <= q_pos[:, None]
        p = jax.nn.softmax(jnp.where(causal, s, -jnp.inf), axis=-1)
        return jnp.einsum("bhqk,bhkd->bhqd", p, vf)

    o = lax.map(block, (jnp.arange(nb), to_query_blocks(q)))
    return from_query_blocks(o).astype(q.dtype)


def stick_breaking_attention(q, k, v):
    seq = q.shape[2]
    scale = 1.0 / math.sqrt(q.shape[-1])
    kf = k.astype(jnp.float32)
    vf = v.astype(jnp.float32)
    k_pos = jnp.arange(seq)
    nb = seq // Q_BLOCK

    def block(args):
        i, qb = args
        z = jnp.einsum("bhqd,bhkd->bhqk", qb.astype(jnp.float32), kf) * scale
        q_pos = i * Q_BLOCK + jnp.arange(Q_BLOCK)
        strict = k_pos[None, :] < q_pos[:, None]
        log_beta = jax.nn.log_sigmoid(z)
        log_one_minus = jnp.where(strict, jax.nn.log_sigmoid(-z), 0.0)
        suffix = lax.cumsum(log_one_minus, axis=3, reverse=True) - log_one_minus
        a = jnp.where(strict, jnp.exp(log_beta + suffix), 0.0)
        return jnp.einsum("bhqk,bhkd->bhqd", a, vf)

    o = lax.map(block, (jnp.arange(nb), to_query_blocks(q)))
    return from_query_blocks(o).astype(q.dtype)


def split_heads(t, n_heads):
    b, s, _ = t.shape
    return t.reshape(b, s, n_heads, -1).transpose(0, 2, 1, 3)


def merge_heads(t):
    b, h, s, d = t.shape
    return t.transpose(0, 2, 1, 3).reshape(b, s, h * d)


def _fwd_setup_inputs(seed: int = 0) -> dict:
    key = jax.random.key(seed)
    ks = jax.random.split(key, 20)
    f32 = jnp.float32

    def w(k, shape, fan_in):
        return jax.random.normal(k, shape, f32) * (fan_in ** -0.5)

    def gain(k, shape):
        return 1.0 + 0.02 * jax.random.normal(k, shape, f32)

    x = jax.random.normal(ks[0], (BATCH, SEQ, D_MODEL), f32)
    positions = jnp.broadcast_to(jnp.arange(SEQ, dtype=jnp.int32), (BATCH, SEQ))
    return {
        "x": x,
        "positions": positions,
        "norm_mix": gain(ks[1], (DEPTH, D_MODEL)),
        "w_in": w(ks[2], (DEPTH, D_MODEL, IN_PROJ_WIDTH), D_MODEL),
        "q_latent_norm": gain(ks[3], (DEPTH, Q_LORA_RANK)),
        "w_uq": w(ks[4], (DEPTH, Q_LORA_RANK, MLA_HEADS * MLA_QK_DIM), Q_LORA_RANK),
        "kv_latent_norm": gain(ks[5], (DEPTH, KV_LORA_RANK)),
        "w_ukv": w(ks[6], (DEPTH, KV_LORA_RANK, MLA_HEADS * (MLA_NOPE_DIM + MLA_V_DIM)), KV_LORA_RANK),
        "out_norm_mla": gain(ks[7], (DEPTH, MLA_WIDTH)),
        "out_norm_sb": gain(ks[8], (DEPTH, SB_WIDTH)),
        "w_o": w(ks[9], (DEPTH, MIX_WIDTH, D_MODEL), MIX_WIDTH),
        "norm_ffn": gain(ks[10], (DEPTH, D_MODEL)),
        "w_gate": w(ks[11], (DEPTH, D_MODEL, D_FF), D_MODEL),
        "w_up": w(ks[12], (DEPTH, D_MODEL, D_FF), D_MODEL),
        "w_down": w(ks[13], (DEPTH, D_FF, D_MODEL), D_FF),
        "norm_final": gain(ks[14], (D_MODEL,)),
    }


def _fwd_reference(x, positions, norm_mix, w_in, q_latent_norm, w_uq, kv_latent_norm,
              w_ukv, out_norm_mla, out_norm_sb, w_o, norm_ffn, w_gate, w_up,
              w_down, norm_final):
    b, s, _ = x.shape
    cos, sin = rope_tables(positions, MLA_ROPE_DIM)
    h = x
    for l in range(DEPTH):
        u = rmsnorm(h, norm_mix[l])
        proj = jnp.einsum("bsd,de->bse", u, w_in[l])
        c_q, c_kv, k_r, q_sb, k_sb, v_sb = jnp.split(proj, IN_SPLIT_POINTS, axis=-1)

        q = split_heads(jnp.einsum("bsr,re->bse", rmsnorm(c_q, q_latent_norm[l]), w_uq[l]), MLA_HEADS)
        q_nope, q_rope = q[..., :MLA_NOPE_DIM], q[..., MLA_NOPE_DIM:]
        q_rope = apply_rope(q_rope, cos, sin)
        kv = split_heads(jnp.einsum("bsr,re->bse", rmsnorm(c_kv, kv_latent_norm[l]), w_ukv[l]), MLA_HEADS)
        k_nope, v_mla = kv[..., :MLA_NOPE_DIM], kv[..., MLA_NOPE_DIM:]
        k_rope = apply_rope(k_r[:, None], cos, sin)
        q_mla = jnp.concatenate([q_nope, q_rope], axis=-1)
        k_mla = jnp.concatenate([k_nope, jnp.broadcast_to(k_rope, (b, MLA_HEADS, s, MLA_ROPE_DIM))], axis=-1)
        o_mla = merge_heads(mla_causal_attention(q_mla, k_mla, v_mla))

        o_sb = merge_heads(stick_breaking_attention(
            split_heads(q_sb, SB_HEADS), split_heads(k_sb, SB_HEADS), split_heads(v_sb, SB_HEADS)))

        merged = jnp.concatenate([rmsnorm(o_mla, out_norm_mla[l]), rmsnorm(o_sb, out_norm_sb[l])], axis=-1)
        h = h + jnp.einsum("bse,ed->bsd", merged, w_o[l])

        f = rmsnorm(h, norm_ffn[l])
        gate = jnp.einsum("bsd,df->bsf", f, w_gate[l])
        up = jnp.einsum("bsd,df->bsf", f, w_up[l])
        h = h + jnp.einsum("bsf,fd->bsd", jax.nn.silu(gate) * up, w_down[l])
    return rmsnorm(h, norm_final)


import jax as _jax
import jax.numpy as _jnp

TWIN_FORMAT = 'train_step'
FWD_PARAMS = ['x', 'positions', 'norm_mix', 'w_in', 'q_latent_norm', 'w_uq', 'kv_latent_norm', 'w_ukv', 'out_norm_mla', 'out_norm_sb', 'w_o', 'norm_ffn', 'w_gate', 'w_up', 'w_down', 'norm_final']
TWIN_WEIGHTS = ['norm_mix', 'w_in', 'q_latent_norm', 'w_uq', 'kv_latent_norm', 'w_ukv', 'out_norm_mla', 'out_norm_sb', 'w_o', 'norm_ffn', 'w_gate', 'w_up', 'w_down', 'norm_final']
TWIN_DIFF_INPUT = 'x'
TWIN_INPUTS = ['x', 'positions', 'norm_mix', 'w_in', 'q_latent_norm', 'w_uq', 'kv_latent_norm', 'w_ukv', 'out_norm_mla', 'out_norm_sb', 'w_o', 'norm_ffn', 'w_gate', 'w_up', 'w_down', 'norm_final', 'loss_target', 'm_norm_mix', 'm_w_in', 'm_q_latent_norm', 'm_w_uq', 'm_kv_latent_norm', 'm_w_ukv', 'm_out_norm_mla', 'm_out_norm_sb', 'm_w_o', 'm_norm_ffn', 'm_w_gate', 'm_w_up', 'm_w_down', 'm_norm_final', 'v_norm_mix', 'v_w_in', 'v_q_latent_norm', 'v_w_uq', 'v_kv_latent_norm', 'v_w_ukv', 'v_out_norm_mla', 'v_out_norm_sb', 'v_w_o', 'v_norm_ffn', 'v_w_gate', 'v_w_up', 'v_w_down', 'v_norm_final']
TWIN_OUTPUTS = ['loss', 'grad_x', 'grad_norm_mix', 'grad_w_in', 'grad_q_latent_norm', 'grad_w_uq', 'grad_kv_latent_norm', 'grad_w_ukv', 'grad_out_norm_mla', 'grad_out_norm_sb', 'grad_w_o', 'grad_norm_ffn', 'grad_w_gate', 'grad_w_up', 'grad_w_down', 'grad_norm_final', 'delta_norm_mix', 'delta_w_in', 'delta_q_latent_norm', 'delta_w_uq', 'delta_kv_latent_norm', 'delta_w_ukv', 'delta_out_norm_mla', 'delta_out_norm_sb', 'delta_w_o', 'delta_norm_ffn', 'delta_w_gate', 'delta_w_up', 'delta_w_down', 'delta_norm_final', 'new_m_norm_mix', 'new_m_w_in', 'new_m_q_latent_norm', 'new_m_w_uq', 'new_m_kv_latent_norm', 'new_m_w_ukv', 'new_m_out_norm_mla', 'new_m_out_norm_sb', 'new_m_w_o', 'new_m_norm_ffn', 'new_m_w_gate', 'new_m_w_up', 'new_m_w_down', 'new_m_norm_final', 'new_v_norm_mix', 'new_v_w_in', 'new_v_q_latent_norm', 'new_v_w_uq', 'new_v_kv_latent_norm', 'new_v_w_ukv', 'new_v_out_norm_mla', 'new_v_out_norm_sb', 'new_v_w_o', 'new_v_norm_ffn', 'new_v_w_gate', 'new_v_w_up', 'new_v_w_down', 'new_v_norm_final']
TWIN_LEAF_KINDS = {'loss': 'loss', 'grad_x': 'grad_x', 'grad_norm_mix': 'grad_w', 'grad_w_in': 'grad_w', 'grad_q_latent_norm': 'grad_w', 'grad_w_uq': 'grad_w', 'grad_kv_latent_norm': 'grad_w', 'grad_w_ukv': 'grad_w', 'grad_out_norm_mla': 'grad_w', 'grad_out_norm_sb': 'grad_w', 'grad_w_o': 'grad_w', 'grad_norm_ffn': 'grad_w', 'grad_w_gate': 'grad_w', 'grad_w_up': 'grad_w', 'grad_w_down': 'grad_w', 'grad_norm_final': 'grad_w', 'delta_norm_mix': 'delta_w', 'delta_w_in': 'delta_w', 'delta_q_latent_norm': 'delta_w', 'delta_w_uq': 'delta_w', 'delta_kv_latent_norm': 'delta_w', 'delta_w_ukv': 'delta_w', 'delta_out_norm_mla': 'delta_w', 'delta_out_norm_sb': 'delta_w', 'delta_w_o': 'delta_w', 'delta_norm_ffn': 'delta_w', 'delta_w_gate': 'delta_w', 'delta_w_up': 'delta_w', 'delta_w_down': 'delta_w', 'delta_norm_final': 'delta_w', 'new_m_norm_mix': 'new_m', 'new_m_w_in': 'new_m', 'new_m_q_latent_norm': 'new_m', 'new_m_w_uq': 'new_m', 'new_m_kv_latent_norm': 'new_m', 'new_m_w_ukv': 'new_m', 'new_m_out_norm_mla': 'new_m', 'new_m_out_norm_sb': 'new_m', 'new_m_w_o': 'new_m', 'new_m_norm_ffn': 'new_m', 'new_m_w_gate': 'new_m', 'new_m_w_up': 'new_m', 'new_m_w_down': 'new_m', 'new_m_norm_final': 'new_m', 'new_v_norm_mix': 'new_v', 'new_v_w_in': 'new_v', 'new_v_q_latent_norm': 'new_v', 'new_v_w_uq': 'new_v', 'new_v_kv_latent_norm': 'new_v', 'new_v_w_ukv': 'new_v', 'new_v_out_norm_mla': 'new_v', 'new_v_out_norm_sb': 'new_v', 'new_v_w_o': 'new_v', 'new_v_norm_ffn': 'new_v', 'new_v_w_gate': 'new_v', 'new_v_w_up': 'new_v', 'new_v_w_down': 'new_v', 'new_v_norm_final': 'new_v'}


def _forward(args):
    return _fwd_reference(*[args[k] for k in FWD_PARAMS])


def _output_shape():
    def fwd():
        inp = _fwd_setup_inputs(0)
        return _fwd_reference(*[inp[k] for k in FWD_PARAMS])
    out = _jax.eval_shape(fwd)
    return out.shape, out.dtype

N_MICROBATCH = 1
ADAM_LR = 0.001
ADAM_B1 = 0.9
ADAM_B2 = 0.999
ADAM_EPS = 1e-08
ADAM_WD = 0.01
ADAM_STEP = 10
PER_EXAMPLE_BATCH_AXIS = {'x': 0, 'positions': 0, 'loss_target': 0}
SHARED_INPUTS = []
_WEIGHT_DTYPES = {'norm_mix': _jnp.float32, 'w_in': _jnp.float32, 'q_latent_norm': _jnp.float32, 'w_uq': _jnp.float32, 'kv_latent_norm': _jnp.float32, 'w_ukv': _jnp.float32, 'out_norm_mla': _jnp.float32, 'out_norm_sb': _jnp.float32, 'w_o': _jnp.float32, 'norm_ffn': _jnp.float32, 'w_gate': _jnp.float32, 'w_up': _jnp.float32, 'w_down': _jnp.float32, 'norm_final': _jnp.float32}
MOMENT_SCALE = {'norm_mix': 2.668472e-01, 'w_in': 1.920137e-01, 'q_latent_norm': 3.251806e-01, 'w_uq': 1.627584e-01, 'kv_latent_norm': 7.330432e-01, 'w_ukv': 1.966289e-01, 'out_norm_mla': 2.061235e-01, 'out_norm_sb': 2.050439e-01, 'w_o': 1.934607e-01, 'norm_ffn': 1.458199e-01, 'w_gate': 6.077785e-02, 'w_up': 5.891646e-02, 'w_down': 9.746403e-02, 'norm_final': 6.403854e+01}


def _to_microbatches(a, axis):
    t = _jnp.moveaxis(a, axis, 0)
    t = t.reshape((N_MICROBATCH, t.shape[0] // N_MICROBATCH) + t.shape[1:])
    return _jnp.moveaxis(t, 1, axis + 1)


def setup_inputs(seed: int = 0) -> dict:
    inp = _fwd_setup_inputs(seed)
    key = _jax.random.fold_in(_jax.random.key(seed), 7919)
    shape, _ = _output_shape()
    out = dict(inp)
    out["loss_target"] = _jax.random.normal(_jax.random.fold_in(key, 0), shape, _jnp.float32)
    for i, name in enumerate(TWIN_WEIGHTS):
        w = inp[name].astype(_jnp.float32)
        if MOMENT_SCALE is None:
            s = _jnp.sqrt(_jnp.mean(_jnp.square(w)) + 1e-30)
        else:
            s = MOMENT_SCALE[name]
        km, kv = _jax.random.split(_jax.random.fold_in(key, i + 1))
        out[name] = w
        out["m_" + name] = s * _jax.random.normal(km, w.shape, _jnp.float32)
        out["v_" + name] = (s * s) * _jax.random.uniform(kv, w.shape, _jnp.float32, 0.5, 1.5)
    if N_MICROBATCH > 1:
        for name, axis in PER_EXAMPLE_BATCH_AXIS.items():
            out[name] = _to_microbatches(out[name], axis)
    return {'x': out['x'], 'positions': out['positions'], 'norm_mix': out['norm_mix'], 'w_in': out['w_in'], 'q_latent_norm': out['q_latent_norm'], 'w_uq': out['w_uq'], 'kv_latent_norm': out['kv_latent_norm'], 'w_ukv': out['w_ukv'], 'out_norm_mla': out['out_norm_mla'], 'out_norm_sb': out['out_norm_sb'], 'w_o': out['w_o'], 'norm_ffn': out['norm_ffn'], 'w_gate': out['w_gate'], 'w_up': out['w_up'], 'w_down': out['w_down'], 'norm_final': out['norm_final'], 'loss_target': out['loss_target'], 'm_norm_mix': out['m_norm_mix'], 'm_w_in': out['m_w_in'], 'm_q_latent_norm': out['m_q_latent_norm'], 'm_w_uq': out['m_w_uq'], 'm_kv_latent_norm': out['m_kv_latent_norm'], 'm_w_ukv': out['m_w_ukv'], 'm_out_norm_mla': out['m_out_norm_mla'], 'm_out_norm_sb': out['m_out_norm_sb'], 'm_w_o': out['m_w_o'], 'm_norm_ffn': out['m_norm_ffn'], 'm_w_gate': out['m_w_gate'], 'm_w_up': out['m_w_up'], 'm_w_down': out['m_w_down'], 'm_norm_final': out['m_norm_final'], 'v_norm_mix': out['v_norm_mix'], 'v_w_in': out['v_w_in'], 'v_q_latent_norm': out['v_q_latent_norm'], 'v_w_uq': out['v_w_uq'], 'v_kv_latent_norm': out['v_kv_latent_norm'], 'v_w_ukv': out['v_w_ukv'], 'v_out_norm_mla': out['v_out_norm_mla'], 'v_out_norm_sb': out['v_out_norm_sb'], 'v_w_o': out['v_w_o'], 'v_norm_ffn': out['v_norm_ffn'], 'v_w_gate': out['v_w_gate'], 'v_w_up': out['v_w_up'], 'v_w_down': out['v_w_down'], 'v_norm_final': out['v_norm_final']}


def _loss(weights, diff, rest, loss_target):
    with _jax.named_scope("forward"):
        args = {**rest, TWIN_DIFF_INPUT: diff, **{k: w.astype(_WEIGHT_DTYPES[k]) for k, w in weights.items()}}
        y = _forward(args)
    with _jax.named_scope("loss_head"):
        err = _jnp.square(y.astype(_jnp.float32) - loss_target)
        return 0.5 * _jnp.sum(_jnp.mean(err, axis=-1)) if err.ndim else 0.5 * err


def _adamw(w, g, m, v):
    m = ADAM_B1 * m + (1.0 - ADAM_B1) * g
    v = ADAM_B2 * v + (1.0 - ADAM_B2) * _jnp.square(g)
    m_hat = m / (1.0 - ADAM_B1 ** ADAM_STEP)
    v_hat = v / (1.0 - ADAM_B2 ** ADAM_STEP)
    delta = -ADAM_LR * (m_hat / (_jnp.sqrt(v_hat) + ADAM_EPS) + ADAM_WD * w)
    return delta, m, v


def reference(x, positions, norm_mix, w_in, q_latent_norm, w_uq, kv_latent_norm, w_ukv, out_norm_mla, out_norm_sb, w_o, norm_ffn, w_gate, w_up, w_down, norm_final, loss_target, m_norm_mix, m_w_in, m_q_latent_norm, m_w_uq, m_kv_latent_norm, m_w_ukv, m_out_norm_mla, m_out_norm_sb, m_w_o, m_norm_ffn, m_w_gate, m_w_up, m_w_down, m_norm_final, v_norm_mix, v_w_in, v_q_latent_norm, v_w_uq, v_kv_latent_norm, v_w_ukv, v_out_norm_mla, v_out_norm_sb, v_w_o, v_norm_ffn, v_w_gate, v_w_up, v_w_down, v_norm_final):
    given = dict(x=x, positions=positions, norm_mix=norm_mix, w_in=w_in, q_latent_norm=q_latent_norm, w_uq=w_uq, kv_latent_norm=kv_latent_norm, w_ukv=w_ukv, out_norm_mla=out_norm_mla, out_norm_sb=out_norm_sb, w_o=w_o, norm_ffn=norm_ffn, w_gate=w_gate, w_up=w_up, w_down=w_down, norm_final=norm_final, loss_target=loss_target, m_norm_mix=m_norm_mix, m_w_in=m_w_in, m_q_latent_norm=m_q_latent_norm, m_w_uq=m_w_uq, m_kv_latent_norm=m_kv_latent_norm, m_w_ukv=m_w_ukv, m_out_norm_mla=m_out_norm_mla, m_out_norm_sb=m_out_norm_sb, m_w_o=m_w_o, m_norm_ffn=m_norm_ffn, m_w_gate=m_w_gate, m_w_up=m_w_up, m_w_down=m_w_down, m_norm_final=m_norm_final, v_norm_mix=v_norm_mix, v_w_in=v_w_in, v_q_latent_norm=v_q_latent_norm, v_w_uq=v_w_uq, v_kv_latent_norm=v_kv_latent_norm, v_w_ukv=v_w_ukv, v_out_norm_mla=v_out_norm_mla, v_out_norm_sb=v_out_norm_sb, v_w_o=v_w_o, v_norm_ffn=v_norm_ffn, v_w_gate=v_w_gate, v_w_up=v_w_up, v_w_down=v_w_down, v_norm_final=v_norm_final)
    weights = {n: given[n] for n in TWIN_WEIGHTS}
    shared = {n: given[n] for n in SHARED_INPUTS}
    per_example = {n: given[n] for n in ['x', 'positions']}
    grad_fn = _jax.value_and_grad(_loss, argnums=(0, 1))

    def one_microbatch(ex, loss_target):
        ex = dict(ex)
        diff = ex.pop(TWIN_DIFF_INPUT)
        return grad_fn(weights, diff, {**shared, **ex}, loss_target)

    if N_MICROBATCH == 1:
        loss, (grad_w, grad_x) = one_microbatch(per_example, given["loss_target"])
    else:
        def body(carry, xs):
            loss_sum, grad_sum = carry
            l_k, (gw_k, gx_k) = one_microbatch(xs[0], xs[1])
            with _jax.named_scope("update"):
                return (loss_sum + l_k, _jax.tree.map(_jnp.add, grad_sum, gw_k)), gx_k

        init = (_jnp.zeros((), _jnp.float32), _jax.tree.map(_jnp.zeros_like, weights))
        (loss, grad_w), grad_x = _jax.lax.scan(body, init, (per_example, given["loss_target"]))
    with _jax.named_scope("update"):
        delta_w, new_m, new_v = {}, {}, {}
        for n in TWIN_WEIGHTS:
            delta_w[n], new_m[n], new_v[n] = _adamw(weights[n], grad_w[n], given["m_" + n], given["v_" + n])
    return (loss, grad_x, *[grad_w[n] for n in TWIN_WEIGHTS], *[delta_w[n] for n in TWIN_WEIGHTS],
            *[new_m[n] for n in TWIN_WEIGHTS], *[new_v[n] for n in TWIN_WEIGHTS])
```

```python
import functools
import math

import jax
import jax.numpy as jnp
from jax import lax
from jax.experimental import pallas as pl
from jax.experimental.pallas import tpu as pltpu

F32 = jnp.float32
BF16 = jnp.bfloat16
MESH = pl.DeviceIdType.MESH

D_MODEL = 1024
EPS = 1e-6
MLA_HEADS = 8
MLA_NOPE = 64
MLA_ROPE = 32
MLA_V = 64
MLA_QK = MLA_NOPE + MLA_ROPE
Q_RANK = 256
KV_RANK = 128
ROPE_THETA = 10000.0
SB_HEADS = 8
SB_DIM = 64
MLA_WIDTH = MLA_HEADS * MLA_V
SB_WIDTH = SB_HEADS * SB_DIM
D_FF = 2816
IN_WIDTH = Q_RANK + KV_RANK + MLA_ROPE + 3 * SB_WIDTH

ADAM_LR = 0.001
ADAM_B1 = 0.9
ADAM_B2 = 0.999
ADAM_EPS = 1e-08
ADAM_WD = 0.01
ADAM_STEP = 10

N_SHARD = 4
PACK_W = 256
LANES = 128
VMEM_LIMIT = 56 * 1024 * 1024
TN_ACC_BYTES = 6 * 1024 * 1024 + 512 * 1024
NEG = -1e30

ROW_TILE = 256
ATT_TQ = 512
SB_TK = 256
TN_TS = 512
ADD_ROWS = 3072


def _dot(a, b):
    return jnp.dot(a, b, preferred_element_type=F32)


def _dot_nt(a, b):
    return lax.dot_general(a, b, (((1,), (1,)), ((), ())), preferred_element_type=F32)


def _dot_tn(a, b):
    return lax.dot_general(a, b, (((0,), (0,)), ((), ())), preferred_element_type=F32)


def _params(n_grid, vmem=VMEM_LIMIT):
    return pltpu.CompilerParams(dimension_semantics=("arbitrary",) * n_grid, vmem_limit_bytes=vmem)


def _rms(x):
    r = lax.rsqrt(jnp.mean(x * x, axis=-1, keepdims=True) + EPS)
    return x * r, r


def _rms_bwd(n, r, g, dy):
    dn = dy * g
    dx = r * (dn - n * jnp.mean(dn * n, axis=-1, keepdims=True))
    return dx, jnp.sum(dy * n, axis=0, keepdims=True)


def _accumulate(ref, val, step):
    @pl.when(step == 0)
    def _():
        ref[...] = val

    @pl.when(step != 0)
    def _():
        ref[...] += val


def _rowwise(name, body, rows, consts, row_out, acc_out, tm):
    n_rows = rows[0].shape[0]
    tm = min(tm, n_rows)
    nr, nc, no = len(rows), len(consts), len(row_out)

    def kern(*refs):
        body(refs[:nr], refs[nr:nr + nc], refs[nr + nc:nr + nc + no], refs[nr + nc + no:], pl.program_id(0))

    in_specs = [pl.BlockSpec((tm, a.shape[1]), lambda i: (i, 0)) for a in rows]
    in_specs += [pl.BlockSpec(a.shape, lambda i: (0, 0), pipeline_mode=pl.Buffered(1)) for a in consts]
    out_specs = [pl.BlockSpec((tm, s.shape[1]), lambda i: (i, 0)) for s in row_out]
    out_specs += [pl.BlockSpec(s.shape, lambda i: (0, 0)) for s in acc_out]
    return pl.pallas_call(
        kern, name=name, grid=(n_rows // tm,), in_specs=in_specs, out_specs=out_specs,
        out_shape=list(row_out) + list(acc_out), compiler_params=_params(1),
    )(*rows, *consts)


def _sds(shape, dtype):
    return jax.ShapeDtypeStruct(shape, dtype)


def _fwd_a(x, tabs, w):
    s = x.shape[0]

    def body(r, c, o, a, step):
        x_ref, c768, s768, c32, s32 = r
        gmix, wcq, wckv, wkr, wkrr, wsb, gq, wuq, wuqr, gkv, wukv = c
        u_o, cq_o, ckv_o, cqn_o, ckvn_o, q_o, kv_o, kr_o, sb_o = o
        n, _ = _rms(x_ref[...])
        u = (n * gmix[...]).astype(BF16)
        u_o[...] = u
        cq = _dot(u, wcq[...])
        ckv = _dot(u, wckv[...])
        kr = _dot(u, wkr[...]) * c32[...] + _dot(u, wkrr[...]) * s32[...]
        sb_o[...] = _dot(u, wsb[...]).astype(BF16)
        cq_o[...] = cq
        ckv_o[...] = ckv
        kr_o[...] = kr.astype(BF16)
        nq, _ = _rms(cq)
        cqn = (nq * gq[...]).astype(BF16)
        cqn_o[...] = cqn
        q = _dot(cqn, wuq[...]) * c768[...] + _dot(cqn, wuqr[...]) * s768[...]
        q_o[...] = q.astype(BF16)
        nkv, _ = _rms(ckv)
        ckvn = (nkv * gkv[...]).astype(BF16)
        ckvn_o[...] = ckvn
        kv_o[...] = _dot(ckvn, wukv[...]).astype(BF16)

    outs = [
        _sds((s, D_MODEL), BF16), _sds((s, Q_RANK), F32), _sds((s, KV_RANK), F32), _sds((s, Q_RANK), BF16),
        _sds((s, KV_RANK), BF16), _sds((s, MLA_HEADS * MLA_QK), BF16), _sds((s, MLA_HEADS * (MLA_NOPE + MLA_V)), BF16),
        _sds((s, MLA_ROPE), BF16), _sds((s, 3 * SB_WIDTH), BF16),
    ]
    consts = [w["g_mix"], w["w_cq"], w["w_ckv"], w["w_kr"], w["w_krr"], w["w_sb"], w["g_q"], w["w_uq"], w["w_uqr"],
              w["g_kv"], w["w_ukv"]]
    return _rowwise("fwd_a", body, [x, tabs["c768"], tabs["s768"], tabs["c32"], tabs["s32"]], consts, outs, [], ROW_TILE)


def _fwd_b1(x, o_mla, o_sb, w):
    s = x.shape[0]

    def body(r, c, o, a, step):
        x_ref, oa_ref, ob_ref = r
        ga, gb, woa, wob, gf, wg, wu = c
        mg_o, h1_o, f_o, gate_o, up_o, act_o = o
        na, _ = _rms(oa_ref[...])
        nb, _ = _rms(ob_ref[...])
        ma = (na * ga[...]).astype(BF16)
        mb = (nb * gb[...]).astype(BF16)
        mg_o[:, :MLA_WIDTH] = ma
        mg_o[:, MLA_WIDTH:] = mb
        h1 = x_ref[...] + _dot(ma, woa[...]) + _dot(mb, wob[...])
        h1_o[...] = h1
        nf, _ = _rms(h1)
        f = (nf * gf[...]).astype(BF16)
        f_o[...] = f
        gate = _dot(f, wg[...])
        up = _dot(f, wu[...])
        gate_o[...] = gate.astype(BF16)
        up_o[...] = up.astype(BF16)
        act_o[...] = (gate * (1.0 / (1.0 + jnp.exp(-gate))) * up).astype(BF16)

    outs = [_sds((s, D_MODEL), BF16), _sds((s, D_MODEL), F32), _sds((s, D_MODEL), BF16), _sds((s, D_FF), BF16),
            _sds((s, D_FF), BF16), _sds((s, D_FF), BF16)]
    consts = [w["g_a"], w["g_b"], w["w_oa"], w["w_ob"], w["g_f"], w["w_gate"], w["w_up"]]
    return _rowwise("fwd_b1", body, [x, o_mla, o_sb], consts, outs, [], ROW_TILE)


def _fwd_b2(h1, act, tgt, w):
    s = h1.shape[0]

    def body(r, c, o, a, step):
        h1_ref, act_ref, t_ref = r
        wd, gn = c
        (dh2_o,) = o
        loss_o, dgn_o = a
        h2 = h1_ref[...] + _dot(act_ref[...], wd[...])
        n2, r2 = _rms(h2)
        err = n2 * gn[...] - t_ref[...]
        part = jnp.sum(jnp.sum(err * err, axis=1, keepdims=True), axis=0, keepdims=True) * (0.5 / D_MODEL)
        _accumulate(loss_o, jnp.broadcast_to(part, (1, LANES)), step)
        dh2, dgn = _rms_bwd(n2, r2, gn[...], err * (1.0 / D_MODEL))
        dh2_o[...] = dh2
        _accumulate(dgn_o, dgn, step)

    return _rowwise("fwd_b2", body, [h1, act, tgt], [w["w_down"], w["g_n"]], [_sds((s, D_MODEL), F32)],
                    [_sds((1, LANES), F32), _sds((1, D_MODEL), F32)], ROW_TILE)


def _bwd_b1(dh2, gate, up, w):
    s = dh2.shape[0]

    def body(r, c, o, a, step):
        dh2_ref, gate_ref, up_ref = r
        (wdt,) = c
        dgate_o, dup_o = o
        dact = _dot(dh2_ref[...].astype(BF16), wdt[...])
        gate = gate_ref[...].astype(F32)
        sig = 1.0 / (1.0 + jnp.exp(-gate))
        dup_o[...] = (dact * (gate * sig)).astype(BF16)
        dgate_o[...] = (dact * up_ref[...].astype(F32) * (sig * (1.0 + gate * (1.0 - sig)))).astype(BF16)

    return _rowwise("bwd_b1", body, [dh2, gate, up], [w["w_down_t"]], [_sds((s, D_FF), BF16), _sds((s, D_FF), BF16)],
                    [], ROW_TILE)


def _bwd_b2(dgate, dup, h1, dh2, o_mla, o_sb, w):
    s = h1.shape[0]

    def body(r, c, o, a, step):
        dgate_ref, dup_ref, h1_ref, dh2_ref, oa_ref, ob_ref = r
        wgt, wut, gf, woat, wobt, ga, gb = c
        dh1_o, doa_o, dob_o = o
        dgf_o, dga_o, dgb_o = a
        df = _dot(dgate_ref[...], wgt[...]) + _dot(dup_ref[...], wut[...])
        nf, rf = _rms(h1_ref[...])
        dres, dgf = _rms_bwd(nf, rf, gf[...], df)
        dh1 = dh2_ref[...] + dres
        dh1_o[...] = dh1
        dh1b = dh1.astype(BF16)
        na, ra = _rms(oa_ref[...])
        doa, dga = _rms_bwd(na, ra, ga[...], _dot(dh1b, woat[...]))
        nb, rb = _rms(ob_ref[...])
        dob, dgb = _rms_bwd(nb, rb, gb[...], _dot(dh1b, wobt[...]))
        doa_o[...] = doa
        dob_o[...] = dob
        _accumulate(dgf_o, dgf, step)
        _accumulate(dga_o, dga, step)
        _accumulate(dgb_o, dgb, step)

    consts = [w["w_gate_t"], w["w_up_t"], w["g_f"], w["w_oa_t"], w["w_ob_t"], w["g_a"], w["g_b"]]
    outs = [_sds((s, D_MODEL), F32), _sds((s, MLA_WIDTH), F32), _sds((s, SB_WIDTH), F32)]
    accs = [_sds((1, D_MODEL), F32), _sds((1, MLA_WIDTH), F32), _sds((1, SB_WIDTH), F32)]
    return _rowwise("bwd_b2", body, [dgate, dup, h1, dh2, o_mla, o_sb], consts, outs, accs, ROW_TILE)


def _bwd_a(x, dh1, cq, ckv, dq, dkv, dkr, dsb, tabs, w):
    s = x.shape[0]

    def body(r, c, o, a, step):
        x_ref, dh1_ref, cq_ref, ckv_ref, dq_ref, dkv_ref, dkr_ref, dsb_ref, c768, s768, c32, s32 = r
        wuqt, wuqrt, gq, wukvt, gkv, wcqt, wckvt, wkrt, wkrrt, wsbt, gmix = c
        dx_o, a1_o, a2_o, dcq_o, dckv_o, dkrc_o, dkrs_o = o
        dgq_o, dgkv_o, dgmix_o = a
        dq = dq_ref[...]
        a1 = (dq * c768[...]).astype(BF16)
        a2 = (dq * s768[...]).astype(BF16)
        a1_o[...] = a1
        a2_o[...] = a2
        nq, rq = _rms(cq_ref[...])
        dcq, dgq = _rms_bwd(nq, rq, gq[...], _dot(a1, wuqt[...]) + _dot(a2, wuqrt[...]))
        nkv, rkv = _rms(ckv_ref[...])
        dckv, dgkv = _rms_bwd(nkv, rkv, gkv[...], _dot(dkv_ref[...].astype(BF16), wukvt[...]))
        dkr = dkr_ref[...]
        dcq_b = dcq.astype(BF16)
        dckv_b = dckv.astype(BF16)
        dkrc = (dkr * c32[...]).astype(BF16)
        dkrs = (dkr * s32[...]).astype(BF16)
        dcq_o[...] = dcq_b
        dckv_o[...] = dckv_b
        dkrc_o[...] = dkrc
        dkrs_o[...] = dkrs
        du = (_dot(dcq_b, wcqt[...]) + _dot(dckv_b, wckvt[...]) + _dot(dkrc, wkrt[...]) + _dot(dkrs, wkrrt[...])
              + _dot(dsb_ref[...], wsbt[...]))
        nx, rx = _rms(x_ref[...])
        dres, dgmix = _rms_bwd(nx, rx, gmix[...], du)
        dx_o[...] = dh1_ref[...] + dres
        _accumulate(dgq_o, dgq, step)
        _accumulate(dgkv_o, dgkv, step)
        _accumulate(dgmix_o, dgmix, step)

    consts = [w["w_uq_t"], w["w_uqr_t"], w["g_q"], w["w_ukv_t"], w["g_kv"], w["w_cq_t"], w["w_ckv_t"], w["w_kr_t"],
              w["w_krr_t"], w["w_sb_t"], w["g_mix"]]
    outs = [_sds((s, D_MODEL), F32), _sds((s, MLA_HEADS * MLA_QK), BF16), _sds((s, MLA_HEADS * MLA_QK), BF16),
            _sds((s, Q_RANK), BF16), _sds((s, KV_RANK), BF16), _sds((s, MLA_ROPE), BF16), _sds((s, MLA_ROPE), BF16)]
    accs = [_sds((1, Q_RANK), F32), _sds((1, KV_RANK), F32), _sds((1, D_MODEL), F32)]
    rows = [x, dh1, cq, ckv, dq, dkv, dkr, dsb, tabs["c768"], tabs["s768"], tabs["c32"], tabs["s32"]]
    return _rowwise("bwd_a", body, rows, consts, outs, accs, ROW_TILE)


def _tn_tile(k, n):
    if n % LANES or k * n * 4 <= TN_ACC_BYTES:
        return n
    units = n // LANES
    best = 1
    for d in range(1, units + 1):
        if units % d == 0 and k * d * LANES * 4 <= TN_ACC_BYTES:
            best = d
    return best * LANES


def _tn_matmul(name, x, y):
    s, k = x.shape
    n = y.shape[1]
    ts = min(TN_TS, s)
    tn = _tn_tile(k, n)

    def kern(x_ref, y_ref, o_ref):
        step = pl.program_id(1)
        _accumulate(o_ref, _dot_tn(x_ref[...].astype(BF16), y_ref[...].astype(BF16)), step)

    return pl.pallas_call(
        kern, name=name, grid=(n // tn, s // ts),
        in_specs=[pl.BlockSpec((ts, k), lambda j, i: (i, 0)), pl.BlockSpec((ts, tn), lambda j, i: (i, j))],
        out_specs=pl.BlockSpec((k, tn), lambda j, i: (0, j)), out_shape=_sds((k, n), F32), compiler_params=_params(2),
    )(x, y)


def _mla_fwd(q, k, v):
    h, s, dq = q.shape
    dv = v.shape[2]
    tq = min(ATT_TQ, s)
    scale = 1.0 / math.sqrt(dq)

    def kern(q_ref, k_ref, v_ref, o_ref, lse_ref):
        i = pl.program_id(1)
        qv = q_ref[...]

        def block(kb, carry, masked):
            m, l, acc = carry
            ks = pl.ds(pl.multiple_of(kb * tq, tq), tq)
            sc = _dot_nt(qv, k_ref[ks, :]) * scale
            if masked:
                row = lax.broadcasted_iota(jnp.int32, (tq, tq), 0)
                col = lax.broadcasted_iota(jnp.int32, (tq, tq), 1)
                sc = jnp.where(col <= row, sc, NEG)
            m_new = jnp.maximum(m, jnp.max(sc, axis=1, keepdims=True))
            p = jnp.exp(sc - m_new)
            alpha = jnp.exp(m - m_new)
            l = alpha * l + jnp.sum(p, axis=1, keepdims=True)
            acc = alpha * acc + _dot(p.astype(BF16), v_ref[ks, :])
            return m_new, l, acc

        init = (jnp.full((tq, 1), NEG, F32), jnp.zeros((tq, 1), F32), jnp.zeros((tq, dv), F32))
        carry = lax.fori_loop(0, i, lambda kb, c: block(kb, c, False), init)
        m, l, acc = block(i, carry, True)
        o_ref[...] = acc / l
        lse_ref[...] = jnp.broadcast_to(m + jnp.log(l), (tq, LANES))

    return pl.pallas_call(
        kern, name="mla_fwd", grid=(h, s // tq),
        in_specs=[pl.BlockSpec((None, tq, dq), lambda a, i: (a, i, 0)), pl.BlockSpec((None, s, dq), lambda a, i: (a, 0, 0)),
                  pl.BlockSpec((None, s, dv), lambda a, i: (a, 0, 0))],
        out_specs=[pl.BlockSpec((None, tq, dv), lambda a, i: (a, i, 0)), pl.BlockSpec((None, tq, LANES), lambda a, i: (a, i, 0))],
        out_shape=[_sds((h, s, dv), F32), _sds((h, s, LANES), F32)], compiler_params=_params(2),
    )(q, k, v)


def _mla_bwd(q, k, v, o, do, lse):
    h, s, dq = q.shape
    dv = v.shape[2]
    tq = min(ATT_TQ, s)
    scale = 1.0 / math.sqrt(dq)

    def kern(q_ref, k_ref, v_ref, o_ref, do_ref, lse_ref, dq_ref, dk_ref, dv_ref):
        i = pl.program_id(1)

        @pl.when(i == 0)
        def _():
            dk_ref[...] = jnp.zeros_like(dk_ref)
            dv_ref[...] = jnp.zeros_like(dv_ref)

        qv = q_ref[...]
        do_f = do_ref[...]
        do_b = do_f.astype(BF16)
        delta = jnp.sum(do_f * o_ref[...], axis=1, keepdims=True)
        lse_v = lse_ref[:, 0:1]

        def block(kb, dq_acc, masked):
            ks = pl.ds(pl.multiple_of(kb * tq, tq), tq)
            kk = k_ref[ks, :]
            vv = v_ref[ks, :]
            p = jnp.exp(_dot_nt(qv, kk) * scale - lse_v)
            if masked:
                row = lax.broadcasted_iota(jnp.int32, (tq, tq), 0)
                col = lax.broadcasted_iota(jnp.int32, (tq, tq), 1)
                p = jnp.where(col <= row, p, 0.0)
            ds = (p * (_dot_nt(do_b, vv) - delta) * scale).astype(BF16)
            dv_ref[ks, :] += _dot_tn(p.astype(BF16), do_b)
            dk_ref[ks, :] += _dot_tn(ds, qv)
            return dq_acc + _dot(ds, kk)

        acc = lax.fori_loop(0, i, lambda kb, c: block(kb, c, False), jnp.zeros((tq, dq), F32))
        dq_ref[...] = block(i, acc, True)

    qspec = lambda d: pl.BlockSpec((None, tq, d), lambda a, i: (a, i, 0))
    full = lambda d: pl.BlockSpec((None, s, d), lambda a, i: (a, 0, 0))
    return pl.pallas_call(
        kern, name="mla_bwd", grid=(h, s // tq),
        in_specs=[qspec(dq), full(dq), full(dv), qspec(dv), qspec(dv), qspec(LANES)],
        out_specs=[qspec(dq), full(dq), full(dv)],
        out_shape=[_sds((h, s, dq), F32), _sds((h, s, dq), F32), _sds((h, s, dv), F32)], compiler_params=_params(2),
    )(q, k, v, o, do, lse)


def _sb_masks(tk):
    j = lax.broadcasted_iota(jnp.int32, (tk, tk), 0)
    c = lax.broadcasted_iota(jnp.int32, (tk, tk), 1)
    return (j > c).astype(BF16), (j < c).astype(BF16)


def _sb_scores(qs, kk, msuf, strict):
    z = _dot_nt(qs, kk)
    lom = -(jnp.maximum(z, 0.0) + jnp.log(1.0 + jnp.exp(-jnp.abs(z))))
    if strict is not None:
        lom = jnp.where(strict, lom, 0.0)
    hi = lom.astype(BF16)
    lo = (lom - hi.astype(F32)).astype(BF16)
    return z, lom, _dot(hi, msuf) + _dot(lo, msuf)


def _sb_strict(tq, tk, d):
    row = lax.broadcasted_iota(jnp.int32, (tq, tk), 0)
    col = lax.broadcasted_iota(jnp.int32, (tq, tk), 1)
    return col + d * tk < row


def _sb_fwd(q, k, v, msuf):
    h, s, d = q.shape
    tq = min(ATT_TQ, s)
    tk = min(SB_TK, s)
    ratio = tq // tk

    def kern(q_ref, k_ref, v_ref, m_ref, o_ref, c_ref):
        i = pl.program_id(1)
        qs = q_ref[...] * 0.125
        msf = m_ref[...]
        lane = lax.broadcasted_iota(jnp.int32, (tq, LANES), 1)

        def block(kb, carry, dd):
            c, acc, cm = carry
            ks = pl.ds(pl.multiple_of(kb * tk, tk), tk)
            strict = None if dd is None else _sb_strict(tq, tk, dd)
            z, lom, suf = _sb_scores(qs, k_ref[ks, :], msf, strict)
            a = jnp.exp(z + lom + (suf + c))
            if strict is not None:
                a = jnp.where(strict, a, 0.0)
            acc = acc + _dot(a.astype(BF16), v_ref[ks, :])
            cm = jnp.where(lane == kb, c, cm)
            return c + jnp.sum(lom, axis=1, keepdims=True), acc, cm

        carry = (jnp.zeros((tq, 1), F32), jnp.zeros((tq, d), F32), jnp.zeros((tq, LANES), F32))
        for dd in range(ratio - 1, -1, -1):
            carry = block(i * ratio + dd, carry, dd)
        carry = lax.fori_loop(0, i * ratio, lambda j, c: block(i * ratio - 1 - j, c, None), carry)
        o_ref[...] = carry[1]
        c_ref[...] = carry[2]

    return pl.pallas_call(
        kern, name="sb_fwd", grid=(h, s // tq),
        in_specs=[pl.BlockSpec((None, tq, d), lambda a, i: (a, i, 0)), pl.BlockSpec((None, s, d), lambda a, i: (a, 0, 0)),
                  pl.BlockSpec((None, s, d), lambda a, i: (a, 0, 0)), pl.BlockSpec((tk, tk), lambda a, i: (0, 0))],
        out_specs=[pl.BlockSpec((None, tq, d), lambda a, i: (a, i, 0)), pl.BlockSpec((None, tq, LANES), lambda a, i: (a, i, 0))],
        out_shape=[_sds((h, s, d), F32), _sds((h, s, LANES), F32)], compiler_params=_params(2),
    )(q, k, v, msuf)


def _sb_bwd(q, k, v, do, cmat, msuf, mpre):
    h, s, d = q.shape
    tq = min(ATT_TQ, s)
    tk = min(SB_TK, s)
    ratio = tq // tk

    def kern(q_ref, k_ref, v_ref, do_ref, c_ref, ms_ref, mp_ref, dq_ref, dk_ref, dv_ref):
        i = pl.program_id(1)

        @pl.when(i == 0)
        def _():
            dk_ref[...] = jnp.zeros_like(dk_ref)
            dv_ref[...] = jnp.zeros_like(dv_ref)

        qv = q_ref[...]
        qs = qv * 0.125
        do_b = do_ref[...].astype(BF16)
        msf = ms_ref[...]
        mpf = mp_ref[...]
        cm = c_ref[...]
        lane = lax.broadcasted_iota(jnp.int32, (tq, LANES), 1)

        def block(kb, carry, dd):
            dq_acc, pc = carry
            ks = pl.ds(pl.multiple_of(kb * tk, tk), tk)
            kk = k_ref[ks, :]
            strict = None if dd is None else _sb_strict(tq, tk, dd)
            z, lom, suf = _sb_scores(qs, kk, msf, strict)
            c = jnp.sum(jnp.where(lane == kb, cm, 0.0), axis=1, keepdims=True)
            a = jnp.exp(z + lom + (suf + c))
            if strict is not None:
                a = jnp.where(strict, a, 0.0)
            g = _dot_nt(do_b, v_ref[ks, :]) * a
            p = pc + _dot(g.astype(BF16), mpf)
            omb = jnp.exp(lom)
            dz = (g * omb - (1.0 - omb) * p) * 0.125
            if strict is not None:
                dz = jnp.where(strict, dz, 0.0)
            dz = dz.astype(BF16)
            dv_ref[ks, :] += _dot_tn(a.astype(BF16), do_b)
            dk_ref[ks, :] += _dot_tn(dz, qv)
            return dq_acc + _dot(dz, kk), pc + jnp.sum(g, axis=1, keepdims=True)

        carry = (jnp.zeros((tq, d), F32), jnp.zeros((tq, 1), F32))
        carry = lax.fori_loop(0, i * ratio, lambda kb, c: block(kb, c, None), carry)
        for dd in range(ratio):
            carry = block(i * ratio + dd, carry, dd)
        dq_ref[...] = carry[0]

    qspec = lambda n: pl.BlockSpec((None, tq, n), lambda a, i: (a, i, 0))
    full = pl.BlockSpec((None, s, d), lambda a, i: (a, 0, 0))
    msk = pl.BlockSpec((tk, tk), lambda a, i: (0, 0))
    return pl.pallas_call(
        kern, name="sb_bwd", grid=(h, s // tq),
        in_specs=[qspec(d), full, full, qspec(d), qspec(LANES), msk, msk],
        out_specs=[qspec(d), full, full],
        out_shape=[_sds((h, s, d), F32)] * 3, compiler_params=_params(2),
    )(q, k, v, do, cmat, msuf, mpre)


def _place():
    return lax.axis_index("x"), lax.axis_index("y"), lax.axis_index("c")


def _other_chips(x, y):
    return [(1 - x, y), (x, 1 - y), (1 - x, 1 - y)]


HBM_SPEC = pl.BlockSpec(memory_space=pl.ANY)


def _allgather_shards(wp):
    rows, width = wp.shape
    half = rows // 2

    def body(x_ref, out_ref, send_sems, recv_sems, local_sem):
        x, y, c = _place()
        me, sibling = (x, y, c), (x, y, 1 - c)
        chips = _other_chips(x, y)

        def blk(ref, px, py, pc, per_chip):
            return ref.at[pl.ds(pl.multiple_of((2 * px + py) * per_chip + pc * half, 16), half), :]

        def copy(k, block, to, src=None):
            return pltpu.make_async_remote_copy(
                src_ref=blk(out_ref, *block, rows) if src is None else src, dst_ref=blk(out_ref, *block, rows),
                send_sem=send_sems.at[k], recv_sem=recv_sems.at[k], device_id=to, device_id_type=MESH)

        mine = pltpu.make_async_copy(x_ref, out_ref.at[pl.ds(pl.multiple_of((2 * x + y) * rows, 16), rows), :], local_sem)
        mine.start()
        my_half = blk(x_ref, 0, 0, c, 0)
        first = [copy(j, me, (*chip, c), src=my_half) for j, chip in enumerate(chips)]
        for cp in first:
            cp.start()
        passed = [copy(3 + j, (*chip, c), sibling) for j, chip in enumerate(chips)]
        for j, chip in enumerate(chips):
            copy(j, (*chip, c), me).wait_recv()
            passed[j].start()
        for j, chip in enumerate(chips):
            copy(3 + j, (*chip, 1 - c), me).wait_recv()
        for cp in first + passed:
            cp.wait_send()
        mine.wait()

    return pl.pallas_call(
        body, name="allgather_w", out_shape=_sds((N_SHARD * rows, width), wp.dtype), in_specs=[HBM_SPEC], out_specs=HBM_SPEC,
        scratch_shapes=[pltpu.SemaphoreType.DMA((6,)), pltpu.SemaphoreType.DMA((6,)), pltpu.SemaphoreType.DMA],
    )(wp)


def _sibling_swap(name, v):
    def body(v_ref, got_ref, send_sem, recv_sem):
        x, y, c = _place()
        cp = pltpu.make_async_remote_copy(src_ref=v_ref, dst_ref=got_ref, send_sem=send_sem, recv_sem=recv_sem,
                                          device_id=(x, y, 1 - c), device_id_type=MESH)
        cp.start()
        cp.wait()

    return pl.pallas_call(
        body, name=name, out_shape=_sds(v.shape, v.dtype), in_specs=[HBM_SPEC], out_specs=HBM_SPEC,
        scratch_shapes=[pltpu.SemaphoreType.DMA, pltpu.SemaphoreType.DMA],
    )(v)


def _chip_scatter(p):
    def body(p_ref, out_ref, send_sems, recv_sems, local_sem):
        x, y, c = _place()
        mine = 2 * x + y
        chips = _other_chips(x, y)
        local = pltpu.make_async_copy(p_ref.at[mine], out_ref.at[mine], local_sem)
        local.start()

        def copy(j, px, py):
            return pltpu.make_async_remote_copy(
                src_ref=p_ref.at[2 * px + py], dst_ref=out_ref.at[mine], send_sem=send_sems.at[j], recv_sem=recv_sems.at[j],
                device_id=(px, py, c), device_id_type=MESH)

        sends = [copy(j, px, py) for j, (px, py) in enumerate(chips)]
        for cp in sends:
            cp.start()
        for j, (px, py) in enumerate(chips):
            pltpu.make_async_remote_copy(
                src_ref=p_ref.at[mine], dst_ref=out_ref.at[2 * px + py], send_sem=send_sems.at[j], recv_sem=recv_sems.at[j],
                device_id=(px, py, c), device_id_type=MESH).wait_recv()
        for cp in sends:
            cp.wait_send()
        local.wait()

    return pl.pallas_call(
        body, name="chip_scatter", out_shape=_sds(p.shape, p.dtype), in_specs=[HBM_SPEC], out_specs=HBM_SPEC,
        scratch_shapes=[pltpu.SemaphoreType.DMA((3,)), pltpu.SemaphoreType.DMA((3,)), pltpu.SemaphoreType.DMA],
    )(p)


def _allreduce_small(v):
    m_per, n = v.shape

    def body(x_ref, tot_ref, all_ref, send_sems, recv_sems, local_sem):
        x, y, c = _place()
        me, sibling = (x, y, c), (x, y, 1 - c)
        chips = _other_chips(x, y)

        def rows(px, py, pc):
            return all_ref.at[pl.ds(pl.multiple_of((4 * px + 2 * py + pc) * m_per, 8), m_per), :]

        def copy(k, block, to, src=None):
            return pltpu.make_async_remote_copy(
                src_ref=rows(*block) if src is None else src, dst_ref=rows(*block), send_sem=send_sems.at[k],
                recv_sem=recv_sems.at[k], device_id=to, device_id_type=MESH)

        mine = pltpu.make_async_copy(x_ref, rows(*me), local_sem)
        mine.start()
        first = [copy(0, me, sibling, src=x_ref)] + [copy(1 + j, me, (*chip, c), src=x_ref) for j, chip in enumerate(chips)]
        for cp in first:
            cp.start()
        passed = [copy(4 + j, (*chip, c), sibling) for j, chip in enumerate(chips)]
        for j, chip in enumerate(chips):
            copy(1 + j, (*chip, c), me).wait_recv()
            passed[j].start()
        copy(0, sibling, me).wait_recv()
        for j, chip in enumerate(chips):
            copy(4 + j, (*chip, 1 - c), me).wait_recv()
        for cp in first + passed:
            cp.wait_send()
        mine.wait()
        tot = all_ref[0:m_per, :]
        for dev in range(1, 8):
            tot = tot + all_ref[dev * m_per:(dev + 1) * m_per, :]
        tot_ref[...] = tot

    vmem = pl.BlockSpec(memory_space=pltpu.VMEM)
    return pl.pallas_call(
        body, name="allreduce_small", out_shape=_sds((m_per, n), F32), in_specs=[vmem], out_specs=vmem,
        scratch_shapes=[pltpu.VMEM((8 * m_per, n), F32), pltpu.SemaphoreType.DMA((7,)), pltpu.SemaphoreType.DMA((7,)),
                        pltpu.SemaphoreType.DMA],
    )(v)


def _add_halves(name, terms):
    rows, width = terms[0].shape
    tr = max(t for t in range(8, ADD_ROWS + 1, 8) if rows % t == 0)

    def kern(*refs):
        acc = refs[0][...]
        for r in refs[1:-1]:
            acc = acc + r[...]
        refs[-1][...] = acc

    spec = pl.BlockSpec((tr, width), lambda i: (i, 0))
    return pl.pallas_call(kern, name=name, grid=(rows // tr,), in_specs=[spec] * len(terms), out_specs=spec,
                          out_shape=_sds((rows, width), F32), compiler_params=_params(1))(*terms)


def _adamw(name, w, g, m, v):
    rows, width = w.shape
    tr = rows // 4 if rows % 32 == 0 else rows

    def kern(w_ref, g_ref, m_ref, v_ref, d_ref, mo_ref, vo_ref):
        g_v = g_ref[...]
        m_new = ADAM_B1 * m_ref[...] + (1.0 - ADAM_B1) * g_v
        v_new = ADAM_B2 * v_ref[...] + (1.0 - ADAM_B2) * (g_v * g_v)
        m_hat = m_new / (1.0 - ADAM_B1 ** ADAM_STEP)
        v_hat = v_new / (1.0 - ADAM_B2 ** ADAM_STEP)
        d_ref[...] = -ADAM_LR * (m_hat / (jnp.sqrt(v_hat) + ADAM_EPS) + ADAM_WD * w_ref[...])
        mo_ref[...] = m_new
        vo_ref[...] = v_new

    spec = pl.BlockSpec((tr, width), lambda i: (i, 0))
    return pl.pallas_call(kern, name=name, grid=(rows // tr,), in_specs=[spec] * 4, out_specs=[spec] * 3,
                          out_shape=[_sds((rows, width), F32)] * 3, compiler_params=_params(1))(w, g, m, v)


SHARDED = (("w_in", D_MODEL, IN_WIDTH, 1), ("w_uq", Q_RANK, MLA_HEADS * MLA_QK, 1),
           ("w_ukv", KV_RANK, MLA_HEADS * (MLA_NOPE + MLA_V), 1), ("w_o", D_MODEL, D_MODEL, 0),
           ("w_gate", D_MODEL, D_FF, 1), ("w_up", D_MODEL, D_FF, 1), ("w_down", D_FF, D_MODEL, 0))
SMALL = (("norm_mix", D_MODEL), ("q_latent_norm", Q_RANK), ("kv_latent_norm", KV_RANK), ("out_norm_mla", MLA_WIDTH),
         ("out_norm_sb", SB_WIDTH), ("norm_ffn", D_MODEL), ("norm_final", D_MODEL))


def _pack_local(blocks):
    return jnp.concatenate([b.reshape(-1, PACK_W) for b in blocks], axis=0)


def _unpack_full(packed):
    out, off = {}, 0
    for name, r, c, axis in SHARDED:
        n = r * c // N_SHARD // PACK_W
        piece = packed[:, off:off + n, :]
        off += n
        if axis == 1:
            out[name] = piece.reshape(N_SHARD, r, c // N_SHARD).transpose(1, 0, 2).reshape(r, c)
        else:
            out[name] = piece.reshape(r, c)
    return out


def _pack_grads(grads):
    parts = []
    for name, r, c, axis in SHARDED:
        g = grads[name]
        if axis == 1:
            g = g.reshape(r, N_SHARD, c // N_SHARD).transpose(1, 0, 2)
        parts.append(g.reshape(N_SHARD, -1, PACK_W))
    return jnp.concatenate(parts, axis=1)


def _unpack_shard(packed):
    out, off = {}, 0
    for name, r, c, axis in SHARDED:
        n = r * c // N_SHARD // PACK_W
        shape = (r, c // N_SHARD) if axis == 1 else (r // N_SHARD, c)
        out[name] = packed[off:off + n, :].reshape(shape)
        off += n
    return out


def _rot_cols(w):
    hh = MLA_ROPE // 2
    return jnp.concatenate([-w[..., hh:], w[..., :hh]], axis=-1)


def _rot_cols_t(g):
    hh = MLA_ROPE // 2
    return jnp.concatenate([g[..., hh:], -g[..., :hh]], axis=-1)


def _prepare_weights(full, small):
    w_in = full["w_in"]
    s0, s1, s2 = Q_RANK, Q_RANK + KV_RANK, Q_RANK + KV_RANK + MLA_ROPE
    w_uq3 = full["w_uq"].reshape(Q_RANK, MLA_HEADS, MLA_QK)
    w_uqr = jnp.concatenate([jnp.zeros_like(w_uq3[..., :MLA_NOPE]), _rot_cols(w_uq3[..., MLA_NOPE:])], axis=-1)
    w = {
        "w_cq": w_in[:, :s0], "w_ckv": w_in[:, s0:s1], "w_kr": w_in[:, s1:s2], "w_krr": _rot_cols(w_in[:, s1:s2]),
        "w_sb": w_in[:, s2:], "w_uq": full["w_uq"], "w_uqr": w_uqr.reshape(Q_RANK, MLA_HEADS * MLA_QK),
        "w_ukv": full["w_ukv"], "w_oa": full["w_o"][:MLA_WIDTH], "w_ob": full["w_o"][MLA_WIDTH:],
        "w_gate": full["w_gate"], "w_up": full["w_up"], "w_down": full["w_down"],
    }
    for name in list(w):
        w[name + "_t"] = w[name].T
    w.update(g_mix=small["norm_mix"], g_q=small["q_latent_norm"], g_kv=small["kv_latent_norm"], g_a=small["out_norm_mla"],
             g_b=small["out_norm_sb"], g_f=small["norm_ffn"], g_n=small["norm_final"])
    return w


def _rope_tables(positions):
    inv_freq = ROPE_THETA ** (-jnp.arange(0, MLA_ROPE, 2, dtype=F32) / MLA_ROPE)
    ang = positions.astype(F32)[:, None] * inv_freq[None, :]
    cos, sin = jnp.cos(ang), jnp.sin(ang)
    c32 = jnp.concatenate([cos, cos], axis=1)
    s32 = jnp.concatenate([sin, sin], axis=1)
    s = positions.shape[0]
    c96 = jnp.concatenate([jnp.ones((s, MLA_NOPE), F32), c32], axis=1)
    s96 = jnp.concatenate([jnp.zeros((s, MLA_NOPE), F32), s32], axis=1)
    return {"c32": c32, "s32": s32, "c768": jnp.tile(c96, (1, MLA_HEADS)), "s768": jnp.tile(s96, (1, MLA_HEADS))}


def _heads(t, n_heads):
    s = t.shape[0]
    return t.reshape(s, n_heads, -1).transpose(1, 0, 2)


def _tokens(t):
    h, s, d = t.shape
    return t.transpose(1, 0, 2).reshape(s, h * d)


def kernel(x, positions, norm_mix, w_in, q_latent_norm, w_uq, kv_latent_norm, w_ukv, out_norm_mla, out_norm_sb, w_o, norm_ffn, w_gate, w_up, w_down, norm_final, loss_target, m_norm_mix, m_w_in, m_q_latent_norm, m_w_uq, m_kv_latent_norm, m_w_ukv, m_out_norm_mla, m_out_norm_sb, m_w_o, m_norm_ffn, m_w_gate, m_w_up, m_w_down, m_norm_final, v_norm_mix, v_w_in, v_q_latent_norm, v_w_uq, v_kv_latent_norm, v_w_ukv, v_out_norm_mla, v_out_norm_sb, v_w_o, v_norm_ffn, v_w_gate, v_w_up, v_w_down, v_norm_final):
    given = dict(norm_mix=norm_mix, w_in=w_in, q_latent_norm=q_latent_norm, w_uq=w_uq, kv_latent_norm=kv_latent_norm, w_ukv=w_ukv,
                 out_norm_mla=out_norm_mla, out_norm_sb=out_norm_sb, w_o=w_o, norm_ffn=norm_ffn, w_gate=w_gate, w_up=w_up,
                 w_down=w_down, norm_final=norm_final)
    mom_m = dict(norm_mix=m_norm_mix, w_in=m_w_in, q_latent_norm=m_q_latent_norm, w_uq=m_w_uq, kv_latent_norm=m_kv_latent_norm,
                 w_ukv=m_w_ukv, out_norm_mla=m_out_norm_mla, out_norm_sb=m_out_norm_sb, w_o=m_w_o, norm_ffn=m_norm_ffn,
                 w_gate=m_w_gate, w_up=m_w_up, w_down=m_w_down, norm_final=m_norm_final)
    mom_v = dict(norm_mix=v_norm_mix, w_in=v_w_in, q_latent_norm=v_q_latent_norm, w_uq=v_w_uq, kv_latent_norm=v_kv_latent_norm,
                 w_ukv=v_w_ukv, out_norm_mla=v_out_norm_mla, out_norm_sb=v_out_norm_sb, w_o=v_w_o, norm_ffn=v_norm_ffn,
                 w_gate=v_w_gate, w_up=v_w_up, w_down=v_w_down, norm_final=v_norm_final)
    xs = x[0]
    tgt = loss_target[0]
    s = xs.shape[0]
    c_idx = lax.axis_index("c")

    shard2d = {name: given[name].reshape(given[name].shape[-2:]) for name, *_ in SHARDED}
    packed_w = _pack_local([shard2d[name].astype(BF16) for name, *_ in SHARDED])
    gathered = _allgather_shards(packed_w).reshape(N_SHARD, packed_w.shape[0], PACK_W)
    small = {name: given[name].reshape(1, n) for name, n in SMALL}
    w = _prepare_weights(_unpack_full(gathered), small)
    tabs = _rope_tables(positions[0])
    msuf, mpre = _sb_masks(min(SB_TK, s))

    u, cq, ckv, cqn, ckvn, q768, kv, kr, sb = _fwd_a(xs, tabs, w)
    q_mla = _heads(q768, MLA_HEADS)
    kv_h = _heads(kv, MLA_HEADS)
    k_mla = jnp.concatenate([kv_h[..., :MLA_NOPE], jnp.broadcast_to(kr[None], (MLA_HEADS, s, MLA_ROPE))], axis=-1)
    v_mla = kv_h[..., MLA_NOPE:]
    sb_h = sb.reshape(s, 3, SB_HEADS, SB_DIM).transpose(1, 2, 0, 3)
    o_mla_h, lse = _mla_fwd(q_mla, k_mla, v_mla)
    o_sb_h, cmat = _sb_fwd(sb_h[0], sb_h[1], sb_h[2], msuf)
    o_mla, o_sb = _tokens(o_mla_h), _tokens(o_sb_h)
    merged, h1, f, gate, up, act = _fwd_b1(xs, o_mla, o_sb, w)
    dh2, loss_part, dg_n = _fwd_b2(h1, act, tgt, w)

    dgate, dup = _bwd_b1(dh2, gate, up, w)
    dh1, do_mla, do_sb, dg_f, dg_a, dg_b = _bwd_b2(dgate, dup, h1, dh2, o_mla, o_sb, w)
    dq_mla, dk_mla, dv_mla = _mla_bwd(q_mla, k_mla, v_mla, o_mla_h, _heads(do_mla, MLA_HEADS), lse)
    dq_sb, dk_sb, dv_sb = _sb_bwd(sb_h[0], sb_h[1], sb_h[2], _heads(do_sb, SB_HEADS), cmat, msuf, mpre)
    dkv = _tokens(jnp.concatenate([dk_mla[..., :MLA_NOPE], dv_mla], axis=-1))
    dkr = jnp.sum(dk_mla[..., MLA_NOPE:], axis=0)
    dsb = jnp.stack([dq_sb, dk_sb, dv_sb]).astype(BF16).transpose(2, 0, 1, 3).reshape(s, 3 * SB_WIDTH)
    dx, a1, a2, dcq, dckv, dkrc, dkrs, dg_q, dg_kv, dg_mix = _bwd_a(xs, dh1, cq, ckv, _tokens(dq_mla), dkv, dkr, dsb, tabs, w)

    g_kr = _tn_matmul("dw_kr", u, dkrc) + _rot_cols_t(_tn_matmul("dw_krr", u, dkrs))
    g_uqr = _tn_matmul("dw_uqr", cqn, a2).reshape(Q_RANK, MLA_HEADS, MLA_QK)
    g_uqr = jnp.concatenate([jnp.zeros_like(g_uqr[..., :MLA_NOPE]), _rot_cols_t(g_uqr[..., MLA_NOPE:])], axis=-1)
    grads = {
        "w_in": jnp.concatenate([_tn_matmul("dw_cq", u, dcq), _tn_matmul("dw_ckv", u, dckv), g_kr, _tn_matmul("dw_sb", u, dsb)], axis=1),
        "w_uq": _tn_matmul("dw_uq", cqn, a1) + g_uqr.reshape(Q_RANK, MLA_HEADS * MLA_QK),
        "w_ukv": _tn_matmul("dw_ukv", ckvn, dkv),
        "w_o": _tn_matmul("dw_o", merged, dh1),
        "w_gate": _tn_matmul("dw_gate", f, dgate),
        "w_up": _tn_matmul("dw_up", f, dup),
        "w_down": _tn_matmul("dw_down", act, dh2),
    }

    packed_g = _pack_grads(grads)
    half = packed_g.shape[1] // 2
    keep = lax.dynamic_slice_in_dim(packed_g, c_idx * half, half, axis=1)
    give = lax.dynamic_slice_in_dim(packed_g, (1 - c_idx) * half, half, axis=1)
    got = _sibling_swap("swap_halves", give)
    chip_sum = _add_halves("add_sibling", [keep.reshape(-1, PACK_W), got.reshape(-1, PACK_W)]).reshape(keep.shape)
    slots = _chip_scatter(chip_sum)
    mine = _add_halves("add_chips", [slots[b] for b in range(N_SHARD)])
    theirs = _sibling_swap("swap_result", mine)
    lo = jnp.where(c_idx == 0, mine, theirs)
    hi = jnp.where(c_idx == 0, theirs, mine)
    g_shard = _unpack_shard(jnp.concatenate([lo, hi], axis=0))

    small_parts = jnp.concatenate([dg_mix, dg_q, dg_kv, dg_a, dg_b, dg_f, dg_n], axis=1)
    small_g = _allreduce_small(jnp.broadcast_to(small_parts, (8, small_parts.shape[1])))[0:1]
    loss = lax.psum(loss_part[0, 0], ("x", "y", "c"))

    g_out, d_out, m_out, v_out = {}, {}, {}, {}
    for name, *_ in SHARDED:
        shape = given[name].shape
        d, mn, vn = _adamw("adamw_" + name, shard2d[name], g_shard[name], mom_m[name].reshape(shard2d[name].shape),
                           mom_v[name].reshape(shard2d[name].shape))
        g_out[name], d_out[name], m_out[name], v_out[name] = (t.reshape(shape) for t in (g_shard[name], d, mn, vn))
    cat = lambda src: jnp.concatenate([src[name].reshape(1, n) for name, n in SMALL], axis=1)
    d, mn, vn = _adamw("adamw_small", cat(given), small_g, cat(mom_m), cat(mom_v))
    off = 0
    for name, n in SMALL:
        shape = given[name].shape
        g_out[name], d_out[name], m_out[name], v_out[name] = (t[:, off:off + n].reshape(shape) for t in (small_g, d, mn, vn))
        off += n

    order = ["norm_mix", "w_in", "q_latent_norm", "w_uq", "kv_latent_norm", "w_ukv", "out_norm_mla", "out_norm_sb", "w_o",
             "norm_ffn", "w_gate", "w_up", "w_down", "norm_final"]
    return (loss, dx[None], *[g_out[n] for n in order], *[d_out[n] for n in order], *[m_out[n] for n in order],
            *[v_out[n] for n in order])
```

```python
import functools
import math

import jax
import jax.numpy as jnp
from jax import lax
from jax.experimental import pallas as pl
from jax.experimental.pallas import tpu as pltpu

F32 = jnp.float32
BF16 = jnp.bfloat16
MESH = pl.DeviceIdType.MESH

D_MODEL = 1024
EPS = 1e-6
MLA_HEADS = 8
MLA_NOPE = 64
MLA_ROPE = 32
MLA_V = 64
MLA_QK = MLA_NOPE + MLA_ROPE
Q_RANK = 256
KV_RANK = 128
ROPE_THETA = 10000.0
SB_HEADS = 8
SB_DIM = 64
MLA_WIDTH = MLA_HEADS * MLA_V
SB_WIDTH = SB_HEADS * SB_DIM
D_FF = 2816
IN_WIDTH = Q_RANK + KV_RANK + MLA_ROPE + 3 * SB_WIDTH

ADAM_LR = 0.001
ADAM_B1 = 0.9
ADAM_B2 = 0.999
ADAM_EPS = 1e-08
ADAM_WD = 0.01
ADAM_STEP = 10

N_SHARD = 4
PACK_W = 256
LANES = 128
VMEM_LIMIT = 56 * 1024 * 1024
TN_ACC_BYTES = 6 * 1024 * 1024 + 512 * 1024
NEG = -1e30
SB_SKIP = 110.0

ROW_TILE = 256
MLA_TQ = 1024
SB_TQ = 512
MLA_TK = 1024
SB_TK = 256
TN_TS = 512
ADD_ROWS = 3072


def _dot(a, b):
    return jnp.dot(a, b, preferred_element_type=F32)


def _dot_nt(a, b):
    return lax.dot_general(a, b, (((1,), (1,)), ((), ())), preferred_element_type=F32)


def _dot_tn(a, b):
    return lax.dot_general(a, b, (((0,), (0,)), ((), ())), preferred_element_type=F32)


def _params(n_grid, vmem=VMEM_LIMIT):
    return pltpu.CompilerParams(dimension_semantics=("arbitrary",) * n_grid, vmem_limit_bytes=vmem)


def _rms(x):
    r = lax.rsqrt(jnp.mean(x * x, axis=-1, keepdims=True) + EPS)
    return x * r, r


def _rms_bwd(n, r, g, dy):
    dn = dy * g
    dx = r * (dn - n * jnp.mean(dn * n, axis=-1, keepdims=True))
    return dx, jnp.sum(dy * n, axis=0, keepdims=True)


def _accumulate(ref, val, step):
    @pl.when(step == 0)
    def _():
        ref[...] = val

    @pl.when(step != 0)
    def _():
        ref[...] += val


def _rowwise(name, body, rows, consts, row_out, acc_out, tm):
    n_rows = rows[0].shape[0]
    tm = min(tm, n_rows)
    nr, nc, no = len(rows), len(consts), len(row_out)

    def kern(*refs):
        body(refs[:nr], refs[nr:nr + nc], refs[nr + nc:nr + nc + no], refs[nr + nc + no:], pl.program_id(0))

    in_specs = [pl.BlockSpec((tm, a.shape[1]), lambda i: (i, 0)) for a in rows]
    in_specs += [pl.BlockSpec(a.shape, lambda i: (0, 0), pipeline_mode=pl.Buffered(1)) for a in consts]
    out_specs = [pl.BlockSpec((tm, s.shape[1]), lambda i: (i, 0)) for s in row_out]
    out_specs += [pl.BlockSpec(s.shape, lambda i: (0, 0)) for s in acc_out]
    return pl.pallas_call(
        kern, name=name, grid=(n_rows // tm,), in_specs=in_specs, out_specs=out_specs,
        out_shape=list(row_out) + list(acc_out), compiler_params=_params(1),
    )(*rows, *consts)


def _sds(shape, dtype):
    return jax.ShapeDtypeStruct(shape, dtype)


def _fwd_a(x, tabs, w):
    s = x.shape[0]

    def body(r, c, o, a, step):
        x_ref, c768, s768, c32, s32 = r
        gmix, wcq, wckv, wkr, wkrr, wsb, gq, wuq, wuqr, gkv, wukv = c
        u_o, cq_o, ckv_o, cqn_o, ckvn_o, q_o, kv_o, kr_o, sb_o = o
        n, _ = _rms(x_ref[...])
        u = (n * gmix[...]).astype(BF16)
        u_o[...] = u
        cq = _dot(u, wcq[...])
        ckv = _dot(u, wckv[...])
        kr = _dot(u, wkr[...]) * c32[...] + _dot(u, wkrr[...]) * s32[...]
        sb_o[...] = _dot(u, wsb[...]).astype(BF16)
        cq_o[...] = cq
        ckv_o[...] = ckv
        kr_o[...] = kr.astype(BF16)
        nq, _ = _rms(cq)
        cqn = (nq * gq[...]).astype(BF16)
        cqn_o[...] = cqn
        q = _dot(cqn, wuq[...]) * c768[...] + _dot(cqn, wuqr[...]) * s768[...]
        q_o[...] = q.astype(BF16)
        nkv, _ = _rms(ckv)
        ckvn = (nkv * gkv[...]).astype(BF16)
        ckvn_o[...] = ckvn
        kv_o[...] = _dot(ckvn, wukv[...]).astype(BF16)

    outs = [
        _sds((s, D_MODEL), BF16), _sds((s, Q_RANK), F32), _sds((s, KV_RANK), F32), _sds((s, Q_RANK), BF16),
        _sds((s, KV_RANK), BF16), _sds((s, MLA_HEADS * MLA_QK), BF16), _sds((s, MLA_HEADS * (MLA_NOPE + MLA_V)), BF16),
        _sds((s, MLA_ROPE), BF16), _sds((s, 3 * SB_WIDTH), BF16),
    ]
    consts = [w["g_mix"], w["w_cq"], w["w_ckv"], w["w_kr"], w["w_krr"], w["w_sb"], w["g_q"], w["w_uq"], w["w_uqr"],
              w["g_kv"], w["w_ukv"]]
    return _rowwise("fwd_a", body, [x, tabs["c768"], tabs["s768"], tabs["c32"], tabs["s32"]], consts, outs, [], ROW_TILE)


def _fwd_b1(x, o_mla, o_sb, w):
    s = x.shape[0]

    def body(r, c, o, a, step):
        x_ref, oa_ref, ob_ref = r
        ga, gb, woa, wob, gf, wg, wu = c
        mg_o, h1_o, f_o, gate_o, up_o, act_o = o
        na, _ = _rms(oa_ref[...])
        nb, _ = _rms(ob_ref[...])
        ma = (na * ga[...]).astype(BF16)
        mb = (nb * gb[...]).astype(BF16)
        mg_o[:, :MLA_WIDTH] = ma
        mg_o[:, MLA_WIDTH:] = mb
        h1 = x_ref[...] + _dot(ma, woa[...]) + _dot(mb, wob[...])
        h1_o[...] = h1
        nf, _ = _rms(h1)
        f = (nf * gf[...]).astype(BF16)
        f_o[...] = f
        gate = _dot(f, wg[...])
        up = _dot(f, wu[...])
        gate_o[...] = gate.astype(BF16)
        up_o[...] = up.astype(BF16)
        act_o[...] = (gate * (1.0 / (1.0 + jnp.exp(-gate))) * up).astype(BF16)

    outs = [_sds((s, D_MODEL), BF16), _sds((s, D_MODEL), F32), _sds((s, D_MODEL), BF16), _sds((s, D_FF), BF16),
            _sds((s, D_FF), BF16), _sds((s, D_FF), BF16)]
    consts = [w["g_a"], w["g_b"], w["w_oa"], w["w_ob"], w["g_f"], w["w_gate"], w["w_up"]]
    return _rowwise("fwd_b1", body, [x, o_mla, o_sb], consts, outs, [], ROW_TILE)


def _fwd_b2(h1, act, tgt, w):
    s = h1.shape[0]

    def body(r, c, o, a, step):
        h1_ref, act_ref, t_ref = r
        wd, gn = c
        (dh2_o,) = o
        loss_o, dgn_o = a
        h2 = h1_ref[...] + _dot(act_ref[...], wd[...])
        n2, r2 = _rms(h2)
        err = n2 * gn[...] - t_ref[...]
        part = jnp.sum(jnp.sum(err * err, axis=1, keepdims=True), axis=0, keepdims=True) * (0.5 / D_MODEL)
        _accumulate(loss_o, jnp.broadcast_to(part, (1, LANES)), step)
        dh2, dgn = _rms_bwd(n2, r2, gn[...], err * (1.0 / D_MODEL))
        dh2_o[...] = dh2
        _accumulate(dgn_o, dgn, step)

    return _rowwise("fwd_b2", body, [h1, act, tgt], [w["w_down"], w["g_n"]], [_sds((s, D_MODEL), F32)],
                    [_sds((1, LANES), F32), _sds((1, D_MODEL), F32)], ROW_TILE)


def _bwd_b1(dh2, gate, up, w):
    s = dh2.shape[0]

    def body(r, c, o, a, step):
        dh2_ref, gate_ref, up_ref = r
        (wdt,) = c
        dgate_o, dup_o = o
        dact = _dot(dh2_ref[...].astype(BF16), wdt[...])
        gate = gate_ref[...].astype(F32)
        sig = 1.0 / (1.0 + jnp.exp(-gate))
        dup_o[...] = (dact * (gate * sig)).astype(BF16)
        dgate_o[...] = (dact * up_ref[...].astype(F32) * (sig * (1.0 + gate * (1.0 - sig)))).astype(BF16)

    return _rowwise("bwd_b1", body, [dh2, gate, up], [w["w_down_t"]], [_sds((s, D_FF), BF16), _sds((s, D_FF), BF16)],
                    [], ROW_TILE)


def _bwd_b2(dgate, dup, h1, dh2, o_mla, o_sb, w):
    s = h1.shape[0]

    def body(r, c, o, a, step):
        dgate_ref, dup_ref, h1_ref, dh2_ref, oa_ref, ob_ref = r
        wgt, wut, gf, woat, wobt, ga, gb = c
        dh1_o, doa_o, dob_o = o
        dgf_o, dga_o, dgb_o = a
        df = _dot(dgate_ref[...], wgt[...]) + _dot(dup_ref[...], wut[...])
        nf, rf = _rms(h1_ref[...])
        dres, dgf = _rms_bwd(nf, rf, gf[...], df)
        dh1 = dh2_ref[...] + dres
        dh1_o[...] = dh1
        dh1b = dh1.astype(BF16)
        na, ra = _rms(oa_ref[...])
        doa, dga = _rms_bwd(na, ra, ga[...], _dot(dh1b, woat[...]))
        nb, rb = _rms(ob_ref[...])
        dob, dgb = _rms_bwd(nb, rb, gb[...], _dot(dh1b, wobt[...]))
        doa_o[...] = doa
        dob_o[...] = dob
        _accumulate(dgf_o, dgf, step)
        _accumulate(dga_o, dga, step)
        _accumulate(dgb_o, dgb, step)

    consts = [w["w_gate_t"], w["w_up_t"], w["g_f"], w["w_oa_t"], w["w_ob_t"], w["g_a"], w["g_b"]]
    outs = [_sds((s, D_MODEL), F32), _sds((s, MLA_WIDTH), F32), _sds((s, SB_WIDTH), F32)]
    accs = [_sds((1, D_MODEL), F32), _sds((1, MLA_WIDTH), F32), _sds((1, SB_WIDTH), F32)]
    return _rowwise("bwd_b2", body, [dgate, dup, h1, dh2, o_mla, o_sb], consts, outs, accs, ROW_TILE)


def _bwd_a(x, dh1, cq, ckv, dq, dkv, dkr, dsb, tabs, w):
    s = x.shape[0]

    def body(r, c, o, a, step):
        x_ref, dh1_ref, cq_ref, ckv_ref, dq_ref, dkv_ref, dkr_ref, dsb_ref, c768, s768, c32, s32 = r
        wuqt, wuqrt, gq, wukvt, gkv, wcqt, wckvt, wkrt, wkrrt, wsbt, gmix = c
        dx_o, a1_o, a2_o, dcq_o, dckv_o, dkrc_o, dkrs_o = o
        dgq_o, dgkv_o, dgmix_o = a
        dq = dq_ref[...]
        a1 = (dq * c768[...]).astype(BF16)
        a2 = (dq * s768[...]).astype(BF16)
        a1_o[...] = a1
        a2_o[...] = a2
        nq, rq = _rms(cq_ref[...])
        dcq, dgq = _rms_bwd(nq, rq, gq[...], _dot(a1, wuqt[...]) + _dot(a2, wuqrt[...]))
        nkv, rkv = _rms(ckv_ref[...])
        dckv, dgkv = _rms_bwd(nkv, rkv, gkv[...], _dot(dkv_ref[...].astype(BF16), wukvt[...]))
        dkr = dkr_ref[...]
        dcq_b = dcq.astype(BF16)
        dckv_b = dckv.astype(BF16)
        dkrc = (dkr * c32[...]).astype(BF16)
        dkrs = (dkr * s32[...]).astype(BF16)
        dcq_o[...] = dcq_b
        dckv_o[...] = dckv_b
        dkrc_o[...] = dkrc
        dkrs_o[...] = dkrs
        du = (_dot(dcq_b, wcqt[...]) + _dot(dckv_b, wckvt[...]) + _dot(dkrc, wkrt[...]) + _dot(dkrs, wkrrt[...])
              + _dot(dsb_ref[...], wsbt[...]))
        nx, rx = _rms(x_ref[...])
        dres, dgmix = _rms_bwd(nx, rx, gmix[...], du)
        dx_o[...] = dh1_ref[...] + dres
        _accumulate(dgq_o, dgq, step)
        _accumulate(dgkv_o, dgkv, step)
        _accumulate(dgmix_o, dgmix, step)

    consts = [w["w_uq_t"], w["w_uqr_t"], w["g_q"], w["w_ukv_t"], w["g_kv"], w["w_cq_t"], w["w_ckv_t"], w["w_kr_t"],
              w["w_krr_t"], w["w_sb_t"], w["g_mix"]]
    outs = [_sds((s, D_MODEL), F32), _sds((s, MLA_HEADS * MLA_QK), BF16), _sds((s, MLA_HEADS * MLA_QK), BF16),
            _sds((s, Q_RANK), BF16), _sds((s, KV_RANK), BF16), _sds((s, MLA_ROPE), BF16), _sds((s, MLA_ROPE), BF16)]
    accs = [_sds((1, Q_RANK), F32), _sds((1, KV_RANK), F32), _sds((1, D_MODEL), F32)]
    rows = [x, dh1, cq, ckv, dq, dkv, dkr, dsb, tabs["c768"], tabs["s768"], tabs["c32"], tabs["s32"]]
    return _rowwise("bwd_a", body, rows, consts, outs, accs, ROW_TILE)


def _tn_tile(k, n):
    if n % LANES or k * n * 4 <= TN_ACC_BYTES:
        return n
    units = n // LANES
    best = 1
    for d in range(1, units + 1):
        if units % d == 0 and k * d * LANES * 4 <= TN_ACC_BYTES:
            best = d
    return best * LANES


def _tn_matmul(name, x, y):
    s, k = x.shape
    n = y.shape[1]
    ts = min(TN_TS, s)
    tn = _tn_tile(k, n)

    def kern(x_ref, y_ref, o_ref):
        step = pl.program_id(1)
        _accumulate(o_ref, _dot_tn(x_ref[...].astype(BF16), y_ref[...].astype(BF16)), step)

    return pl.pallas_call(
        kern, name=name, grid=(n // tn, s // ts),
        in_specs=[pl.BlockSpec((ts, k), lambda j, i: (i, 0)), pl.BlockSpec((ts, tn), lambda j, i: (i, j))],
        out_specs=pl.BlockSpec((k, tn), lambda j, i: (0, j)), out_shape=_sds((k, n), F32), compiler_params=_params(2),
    )(x, y)


def _causal(tq, tk, d):
    row = lax.broadcasted_iota(jnp.int32, (tq, tk), 0)
    col = lax.broadcasted_iota(jnp.int32, (tq, tk), 1)
    return col + d * tk <= row


def _mla_fwd(q, k, v, tq=MLA_TQ, tk=MLA_TK):
    h, s, dq = q.shape
    dv = v.shape[2]
    tq, tk = min(tq, s), min(tk, s)
    ratio = tq // tk
    scale = 1.0 / math.sqrt(dq)

    def kern(q_ref, k_ref, v_ref, o_ref, lse_ref):
        i = pl.program_id(1)
        qv = q_ref[...]

        def block(kb, carry, dd):
            m, l, acc = carry
            ks = pl.ds(pl.multiple_of(kb * tk, tk), tk)
            sc = _dot_nt(qv, k_ref[ks, :]) * scale
            if dd is not None:
                sc = jnp.where(_causal(tq, tk, dd), sc, NEG)
            m_new = jnp.maximum(m, jnp.max(sc, axis=1, keepdims=True))
            p = jnp.exp(sc - m_new)
            alpha = jnp.exp(m - m_new)
            l = alpha * l + jnp.sum(p, axis=1, keepdims=True)
            acc = alpha * acc + _dot(p.astype(BF16), v_ref[ks, :])
            return m_new, l, acc

        carry = (jnp.full((tq, 1), NEG, F32), jnp.zeros((tq, 1), F32), jnp.zeros((tq, dv), F32))
        carry = lax.fori_loop(0, i * ratio, lambda kb, c: block(kb, c, None), carry)
        for dd in range(ratio):
            carry = block(i * ratio + dd, carry, dd)
        m, l, acc = carry
        o_ref[...] = acc / l
        lse_ref[...] = jnp.broadcast_to(m + jnp.log(l), (tq, LANES))

    return pl.pallas_call(
        kern, name="mla_fwd", grid=(h, s // tq),
        in_specs=[pl.BlockSpec((None, tq, dq), lambda a, i: (a, i, 0)), pl.BlockSpec((None, s, dq), lambda a, i: (a, 0, 0)),
                  pl.BlockSpec((None, s, dv), lambda a, i: (a, 0, 0))],
        out_specs=[pl.BlockSpec((None, tq, dv), lambda a, i: (a, i, 0)), pl.BlockSpec((None, tq, LANES), lambda a, i: (a, i, 0))],
        out_shape=[_sds((h, s, dv), F32), _sds((h, s, LANES), F32)], compiler_params=_params(2),
    )(q, k, v)


def _mla_bwd(q, k, v, o, do, lse, tq=MLA_TQ, tk=MLA_TK):
    h, s, dq = q.shape
    dv = v.shape[2]
    tq, tk = min(tq, s), min(tk, s)
    ratio = tq // tk
    scale = 1.0 / math.sqrt(dq)

    def kern(q_ref, k_ref, v_ref, o_ref, do_ref, lse_ref, dq_ref, dk_ref, dv_ref):
        i = pl.program_id(1)

        @pl.when(i == 0)
        def _():
            dk_ref[...] = jnp.zeros_like(dk_ref)
            dv_ref[...] = jnp.zeros_like(dv_ref)

        qv = q_ref[...]
        do_f = do_ref[...]
        do_b = do_f.astype(BF16)
        delta = jnp.sum(do_f * o_ref[...], axis=1, keepdims=True)
        lse_v = lse_ref[:, 0:1]

        def block(kb, dq_acc, dd):
            ks = pl.ds(pl.multiple_of(kb * tk, tk), tk)
            kk = k_ref[ks, :]
            vv = v_ref[ks, :]
            p = jnp.exp(_dot_nt(qv, kk) * scale - lse_v)
            if dd is not None:
                p = jnp.where(_causal(tq, tk, dd), p, 0.0)
            ds = (p * (_dot_nt(do_b, vv) - delta) * scale).astype(BF16)
            dv_ref[ks, :] += _dot_tn(p.astype(BF16), do_b)
            dk_ref[ks, :] += _dot_tn(ds, qv)
            return dq_acc + _dot(ds, kk)

        acc = lax.fori_loop(0, i * ratio, lambda kb, c: block(kb, c, None), jnp.zeros((tq, dq), F32))
        for dd in range(ratio):
            acc = block(i * ratio + dd, acc, dd)
        dq_ref[...] = acc

    qspec = lambda d: pl.BlockSpec((None, tq, d), lambda a, i: (a, i, 0))
    full = lambda d: pl.BlockSpec((None, s, d), lambda a, i: (a, 0, 0))
    return pl.pallas_call(
        kern, name="mla_bwd", grid=(h, s // tq),
        in_specs=[qspec(dq), full(dq), full(dv), qspec(dv), qspec(dv), qspec(LANES)],
        out_specs=[qspec(dq), full(dq), full(dv)],
        out_shape=[_sds((h, s, dq), F32), _sds((h, s, dq), F32), _sds((h, s, dv), F32)], compiler_params=_params(2),
    )(q, k, v, o, do, lse)


def _sb_masks(tk):
    j = lax.broadcasted_iota(jnp.int32, (tk, tk), 0)
    c = lax.broadcasted_iota(jnp.int32, (tk, tk), 1)
    return (j > c).astype(BF16), (j < c).astype(BF16)


def _sb_scores(qs, kk, msuf, strict):
    z = _dot_nt(qs, kk)
    lom = -(jnp.maximum(z, 0.0) + jnp.log(1.0 + jnp.exp(-jnp.abs(z))))
    if strict is not None:
        lom = jnp.where(strict, lom, 0.0)
    hi = lom.astype(BF16)
    lo = (lom - hi.astype(F32)).astype(BF16)
    return z, lom, _dot(hi, msuf) + _dot(lo, msuf)


def _sb_strict(tq, tk, d):
    row = lax.broadcasted_iota(jnp.int32, (tq, tk), 0)
    col = lax.broadcasted_iota(jnp.int32, (tq, tk), 1)
    return col + d * tk < row


def _sb_fwd(q, k, v, msuf, tq=SB_TQ, tk=SB_TK):
    h, s, d = q.shape
    tq, tk = min(tq, s), min(tk, s)
    ratio = tq // tk

    def kern(q_ref, k_ref, v_ref, m_ref, o_ref, c_ref):
        i = pl.program_id(1)
        qs = q_ref[...] * 0.125
        msf = m_ref[...]
        lane = lax.broadcasted_iota(jnp.int32, (tq, LANES), 1)

        def block(kb, carry, dd):
            c, acc, cm = carry
            ks = pl.ds(pl.multiple_of(kb * tk, tk), tk)
            strict = None if dd is None else _sb_strict(tq, tk, dd)
            z, lom, suf = _sb_scores(qs, k_ref[ks, :], msf, strict)
            a = jnp.exp(z + lom + (suf + c))
            if strict is not None:
                a = jnp.where(strict, a, 0.0)
            acc = acc + _dot(a.astype(BF16), v_ref[ks, :])
            cm = jnp.where(lane == kb, c, cm)
            return c + jnp.sum(lom, axis=1, keepdims=True), acc, cm

        carry = (jnp.zeros((tq, 1), F32), jnp.zeros((tq, d), F32), jnp.full((tq, LANES), NEG, F32))
        for dd in range(ratio - 1, -1, -1):
            carry = block(i * ratio + dd, carry, dd)

        def live(st):
            return jnp.logical_and(st[0] >= 0, jnp.max(st[1]) > -SB_SKIP)

        def step(st):
            return (st[0] - 1, *block(st[0], st[1:], None))

        _, _, acc, cm = lax.while_loop(live, step, (i * ratio - 1, *carry))
        o_ref[...] = acc
        c_ref[...] = cm

    return pl.pallas_call(
        kern, name="sb_fwd", grid=(h, s // tq),
        in_specs=[pl.BlockSpec((None, tq, d), lambda a, i: (a, i, 0)), pl.BlockSpec((None, s, d), lambda a, i: (a, 0, 0)),
                  pl.BlockSpec((None, s, d), lambda a, i: (a, 0, 0)), pl.BlockSpec((tk, tk), lambda a, i: (0, 0))],
        out_specs=[pl.BlockSpec((None, tq, d), lambda a, i: (a, i, 0)), pl.BlockSpec((None, tq, LANES), lambda a, i: (a, i, 0))],
        out_shape=[_sds((h, s, d), F32), _sds((h, s, LANES), F32)], compiler_params=_params(2),
    )(q, k, v, msuf)


def _sb_bwd(q, k, v, do, cmat, msuf, mpre, tq=SB_TQ, tk=SB_TK):
    h, s, d = q.shape
    tq, tk = min(tq, s), min(tk, s)
    ratio = tq // tk

    def kern(q_ref, k_ref, v_ref, do_ref, c_ref, ms_ref, mp_ref, dq_ref, dk_ref, dv_ref):
        i = pl.program_id(1)

        @pl.when(i == 0)
        def _():
            dk_ref[...] = jnp.zeros_like(dk_ref)
            dv_ref[...] = jnp.zeros_like(dv_ref)

        qv = q_ref[...]
        qs = qv * 0.125
        do_b = do_ref[...].astype(BF16)
        msf = ms_ref[...]
        mpf = mp_ref[...]
        cm = c_ref[...]
        lane = lax.broadcasted_iota(jnp.int32, (tq, LANES), 1)

        def block(kb, carry, dd):
            dq_acc, pc = carry
            ks = pl.ds(pl.multiple_of(kb * tk, tk), tk)
            kk = k_ref[ks, :]
            strict = None if dd is None else _sb_strict(tq, tk, dd)
            z, lom, suf = _sb_scores(qs, kk, msf, strict)
            c = jnp.sum(jnp.where(lane == kb, cm, 0.0), axis=1, keepdims=True)
            a = jnp.exp(z + lom + (suf + c))
            if strict is not None:
                a = jnp.where(strict, a, 0.0)
            g = _dot_nt(do_b, v_ref[ks, :]) * a
            p = pc + _dot(g.astype(BF16), mpf)
            omb = jnp.exp(lom)
            dz = (g * omb - (1.0 - omb) * p) * 0.125
            if strict is not None:
                dz = jnp.where(strict, dz, 0.0)
            dz = dz.astype(BF16)
            dv_ref[ks, :] += _dot_tn(a.astype(BF16), do_b)
            dk_ref[ks, :] += _dot_tn(dz, qv)
            return dq_acc + _dot(dz, kk), pc + jnp.sum(g, axis=1, keepdims=True)

        lane1 = lax.broadcasted_iota(jnp.int32, (1, LANES), 1)
        seen = jnp.logical_and(jnp.max(cm, axis=0, keepdims=True) > -SB_SKIP, lane1 < i * ratio)
        first = i * ratio - jnp.sum(seen.astype(jnp.int32))
        carry = (jnp.zeros((tq, d), F32), jnp.zeros((tq, 1), F32))
        carry = lax.fori_loop(first, i * ratio, lambda kb, c: block(kb, c, None), carry)
        for dd in range(ratio):
            carry = block(i * ratio + dd, carry, dd)
        dq_ref[...] = carry[0]

    qspec = lambda n: pl.BlockSpec((None, tq, n), lambda a, i: (a, i, 0))
    full = pl.BlockSpec((None, s, d), lambda a, i: (a, 0, 0))
    msk = pl.BlockSpec((tk, tk), lambda a, i: (0, 0))
    return pl.pallas_call(
        kern, name="sb_bwd", grid=(h, s // tq),
        in_specs=[qspec(d), full, full, qspec(d), qspec(LANES), msk, msk],
        out_specs=[qspec(d), full, full],
        out_shape=[_sds((h, s, d), F32)] * 3, compiler_params=_params(2),
    )(q, k, v, do, cmat, msuf, mpre)


def _place():
    return lax.axis_index("x"), lax.axis_index("y"), lax.axis_index("c")


def _other_chips(x, y):
    return [(1 - x, y), (x, 1 - y), (1 - x, 1 - y)]


HBM_SPEC = pl.BlockSpec(memory_space=pl.ANY)


def _allgather_shards(wp):
    rows, width = wp.shape
    half = rows // 2

    def body(x_ref, out_ref, send_sems, recv_sems, local_sem):
        x, y, c = _place()
        me, sibling = (x, y, c), (x, y, 1 - c)
        chips = _other_chips(x, y)

        def blk(ref, px, py, pc, per_chip):
            return ref.at[pl.ds(pl.multiple_of((2 * px + py) * per_chip + pc * half, 16), half), :]

        def copy(k, block, to, src=None):
            return pltpu.make_async_remote_copy(
                src_ref=blk(out_ref, *block, rows) if src is None else src, dst_ref=blk(out_ref, *block, rows),
                send_sem=send_sems.at[k], recv_sem=recv_sems.at[k], device_id=to, device_id_type=MESH)

        mine = pltpu.make_async_copy(x_ref, out_ref.at[pl.ds(pl.multiple_of((2 * x + y) * rows, 16), rows), :], local_sem)
        mine.start()
        my_half = blk(x_ref, 0, 0, c, 0)
        first = [copy(j, me, (*chip, c), src=my_half) for j, chip in enumerate(chips)]
        for cp in first:
            cp.start()
        passed = [copy(3 + j, (*chip, c), sibling) for j, chip in enumerate(chips)]
        for j, chip in enumerate(chips):
            copy(j, (*chip, c), me).wait_recv()
            passed[j].start()
        for j, chip in enumerate(chips):
            copy(3 + j, (*chip, 1 - c), me).wait_recv()
        for cp in first + passed:
            cp.wait_send()
        mine.wait()

    return pl.pallas_call(
        body, name="allgather_w", out_shape=_sds((N_SHARD * rows, width), wp.dtype), in_specs=[HBM_SPEC], out_specs=HBM_SPEC,
        scratch_shapes=[pltpu.SemaphoreType.DMA((6,)), pltpu.SemaphoreType.DMA((6,)), pltpu.SemaphoreType.DMA],
    )(wp)


def _sibling_swap(name, v):
    def body(v_ref, got_ref, send_sem, recv_sem):
        x, y, c = _place()
        cp = pltpu.make_async_remote_copy(src_ref=v_ref, dst_ref=got_ref, send_sem=send_sem, recv_sem=recv_sem,
                                          device_id=(x, y, 1 - c), device_id_type=MESH)
        cp.start()
        cp.wait()

    return pl.pallas_call(
        body, name=name, out_shape=_sds(v.shape, v.dtype), in_specs=[HBM_SPEC], out_specs=HBM_SPEC,
        scratch_shapes=[pltpu.SemaphoreType.DMA, pltpu.SemaphoreType.DMA],
    )(v)


def _chip_scatter(p):
    def body(p_ref, out_ref, send_sems, recv_sems, local_sem):
        x, y, c = _place()
        mine = 2 * x + y
        chips = _other_chips(x, y)
        local = pltpu.make_async_copy(p_ref.at[mine], out_ref.at[mine], local_sem)
        local.start()

        def copy(j, px, py):
            return pltpu.make_async_remote_copy(
                src_ref=p_ref.at[2 * px + py], dst_ref=out_ref.at[mine], send_sem=send_sems.at[j], recv_sem=recv_sems.at[j],
                device_id=(px, py, c), device_id_type=MESH)

        sends = [copy(j, px, py) for j, (px, py) in enumerate(chips)]
        for cp in sends:
            cp.start()
        for j, (px, py) in enumerate(chips):
            pltpu.make_async_remote_copy(
                src_ref=p_ref.at[mine], dst_ref=out_ref.at[2 * px + py], send_sem=send_sems.at[j], recv_sem=recv_sems.at[j],
                device_id=(px, py, c), device_id_type=MESH).wait_recv()
        for cp in sends:
            cp.wait_send()
        local.wait()

    return pl.pallas_call(
        body, name="chip_scatter", out_shape=_sds(p.shape, p.dtype), in_specs=[HBM_SPEC], out_specs=HBM_SPEC,
        scratch_shapes=[pltpu.SemaphoreType.DMA((3,)), pltpu.SemaphoreType.DMA((3,)), pltpu.SemaphoreType.DMA],
    )(p)


def _allreduce_small(v):
    m_per, n = v.shape

    def body(x_ref, tot_ref, all_ref, send_sems, recv_sems, local_sem):
        x, y, c = _place()
        me, sibling = (x, y, c), (x, y, 1 - c)
        chips = _other_chips(x, y)

        def rows(px, py, pc):
            return all_ref.at[pl.ds(pl.multiple_of((4 * px + 2 * py + pc) * m_per, 8), m_per), :]

        def copy(k, block, to, src=None):
            return pltpu.make_async_remote_copy(
                src_ref=rows(*block) if src is None else src, dst_ref=rows(*block), send_sem=send_sems.at[k],
                recv_sem=recv_sems.at[k], device_id=to, device_id_type=MESH)

        mine = pltpu.make_async_copy(x_ref, rows(*me), local_sem)
        mine.start()
        first = [copy(0, me, sibling, src=x_ref)] + [copy(1 + j, me, (*chip, c), src=x_ref) for j, chip in enumerate(chips)]
        for cp in first:
            cp.start()
        passed = [copy(4 + j, (*chip, c), sibling) for j, chip in enumerate(chips)]
        for j, chip in enumerate(chips):
            copy(1 + j, (*chip, c), me).wait_recv()
            passed[j].start()
        copy(0, sibling, me).wait_recv()
        for j, chip in enumerate(chips):
            copy(4 + j, (*chip, 1 - c), me).wait_recv()
        for cp in first + passed:
            cp.wait_send()
        mine.wait()
        tot = all_ref[0:m_per, :]
        for dev in range(1, 8):
            tot = tot + all_ref[dev * m_per:(dev + 1) * m_per, :]
        tot_ref[...] = tot

    vmem = pl.BlockSpec(memory_space=pltpu.VMEM)
    return pl.pallas_call(
        body, name="allreduce_small", out_shape=_sds((m_per, n), F32), in_specs=[vmem], out_specs=vmem,
        scratch_shapes=[pltpu.VMEM((8 * m_per, n), F32), pltpu.SemaphoreType.DMA((7,)), pltpu.SemaphoreType.DMA((7,)),
                        pltpu.SemaphoreType.DMA],
    )(v)


def _add_halves(name, terms):
    rows, width = terms[0].shape
    tr = max(t for t in range(8, ADD_ROWS + 1, 8) if rows % t == 0)

    def kern(*refs):
        acc = refs[0][...]
        for r in refs[1:-1]:
            acc = acc + r[...]
        refs[-1][...] = acc

    spec = pl.BlockSpec((tr, width), lambda i: (i, 0))
    return pl.pallas_call(kern, name=name, grid=(rows // tr,), in_specs=[spec] * len(terms), out_specs=spec,
                          out_shape=_sds((rows, width), F32), compiler_params=_params(1))(*terms)


def _adamw(name, w, g, m, v):
    rows, width = w.shape
    tr = rows // 4 if rows % 32 == 0 else rows

    def kern(w_ref, g_ref, m_ref, v_ref, d_ref, mo_ref, vo_ref):
        g_v = g_ref[...]
        m_new = ADAM_B1 * m_ref[...] + (1.0 - ADAM_B1) * g_v
        v_new = ADAM_B2 * v_ref[...] + (1.0 - ADAM_B2) * (g_v * g_v)
        m_hat = m_new / (1.0 - ADAM_B1 ** ADAM_STEP)
        v_hat = v_new / (1.0 - ADAM_B2 ** ADAM_STEP)
        d_ref[...] = -ADAM_LR * (m_hat / (jnp.sqrt(v_hat) + ADAM_EPS) + ADAM_WD * w_ref[...])
        mo_ref[...] = m_new
        vo_ref[...] = v_new

    spec = pl.BlockSpec((tr, width), lambda i: (i, 0))
    return pl.pallas_call(kern, name=name, grid=(rows // tr,), in_specs=[spec] * 4, out_specs=[spec] * 3,
                          out_shape=[_sds((rows, width), F32)] * 3, compiler_params=_params(1))(w, g, m, v)


SHARDED = (("w_in", D_MODEL, IN_WIDTH, 1), ("w_uq", Q_RANK, MLA_HEADS * MLA_QK, 1),
           ("w_ukv", KV_RANK, MLA_HEADS * (MLA_NOPE + MLA_V), 1), ("w_o", D_MODEL, D_MODEL, 0),
           ("w_gate", D_MODEL, D_FF, 1), ("w_up", D_MODEL, D_FF, 1), ("w_down", D_FF, D_MODEL, 0))
SMALL = (("norm_mix", D_MODEL), ("q_latent_norm", Q_RANK), ("kv_latent_norm", KV_RANK), ("out_norm_mla", MLA_WIDTH),
         ("out_norm_sb", SB_WIDTH), ("norm_ffn", D_MODEL), ("norm_final", D_MODEL))


def _pack_local(blocks):
    return jnp.concatenate([b.reshape(-1, PACK_W) for b in blocks], axis=0)


def _unpack_full(packed):
    out, off = {}, 0
    for name, r, c, axis in SHARDED:
        n = r * c // N_SHARD // PACK_W
        piece = packed[:, off:off + n, :]
        off += n
        if axis == 1:
            out[name] = piece.reshape(N_SHARD, r, c // N_SHARD).transpose(1, 0, 2).reshape(r, c)
        else:
            out[name] = piece.reshape(r, c)
    return out


def _pack_grads(grads):
    parts = []
    for name, r, c, axis in SHARDED:
        g = grads[name]
        if axis == 1:
            g = g.reshape(r, N_SHARD, c // N_SHARD).transpose(1, 0, 2)
        parts.append(g.reshape(N_SHARD, -1, PACK_W))
    return jnp.concatenate(parts, axis=1)


def _unpack_shard(packed):
    out, off = {}, 0
    for name, r, c, axis in SHARDED:
        n = r * c // N_SHARD // PACK_W
        shape = (r, c // N_SHARD) if axis == 1 else (r // N_SHARD, c)
        out[name] = packed[off:off + n, :].reshape(shape)
        off += n
    return out


def _rot_cols(w):
    hh = MLA_ROPE // 2
    return jnp.concatenate([-w[..., hh:], w[..., :hh]], axis=-1)


def _rot_cols_t(g):
    hh = MLA_ROPE // 2
    return jnp.concatenate([g[..., hh:], -g[..., :hh]], axis=-1)


def _prepare_weights(full, small):
    w_in = full["w_in"]
    s0, s1, s2 = Q_RANK, Q_RANK + KV_RANK, Q_RANK + KV_RANK + MLA_ROPE
    w_uq3 = full["w_uq"].reshape(Q_RANK, MLA_HEADS, MLA_QK)
    w_uqr = jnp.concatenate([jnp.zeros_like(w_uq3[..., :MLA_NOPE]), _rot_cols(w_uq3[..., MLA_NOPE:])], axis=-1)
    w = {
        "w_cq": w_in[:, :s0], "w_ckv": w_in[:, s0:s1], "w_kr": w_in[:, s1:s2], "w_krr": _rot_cols(w_in[:, s1:s2]),
        "w_sb": w_in[:, s2:], "w_uq": full["w_uq"], "w_uqr": w_uqr.reshape(Q_RANK, MLA_HEADS * MLA_QK),
        "w_ukv": full["w_ukv"], "w_oa": full["w_o"][:MLA_WIDTH], "w_ob": full["w_o"][MLA_WIDTH:],
        "w_gate": full["w_gate"], "w_up": full["w_up"], "w_down": full["w_down"],
    }
    for name in list(w):
        w[name + "_t"] = w[name].T
    w.update(g_mix=small["norm_mix"], g_q=small["q_latent_norm"], g_kv=small["kv_latent_norm"], g_a=small["out_norm_mla"],
             g_b=small["out_norm_sb"], g_f=small["norm_ffn"], g_n=small["norm_final"])
    return w


def _rope_tables(positions):
    inv_freq = ROPE_THETA ** (-jnp.arange(0, MLA_ROPE, 2, dtype=F32) / MLA_ROPE)
    ang = positions.astype(F32)[:, None] * inv_freq[None, :]
    cos, sin = jnp.cos(ang), jnp.sin(ang)
    c32 = jnp.concatenate([cos, cos], axis=1)
    s32 = jnp.concatenate([sin, sin], axis=1)
    s = positions.shape[0]
    c96 = jnp.concatenate([jnp.ones((s, MLA_NOPE), F32), c32], axis=1)
    s96 = jnp.concatenate([jnp.zeros((s, MLA_NOPE), F32), s32], axis=1)
    return {"c32": c32, "s32": s32, "c768": jnp.tile(c96, (1, MLA_HEADS)), "s768": jnp.tile(s96, (1, MLA_HEADS))}


def _heads(t, n_heads):
    s = t.shape[0]
    return t.reshape(s, n_heads, -1).transpose(1, 0, 2)


def _tokens(t):
    h, s, d = t.shape
    return t.transpose(1, 0, 2).reshape(s, h * d)


def kernel(x, positions, norm_mix, w_in, q_latent_norm, w_uq, kv_latent_norm, w_ukv, out_norm_mla, out_norm_sb, w_o, norm_ffn, w_gate, w_up, w_down, norm_final, loss_target, m_norm_mix, m_w_in, m_q_latent_norm, m_w_uq, m_kv_latent_norm, m_w_ukv, m_out_norm_mla, m_out_norm_sb, m_w_o, m_norm_ffn, m_w_gate, m_w_up, m_w_down, m_norm_final, v_norm_mix, v_w_in, v_q_latent_norm, v_w_uq, v_kv_latent_norm, v_w_ukv, v_out_norm_mla, v_out_norm_sb, v_w_o, v_norm_ffn, v_w_gate, v_w_up, v_w_down, v_norm_final):
    given = dict(norm_mix=norm_mix, w_in=w_in, q_latent_norm=q_latent_norm, w_uq=w_uq, kv_latent_norm=kv_latent_norm, w_ukv=w_ukv,
                 out_norm_mla=out_norm_mla, out_norm_sb=out_norm_sb, w_o=w_o, norm_ffn=norm_ffn, w_gate=w_gate, w_up=w_up,
                 w_down=w_down, norm_final=norm_final)
    mom_m = dict(norm_mix=m_norm_mix, w_in=m_w_in, q_latent_norm=m_q_latent_norm, w_uq=m_w_uq, kv_latent_norm=m_kv_latent_norm,
                 w_ukv=m_w_ukv, out_norm_mla=m_out_norm_mla, out_norm_sb=m_out_norm_sb, w_o=m_w_o, norm_ffn=m_norm_ffn,
                 w_gate=m_w_gate, w_up=m_w_up, w_down=m_w_down, norm_final=m_norm_final)
    mom_v = dict(norm_mix=v_norm_mix, w_in=v_w_in, q_latent_norm=v_q_latent_norm, w_uq=v_w_uq, kv_latent_norm=v_kv_latent_norm,
                 w_ukv=v_w_ukv, out_norm_mla=v_out_norm_mla, out_norm_sb=v_out_norm_sb, w_o=v_w_o, norm_ffn=v_norm_ffn,
                 w_gate=v_w_gate, w_up=v_w_up, w_down=v_w_down, norm_final=v_norm_final)
    xs = x[0]
    tgt = loss_target[0]
    s = xs.shape[0]
    c_idx = lax.axis_index("c")

    shard2d = {name: given[name].reshape(given[name].shape[-2:]) for name, *_ in SHARDED}
    packed_w = _pack_local([shard2d[name].astype(BF16) for name, *_ in SHARDED])
    gathered = _allgather_shards(packed_w).reshape(N_SHARD, packed_w.shape[0], PACK_W)
    small = {name: given[name].reshape(1, n) for name, n in SMALL}
    w = _prepare_weights(_unpack_full(gathered), small)
    tabs = _rope_tables(positions[0])
    msuf, mpre = _sb_masks(min(SB_TK, s))

    u, cq, ckv, cqn, ckvn, q768, kv, kr, sb = _fwd_a(xs, tabs, w)
    q_mla = _heads(q768, MLA_HEADS)
    kv_h = _heads(kv, MLA_HEADS)
    k_mla = jnp.concatenate([kv_h[..., :MLA_NOPE], jnp.broadcast_to(kr[None], (MLA_HEADS, s, MLA_ROPE))], axis=-1)
    v_mla = kv_h[..., MLA_NOPE:]
    sb_h = sb.reshape(s, 3, SB_HEADS, SB_DIM).transpose(1, 2, 0, 3)
    o_mla_h, lse = _mla_fwd(q_mla, k_mla, v_mla)
    o_sb_h, cmat = _sb_fwd(sb_h[0], sb_h[1], sb_h[2], msuf)
    o_mla, o_sb = _tokens(o_mla_h), _tokens(o_sb_h)
    merged, h1, f, gate, up, act = _fwd_b1(xs, o_mla, o_sb, w)
    dh2, loss_part, dg_n = _fwd_b2(h1, act, tgt, w)

    dgate, dup = _bwd_b1(dh2, gate, up, w)
    dh1, do_mla, do_sb, dg_f, dg_a, dg_b = _bwd_b2(dgate, dup, h1, dh2, o_mla, o_sb, w)
    dq_mla, dk_mla, dv_mla = _mla_bwd(q_mla, k_mla, v_mla, o_mla_h, _heads(do_mla, MLA_HEADS), lse)
    dq_sb, dk_sb, dv_sb = _sb_bwd(sb_h[0], sb_h[1], sb_h[2], _heads(do_sb, SB_HEADS), cmat, msuf, mpre)
    dkv = _tokens(jnp.concatenate([dk_mla[..., :MLA_NOPE], dv_mla], axis=-1))
    dkr = jnp.sum(dk_mla[..., MLA_NOPE:], axis=0)
    dsb = jnp.stack([dq_sb, dk_sb, dv_sb]).astype(BF16).transpose(2, 0, 1, 3).reshape(s, 3 * SB_WIDTH)
    dx, a1, a2, dcq, dckv, dkrc, dkrs, dg_q, dg_kv, dg_mix = _bwd_a(xs, dh1, cq, ckv, _tokens(dq_mla), dkv, dkr, dsb, tabs, w)

    g_kr = _tn_matmul("dw_kr", u, dkrc) + _rot_cols_t(_tn_matmul("dw_krr", u, dkrs))
    g_uqr = _tn_matmul("dw_uqr", cqn, a2).reshape(Q_RANK, MLA_HEADS, MLA_QK)
    g_uqr = jnp.concatenate([jnp.zeros_like(g_uqr[..., :MLA_NOPE]), _rot_cols_t(g_uqr[..., MLA_NOPE:])], axis=-1)
    grads = {
        "w_in": jnp.concatenate([_tn_matmul("dw_cq", u, dcq), _tn_matmul("dw_ckv", u, dckv), g_kr, _tn_matmul("dw_sb", u, dsb)], axis=1),
        "w_uq": _tn_matmul("dw_uq", cqn, a1) + g_uqr.reshape(Q_RANK, MLA_HEADS * MLA_QK),
        "w_ukv": _tn_matmul("dw_ukv", ckvn, dkv),
        "w_o": _tn_matmul("dw_o", merged, dh1),
        "w_gate": _tn_matmul("dw_gate", f, dgate),
        "w_up": _tn_matmul("dw_up", f, dup),
        "w_down": _tn_matmul("dw_down", act, dh2),
    }

    packed_g = _pack_grads(grads)
    half = packed_g.shape[1] // 2
    keep = lax.dynamic_slice_in_dim(packed_g, c_idx * half, half, axis=1)
    give = lax.dynamic_slice_in_dim(packed_g, (1 - c_idx) * half, half, axis=1)
    got = _sibling_swap("swap_halves", give)
    chip_sum = _add_halves("add_sibling", [keep.reshape(-1, PACK_W), got.reshape(-1, PACK_W)]).reshape(keep.shape)
    slots = _chip_scatter(chip_sum)
    mine = _add_halves("add_chips", [slots[b] for b in range(N_SHARD)])
    theirs = _sibling_swap("swap_result", mine)
    lo = jnp.where(c_idx == 0, mine, theirs)
    hi = jnp.where(c_idx == 0, theirs, mine)
    g_shard = _unpack_shard(jnp.concatenate([lo, hi], axis=0))

    small_parts = jnp.concatenate([dg_mix, dg_q, dg_kv, dg_a, dg_b, dg_f, dg_n], axis=1)
    small_g = _allreduce_small(jnp.broadcast_to(small_parts, (8, small_parts.shape[1])))[0:1]
    loss = lax.psum(loss_part[0, 0], ("x", "y", "c"))

    g_out, d_out, m_out, v_out = {}, {}, {}, {}
    for name, *_ in SHARDED:
        shape = given[name].shape
        d, mn, vn = _adamw("adamw_" + name, shard2d[name], g_shard[name], mom_m[name].reshape(shard2d[name].shape),
                           mom_v[name].reshape(shard2d[name].shape))
        g_out[name], d_out[name], m_out[name], v_out[name] = (t.reshape(shape) for t in (g_shard[name], d, mn, vn))
    cat = lambda src: jnp.concatenate([src[name].reshape(1, n) for name, n in SMALL], axis=1)
    d, mn, vn = _adamw("adamw_small", cat(given), small_g, cat(mom_m), cat(mom_v))
    off = 0
    for name, n in SMALL:
        shape = given[name].shape
        g_out[name], d_out[name], m_out[name], v_out[name] = (t[:, off:off + n].reshape(shape) for t in (small_g, d, mn, vn))
        off += n

    order = ["norm_mix", "w_in", "q_latent_norm", "w_uq", "kv_latent_norm", "w_ukv", "out_norm_mla", "out_norm_sb", "w_o",
             "norm_ffn", "w_gate", "w_up", "w_down", "norm_final"]
    return (loss, dx[None], *[g_out[n] for n in order], *[d_out[n] for n in order], *[m_out[n] for n in order],
            *[v_out[n] for n in order])
```

```python
import functools
import math

import jax
import jax.numpy as jnp
from jax import lax
from jax.experimental import pallas as pl
from jax.experimental.pallas import tpu as pltpu

F32 = jnp.float32
BF16 = jnp.bfloat16
MESH = pl.DeviceIdType.MESH

D_MODEL = 1024
EPS = 1e-6
MLA_HEADS = 8
MLA_NOPE = 64
MLA_ROPE = 32
MLA_V = 64
MLA_QK = MLA_NOPE + MLA_ROPE
Q_RANK = 256
KV_RANK = 128
ROPE_THETA = 10000.0
SB_HEADS = 8
SB_DIM = 64
MLA_WIDTH = MLA_HEADS * MLA_V
SB_WIDTH = SB_HEADS * SB_DIM
D_FF = 2816
IN_WIDTH = Q_RANK + KV_RANK + MLA_ROPE + 3 * SB_WIDTH

ADAM_LR = 0.001
ADAM_B1 = 0.9
ADAM_B2 = 0.999
ADAM_EPS = 1e-08
ADAM_WD = 0.01
ADAM_STEP = 10

N_SHARD = 4
PACK_W = 256
PACK_ROWS = 12288
LANES = 128
ROPE_TILE = LANES
VMEM_LIMIT = 56 * 1024 * 1024
TN_ACC_BYTES = 6 * 1024 * 1024 + 512 * 1024
NEG = -1e30
SB_SKIP = 110.0

ROW_TILE = 256
MLA_TQ = 1024
SB_TQ = 512
MLA_TK = 1024
MLA_BWD_TK = 512
SB_TK = 256
TN_TS = 512
ADD_ROWS = 3072


def _dot(a, b):
    return jnp.dot(a, b, preferred_element_type=F32)


def _dot_nt(a, b):
    return lax.dot_general(a, b, (((1,), (1,)), ((), ())), preferred_element_type=F32)


def _dot_tn(a, b):
    return lax.dot_general(a, b, (((0,), (0,)), ((), ())), preferred_element_type=F32)


def _params(n_grid, vmem=VMEM_LIMIT):
    return pltpu.CompilerParams(dimension_semantics=("arbitrary",) * n_grid, vmem_limit_bytes=vmem)


def _rms(x):
    r = lax.rsqrt(jnp.mean(x * x, axis=-1, keepdims=True) + EPS)
    return x * r, r


def _rms_bwd(n, r, g, dy):
    dn = dy * g
    dx = r * (dn - n * jnp.mean(dn * n, axis=-1, keepdims=True))
    return dx, jnp.sum(dy * n, axis=0, keepdims=True)


def _accumulate(ref, val, step):
    @pl.when(step == 0)
    def _():
        ref[...] = val

    @pl.when(step != 0)
    def _():
        ref[...] += val


def _rowwise(name, body, rows, consts, row_out, acc_out, tm):
    n_rows = rows[0].shape[0]
    tm = min(tm, n_rows)
    nr, nc, no = len(rows), len(consts), len(row_out)

    def kern(*refs):
        body(refs[:nr], refs[nr:nr + nc], refs[nr + nc:nr + nc + no], refs[nr + nc + no:], pl.program_id(0))

    in_specs = [pl.BlockSpec((tm, a.shape[1]), lambda i: (i, 0)) for a in rows]
    in_specs += [pl.BlockSpec(a.shape, lambda i: (0, 0), pipeline_mode=pl.Buffered(1)) for a in consts]
    out_specs = [pl.BlockSpec((tm, s.shape[1]), lambda i: (i, 0)) for s in row_out]
    out_specs += [pl.BlockSpec(s.shape, lambda i: (0, 0)) for s in acc_out]
    return pl.pallas_call(
        kern, name=name, grid=(n_rows // tm,), in_specs=in_specs, out_specs=out_specs,
        out_shape=list(row_out) + list(acc_out), compiler_params=_params(1),
    )(*rows, *consts)


def _sds(shape, dtype):
    return jax.ShapeDtypeStruct(shape, dtype)


def _fwd_a(x, tabs, w):
    s = x.shape[0]

    def body(r, c, o, a, step):
        x_ref, cos_ref, sin_ref = r
        gmix, wcq, wckv, wkr, wkrr, wsq, wsk, wsv, gq, wqn, wqr, wqrr, gkv, wkn, wv = c
        u_o, cq_o, ckv_o, cqn_o, ckvn_o, qn_o, qr_o, kn_o, v_o, kr_o, sq_o, sk_o, sv_o = o
        cos, sin = cos_ref[...], sin_ref[...]
        n, _ = _rms(x_ref[...])
        u = (n * gmix[...]).astype(BF16)
        u_o[...] = u
        cq = _dot(u, wcq[...])
        ckv = _dot(u, wckv[...])
        kr_o[...] = (_dot(u, wkr[...]) * cos[:, :ROPE_TILE] + _dot(u, wkrr[...]) * sin[:, :ROPE_TILE]).astype(BF16)
        sq_o[...] = _dot(u, wsq[...]).astype(BF16)
        sk_o[...] = _dot(u, wsk[...]).astype(BF16)
        sv_o[...] = _dot(u, wsv[...]).astype(BF16)
        cq_o[...] = cq
        ckv_o[...] = ckv
        nq, _ = _rms(cq)
        cqn = (nq * gq[...]).astype(BF16)
        cqn_o[...] = cqn
        qn_o[...] = _dot(cqn, wqn[...]).astype(BF16)
        qr_o[...] = (_dot(cqn, wqr[...]) * cos + _dot(cqn, wqrr[...]) * sin).astype(BF16)
        nkv, _ = _rms(ckv)
        ckvn = (nkv * gkv[...]).astype(BF16)
        ckvn_o[...] = ckvn
        kn_o[...] = _dot(ckvn, wkn[...]).astype(BF16)
        v_o[...] = _dot(ckvn, wv[...]).astype(BF16)

    outs = [
        _sds((s, D_MODEL), BF16), _sds((s, Q_RANK), F32), _sds((s, KV_RANK), F32), _sds((s, Q_RANK), BF16),
        _sds((s, KV_RANK), BF16), _sds((s, MLA_HEADS * MLA_NOPE), BF16), _sds((s, MLA_HEADS * MLA_ROPE), BF16),
        _sds((s, MLA_HEADS * MLA_NOPE), BF16), _sds((s, MLA_WIDTH), BF16), _sds((s, ROPE_TILE), BF16),
        _sds((s, SB_WIDTH), BF16), _sds((s, SB_WIDTH), BF16), _sds((s, SB_WIDTH), BF16),
    ]
    consts = [w["g_mix"], w["w_cq"], w["w_ckv"], w["w_kr4"], w["w_kr4r"], w["w_sbq"], w["w_sbk"], w["w_sbv"], w["g_q"],
              w["w_qn"], w["w_qr"], w["w_qrr"], w["g_kv"], w["w_kn"], w["w_v"]]
    return _rowwise("fwd_a", body, [x, tabs["cos"], tabs["sin"]], consts, outs, [], ROW_TILE)


def _fwd_b1(x, o_mla, o_sb, w):
    s = x.shape[0]

    def body(r, c, o, a, step):
        x_ref, oa_ref, ob_ref = r
        ga, gb, woa, wob, gf, wg, wu = c
        mg_o, h1_o, f_o, gate_o, up_o, act_o = o
        na, _ = _rms(oa_ref[...])
        nb, _ = _rms(ob_ref[...])
        ma = (na * ga[...]).astype(BF16)
        mb = (nb * gb[...]).astype(BF16)
        mg_o[:, :MLA_WIDTH] = ma
        mg_o[:, MLA_WIDTH:] = mb
        h1 = x_ref[...] + _dot(ma, woa[...]) + _dot(mb, wob[...])
        h1_o[...] = h1
        nf, _ = _rms(h1)
        f = (nf * gf[...]).astype(BF16)
        f_o[...] = f
        gate = _dot(f, wg[...])
        up = _dot(f, wu[...])
        gate_o[...] = gate.astype(BF16)
        up_o[...] = up.astype(BF16)
        act_o[...] = (gate * (1.0 / (1.0 + jnp.exp(-gate))) * up).astype(BF16)

    outs = [_sds((s, D_MODEL), BF16), _sds((s, D_MODEL), F32), _sds((s, D_MODEL), BF16), _sds((s, D_FF), BF16),
            _sds((s, D_FF), BF16), _sds((s, D_FF), BF16)]
    consts = [w["g_a"], w["g_b"], w["w_oa"], w["w_ob"], w["g_f"], w["w_gate"], w["w_up"]]
    return _rowwise("fwd_b1", body, [x, o_mla, o_sb], consts, outs, [], ROW_TILE)


def _fwd_b2(h1, act, tgt, w):
    s = h1.shape[0]

    def body(r, c, o, a, step):
        h1_ref, act_ref, t_ref = r
        wd, gn = c
        (dh2_o,) = o
        loss_o, dgn_o = a
        h2 = h1_ref[...] + _dot(act_ref[...], wd[...])
        n2, r2 = _rms(h2)
        err = n2 * gn[...] - t_ref[...]
        part = jnp.sum(jnp.sum(err * err, axis=1, keepdims=True), axis=0, keepdims=True) * (0.5 / D_MODEL)
        _accumulate(loss_o, jnp.broadcast_to(part, (1, LANES)), step)
        dh2, dgn = _rms_bwd(n2, r2, gn[...], err * (1.0 / D_MODEL))
        dh2_o[...] = dh2
        _accumulate(dgn_o, dgn, step)

    return _rowwise("fwd_b2", body, [h1, act, tgt], [w["w_down"], w["g_n"]], [_sds((s, D_MODEL), F32)],
                    [_sds((1, LANES), F32), _sds((1, D_MODEL), F32)], ROW_TILE)


def _bwd_b1(dh2, gate, up, w):
    s = dh2.shape[0]

    def body(r, c, o, a, step):
        dh2_ref, gate_ref, up_ref = r
        (wdt,) = c
        dgate_o, dup_o = o
        dact = _dot(dh2_ref[...].astype(BF16), wdt[...])
        gate = gate_ref[...].astype(F32)
        sig = 1.0 / (1.0 + jnp.exp(-gate))
        dup_o[...] = (dact * (gate * sig)).astype(BF16)
        dgate_o[...] = (dact * up_ref[...].astype(F32) * (sig * (1.0 + gate * (1.0 - sig)))).astype(BF16)

    return _rowwise("bwd_b1", body, [dh2, gate, up], [w["w_down_t"]], [_sds((s, D_FF), BF16), _sds((s, D_FF), BF16)],
                    [], ROW_TILE)


def _bwd_b2(dgate, dup, h1, dh2, o_mla, o_sb, w):
    s = h1.shape[0]

    def body(r, c, o, a, step):
        dgate_ref, dup_ref, h1_ref, dh2_ref, oa_ref, ob_ref = r
        wgt, wut, gf, woat, wobt, ga, gb = c
        dh1_o, doa_o, dob_o = o
        dgf_o, dga_o, dgb_o = a
        df = _dot(dgate_ref[...], wgt[...]) + _dot(dup_ref[...], wut[...])
        nf, rf = _rms(h1_ref[...])
        dres, dgf = _rms_bwd(nf, rf, gf[...], df)
        dh1 = dh2_ref[...] + dres
        dh1_o[...] = dh1
        dh1b = dh1.astype(BF16)
        na, ra = _rms(oa_ref[...])
        doa, dga = _rms_bwd(na, ra, ga[...], _dot(dh1b, woat[...]))
        nb, rb = _rms(ob_ref[...])
        dob, dgb = _rms_bwd(nb, rb, gb[...], _dot(dh1b, wobt[...]))
        doa_o[...] = doa
        dob_o[...] = dob
        _accumulate(dgf_o, dgf, step)
        _accumulate(dga_o, dga, step)
        _accumulate(dgb_o, dgb, step)

    consts = [w["w_gate_t"], w["w_up_t"], w["g_f"], w["w_oa_t"], w["w_ob_t"], w["g_a"], w["g_b"]]
    outs = [_sds((s, D_MODEL), F32), _sds((s, MLA_WIDTH), F32), _sds((s, SB_WIDTH), F32)]
    accs = [_sds((1, D_MODEL), F32), _sds((1, MLA_WIDTH), F32), _sds((1, SB_WIDTH), F32)]
    return _rowwise("bwd_b2", body, [dgate, dup, h1, dh2, o_mla, o_sb], consts, outs, accs, ROW_TILE)


def _fold_pairs(t):
    return jnp.concatenate([t[:, :LANES] + t[:, LANES:2 * LANES], t[:, 2 * LANES:3 * LANES] + t[:, 3 * LANES:]], axis=1)


def _bwd_a(x, dh1, cq, ckv, dqn, dqr, dkn, dvm, dkr, dsq, dsk, dsv, tabs, w):
    s = x.shape[0]

    def body(r, c, o, a, step):
        x_ref, dh1_ref, cq_ref, ckv_ref, dqn_ref, dqr_ref, dkn_ref, dvm_ref, dkr_ref, dsq_ref, dsk_ref, dsv_ref, cos_ref, sin_ref = r
        wqnt, wqrt, wqrrt, gq, wknt, wvt, gkv, wcqt, wckvt, wkrt, wkrrt, wsqt, wskt, wsvt, gmix = c
        dx_o, a1_o, a2_o, dcq_o, dckv_o, dkrc_o, dkrs_o = o
        dgq_o, dgkv_o, dgmix_o = a
        cos, sin = cos_ref[...], sin_ref[...]
        dqr = _fold_pairs(dqr_ref[...])
        a1 = (dqr * cos).astype(BF16)
        a2 = (dqr * sin).astype(BF16)
        a1_o[...] = a1
        a2_o[...] = a2
        nq, rq = _rms(cq_ref[...])
        dcqn = _dot(dqn_ref[...].astype(BF16), wqnt[...]) + _dot(a1, wqrt[...]) + _dot(a2, wqrrt[...])
        dcq, dgq = _rms_bwd(nq, rq, gq[...], dcqn)
        nkv, rkv = _rms(ckv_ref[...])
        dckvn = _dot(dkn_ref[...].astype(BF16), wknt[...]) + _dot(dvm_ref[...].astype(BF16), wvt[...])
        dckv, dgkv = _rms_bwd(nkv, rkv, gkv[...], dckvn)
        dkr = _fold_pairs(dkr_ref[...])
        dcq_b = dcq.astype(BF16)
        dckv_b = dckv.astype(BF16)
        dkrc = (dkr * cos).astype(BF16)
        dkrs = (dkr * sin).astype(BF16)
        dcq_o[...] = dcq_b
        dckv_o[...] = dckv_b
        dkrc_o[...] = dkrc
        dkrs_o[...] = dkrs
        du = (_dot(dcq_b, wcqt[...]) + _dot(dckv_b, wckvt[...]) + _dot(dkrc, wkrt[...]) + _dot(dkrs, wkrrt[...])
              + _dot(dsq_ref[...].astype(BF16), wsqt[...]) + _dot(dsk_ref[...].astype(BF16), wskt[...])
              + _dot(dsv_ref[...].astype(BF16), wsvt[...]))
        nx, rx = _rms(x_ref[...])
        dres, dgmix = _rms_bwd(nx, rx, gmix[...], du)
        dx_o[...] = dh1_ref[...] + dres
        _accumulate(dgq_o, dgq, step)
        _accumulate(dgkv_o, dgkv, step)
        _accumulate(dgmix_o, dgmix, step)

    consts = [w["w_qn_t"], w["w_qr_t"], w["w_qrr_t"], w["g_q"], w["w_kn_t"], w["w_v_t"], w["g_kv"], w["w_cq_t"], w["w_ckv_t"],
              w["w_kr8_t"], w["w_kr8r_t"], w["w_sbq_t"], w["w_sbk_t"], w["w_sbv_t"], w["g_mix"]]
    rope_w = MLA_HEADS * MLA_ROPE
    outs = [_sds((s, D_MODEL), F32), _sds((s, rope_w), BF16), _sds((s, rope_w), BF16), _sds((s, Q_RANK), BF16),
            _sds((s, KV_RANK), BF16), _sds((s, rope_w), BF16), _sds((s, rope_w), BF16)]
    accs = [_sds((1, Q_RANK), F32), _sds((1, KV_RANK), F32), _sds((1, D_MODEL), F32)]
    rows = [x, dh1, cq, ckv, dqn, dqr, dkn, dvm, dkr, dsq, dsk, dsv, tabs["cos"], tabs["sin"]]
    return _rowwise("bwd_a", body, rows, consts, outs, accs, ROW_TILE)


def _tn_multi(name, x, ys):
    s, k = x.shape
    ts = min(TN_TS, s)
    n_y = len(ys)

    def kern(*refs):
        step = pl.program_id(0)
        xb = refs[0][...].astype(BF16)
        for j in range(n_y):
            _accumulate(refs[1 + n_y + j], _dot_tn(xb, refs[1 + j][...].astype(BF16)), step)

    return pl.pallas_call(
        kern, name=name, grid=(s // ts,),
        in_specs=[pl.BlockSpec((ts, k), lambda i: (i, 0))] + [pl.BlockSpec((ts, y.shape[1]), lambda i: (i, 0)) for y in ys],
        out_specs=[pl.BlockSpec((k, y.shape[1]), lambda i: (0, 0)) for y in ys],
        out_shape=[_sds((k, y.shape[1]), F32) for y in ys], compiler_params=_params(1),
    )(x, *ys)


def _tn_tile(k, n):
    if n % LANES or k * n * 4 <= TN_ACC_BYTES:
        return n
    units = n // LANES
    best = 1
    for d in range(1, units + 1):
        if units % d == 0 and k * d * LANES * 4 <= TN_ACC_BYTES:
            best = d
    return best * LANES


def _tn_matmul(name, x, y):
    s, k = x.shape
    n = y.shape[1]
    ts = min(TN_TS, s)
    tn = _tn_tile(k, n)

    def kern(x_ref, y_ref, o_ref):
        step = pl.program_id(1)
        _accumulate(o_ref, _dot_tn(x_ref[...].astype(BF16), y_ref[...].astype(BF16)), step)

    return pl.pallas_call(
        kern, name=name, grid=(n // tn, s // ts),
        in_specs=[pl.BlockSpec((ts, k), lambda j, i: (i, 0)), pl.BlockSpec((ts, tn), lambda j, i: (i, j))],
        out_specs=pl.BlockSpec((k, tn), lambda j, i: (0, j)), out_shape=_sds((k, n), F32), compiler_params=_params(2),
    )(x, y)


def _causal(tq, tk, d):
    row = lax.broadcasted_iota(jnp.int32, (tq, tk), 0)
    col = lax.broadcasted_iota(jnp.int32, (tq, tk), 1)
    return col + d * tk <= row


def _lanes(rows, lo, width):
    lane = lax.broadcasted_iota(jnp.int32, (rows, LANES), 1)
    return jnp.logical_and(lane >= lo, lane < lo + width)


def _keep(mask, t):
    return jnp.where(mask, t, jnp.zeros_like(t))


def _mla_qcat(qn_ref, qr_ref, rope_lo, half, rows):
    qn = _keep(_lanes(rows, MLA_NOPE * half, MLA_NOPE), qn_ref[...])
    qr = _keep(_lanes(rows, rope_lo, MLA_ROPE), qr_ref[...])
    return jnp.concatenate([qn, qr], axis=1)


def _mla_fwd(qn, qr, kn, kr, v, tq=MLA_TQ, tk=MLA_TK):
    s = qn.shape[0]
    tq, tk = min(tq, s), min(tk, s)
    ratio = tq // tk
    scale = 1.0 / math.sqrt(MLA_QK)

    def kern(qn_ref, qr_ref, kn_ref, kr_ref, v_ref, o_ref, lse_ref):
        g = pl.program_id(0)
        i = pl.program_id(1)
        for half in range(2):
            qcat = _mla_qcat(qn_ref, qr_ref, MLA_ROPE * (2 * (g % 2) + half), half, tq)

            def block(kb, carry, dd, qcat=qcat):
                m, l, acc = carry
                ks = pl.ds(pl.multiple_of(kb * tk, tk), tk)
                kcat = jnp.concatenate([kn_ref[ks, :], kr_ref[ks, :]], axis=1)
                sc = _dot_nt(qcat, kcat) * scale
                if dd is not None:
                    sc = jnp.where(_causal(tq, tk, dd), sc, NEG)
                m_new = jnp.maximum(m, jnp.max(sc, axis=1, keepdims=True))
                p = jnp.exp(sc - m_new)
                alpha = jnp.exp(m - m_new)
                l = alpha * l + jnp.sum(p, axis=1, keepdims=True)
                acc = alpha * acc + _dot(p.astype(BF16), v_ref[ks, :])
                return m_new, l, acc

            carry = (jnp.full((tq, 1), NEG, F32), jnp.zeros((tq, 1), F32), jnp.zeros((tq, LANES), F32))
            carry = lax.fori_loop(0, i * ratio, lambda kb, c, block=block: block(kb, c, None), carry)
            for dd in range(ratio):
                carry = block(i * ratio + dd, carry, dd)
            m, l, acc = carry
            out = _keep(_lanes(tq, MLA_V * half, MLA_V), acc / l)
            lse = _keep(_lanes(tq, MLA_ROPE * half, MLA_ROPE), jnp.broadcast_to(m + jnp.log(l), (tq, LANES)))
            if half == 0:
                o_ref[...] = out
                lse_ref[...] = lse
            else:
                o_ref[...] += out
                lse_ref[...] += lse

    qblk = pl.BlockSpec((tq, LANES), lambda g, i: (i, g))
    full = pl.BlockSpec((s, LANES), lambda g, i: (0, g))
    return pl.pallas_call(
        kern, name="mla_fwd", grid=(MLA_HEADS // 2, s // tq),
        in_specs=[qblk, pl.BlockSpec((tq, LANES), lambda g, i: (i, g // 2)), full, pl.BlockSpec((s, LANES), lambda g, i: (0, 0)), full],
        out_specs=[qblk, qblk],
        out_shape=[_sds((s, MLA_WIDTH), F32), _sds((s, MLA_HEADS // 2 * LANES), F32)], compiler_params=_params(2),
    )(qn, qr, kn, kr, v)


def _mla_bwd(qn, qr, kn, kr, v, o, do, lse, tq=MLA_TQ, tk=MLA_BWD_TK):
    s = qn.shape[0]
    tq, tk = min(tq, s), min(tk, s)
    ratio = tq // tk
    scale = 1.0 / math.sqrt(MLA_QK)

    def kern(qn_ref, qr_ref, kn_ref, kr_ref, v_ref, o_ref, do_ref, lse_ref, dqn_ref, dqr_ref, dkn_ref, dkr_ref, dv_ref):
        g = pl.program_id(0)
        i = pl.program_id(1)

        @pl.when(i == 0)
        def _():
            dkn_ref[...] = jnp.zeros_like(dkn_ref)
            dkr_ref[...] = jnp.zeros_like(dkr_ref)
            dv_ref[...] = jnp.zeros_like(dv_ref)

        for half in range(2):
            rope_lo = MLA_ROPE * (2 * (g % 2) + half)
            qcat = _mla_qcat(qn_ref, qr_ref, rope_lo, half, tq)
            mine = _lanes(tq, MLA_V * half, MLA_V)
            do_f = _keep(mine, do_ref[...])
            do_b = do_f.astype(BF16)
            delta = jnp.sum(do_f * o_ref[...], axis=1, keepdims=True)
            lse_v = lse_ref[:, MLA_ROPE * half:MLA_ROPE * half + 1]

            def block(kb, dq_acc, dd, qcat=qcat, do_b=do_b, delta=delta, lse_v=lse_v):
                ks = pl.ds(pl.multiple_of(kb * tk, tk), tk)
                kcat = jnp.concatenate([kn_ref[ks, :], kr_ref[ks, :]], axis=1)
                p = jnp.exp(_dot_nt(qcat, kcat) * scale - lse_v)
                if dd is not None:
                    p = jnp.where(_causal(tq, tk, dd), p, 0.0)
                ds = (p * (_dot_nt(do_b, v_ref[ks, :]) - delta) * scale).astype(BF16)
                dv_ref[ks, :] += _dot_tn(p.astype(BF16), do_b)
                dkc = _dot_tn(ds, qcat)
                dkn_ref[ks, :] += dkc[:, :LANES]
                dkr_ref[ks, :] += dkc[:, LANES:]
                return dq_acc + _dot(ds, kcat)

            acc = lax.fori_loop(0, i * ratio, lambda kb, c, block=block: block(kb, c, None), jnp.zeros((tq, 2 * LANES), F32))
            for dd in range(ratio):
                acc = block(i * ratio + dd, acc, dd)
            dqn = _keep(_lanes(tq, MLA_NOPE * half, MLA_NOPE), acc[:, :LANES])
            dqr = _keep(_lanes(tq, rope_lo, MLA_ROPE), acc[:, LANES:])
            if half == 0:
                dqn_ref[...] = dqn
                dqr_ref[...] = dqr
            else:
                dqn_ref[...] += dqn
                dqr_ref[...] += dqr

    qblk = pl.BlockSpec((tq, LANES), lambda g, i: (i, g))
    full = pl.BlockSpec((s, LANES), lambda g, i: (0, g))
    once = lambda spec_map: pl.BlockSpec((s, LANES), spec_map, pipeline_mode=pl.Buffered(1))
    wide = _sds((s, MLA_HEADS // 2 * LANES), F32)
    return pl.pallas_call(
        kern, name="mla_bwd", grid=(MLA_HEADS // 2, s // tq),
        in_specs=[qblk, pl.BlockSpec((tq, LANES), lambda g, i: (i, g // 2)), once(lambda g, i: (0, g)), once(lambda g, i: (0, 0)),
                  once(lambda g, i: (0, g)), qblk, qblk, qblk],
        out_specs=[qblk, qblk, full, full, full],
        out_shape=[wide, wide, wide, wide, wide], compiler_params=_params(2),
    )(qn, qr, kn, kr, v, o, do, lse)


def _sb_masks(tk):
    j = lax.broadcasted_iota(jnp.int32, (tk, tk), 0)
    c = lax.broadcasted_iota(jnp.int32, (tk, tk), 1)
    return (j > c).astype(BF16), (j < c).astype(BF16)


def _sb_scores(qs, kk, msuf, strict):
    z = _dot_nt(qs, kk)
    lom = -(jnp.maximum(z, 0.0) + jnp.log(1.0 + jnp.exp(-jnp.abs(z))))
    if strict is not None:
        lom = jnp.where(strict, lom, 0.0)
    hi = lom.astype(BF16)
    lo = (lom - hi.astype(F32)).astype(BF16)
    return z, lom, _dot(hi, msuf) + _dot(lo, msuf)


def _sb_strict(tq, tk, d):
    row = lax.broadcasted_iota(jnp.int32, (tq, tk), 0)
    col = lax.broadcasted_iota(jnp.int32, (tq, tk), 1)
    return col + d * tk < row


def _sb_fwd(q, k, v, msuf, tq=SB_TQ, tk=SB_TK):
    s = q.shape[0]
    tq, tk = min(tq, s), min(tk, s)
    ratio = tq // tk

    def kern(q_ref, k_ref, v_ref, m_ref, o_ref, c_ref):
        i = pl.program_id(1)
        msf = m_ref[...]
        lane = lax.broadcasted_iota(jnp.int32, (tq, LANES), 1)
        for half in range(2):
            mine = _lanes(tq, SB_DIM * half, SB_DIM)
            qs = _keep(mine, q_ref[...]) * 0.125

            def block(kb, carry, dd, qs=qs):
                c, acc, cm = carry
                ks = pl.ds(pl.multiple_of(kb * tk, tk), tk)
                strict = None if dd is None else _sb_strict(tq, tk, dd)
                z, lom, suf = _sb_scores(qs, k_ref[ks, :], msf, strict)
                a = jnp.exp(z + lom + (suf + c))
                if strict is not None:
                    a = jnp.where(strict, a, 0.0)
                acc = acc + _dot(a.astype(BF16), v_ref[ks, :])
                cm = jnp.where(lane == kb, c, cm)
                return c + jnp.sum(lom, axis=1, keepdims=True), acc, cm

            carry = (jnp.zeros((tq, 1), F32), jnp.zeros((tq, LANES), F32), jnp.full((tq, LANES), NEG, F32))
            for dd in range(ratio - 1, -1, -1):
                carry = block(i * ratio + dd, carry, dd)

            def live(st):
                return jnp.logical_and(st[0] >= 0, jnp.max(st[1]) > -SB_SKIP)

            def step(st, block=block):
                return (st[0] - 1, *block(st[0], st[1:], None))

            _, _, acc, cm = lax.while_loop(live, step, (i * ratio - 1, *carry))
            if half == 0:
                o_ref[...] = _keep(mine, acc)
            else:
                o_ref[...] += _keep(mine, acc)
            c_ref[:, LANES * half:LANES * (half + 1)] = cm

    qblk = lambda n: pl.BlockSpec((tq, n), lambda g, i: (i, g))
    full = pl.BlockSpec((s, LANES), lambda g, i: (0, g))
    return pl.pallas_call(
        kern, name="sb_fwd", grid=(SB_HEADS // 2, s // tq),
        in_specs=[qblk(LANES), full, full, pl.BlockSpec((tk, tk), lambda g, i: (0, 0))],
        out_specs=[qblk(LANES), qblk(2 * LANES)],
        out_shape=[_sds((s, SB_WIDTH), F32), _sds((s, SB_HEADS * LANES), F32)], compiler_params=_params(2),
    )(q, k, v, msuf)


def _sb_bwd(q, k, v, do, cmat, msuf, mpre, tq=SB_TQ, tk=SB_TK):
    s = q.shape[0]
    tq, tk = min(tq, s), min(tk, s)
    ratio = tq // tk

    def kern(q_ref, k_ref, v_ref, do_ref, c_ref, ms_ref, mp_ref, dq_ref, dk_ref, dv_ref):
        i = pl.program_id(1)

        @pl.when(i == 0)
        def _():
            dk_ref[...] = jnp.zeros_like(dk_ref)
            dv_ref[...] = jnp.zeros_like(dv_ref)

        msf = ms_ref[...]
        mpf = mp_ref[...]
        lane = lax.broadcasted_iota(jnp.int32, (tq, LANES), 1)
        lane1 = lax.broadcasted_iota(jnp.int32, (1, LANES), 1)
        for half in range(2):
            mine = _lanes(tq, SB_DIM * half, SB_DIM)
            qv = _keep(mine, q_ref[...])
            qs = qv * 0.125
            do_b = _keep(mine, do_ref[...]).astype(BF16)
            cm = c_ref[:, LANES * half:LANES * (half + 1)]

            def block(kb, carry, dd, qv=qv, qs=qs, do_b=do_b, cm=cm):
                dq_acc, pc = carry
                ks = pl.ds(pl.multiple_of(kb * tk, tk), tk)
                kk = k_ref[ks, :]
                strict = None if dd is None else _sb_strict(tq, tk, dd)
                z, lom, suf = _sb_scores(qs, kk, msf, strict)
                c = jnp.sum(jnp.where(lane == kb, cm, 0.0), axis=1, keepdims=True)
                a = jnp.exp(z + lom + (suf + c))
                if strict is not None:
                    a = jnp.where(strict, a, 0.0)
                g = _dot_nt(do_b, v_ref[ks, :]) * a
                p = pc + _dot(g.astype(BF16), mpf)
                omb = jnp.exp(lom)
                dz = (g * omb - (1.0 - omb) * p) * 0.125
                if strict is not None:
                    dz = jnp.where(strict, dz, 0.0)
                dz = dz.astype(BF16)
                dv_ref[ks, :] += _dot_tn(a.astype(BF16), do_b)
                dk_ref[ks, :] += _dot_tn(dz, qv)
                return dq_acc + _dot(dz, kk), pc + jnp.sum(g, axis=1, keepdims=True)

            seen = jnp.logical_and(jnp.max(cm, axis=0, keepdims=True) > -SB_SKIP, lane1 < i * ratio)
            first = i * ratio - jnp.sum(seen.astype(jnp.int32))
            carry = (jnp.zeros((tq, LANES), F32), jnp.zeros((tq, 1), F32))
            carry = lax.fori_loop(first, i * ratio, lambda kb, c, block=block: block(kb, c, None), carry)
            for dd in range(ratio):
                carry = block(i * ratio + dd, carry, dd)
            if half == 0:
                dq_ref[...] = _keep(mine, carry[0])
            else:
                dq_ref[...] += _keep(mine, carry[0])

    qblk = lambda n: pl.BlockSpec((tq, n), lambda g, i: (i, g))
    full = pl.BlockSpec((s, LANES), lambda g, i: (0, g))
    msk = pl.BlockSpec((tk, tk), lambda g, i: (0, 0))
    return pl.pallas_call(
        kern, name="sb_bwd", grid=(SB_HEADS // 2, s // tq),
        in_specs=[qblk(LANES), full, full, qblk(LANES), qblk(2 * LANES), msk, msk],
        out_specs=[qblk(LANES), full, full],
        out_shape=[_sds((s, SB_WIDTH), F32)] * 3, compiler_params=_params(2),
    )(q, k, v, do, cmat, msuf, mpre)


def _place():
    return lax.axis_index("x"), lax.axis_index("y"), lax.axis_index("c")


def _other_chips(x, y):
    return [(1 - x, y), (x, 1 - y), (1 - x, 1 - y)]


HBM_SPEC = pl.BlockSpec(memory_space=pl.ANY)


def _allgather_shards(wp):
    rows, width = wp.shape
    half = rows // 2

    def body(x_ref, out_ref, send_sems, recv_sems, local_sem):
        x, y, c = _place()
        me, sibling = (x, y, c), (x, y, 1 - c)
        chips = _other_chips(x, y)

        def blk(ref, px, py, pc, per_chip):
            return ref.at[pl.ds(pl.multiple_of((2 * px + py) * per_chip + pc * half, 16), half), :]

        def copy(k, block, to, src=None):
            return pltpu.make_async_remote_copy(
                src_ref=blk(out_ref, *block, rows) if src is None else src, dst_ref=blk(out_ref, *block, rows),
                send_sem=send_sems.at[k], recv_sem=recv_sems.at[k], device_id=to, device_id_type=MESH)

        mine = pltpu.make_async_copy(x_ref, out_ref.at[pl.ds(pl.multiple_of((2 * x + y) * rows, 16), rows), :], local_sem)
        mine.start()
        my_half = blk(x_ref, 0, 0, c, 0)
        first = [copy(j, me, (*chip, c), src=my_half) for j, chip in enumerate(chips)]
        for cp in first:
            cp.start()
        passed = [copy(3 + j, (*chip, c), sibling) for j, chip in enumerate(chips)]
        for j, chip in enumerate(chips):
            copy(j, (*chip, c), me).wait_recv()
            passed[j].start()
        for j, chip in enumerate(chips):
            copy(3 + j, (*chip, 1 - c), me).wait_recv()
        for cp in first + passed:
            cp.wait_send()
        mine.wait()

    return pl.pallas_call(
        body, name="allgather_w", out_shape=_sds((N_SHARD * rows, width), wp.dtype), in_specs=[HBM_SPEC], out_specs=HBM_SPEC,
        scratch_shapes=[pltpu.SemaphoreType.DMA((6,)), pltpu.SemaphoreType.DMA((6,)), pltpu.SemaphoreType.DMA],
    )(wp)


def _sibling_swap(name, v):
    def body(v_ref, got_ref, send_sem, recv_sem):
        x, y, c = _place()
        cp = pltpu.make_async_remote_copy(src_ref=v_ref, dst_ref=got_ref, send_sem=send_sem, recv_sem=recv_sem,
                                          device_id=(x, y, 1 - c), device_id_type=MESH)
        cp.start()
        cp.wait()

    return pl.pallas_call(
        body, name=name, out_shape=_sds(v.shape, v.dtype), in_specs=[HBM_SPEC], out_specs=HBM_SPEC,
        scratch_shapes=[pltpu.SemaphoreType.DMA, pltpu.SemaphoreType.DMA],
    )(v)


def _chip_scatter(p):
    def body(p_ref, out_ref, send_sems, recv_sems):
        x, y, c = _place()
        chips = _other_chips(x, y)

        def copy(j, px, py):
            return pltpu.make_async_remote_copy(
                src_ref=p_ref.at[2 * px + py], dst_ref=out_ref.at[j], send_sem=send_sems.at[j], recv_sem=recv_sems.at[j],
                device_id=(px, py, c), device_id_type=MESH)

        sends = [copy(j, px, py) for j, (px, py) in enumerate(chips)]
        for cp in sends:
            cp.start()
        for cp in sends:
            cp.wait()

    return pl.pallas_call(
        body, name="chip_scatter", out_shape=_sds((N_SHARD - 1,) + p.shape[1:], p.dtype), in_specs=[HBM_SPEC], out_specs=HBM_SPEC,
        scratch_shapes=[pltpu.SemaphoreType.DMA((3,)), pltpu.SemaphoreType.DMA((3,))],
    )(p)


def _allreduce_small(v):
    m_per, n = v.shape

    def body(x_ref, tot_ref, all_ref, send_sems, recv_sems, local_sem):
        x, y, c = _place()
        me, sibling = (x, y, c), (x, y, 1 - c)
        chips = _other_chips(x, y)

        def rows(px, py, pc):
            return all_ref.at[pl.ds(pl.multiple_of((4 * px + 2 * py + pc) * m_per, 8), m_per), :]

        def copy(k, block, to, src=None):
            return pltpu.make_async_remote_copy(
                src_ref=rows(*block) if src is None else src, dst_ref=rows(*block), send_sem=send_sems.at[k],
                recv_sem=recv_sems.at[k], device_id=to, device_id_type=MESH)

        mine = pltpu.make_async_copy(x_ref, rows(*me), local_sem)
        mine.start()
        first = [copy(0, me, sibling, src=x_ref)] + [copy(1 + j, me, (*chip, c), src=x_ref) for j, chip in enumerate(chips)]
        for cp in first:
            cp.start()
        passed = [copy(4 + j, (*chip, c), sibling) for j, chip in enumerate(chips)]
        for j, chip in enumerate(chips):
            copy(1 + j, (*chip, c), me).wait_recv()
            passed[j].start()
        copy(0, sibling, me).wait_recv()
        for j, chip in enumerate(chips):
            copy(4 + j, (*chip, 1 - c), me).wait_recv()
        for cp in first + passed:
            cp.wait_send()
        mine.wait()
        tot = all_ref[0:m_per, :]
        for dev in range(1, 8):
            tot = tot + all_ref[dev * m_per:(dev + 1) * m_per, :]
        tot_ref[...] = tot

    vmem = pl.BlockSpec(memory_space=pltpu.VMEM)
    return pl.pallas_call(
        body, name="allreduce_small", out_shape=_sds((m_per, n), F32), in_specs=[vmem], out_specs=vmem,
        scratch_shapes=[pltpu.VMEM((8 * m_per, n), F32), pltpu.SemaphoreType.DMA((7,)), pltpu.SemaphoreType.DMA((7,)),
                        pltpu.SemaphoreType.DMA],
    )(v)


def _add_halves(name, terms):
    rows, width = terms[0].shape
    tr = max(t for t in range(16, ADD_ROWS + 1, 16) if rows % t == 0)

    def kern(*refs):
        acc = refs[0][...].astype(F32)
        for r in refs[1:-1]:
            acc = acc + r[...].astype(F32)
        refs[-1][...] = acc

    spec = pl.BlockSpec((tr, width), lambda i: (i, 0))
    return pl.pallas_call(kern, name=name, grid=(rows // tr,), in_specs=[spec] * len(terms), out_specs=spec,
                          out_shape=_sds((rows, width), F32), compiler_params=_params(1))(*terms)


def _adamw(name, w, g, m, v):
    rows, width = w.shape
    tr = rows // 4 if rows % 32 == 0 else rows

    def kern(w_ref, g_ref, m_ref, v_ref, d_ref, mo_ref, vo_ref):
        g_v = g_ref[...]
        m_new = ADAM_B1 * m_ref[...] + (1.0 - ADAM_B1) * g_v
        v_new = ADAM_B2 * v_ref[...] + (1.0 - ADAM_B2) * (g_v * g_v)
        m_hat = m_new / (1.0 - ADAM_B1 ** ADAM_STEP)
        v_hat = v_new / (1.0 - ADAM_B2 ** ADAM_STEP)
        d_ref[...] = -ADAM_LR * (m_hat / (jnp.sqrt(v_hat) + ADAM_EPS) + ADAM_WD * w_ref[...])
        mo_ref[...] = m_new
        vo_ref[...] = v_new

    spec = pl.BlockSpec((tr, width), lambda i: (i, 0))
    return pl.pallas_call(kern, name=name, grid=(rows // tr,), in_specs=[spec] * 4, out_specs=[spec] * 3,
                          out_shape=[_sds((rows, width), F32)] * 3, compiler_params=_params(1))(w, g, m, v)


SHARDED = (("w_in", D_MODEL, IN_WIDTH, 1), ("w_uq", Q_RANK, MLA_HEADS * MLA_QK, 1),
           ("w_ukv", KV_RANK, MLA_HEADS * (MLA_NOPE + MLA_V), 1), ("w_o", D_MODEL, D_MODEL, 0),
           ("w_gate", D_MODEL, D_FF, 1), ("w_up", D_MODEL, D_FF, 1), ("w_down", D_FF, D_MODEL, 0))
SMALL = (("norm_mix", D_MODEL), ("q_latent_norm", Q_RANK), ("kv_latent_norm", KV_RANK), ("out_norm_mla", MLA_WIDTH),
         ("out_norm_sb", SB_WIDTH), ("norm_ffn", D_MODEL), ("norm_final", D_MODEL))


def _pack_local(blocks):
    packed = jnp.concatenate([b.reshape(-1, PACK_W) for b in blocks], axis=0)
    return jnp.pad(packed, ((0, PACK_ROWS - packed.shape[0]), (0, 0)))


def _unpack_full(packed):
    out, off = {}, 0
    for name, r, c, axis in SHARDED:
        n = r * c // N_SHARD // PACK_W
        piece = packed[:, off:off + n, :]
        off += n
        if axis == 1:
            out[name] = piece.reshape(N_SHARD, r, c // N_SHARD).transpose(1, 0, 2).reshape(r, c)
        else:
            out[name] = piece.reshape(r, c)
    return out


def _pack_grads(grads):
    parts = []
    for name, r, c, axis in SHARDED:
        g = grads[name]
        if axis == 1:
            g = g.reshape(r, N_SHARD, c // N_SHARD).transpose(1, 0, 2)
        parts.append(g.reshape(N_SHARD, -1, PACK_W))
    packed = jnp.concatenate(parts, axis=1)
    return jnp.pad(packed, ((0, 0), (0, PACK_ROWS - packed.shape[1]), (0, 0)))


def _unpack_shard(packed):
    out, off = {}, 0
    for name, r, c, axis in SHARDED:
        n = r * c // N_SHARD // PACK_W
        shape = (r, c // N_SHARD) if axis == 1 else (r // N_SHARD, c)
        out[name] = packed[off:off + n, :].reshape(shape)
        off += n
    return out


def _rot_cols(w):
    hh = MLA_ROPE // 2
    return jnp.concatenate([-w[..., hh:], w[..., :hh]], axis=-1)


def _rot_cols_t(g):
    hh = MLA_ROPE // 2
    return jnp.concatenate([g[..., hh:], -g[..., :hh]], axis=-1)


def _prepare_weights(full, small):
    w_in = full["w_in"]
    s0, s1, s2 = Q_RANK, Q_RANK + KV_RANK, Q_RANK + KV_RANK + MLA_ROPE
    uq = full["w_uq"].reshape(Q_RANK, MLA_HEADS, MLA_QK)
    ukv = full["w_ukv"].reshape(KV_RANK, MLA_HEADS, MLA_NOPE + MLA_V)
    w_kr = w_in[:, s1:s2]
    per_tile = ROPE_TILE // MLA_ROPE
    w = {
        "w_cq": w_in[:, :s0], "w_ckv": w_in[:, s0:s1],
        "w_kr4": jnp.tile(w_kr, (1, per_tile)), "w_kr4r": jnp.tile(_rot_cols(w_kr), (1, per_tile)),
        "w_kr8": jnp.tile(w_kr, (1, MLA_HEADS)), "w_kr8r": jnp.tile(_rot_cols(w_kr), (1, MLA_HEADS)),
        "w_sbq": w_in[:, s2:s2 + SB_WIDTH], "w_sbk": w_in[:, s2 + SB_WIDTH:s2 + 2 * SB_WIDTH], "w_sbv": w_in[:, s2 + 2 * SB_WIDTH:],
        "w_qn": uq[..., :MLA_NOPE].reshape(Q_RANK, -1), "w_qr": uq[..., MLA_NOPE:].reshape(Q_RANK, -1),
        "w_qrr": _rot_cols(uq[..., MLA_NOPE:]).reshape(Q_RANK, -1),
        "w_kn": ukv[..., :MLA_NOPE].reshape(KV_RANK, -1), "w_v": ukv[..., MLA_NOPE:].reshape(KV_RANK, -1),
        "w_oa": full["w_o"][:MLA_WIDTH], "w_ob": full["w_o"][MLA_WIDTH:],
        "w_gate": full["w_gate"], "w_up": full["w_up"], "w_down": full["w_down"],
    }
    for name in list(w):
        w[name + "_t"] = w[name].T
    w.update(g_mix=small["norm_mix"], g_q=small["q_latent_norm"], g_kv=small["kv_latent_norm"], g_a=small["out_norm_mla"],
             g_b=small["out_norm_sb"], g_f=small["norm_ffn"], g_n=small["norm_final"])
    return w


def _rope_tables(positions):
    inv_freq = ROPE_THETA ** (-jnp.arange(0, MLA_ROPE, 2, dtype=F32) / MLA_ROPE)
    ang = positions.astype(F32)[:, None] * inv_freq[None, :]
    cos, sin = jnp.cos(ang), jnp.sin(ang)
    return {"cos": jnp.tile(jnp.concatenate([cos, cos], axis=1), (1, MLA_HEADS)),
            "sin": jnp.tile(jnp.concatenate([sin, sin], axis=1), (1, MLA_HEADS))}


def _by_head(g_wide, g_narrow, wide, narrow):
    r = g_wide.shape[0]
    return jnp.concatenate([g_wide.reshape(r, MLA_HEADS, wide), g_narrow.reshape(r, MLA_HEADS, narrow)], axis=-1).reshape(r, -1)


def kernel(x, positions, norm_mix, w_in, q_latent_norm, w_uq, kv_latent_norm, w_ukv, out_norm_mla, out_norm_sb, w_o, norm_ffn, w_gate, w_up, w_down, norm_final, loss_target, m_norm_mix, m_w_in, m_q_latent_norm, m_w_uq, m_kv_latent_norm, m_w_ukv, m_out_norm_mla, m_out_norm_sb, m_w_o, m_norm_ffn, m_w_gate, m_w_up, m_w_down, m_norm_final, v_norm_mix, v_w_in, v_q_latent_norm, v_w_uq, v_kv_latent_norm, v_w_ukv, v_out_norm_mla, v_out_norm_sb, v_w_o, v_norm_ffn, v_w_gate, v_w_up, v_w_down, v_norm_final):
    given = dict(norm_mix=norm_mix, w_in=w_in, q_latent_norm=q_latent_norm, w_uq=w_uq, kv_latent_norm=kv_latent_norm, w_ukv=w_ukv,
                 out_norm_mla=out_norm_mla, out_norm_sb=out_norm_sb, w_o=w_o, norm_ffn=norm_ffn, w_gate=w_gate, w_up=w_up,
                 w_down=w_down, norm_final=norm_final)
    mom_m = dict(norm_mix=m_norm_mix, w_in=m_w_in, q_latent_norm=m_q_latent_norm, w_uq=m_w_uq, kv_latent_norm=m_kv_latent_norm,
                 w_ukv=m_w_ukv, out_norm_mla=m_out_norm_mla, out_norm_sb=m_out_norm_sb, w_o=m_w_o, norm_ffn=m_norm_ffn,
                 w_gate=m_w_gate, w_up=m_w_up, w_down=m_w_down, norm_final=m_norm_final)
    mom_v = dict(norm_mix=v_norm_mix, w_in=v_w_in, q_latent_norm=v_q_latent_norm, w_uq=v_w_uq, kv_latent_norm=v_kv_latent_norm,
                 w_ukv=v_w_ukv, out_norm_mla=v_out_norm_mla, out_norm_sb=v_out_norm_sb, w_o=v_w_o, norm_ffn=v_norm_ffn,
                 w_gate=v_w_gate, w_up=v_w_up, w_down=v_w_down, norm_final=v_norm_final)
    xs = x[0]
    tgt = loss_target[0]
    s = xs.shape[0]
    c_idx = lax.axis_index("c")

    shard2d = {name: given[name].reshape(given[name].shape[-2:]) for name, *_ in SHARDED}
    packed_w = _pack_local([shard2d[name].astype(BF16) for name, *_ in SHARDED])
    gathered = _allgather_shards(packed_w).reshape(N_SHARD, packed_w.shape[0], PACK_W)
    small = {name: given[name].reshape(1, n) for name, n in SMALL}
    w = _prepare_weights(_unpack_full(gathered), small)
    tabs = _rope_tables(positions[0])
    msuf, mpre = _sb_masks(min(SB_TK, s))

    u, cq, ckv, cqn, ckvn, qn, qr, kn, vm, kr, sq, sk, sv = _fwd_a(xs, tabs, w)
    o_mla, lse = _mla_fwd(qn, qr, kn, kr, vm)
    o_sb, cmat = _sb_fwd(sq, sk, sv, msuf)
    merged, h1, f, gate, up, act = _fwd_b1(xs, o_mla, o_sb, w)
    dh2, loss_part, dg_n = _fwd_b2(h1, act, tgt, w)

    dgate, dup = _bwd_b1(dh2, gate, up, w)
    dh1, do_mla, do_sb, dg_f, dg_a, dg_b = _bwd_b2(dgate, dup, h1, dh2, o_mla, o_sb, w)
    dqn, dqr, dkn, dkr, dvm = _mla_bwd(qn, qr, kn, kr, vm, o_mla, do_mla, lse)
    dsq, dsk, dsv = _sb_bwd(sq, sk, sv, do_sb, cmat, msuf, mpre)
    dx, a1, a2, dcq, dckv, dkrc, dkrs, dg_q, dg_kv, dg_mix = _bwd_a(xs, dh1, cq, ckv, dqn, dqr, dkn, dvm, dkr, dsq, dsk, dsv, tabs, w)

    g_cq, g_ckv, g_krc, g_krs, g_sq, g_sk, g_sv = _tn_multi("dw_in", u, [dcq, dckv, dkrc, dkrs, dsq, dsk, dsv])
    g_qn, g_qr1, g_qr2 = _tn_multi("dw_uq", cqn, [dqn, a1, a2])
    g_kn, g_v = _tn_multi("dw_ukv", ckvn, [dkn, dvm])
    slots = lambda g: g.reshape(g.shape[0], MLA_HEADS, MLA_ROPE)
    g_kr = jnp.sum(slots(g_krc), axis=1) + _rot_cols_t(jnp.sum(slots(g_krs), axis=1))
    g_qr = (slots(g_qr1) + _rot_cols_t(slots(g_qr2))).reshape(Q_RANK, -1)
    grads = {
        "w_in": jnp.concatenate([g_cq, g_ckv, g_kr, g_sq, g_sk, g_sv], axis=1),
        "w_uq": _by_head(g_qn, g_qr, MLA_NOPE, MLA_ROPE),
        "w_ukv": _by_head(g_kn, g_v, MLA_NOPE, MLA_V),
        "w_o": _tn_matmul("dw_o", merged, dh1),
        "w_gate": _tn_matmul("dw_gate", f, dgate),
        "w_up": _tn_matmul("dw_up", f, dup),
        "w_down": _tn_matmul("dw_down", act, dh2),
    }

    packed_g = _pack_grads(grads)
    half = packed_g.shape[1] // 2
    keep = lax.dynamic_slice_in_dim(packed_g, c_idx * half, half, axis=1)
    give = lax.dynamic_slice_in_dim(packed_g, (1 - c_idx) * half, half, axis=1)
    got = _sibling_swap("swap_halves", give.astype(BF16))
    chip_sum = _add_halves("add_sibling", [keep.reshape(-1, PACK_W), got.reshape(-1, PACK_W)]).reshape(keep.shape)
    others = _chip_scatter(chip_sum.astype(BF16))
    own = lax.dynamic_index_in_dim(chip_sum, 2 * lax.axis_index("x") + lax.axis_index("y"), axis=0, keepdims=False)
    mine = _add_halves("add_chips", [own] + [others[j] for j in range(N_SHARD - 1)])
    theirs = _sibling_swap("swap_result", mine)
    lo = jnp.where(c_idx == 0, mine, theirs)
    hi = jnp.where(c_idx == 0, theirs, mine)
    g_shard = _unpack_shard(jnp.concatenate([lo, hi], axis=0))

    small_parts = jnp.concatenate([dg_mix, dg_q, dg_kv, dg_a, dg_b, dg_f, dg_n], axis=1)
    small_g = _allreduce_small(jnp.broadcast_to(small_parts, (8, small_parts.shape[1])))[0:1]
    loss = lax.psum(loss_part[0, 0], ("x", "y", "c"))

    g_out, d_out, m_out, v_out = {}, {}, {}, {}
    for name, *_ in SHARDED:
        shape = given[name].shape
        d, mn, vn = _adamw("adamw_" + name, shard2d[name], g_shard[name], mom_m[name].reshape(shard2d[name].shape),
                           mom_v[name].reshape(shard2d[name].shape))
        g_out[name], d_out[name], m_out[name], v_out[name] = (t.reshape(shape) for t in (g_shard[name], d, mn, vn))
    cat = lambda src: jnp.concatenate([src[name].reshape(1, n) for name, n in SMALL], axis=1)
    d, mn, vn = _adamw("adamw_small", cat(given), small_g, cat(mom_m), cat(mom_v))
    off = 0
    for name, n in SMALL:
        shape = given[name].shape
        g_out[name], d_out[name], m_out[name], v_out[name] = (t[:, off:off + n].reshape(shape) for t in (small_g, d, mn, vn))
        off += n

    order = ["norm_mix", "w_in", "q_latent_norm", "w_uq", "kv_latent_norm", "w_ukv", "out_norm_mla", "out_norm_sb", "w_o",
             "norm_ffn", "w_gate", "w_up", "w_down", "norm_final"]
    return (loss, dx[None], *[g_out[n] for n in order], *[d_out[n] for n in order], *[m_out[n] for n in order],
            *[v_out[n] for n in order])
```

```python
import functools
import math

import jax
import jax.numpy as jnp
from jax import lax
from jax.experimental import pallas as pl
from jax.experimental.pallas import tpu as pltpu

F32 = jnp.float32
BF16 = jnp.bfloat16
MESH = pl.DeviceIdType.MESH

D_MODEL = 1024
EPS = 1e-6
MLA_HEADS = 8
MLA_NOPE = 64
MLA_ROPE = 32
MLA_V = 64
MLA_QK = MLA_NOPE + MLA_ROPE
Q_RANK = 256
KV_RANK = 128
ROPE_THETA = 10000.0
SB_HEADS = 8
SB_DIM = 64
MLA_WIDTH = MLA_HEADS * MLA_V
SB_WIDTH = SB_HEADS * SB_DIM
D_FF = 2816
IN_WIDTH = Q_RANK + KV_RANK + MLA_ROPE + 3 * SB_WIDTH

ADAM_LR = 0.001
ADAM_B1 = 0.9
ADAM_B2 = 0.999
ADAM_EPS = 1e-08
ADAM_WD = 0.01
ADAM_STEP = 10

N_SHARD = 4
PACK_W = 256
PACK_ROWS = 12288
LANES = 128
ROPE_TILE = LANES
VMEM_LIMIT = 56 * 1024 * 1024
TN_ACC_BYTES = 6 * 1024 * 1024 + 512 * 1024
NEG = -1e30
MLA_SCALE = 1.0 / math.sqrt(MLA_QK)
MLA_DK_SCALE = math.log(2.0)
MLA_QSCALE = MLA_SCALE * math.log2(math.e)
SB_SKIP = 110.0

ROW_TILE = 512
ROW_TILE_ELEMENTWISE = 256
MLA_TQ = 1024
SB_TQ = 512
MLA_TK = 1024
MLA_BWD_TK = 512
MLA_DIAG_TK = 512
SB_TK = 256
TN_TS = 2048
ADD_ROWS = 3072


def _dot(a, b):
    return jnp.dot(a, b, preferred_element_type=F32)


def _dot_nt(a, b):
    return lax.dot_general(a, b, (((1,), (1,)), ((), ())), preferred_element_type=F32)


def _dot_tn(a, b):
    return lax.dot_general(a, b, (((0,), (0,)), ((), ())), preferred_element_type=F32)


def _params(n_grid, vmem=VMEM_LIMIT):
    return pltpu.CompilerParams(dimension_semantics=("arbitrary",) * n_grid, vmem_limit_bytes=vmem)


def _rms(x):
    r = lax.rsqrt(jnp.mean(x * x, axis=-1, keepdims=True) + EPS)
    return x * r, r


def _rms_bwd(n, r, g, dy):
    dn = dy * g
    dx = r * (dn - n * jnp.mean(dn * n, axis=-1, keepdims=True))
    return dx, jnp.sum(dy * n, axis=0, keepdims=True)


def _accumulate(ref, val, step):
    @pl.when(step == 0)
    def _():
        ref[...] = val

    @pl.when(step != 0)
    def _():
        ref[...] += val


def _rowwise(name, body, rows, consts, row_out, acc_out, tm):
    n_rows = rows[0].shape[0]
    tm = min(tm, n_rows)
    nr, nc, no = len(rows), len(consts), len(row_out)

    def kern(*refs):
        body(refs[:nr], refs[nr:nr + nc], refs[nr + nc:nr + nc + no], refs[nr + nc + no:], pl.program_id(0))

    in_specs = [pl.BlockSpec((tm, a.shape[1]), lambda i: (i, 0)) for a in rows]
    in_specs += [pl.BlockSpec(a.shape, lambda i: (0, 0), pipeline_mode=pl.Buffered(1)) for a in consts]
    out_specs = [pl.BlockSpec((tm, s.shape[1]), lambda i: (i, 0)) for s in row_out]
    out_specs += [pl.BlockSpec(s.shape, lambda i: (0, 0)) for s in acc_out]
    return pl.pallas_call(
        kern, name=name, grid=(n_rows // tm,), in_specs=in_specs, out_specs=out_specs,
        out_shape=list(row_out) + list(acc_out), compiler_params=_params(1),
    )(*rows, *consts)


def _sds(shape, dtype):
    return jax.ShapeDtypeStruct(shape, dtype)


def _fwd_a(x, tabs, w):
    s = x.shape[0]

    def body(r, c, o, a, step):
        x_ref, cos_ref, sin_ref = r
        gmix, wcq, wckv, wkr, wkrr, wsq, wsk, wsv, gq, wqn, wqr, wqrr, gkv, wkn, wv = c
        u_o, cq_o, ckv_o, cqn_o, ckvn_o, qn_o, qr_o, kn_o, v_o, kr_o, sq_o, sk_o, sv_o = o
        cos, sin = cos_ref[...], sin_ref[...]
        n, _ = _rms(x_ref[...])
        u = (n * gmix[...]).astype(BF16)
        u_o[...] = u
        cq = _dot(u, wcq[...])
        ckv = _dot(u, wckv[...])
        kr_o[...] = (_dot(u, wkr[...]) * cos[:, :ROPE_TILE] + _dot(u, wkrr[...]) * sin[:, :ROPE_TILE]).astype(BF16)
        sq_o[...] = _dot(u, wsq[...]).astype(BF16)
        sk_o[...] = _dot(u, wsk[...]).astype(BF16)
        sv_o[...] = _dot(u, wsv[...]).astype(BF16)
        cq_o[...] = cq
        ckv_o[...] = ckv
        nq, _ = _rms(cq)
        cqn = (nq * gq[...]).astype(BF16)
        cqn_o[...] = cqn
        qn_o[...] = (_dot(cqn, wqn[...]) * MLA_QSCALE).astype(BF16)
        qr_o[...] = ((_dot(cqn, wqr[...]) * cos + _dot(cqn, wqrr[...]) * sin) * MLA_QSCALE).astype(BF16)
        nkv, _ = _rms(ckv)
        ckvn = (nkv * gkv[...]).astype(BF16)
        ckvn_o[...] = ckvn
        kn_o[...] = _dot(ckvn, wkn[...]).astype(BF16)
        v_o[...] = _dot(ckvn, wv[...]).astype(BF16)

    outs = [
        _sds((s, D_MODEL), BF16), _sds((s, Q_RANK), F32), _sds((s, KV_RANK), F32), _sds((s, Q_RANK), BF16),
        _sds((s, KV_RANK), BF16), _sds((s, MLA_HEADS * MLA_NOPE), BF16), _sds((s, MLA_HEADS * MLA_ROPE), BF16),
        _sds((s, MLA_HEADS * MLA_NOPE), BF16), _sds((s, MLA_WIDTH), BF16), _sds((s, ROPE_TILE), BF16),
        _sds((s, SB_WIDTH), BF16), _sds((s, SB_WIDTH), BF16), _sds((s, SB_WIDTH), BF16),
    ]
    consts = [w["g_mix"], w["w_cq"], w["w_ckv"], w["w_kr4"], w["w_kr4r"], w["w_sbq"], w["w_sbk"], w["w_sbv"], w["g_q"],
              w["w_qn"], w["w_qr"], w["w_qrr"], w["g_kv"], w["w_kn"], w["w_v"]]
    return _rowwise("fwd_a", body, [x, tabs["cos"], tabs["sin"]], consts, outs, [], ROW_TILE)


def _fwd_b1(x, o_mla, o_sb, w):
    s = x.shape[0]

    def body(r, c, o, a, step):
        x_ref, oa_ref, ob_ref = r
        ga, gb, woa, wob, gf, wg, wu = c
        mg_o, h1_o, f_o, gate_o, up_o, act_o = o
        na, _ = _rms(oa_ref[...])
        nb, _ = _rms(ob_ref[...])
        ma = (na * ga[...]).astype(BF16)
        mb = (nb * gb[...]).astype(BF16)
        mg_o[:, :MLA_WIDTH] = ma
        mg_o[:, MLA_WIDTH:] = mb
        h1 = x_ref[...] + _dot(ma, woa[...]) + _dot(mb, wob[...])
        h1_o[...] = h1
        nf, _ = _rms(h1)
        f = (nf * gf[...]).astype(BF16)
        f_o[...] = f
        gate = _dot(f, wg[...])
        up = _dot(f, wu[...])
        gate_o[...] = gate.astype(BF16)
        up_o[...] = up.astype(BF16)
        act_o[...] = (gate * (1.0 / (1.0 + jnp.exp(-gate))) * up).astype(BF16)

    outs = [_sds((s, D_MODEL), BF16), _sds((s, D_MODEL), F32), _sds((s, D_MODEL), BF16), _sds((s, D_FF), BF16),
            _sds((s, D_FF), BF16), _sds((s, D_FF), BF16)]
    consts = [w["g_a"], w["g_b"], w["w_oa"], w["w_ob"], w["g_f"], w["w_gate"], w["w_up"]]
    return _rowwise("fwd_b1", body, [x, o_mla, o_sb], consts, outs, [], ROW_TILE)


def _fwd_b2(h1, act, tgt, w):
    s = h1.shape[0]

    def body(r, c, o, a, step):
        h1_ref, act_ref, t_ref = r
        wd, gn = c
        (dh2_o,) = o
        loss_o, dgn_o = a
        h2 = h1_ref[...] + _dot(act_ref[...], wd[...])
        n2, r2 = _rms(h2)
        err = n2 * gn[...] - t_ref[...]
        part = jnp.sum(jnp.sum(err * err, axis=1, keepdims=True), axis=0, keepdims=True) * (0.5 / D_MODEL)
        _accumulate(loss_o, jnp.broadcast_to(part, (1, LANES)), step)
        dh2, dgn = _rms_bwd(n2, r2, gn[...], err * (1.0 / D_MODEL))
        dh2_o[...] = dh2
        _accumulate(dgn_o, dgn, step)

    return _rowwise("fwd_b2", body, [h1, act, tgt], [w["w_down"], w["g_n"]], [_sds((s, D_MODEL), F32)],
                    [_sds((1, LANES), F32), _sds((1, D_MODEL), F32)], ROW_TILE)


def _bwd_b1(dh2, gate, up, w):
    s = dh2.shape[0]

    def body(r, c, o, a, step):
        dh2_ref, gate_ref, up_ref = r
        (wdt,) = c
        dgate_o, dup_o = o
        dact = _dot(dh2_ref[...].astype(BF16), wdt[...])
        gate = gate_ref[...].astype(F32)
        sig = 1.0 / (1.0 + jnp.exp(-gate))
        dup_o[...] = (dact * (gate * sig)).astype(BF16)
        dgate_o[...] = (dact * up_ref[...].astype(F32) * (sig * (1.0 + gate * (1.0 - sig)))).astype(BF16)

    return _rowwise("bwd_b1", body, [dh2, gate, up], [w["w_down_t"]], [_sds((s, D_FF), BF16), _sds((s, D_FF), BF16)],
                    [], ROW_TILE_ELEMENTWISE)


def _bwd_b2(dgate, dup, h1, dh2, o_mla, o_sb, w):
    s = h1.shape[0]

    def body(r, c, o, a, step):
        dgate_ref, dup_ref, h1_ref, dh2_ref, oa_ref, ob_ref = r
        wgt, wut, gf, woat, wobt, ga, gb = c
        dh1_o, doa_o, dob_o = o
        dgf_o, dga_o, dgb_o = a
        df = _dot(dgate_ref[...], wgt[...]) + _dot(dup_ref[...], wut[...])
        nf, rf = _rms(h1_ref[...])
        dres, dgf = _rms_bwd(nf, rf, gf[...], df)
        dh1 = dh2_ref[...] + dres
        dh1_o[...] = dh1
        dh1b = dh1.astype(BF16)
        na, ra = _rms(oa_ref[...])
        doa, dga = _rms_bwd(na, ra, ga[...], _dot(dh1b, woat[...]))
        nb, rb = _rms(ob_ref[...])
        dob, dgb = _rms_bwd(nb, rb, gb[...], _dot(dh1b, wobt[...]))
        doa_o[...] = doa
        dob_o[...] = dob
        _accumulate(dgf_o, dgf, step)
        _accumulate(dga_o, dga, step)
        _accumulate(dgb_o, dgb, step)

    consts = [w["w_gate_t"], w["w_up_t"], w["g_f"], w["w_oa_t"], w["w_ob_t"], w["g_a"], w["g_b"]]
    outs = [_sds((s, D_MODEL), F32), _sds((s, MLA_WIDTH), F32), _sds((s, SB_WIDTH), F32)]
    accs = [_sds((1, D_MODEL), F32), _sds((1, MLA_WIDTH), F32), _sds((1, SB_WIDTH), F32)]
    return _rowwise("bwd_b2", body, [dgate, dup, h1, dh2, o_mla, o_sb], consts, outs, accs, ROW_TILE)


def _fold_pairs(t):
    return jnp.concatenate([t[:, :LANES] + t[:, LANES:2 * LANES], t[:, 2 * LANES:3 * LANES] + t[:, 3 * LANES:]], axis=1)


def _bwd_a(x, dh1, cq, ckv, dqn, dqr, dkn, dvm, dkr, dsq, dsk, dsv, tabs, w):
    s = x.shape[0]

    def body(r, c, o, a, step):
        x_ref, dh1_ref, cq_ref, ckv_ref, dqn_ref, dqr_ref, dkn_ref, dvm_ref, dkr_ref, dsq_ref, dsk_ref, dsv_ref, cos_ref, sin_ref = r
        wqnt, wqrt, wqrrt, gq, wknt, wvt, gkv, wcqt, wckvt, wkrt, wkrrt, wsqt, wskt, wsvt, gmix = c
        dx_o, a1_o, a2_o, dcq_o, dckv_o, dkrc_o, dkrs_o = o
        dgq_o, dgkv_o, dgmix_o = a
        cos, sin = cos_ref[...], sin_ref[...]
        dqr = _fold_pairs(dqr_ref[...])
        a1 = (dqr * cos).astype(BF16)
        a2 = (dqr * sin).astype(BF16)
        a1_o[...] = a1
        a2_o[...] = a2
        nq, rq = _rms(cq_ref[...])
        dcqn = _dot(dqn_ref[...].astype(BF16), wqnt[...]) + _dot(a1, wqrt[...]) + _dot(a2, wqrrt[...])
        dcq, dgq = _rms_bwd(nq, rq, gq[...], dcqn)
        nkv, rkv = _rms(ckv_ref[...])
        dckvn = _dot((dkn_ref[...] * MLA_DK_SCALE).astype(BF16), wknt[...]) + _dot(dvm_ref[...].astype(BF16), wvt[...])
        dckv, dgkv = _rms_bwd(nkv, rkv, gkv[...], dckvn)
        dkr = _fold_pairs(dkr_ref[...]) * MLA_DK_SCALE
        dcq_b = dcq.astype(BF16)
        dckv_b = dckv.astype(BF16)
        dkrc = (dkr * cos).astype(BF16)
        dkrs = (dkr * sin).astype(BF16)
        dcq_o[...] = dcq_b
        dckv_o[...] = dckv_b
        dkrc_o[...] = dkrc
        dkrs_o[...] = dkrs
        du = (_dot(dcq_b, wcqt[...]) + _dot(dckv_b, wckvt[...]) + _dot(dkrc, wkrt[...]) + _dot(dkrs, wkrrt[...])
              + _dot(dsq_ref[...].astype(BF16), wsqt[...]) + _dot(dsk_ref[...].astype(BF16), wskt[...])
              + _dot(dsv_ref[...].astype(BF16), wsvt[...]))
        nx, rx = _rms(x_ref[...])
        dres, dgmix = _rms_bwd(nx, rx, gmix[...], du)
        dx_o[...] = dh1_ref[...] + dres
        _accumulate(dgq_o, dgq, step)
        _accumulate(dgkv_o, dgkv, step)
        _accumulate(dgmix_o, dgmix, step)

    consts = [w["w_qn_t"], w["w_qr_t"], w["w_qrr_t"], w["g_q"], w["w_kn_t"], w["w_v_t"], w["g_kv"], w["w_cq_t"], w["w_ckv_t"],
              w["w_kr8_t"], w["w_kr8r_t"], w["w_sbq_t"], w["w_sbk_t"], w["w_sbv_t"], w["g_mix"]]
    rope_w = MLA_HEADS * MLA_ROPE
    outs = [_sds((s, D_MODEL), F32), _sds((s, rope_w), BF16), _sds((s, rope_w), BF16), _sds((s, Q_RANK), BF16),
            _sds((s, KV_RANK), BF16), _sds((s, rope_w), BF16), _sds((s, rope_w), BF16)]
    accs = [_sds((1, Q_RANK), F32), _sds((1, KV_RANK), F32), _sds((1, D_MODEL), F32)]
    rows = [x, dh1, cq, ckv, dqn, dqr, dkn, dvm, dkr, dsq, dsk, dsv, tabs["cos"], tabs["sin"]]
    return _rowwise("bwd_a", body, rows, consts, outs, accs, ROW_TILE)


def _tn_multi(name, x, ys):
    s, k = x.shape
    ts = min(TN_TS, s)
    n_y = len(ys)

    def kern(*refs):
        step = pl.program_id(0)
        xb = refs[0][...].astype(BF16)
        for j in range(n_y):
            _accumulate(refs[1 + n_y + j], _dot_tn(xb, refs[1 + j][...].astype(BF16)), step)

    return pl.pallas_call(
        kern, name=name, grid=(s // ts,),
        in_specs=[pl.BlockSpec((ts, k), lambda i: (i, 0))] + [pl.BlockSpec((ts, y.shape[1]), lambda i: (i, 0)) for y in ys],
        out_specs=[pl.BlockSpec((k, y.shape[1]), lambda i: (0, 0)) for y in ys],
        out_shape=[_sds((k, y.shape[1]), F32) for y in ys], compiler_params=_params(1),
    )(x, *ys)


def _tn_tile(k, n):
    if n % LANES or k * n * 4 <= TN_ACC_BYTES:
        return n
    units = n // LANES
    best = 1
    for d in range(1, units + 1):
        if units % d == 0 and k * d * LANES * 4 <= TN_ACC_BYTES:
            best = d
    return best * LANES


def _tn_matmul(name, x, y):
    s, k = x.shape
    n = y.shape[1]
    ts = min(TN_TS, s)
    tn = _tn_tile(k, n)

    def kern(x_ref, y_ref, o_ref):
        step = pl.program_id(1)
        _accumulate(o_ref, _dot_tn(x_ref[...].astype(BF16), y_ref[...].astype(BF16)), step)

    return pl.pallas_call(
        kern, name=name, grid=(n // tn, s // ts),
        in_specs=[pl.BlockSpec((ts, k), lambda j, i: (i, 0)), pl.BlockSpec((ts, tn), lambda j, i: (i, j))],
        out_specs=pl.BlockSpec((k, tn), lambda j, i: (0, j)), out_shape=_sds((k, n), F32), compiler_params=_params(2),
    )(x, y)


def _lanes(rows, lo, width):
    lane = lax.broadcasted_iota(jnp.int32, (rows, LANES), 1)
    return jnp.logical_and(lane >= lo, lane < lo + width)


def _keep(mask, t):
    return jnp.where(mask, t, jnp.zeros_like(t))


def _mla_qcat(qn_ref, qr_ref, rope_lo, half, rows):
    qn = _keep(_lanes(rows, MLA_NOPE * half, MLA_NOPE), qn_ref[...])
    qr = _keep(_lanes(rows, rope_lo, MLA_ROPE), qr_ref[...])
    return jnp.concatenate([qn, qr], axis=1)


def _diag_mask(rows, width, row0, col0):
    row = lax.broadcasted_iota(jnp.int32, (rows, width), 0)
    col = lax.broadcasted_iota(jnp.int32, (rows, width), 1)
    return col + (col0 - row0) <= row


def _mla_fwd(qn, qr, kn, kr, v, tq=MLA_TQ, tk=MLA_TK, td=MLA_TQ):
    s = qn.shape[0]
    tq, tk, td = min(tq, s), min(tk, s), min(td, s)
    ratio = tq // tk

    def kern(qn_ref, qr_ref, kn_ref, kr_ref, v_ref, o_ref, lse_ref):
        g = pl.program_id(0)
        i = pl.program_id(1)
        for half in range(2):
            qcat = _mla_qcat(qn_ref, qr_ref, MLA_ROPE * (2 * (g % 2) + half), half, tq)

            def block(k0, width, carry, row0, masked, qcat=qcat):
                m, l, acc = (c[row0:] for c in carry)
                ks = pl.ds(pl.multiple_of(k0, width), width)
                kcat = jnp.concatenate([kn_ref[ks, :], kr_ref[ks, :]], axis=1)
                sc = _dot_nt(qcat[row0:], kcat)
                if masked:
                    sc = jnp.where(_diag_mask(tq - row0, width, row0, row0), sc, NEG)
                m_new = jnp.maximum(m, jnp.max(sc, axis=1, keepdims=True))
                p = jnp.exp2(sc - m_new)
                alpha = jnp.exp2(m - m_new)
                l = alpha * l + jnp.sum(p, axis=1, keepdims=True)
                acc = alpha * acc + _dot(p.astype(BF16), v_ref[ks, :])
                new = (m_new, l, acc)
                return new if row0 == 0 else tuple(jnp.concatenate([c[:row0], n], axis=0) for c, n in zip(carry, new))

            carry = (jnp.full((tq, 1), NEG, F32), jnp.zeros((tq, 1), F32), jnp.zeros((tq, LANES), F32))
            carry = lax.fori_loop(0, i * ratio, lambda kb, c, block=block: block(kb * tk, tk, c, 0, False), carry)
            for row0 in range(0, tq, td):
                carry = block(i * tq + row0, td, carry, row0, True)
            m, l, acc = carry
            out = _keep(_lanes(tq, MLA_V * half, MLA_V), acc / l)
            lse = _keep(_lanes(tq, MLA_ROPE * half, MLA_ROPE), jnp.broadcast_to(m + jnp.log2(l), (tq, LANES)))
            if half == 0:
                o_ref[...] = out
                lse_ref[...] = lse
            else:
                o_ref[...] += out
                lse_ref[...] += lse

    qblk = pl.BlockSpec((tq, LANES), lambda g, i: (i, g))
    full = pl.BlockSpec((s, LANES), lambda g, i: (0, g))
    return pl.pallas_call(
        kern, name="mla_fwd", grid=(MLA_HEADS // 2, s // tq),
        in_specs=[qblk, pl.BlockSpec((tq, LANES), lambda g, i: (i, g // 2)), full, pl.BlockSpec((s, LANES), lambda g, i: (0, 0)), full],
        out_specs=[qblk, qblk],
        out_shape=[_sds((s, MLA_WIDTH), F32), _sds((s, MLA_HEADS // 2 * LANES), F32)], compiler_params=_params(2),
    )(qn, qr, kn, kr, v)


def _mla_bwd(qn, qr, kn, kr, v, o, do, lse, tq=MLA_TQ, tk=MLA_BWD_TK, td=MLA_DIAG_TK):
    s = qn.shape[0]
    tq, tk, td = min(tq, s), min(tk, s), min(td, s)
    ratio = tq // tk

    def kern(qn_ref, qr_ref, kn_ref, kr_ref, v_ref, o_ref, do_ref, lse_ref, dqn_ref, dqr_ref, dkn_ref, dkr_ref, dv_ref):
        g = pl.program_id(0)
        i = pl.program_id(1)

        @pl.when(i == 0)
        def _():
            dkn_ref[...] = jnp.zeros_like(dkn_ref)
            dkr_ref[...] = jnp.zeros_like(dkr_ref)
            dv_ref[...] = jnp.zeros_like(dv_ref)

        for half in range(2):
            rope_lo = MLA_ROPE * (2 * (g % 2) + half)
            qcat = _mla_qcat(qn_ref, qr_ref, rope_lo, half, tq)
            mine = _lanes(tq, MLA_V * half, MLA_V)
            do_f = _keep(mine, do_ref[...])
            do_b = do_f.astype(BF16)
            delta = jnp.sum(do_f * o_ref[...], axis=1, keepdims=True)
            lse_v = lse_ref[:, MLA_ROPE * half:MLA_ROPE * half + 1]

            def block(k0, width, dq_acc, row0, masked, qcat=qcat, do_b=do_b, delta=delta, lse_v=lse_v):
                ks = pl.ds(pl.multiple_of(k0, width), width)
                kcat = jnp.concatenate([kn_ref[ks, :], kr_ref[ks, :]], axis=1)
                qc, dob = qcat[row0:], do_b[row0:]
                p = jnp.exp2(_dot_nt(qc, kcat) - lse_v[row0:])
                if masked:
                    p = jnp.where(_diag_mask(tq - row0, width, row0, row0), p, 0.0)
                ds = (p * (_dot_nt(dob, v_ref[ks, :]) - delta[row0:])).astype(BF16)
                dv_ref[ks, :] += _dot_tn(p.astype(BF16), dob)
                dkc = _dot_tn(ds, qc)
                dkn_ref[ks, :] += dkc[:, :LANES]
                dkr_ref[ks, :] += dkc[:, LANES:]
                new = dq_acc[row0:] + _dot(ds, kcat)
                return new if row0 == 0 else jnp.concatenate([dq_acc[:row0], new], axis=0)

            acc = lax.fori_loop(0, i * ratio, lambda kb, c, block=block: block(kb * tk, tk, c, 0, False),
                                jnp.zeros((tq, 2 * LANES), F32))
            for row0 in range(0, tq, td):
                acc = block(i * tq + row0, td, acc, row0, True)
            dqn = _keep(_lanes(tq, MLA_NOPE * half, MLA_NOPE), acc[:, :LANES] * MLA_SCALE)
            dqr = _keep(_lanes(tq, rope_lo, MLA_ROPE), acc[:, LANES:] * MLA_SCALE)
            if half == 0:
                dqn_ref[...] = dqn
                dqr_ref[...] = dqr
            else:
                dqn_ref[...] += dqn
                dqr_ref[...] += dqr

    qblk = pl.BlockSpec((tq, LANES), lambda g, i: (i, g))
    full = pl.BlockSpec((s, LANES), lambda g, i: (0, g))
    once = lambda spec_map: pl.BlockSpec((s, LANES), spec_map, pipeline_mode=pl.Buffered(1))
    wide = _sds((s, MLA_HEADS // 2 * LANES), F32)
    return pl.pallas_call(
        kern, name="mla_bwd", grid=(MLA_HEADS // 2, s // tq),
        in_specs=[qblk, pl.BlockSpec((tq, LANES), lambda g, i: (i, g // 2)), once(lambda g, i: (0, g)), once(lambda g, i: (0, 0)),
                  once(lambda g, i: (0, g)), qblk, qblk, qblk],
        out_specs=[qblk, qblk, full, full, full],
        out_shape=[wide, wide, wide, wide, wide], compiler_params=_params(2),
    )(qn, qr, kn, kr, v, o, do, lse)


def _sb_masks(tk):
    j = lax.broadcasted_iota(jnp.int32, (tk, tk), 0)
    c = lax.broadcasted_iota(jnp.int32, (tk, tk), 1)
    return (j > c).astype(BF16), (j < c).astype(BF16)


def _sb_scores(qs, kk, msuf, strict):
    z = _dot_nt(qs, kk)
    lom = -(jnp.maximum(z, 0.0) + jnp.log(1.0 + jnp.exp(-jnp.abs(z))))
    if strict is not None:
        lom = jnp.where(strict, lom, 0.0)
    hi = lom.astype(BF16)
    lo = (lom - hi.astype(F32)).astype(BF16)
    return z, lom, _dot(hi, msuf) + _dot(lo, msuf)


def _sb_strict(tq, tk, d):
    row = lax.broadcasted_iota(jnp.int32, (tq, tk), 0)
    col = lax.broadcasted_iota(jnp.int32, (tq, tk), 1)
    return col + d * tk < row


def _sb_fwd(q, k, v, msuf, tq=SB_TQ, tk=SB_TK):
    s = q.shape[0]
    tq, tk = min(tq, s), min(tk, s)
    ratio = tq // tk

    def kern(q_ref, k_ref, v_ref, m_ref, o_ref, c_ref):
        i = pl.program_id(1)
        msf = m_ref[...]
        lane = lax.broadcasted_iota(jnp.int32, (tq, LANES), 1)
        for half in range(2):
            mine = _lanes(tq, SB_DIM * half, SB_DIM)
            qs = _keep(mine, q_ref[...]) * 0.125

            def block(kb, carry, dd, qs=qs):
                c, acc, cm = carry
                ks = pl.ds(pl.multiple_of(kb * tk, tk), tk)
                strict = None if dd is None else _sb_strict(tq, tk, dd)
                z, lom, suf = _sb_scores(qs, k_ref[ks, :], msf, strict)
                a = jnp.exp(z + lom + (suf + c))
                if strict is not None:
                    a = jnp.where(strict, a, 0.0)
                acc = acc + _dot(a.astype(BF16), v_ref[ks, :])
                cm = jnp.where(lane == kb, c, cm)
                return c + jnp.sum(lom, axis=1, keepdims=True), acc, cm

            carry = (jnp.zeros((tq, 1), F32), jnp.zeros((tq, LANES), F32), jnp.full((tq, LANES), NEG, F32))
            for dd in range(ratio - 1, -1, -1):
                carry = block(i * ratio + dd, carry, dd)

            def live(st):
                return jnp.logical_and(st[0] >= 0, jnp.max(st[1]) > -SB_SKIP)

            def step(st, block=block):
                return (st[0] - 1, *block(st[0], st[1:], None))

            _, _, acc, cm = lax.while_loop(live, step, (i * ratio - 1, *carry))
            if half == 0:
                o_ref[...] = _keep(mine, acc)
            else:
                o_ref[...] += _keep(mine, acc)
            c_ref[:, LANES * half:LANES * (half + 1)] = cm

    qblk = lambda n: pl.BlockSpec((tq, n), lambda g, i: (i, g))
    full = pl.BlockSpec((s, LANES), lambda g, i: (0, g))
    return pl.pallas_call(
        kern, name="sb_fwd", grid=(SB_HEADS // 2, s // tq),
        in_specs=[qblk(LANES), full, full, pl.BlockSpec((tk, tk), lambda g, i: (0, 0))],
        out_specs=[qblk(LANES), qblk(2 * LANES)],
        out_shape=[_sds((s, SB_WIDTH), F32), _sds((s, SB_HEADS * LANES), F32)], compiler_params=_params(2),
    )(q, k, v, msuf)


def _sb_bwd(q, k, v, do, cmat, msuf, mpre, tq=SB_TQ, tk=SB_TK):
    s = q.shape[0]
    tq, tk = min(tq, s), min(tk, s)
    ratio = tq // tk

    def kern(q_ref, k_ref, v_ref, do_ref, c_ref, ms_ref, mp_ref, dq_ref, dk_ref, dv_ref):
        i = pl.program_id(1)

        @pl.when(i == 0)
        def _():
            dk_ref[...] = jnp.zeros_like(dk_ref)
            dv_ref[...] = jnp.zeros_like(dv_ref)

        msf = ms_ref[...]
        mpf = mp_ref[...]
        lane = lax.broadcasted_iota(jnp.int32, (tq, LANES), 1)
        lane1 = lax.broadcasted_iota(jnp.int32, (1, LANES), 1)
        for half in range(2):
            mine = _lanes(tq, SB_DIM * half, SB_DIM)
            qv = _keep(mine, q_ref[...])
            qs = qv * 0.125
            do_b = _keep(mine, do_ref[...]).astype(BF16)
            cm = c_ref[:, LANES * half:LANES * (half + 1)]

            def block(kb, carry, dd, qv=qv, qs=qs, do_b=do_b, cm=cm):
                dq_acc, pc = carry
                ks = pl.ds(pl.multiple_of(kb * tk, tk), tk)
                kk = k_ref[ks, :]
                strict = None if dd is None else _sb_strict(tq, tk, dd)
                z, lom, suf = _sb_scores(qs, kk, msf, strict)
                c = jnp.sum(jnp.where(lane == kb, cm, 0.0), axis=1, keepdims=True)
                a = jnp.exp(z + lom + (suf + c))
                if strict is not None:
                    a = jnp.where(strict, a, 0.0)
                g = _dot_nt(do_b, v_ref[ks, :]) * a
                p = pc + _dot(g.astype(BF16), mpf)
                omb = jnp.exp(lom)
                dz = (g * omb - (1.0 - omb) * p) * 0.125
                if strict is not None:
                    dz = jnp.where(strict, dz, 0.0)
                dz = dz.astype(BF16)
                dv_ref[ks, :] += _dot_tn(a.astype(BF16), do_b)
                dk_ref[ks, :] += _dot_tn(dz, qv)
                return dq_acc + _dot(dz, kk), pc + jnp.sum(g, axis=1, keepdims=True)

            seen = jnp.logical_and(jnp.max(cm, axis=0, keepdims=True) > -SB_SKIP, lane1 < i * ratio)
            first = i * ratio - jnp.sum(seen.astype(jnp.int32))
            carry = (jnp.zeros((tq, LANES), F32), jnp.zeros((tq, 1), F32))
            carry = lax.fori_loop(first, i * ratio, lambda kb, c, block=block: block(kb, c, None), carry)
            for dd in range(ratio):
                carry = block(i * ratio + dd, carry, dd)
            if half == 0:
                dq_ref[...] = _keep(mine, carry[0])
            else:
                dq_ref[...] += _keep(mine, carry[0])

    qblk = lambda n: pl.BlockSpec((tq, n), lambda g, i: (i, g))
    full = pl.BlockSpec((s, LANES), lambda g, i: (0, g))
    msk = pl.BlockSpec((tk, tk), lambda g, i: (0, 0))
    return pl.pallas_call(
        kern, name="sb_bwd", grid=(SB_HEADS // 2, s // tq),
        in_specs=[qblk(LANES), full, full, qblk(LANES), qblk(2 * LANES), msk, msk],
        out_specs=[qblk(LANES), full, full],
        out_shape=[_sds((s, SB_WIDTH), F32)] * 3, compiler_params=_params(2),
    )(q, k, v, do, cmat, msuf, mpre)


def _place():
    return lax.axis_index("x"), lax.axis_index("y"), lax.axis_index("c")


def _other_chips(x, y):
    return [(1 - x, y), (x, 1 - y), (1 - x, 1 - y)]


HBM_SPEC = pl.BlockSpec(memory_space=pl.ANY)


def _allgather_shards(wp):
    rows, width = wp.shape
    half = rows // 2

    def body(x_ref, out_ref, send_sems, recv_sems, local_sem):
        x, y, c = _place()
        me, sibling = (x, y, c), (x, y, 1 - c)
        chips = _other_chips(x, y)

        def blk(ref, px, py, pc, per_chip):
            return ref.at[pl.ds(pl.multiple_of((2 * px + py) * per_chip + pc * half, 16), half), :]

        def copy(k, block, to, src=None):
            return pltpu.make_async_remote_copy(
                src_ref=blk(out_ref, *block, rows) if src is None else src, dst_ref=blk(out_ref, *block, rows),
                send_sem=send_sems.at[k], recv_sem=recv_sems.at[k], device_id=to, device_id_type=MESH)

        mine = pltpu.make_async_copy(x_ref, out_ref.at[pl.ds(pl.multiple_of((2 * x + y) * rows, 16), rows), :], local_sem)
        mine.start()
        my_half = blk(x_ref, 0, 0, c, 0)
        first = [copy(j, me, (*chip, c), src=my_half) for j, chip in enumerate(chips)]
        for cp in first:
            cp.start()
        passed = [copy(3 + j, (*chip, c), sibling) for j, chip in enumerate(chips)]
        for j, chip in enumerate(chips):
            copy(j, (*chip, c), me).wait_recv()
            passed[j].start()
        for j, chip in enumerate(chips):
            copy(3 + j, (*chip, 1 - c), me).wait_recv()
        for cp in first + passed:
            cp.wait_send()
        mine.wait()

    return pl.pallas_call(
        body, name="allgather_w", out_shape=_sds((N_SHARD * rows, width), wp.dtype), in_specs=[HBM_SPEC], out_specs=HBM_SPEC,
        scratch_shapes=[pltpu.SemaphoreType.DMA((6,)), pltpu.SemaphoreType.DMA((6,)), pltpu.SemaphoreType.DMA],
    )(wp)


def _sibling_swap(name, v):
    def body(v_ref, got_ref, send_sem, recv_sem):
        x, y, c = _place()
        cp = pltpu.make_async_remote_copy(src_ref=v_ref, dst_ref=got_ref, send_sem=send_sem, recv_sem=recv_sem,
                                          device_id=(x, y, 1 - c), device_id_type=MESH)
        cp.start()
        cp.wait()

    return pl.pallas_call(
        body, name=name, out_shape=_sds(v.shape, v.dtype), in_specs=[HBM_SPEC], out_specs=HBM_SPEC,
        scratch_shapes=[pltpu.SemaphoreType.DMA, pltpu.SemaphoreType.DMA],
    )(v)


def _chip_scatter(p):
    def body(p_ref, out_ref, send_sems, recv_sems):
        x, y, c = _place()
        chips = _other_chips(x, y)

        def copy(j, px, py):
            return pltpu.make_async_remote_copy(
                src_ref=p_ref.at[2 * px + py], dst_ref=out_ref.at[j], send_sem=send_sems.at[j], recv_sem=recv_sems.at[j],
                device_id=(px, py, c), device_id_type=MESH)

        sends = [copy(j, px, py) for j, (px, py) in enumerate(chips)]
        for cp in sends:
            cp.start()
        for cp in sends:
            cp.wait()

    return pl.pallas_call(
        body, name="chip_scatter", out_shape=_sds((N_SHARD - 1,) + p.shape[1:], p.dtype), in_specs=[HBM_SPEC], out_specs=HBM_SPEC,
        scratch_shapes=[pltpu.SemaphoreType.DMA((3,)), pltpu.SemaphoreType.DMA((3,))],
    )(p)


def _allreduce_small(v):
    m_per, n = v.shape

    def body(x_ref, tot_ref, all_ref, send_sems, recv_sems, local_sem):
        x, y, c = _place()
        me, sibling = (x, y, c), (x, y, 1 - c)
        chips = _other_chips(x, y)

        def rows(px, py, pc):
            return all_ref.at[pl.ds(pl.multiple_of((4 * px + 2 * py + pc) * m_per, 8), m_per), :]

        def copy(k, block, to, src=None):
            return pltpu.make_async_remote_copy(
                src_ref=rows(*block) if src is None else src, dst_ref=rows(*block), send_sem=send_sems.at[k],
                recv_sem=recv_sems.at[k], device_id=to, device_id_type=MESH)

        mine = pltpu.make_async_copy(x_ref, rows(*me), local_sem)
        mine.start()
        first = [copy(0, me, sibling, src=x_ref)] + [copy(1 + j, me, (*chip, c), src=x_ref) for j, chip in enumerate(chips)]
        for cp in first:
            cp.start()
        passed = [copy(4 + j, (*chip, c), sibling) for j, chip in enumerate(chips)]
        for j, chip in enumerate(chips):
            copy(1 + j, (*chip, c), me).wait_recv()
            passed[j].start()
        copy(0, sibling, me).wait_recv()
        for j, chip in enumerate(chips):
            copy(4 + j, (*chip, 1 - c), me).wait_recv()
        for cp in first + passed:
            cp.wait_send()
        mine.wait()
        tot = all_ref[0:m_per, :]
        for dev in range(1, 8):
            tot = tot + all_ref[dev * m_per:(dev + 1) * m_per, :]
        tot_ref[...] = tot

    vmem = pl.BlockSpec(memory_space=pltpu.VMEM)
    return pl.pallas_call(
        body, name="allreduce_small", out_shape=_sds((m_per, n), F32), in_specs=[vmem], out_specs=vmem,
        scratch_shapes=[pltpu.VMEM((8 * m_per, n), F32), pltpu.SemaphoreType.DMA((7,)), pltpu.SemaphoreType.DMA((7,)),
                        pltpu.SemaphoreType.DMA],
    )(v)


def _add_halves(name, terms):
    rows, width = terms[0].shape
    tr = max(t for t in range(16, ADD_ROWS + 1, 16) if rows % t == 0)

    def kern(*refs):
        acc = refs[0][...].astype(F32)
        for r in refs[1:-1]:
            acc = acc + r[...].astype(F32)
        refs[-1][...] = acc

    spec = pl.BlockSpec((tr, width), lambda i: (i, 0))
    return pl.pallas_call(kern, name=name, grid=(rows // tr,), in_specs=[spec] * len(terms), out_specs=spec,
                          out_shape=_sds((rows, width), F32), compiler_params=_params(1))(*terms)


def _adamw(name, w, g, m, v):
    rows, width = w.shape
    tr = rows // 4 if rows % 32 == 0 else rows

    def kern(w_ref, g_ref, m_ref, v_ref, d_ref, mo_ref, vo_ref):
        g_v = g_ref[...]
        m_new = ADAM_B1 * m_ref[...] + (1.0 - ADAM_B1) * g_v
        v_new = ADAM_B2 * v_ref[...] + (1.0 - ADAM_B2) * (g_v * g_v)
        m_hat = m_new / (1.0 - ADAM_B1 ** ADAM_STEP)
        v_hat = v_new / (1.0 - ADAM_B2 ** ADAM_STEP)
        d_ref[...] = -ADAM_LR * (m_hat / (jnp.sqrt(v_hat) + ADAM_EPS) + ADAM_WD * w_ref[...])
        mo_ref[...] = m_new
        vo_ref[...] = v_new

    spec = pl.BlockSpec((tr, width), lambda i: (i, 0))
    return pl.pallas_call(kern, name=name, grid=(rows // tr,), in_specs=[spec] * 4, out_specs=[spec] * 3,
                          out_shape=[_sds((rows, width), F32)] * 3, compiler_params=_params(1))(w, g, m, v)


SHARDED = (("w_in", D_MODEL, IN_WIDTH, 1), ("w_uq", Q_RANK, MLA_HEADS * MLA_QK, 1),
           ("w_ukv", KV_RANK, MLA_HEADS * (MLA_NOPE + MLA_V), 1), ("w_o", D_MODEL, D_MODEL, 0),
           ("w_gate", D_MODEL, D_FF, 1), ("w_up", D_MODEL, D_FF, 1), ("w_down", D_FF, D_MODEL, 0))
SMALL = (("norm_mix", D_MODEL), ("q_latent_norm", Q_RANK), ("kv_latent_norm", KV_RANK), ("out_norm_mla", MLA_WIDTH),
         ("out_norm_sb", SB_WIDTH), ("norm_ffn", D_MODEL), ("norm_final", D_MODEL))


def _pack_local(blocks):
    packed = jnp.concatenate([b.reshape(-1, PACK_W) for b in blocks], axis=0)
    return jnp.pad(packed, ((0, PACK_ROWS - packed.shape[0]), (0, 0)))


def _unpack_full(packed):
    out, off = {}, 0
    for name, r, c, axis in SHARDED:
        n = r * c // N_SHARD // PACK_W
        piece = packed[:, off:off + n, :]
        off += n
        if axis == 1:
            out[name] = piece.reshape(N_SHARD, r, c // N_SHARD).transpose(1, 0, 2).reshape(r, c)
        else:
            out[name] = piece.reshape(r, c)
    return out


def _pack_grads(grads):
    parts = []
    for name, r, c, axis in SHARDED:
        g = grads[name]
        if axis == 1:
            g = g.reshape(r, N_SHARD, c // N_SHARD).transpose(1, 0, 2)
        parts.append(g.reshape(N_SHARD, -1, PACK_W))
    packed = jnp.concatenate(parts, axis=1)
    return jnp.pad(packed, ((0, 0), (0, PACK_ROWS - packed.shape[1]), (0, 0)))


def _unpack_shard(packed):
    out, off = {}, 0
    for name, r, c, axis in SHARDED:
        n = r * c // N_SHARD // PACK_W
        shape = (r, c // N_SHARD) if axis == 1 else (r // N_SHARD, c)
        out[name] = packed[off:off + n, :].reshape(shape)
        off += n
    return out


def _rot_cols(w):
    hh = MLA_ROPE // 2
    return jnp.concatenate([-w[..., hh:], w[..., :hh]], axis=-1)


def _rot_cols_t(g):
    hh = MLA_ROPE // 2
    return jnp.concatenate([g[..., hh:], -g[..., :hh]], axis=-1)


def _prepare_weights(full, small):
    w_in = full["w_in"]
    s0, s1, s2 = Q_RANK, Q_RANK + KV_RANK, Q_RANK + KV_RANK + MLA_ROPE
    uq = full["w_uq"].reshape(Q_RANK, MLA_HEADS, MLA_QK)
    ukv = full["w_ukv"].reshape(KV_RANK, MLA_HEADS, MLA_NOPE + MLA_V)
    w_kr = w_in[:, s1:s2]
    per_tile = ROPE_TILE // MLA_ROPE
    w = {
        "w_cq": w_in[:, :s0], "w_ckv": w_in[:, s0:s1],
        "w_kr4": jnp.tile(w_kr, (1, per_tile)), "w_kr4r": jnp.tile(_rot_cols(w_kr), (1, per_tile)),
        "w_kr8": jnp.tile(w_kr, (1, MLA_HEADS)), "w_kr8r": jnp.tile(_rot_cols(w_kr), (1, MLA_HEADS)),
        "w_sbq": w_in[:, s2:s2 + SB_WIDTH], "w_sbk": w_in[:, s2 + SB_WIDTH:s2 + 2 * SB_WIDTH], "w_sbv": w_in[:, s2 + 2 * SB_WIDTH:],
        "w_qn": uq[..., :MLA_NOPE].reshape(Q_RANK, -1), "w_qr": uq[..., MLA_NOPE:].reshape(Q_RANK, -1),
        "w_qrr": _rot_cols(uq[..., MLA_NOPE:]).reshape(Q_RANK, -1),
        "w_kn": ukv[..., :MLA_NOPE].reshape(KV_RANK, -1), "w_v": ukv[..., MLA_NOPE:].reshape(KV_RANK, -1),
        "w_oa": full["w_o"][:MLA_WIDTH], "w_ob": full["w_o"][MLA_WIDTH:],
        "w_gate": full["w_gate"], "w_up": full["w_up"], "w_down": full["w_down"],
    }
    for name in list(w):
        w[name + "_t"] = w[name].T
    w.update(g_mix=small["norm_mix"], g_q=small["q_latent_norm"], g_kv=small["kv_latent_norm"], g_a=small["out_norm_mla"],
             g_b=small["out_norm_sb"], g_f=small["norm_ffn"], g_n=small["norm_final"])
    return w


def _rope_tables(positions):
    inv_freq = ROPE_THETA ** (-jnp.arange(0, MLA_ROPE, 2, dtype=F32) / MLA_ROPE)
    ang = positions.astype(F32)[:, None] * inv_freq[None, :]
    cos, sin = jnp.cos(ang), jnp.sin(ang)
    return {"cos": jnp.tile(jnp.concatenate([cos, cos], axis=1), (1, MLA_HEADS)),
            "sin": jnp.tile(jnp.concatenate([sin, sin], axis=1), (1, MLA_HEADS))}


def _by_head(g_wide, g_narrow, wide, narrow):
    r = g_wide.shape[0]
    return jnp.concatenate([g_wide.reshape(r, MLA_HEADS, wide), g_narrow.reshape(r, MLA_HEADS, narrow)], axis=-1).reshape(r, -1)


def kernel(x, positions, norm_mix, w_in, q_latent_norm, w_uq, kv_latent_norm, w_ukv, out_norm_mla, out_norm_sb, w_o, norm_ffn, w_gate, w_up, w_down, norm_final, loss_target, m_norm_mix, m_w_in, m_q_latent_norm, m_w_uq, m_kv_latent_norm, m_w_ukv, m_out_norm_mla, m_out_norm_sb, m_w_o, m_norm_ffn, m_w_gate, m_w_up, m_w_down, m_norm_final, v_norm_mix, v_w_in, v_q_latent_norm, v_w_uq, v_kv_latent_norm, v_w_ukv, v_out_norm_mla, v_out_norm_sb, v_w_o, v_norm_ffn, v_w_gate, v_w_up, v_w_down, v_norm_final):
    given = dict(norm_mix=norm_mix, w_in=w_in, q_latent_norm=q_latent_norm, w_uq=w_uq, kv_latent_norm=kv_latent_norm, w_ukv=w_ukv,
                 out_norm_mla=out_norm_mla, out_norm_sb=out_norm_sb, w_o=w_o, norm_ffn=norm_ffn, w_gate=w_gate, w_up=w_up,
                 w_down=w_down, norm_final=norm_final)
    mom_m = dict(norm_mix=m_norm_mix, w_in=m_w_in, q_latent_norm=m_q_latent_norm, w_uq=m_w_uq, kv_latent_norm=m_kv_latent_norm,
                 w_ukv=m_w_ukv, out_norm_mla=m_out_norm_mla, out_norm_sb=m_out_norm_sb, w_o=m_w_o, norm_ffn=m_norm_ffn,
                 w_gate=m_w_gate, w_up=m_w_up, w_down=m_w_down, norm_final=m_norm_final)
    mom_v = dict(norm_mix=v_norm_mix, w_in=v_w_in, q_latent_norm=v_q_latent_norm, w_uq=v_w_uq, kv_latent_norm=v_kv_latent_norm,
                 w_ukv=v_w_ukv, out_norm_mla=v_out_norm_mla, out_norm_sb=v_out_norm_sb, w_o=v_w_o, norm_ffn=v_norm_ffn,
                 w_gate=v_w_gate, w_up=v_w_up, w_down=v_w_down, norm_final=v_norm_final)
    xs = x[0]
    tgt = loss_target[0]
    s = xs.shape[0]
    c_idx = lax.axis_index("c")

    shard2d = {name: given[name].reshape(given[name].shape[-2:]) for name, *_ in SHARDED}
    packed_w = _pack_local([shard2d[name].astype(BF16) for name, *_ in SHARDED])
    gathered = _allgather_shards(packed_w).reshape(N_SHARD, packed_w.shape[0], PACK_W)
    small = {name: given[name].reshape(1, n) for name, n in SMALL}
    w = _prepare_weights(_unpack_full(gathered), small)
    tabs = _rope_tables(positions[0])
    msuf, mpre = _sb_masks(min(SB_TK, s))

    u, cq, ckv, cqn, ckvn, qn, qr, kn, vm, kr, sq, sk, sv = _fwd_a(xs, tabs, w)
    o_mla, lse = _mla_fwd(qn, qr, kn, kr, vm)
    o_sb, cmat = _sb_fwd(sq, sk, sv, msuf)
    merged, h1, f, gate, up, act = _fwd_b1(xs, o_mla, o_sb, w)
    dh2, loss_part, dg_n = _fwd_b2(h1, act, tgt, w)

    dgate, dup = _bwd_b1(dh2, gate, up, w)
    dh1, do_mla, do_sb, dg_f, dg_a, dg_b = _bwd_b2(dgate, dup, h1, dh2, o_mla, o_sb, w)
    dqn, dqr, dkn, dkr, dvm = _mla_bwd(qn, qr, kn, kr, vm, o_mla, do_mla, lse)
    dsq, dsk, dsv = _sb_bwd(sq, sk, sv, do_sb, cmat, msuf, mpre)
    dx, a1, a2, dcq, dckv, dkrc, dkrs, dg_q, dg_kv, dg_mix = _bwd_a(xs, dh1, cq, ckv, dqn, dqr, dkn, dvm, dkr, dsq, dsk, dsv, tabs, w)

    g_cq, g_ckv, g_krc, g_krs, g_sq, g_sk, g_sv = _tn_multi("dw_in", u, [dcq, dckv, dkrc, dkrs, dsq, dsk, dsv])
    g_qn, g_qr1, g_qr2 = _tn_multi("dw_uq", cqn, [dqn, a1, a2])
    g_kn, g_v = _tn_multi("dw_ukv", ckvn, [dkn, dvm])
    slots = lambda g: g.reshape(g.shape[0], MLA_HEADS, MLA_ROPE)
    g_kr = jnp.sum(slots(g_krc), axis=1) + _rot_cols_t(jnp.sum(slots(g_krs), axis=1))
    g_qr = (slots(g_qr1) + _rot_cols_t(slots(g_qr2))).reshape(Q_RANK, -1)
    grads = {
        "w_in": jnp.concatenate([g_cq, g_ckv, g_kr, g_sq, g_sk, g_sv], axis=1),
        "w_uq": _by_head(g_qn, g_qr, MLA_NOPE, MLA_ROPE),
        "w_ukv": _by_head(g_kn * MLA_DK_SCALE, g_v, MLA_NOPE, MLA_V),
        "w_o": _tn_matmul("dw_o", merged, dh1),
        "w_gate": _tn_matmul("dw_gate", f, dgate),
        "w_up": _tn_matmul("dw_up", f, dup),
        "w_down": _tn_matmul("dw_down", act, dh2),
    }

    packed_g = _pack_grads(grads)
    half = packed_g.shape[1] // 2
    keep = lax.dynamic_slice_in_dim(packed_g, c_idx * half, half, axis=1)
    give = lax.dynamic_slice_in_dim(packed_g, (1 - c_idx) * half, half, axis=1)
    got = _sibling_swap("swap_halves", give.astype(BF16))
    chip_sum = _add_halves("add_sibling", [keep.reshape(-1, PACK_W), got.reshape(-1, PACK_W)]).reshape(keep.shape)
    others = _chip_scatter(chip_sum.astype(BF16))
    own = lax.dynamic_index_in_dim(chip_sum, 2 * lax.axis_index("x") + lax.axis_index("y"), axis=0, keepdims=False)
    mine = _add_halves("add_chips", [own] + [others[j] for j in range(N_SHARD - 1)])
    theirs = _sibling_swap("swap_result", mine)
    lo = jnp.where(c_idx == 0, mine, theirs)
    hi = jnp.where(c_idx == 0, theirs, mine)
    g_shard = _unpack_shard(jnp.concatenate([lo, hi], axis=0))

    small_parts = jnp.concatenate([dg_mix, dg_q, dg_kv, dg_a, dg_b, dg_f, dg_n], axis=1)
    small_g = _allreduce_small(jnp.broadcast_to(small_parts, (8, small_parts.shape[1])))[0:1]
    loss = lax.psum(loss_part[0, 0], ("x", "y", "c"))

    g_out, d_out, m_out, v_out = {}, {}, {}, {}
    for name, *_ in SHARDED:
        shape = given[name].shape
        d, mn, vn = _adamw("adamw_" + name, shard2d[name], g_shard[name], mom_m[name].reshape(shard2d[name].shape),
                           mom_v[name].reshape(shard2d[name].shape))
        g_out[name], d_out[name], m_out[name], v_out[name] = (t.reshape(shape) for t in (g_shard[name], d, mn, vn))
    cat = lambda src: jnp.concatenate([src[name].reshape(1, n) for name, n in SMALL], axis=1)
    d, mn, vn = _adamw("adamw_small", cat(given), small_g, cat(mom_m), cat(mom_v))
    off = 0
    for name, n in SMALL:
        shape = given[name].shape
        g_out[name], d_out[name], m_out[name], v_out[name] = (t[:, off:off + n].reshape(shape) for t in (small_g, d, mn, vn))
        off += n

    order = ["norm_mix", "w_in", "q_latent_norm", "w_uq", "kv_latent_norm", "w_ukv", "out_norm_mla", "out_norm_sb", "w_o",
             "norm_ffn", "w_gate", "w_up", "w_down", "norm_final"]
    return (loss, dx[None], *[g_out[n] for n in order], *[d_out[n] for n in order], *[m_out[n] for n in order],
            *[v_out[n] for n in order])
```

```python
import functools
import math

import jax
import jax.numpy as jnp
from jax import lax
from jax.experimental import pallas as pl
from jax.experimental.pallas import tpu as pltpu

F32 = jnp.float32
BF16 = jnp.bfloat16
MESH = pl.DeviceIdType.MESH

D_MODEL = 1024
EPS = 1e-6
MLA_HEADS = 8
MLA_NOPE = 64
MLA_ROPE = 32
MLA_V = 64
MLA_QK = MLA_NOPE + MLA_ROPE
Q_RANK = 256
KV_RANK = 128
ROPE_THETA = 10000.0
SB_HEADS = 8
SB_DIM = 64
MLA_WIDTH = MLA_HEADS * MLA_V
SB_WIDTH = SB_HEADS * SB_DIM
D_FF = 2816
IN_WIDTH = Q_RANK + KV_RANK + MLA_ROPE + 3 * SB_WIDTH

ADAM_LR = 0.001
ADAM_B1 = 0.9
ADAM_B2 = 0.999
ADAM_EPS = 1e-08
ADAM_WD = 0.01
ADAM_STEP = 10

N_SHARD = 4
LANES = 128
ROPE_TILE = LANES
VMEM_LIMIT = 56 * 1024 * 1024
TN_ACC_BYTES = 6 * 1024 * 1024 + 512 * 1024
NEG = -1e30
MLA_SCALE = 1.0 / math.sqrt(MLA_QK)
MLA_DK_SCALE = math.log(2.0)
MLA_QSCALE = MLA_SCALE * math.log2(math.e)
SB_SKIP = 110.0

ROW_TILE = 512
ROW_TILE_ELEMENTWISE = 256
MLA_TQ = 1024
SB_TQ = 512
MLA_TK = 1024
MLA_BWD_TK = 512
MLA_DIAG_TK = 512
SB_TK = 256
TN_TS = 2048


def _dot(a, b):
    return jnp.dot(a, b, preferred_element_type=F32)


def _dot_nt(a, b):
    return lax.dot_general(a, b, (((1,), (1,)), ((), ())), preferred_element_type=F32)


def _dot_tn(a, b):
    return lax.dot_general(a, b, (((0,), (0,)), ((), ())), preferred_element_type=F32)


def _params(n_grid, vmem=VMEM_LIMIT):
    return pltpu.CompilerParams(dimension_semantics=("arbitrary",) * n_grid, vmem_limit_bytes=vmem)


def _rms(x):
    r = lax.rsqrt(jnp.mean(x * x, axis=-1, keepdims=True) + EPS)
    return x * r, r


def _rms_bwd(n, r, g, dy):
    dn = dy * g
    dx = r * (dn - n * jnp.mean(dn * n, axis=-1, keepdims=True))
    return dx, jnp.sum(dy * n, axis=0, keepdims=True)


def _accumulate(ref, val, step):
    @pl.when(step == 0)
    def _():
        ref[...] = val

    @pl.when(step != 0)
    def _():
        ref[...] += val


def _rowwise(name, body, rows, consts, row_out, acc_out, tm):
    n_rows = rows[0].shape[0]
    tm = min(tm, n_rows)
    nr, nc, no = len(rows), len(consts), len(row_out)

    def kern(*refs):
        body(refs[:nr], refs[nr:nr + nc], refs[nr + nc:nr + nc + no], refs[nr + nc + no:], pl.program_id(0))

    in_specs = [pl.BlockSpec((tm, a.shape[1]), lambda i: (i, 0)) for a in rows]
    in_specs += [pl.BlockSpec(a.shape, lambda i: (0, 0), pipeline_mode=pl.Buffered(1)) for a in consts]
    out_specs = [pl.BlockSpec((tm, s.shape[1]), lambda i: (i, 0)) for s in row_out]
    out_specs += [pl.BlockSpec(s.shape, lambda i: (0, 0)) for s in acc_out]
    return pl.pallas_call(
        kern, name=name, grid=(n_rows // tm,), in_specs=in_specs, out_specs=out_specs,
        out_shape=list(row_out) + list(acc_out), compiler_params=_params(1),
    )(*rows, *consts)


def _sds(shape, dtype):
    return jax.ShapeDtypeStruct(shape, dtype)


def _fwd_a(x, tabs, w):
    s = x.shape[0]

    def body(r, c, o, a, step):
        x_ref, cos_ref, sin_ref = r
        gmix, wcq, wckv, wkr, wkrr, wsq, wsk, wsv, gq, wqn, wqr, wqrr, gkv, wkn, wv = c
        u_o, cq_o, ckv_o, cqn_o, ckvn_o, qn_o, qr_o, kn_o, v_o, kr_o, sq_o, sk_o, sv_o = o
        cos, sin = cos_ref[...], sin_ref[...]
        n, _ = _rms(x_ref[...])
        u = (n * gmix[...]).astype(BF16)
        u_o[...] = u
        cq = _dot(u, wcq[...])
        ckv = _dot(u, wckv[...])
        kr_o[...] = (_dot(u, wkr[...]) * cos[:, :ROPE_TILE] + _dot(u, wkrr[...]) * sin[:, :ROPE_TILE]).astype(BF16)
        sq_o[...] = _dot(u, wsq[...]).astype(BF16)
        sk_o[...] = _dot(u, wsk[...]).astype(BF16)
        sv_o[...] = _dot(u, wsv[...]).astype(BF16)
        cq_o[...] = cq
        ckv_o[...] = ckv
        nq, _ = _rms(cq)
        cqn = (nq * gq[...]).astype(BF16)
        cqn_o[...] = cqn
        qn_o[...] = (_dot(cqn, wqn[...]) * MLA_QSCALE).astype(BF16)
        qr_o[...] = ((_dot(cqn, wqr[...]) * cos + _dot(cqn, wqrr[...]) * sin) * MLA_QSCALE).astype(BF16)
        nkv, _ = _rms(ckv)
        ckvn = (nkv * gkv[...]).astype(BF16)
        ckvn_o[...] = ckvn
        kn_o[...] = _dot(ckvn, wkn[...]).astype(BF16)
        v_o[...] = _dot(ckvn, wv[...]).astype(BF16)

    outs = [
        _sds((s, D_MODEL), BF16), _sds((s, Q_RANK), F32), _sds((s, KV_RANK), F32), _sds((s, Q_RANK), BF16),
        _sds((s, KV_RANK), BF16), _sds((s, MLA_HEADS * MLA_NOPE), BF16), _sds((s, MLA_HEADS * MLA_ROPE), BF16),
        _sds((s, MLA_HEADS * MLA_NOPE), BF16), _sds((s, MLA_WIDTH), BF16), _sds((s, ROPE_TILE), BF16),
        _sds((s, SB_WIDTH), BF16), _sds((s, SB_WIDTH), BF16), _sds((s, SB_WIDTH), BF16),
    ]
    consts = [w["g_mix"], w["w_cq"], w["w_ckv"], w["w_kr4"], w["w_kr4r"], w["w_sbq"], w["w_sbk"], w["w_sbv"], w["g_q"],
              w["w_qn"], w["w_qr"], w["w_qrr"], w["g_kv"], w["w_kn"], w["w_v"]]
    return _rowwise("fwd_a", body, [x, tabs["cos"], tabs["sin"]], consts, outs, [], ROW_TILE)


def _fwd_b1(x, o_mla, o_sb, w):
    s = x.shape[0]

    def body(r, c, o, a, step):
        x_ref, oa_ref, ob_ref = r
        ga, gb, woa, wob, gf, wg, wu = c
        mg_o, h1_o, f_o, gate_o, up_o, act_o = o
        na, _ = _rms(oa_ref[...])
        nb, _ = _rms(ob_ref[...])
        ma = (na * ga[...]).astype(BF16)
        mb = (nb * gb[...]).astype(BF16)
        mg_o[:, :MLA_WIDTH] = ma
        mg_o[:, MLA_WIDTH:] = mb
        h1 = x_ref[...] + _dot(ma, woa[...]) + _dot(mb, wob[...])
        h1_o[...] = h1
        nf, _ = _rms(h1)
        f = (nf * gf[...]).astype(BF16)
        f_o[...] = f
        gate = _dot(f, wg[...])
        up = _dot(f, wu[...])
        gate_o[...] = gate.astype(BF16)
        up_o[...] = up.astype(BF16)
        act_o[...] = (gate * (1.0 / (1.0 + jnp.exp(-gate))) * up).astype(BF16)

    outs = [_sds((s, D_MODEL), BF16), _sds((s, D_MODEL), F32), _sds((s, D_MODEL), BF16), _sds((s, D_FF), BF16),
            _sds((s, D_FF), BF16), _sds((s, D_FF), BF16)]
    consts = [w["g_a"], w["g_b"], w["w_oa"], w["w_ob"], w["g_f"], w["w_gate"], w["w_up"]]
    return _rowwise("fwd_b1", body, [x, o_mla, o_sb], consts, outs, [], ROW_TILE)


def _fwd_b2(h1, act, tgt, w):
    s = h1.shape[0]

    def body(r, c, o, a, step):
        h1_ref, act_ref, t_ref = r
        wd, gn = c
        (dh2_o,) = o
        loss_o, dgn_o = a
        h2 = h1_ref[...] + _dot(act_ref[...], wd[...])
        n2, r2 = _rms(h2)
        err = n2 * gn[...] - t_ref[...]
        part = jnp.sum(jnp.sum(err * err, axis=1, keepdims=True), axis=0, keepdims=True) * (0.5 / D_MODEL)
        _accumulate(loss_o, jnp.broadcast_to(part, (1, LANES)), step)
        dh2, dgn = _rms_bwd(n2, r2, gn[...], err * (1.0 / D_MODEL))
        dh2_o[...] = dh2
        _accumulate(dgn_o, dgn, step)

    return _rowwise("fwd_b2", body, [h1, act, tgt], [w["w_down"], w["g_n"]], [_sds((s, D_MODEL), F32)],
                    [_sds((1, LANES), F32), _sds((1, D_MODEL), F32)], ROW_TILE)


def _bwd_b1(dh2, gate, up, w):
    s = dh2.shape[0]

    def body(r, c, o, a, step):
        dh2_ref, gate_ref, up_ref = r
        (wdt,) = c
        dgate_o, dup_o = o
        dact = _dot(dh2_ref[...].astype(BF16), wdt[...])
        gate = gate_ref[...].astype(F32)
        sig = 1.0 / (1.0 + jnp.exp(-gate))
        dup_o[...] = (dact * (gate * sig)).astype(BF16)
        dgate_o[...] = (dact * up_ref[...].astype(F32) * (sig * (1.0 + gate * (1.0 - sig)))).astype(BF16)

    return _rowwise("bwd_b1", body, [dh2, gate, up], [w["w_down_t"]], [_sds((s, D_FF), BF16), _sds((s, D_FF), BF16)],
                    [], ROW_TILE_ELEMENTWISE)


def _bwd_b2(dgate, dup, h1, dh2, o_mla, o_sb, w):
    s = h1.shape[0]

    def body(r, c, o, a, step):
        dgate_ref, dup_ref, h1_ref, dh2_ref, oa_ref, ob_ref = r
        wgt, wut, gf, woat, wobt, ga, gb = c
        dh1_o, doa_o, dob_o = o
        dgf_o, dga_o, dgb_o = a
        df = _dot(dgate_ref[...], wgt[...]) + _dot(dup_ref[...], wut[...])
        nf, rf = _rms(h1_ref[...])
        dres, dgf = _rms_bwd(nf, rf, gf[...], df)
        dh1 = dh2_ref[...] + dres
        dh1_o[...] = dh1
        dh1b = dh1.astype(BF16)
        na, ra = _rms(oa_ref[...])
        doa, dga = _rms_bwd(na, ra, ga[...], _dot(dh1b, woat[...]))
        nb, rb = _rms(ob_ref[...])
        dob, dgb = _rms_bwd(nb, rb, gb[...], _dot(dh1b, wobt[...]))
        doa_o[...] = doa
        dob_o[...] = dob
        _accumulate(dgf_o, dgf, step)
        _accumulate(dga_o, dga, step)
        _accumulate(dgb_o, dgb, step)

    consts = [w["w_gate_t"], w["w_up_t"], w["g_f"], w["w_oa_t"], w["w_ob_t"], w["g_a"], w["g_b"]]
    outs = [_sds((s, D_MODEL), F32), _sds((s, MLA_WIDTH), F32), _sds((s, SB_WIDTH), F32)]
    accs = [_sds((1, D_MODEL), F32), _sds((1, MLA_WIDTH), F32), _sds((1, SB_WIDTH), F32)]
    return _rowwise("bwd_b2", body, [dgate, dup, h1, dh2, o_mla, o_sb], consts, outs, accs, ROW_TILE)


def _fold_pairs(t):
    return jnp.concatenate([t[:, :LANES] + t[:, LANES:2 * LANES], t[:, 2 * LANES:3 * LANES] + t[:, 3 * LANES:]], axis=1)


def _bwd_a(x, dh1, cq, ckv, dqn, dqr, dkn, dvm, dkr, dsq, dsk, dsv, tabs, w):
    s = x.shape[0]

    def body(r, c, o, a, step):
        x_ref, dh1_ref, cq_ref, ckv_ref, dqn_ref, dqr_ref, dkn_ref, dvm_ref, dkr_ref, dsq_ref, dsk_ref, dsv_ref, cos_ref, sin_ref = r
        wqnt, wqrt, wqrrt, gq, wknt, wvt, gkv, wcqt, wckvt, wkrt, wkrrt, wsqt, wskt, wsvt, gmix = c
        dx_o, a1_o, a2_o, dcq_o, dckv_o, dkrc_o, dkrs_o = o
        dgq_o, dgkv_o, dgmix_o = a
        cos, sin = cos_ref[...], sin_ref[...]
        dqr = _fold_pairs(dqr_ref[...])
        a1 = (dqr * cos).astype(BF16)
        a2 = (dqr * sin).astype(BF16)
        a1_o[...] = a1
        a2_o[...] = a2
        nq, rq = _rms(cq_ref[...])
        dcqn = _dot(dqn_ref[...].astype(BF16), wqnt[...]) + _dot(a1, wqrt[...]) + _dot(a2, wqrrt[...])
        dcq, dgq = _rms_bwd(nq, rq, gq[...], dcqn)
        nkv, rkv = _rms(ckv_ref[...])
        dckvn = _dot((dkn_ref[...] * MLA_DK_SCALE).astype(BF16), wknt[...]) + _dot(dvm_ref[...].astype(BF16), wvt[...])
        dckv, dgkv = _rms_bwd(nkv, rkv, gkv[...], dckvn)
        dkr = _fold_pairs(dkr_ref[...]) * MLA_DK_SCALE
        dcq_b = dcq.astype(BF16)
        dckv_b = dckv.astype(BF16)
        dkrc = (dkr * cos).astype(BF16)
        dkrs = (dkr * sin).astype(BF16)
        dcq_o[...] = dcq_b
        dckv_o[...] = dckv_b
        dkrc_o[...] = dkrc
        dkrs_o[...] = dkrs
        du = (_dot(dcq_b, wcqt[...]) + _dot(dckv_b, wckvt[...]) + _dot(dkrc, wkrt[...]) + _dot(dkrs, wkrrt[...])
              + _dot(dsq_ref[...].astype(BF16), wsqt[...]) + _dot(dsk_ref[...].astype(BF16), wskt[...])
              + _dot(dsv_ref[...].astype(BF16), wsvt[...]))
        nx, rx = _rms(x_ref[...])
        dres, dgmix = _rms_bwd(nx, rx, gmix[...], du)
        dx_o[...] = dh1_ref[...] + dres
        _accumulate(dgq_o, dgq, step)
        _accumulate(dgkv_o, dgkv, step)
        _accumulate(dgmix_o, dgmix, step)

    consts = [w["w_qn_t"], w["w_qr_t"], w["w_qrr_t"], w["g_q"], w["w_kn_t"], w["w_v_t"], w["g_kv"], w["w_cq_t"], w["w_ckv_t"],
              w["w_kr8_t"], w["w_kr8r_t"], w["w_sbq_t"], w["w_sbk_t"], w["w_sbv_t"], w["g_mix"]]
    rope_w = MLA_HEADS * MLA_ROPE
    outs = [_sds((s, D_MODEL), F32), _sds((s, rope_w), BF16), _sds((s, rope_w), BF16), _sds((s, Q_RANK), BF16),
            _sds((s, KV_RANK), BF16), _sds((s, rope_w), BF16), _sds((s, rope_w), BF16)]
    accs = [_sds((1, Q_RANK), F32), _sds((1, KV_RANK), F32), _sds((1, D_MODEL), F32)]
    rows = [x, dh1, cq, ckv, dqn, dqr, dkn, dvm, dkr, dsq, dsk, dsv, tabs["cos"], tabs["sin"]]
    return _rowwise("bwd_a", body, rows, consts, outs, accs, ROW_TILE)


def _tn_multi(name, x, ys):
    s, k = x.shape
    ts = min(TN_TS, s)
    n_y = len(ys)

    def kern(*refs):
        step = pl.program_id(0)
        xb = refs[0][...].astype(BF16)
        for j in range(n_y):
            _accumulate(refs[1 + n_y + j], _dot_tn(xb, refs[1 + j][...].astype(BF16)), step)

    return pl.pallas_call(
        kern, name=name, grid=(s // ts,),
        in_specs=[pl.BlockSpec((ts, k), lambda i: (i, 0))] + [pl.BlockSpec((ts, y.shape[1]), lambda i: (i, 0)) for y in ys],
        out_specs=[pl.BlockSpec((k, y.shape[1]), lambda i: (0, 0)) for y in ys],
        out_shape=[_sds((k, y.shape[1]), F32) for y in ys], compiler_params=_params(1),
    )(x, *ys)


def _tn_tile(k, n):
    if n % LANES or k * n * 4 <= TN_ACC_BYTES:
        return n
    units = n // LANES
    best = 1
    for d in range(1, units + 1):
        if units % d == 0 and k * d * LANES * 4 <= TN_ACC_BYTES:
            best = d
    return best * LANES


def _tn_matmul(name, x, y):
    s, k = x.shape
    n = y.shape[1]
    ts = min(TN_TS, s)
    tn = _tn_tile(k, n)

    def kern(x_ref, y_ref, o_ref):
        step = pl.program_id(1)
        _accumulate(o_ref, _dot_tn(x_ref[...].astype(BF16), y_ref[...].astype(BF16)), step)

    return pl.pallas_call(
        kern, name=name, grid=(n // tn, s // ts),
        in_specs=[pl.BlockSpec((ts, k), lambda j, i: (i, 0)), pl.BlockSpec((ts, tn), lambda j, i: (i, j))],
        out_specs=pl.BlockSpec((k, tn), lambda j, i: (0, j)), out_shape=_sds((k, n), F32), compiler_params=_params(2),
    )(x, y)


def _lanes(rows, lo, width):
    lane = lax.broadcasted_iota(jnp.int32, (rows, LANES), 1)
    return jnp.logical_and(lane >= lo, lane < lo + width)


def _keep(mask, t):
    return jnp.where(mask, t, jnp.zeros_like(t))


def _mla_qcat(qn_ref, qr_ref, rope_lo, half, rows):
    qn = _keep(_lanes(rows, MLA_NOPE * half, MLA_NOPE), qn_ref[...])
    qr = _keep(_lanes(rows, rope_lo, MLA_ROPE), qr_ref[...])
    return jnp.concatenate([qn, qr], axis=1)


def _diag_mask(rows, width, row0, col0):
    row = lax.broadcasted_iota(jnp.int32, (rows, width), 0)
    col = lax.broadcasted_iota(jnp.int32, (rows, width), 1)
    return col + (col0 - row0) <= row


def _mla_fwd(qn, qr, kn, kr, v, tq=MLA_TQ, tk=MLA_TK, td=MLA_TQ):
    s = qn.shape[0]
    tq, tk, td = min(tq, s), min(tk, s), min(td, s)
    ratio = tq // tk

    def kern(qn_ref, qr_ref, kn_ref, kr_ref, v_ref, o_ref, lse_ref):
        g = pl.program_id(0)
        i = pl.program_id(1)
        for half in range(2):
            qcat = _mla_qcat(qn_ref, qr_ref, MLA_ROPE * (2 * (g % 2) + half), half, tq)

            def block(k0, width, carry, row0, masked, qcat=qcat):
                m, l, acc = (c[row0:] for c in carry)
                ks = pl.ds(pl.multiple_of(k0, width), width)
                kcat = jnp.concatenate([kn_ref[ks, :], kr_ref[ks, :]], axis=1)
                sc = _dot_nt(qcat[row0:], kcat)
                if masked:
                    sc = jnp.where(_diag_mask(tq - row0, width, row0, row0), sc, NEG)
                m_new = jnp.maximum(m, jnp.max(sc, axis=1, keepdims=True))
                p = jnp.exp2(sc - m_new)
                alpha = jnp.exp2(m - m_new)
                l = alpha * l + jnp.sum(p, axis=1, keepdims=True)
                acc = alpha * acc + _dot(p.astype(BF16), v_ref[ks, :])
                new = (m_new, l, acc)
                return new if row0 == 0 else tuple(jnp.concatenate([c[:row0], n], axis=0) for c, n in zip(carry, new))

            carry = (jnp.full((tq, 1), NEG, F32), jnp.zeros((tq, 1), F32), jnp.zeros((tq, LANES), F32))
            carry = lax.fori_loop(0, i * ratio, lambda kb, c, block=block: block(kb * tk, tk, c, 0, False), carry)
            for row0 in range(0, tq, td):
                carry = block(i * tq + row0, td, carry, row0, True)
            m, l, acc = carry
            out = _keep(_lanes(tq, MLA_V * half, MLA_V), acc / l)
            lse = _keep(_lanes(tq, MLA_ROPE * half, MLA_ROPE), jnp.broadcast_to(m + jnp.log2(l), (tq, LANES)))
            if half == 0:
                o_ref[...] = out
                lse_ref[...] = lse
            else:
                o_ref[...] += out
                lse_ref[...] += lse

    qblk = pl.BlockSpec((tq, LANES), lambda g, i: (i, g))
    full = pl.BlockSpec((s, LANES), lambda g, i: (0, g))
    return pl.pallas_call(
        kern, name="mla_fwd", grid=(MLA_HEADS // 2, s // tq),
        in_specs=[qblk, pl.BlockSpec((tq, LANES), lambda g, i: (i, g // 2)), full, pl.BlockSpec((s, LANES), lambda g, i: (0, 0)), full],
        out_specs=[qblk, qblk],
        out_shape=[_sds((s, MLA_WIDTH), F32), _sds((s, MLA_HEADS // 2 * LANES), F32)], compiler_params=_params(2),
    )(qn, qr, kn, kr, v)


def _mla_bwd(qn, qr, kn, kr, v, o, do, lse, tq=MLA_TQ, tk=MLA_BWD_TK, td=MLA_DIAG_TK):
    s = qn.shape[0]
    tq, tk, td = min(tq, s), min(tk, s), min(td, s)
    ratio = tq // tk

    def kern(qn_ref, qr_ref, kn_ref, kr_ref, v_ref, o_ref, do_ref, lse_ref, dqn_ref, dqr_ref, dkn_ref, dkr_ref, dv_ref):
        g = pl.program_id(0)
        i = pl.program_id(1)

        @pl.when(i == 0)
        def _():
            dkn_ref[...] = jnp.zeros_like(dkn_ref)
            dkr_ref[...] = jnp.zeros_like(dkr_ref)
            dv_ref[...] = jnp.zeros_like(dv_ref)

        for half in range(2):
            rope_lo = MLA_ROPE * (2 * (g % 2) + half)
            qcat = _mla_qcat(qn_ref, qr_ref, rope_lo, half, tq)
            mine = _lanes(tq, MLA_V * half, MLA_V)
            do_f = _keep(mine, do_ref[...])
            do_b = do_f.astype(BF16)
            delta = jnp.sum(do_f * o_ref[...], axis=1, keepdims=True)
            lse_v = lse_ref[:, MLA_ROPE * half:MLA_ROPE * half + 1]

            def block(k0, width, dq_acc, row0, masked, qcat=qcat, do_b=do_b, delta=delta, lse_v=lse_v):
                ks = pl.ds(pl.multiple_of(k0, width), width)
                kcat = jnp.concatenate([kn_ref[ks, :], kr_ref[ks, :]], axis=1)
                qc, dob = qcat[row0:], do_b[row0:]
                p = jnp.exp2(_dot_nt(qc, kcat) - lse_v[row0:])
                if masked:
                    p = jnp.where(_diag_mask(tq - row0, width, row0, row0), p, 0.0)
                ds = (p * (_dot_nt(dob, v_ref[ks, :]) - delta[row0:])).astype(BF16)
                dv_ref[ks, :] += _dot_tn(p.astype(BF16), dob)
                dkc = _dot_tn(ds, qc)
                dkn_ref[ks, :] += dkc[:, :LANES]
                dkr_ref[ks, :] += dkc[:, LANES:]
                new = dq_acc[row0:] + _dot(ds, kcat)
                return new if row0 == 0 else jnp.concatenate([dq_acc[:row0], new], axis=0)

            acc = lax.fori_loop(0, i * ratio, lambda kb, c, block=block: block(kb * tk, tk, c, 0, False),
                                jnp.zeros((tq, 2 * LANES), F32))
            for row0 in range(0, tq, td):
                acc = block(i * tq + row0, td, acc, row0, True)
            dqn = _keep(_lanes(tq, MLA_NOPE * half, MLA_NOPE), acc[:, :LANES] * MLA_SCALE)
            dqr = _keep(_lanes(tq, rope_lo, MLA_ROPE), acc[:, LANES:] * MLA_SCALE)
            if half == 0:
                dqn_ref[...] = dqn
                dqr_ref[...] = dqr
            else:
                dqn_ref[...] += dqn
                dqr_ref[...] += dqr

    qblk = pl.BlockSpec((tq, LANES), lambda g, i: (i, g))
    full = pl.BlockSpec((s, LANES), lambda g, i: (0, g))
    once = lambda spec_map: pl.BlockSpec((s, LANES), spec_map, pipeline_mode=pl.Buffered(1))
    wide = _sds((s, MLA_HEADS // 2 * LANES), F32)
    return pl.pallas_call(
        kern, name="mla_bwd", grid=(MLA_HEADS // 2, s // tq),
        in_specs=[qblk, pl.BlockSpec((tq, LANES), lambda g, i: (i, g // 2)), once(lambda g, i: (0, g)), once(lambda g, i: (0, 0)),
                  once(lambda g, i: (0, g)), qblk, qblk, qblk],
        out_specs=[qblk, qblk, full, full, full],
        out_shape=[wide, wide, wide, wide, wide], compiler_params=_params(2),
    )(qn, qr, kn, kr, v, o, do, lse)


def _sb_masks(tk):
    j = lax.broadcasted_iota(jnp.int32, (tk, tk), 0)
    c = lax.broadcasted_iota(jnp.int32, (tk, tk), 1)
    return (j > c).astype(BF16), (j < c).astype(BF16)


def _sb_scores(qs, kk, msuf, strict):
    z = _dot_nt(qs, kk)
    lom = -(jnp.maximum(z, 0.0) + jnp.log(1.0 + jnp.exp(-jnp.abs(z))))
    if strict is not None:
        lom = jnp.where(strict, lom, 0.0)
    hi = lom.astype(BF16)
    lo = (lom - hi.astype(F32)).astype(BF16)
    return z, lom, _dot(hi, msuf) + _dot(lo, msuf)


def _sb_strict(tq, tk, d):
    row = lax.broadcasted_iota(jnp.int32, (tq, tk), 0)
    col = lax.broadcasted_iota(jnp.int32, (tq, tk), 1)
    return col + d * tk < row


def _sb_fwd(q, k, v, msuf, tq=SB_TQ, tk=SB_TK):
    s = q.shape[0]
    tq, tk = min(tq, s), min(tk, s)
    ratio = tq // tk

    def kern(q_ref, k_ref, v_ref, m_ref, o_ref, c_ref):
        i = pl.program_id(1)
        msf = m_ref[...]
        lane = lax.broadcasted_iota(jnp.int32, (tq, LANES), 1)
        for half in range(2):
            mine = _lanes(tq, SB_DIM * half, SB_DIM)
            qs = _keep(mine, q_ref[...]) * 0.125

            def block(kb, carry, dd, qs=qs):
                c, acc, cm = carry
                ks = pl.ds(pl.multiple_of(kb * tk, tk), tk)
                strict = None if dd is None else _sb_strict(tq, tk, dd)
                z, lom, suf = _sb_scores(qs, k_ref[ks, :], msf, strict)
                a = jnp.exp(z + lom + (suf + c))
                if strict is not None:
                    a = jnp.where(strict, a, 0.0)
                acc = acc + _dot(a.astype(BF16), v_ref[ks, :])
                cm = jnp.where(lane == kb, c, cm)
                return c + jnp.sum(lom, axis=1, keepdims=True), acc, cm

            carry = (jnp.zeros((tq, 1), F32), jnp.zeros((tq, LANES), F32), jnp.full((tq, LANES), NEG, F32))
            for dd in range(ratio - 1, -1, -1):
                carry = block(i * ratio + dd, carry, dd)

            def live(st):
                return jnp.logical_and(st[0] >= 0, jnp.max(st[1]) > -SB_SKIP)

            def step(st, block=block):
                return (st[0] - 1, *block(st[0], st[1:], None))

            _, _, acc, cm = lax.while_loop(live, step, (i * ratio - 1, *carry))
            if half == 0:
                o_ref[...] = _keep(mine, acc)
            else:
                o_ref[...] += _keep(mine, acc)
            c_ref[:, LANES * half:LANES * (half + 1)] = cm

    qblk = lambda n: pl.BlockSpec((tq, n), lambda g, i: (i, g))
    full = pl.BlockSpec((s, LANES), lambda g, i: (0, g))
    return pl.pallas_call(
        kern, name="sb_fwd", grid=(SB_HEADS // 2, s // tq),
        in_specs=[qblk(LANES), full, full, pl.BlockSpec((tk, tk), lambda g, i: (0, 0))],
        out_specs=[qblk(LANES), qblk(2 * LANES)],
        out_shape=[_sds((s, SB_WIDTH), F32), _sds((s, SB_HEADS * LANES), F32)], compiler_params=_params(2),
    )(q, k, v, msuf)


def _sb_bwd(q, k, v, do, cmat, msuf, mpre, tq=SB_TQ, tk=SB_TK):
    s = q.shape[0]
    tq, tk = min(tq, s), min(tk, s)
    ratio = tq // tk

    def kern(q_ref, k_ref, v_ref, do_ref, c_ref, ms_ref, mp_ref, dq_ref, dk_ref, dv_ref):
        i = pl.program_id(1)

        @pl.when(i == 0)
        def _():
            dk_ref[...] = jnp.zeros_like(dk_ref)
            dv_ref[...] = jnp.zeros_like(dv_ref)

        msf = ms_ref[...]
        mpf = mp_ref[...]
        lane = lax.broadcasted_iota(jnp.int32, (tq, LANES), 1)
        lane1 = lax.broadcasted_iota(jnp.int32, (1, LANES), 1)
        for half in range(2):
            mine = _lanes(tq, SB_DIM * half, SB_DIM)
            qv = _keep(mine, q_ref[...])
            qs = qv * 0.125
            do_b = _keep(mine, do_ref[...]).astype(BF16)
            cm = c_ref[:, LANES * half:LANES * (half + 1)]

            def block(kb, carry, dd, qv=qv, qs=qs, do_b=do_b, cm=cm):
                dq_acc, pc = carry
                ks = pl.ds(pl.multiple_of(kb * tk, tk), tk)
                kk = k_ref[ks, :]
                strict = None if dd is None else _sb_strict(tq, tk, dd)
                z, lom, suf = _sb_scores(qs, kk, msf, strict)
                c = jnp.sum(jnp.where(lane == kb, cm, 0.0), axis=1, keepdims=True)
                a = jnp.exp(z + lom + (suf + c))
                if strict is not None:
                    a = jnp.where(strict, a, 0.0)
                g = _dot_nt(do_b, v_ref[ks, :]) * a
                p = pc + _dot(g.astype(BF16), mpf)
                omb = jnp.exp(lom)
                dz = (g * omb - (1.0 - omb) * p) * 0.125
                if strict is not None:
                    dz = jnp.where(strict, dz, 0.0)
                dz = dz.astype(BF16)
                dv_ref[ks, :] += _dot_tn(a.astype(BF16), do_b)
                dk_ref[ks, :] += _dot_tn(dz, qv)
                return dq_acc + _dot(dz, kk), pc + jnp.sum(g, axis=1, keepdims=True)

            seen = jnp.logical_and(jnp.max(cm, axis=0, keepdims=True) > -SB_SKIP, lane1 < i * ratio)
            first = i * ratio - jnp.sum(seen.astype(jnp.int32))
            carry = (jnp.zeros((tq, LANES), F32), jnp.zeros((tq, 1), F32))
            carry = lax.fori_loop(first, i * ratio, lambda kb, c, block=block: block(kb, c, None), carry)
            for dd in range(ratio):
                carry = block(i * ratio + dd, carry, dd)
            if half == 0:
                dq_ref[...] = _keep(mine, carry[0])
            else:
                dq_ref[...] += _keep(mine, carry[0])

    qblk = lambda n: pl.BlockSpec((tq, n), lambda g, i: (i, g))
    full = pl.BlockSpec((s, LANES), lambda g, i: (0, g))
    msk = pl.BlockSpec((tk, tk), lambda g, i: (0, 0))
    return pl.pallas_call(
        kern, name="sb_bwd", grid=(SB_HEADS // 2, s // tq),
        in_specs=[qblk(LANES), full, full, qblk(LANES), qblk(2 * LANES), msk, msk],
        out_specs=[qblk(LANES), full, full],
        out_shape=[_sds((s, SB_WIDTH), F32)] * 3, compiler_params=_params(2),
    )(q, k, v, do, cmat, msuf, mpre)


def _place():
    return lax.axis_index("x"), lax.axis_index("y"), lax.axis_index("c")


def _other_chips(x, y):
    return [(1 - x, y), (x, 1 - y), (1 - x, 1 - y)]


HBM_SPEC = pl.BlockSpec(memory_space=pl.ANY)


def _allgather_list(name, shards):
    n = len(shards)
    halves = [a.shape[0] // 2 for a in shards]

    def body(*refs):
        ins, outs = refs[:n], refs[n:2 * n]
        send_sems, recv_sems, local_sems = refs[2 * n:]
        x, y, c = _place()
        sibling = (x, y, 1 - c)
        chips = _other_chips(x, y)

        def half_of(a, ref, pc):
            return ref.at[pl.ds(pl.multiple_of(pc * halves[a], 16), halves[a]), :]

        def copy(a, k, chip, pc, to, src=None):
            dst = half_of(a, outs[a].at[2 * chip[0] + chip[1]], pc)
            return pltpu.make_async_remote_copy(src_ref=dst if src is None else src, dst_ref=dst, send_sem=send_sems.at[6 * a + k],
                                                recv_sem=recv_sems.at[6 * a + k], device_id=to, device_id_type=MESH)

        local = [pltpu.make_async_copy(ins[a], outs[a].at[2 * x + y], local_sems.at[a]) for a in range(n)]
        for cp in local:
            cp.start()
        first = [copy(a, j, (x, y), c, (*chip, c), src=half_of(a, ins[a], c)) for a in range(n) for j, chip in enumerate(chips)]
        for cp in first:
            cp.start()
        passed = []
        for j, chip in enumerate(chips):
            for a in range(n):
                copy(a, j, chip, c, sibling).wait_recv()
                passed.append(copy(a, 3 + j, chip, c, sibling))
                passed[-1].start()
        for j, chip in enumerate(chips):
            for a in range(n):
                copy(a, 3 + j, chip, 1 - c, sibling).wait_recv()
        for cp in first + passed:
            cp.wait_send()
        for cp in local:
            cp.wait()

    return pl.pallas_call(
        body, name=name, out_shape=[_sds((N_SHARD,) + a.shape, a.dtype) for a in shards], in_specs=[HBM_SPEC] * n,
        out_specs=[HBM_SPEC] * n,
        scratch_shapes=[pltpu.SemaphoreType.DMA((6 * n,)), pltpu.SemaphoreType.DMA((6 * n,)), pltpu.SemaphoreType.DMA((n,))],
    )(*shards)


def _swap_halves(gs):
    n = len(gs)

    def body(*refs):
        ins, outs = refs[:n], refs[n:2 * n]
        send_sems, recv_sems = refs[2 * n:]
        x, y, c = _place()
        copies = []
        for a in range(n):
            h = gs[a].shape[1] // 2
            src = ins[a].at[:, pl.ds(pl.multiple_of((1 - c) * h, 8), h), :]
            copies.append(pltpu.make_async_remote_copy(src_ref=src, dst_ref=outs[a], send_sem=send_sems.at[a], recv_sem=recv_sems.at[a],
                                                       device_id=(x, y, 1 - c), device_id_type=MESH))
        for cp in copies:
            cp.start()
        for cp in copies:
            cp.wait()

    return pl.pallas_call(
        body, name="swap_halves", out_shape=[_sds((N_SHARD, g.shape[1] // 2, g.shape[2]), g.dtype) for g in gs],
        in_specs=[HBM_SPEC] * n, out_specs=[HBM_SPEC] * n,
        scratch_shapes=[pltpu.SemaphoreType.DMA((n,)), pltpu.SemaphoreType.DMA((n,))],
    )(*gs)


def _add_sibling(gs, gots, c_idx):
    n = len(gs)

    def kern(c_ref, *refs):
        for a in range(n):
            tot = refs[a][...] + refs[n + a][...]
            refs[2 * n + a][...] = tot
            refs[3 * n + a][...] = tot.astype(BF16)

    quarter = lambda g: (None, g.shape[1] // 4, g.shape[2])
    in_specs = [pl.BlockSpec(quarter(g), lambda b, s, c_ref: (b, 2 * c_ref[0] + s, 0)) for g in gs]
    in_specs += [pl.BlockSpec(quarter(g), lambda b, s, c_ref: (b, s, 0)) for g in gs]
    out_specs = [pl.BlockSpec(quarter(g), lambda b, s, c_ref: (b, s, 0)) for g in gs] * 2
    out_shape = [_sds(t.shape, F32) for t in gots] + [_sds(t.shape, BF16) for t in gots]
    outs = pl.pallas_call(
        kern, name="add_sibling", out_shape=out_shape,
        grid_spec=pltpu.PrefetchScalarGridSpec(num_scalar_prefetch=1, grid=(N_SHARD, 2), in_specs=in_specs, out_specs=out_specs),
        compiler_params=_params(2),
    )(c_idx.reshape(1), *gs, *gots)
    return outs[:n], outs[n:]


def _chip_scatter(ps):
    n = len(ps)

    def body(*refs):
        ins, outs = refs[:n], refs[n:2 * n]
        send_sems, recv_sems = refs[2 * n:]
        x, y, c = _place()
        copies = [pltpu.make_async_remote_copy(
            src_ref=ins[a].at[2 * px + py], dst_ref=outs[a].at[j], send_sem=send_sems.at[3 * a + j], recv_sem=recv_sems.at[3 * a + j],
            device_id=(px, py, c), device_id_type=MESH) for a in range(n) for j, (px, py) in enumerate(_other_chips(x, y))]
        for cp in copies:
            cp.start()
        for cp in copies:
            cp.wait()

    return pl.pallas_call(
        body, name="chip_scatter", out_shape=[_sds((N_SHARD - 1,) + p.shape[1:], p.dtype) for p in ps], in_specs=[HBM_SPEC] * n,
        out_specs=[HBM_SPEC] * n, scratch_shapes=[pltpu.SemaphoreType.DMA((3 * n,)), pltpu.SemaphoreType.DMA((3 * n,))],
    )(*ps)


def _add_chips(ps, others, shard_idx):
    n = len(ps)

    def kern(b_ref, *refs):
        for a in range(n):
            tot = refs[a][...]
            for j in range(N_SHARD - 1):
                tot = tot + refs[n + a][j].astype(F32)
            refs[2 * n + a][...] = tot

    in_specs = [pl.BlockSpec((None, p.shape[1] // 2, p.shape[2]), lambda s, b_ref: (b_ref[0], s, 0)) for p in ps]
    in_specs += [pl.BlockSpec((N_SHARD - 1, p.shape[1] // 2, p.shape[2]), lambda s, b_ref: (0, s, 0)) for p in ps]
    out_specs = [pl.BlockSpec((p.shape[1] // 2, p.shape[2]), lambda s, b_ref: (s, 0)) for p in ps]
    return pl.pallas_call(
        kern, name="add_chips", out_shape=[_sds(p.shape[1:], F32) for p in ps],
        grid_spec=pltpu.PrefetchScalarGridSpec(num_scalar_prefetch=1, grid=(2,), in_specs=in_specs, out_specs=out_specs),
        compiler_params=_params(1),
    )(shard_idx.reshape(1), *ps, *others)


def _swap_result(mines):
    n = len(mines)

    def body(*refs):
        ins, outs = refs[:n], refs[n:2 * n]
        send_sems, recv_sems, local_sems = refs[2 * n:]
        x, y, c = _place()
        local, remote = [], []
        for a in range(n):
            h = mines[a].shape[0]
            dst = outs[a].at[pl.ds(pl.multiple_of(c * h, 8), h), :]
            local.append(pltpu.make_async_copy(ins[a], dst, local_sems.at[a]))
            remote.append(pltpu.make_async_remote_copy(src_ref=ins[a], dst_ref=dst, send_sem=send_sems.at[a], recv_sem=recv_sems.at[a],
                                                       device_id=(x, y, 1 - c), device_id_type=MESH))
        for cp in local + remote:
            cp.start()
        for cp in remote:
            cp.wait_send()
        for a in range(n):
            h = mines[a].shape[0]
            theirs = outs[a].at[pl.ds(pl.multiple_of((1 - c) * h, 8), h), :]
            pltpu.make_async_remote_copy(src_ref=ins[a], dst_ref=theirs, send_sem=send_sems.at[a], recv_sem=recv_sems.at[a],
                                         device_id=(x, y, 1 - c), device_id_type=MESH).wait_recv()
        for cp in local:
            cp.wait()

    return pl.pallas_call(
        body, name="swap_result", out_shape=[_sds((2 * m.shape[0], m.shape[1]), m.dtype) for m in mines], in_specs=[HBM_SPEC] * n,
        out_specs=[HBM_SPEC] * n,
        scratch_shapes=[pltpu.SemaphoreType.DMA((n,)), pltpu.SemaphoreType.DMA((n,)), pltpu.SemaphoreType.DMA((n,))],
    )(*mines)


def _allreduce_small(v):
    m_per, n = v.shape

    def body(x_ref, tot_ref, all_ref, send_sems, recv_sems, local_sem):
        x, y, c = _place()
        me, sibling = (x, y, c), (x, y, 1 - c)
        chips = _other_chips(x, y)

        def rows(px, py, pc):
            return all_ref.at[pl.ds(pl.multiple_of((4 * px + 2 * py + pc) * m_per, 8), m_per), :]

        def copy(k, block, to, src=None):
            return pltpu.make_async_remote_copy(
                src_ref=rows(*block) if src is None else src, dst_ref=rows(*block), send_sem=send_sems.at[k],
                recv_sem=recv_sems.at[k], device_id=to, device_id_type=MESH)

        mine = pltpu.make_async_copy(x_ref, rows(*me), local_sem)
        mine.start()
        first = [copy(0, me, sibling, src=x_ref)] + [copy(1 + j, me, (*chip, c), src=x_ref) for j, chip in enumerate(chips)]
        for cp in first:
            cp.start()
        passed = [copy(4 + j, (*chip, c), sibling) for j, chip in enumerate(chips)]
        for j, chip in enumerate(chips):
            copy(1 + j, (*chip, c), me).wait_recv()
            passed[j].start()
        copy(0, sibling, me).wait_recv()
        for j, chip in enumerate(chips):
            copy(4 + j, (*chip, 1 - c), me).wait_recv()
        for cp in first + passed:
            cp.wait_send()
        mine.wait()
        tot = all_ref[0:m_per, :]
        for dev in range(1, 8):
            tot = tot + all_ref[dev * m_per:(dev + 1) * m_per, :]
        tot_ref[...] = tot

    vmem = pl.BlockSpec(memory_space=pltpu.VMEM)
    return pl.pallas_call(
        body, name="allreduce_small", out_shape=_sds((m_per, n), F32), in_specs=[vmem], out_specs=vmem,
        scratch_shapes=[pltpu.VMEM((8 * m_per, n), F32), pltpu.SemaphoreType.DMA((7,)), pltpu.SemaphoreType.DMA((7,)),
                        pltpu.SemaphoreType.DMA],
    )(v)


def _adamw(name, w, g, m, v):
    rows, width = w.shape
    tr = rows // 4 if rows % 32 == 0 else rows

    def kern(w_ref, g_ref, m_ref, v_ref, d_ref, mo_ref, vo_ref):
        g_v = g_ref[...]
        m_new = ADAM_B1 * m_ref[...] + (1.0 - ADAM_B1) * g_v
        v_new = ADAM_B2 * v_ref[...] + (1.0 - ADAM_B2) * (g_v * g_v)
        m_hat = m_new / (1.0 - ADAM_B1 ** ADAM_STEP)
        v_hat = v_new / (1.0 - ADAM_B2 ** ADAM_STEP)
        d_ref[...] = -ADAM_LR * (m_hat / (jnp.sqrt(v_hat) + ADAM_EPS) + ADAM_WD * w_ref[...])
        mo_ref[...] = m_new
        vo_ref[...] = v_new

    spec = pl.BlockSpec((tr, width), lambda i: (i, 0))
    return pl.pallas_call(kern, name=name, grid=(rows // tr,), in_specs=[spec] * 4, out_specs=[spec] * 3,
                          out_shape=[_sds((rows, width), F32)] * 3, compiler_params=_params(1))(w, g, m, v)


SHARDED = (("w_in", D_MODEL, IN_WIDTH, 1), ("w_uq", Q_RANK, MLA_HEADS * MLA_QK, 1),
           ("w_ukv", KV_RANK, MLA_HEADS * (MLA_NOPE + MLA_V), 1), ("w_o", D_MODEL, D_MODEL, 0),
           ("w_gate", D_MODEL, D_FF, 1), ("w_up", D_MODEL, D_FF, 1), ("w_down", D_FF, D_MODEL, 0))
SMALL = (("norm_mix", D_MODEL), ("q_latent_norm", Q_RANK), ("kv_latent_norm", KV_RANK), ("out_norm_mla", MLA_WIDTH),
         ("out_norm_sb", SB_WIDTH), ("norm_ffn", D_MODEL), ("norm_final", D_MODEL))


def _full_weight(gathered, axis):
    n_sh, k, n = gathered.shape
    return gathered.transpose(1, 0, 2).reshape(k, n_sh * n) if axis == 1 else gathered.reshape(n_sh * k, n)


def _shard_major(g, axis):
    r, c = g.shape
    return g.reshape(r, N_SHARD, c // N_SHARD).transpose(1, 0, 2) if axis == 1 else g.reshape(N_SHARD, r // N_SHARD, c)


def _rot_cols(w):
    hh = MLA_ROPE // 2
    return jnp.concatenate([-w[..., hh:], w[..., :hh]], axis=-1)


def _rot_cols_t(g):
    hh = MLA_ROPE // 2
    return jnp.concatenate([g[..., hh:], -g[..., :hh]], axis=-1)


def _prepare_weights(full, small):
    w_in = full["w_in"]
    s0, s1, s2 = Q_RANK, Q_RANK + KV_RANK, Q_RANK + KV_RANK + MLA_ROPE
    uq = full["w_uq"].reshape(Q_RANK, MLA_HEADS, MLA_QK)
    ukv = full["w_ukv"].reshape(KV_RANK, MLA_HEADS, MLA_NOPE + MLA_V)
    w_kr = w_in[:, s1:s2]
    per_tile = ROPE_TILE // MLA_ROPE
    w = {
        "w_cq": w_in[:, :s0], "w_ckv": w_in[:, s0:s1],
        "w_kr4": jnp.tile(w_kr, (1, per_tile)), "w_kr4r": jnp.tile(_rot_cols(w_kr), (1, per_tile)),
        "w_kr8": jnp.tile(w_kr, (1, MLA_HEADS)), "w_kr8r": jnp.tile(_rot_cols(w_kr), (1, MLA_HEADS)),
        "w_sbq": w_in[:, s2:s2 + SB_WIDTH], "w_sbk": w_in[:, s2 + SB_WIDTH:s2 + 2 * SB_WIDTH], "w_sbv": w_in[:, s2 + 2 * SB_WIDTH:],
        "w_qn": uq[..., :MLA_NOPE].reshape(Q_RANK, -1), "w_qr": uq[..., MLA_NOPE:].reshape(Q_RANK, -1),
        "w_qrr": _rot_cols(uq[..., MLA_NOPE:]).reshape(Q_RANK, -1),
        "w_kn": ukv[..., :MLA_NOPE].reshape(KV_RANK, -1), "w_v": ukv[..., MLA_NOPE:].reshape(KV_RANK, -1),
        "w_oa": full["w_o"][:MLA_WIDTH], "w_ob": full["w_o"][MLA_WIDTH:],
        "w_gate": full["w_gate"], "w_up": full["w_up"], "w_down": full["w_down"],
    }
    for name in list(w):
        w[name + "_t"] = w[name].T
    w.update(g_mix=small["norm_mix"], g_q=small["q_latent_norm"], g_kv=small["kv_latent_norm"], g_a=small["out_norm_mla"],
             g_b=small["out_norm_sb"], g_f=small["norm_ffn"], g_n=small["norm_final"])
    return w


def _rope_tables(positions):
    inv_freq = ROPE_THETA ** (-jnp.arange(0, MLA_ROPE, 2, dtype=F32) / MLA_ROPE)
    ang = positions.astype(F32)[:, None] * inv_freq[None, :]
    cos, sin = jnp.cos(ang), jnp.sin(ang)
    return {"cos": jnp.tile(jnp.concatenate([cos, cos], axis=1), (1, MLA_HEADS)),
            "sin": jnp.tile(jnp.concatenate([sin, sin], axis=1), (1, MLA_HEADS))}


def _by_head(g_wide, g_narrow, wide, narrow):
    r = g_wide.shape[0]
    return jnp.concatenate([g_wide.reshape(r, MLA_HEADS, wide), g_narrow.reshape(r, MLA_HEADS, narrow)], axis=-1).reshape(r, -1)


def kernel(x, positions, norm_mix, w_in, q_latent_norm, w_uq, kv_latent_norm, w_ukv, out_norm_mla, out_norm_sb, w_o, norm_ffn, w_gate, w_up, w_down, norm_final, loss_target, m_norm_mix, m_w_in, m_q_latent_norm, m_w_uq, m_kv_latent_norm, m_w_ukv, m_out_norm_mla, m_out_norm_sb, m_w_o, m_norm_ffn, m_w_gate, m_w_up, m_w_down, m_norm_final, v_norm_mix, v_w_in, v_q_latent_norm, v_w_uq, v_kv_latent_norm, v_w_ukv, v_out_norm_mla, v_out_norm_sb, v_w_o, v_norm_ffn, v_w_gate, v_w_up, v_w_down, v_norm_final):
    given = dict(norm_mix=norm_mix, w_in=w_in, q_latent_norm=q_latent_norm, w_uq=w_uq, kv_latent_norm=kv_latent_norm, w_ukv=w_ukv,
                 out_norm_mla=out_norm_mla, out_norm_sb=out_norm_sb, w_o=w_o, norm_ffn=norm_ffn, w_gate=w_gate, w_up=w_up,
                 w_down=w_down, norm_final=norm_final)
    mom_m = dict(norm_mix=m_norm_mix, w_in=m_w_in, q_latent_norm=m_q_latent_norm, w_uq=m_w_uq, kv_latent_norm=m_kv_latent_norm,
                 w_ukv=m_w_ukv, out_norm_mla=m_out_norm_mla, out_norm_sb=m_out_norm_sb, w_o=m_w_o, norm_ffn=m_norm_ffn,
                 w_gate=m_w_gate, w_up=m_w_up, w_down=m_w_down, norm_final=m_norm_final)
    mom_v = dict(norm_mix=v_norm_mix, w_in=v_w_in, q_latent_norm=v_q_latent_norm, w_uq=v_w_uq, kv_latent_norm=v_kv_latent_norm,
                 w_ukv=v_w_ukv, out_norm_mla=v_out_norm_mla, out_norm_sb=v_out_norm_sb, w_o=v_w_o, norm_ffn=v_norm_ffn,
                 w_gate=v_w_gate, w_up=v_w_up, w_down=v_w_down, norm_final=v_norm_final)
    xs = x[0]
    tgt = loss_target[0]
    s = xs.shape[0]
    c_idx = lax.axis_index("c")

    shard2d = {name: given[name].reshape(given[name].shape[-2:]) for name, *_ in SHARDED}
    gathered = _allgather_list("allgather_w", [shard2d[name].astype(BF16) for name, *_ in SHARDED])
    full = {name: _full_weight(t, axis) for (name, _, _, axis), t in zip(SHARDED, gathered)}
    small = {name: given[name].reshape(1, n) for name, n in SMALL}
    w = _prepare_weights(full, small)
    tabs = _rope_tables(positions[0])
    msuf, mpre = _sb_masks(min(SB_TK, s))

    u, cq, ckv, cqn, ckvn, qn, qr, kn, vm, kr, sq, sk, sv = _fwd_a(xs, tabs, w)
    o_mla, lse = _mla_fwd(qn, qr, kn, kr, vm)
    o_sb, cmat = _sb_fwd(sq, sk, sv, msuf)
    merged, h1, f, gate, up, act = _fwd_b1(xs, o_mla, o_sb, w)
    dh2, loss_part, dg_n = _fwd_b2(h1, act, tgt, w)

    dgate, dup = _bwd_b1(dh2, gate, up, w)
    dh1, do_mla, do_sb, dg_f, dg_a, dg_b = _bwd_b2(dgate, dup, h1, dh2, o_mla, o_sb, w)
    dqn, dqr, dkn, dkr, dvm = _mla_bwd(qn, qr, kn, kr, vm, o_mla, do_mla, lse)
    dsq, dsk, dsv = _sb_bwd(sq, sk, sv, do_sb, cmat, msuf, mpre)
    dx, a1, a2, dcq, dckv, dkrc, dkrs, dg_q, dg_kv, dg_mix = _bwd_a(xs, dh1, cq, ckv, dqn, dqr, dkn, dvm, dkr, dsq, dsk, dsv, tabs, w)

    g_cq, g_ckv, g_krc, g_krs, g_sq, g_sk, g_sv = _tn_multi("dw_in", u, [dcq, dckv, dkrc, dkrs, dsq, dsk, dsv])
    g_qn, g_qr1, g_qr2 = _tn_multi("dw_uq", cqn, [dqn, a1, a2])
    g_kn, g_v = _tn_multi("dw_ukv", ckvn, [dkn, dvm])
    slots = lambda g: g.reshape(g.shape[0], MLA_HEADS, MLA_ROPE)
    g_kr = jnp.sum(slots(g_krc), axis=1) + _rot_cols_t(jnp.sum(slots(g_krs), axis=1))
    g_qr = (slots(g_qr1) + _rot_cols_t(slots(g_qr2))).reshape(Q_RANK, -1)
    grads = {
        "w_in": jnp.concatenate([g_cq, g_ckv, g_kr, g_sq, g_sk, g_sv], axis=1),
        "w_uq": _by_head(g_qn, g_qr, MLA_NOPE, MLA_ROPE),
        "w_ukv": _by_head(g_kn * MLA_DK_SCALE, g_v, MLA_NOPE, MLA_V),
        "w_o": _tn_matmul("dw_o", merged, dh1),
        "w_gate": _tn_matmul("dw_gate", f, dgate),
        "w_up": _tn_matmul("dw_up", f, dup),
        "w_down": _tn_matmul("dw_down", act, dh2),
    }

    gs = [_shard_major(grads[name], axis) for name, _, _, axis in SHARDED]
    chip_f32, chip_bf16 = _add_sibling(gs, _swap_halves(gs), c_idx)
    mine = _add_chips(chip_f32, _chip_scatter(chip_bf16), 2 * lax.axis_index("x") + lax.axis_index("y"))
    g_shard = {name: t for (name, *_), t in zip(SHARDED, _swap_result(mine))}

    small_parts = jnp.concatenate([dg_mix, dg_q, dg_kv, dg_a, dg_b, dg_f, dg_n], axis=1)
    small_g = _allreduce_small(jnp.broadcast_to(small_parts, (8, small_parts.shape[1])))[0:1]
    loss = lax.psum(loss_part[0, 0], ("x", "y", "c"))

    g_out, d_out, m_out, v_out = {}, {}, {}, {}
    for name, *_ in SHARDED:
        shape = given[name].shape
        d, mn, vn = _adamw("adamw_" + name, shard2d[name], g_shard[name], mom_m[name].reshape(shard2d[name].shape),
                           mom_v[name].reshape(shard2d[name].shape))
        g_out[name], d_out[name], m_out[name], v_out[name] = (t.reshape(shape) for t in (g_shard[name], d, mn, vn))
    cat = lambda src: jnp.concatenate([src[name].reshape(1, n) for name, n in SMALL], axis=1)
    d, mn, vn = _adamw("adamw_small", cat(given), small_g, cat(mom_m), cat(mom_v))
    off = 0
    for name, n in SMALL:
        shape = given[name].shape
        g_out[name], d_out[name], m_out[name], v_out[name] = (t[:, off:off + n].reshape(shape) for t in (small_g, d, mn, vn))
        off += n

    order = ["norm_mix", "w_in", "q_latent_norm", "w_uq", "kv_latent_norm", "w_ukv", "out_norm_mla", "out_norm_sb", "w_o",
             "norm_ffn", "w_gate", "w_up", "w_down", "norm_final"]
    return (loss, dx[None], *[g_out[n] for n in order], *[d_out[n] for n in order], *[m_out[n] for n in order],
            *[v_out[n] for n in order])
```

```python
import functools
import math

import jax
import jax.numpy as jnp
from jax import lax
from jax.experimental import pallas as pl
from jax.experimental.pallas import tpu as pltpu

F32 = jnp.float32
BF16 = jnp.bfloat16
MESH = pl.DeviceIdType.MESH

D_MODEL = 1024
EPS = 1e-6
MLA_HEADS = 8
MLA_NOPE = 64
MLA_ROPE = 32
MLA_V = 64
MLA_QK = MLA_NOPE + MLA_ROPE
Q_RANK = 256
KV_RANK = 128
ROPE_THETA = 10000.0
SB_HEADS = 8
SB_DIM = 64
MLA_WIDTH = MLA_HEADS * MLA_V
SB_WIDTH = SB_HEADS * SB_DIM
D_FF = 2816
IN_WIDTH = Q_RANK + KV_RANK + MLA_ROPE + 3 * SB_WIDTH

ADAM_LR = 0.001
ADAM_B1 = 0.9
ADAM_B2 = 0.999
ADAM_EPS = 1e-08
ADAM_WD = 0.01
ADAM_STEP = 10

N_SHARD = 4
LANES = 128
ROPE_TILE = LANES
VMEM_LIMIT = 56 * 1024 * 1024
TN_ACC_BYTES = 6 * 1024 * 1024 + 512 * 1024
NEG = -1e30
MLA_SCALE = 1.0 / math.sqrt(MLA_QK)
MLA_DK_SCALE = math.log(2.0)
MLA_QSCALE = MLA_SCALE * math.log2(math.e)
SB_SKIP = 110.0

ROW_TILE = 512
ROW_TILE_ELEMENTWISE = 256
MLA_TQ = 1024
SB_TQ = 512
MLA_TK = 1024
MLA_BWD_TK = 512
MLA_DIAG_TK = 512
SB_TK = 256
TN_TS = 2048


def _dot(a, b):
    return jnp.dot(a, b, preferred_element_type=F32)


def _dot_nt(a, b):
    return lax.dot_general(a, b, (((1,), (1,)), ((), ())), preferred_element_type=F32)


def _dot_tn(a, b):
    return lax.dot_general(a, b, (((0,), (0,)), ((), ())), preferred_element_type=F32)


def _params(n_grid, vmem=VMEM_LIMIT):
    return pltpu.CompilerParams(dimension_semantics=("arbitrary",) * n_grid, vmem_limit_bytes=vmem)


def _rms(x):
    r = lax.rsqrt(jnp.mean(x * x, axis=-1, keepdims=True) + EPS)
    return x * r, r


def _rms_bwd(n, r, g, dy):
    dn = dy * g
    dx = r * (dn - n * jnp.mean(dn * n, axis=-1, keepdims=True))
    return dx, jnp.sum(dy * n, axis=0, keepdims=True)


def _accumulate(ref, val, step):
    @pl.when(step == 0)
    def _():
        ref[...] = val

    @pl.when(step != 0)
    def _():
        ref[...] += val


def _rowwise(name, body, rows, consts, row_out, acc_out, tm):
    n_rows = rows[0].shape[0]
    tm = min(tm, n_rows)
    nr, nc, no = len(rows), len(consts), len(row_out)

    def kern(*refs):
        body(refs[:nr], refs[nr:nr + nc], refs[nr + nc:nr + nc + no], refs[nr + nc + no:], pl.program_id(0))

    in_specs = [pl.BlockSpec((tm, a.shape[1]), lambda i: (i, 0)) for a in rows]
    in_specs += [pl.BlockSpec(a.shape, lambda i: (0, 0), pipeline_mode=pl.Buffered(1)) for a in consts]
    out_specs = [pl.BlockSpec((tm, s.shape[1]), lambda i: (i, 0)) for s in row_out]
    out_specs += [pl.BlockSpec(s.shape, lambda i: (0, 0)) for s in acc_out]
    return pl.pallas_call(
        kern, name=name, grid=(n_rows // tm,), in_specs=in_specs, out_specs=out_specs,
        out_shape=list(row_out) + list(acc_out), compiler_params=_params(1),
    )(*rows, *consts)


def _sds(shape, dtype):
    return jax.ShapeDtypeStruct(shape, dtype)


def _fwd_a(x, tabs, w):
    s = x.shape[0]

    def body(r, c, o, a, step):
        x_ref, cos_ref, sin_ref = r
        gmix, wcq, wckv, wkr, wkrr, wsq, wsk, wsv, gq, wqn, wqr, wqrr, gkv, wkn, wv = c
        u_o, cq_o, ckv_o, cqn_o, ckvn_o, qn_o, qr_o, kn_o, v_o, kr_o, sq_o, sk_o, sv_o = o
        cos, sin = cos_ref[...], sin_ref[...]
        n, _ = _rms(x_ref[...])
        u = (n * gmix[...]).astype(BF16)
        u_o[...] = u
        cq = _dot(u, wcq[...])
        ckv = _dot(u, wckv[...])
        kr_o[...] = (_dot(u, wkr[...]) * cos[:, :ROPE_TILE] + _dot(u, wkrr[...]) * sin[:, :ROPE_TILE]).astype(BF16)
        sq_o[...] = _dot(u, wsq[...]).astype(BF16)
        sk_o[...] = _dot(u, wsk[...]).astype(BF16)
        sv_o[...] = _dot(u, wsv[...]).astype(BF16)
        cq_o[...] = cq
        ckv_o[...] = ckv
        nq, _ = _rms(cq)
        cqn = (nq * gq[...]).astype(BF16)
        cqn_o[...] = cqn
        qn_o[...] = (_dot(cqn, wqn[...]) * MLA_QSCALE).astype(BF16)
        qr_o[...] = ((_dot(cqn, wqr[...]) * cos + _dot(cqn, wqrr[...]) * sin) * MLA_QSCALE).astype(BF16)
        nkv, _ = _rms(ckv)
        ckvn = (nkv * gkv[...]).astype(BF16)
        ckvn_o[...] = ckvn
        kn_o[...] = _dot(ckvn, wkn[...]).astype(BF16)
        v_o[...] = _dot(ckvn, wv[...]).astype(BF16)

    outs = [
        _sds((s, D_MODEL), BF16), _sds((s, Q_RANK), F32), _sds((s, KV_RANK), F32), _sds((s, Q_RANK), BF16),
        _sds((s, KV_RANK), BF16), _sds((s, MLA_HEADS * MLA_NOPE), BF16), _sds((s, MLA_HEADS * MLA_ROPE), BF16),
        _sds((s, MLA_HEADS * MLA_NOPE), BF16), _sds((s, MLA_WIDTH), BF16), _sds((s, ROPE_TILE), BF16),
        _sds((s, SB_WIDTH), BF16), _sds((s, SB_WIDTH), BF16), _sds((s, SB_WIDTH), BF16),
    ]
    consts = [w["g_mix"], w["w_cq"], w["w_ckv"], w["w_kr4"], w["w_kr4r"], w["w_sbq"], w["w_sbk"], w["w_sbv"], w["g_q"],
              w["w_qn"], w["w_qr"], w["w_qrr"], w["g_kv"], w["w_kn"], w["w_v"]]
    return _rowwise("fwd_a", body, [x, tabs["cos"], tabs["sin"]], consts, outs, [], ROW_TILE)


def _fwd_b1(x, o_mla, o_sb, w):
    s = x.shape[0]

    def body(r, c, o, a, step):
        x_ref, oa_ref, ob_ref = r
        ga, gb, woa, wob, gf, wg, wu = c
        mg_o, h1_o, f_o, gate_o, up_o, act_o = o
        na, _ = _rms(oa_ref[...])
        nb, _ = _rms(ob_ref[...])
        ma = (na * ga[...]).astype(BF16)
        mb = (nb * gb[...]).astype(BF16)
        mg_o[:, :MLA_WIDTH] = ma
        mg_o[:, MLA_WIDTH:] = mb
        h1 = x_ref[...] + _dot(ma, woa[...]) + _dot(mb, wob[...])
        h1_o[...] = h1
        nf, _ = _rms(h1)
        f = (nf * gf[...]).astype(BF16)
        f_o[...] = f
        gate = _dot(f, wg[...])
        up = _dot(f, wu[...])
        gate_o[...] = gate.astype(BF16)
        up_o[...] = up.astype(BF16)
        act_o[...] = (gate * (1.0 / (1.0 + jnp.exp(-gate))) * up).astype(BF16)

    outs = [_sds((s, D_MODEL), BF16), _sds((s, D_MODEL), F32), _sds((s, D_MODEL), BF16), _sds((s, D_FF), BF16),
            _sds((s, D_FF), BF16), _sds((s, D_FF), BF16)]
    consts = [w["g_a"], w["g_b"], w["w_oa"], w["w_ob"], w["g_f"], w["w_gate"], w["w_up"]]
    return _rowwise("fwd_b1", body, [x, o_mla, o_sb], consts, outs, [], ROW_TILE)


def _fwd_b2(h1, act, tgt, w):
    s = h1.shape[0]

    def body(r, c, o, a, step):
        h1_ref, act_ref, t_ref = r
        wd, gn = c
        (dh2_o,) = o
        loss_o, dgn_o = a
        h2 = h1_ref[...] + _dot(act_ref[...], wd[...])
        n2, r2 = _rms(h2)
        err = n2 * gn[...] - t_ref[...]
        part = jnp.sum(jnp.sum(err * err, axis=1, keepdims=True), axis=0, keepdims=True) * (0.5 / D_MODEL)
        _accumulate(loss_o, jnp.broadcast_to(part, (1, LANES)), step)
        dh2, dgn = _rms_bwd(n2, r2, gn[...], err * (1.0 / D_MODEL))
        dh2_o[...] = dh2
        _accumulate(dgn_o, dgn, step)

    return _rowwise("fwd_b2", body, [h1, act, tgt], [w["w_down"], w["g_n"]], [_sds((s, D_MODEL), F32)],
                    [_sds((1, LANES), F32), _sds((1, D_MODEL), F32)], ROW_TILE)


def _bwd_b1(dh2, gate, up, w):
    s = dh2.shape[0]

    def body(r, c, o, a, step):
        dh2_ref, gate_ref, up_ref = r
        (wdt,) = c
        dgate_o, dup_o = o
        dact = _dot(dh2_ref[...].astype(BF16), wdt[...])
        gate = gate_ref[...].astype(F32)
        sig = 1.0 / (1.0 + jnp.exp(-gate))
        dup_o[...] = (dact * (gate * sig)).astype(BF16)
        dgate_o[...] = (dact * up_ref[...].astype(F32) * (sig * (1.0 + gate * (1.0 - sig)))).astype(BF16)

    return _rowwise("bwd_b1", body, [dh2, gate, up], [w["w_down_t"]], [_sds((s, D_FF), BF16), _sds((s, D_FF), BF16)],
                    [], ROW_TILE_ELEMENTWISE)


def _bwd_b2(dgate, dup, h1, dh2, o_mla, o_sb, w):
    s = h1.shape[0]

    def body(r, c, o, a, step):
        dgate_ref, dup_ref, h1_ref, dh2_ref, oa_ref, ob_ref = r
        wgt, wut, gf, woat, wobt, ga, gb = c
        dh1_o, doa_o, dob_o = o
        dgf_o, dga_o, dgb_o = a
        df = _dot(dgate_ref[...], wgt[...]) + _dot(dup_ref[...], wut[...])
        nf, rf = _rms(h1_ref[...])
        dres, dgf = _rms_bwd(nf, rf, gf[...], df)
        dh1 = dh2_ref[...] + dres
        dh1_o[...] = dh1
        dh1b = dh1.astype(BF16)
        na, ra = _rms(oa_ref[...])
        doa, dga = _rms_bwd(na, ra, ga[...], _dot(dh1b, woat[...]))
        nb, rb = _rms(ob_ref[...])
        dob, dgb = _rms_bwd(nb, rb, gb[...], _dot(dh1b, wobt[...]))
        doa_o[...] = doa
        dob_o[...] = dob
        _accumulate(dgf_o, dgf, step)
        _accumulate(dga_o, dga, step)
        _accumulate(dgb_o, dgb, step)

    consts = [w["w_gate_t"], w["w_up_t"], w["g_f"], w["w_oa_t"], w["w_ob_t"], w["g_a"], w["g_b"]]
    outs = [_sds((s, D_MODEL), F32), _sds((s, MLA_WIDTH), F32), _sds((s, SB_WIDTH), F32)]
    accs = [_sds((1, D_MODEL), F32), _sds((1, MLA_WIDTH), F32), _sds((1, SB_WIDTH), F32)]
    return _rowwise("bwd_b2", body, [dgate, dup, h1, dh2, o_mla, o_sb], consts, outs, accs, ROW_TILE)


def _fold_pairs(t):
    return jnp.concatenate([t[:, :LANES] + t[:, LANES:2 * LANES], t[:, 2 * LANES:3 * LANES] + t[:, 3 * LANES:]], axis=1)


def _bwd_a(x, dh1, cq, ckv, dqn, dqr, dkn, dvm, dkr, dsq, dsk, dsv, tabs, w):
    s = x.shape[0]

    def body(r, c, o, a, step):
        x_ref, dh1_ref, cq_ref, ckv_ref, dqn_ref, dqr_ref, dkn_ref, dvm_ref, dkr_ref, dsq_ref, dsk_ref, dsv_ref, cos_ref, sin_ref = r
        wqnt, wqrt, wqrrt, gq, wknt, wvt, gkv, wcqt, wckvt, wkrt, wkrrt, wsqt, wskt, wsvt, gmix = c
        dx_o, a1_o, a2_o, dcq_o, dckv_o, dkrc_o, dkrs_o = o
        dgq_o, dgkv_o, dgmix_o = a
        cos, sin = cos_ref[...], sin_ref[...]
        dqr = _fold_pairs(dqr_ref[...])
        a1 = (dqr * cos).astype(BF16)
        a2 = (dqr * sin).astype(BF16)
        a1_o[...] = a1
        a2_o[...] = a2
        nq, rq = _rms(cq_ref[...])
        dcqn = _dot(dqn_ref[...].astype(BF16), wqnt[...]) + _dot(a1, wqrt[...]) + _dot(a2, wqrrt[...])
        dcq, dgq = _rms_bwd(nq, rq, gq[...], dcqn)
        nkv, rkv = _rms(ckv_ref[...])
        dckvn = _dot((dkn_ref[...] * MLA_DK_SCALE).astype(BF16), wknt[...]) + _dot(dvm_ref[...].astype(BF16), wvt[...])
        dckv, dgkv = _rms_bwd(nkv, rkv, gkv[...], dckvn)
        dkr = _fold_pairs(dkr_ref[...]) * MLA_DK_SCALE
        dcq_b = dcq.astype(BF16)
        dckv_b = dckv.astype(BF16)
        dkrc = (dkr * cos).astype(BF16)
        dkrs = (dkr * sin).astype(BF16)
        dcq_o[...] = dcq_b
        dckv_o[...] = dckv_b
        dkrc_o[...] = dkrc
        dkrs_o[...] = dkrs
        du = (_dot(dcq_b, wcqt[...]) + _dot(dckv_b, wckvt[...]) + _dot(dkrc, wkrt[...]) + _dot(dkrs, wkrrt[...])
              + _dot(dsq_ref[...].astype(BF16), wsqt[...]) + _dot(dsk_ref[...].astype(BF16), wskt[...])
              + _dot(dsv_ref[...].astype(BF16), wsvt[...]))
        nx, rx = _rms(x_ref[...])
        dres, dgmix = _rms_bwd(nx, rx, gmix[...], du)
        dx_o[...] = dh1_ref[...] + dres
        _accumulate(dgq_o, dgq, step)
        _accumulate(dgkv_o, dgkv, step)
        _accumulate(dgmix_o, dgmix, step)

    consts = [w["w_qn_t"], w["w_qr_t"], w["w_qrr_t"], w["g_q"], w["w_kn_t"], w["w_v_t"], w["g_kv"], w["w_cq_t"], w["w_ckv_t"],
              w["w_kr8_t"], w["w_kr8r_t"], w["w_sbq_t"], w["w_sbk_t"], w["w_sbv_t"], w["g_mix"]]
    rope_w = MLA_HEADS * MLA_ROPE
    outs = [_sds((s, D_MODEL), F32), _sds((s, rope_w), BF16), _sds((s, rope_w), BF16), _sds((s, Q_RANK), BF16),
            _sds((s, KV_RANK), BF16), _sds((s, rope_w), BF16), _sds((s, rope_w), BF16)]
    accs = [_sds((1, Q_RANK), F32), _sds((1, KV_RANK), F32), _sds((1, D_MODEL), F32)]
    rows = [x, dh1, cq, ckv, dqn, dqr, dkn, dvm, dkr, dsq, dsk, dsv, tabs["cos"], tabs["sin"]]
    return _rowwise("bwd_a", body, rows, consts, outs, accs, ROW_TILE)


def _tn_multi(name, x, ys):
    s, k = x.shape
    ts = min(TN_TS, s)
    n_y = len(ys)

    def kern(*refs):
        step = pl.program_id(0)
        xb = refs[0][...].astype(BF16)
        for j in range(n_y):
            _accumulate(refs[1 + n_y + j], _dot_tn(xb, refs[1 + j][...].astype(BF16)), step)

    return pl.pallas_call(
        kern, name=name, grid=(s // ts,),
        in_specs=[pl.BlockSpec((ts, k), lambda i: (i, 0))] + [pl.BlockSpec((ts, y.shape[1]), lambda i: (i, 0)) for y in ys],
        out_specs=[pl.BlockSpec((k, y.shape[1]), lambda i: (0, 0)) for y in ys],
        out_shape=[_sds((k, y.shape[1]), F32) for y in ys], compiler_params=_params(1),
    )(x, *ys)


def _tn_tile(k, n):
    if n % LANES or k * n * 4 <= TN_ACC_BYTES:
        return n
    units = n // LANES
    best = 1
    for d in range(1, units + 1):
        if units % d == 0 and k * d * LANES * 4 <= TN_ACC_BYTES:
            best = d
    return best * LANES


def _tn_matmul(name, x, y):
    s, k = x.shape
    n = y.shape[1]
    ts = min(TN_TS, s)
    tn = _tn_tile(k, n)

    def kern(x_ref, y_ref, o_ref):
        step = pl.program_id(1)
        _accumulate(o_ref, _dot_tn(x_ref[...].astype(BF16), y_ref[...].astype(BF16)), step)

    return pl.pallas_call(
        kern, name=name, grid=(n // tn, s // ts),
        in_specs=[pl.BlockSpec((ts, k), lambda j, i: (i, 0)), pl.BlockSpec((ts, tn), lambda j, i: (i, j))],
        out_specs=pl.BlockSpec((k, tn), lambda j, i: (0, j)), out_shape=_sds((k, n), F32), compiler_params=_params(2),
    )(x, y)


def _lanes(rows, lo, width):
    lane = lax.broadcasted_iota(jnp.int32, (rows, LANES), 1)
    return jnp.logical_and(lane >= lo, lane < lo + width)


def _keep(mask, t):
    return jnp.where(mask, t, jnp.zeros_like(t))


def _mla_qcat(qn_ref, qr_ref, rope_lo, half, rows):
    qn = _keep(_lanes(rows, MLA_NOPE * half, MLA_NOPE), qn_ref[...])
    qr = _keep(_lanes(rows, rope_lo, MLA_ROPE), qr_ref[...])
    return jnp.concatenate([qn, qr], axis=1)


def _diag_mask(rows, width, row0, col0):
    row = lax.broadcasted_iota(jnp.int32, (rows, width), 0)
    col = lax.broadcasted_iota(jnp.int32, (rows, width), 1)
    return col + (col0 - row0) <= row


def _mla_fwd(qn, qr, kn, kr, v, tq=MLA_TQ, tk=MLA_TK, td=MLA_TQ):
    s = qn.shape[0]
    tq, tk, td = min(tq, s), min(tk, s), min(td, s)
    ratio = tq // tk

    def kern(qn_ref, qr_ref, kn_ref, kr_ref, v_ref, o_ref, lse_ref):
        g = pl.program_id(0)
        i = pl.program_id(1)
        for half in range(2):
            qcat = _mla_qcat(qn_ref, qr_ref, MLA_ROPE * (2 * (g % 2) + half), half, tq)

            def block(k0, width, carry, row0, masked, qcat=qcat):
                m, l, acc = (c[row0:] for c in carry)
                ks = pl.ds(pl.multiple_of(k0, width), width)
                kcat = jnp.concatenate([kn_ref[ks, :], kr_ref[ks, :]], axis=1)
                sc = _dot_nt(qcat[row0:], kcat)
                if masked:
                    sc = jnp.where(_diag_mask(tq - row0, width, row0, row0), sc, NEG)
                m_new = jnp.maximum(m, jnp.max(sc, axis=1, keepdims=True))
                p = jnp.exp2(sc - m_new)
                alpha = jnp.exp2(m - m_new)
                l = alpha * l + jnp.sum(p, axis=1, keepdims=True)
                acc = alpha * acc + _dot(p.astype(BF16), v_ref[ks, :])
                new = (m_new, l, acc)
                return new if row0 == 0 else tuple(jnp.concatenate([c[:row0], n], axis=0) for c, n in zip(carry, new))

            carry = (jnp.full((tq, 1), NEG, F32), jnp.zeros((tq, 1), F32), jnp.zeros((tq, LANES), F32))
            carry = lax.fori_loop(0, i * ratio, lambda kb, c, block=block: block(kb * tk, tk, c, 0, False), carry)
            for row0 in range(0, tq, td):
                carry = block(i * tq + row0, td, carry, row0, True)
            m, l, acc = carry
            out = _keep(_lanes(tq, MLA_V * half, MLA_V), acc / l)
            lse = _keep(_lanes(tq, MLA_ROPE * half, MLA_ROPE), jnp.broadcast_to(m + jnp.log2(l), (tq, LANES)))
            if half == 0:
                o_ref[...] = out
                lse_ref[...] = lse
            else:
                o_ref[...] += out
                lse_ref[...] += lse

    qblk = pl.BlockSpec((tq, LANES), lambda g, i: (i, g))
    full = pl.BlockSpec((s, LANES), lambda g, i: (0, g))
    return pl.pallas_call(
        kern, name="mla_fwd", grid=(MLA_HEADS // 2, s // tq),
        in_specs=[qblk, pl.BlockSpec((tq, LANES), lambda g, i: (i, g // 2)), full, pl.BlockSpec((s, LANES), lambda g, i: (0, 0)), full],
        out_specs=[qblk, qblk],
        out_shape=[_sds((s, MLA_WIDTH), F32), _sds((s, MLA_HEADS // 2 * LANES), F32)], compiler_params=_params(2),
    )(qn, qr, kn, kr, v)


def _mla_bwd(qn, qr, kn, kr, v, o, do, lse, tq=MLA_TQ, tk=MLA_BWD_TK, td=MLA_DIAG_TK):
    s = qn.shape[0]
    tq, tk, td = min(tq, s), min(tk, s), min(td, s)
    ratio = tq // tk

    def kern(qn_ref, qr_ref, kn_ref, kr_ref, v_ref, o_ref, do_ref, lse_ref, dqn_ref, dqr_ref, dkn_ref, dkr_ref, dv_ref):
        g = pl.program_id(0)
        i = pl.program_id(1)

        @pl.when(i == 0)
        def _():
            dkn_ref[...] = jnp.zeros_like(dkn_ref)
            dkr_ref[...] = jnp.zeros_like(dkr_ref)
            dv_ref[...] = jnp.zeros_like(dv_ref)

        for half in range(2):
            rope_lo = MLA_ROPE * (2 * (g % 2) + half)
            qcat = _mla_qcat(qn_ref, qr_ref, rope_lo, half, tq)
            mine = _lanes(tq, MLA_V * half, MLA_V)
            do_f = _keep(mine, do_ref[...])
            do_b = do_f.astype(BF16)
            delta = jnp.sum(do_f * o_ref[...], axis=1, keepdims=True)
            lse_v = lse_ref[:, MLA_ROPE * half:MLA_ROPE * half + 1]

            def block(k0, width, dq_acc, row0, masked, qcat=qcat, do_b=do_b, delta=delta, lse_v=lse_v):
                ks = pl.ds(pl.multiple_of(k0, width), width)
                kcat = jnp.concatenate([kn_ref[ks, :], kr_ref[ks, :]], axis=1)
                qc, dob = qcat[row0:], do_b[row0:]
                p = jnp.exp2(_dot_nt(qc, kcat) - lse_v[row0:])
                if masked:
                    p = jnp.where(_diag_mask(tq - row0, width, row0, row0), p, 0.0)
                ds = (p * (_dot_nt(dob, v_ref[ks, :]) - delta[row0:])).astype(BF16)
                dv_ref[ks, :] += _dot_tn(p.astype(BF16), dob)
                dkc = _dot_tn(ds, qc)
                dkn_ref[ks, :] += dkc[:, :LANES]
                dkr_ref[ks, :] += dkc[:, LANES:]
                new = dq_acc[row0:] + _dot(ds, kcat)
                return new if row0 == 0 else jnp.concatenate([dq_acc[:row0], new], axis=0)

            acc = lax.fori_loop(0, i * ratio, lambda kb, c, block=block: block(kb * tk, tk, c, 0, False),
                                jnp.zeros((tq, 2 * LANES), F32))
            for row0 in range(0, tq, td):
                acc = block(i * tq + row0, td, acc, row0, True)
            dqn = _keep(_lanes(tq, MLA_NOPE * half, MLA_NOPE), acc[:, :LANES] * MLA_SCALE)
            dqr = _keep(_lanes(tq, rope_lo, MLA_ROPE), acc[:, LANES:] * MLA_SCALE)
            if half == 0:
                dqn_ref[...] = dqn
                dqr_ref[...] = dqr
            else:
                dqn_ref[...] += dqn
                dqr_ref[...] += dqr

    qblk = pl.BlockSpec((tq, LANES), lambda g, i: (i, g))
    full = pl.BlockSpec((s, LANES), lambda g, i: (0, g))
    once = lambda spec_map: pl.BlockSpec((s, LANES), spec_map, pipeline_mode=pl.Buffered(1))
    wide = _sds((s, MLA_HEADS // 2 * LANES), F32)
    return pl.pallas_call(
        kern, name="mla_bwd", grid=(MLA_HEADS // 2, s // tq),
        in_specs=[qblk, pl.BlockSpec((tq, LANES), lambda g, i: (i, g // 2)), once(lambda g, i: (0, g)), once(lambda g, i: (0, 0)),
                  once(lambda g, i: (0, g)), qblk, qblk, qblk],
        out_specs=[qblk, qblk, full, full, full],
        out_shape=[wide, wide, wide, wide, wide], compiler_params=_params(2),
    )(qn, qr, kn, kr, v, o, do, lse)


def _sb_masks(tk):
    j = lax.broadcasted_iota(jnp.int32, (tk, tk), 0)
    c = lax.broadcasted_iota(jnp.int32, (tk, tk), 1)
    return (j > c).astype(BF16), (j < c).astype(BF16)


def _sb_scores(qs, kk, msuf, strict):
    z = _dot_nt(qs, kk)
    lom = -(jnp.maximum(z, 0.0) + jnp.log(1.0 + jnp.exp(-jnp.abs(z))))
    if strict is not None:
        lom = jnp.where(strict, lom, 0.0)
    hi = lom.astype(BF16)
    lo = (lom - hi.astype(F32)).astype(BF16)
    return z, lom, _dot(hi, msuf) + _dot(lo, msuf)


def _sb_strict(tq, tk, d):
    row = lax.broadcasted_iota(jnp.int32, (tq, tk), 0)
    col = lax.broadcasted_iota(jnp.int32, (tq, tk), 1)
    return col + d * tk < row


def _sb_fwd(q, k, v, msuf, tq=SB_TQ, tk=SB_TK):
    s = q.shape[0]
    tq, tk = min(tq, s), min(tk, s)
    ratio = tq // tk

    def kern(q_ref, k_ref, v_ref, m_ref, o_ref, c_ref):
        i = pl.program_id(1)
        msf = m_ref[...]
        lane = lax.broadcasted_iota(jnp.int32, (tq, LANES), 1)
        for half in range(2):
            mine = _lanes(tq, SB_DIM * half, SB_DIM)
            qs = _keep(mine, q_ref[...]) * 0.125

            def block(kb, carry, dd, qs=qs):
                c, acc, cm = carry
                ks = pl.ds(pl.multiple_of(kb * tk, tk), tk)
                strict = None if dd is None else _sb_strict(tq, tk, dd)
                z, lom, suf = _sb_scores(qs, k_ref[ks, :], msf, strict)
                a = jnp.exp(z + lom + (suf + c))
                if strict is not None:
                    a = jnp.where(strict, a, 0.0)
                acc = acc + _dot(a.astype(BF16), v_ref[ks, :])
                cm = jnp.where(lane == kb, c, cm)
                return c + jnp.sum(lom, axis=1, keepdims=True), acc, cm

            carry = (jnp.zeros((tq, 1), F32), jnp.zeros((tq, LANES), F32), jnp.full((tq, LANES), NEG, F32))
            for dd in range(ratio - 1, -1, -1):
                carry = block(i * ratio + dd, carry, dd)

            def live(st):
                return jnp.logical_and(st[0] >= 0, jnp.max(st[1]) > -SB_SKIP)

            def step(st, block=block):
                return (st[0] - 1, *block(st[0], st[1:], None))

            _, _, acc, cm = lax.while_loop(live, step, (i * ratio - 1, *carry))
            if half == 0:
                o_ref[...] = _keep(mine, acc)
            else:
                o_ref[...] += _keep(mine, acc)
            c_ref[:, LANES * half:LANES * (half + 1)] = cm

    qblk = lambda n: pl.BlockSpec((tq, n), lambda g, i: (i, g))
    full = pl.BlockSpec((s, LANES), lambda g, i: (0, g))
    return pl.pallas_call(
        kern, name="sb_fwd", grid=(SB_HEADS // 2, s // tq),
        in_specs=[qblk(LANES), full, full, pl.BlockSpec((tk, tk), lambda g, i: (0, 0))],
        out_specs=[qblk(LANES), qblk(2 * LANES)],
        out_shape=[_sds((s, SB_WIDTH), F32), _sds((s, SB_HEADS * LANES), F32)], compiler_params=_params(2),
    )(q, k, v, msuf)


def _sb_bwd(q, k, v, do, cmat, msuf, mpre, tq=SB_TQ, tk=SB_TK):
    s = q.shape[0]
    tq, tk = min(tq, s), min(tk, s)
    ratio = tq // tk

    def kern(q_ref, k_ref, v_ref, do_ref, c_ref, ms_ref, mp_ref, dq_ref, dk_ref, dv_ref):
        i = pl.program_id(1)

        @pl.when(i == 0)
        def _():
            dk_ref[...] = jnp.zeros_like(dk_ref)
            dv_ref[...] = jnp.zeros_like(dv_ref)

        msf = ms_ref[...]
        mpf = mp_ref[...]
        lane = lax.broadcasted_iota(jnp.int32, (tq, LANES), 1)
        lane1 = lax.broadcasted_iota(jnp.int32, (1, LANES), 1)
        for half in range(2):
            mine = _lanes(tq, SB_DIM * half, SB_DIM)
            qv = _keep(mine, q_ref[...])
            qs = qv * 0.125
            do_b = _keep(mine, do_ref[...]).astype(BF16)
            cm = c_ref[:, LANES * half:LANES * (half + 1)]

            def block(kb, carry, dd, qv=qv, qs=qs, do_b=do_b, cm=cm):
                dq_acc, pc = carry
                ks = pl.ds(pl.multiple_of(kb * tk, tk), tk)
                kk = k_ref[ks, :]
                strict = None if dd is None else _sb_strict(tq, tk, dd)
                z, lom, suf = _sb_scores(qs, kk, msf, strict)
                c = jnp.sum(jnp.where(lane == kb, cm, 0.0), axis=1, keepdims=True)
                a = jnp.exp(z + lom + (suf + c))
                if strict is not None:
                    a = jnp.where(strict, a, 0.0)
                g = _dot_nt(do_b, v_ref[ks, :]) * a
                p = pc + _dot(g.astype(BF16), mpf)
                omb = jnp.exp(lom)
                dz = (g * omb - (1.0 - omb) * p) * 0.125
                if strict is not None:
                    dz = jnp.where(strict, dz, 0.0)
                dz = dz.astype(BF16)
                dv_ref[ks, :] += _dot_tn(a.astype(BF16), do_b)
                dk_ref[ks, :] += _dot_tn(dz, qv)
                return dq_acc + _dot(dz, kk), pc + jnp.sum(g, axis=1, keepdims=True)

            seen = jnp.logical_and(jnp.max(cm, axis=0, keepdims=True) > -SB_SKIP, lane1 < i * ratio)
            first = i * ratio - jnp.sum(seen.astype(jnp.int32))
            carry = (jnp.zeros((tq, LANES), F32), jnp.zeros((tq, 1), F32))
            carry = lax.fori_loop(first, i * ratio, lambda kb, c, block=block: block(kb, c, None), carry)
            for dd in range(ratio):
                carry = block(i * ratio + dd, carry, dd)
            if half == 0:
                dq_ref[...] = _keep(mine, carry[0])
            else:
                dq_ref[...] += _keep(mine, carry[0])

    qblk = lambda n: pl.BlockSpec((tq, n), lambda g, i: (i, g))
    full = pl.BlockSpec((s, LANES), lambda g, i: (0, g))
    msk = pl.BlockSpec((tk, tk), lambda g, i: (0, 0))
    return pl.pallas_call(
        kern, name="sb_bwd", grid=(SB_HEADS // 2, s // tq),
        in_specs=[qblk(LANES), full, full, qblk(LANES), qblk(2 * LANES), msk, msk],
        out_specs=[qblk(LANES), full, full],
        out_shape=[_sds((s, SB_WIDTH), F32)] * 3, compiler_params=_params(2),
    )(q, k, v, do, cmat, msuf, mpre)


def _place():
    return lax.axis_index("x"), lax.axis_index("y"), lax.axis_index("c")


def _other_chips(x, y):
    return [(1 - x, y), (x, 1 - y), (1 - x, 1 - y)]


HBM_SPEC = pl.BlockSpec(memory_space=pl.ANY)


def _allgather_list(name, shards):
    n = len(shards)
    halves = [a.shape[0] // 2 for a in shards]

    def body(*refs):
        ins, outs = refs[:n], refs[n:2 * n]
        send_sems, recv_sems = refs[2 * n:]
        x, y, c = _place()
        sibling = (x, y, 1 - c)
        chips = _other_chips(x, y)

        def half_of(a, ref, pc):
            return ref.at[pl.ds(pl.multiple_of(pc * halves[a], 16), halves[a]), :]

        def copy(a, k, chip, pc, to, src=None):
            dst = half_of(a, outs[a].at[2 * chip[0] + chip[1]], pc)
            return pltpu.make_async_remote_copy(src_ref=dst if src is None else src, dst_ref=dst, send_sem=send_sems.at[6 * a + k],
                                                recv_sem=recv_sems.at[6 * a + k], device_id=to, device_id_type=MESH)

        first = [copy(a, j, (x, y), c, (*chip, c), src=half_of(a, ins[a], c)) for a in range(n) for j, chip in enumerate(chips)]
        for cp in first:
            cp.start()
        passed = []
        for j, chip in enumerate(chips):
            for a in range(n):
                copy(a, j, chip, c, sibling).wait_recv()
                passed.append(copy(a, 3 + j, chip, c, sibling))
                passed[-1].start()
        for j, chip in enumerate(chips):
            for a in range(n):
                copy(a, 3 + j, chip, 1 - c, sibling).wait_recv()
        for cp in first + passed:
            cp.wait_send()

    return pl.pallas_call(
        body, name=name, out_shape=[_sds((N_SHARD,) + a.shape, a.dtype) for a in shards], in_specs=[HBM_SPEC] * n,
        out_specs=[HBM_SPEC] * n, scratch_shapes=[pltpu.SemaphoreType.DMA((6 * n,)), pltpu.SemaphoreType.DMA((6 * n,))],
    )(*shards)


def _swap_halves(gs):
    n = len(gs)

    def body(*refs):
        ins, outs = refs[:n], refs[n:2 * n]
        send_sems, recv_sems = refs[2 * n:]
        x, y, c = _place()
        copies = []
        for a in range(n):
            h = gs[a].shape[1] // 2
            src = ins[a].at[:, pl.ds(pl.multiple_of((1 - c) * h, 8), h), :]
            copies.append(pltpu.make_async_remote_copy(src_ref=src, dst_ref=outs[a], send_sem=send_sems.at[a], recv_sem=recv_sems.at[a],
                                                       device_id=(x, y, 1 - c), device_id_type=MESH))
        for cp in copies:
            cp.start()
        for cp in copies:
            cp.wait()

    return pl.pallas_call(
        body, name="swap_halves", out_shape=[_sds((N_SHARD, g.shape[1] // 2, g.shape[2]), g.dtype) for g in gs],
        in_specs=[HBM_SPEC] * n, out_specs=[HBM_SPEC] * n,
        scratch_shapes=[pltpu.SemaphoreType.DMA((n,)), pltpu.SemaphoreType.DMA((n,))],
    )(*gs)


def _add_sibling(gs, gots, c_idx):
    n = len(gs)

    def kern(c_ref, *refs):
        for a in range(n):
            tot = refs[a][...] + refs[n + a][...]
            refs[2 * n + a][...] = tot
            refs[3 * n + a][...] = tot.astype(BF16)

    quarter = lambda g: (None, g.shape[1] // 4, g.shape[2])
    in_specs = [pl.BlockSpec(quarter(g), lambda b, s, c_ref: (b, 2 * c_ref[0] + s, 0)) for g in gs]
    in_specs += [pl.BlockSpec(quarter(g), lambda b, s, c_ref: (b, s, 0)) for g in gs]
    out_specs = [pl.BlockSpec(quarter(g), lambda b, s, c_ref: (b, s, 0)) for g in gs] * 2
    out_shape = [_sds(t.shape, F32) for t in gots] + [_sds(t.shape, BF16) for t in gots]
    outs = pl.pallas_call(
        kern, name="add_sibling", out_shape=out_shape,
        grid_spec=pltpu.PrefetchScalarGridSpec(num_scalar_prefetch=1, grid=(N_SHARD, 2), in_specs=in_specs, out_specs=out_specs),
        compiler_params=_params(2),
    )(c_idx.reshape(1), *gs, *gots)
    return outs[:n], outs[n:]


def _chip_scatter(ps):
    n = len(ps)

    def body(*refs):
        ins, outs = refs[:n], refs[n:2 * n]
        send_sems, recv_sems = refs[2 * n:]
        x, y, c = _place()
        copies = [pltpu.make_async_remote_copy(
            src_ref=ins[a].at[2 * px + py], dst_ref=outs[a].at[j], send_sem=send_sems.at[3 * a + j], recv_sem=recv_sems.at[3 * a + j],
            device_id=(px, py, c), device_id_type=MESH) for a in range(n) for j, (px, py) in enumerate(_other_chips(x, y))]
        for cp in copies:
            cp.start()
        for cp in copies:
            cp.wait()

    return pl.pallas_call(
        body, name="chip_scatter", out_shape=[_sds((N_SHARD - 1,) + p.shape[1:], p.dtype) for p in ps], in_specs=[HBM_SPEC] * n,
        out_specs=[HBM_SPEC] * n, scratch_shapes=[pltpu.SemaphoreType.DMA((3 * n,)), pltpu.SemaphoreType.DMA((3 * n,))],
    )(*ps)


def _add_chips(ps, others, shard_idx):
    n = len(ps)

    def kern(b_ref, *refs):
        for a in range(n):
            tot = refs[a][...]
            for j in range(N_SHARD - 1):
                tot = tot + refs[n + a][j].astype(F32)
            refs[2 * n + a][...] = tot

    in_specs = [pl.BlockSpec((None, p.shape[1] // 2, p.shape[2]), lambda s, b_ref: (b_ref[0], s, 0)) for p in ps]
    in_specs += [pl.BlockSpec((N_SHARD - 1, p.shape[1] // 2, p.shape[2]), lambda s, b_ref: (0, s, 0)) for p in ps]
    out_specs = [pl.BlockSpec((p.shape[1] // 2, p.shape[2]), lambda s, b_ref: (s, 0)) for p in ps]
    return pl.pallas_call(
        kern, name="add_chips", out_shape=[_sds(p.shape[1:], F32) for p in ps],
        grid_spec=pltpu.PrefetchScalarGridSpec(num_scalar_prefetch=1, grid=(2,), in_specs=in_specs, out_specs=out_specs),
        compiler_params=_params(1),
    )(shard_idx.reshape(1), *ps, *others)


def _swap_result(mines):
    n = len(mines)

    def body(*refs):
        ins, outs = refs[:n], refs[n:2 * n]
        send_sems, recv_sems = refs[2 * n:]
        x, y, c = _place()
        copies = [pltpu.make_async_remote_copy(src_ref=ins[a], dst_ref=outs[a], send_sem=send_sems.at[a], recv_sem=recv_sems.at[a],
                                               device_id=(x, y, 1 - c), device_id_type=MESH) for a in range(n)]
        for cp in copies:
            cp.start()
        for cp in copies:
            cp.wait()

    return pl.pallas_call(
        body, name="swap_result", out_shape=[_sds(m.shape, m.dtype) for m in mines], in_specs=[HBM_SPEC] * n,
        out_specs=[HBM_SPEC] * n, scratch_shapes=[pltpu.SemaphoreType.DMA((n,)), pltpu.SemaphoreType.DMA((n,))],
    )(*mines)


def _allreduce_small(v):
    m_per, n = v.shape

    def body(x_ref, tot_ref, all_ref, send_sems, recv_sems, local_sem):
        x, y, c = _place()
        me, sibling = (x, y, c), (x, y, 1 - c)
        chips = _other_chips(x, y)

        def rows(px, py, pc):
            return all_ref.at[pl.ds(pl.multiple_of((4 * px + 2 * py + pc) * m_per, 8), m_per), :]

        def copy(k, block, to, src=None):
            return pltpu.make_async_remote_copy(
                src_ref=rows(*block) if src is None else src, dst_ref=rows(*block), send_sem=send_sems.at[k],
                recv_sem=recv_sems.at[k], device_id=to, device_id_type=MESH)

        mine = pltpu.make_async_copy(x_ref, rows(*me), local_sem)
        mine.start()
        first = [copy(0, me, sibling, src=x_ref)] + [copy(1 + j, me, (*chip, c), src=x_ref) for j, chip in enumerate(chips)]
        for cp in first:
            cp.start()
        passed = [copy(4 + j, (*chip, c), sibling) for j, chip in enumerate(chips)]
        for j, chip in enumerate(chips):
            copy(1 + j, (*chip, c), me).wait_recv()
            passed[j].start()
        copy(0, sibling, me).wait_recv()
        for j, chip in enumerate(chips):
            copy(4 + j, (*chip, 1 - c), me).wait_recv()
        for cp in first + passed:
            cp.wait_send()
        mine.wait()
        tot = all_ref[0:m_per, :]
        for dev in range(1, 8):
            tot = tot + all_ref[dev * m_per:(dev + 1) * m_per, :]
        tot_ref[...] = tot

    vmem = pl.BlockSpec(memory_space=pltpu.VMEM)
    return pl.pallas_call(
        body, name="allreduce_small", out_shape=_sds((m_per, n), F32), in_specs=[vmem], out_specs=vmem,
        scratch_shapes=[pltpu.VMEM((8 * m_per, n), F32), pltpu.SemaphoreType.DMA((7,)), pltpu.SemaphoreType.DMA((7,)),
                        pltpu.SemaphoreType.DMA],
    )(v)


def _adam_update(w, g, m, v):
    m_new = ADAM_B1 * m + (1.0 - ADAM_B1) * g
    v_new = ADAM_B2 * v + (1.0 - ADAM_B2) * (g * g)
    m_hat = m_new / (1.0 - ADAM_B1 ** ADAM_STEP)
    v_hat = v_new / (1.0 - ADAM_B2 ** ADAM_STEP)
    return -ADAM_LR * (m_hat / (jnp.sqrt(v_hat) + ADAM_EPS) + ADAM_WD * w), m_new, v_new


def _adamw(name, w, g, m, v):
    rows, width = w.shape
    tr = rows // 4 if rows % 32 == 0 else rows

    def kern(w_ref, g_ref, m_ref, v_ref, d_ref, mo_ref, vo_ref):
        d_ref[...], mo_ref[...], vo_ref[...] = _adam_update(w_ref[...], g_ref[...], m_ref[...], v_ref[...])

    spec = pl.BlockSpec((tr, width), lambda i: (i, 0))
    return pl.pallas_call(kern, name=name, grid=(rows // tr,), in_specs=[spec] * 4, out_specs=[spec] * 3,
                          out_shape=[_sds((rows, width), F32)] * 3, compiler_params=_params(1))(w, g, m, v)


def _adamw_halves(name, w, mine, theirs, m, v, c_idx):
    rows, width = w.shape
    tr = rows // 4

    def kern(c_ref, w_ref, mine_ref, theirs_ref, m_ref, v_ref, g_ref, d_ref, mo_ref, vo_ref):
        g = jnp.where(pl.program_id(0) == c_ref[0], mine_ref[...], theirs_ref[...])
        g_ref[...] = g
        d_ref[...], mo_ref[...], vo_ref[...] = _adam_update(w_ref[...], g, m_ref[...], v_ref[...])

    whole = pl.BlockSpec((tr, width), lambda h, j, c_ref: (2 * h + j, 0))
    part = pl.BlockSpec((tr, width), lambda h, j, c_ref: (j, 0))
    return pl.pallas_call(
        kern, name=name, out_shape=[_sds((rows, width), F32)] * 4,
        grid_spec=pltpu.PrefetchScalarGridSpec(num_scalar_prefetch=1, grid=(2, 2), in_specs=[whole, part, part, whole, whole],
                                               out_specs=[whole] * 4),
        compiler_params=_params(2),
    )(c_idx.reshape(1), w, mine, theirs, m, v)


SHARDED = (("w_in", D_MODEL, IN_WIDTH, 1), ("w_uq", Q_RANK, MLA_HEADS * MLA_QK, 1),
           ("w_ukv", KV_RANK, MLA_HEADS * (MLA_NOPE + MLA_V), 1), ("w_o", D_MODEL, D_MODEL, 0),
           ("w_gate", D_MODEL, D_FF, 1), ("w_up", D_MODEL, D_FF, 1), ("w_down", D_FF, D_MODEL, 0))
SMALL = (("norm_mix", D_MODEL), ("q_latent_norm", Q_RANK), ("kv_latent_norm", KV_RANK), ("out_norm_mla", MLA_WIDTH),
         ("out_norm_sb", SB_WIDTH), ("norm_ffn", D_MODEL), ("norm_final", D_MODEL))


def _full_weight(gathered, axis):
    n_sh, k, n = gathered.shape
    return gathered.transpose(1, 0, 2).reshape(k, n_sh * n) if axis == 1 else gathered.reshape(n_sh * k, n)


def _shard_major(g, axis):
    r, c = g.shape
    return g.reshape(r, N_SHARD, c // N_SHARD).transpose(1, 0, 2) if axis == 1 else g.reshape(N_SHARD, r // N_SHARD, c)


def _rot_cols(w):
    hh = MLA_ROPE // 2
    return jnp.concatenate([-w[..., hh:], w[..., :hh]], axis=-1)


def _rot_cols_t(g):
    hh = MLA_ROPE // 2
    return jnp.concatenate([g[..., hh:], -g[..., :hh]], axis=-1)


def _prepare_weights(full, small):
    w_in = full["w_in"]
    s0, s1, s2 = Q_RANK, Q_RANK + KV_RANK, Q_RANK + KV_RANK + MLA_ROPE
    uq = full["w_uq"].reshape(Q_RANK, MLA_HEADS, MLA_QK)
    ukv = full["w_ukv"].reshape(KV_RANK, MLA_HEADS, MLA_NOPE + MLA_V)
    w_kr = w_in[:, s1:s2]
    per_tile = ROPE_TILE // MLA_ROPE
    w = {
        "w_cq": w_in[:, :s0], "w_ckv": w_in[:, s0:s1],
        "w_kr4": jnp.tile(w_kr, (1, per_tile)), "w_kr4r": jnp.tile(_rot_cols(w_kr), (1, per_tile)),
        "w_kr8": jnp.tile(w_kr, (1, MLA_HEADS)), "w_kr8r": jnp.tile(_rot_cols(w_kr), (1, MLA_HEADS)),
        "w_sbq": w_in[:, s2:s2 + SB_WIDTH], "w_sbk": w_in[:, s2 + SB_WIDTH:s2 + 2 * SB_WIDTH], "w_sbv": w_in[:, s2 + 2 * SB_WIDTH:],
        "w_qn": uq[..., :MLA_NOPE].reshape(Q_RANK, -1), "w_qr": uq[..., MLA_NOPE:].reshape(Q_RANK, -1),
        "w_qrr": _rot_cols(uq[..., MLA_NOPE:]).reshape(Q_RANK, -1),
        "w_kn": ukv[..., :MLA_NOPE].reshape(KV_RANK, -1), "w_v": ukv[..., MLA_NOPE:].reshape(KV_RANK, -1),
        "w_oa": full["w_o"][:MLA_WIDTH], "w_ob": full["w_o"][MLA_WIDTH:],
        "w_gate": full["w_gate"], "w_up": full["w_up"], "w_down": full["w_down"],
    }
    for name in list(w):
        w[name + "_t"] = w[name].T
    w.update(g_mix=small["norm_mix"], g_q=small["q_latent_norm"], g_kv=small["kv_latent_norm"], g_a=small["out_norm_mla"],
             g_b=small["out_norm_sb"], g_f=small["norm_ffn"], g_n=small["norm_final"])
    return w


def _rope_tables(positions):
    inv_freq = ROPE_THETA ** (-jnp.arange(0, MLA_ROPE, 2, dtype=F32) / MLA_ROPE)
    ang = positions.astype(F32)[:, None] * inv_freq[None, :]
    cos, sin = jnp.cos(ang), jnp.sin(ang)
    return {"cos": jnp.tile(jnp.concatenate([cos, cos], axis=1), (1, MLA_HEADS)),
            "sin": jnp.tile(jnp.concatenate([sin, sin], axis=1), (1, MLA_HEADS))}


def _by_head(g_wide, g_narrow, wide, narrow):
    r = g_wide.shape[0]
    return jnp.concatenate([g_wide.reshape(r, MLA_HEADS, wide), g_narrow.reshape(r, MLA_HEADS, narrow)], axis=-1).reshape(r, -1)


def kernel(x, positions, norm_mix, w_in, q_latent_norm, w_uq, kv_latent_norm, w_ukv, out_norm_mla, out_norm_sb, w_o, norm_ffn, w_gate, w_up, w_down, norm_final, loss_target, m_norm_mix, m_w_in, m_q_latent_norm, m_w_uq, m_kv_latent_norm, m_w_ukv, m_out_norm_mla, m_out_norm_sb, m_w_o, m_norm_ffn, m_w_gate, m_w_up, m_w_down, m_norm_final, v_norm_mix, v_w_in, v_q_latent_norm, v_w_uq, v_kv_latent_norm, v_w_ukv, v_out_norm_mla, v_out_norm_sb, v_w_o, v_norm_ffn, v_w_gate, v_w_up, v_w_down, v_norm_final):
    given = dict(norm_mix=norm_mix, w_in=w_in, q_latent_norm=q_latent_norm, w_uq=w_uq, kv_latent_norm=kv_latent_norm, w_ukv=w_ukv,
                 out_norm_mla=out_norm_mla, out_norm_sb=out_norm_sb, w_o=w_o, norm_ffn=norm_ffn, w_gate=w_gate, w_up=w_up,
                 w_down=w_down, norm_final=norm_final)
    mom_m = dict(norm_mix=m_norm_mix, w_in=m_w_in, q_latent_norm=m_q_latent_norm, w_uq=m_w_uq, kv_latent_norm=m_kv_latent_norm,
                 w_ukv=m_w_ukv, out_norm_mla=m_out_norm_mla, out_norm_sb=m_out_norm_sb, w_o=m_w_o, norm_ffn=m_norm_ffn,
                 w_gate=m_w_gate, w_up=m_w_up, w_down=m_w_down, norm_final=m_norm_final)
    mom_v = dict(norm_mix=v_norm_mix, w_in=v_w_in, q_latent_norm=v_q_latent_norm, w_uq=v_w_uq, kv_latent_norm=v_kv_latent_norm,
                 w_ukv=v_w_ukv, out_norm_mla=v_out_norm_mla, out_norm_sb=v_out_norm_sb, w_o=v_w_o, norm_ffn=v_norm_ffn,
                 w_gate=v_w_gate, w_up=v_w_up, w_down=v_w_down, norm_final=v_norm_final)
    xs = x[0]
    tgt = loss_target[0]
    s = xs.shape[0]
    c_idx = lax.axis_index("c")
    shard_idx = 2 * lax.axis_index("x") + lax.axis_index("y")

    shard2d = {name: given[name].reshape(given[name].shape[-2:]) for name, *_ in SHARDED}
    gathered = _allgather_list("allgather_w", [shard2d[name].astype(BF16) for name, *_ in SHARDED])
    is_mine = (jnp.arange(N_SHARD) == shard_idx)[:, None, None]
    full = {name: _full_weight(jnp.where(is_mine, shard2d[name].astype(BF16)[None], t), axis)
            for (name, _, _, axis), t in zip(SHARDED, gathered)}
    small = {name: given[name].reshape(1, n) for name, n in SMALL}
    w = _prepare_weights(full, small)
    tabs = _rope_tables(positions[0])
    msuf, mpre = _sb_masks(min(SB_TK, s))

    u, cq, ckv, cqn, ckvn, qn, qr, kn, vm, kr, sq, sk, sv = _fwd_a(xs, tabs, w)
    o_mla, lse = _mla_fwd(qn, qr, kn, kr, vm)
    o_sb, cmat = _sb_fwd(sq, sk, sv, msuf)
    merged, h1, f, gate, up, act = _fwd_b1(xs, o_mla, o_sb, w)
    dh2, loss_part, dg_n = _fwd_b2(h1, act, tgt, w)

    dgate, dup = _bwd_b1(dh2, gate, up, w)
    dh1, do_mla, do_sb, dg_f, dg_a, dg_b = _bwd_b2(dgate, dup, h1, dh2, o_mla, o_sb, w)
    dqn, dqr, dkn, dkr, dvm = _mla_bwd(qn, qr, kn, kr, vm, o_mla, do_mla, lse)
    dsq, dsk, dsv = _sb_bwd(sq, sk, sv, do_sb, cmat, msuf, mpre)
    dx, a1, a2, dcq, dckv, dkrc, dkrs, dg_q, dg_kv, dg_mix = _bwd_a(xs, dh1, cq, ckv, dqn, dqr, dkn, dvm, dkr, dsq, dsk, dsv, tabs, w)

    g_cq, g_ckv, g_krc, g_krs, g_sq, g_sk, g_sv = _tn_multi("dw_in", u, [dcq, dckv, dkrc, dkrs, dsq, dsk, dsv])
    g_qn, g_qr1, g_qr2 = _tn_multi("dw_uq", cqn, [dqn, a1, a2])
    g_kn, g_v = _tn_multi("dw_ukv", ckvn, [dkn, dvm])
    slots = lambda g: g.reshape(g.shape[0], MLA_HEADS, MLA_ROPE)
    g_kr = jnp.sum(slots(g_krc), axis=1) + _rot_cols_t(jnp.sum(slots(g_krs), axis=1))
    g_qr = (slots(g_qr1) + _rot_cols_t(slots(g_qr2))).reshape(Q_RANK, -1)
    grads = {
        "w_in": jnp.concatenate([g_cq, g_ckv, g_kr, g_sq, g_sk, g_sv], axis=1),
        "w_uq": _by_head(g_qn, g_qr, MLA_NOPE, MLA_ROPE),
        "w_ukv": _by_head(g_kn * MLA_DK_SCALE, g_v, MLA_NOPE, MLA_V),
        "w_o": _tn_matmul("dw_o", merged, dh1),
        "w_gate": _tn_matmul("dw_gate", f, dgate),
        "w_up": _tn_matmul("dw_up", f, dup),
        "w_down": _tn_matmul("dw_down", act, dh2),
    }

    gs = [_shard_major(grads[name], axis) for name, _, _, axis in SHARDED]
    chip_f32, chip_bf16 = _add_sibling(gs, _swap_halves(gs), c_idx)
    mine = _add_chips(chip_f32, _chip_scatter(chip_bf16), shard_idx)
    theirs = _swap_result(mine)

    small_parts = jnp.concatenate([dg_mix, dg_q, dg_kv, dg_a, dg_b, dg_f, dg_n], axis=1)
    small_g = _allreduce_small(jnp.broadcast_to(small_parts, (8, small_parts.shape[1])))[0:1]
    loss = lax.psum(loss_part[0, 0], ("x", "y", "c"))

    g_out, d_out, m_out, v_out = {}, {}, {}, {}
    for (name, *_), g_mine, g_theirs in zip(SHARDED, mine, theirs):
        shape = given[name].shape
        outs = _adamw_halves("adamw_" + name, shard2d[name], g_mine, g_theirs, mom_m[name].reshape(shard2d[name].shape),
                             mom_v[name].reshape(shard2d[name].shape), c_idx)
        g_out[name], d_out[name], m_out[name], v_out[name] = (t.reshape(shape) for t in outs)
    cat = lambda src: jnp.concatenate([src[name].reshape(1, n) for name, n in SMALL], axis=1)
    d, mn, vn = _adamw("adamw_small", cat(given), small_g, cat(mom_m), cat(mom_v))
    off = 0
    for name, n in SMALL:
        shape = given[name].shape
        g_out[name], d_out[name], m_out[name], v_out[name] = (t[:, off:off + n].reshape(shape) for t in (small_g, d, mn, vn))
        off += n

    order = ["norm_mix", "w_in", "q_latent_norm", "w_uq", "kv_latent_norm", "w_ukv", "out_norm_mla", "out_norm_sb", "w_o",
             "norm_ffn", "w_gate", "w_up", "w_down", "norm_final"]
    return (loss, dx[None], *[g_out[n] for n in order], *[d_out[n] for n in order], *[m_out[n] for n in order],
            *[v_out[n] for n in order])
```

```python
import functools
import math

import jax
import jax.numpy as jnp
from jax import lax
from jax.experimental import pallas as pl
from jax.experimental.pallas import tpu as pltpu

F32 = jnp.float32
BF16 = jnp.bfloat16
MESH = pl.DeviceIdType.MESH

D_MODEL = 1024
EPS = 1e-6
MLA_HEADS = 8
MLA_NOPE = 64
MLA_ROPE = 32
MLA_V = 64
MLA_QK = MLA_NOPE + MLA_ROPE
Q_RANK = 256
KV_RANK = 128
ROPE_THETA = 10000.0
SB_HEADS = 8
SB_DIM = 64
MLA_WIDTH = MLA_HEADS * MLA_V
SB_WIDTH = SB_HEADS * SB_DIM
D_FF = 2816
IN_WIDTH = Q_RANK + KV_RANK + MLA_ROPE + 3 * SB_WIDTH

ADAM_LR = 0.001
ADAM_B1 = 0.9
ADAM_B2 = 0.999
ADAM_EPS = 1e-08
ADAM_WD = 0.01
ADAM_STEP = 10

N_SHARD = 4
LANES = 128
ROPE_TILE = LANES
VMEM_LIMIT = 56 * 1024 * 1024
TN_ACC_BYTES = 6 * 1024 * 1024 + 512 * 1024
NEG = -1e30
MLA_SCALE = 1.0 / math.sqrt(MLA_QK)
MLA_DK_SCALE = math.log(2.0)
MLA_QSCALE = MLA_SCALE * math.log2(math.e)
SB_SKIP = 110.0

ROW_TILE = 512
ROW_TILE_ELEMENTWISE = 256
MLA_TQ = 1024
SB_TQ = 512
MLA_TK = 1024
MLA_BWD_TK = 512
MLA_DIAG_TK = 512
SB_TK = 256
TN_TS = 2048


def _dot(a, b):
    return jnp.dot(a, b, preferred_element_type=F32)


def _dot_nt(a, b):
    return lax.dot_general(a, b, (((1,), (1,)), ((), ())), preferred_element_type=F32)


def _dot_tn(a, b):
    return lax.dot_general(a, b, (((0,), (0,)), ((), ())), preferred_element_type=F32)


def _params(n_grid, vmem=VMEM_LIMIT):
    return pltpu.CompilerParams(dimension_semantics=("arbitrary",) * n_grid, vmem_limit_bytes=vmem)


def _rms(x):
    r = lax.rsqrt(jnp.mean(x * x, axis=-1, keepdims=True) + EPS)
    return x * r, r


def _rms_bwd(n, r, g, dy):
    dn = dy * g
    dx = r * (dn - n * jnp.mean(dn * n, axis=-1, keepdims=True))
    return dx, jnp.sum(dy * n, axis=0, keepdims=True)


def _accumulate(ref, val, step):
    @pl.when(step == 0)
    def _():
        ref[...] = val

    @pl.when(step != 0)
    def _():
        ref[...] += val


def _rowwise(name, body, rows, consts, row_out, acc_out, tm):
    n_rows = rows[0].shape[0]
    tm = min(tm, n_rows)
    nr, nc, no = len(rows), len(consts), len(row_out)

    def kern(*refs):
        body(refs[:nr], refs[nr:nr + nc], refs[nr + nc:nr + nc + no], refs[nr + nc + no:], pl.program_id(0))

    in_specs = [pl.BlockSpec((tm, a.shape[1]), lambda i: (i, 0)) for a in rows]
    in_specs += [pl.BlockSpec(a.shape, lambda i: (0, 0), pipeline_mode=pl.Buffered(1)) for a in consts]
    out_specs = [pl.BlockSpec((tm, s.shape[1]), lambda i: (i, 0)) for s in row_out]
    out_specs += [pl.BlockSpec(s.shape, lambda i: (0, 0)) for s in acc_out]
    return pl.pallas_call(
        kern, name=name, grid=(n_rows // tm,), in_specs=in_specs, out_specs=out_specs,
        out_shape=list(row_out) + list(acc_out), compiler_params=_params(1),
    )(*rows, *consts)


def _sds(shape, dtype):
    return jax.ShapeDtypeStruct(shape, dtype)


def _fwd_a(x, tabs, w):
    s = x.shape[0]

    def body(r, c, o, a, step):
        x_ref, cos_ref, sin_ref = r
        gmix, wcq, wckv, wkr, wkrr, wsq, wsk, wsv, gq, wqn, wqr, wqrr, gkv, wkn, wv = c
        u_o, cq_o, ckv_o, cqn_o, ckvn_o, qn_o, qr_o, kn_o, v_o, kr_o, sq_o, sk_o, sv_o = o
        cos, sin = cos_ref[...], sin_ref[...]
        n, _ = _rms(x_ref[...])
        u = (n * gmix[...]).astype(BF16)
        u_o[...] = u
        cq = _dot(u, wcq[...])
        ckv = _dot(u, wckv[...])
        kr_o[...] = (_dot(u, wkr[...]) * cos[:, :ROPE_TILE] + _dot(u, wkrr[...]) * sin[:, :ROPE_TILE]).astype(BF16)
        sq_o[...] = _dot(u, wsq[...]).astype(BF16)
        sk_o[...] = _dot(u, wsk[...]).astype(BF16)
        sv_o[...] = _dot(u, wsv[...]).astype(BF16)
        cq_o[...] = cq
        ckv_o[...] = ckv
        nq, _ = _rms(cq)
        cqn = (nq * gq[...]).astype(BF16)
        cqn_o[...] = cqn
        qn_o[...] = (_dot(cqn, wqn[...]) * MLA_QSCALE).astype(BF16)
        qr_o[...] = ((_dot(cqn, wqr[...]) * cos + _dot(cqn, wqrr[...]) * sin) * MLA_QSCALE).astype(BF16)
        nkv, _ = _rms(ckv)
        ckvn = (nkv * gkv[...]).astype(BF16)
        ckvn_o[...] = ckvn
        kn_o[...] = _dot(ckvn, wkn[...]).astype(BF16)
        v_o[...] = _dot(ckvn, wv[...]).astype(BF16)

    outs = [
        _sds((s, D_MODEL), BF16), _sds((s, Q_RANK), F32), _sds((s, KV_RANK), F32), _sds((s, Q_RANK), BF16),
        _sds((s, KV_RANK), BF16), _sds((s, MLA_HEADS * MLA_NOPE), BF16), _sds((s, MLA_HEADS * MLA_ROPE), BF16),
        _sds((s, MLA_HEADS * MLA_NOPE), BF16), _sds((s, MLA_WIDTH), BF16), _sds((s, ROPE_TILE), BF16),
        _sds((s, SB_WIDTH), BF16), _sds((s, SB_WIDTH), BF16), _sds((s, SB_WIDTH), BF16),
    ]
    consts = [w["g_mix"], w["w_cq"], w["w_ckv"], w["w_kr4"], w["w_kr4r"], w["w_sbq"], w["w_sbk"], w["w_sbv"], w["g_q"],
              w["w_qn"], w["w_qr"], w["w_qrr"], w["g_kv"], w["w_kn"], w["w_v"]]
    return _rowwise("fwd_a", body, [x, tabs["cos"], tabs["sin"]], consts, outs, [], ROW_TILE)


def _fwd_b1(x, o_mla, o_sb, w):
    s = x.shape[0]

    def body(r, c, o, a, step):
        x_ref, oa_ref, ob_ref = r
        ga, gb, woa, wob, gf, wg, wu = c
        mg_o, h1_o, f_o, gate_o, up_o, act_o = o
        na, _ = _rms(oa_ref[...])
        nb, _ = _rms(ob_ref[...])
        ma = (na * ga[...]).astype(BF16)
        mb = (nb * gb[...]).astype(BF16)
        mg_o[:, :MLA_WIDTH] = ma
        mg_o[:, MLA_WIDTH:] = mb
        h1 = x_ref[...] + _dot(ma, woa[...]) + _dot(mb, wob[...])
        h1_o[...] = h1
        nf, _ = _rms(h1)
        f = (nf * gf[...]).astype(BF16)
        f_o[...] = f
        gate = _dot(f, wg[...])
        up = _dot(f, wu[...])
        gate_o[...] = gate.astype(BF16)
        up_o[...] = up.astype(BF16)
        act_o[...] = (gate * (1.0 / (1.0 + jnp.exp(-gate))) * up).astype(BF16)

    outs = [_sds((s, D_MODEL), BF16), _sds((s, D_MODEL), F32), _sds((s, D_MODEL), BF16), _sds((s, D_FF), BF16),
            _sds((s, D_FF), BF16), _sds((s, D_FF), BF16)]
    consts = [w["g_a"], w["g_b"], w["w_oa"], w["w_ob"], w["g_f"], w["w_gate"], w["w_up"]]
    return _rowwise("fwd_b1", body, [x, o_mla, o_sb], consts, outs, [], ROW_TILE)


def _fwd_b2(h1, act, tgt, w):
    s = h1.shape[0]

    def body(r, c, o, a, step):
        h1_ref, act_ref, t_ref = r
        wd, gn = c
        (dh2_o,) = o
        loss_o, dgn_o = a
        h2 = h1_ref[...] + _dot(act_ref[...], wd[...])
        n2, r2 = _rms(h2)
        err = n2 * gn[...] - t_ref[...]
        part = jnp.sum(jnp.sum(err * err, axis=1, keepdims=True), axis=0, keepdims=True) * (0.5 / D_MODEL)
        _accumulate(loss_o, jnp.broadcast_to(part, (1, LANES)), step)
        dh2, dgn = _rms_bwd(n2, r2, gn[...], err * (1.0 / D_MODEL))
        dh2_o[...] = dh2
        _accumulate(dgn_o, dgn, step)

    return _rowwise("fwd_b2", body, [h1, act, tgt], [w["w_down"], w["g_n"]], [_sds((s, D_MODEL), F32)],
                    [_sds((1, LANES), F32), _sds((1, D_MODEL), F32)], ROW_TILE)


def _bwd_b1(dh2, gate, up, w):
    s = dh2.shape[0]

    def body(r, c, o, a, step):
        dh2_ref, gate_ref, up_ref = r
        (wdt,) = c
        dgate_o, dup_o = o
        dact = _dot(dh2_ref[...].astype(BF16), wdt[...])
        gate = gate_ref[...].astype(F32)
        sig = 1.0 / (1.0 + jnp.exp(-gate))
        dup_o[...] = (dact * (gate * sig)).astype(BF16)
        dgate_o[...] = (dact * up_ref[...].astype(F32) * (sig * (1.0 + gate * (1.0 - sig)))).astype(BF16)

    return _rowwise("bwd_b1", body, [dh2, gate, up], [w["w_down_t"]], [_sds((s, D_FF), BF16), _sds((s, D_FF), BF16)],
                    [], ROW_TILE_ELEMENTWISE)


def _bwd_b2(dgate, dup, h1, dh2, o_mla, o_sb, w):
    s = h1.shape[0]

    def body(r, c, o, a, step):
        dgate_ref, dup_ref, h1_ref, dh2_ref, oa_ref, ob_ref = r
        wgt, wut, gf, woat, wobt, ga, gb = c
        dh1_o, doa_o, dob_o = o
        dgf_o, dga_o, dgb_o = a
        df = _dot(dgate_ref[...], wgt[...]) + _dot(dup_ref[...], wut[...])
        nf, rf = _rms(h1_ref[...])
        dres, dgf = _rms_bwd(nf, rf, gf[...], df)
        dh1 = dh2_ref[...] + dres
        dh1_o[...] = dh1
        dh1b = dh1.astype(BF16)
        na, ra = _rms(oa_ref[...])
        doa, dga = _rms_bwd(na, ra, ga[...], _dot(dh1b, woat[...]))
        nb, rb = _rms(ob_ref[...])
        dob, dgb = _rms_bwd(nb, rb, gb[...], _dot(dh1b, wobt[...]))
        doa_o[...] = doa
        dob_o[...] = dob
        _accumulate(dgf_o, dgf, step)
        _accumulate(dga_o, dga, step)
        _accumulate(dgb_o, dgb, step)

    consts = [w["w_gate_t"], w["w_up_t"], w["g_f"], w["w_oa_t"], w["w_ob_t"], w["g_a"], w["g_b"]]
    outs = [_sds((s, D_MODEL), F32), _sds((s, MLA_WIDTH), F32), _sds((s, SB_WIDTH), F32)]
    accs = [_sds((1, D_MODEL), F32), _sds((1, MLA_WIDTH), F32), _sds((1, SB_WIDTH), F32)]
    return _rowwise("bwd_b2", body, [dgate, dup, h1, dh2, o_mla, o_sb], consts, outs, accs, ROW_TILE)


def _fold_pairs(t):
    return jnp.concatenate([t[:, :LANES] + t[:, LANES:2 * LANES], t[:, 2 * LANES:3 * LANES] + t[:, 3 * LANES:]], axis=1)


def _bwd_a(x, dh1, cq, ckv, dqn, dqr, dkn, dvm, dkr, dsq, dsk, dsv, tabs, w):
    s = x.shape[0]

    def body(r, c, o, a, step):
        x_ref, dh1_ref, cq_ref, ckv_ref, dqn_ref, dqr_ref, dkn_ref, dvm_ref, dkr_ref, dsq_ref, dsk_ref, dsv_ref, cos_ref, sin_ref = r
        wqnt, wqrt, wqrrt, gq, wknt, wvt, gkv, wcqt, wckvt, wkrt, wkrrt, wsqt, wskt, wsvt, gmix = c
        dx_o, a1_o, a2_o, dcq_o, dckv_o, dkrc_o, dkrs_o = o
        dgq_o, dgkv_o, dgmix_o = a
        cos, sin = cos_ref[...], sin_ref[...]
        dqr = _fold_pairs(dqr_ref[...])
        a1 = (dqr * cos).astype(BF16)
        a2 = (dqr * sin).astype(BF16)
        a1_o[...] = a1
        a2_o[...] = a2
        nq, rq = _rms(cq_ref[...])
        dcqn = _dot(dqn_ref[...].astype(BF16), wqnt[...]) + _dot(a1, wqrt[...]) + _dot(a2, wqrrt[...])
        dcq, dgq = _rms_bwd(nq, rq, gq[...], dcqn)
        nkv, rkv = _rms(ckv_ref[...])
        dckvn = _dot((dkn_ref[...] * MLA_DK_SCALE).astype(BF16), wknt[...]) + _dot(dvm_ref[...].astype(BF16), wvt[...])
        dckv, dgkv = _rms_bwd(nkv, rkv, gkv[...], dckvn)
        dkr = _fold_pairs(dkr_ref[...]) * MLA_DK_SCALE
        dcq_b = dcq.astype(BF16)
        dckv_b = dckv.astype(BF16)
        dkrc = (dkr * cos).astype(BF16)
        dkrs = (dkr * sin).astype(BF16)
        dcq_o[...] = dcq_b
        dckv_o[...] = dckv_b
        dkrc_o[...] = dkrc
        dkrs_o[...] = dkrs
        du = (_dot(dcq_b, wcqt[...]) + _dot(dckv_b, wckvt[...]) + _dot(dkrc, wkrt[...]) + _dot(dkrs, wkrrt[...])
              + _dot(dsq_ref[...].astype(BF16), wsqt[...]) + _dot(dsk_ref[...].astype(BF16), wskt[...])
              + _dot(dsv_ref[...].astype(BF16), wsvt[...]))
        nx, rx = _rms(x_ref[...])
        dres, dgmix = _rms_bwd(nx, rx, gmix[...], du)
        dx_o[...] = dh1_ref[...] + dres
        _accumulate(dgq_o, dgq, step)
        _accumulate(dgkv_o, dgkv, step)
        _accumulate(dgmix_o, dgmix, step)

    consts = [w["w_qn_t"], w["w_qr_t"], w["w_qrr_t"], w["g_q"], w["w_kn_t"], w["w_v_t"], w["g_kv"], w["w_cq_t"], w["w_ckv_t"],
              w["w_kr8_t"], w["w_kr8r_t"], w["w_sbq_t"], w["w_sbk_t"], w["w_sbv_t"], w["g_mix"]]
    rope_w = MLA_HEADS * MLA_ROPE
    outs = [_sds((s, D_MODEL), F32), _sds((s, rope_w), BF16), _sds((s, rope_w), BF16), _sds((s, Q_RANK), BF16),
            _sds((s, KV_RANK), BF16), _sds((s, rope_w), BF16), _sds((s, rope_w), BF16)]
    accs = [_sds((1, Q_RANK), F32), _sds((1, KV_RANK), F32), _sds((1, D_MODEL), F32)]
    rows = [x, dh1, cq, ckv, dqn, dqr, dkn, dvm, dkr, dsq, dsk, dsv, tabs["cos"], tabs["sin"]]
    return _rowwise("bwd_a", body, rows, consts, outs, accs, ROW_TILE)


def _tn_multi(name, x, ys):
    s, k = x.shape
    ts = min(TN_TS, s)
    n_y = len(ys)

    def kern(*refs):
        step = pl.program_id(0)
        xb = refs[0][...].astype(BF16)
        for j in range(n_y):
            _accumulate(refs[1 + n_y + j], _dot_tn(xb, refs[1 + j][...].astype(BF16)), step)

    return pl.pallas_call(
        kern, name=name, grid=(s // ts,),
        in_specs=[pl.BlockSpec((ts, k), lambda i: (i, 0))] + [pl.BlockSpec((ts, y.shape[1]), lambda i: (i, 0)) for y in ys],
        out_specs=[pl.BlockSpec((k, y.shape[1]), lambda i: (0, 0)) for y in ys],
        out_shape=[_sds((k, y.shape[1]), F32) for y in ys], compiler_params=_params(1),
    )(x, *ys)


def _tn_tile(k, n):
    if n % LANES or k * n * 4 <= TN_ACC_BYTES:
        return n
    units = n // LANES
    best = 1
    for d in range(1, units + 1):
        if units % d == 0 and k * d * LANES * 4 <= TN_ACC_BYTES:
            best = d
    return best * LANES


def _tn_matmul(name, x, y):
    s, k = x.shape
    n = y.shape[1]
    ts = min(TN_TS, s)
    tn = _tn_tile(k, n)

    def kern(x_ref, y_ref, o_ref):
        step = pl.program_id(1)
        _accumulate(o_ref, _dot_tn(x_ref[...].astype(BF16), y_ref[...].astype(BF16)), step)

    return pl.pallas_call(
        kern, name=name, grid=(n // tn, s // ts),
        in_specs=[pl.BlockSpec((ts, k), lambda j, i: (i, 0)), pl.BlockSpec((ts, tn), lambda j, i: (i, j))],
        out_specs=pl.BlockSpec((k, tn), lambda j, i: (0, j)), out_shape=_sds((k, n), F32), compiler_params=_params(2),
    )(x, y)


def _lanes(rows, lo, width):
    lane = lax.broadcasted_iota(jnp.int32, (rows, LANES), 1)
    return jnp.logical_and(lane >= lo, lane < lo + width)


def _keep(mask, t):
    return jnp.where(mask, t, jnp.zeros_like(t))


def _mla_qcat(qn_ref, qr_ref, rope_lo, half, rows):
    qn = _keep(_lanes(rows, MLA_NOPE * half, MLA_NOPE), qn_ref[...])
    qr = _keep(_lanes(rows, rope_lo, MLA_ROPE), qr_ref[...])
    return jnp.concatenate([qn, qr], axis=1)


def _diag_mask(rows, width, row0, col0):
    row = lax.broadcasted_iota(jnp.int32, (rows, width), 0)
    col = lax.broadcasted_iota(jnp.int32, (rows, width), 1)
    return col + (col0 - row0) <= row


def _mla_fwd(qn, qr, kn, kr, v, riders=(), tq=MLA_TQ, tk=MLA_TK, td=MLA_TQ):
    s = qn.shape[0]
    tq, tk, td = min(tq, s), min(tk, s), min(td, s)
    ratio = tq // tk

    n_ride = len(riders)
    n_pairs = MLA_HEADS // 2

    def kern(qn_ref, qr_ref, kn_ref, kr_ref, v_ref, *rest):
        o_ref, lse_ref = rest[n_ride:n_ride + 2]
        g = pl.program_id(0)
        i = pl.program_id(1)
        if n_ride:
            send, forward, finish = _gather_steps(rest[:n_ride], rest[n_ride + 2:2 * n_ride + 2], *rest[2 * n_ride + 2:])
            pl.when(jnp.logical_and(g == 0, i == 0))(send)
            pl.when(jnp.logical_and(g == 1, i == 0))(forward)
        for half in range(2):
            qcat = _mla_qcat(qn_ref, qr_ref, MLA_ROPE * (2 * (g % 2) + half), half, tq)

            def block(k0, width, carry, row0, masked, qcat=qcat):
                m, l, acc = (c[row0:] for c in carry)
                ks = pl.ds(pl.multiple_of(k0, width), width)
                kcat = jnp.concatenate([kn_ref[ks, :], kr_ref[ks, :]], axis=1)
                sc = _dot_nt(qcat[row0:], kcat)
                if masked:
                    sc = jnp.where(_diag_mask(tq - row0, width, row0, row0), sc, NEG)
                m_new = jnp.maximum(m, jnp.max(sc, axis=1, keepdims=True))
                p = jnp.exp2(sc - m_new)
                alpha = jnp.exp2(m - m_new)
                l = alpha * l + jnp.sum(p, axis=1, keepdims=True)
                acc = alpha * acc + _dot(p.astype(BF16), v_ref[ks, :])
                new = (m_new, l, acc)
                return new if row0 == 0 else tuple(jnp.concatenate([c[:row0], n], axis=0) for c, n in zip(carry, new))

            carry = (jnp.full((tq, 1), NEG, F32), jnp.zeros((tq, 1), F32), jnp.zeros((tq, LANES), F32))
            carry = lax.fori_loop(0, i * ratio, lambda kb, c, block=block: block(kb * tk, tk, c, 0, False), carry)
            for row0 in range(0, tq, td):
                carry = block(i * tq + row0, td, carry, row0, True)
            m, l, acc = carry
            out = _keep(_lanes(tq, MLA_V * half, MLA_V), acc / l)
            lse = _keep(_lanes(tq, MLA_ROPE * half, MLA_ROPE), jnp.broadcast_to(m + jnp.log2(l), (tq, LANES)))
            if half == 0:
                o_ref[...] = out
                lse_ref[...] = lse
            else:
                o_ref[...] += out
                lse_ref[...] += lse
        if n_ride:
            pl.when(jnp.logical_and(g == n_pairs - 1, i == s // tq - 1))(finish)

    qblk = pl.BlockSpec((tq, LANES), lambda g, i: (i, g))
    full = pl.BlockSpec((s, LANES), lambda g, i: (0, g))
    outs = pl.pallas_call(
        kern, name="mla_fwd", grid=(n_pairs, s // tq),
        in_specs=[qblk, pl.BlockSpec((tq, LANES), lambda g, i: (i, g // 2)), full, pl.BlockSpec((s, LANES), lambda g, i: (0, 0)), full]
        + [HBM_SPEC] * n_ride,
        out_specs=[qblk, qblk] + [HBM_SPEC] * n_ride,
        out_shape=[_sds((s, MLA_WIDTH), F32), _sds((s, n_pairs * LANES), F32)] + [_sds((N_SHARD,) + a.shape, a.dtype) for a in riders],
        scratch_shapes=_gather_sems(n_ride) if n_ride else [], compiler_params=_params(2),
    )(qn, qr, kn, kr, v, *riders)
    return outs[0], outs[1], outs[2:]


def _mla_bwd(qn, qr, kn, kr, v, o, do, lse, riders=(), tq=MLA_TQ, tk=MLA_BWD_TK, td=MLA_DIAG_TK):
    s = qn.shape[0]
    tq, tk, td = min(tq, s), min(tk, s), min(td, s)
    ratio = tq // tk

    n_ride = len(riders)
    n_pairs = MLA_HEADS // 2

    def kern(qn_ref, qr_ref, kn_ref, kr_ref, v_ref, o_ref, do_ref, lse_ref, *rest):
        dqn_ref, dqr_ref, dkn_ref, dkr_ref, dv_ref = rest[n_ride:n_ride + 5]
        g = pl.program_id(0)
        i = pl.program_id(1)
        if n_ride:
            start, finish = _scatter_steps(rest[:n_ride], rest[n_ride + 5:2 * n_ride + 5], *rest[2 * n_ride + 5:])
            pl.when(jnp.logical_and(g == 0, i == 0))(start)

        @pl.when(i == 0)
        def _():
            dkn_ref[...] = jnp.zeros_like(dkn_ref)
            dkr_ref[...] = jnp.zeros_like(dkr_ref)
            dv_ref[...] = jnp.zeros_like(dv_ref)

        for half in range(2):
            rope_lo = MLA_ROPE * (2 * (g % 2) + half)
            qcat = _mla_qcat(qn_ref, qr_ref, rope_lo, half, tq)
            mine = _lanes(tq, MLA_V * half, MLA_V)
            do_f = _keep(mine, do_ref[...])
            do_b = do_f.astype(BF16)
            delta = jnp.sum(do_f * o_ref[...], axis=1, keepdims=True)
            lse_v = lse_ref[:, MLA_ROPE * half:MLA_ROPE * half + 1]

            def block(k0, width, dq_acc, row0, masked, qcat=qcat, do_b=do_b, delta=delta, lse_v=lse_v):
                ks = pl.ds(pl.multiple_of(k0, width), width)
                kcat = jnp.concatenate([kn_ref[ks, :], kr_ref[ks, :]], axis=1)
                qc, dob = qcat[row0:], do_b[row0:]
                p = jnp.exp2(_dot_nt(qc, kcat) - lse_v[row0:])
                if masked:
                    p = jnp.where(_diag_mask(tq - row0, width, row0, row0), p, 0.0)
                ds = (p * (_dot_nt(dob, v_ref[ks, :]) - delta[row0:])).astype(BF16)
                dv_ref[ks, :] += _dot_tn(p.astype(BF16), dob)
                dkc = _dot_tn(ds, qc)
                dkn_ref[ks, :] += dkc[:, :LANES]
                dkr_ref[ks, :] += dkc[:, LANES:]
                new = dq_acc[row0:] + _dot(ds, kcat)
                return new if row0 == 0 else jnp.concatenate([dq_acc[:row0], new], axis=0)

            acc = lax.fori_loop(0, i * ratio, lambda kb, c, block=block: block(kb * tk, tk, c, 0, False),
                                jnp.zeros((tq, 2 * LANES), F32))
            for row0 in range(0, tq, td):
                acc = block(i * tq + row0, td, acc, row0, True)
            dqn = _keep(_lanes(tq, MLA_NOPE * half, MLA_NOPE), acc[:, :LANES] * MLA_SCALE)
            dqr = _keep(_lanes(tq, rope_lo, MLA_ROPE), acc[:, LANES:] * MLA_SCALE)
            if half == 0:
                dqn_ref[...] = dqn
                dqr_ref[...] = dqr
            else:
                dqn_ref[...] += dqn
                dqr_ref[...] += dqr
        if n_ride:
            pl.when(jnp.logical_and(g == n_pairs - 1, i == s // tq - 1))(finish)

    qblk = pl.BlockSpec((tq, LANES), lambda g, i: (i, g))
    full = pl.BlockSpec((s, LANES), lambda g, i: (0, g))
    once = lambda spec_map: pl.BlockSpec((s, LANES), spec_map, pipeline_mode=pl.Buffered(1))
    wide = _sds((s, n_pairs * LANES), F32)
    outs = pl.pallas_call(
        kern, name="mla_bwd", grid=(n_pairs, s // tq),
        in_specs=[qblk, pl.BlockSpec((tq, LANES), lambda g, i: (i, g // 2)), once(lambda g, i: (0, g)), once(lambda g, i: (0, 0)),
                  once(lambda g, i: (0, g)), qblk, qblk, qblk] + [HBM_SPEC] * n_ride,
        out_specs=[qblk, qblk, full, full, full] + [HBM_SPEC] * n_ride,
        out_shape=[wide] * 5 + [_sds((N_SHARD - 1,) + p.shape[1:], p.dtype) for p in riders],
        scratch_shapes=_scatter_sems(n_ride) if n_ride else [], compiler_params=_params(2),
    )(qn, qr, kn, kr, v, o, do, lse, *riders)
    return outs[:5], outs[5:]


def _sb_masks(tk):
    j = lax.broadcasted_iota(jnp.int32, (tk, tk), 0)
    c = lax.broadcasted_iota(jnp.int32, (tk, tk), 1)
    return (j > c).astype(BF16), (j < c).astype(BF16)


def _sb_scores(qs, kk, msuf, strict):
    z = _dot_nt(qs, kk)
    lom = -(jnp.maximum(z, 0.0) + jnp.log(1.0 + jnp.exp(-jnp.abs(z))))
    if strict is not None:
        lom = jnp.where(strict, lom, 0.0)
    hi = lom.astype(BF16)
    lo = (lom - hi.astype(F32)).astype(BF16)
    return z, lom, _dot(hi, msuf) + _dot(lo, msuf)


def _sb_strict(tq, tk, d):
    row = lax.broadcasted_iota(jnp.int32, (tq, tk), 0)
    col = lax.broadcasted_iota(jnp.int32, (tq, tk), 1)
    return col + d * tk < row


def _sb_fwd(q, k, v, msuf, tq=SB_TQ, tk=SB_TK):
    s = q.shape[0]
    tq, tk = min(tq, s), min(tk, s)
    ratio = tq // tk

    def kern(q_ref, k_ref, v_ref, m_ref, o_ref, c_ref):
        i = pl.program_id(1)
        msf = m_ref[...]
        lane = lax.broadcasted_iota(jnp.int32, (tq, LANES), 1)
        for half in range(2):
            mine = _lanes(tq, SB_DIM * half, SB_DIM)
            qs = _keep(mine, q_ref[...]) * 0.125

            def block(kb, carry, dd, qs=qs):
                c, acc, cm = carry
                ks = pl.ds(pl.multiple_of(kb * tk, tk), tk)
                strict = None if dd is None else _sb_strict(tq, tk, dd)
                z, lom, suf = _sb_scores(qs, k_ref[ks, :], msf, strict)
                a = jnp.exp(z + lom + (suf + c))
                if strict is not None:
                    a = jnp.where(strict, a, 0.0)
                acc = acc + _dot(a.astype(BF16), v_ref[ks, :])
                cm = jnp.where(lane == kb, c, cm)
                return c + jnp.sum(lom, axis=1, keepdims=True), acc, cm

            carry = (jnp.zeros((tq, 1), F32), jnp.zeros((tq, LANES), F32), jnp.full((tq, LANES), NEG, F32))
            for dd in range(ratio - 1, -1, -1):
                carry = block(i * ratio + dd, carry, dd)

            def live(st):
                return jnp.logical_and(st[0] >= 0, jnp.max(st[1]) > -SB_SKIP)

            def step(st, block=block):
                return (st[0] - 1, *block(st[0], st[1:], None))

            _, _, acc, cm = lax.while_loop(live, step, (i * ratio - 1, *carry))
            if half == 0:
                o_ref[...] = _keep(mine, acc)
            else:
                o_ref[...] += _keep(mine, acc)
            c_ref[:, LANES * half:LANES * (half + 1)] = cm

    qblk = lambda n: pl.BlockSpec((tq, n), lambda g, i: (i, g))
    full = pl.BlockSpec((s, LANES), lambda g, i: (0, g))
    return pl.pallas_call(
        kern, name="sb_fwd", grid=(SB_HEADS // 2, s // tq),
        in_specs=[qblk(LANES), full, full, pl.BlockSpec((tk, tk), lambda g, i: (0, 0))],
        out_specs=[qblk(LANES), qblk(2 * LANES)],
        out_shape=[_sds((s, SB_WIDTH), F32), _sds((s, SB_HEADS * LANES), F32)], compiler_params=_params(2),
    )(q, k, v, msuf)


def _sb_bwd(q, k, v, do, cmat, msuf, mpre, tq=SB_TQ, tk=SB_TK):
    s = q.shape[0]
    tq, tk = min(tq, s), min(tk, s)
    ratio = tq // tk

    def kern(q_ref, k_ref, v_ref, do_ref, c_ref, ms_ref, mp_ref, dq_ref, dk_ref, dv_ref):
        i = pl.program_id(1)

        @pl.when(i == 0)
        def _():
            dk_ref[...] = jnp.zeros_like(dk_ref)
            dv_ref[...] = jnp.zeros_like(dv_ref)

        msf = ms_ref[...]
        mpf = mp_ref[...]
        lane = lax.broadcasted_iota(jnp.int32, (tq, LANES), 1)
        lane1 = lax.broadcasted_iota(jnp.int32, (1, LANES), 1)
        for half in range(2):
            mine = _lanes(tq, SB_DIM * half, SB_DIM)
            qv = _keep(mine, q_ref[...])
            qs = qv * 0.125
            do_b = _keep(mine, do_ref[...]).astype(BF16)
            cm = c_ref[:, LANES * half:LANES * (half + 1)]

            def block(kb, carry, dd, qv=qv, qs=qs, do_b=do_b, cm=cm):
                dq_acc, pc = carry
                ks = pl.ds(pl.multiple_of(kb * tk, tk), tk)
                kk = k_ref[ks, :]
                strict = None if dd is None else _sb_strict(tq, tk, dd)
                z, lom, suf = _sb_scores(qs, kk, msf, strict)
                c = jnp.sum(jnp.where(lane == kb, cm, 0.0), axis=1, keepdims=True)
                a = jnp.exp(z + lom + (suf + c))
                if strict is not None:
                    a = jnp.where(strict, a, 0.0)
                g = _dot_nt(do_b, v_ref[ks, :]) * a
                p = pc + _dot(g.astype(BF16), mpf)
                omb = jnp.exp(lom)
                dz = (g * omb - (1.0 - omb) * p) * 0.125
                if strict is not None:
                    dz = jnp.where(strict, dz, 0.0)
                dz = dz.astype(BF16)
                dv_ref[ks, :] += _dot_tn(a.astype(BF16), do_b)
                dk_ref[ks, :] += _dot_tn(dz, qv)
                return dq_acc + _dot(dz, kk), pc + jnp.sum(g, axis=1, keepdims=True)

            seen = jnp.logical_and(jnp.max(cm, axis=0, keepdims=True) > -SB_SKIP, lane1 < i * ratio)
            first = i * ratio - jnp.sum(seen.astype(jnp.int32))
            carry = (jnp.zeros((tq, LANES), F32), jnp.zeros((tq, 1), F32))
            carry = lax.fori_loop(first, i * ratio, lambda kb, c, block=block: block(kb, c, None), carry)
            for dd in range(ratio):
                carry = block(i * ratio + dd, carry, dd)
            if half == 0:
                dq_ref[...] = _keep(mine, carry[0])
            else:
                dq_ref[...] += _keep(mine, carry[0])

    qblk = lambda n: pl.BlockSpec((tq, n), lambda g, i: (i, g))
    full = pl.BlockSpec((s, LANES), lambda g, i: (0, g))
    msk = pl.BlockSpec((tk, tk), lambda g, i: (0, 0))
    return pl.pallas_call(
        kern, name="sb_bwd", grid=(SB_HEADS // 2, s // tq),
        in_specs=[qblk(LANES), full, full, qblk(LANES), qblk(2 * LANES), msk, msk],
        out_specs=[qblk(LANES), full, full],
        out_shape=[_sds((s, SB_WIDTH), F32)] * 3, compiler_params=_params(2),
    )(q, k, v, do, cmat, msuf, mpre)


def _place():
    return lax.axis_index("x"), lax.axis_index("y"), lax.axis_index("c")


def _other_chips(x, y):
    return [(1 - x, y), (x, 1 - y), (1 - x, 1 - y)]


HBM_SPEC = pl.BlockSpec(memory_space=pl.ANY)


def _gather_steps(ins, outs, send_sems, recv_sems):
    n = len(ins)
    x, y, c = _place()
    sibling = (x, y, 1 - c)
    chips = _other_chips(x, y)

    def half_of(a, ref, pc):
        half = ins[a].shape[0] // 2
        return ref.at[pl.ds(pl.multiple_of(pc * half, 16), half), :]

    def copy(a, k, chip, pc, to, src=None):
        dst = half_of(a, outs[a].at[2 * chip[0] + chip[1]], pc)
        return pltpu.make_async_remote_copy(src_ref=dst if src is None else src, dst_ref=dst, send_sem=send_sems.at[6 * a + k],
                                            recv_sem=recv_sems.at[6 * a + k], device_id=to, device_id_type=MESH)

    def first():
        return [copy(a, j, (x, y), c, (*chip, c), src=half_of(a, ins[a], c)) for a in range(n) for j, chip in enumerate(chips)]

    def passed():
        return [copy(a, 3 + j, chip, c, sibling) for j, chip in enumerate(chips) for a in range(n)]

    def send():
        for cp in first():
            cp.start()

    def forward():
        for j, chip in enumerate(chips):
            for a in range(n):
                copy(a, j, chip, c, sibling).wait_recv()
        for cp in passed():
            cp.start()

    def finish():
        for j, chip in enumerate(chips):
            for a in range(n):
                copy(a, 3 + j, chip, 1 - c, sibling).wait_recv()
        for cp in first() + passed():
            cp.wait_send()

    return send, forward, finish


def _gather_sems(n):
    return [pltpu.SemaphoreType.DMA((6 * n,)), pltpu.SemaphoreType.DMA((6 * n,))]


def _allgather_list(name, shards):
    n = len(shards)

    def body(*refs):
        for stage in _gather_steps(refs[:n], refs[n:2 * n], *refs[2 * n:]):
            stage()

    return pl.pallas_call(
        body, name=name, out_shape=[_sds((N_SHARD,) + a.shape, a.dtype) for a in shards], in_specs=[HBM_SPEC] * n,
        out_specs=[HBM_SPEC] * n, scratch_shapes=_gather_sems(n),
    )(*shards)


def _swap_halves(name, gs):
    n = len(gs)

    def body(*refs):
        ins, outs = refs[:n], refs[n:2 * n]
        send_sems, recv_sems = refs[2 * n:]
        x, y, c = _place()
        copies = []
        for a in range(n):
            h = gs[a].shape[1] // 2
            src = ins[a].at[:, pl.ds(pl.multiple_of((1 - c) * h, 8), h), :]
            copies.append(pltpu.make_async_remote_copy(src_ref=src, dst_ref=outs[a], send_sem=send_sems.at[a], recv_sem=recv_sems.at[a],
                                                       device_id=(x, y, 1 - c), device_id_type=MESH))
        for cp in copies:
            cp.start()
        for cp in copies:
            cp.wait()

    return pl.pallas_call(
        body, name=name, out_shape=[_sds((N_SHARD, g.shape[1] // 2, g.shape[2]), g.dtype) for g in gs],
        in_specs=[HBM_SPEC] * n, out_specs=[HBM_SPEC] * n,
        scratch_shapes=[pltpu.SemaphoreType.DMA((n,)), pltpu.SemaphoreType.DMA((n,))],
    )(*gs)


def _add_sibling(name, gs, gots, c_idx):
    n = len(gs)

    def kern(c_ref, *refs):
        for a in range(n):
            tot = refs[a][...] + refs[n + a][...]
            refs[2 * n + a][...] = tot
            refs[3 * n + a][...] = tot.astype(BF16)

    quarter = lambda g: (None, g.shape[1] // 4, g.shape[2])
    in_specs = [pl.BlockSpec(quarter(g), lambda b, s, c_ref: (b, 2 * c_ref[0] + s, 0)) for g in gs]
    in_specs += [pl.BlockSpec(quarter(g), lambda b, s, c_ref: (b, s, 0)) for g in gs]
    out_specs = [pl.BlockSpec(quarter(g), lambda b, s, c_ref: (b, s, 0)) for g in gs] * 2
    out_shape = [_sds(t.shape, F32) for t in gots] + [_sds(t.shape, BF16) for t in gots]
    outs = pl.pallas_call(
        kern, name=name, out_shape=out_shape,
        grid_spec=pltpu.PrefetchScalarGridSpec(num_scalar_prefetch=1, grid=(N_SHARD, 2), in_specs=in_specs, out_specs=out_specs),
        compiler_params=_params(2),
    )(c_idx.reshape(1), *gs, *gots)
    return outs[:n], outs[n:]


def _scatter_steps(ins, outs, send_sems, recv_sems):
    x, y, c = _place()

    def copies():
        return [pltpu.make_async_remote_copy(
            src_ref=ins[a].at[2 * px + py], dst_ref=outs[a].at[j], send_sem=send_sems.at[3 * a + j], recv_sem=recv_sems.at[3 * a + j],
            device_id=(px, py, c), device_id_type=MESH) for a in range(len(ins)) for j, (px, py) in enumerate(_other_chips(x, y))]

    def start():
        for cp in copies():
            cp.start()

    def finish():
        for cp in copies():
            cp.wait()

    return start, finish


def _scatter_sems(n):
    return [pltpu.SemaphoreType.DMA((3 * n,)), pltpu.SemaphoreType.DMA((3 * n,))]


def _chip_scatter(ps):
    n = len(ps)

    def body(*refs):
        for stage in _scatter_steps(refs[:n], refs[n:2 * n], *refs[2 * n:]):
            stage()

    return pl.pallas_call(
        body, name="chip_scatter", out_shape=[_sds((N_SHARD - 1,) + p.shape[1:], p.dtype) for p in ps], in_specs=[HBM_SPEC] * n,
        out_specs=[HBM_SPEC] * n, scratch_shapes=_scatter_sems(n),
    )(*ps)


def _add_chips(name, ps, others, shard_idx):
    n = len(ps)

    def kern(b_ref, *refs):
        for a in range(n):
            tot = refs[a][...]
            for j in range(N_SHARD - 1):
                tot = tot + refs[n + a][j].astype(F32)
            refs[2 * n + a][...] = tot

    in_specs = [pl.BlockSpec((None, p.shape[1] // 2, p.shape[2]), lambda s, b_ref: (b_ref[0], s, 0)) for p in ps]
    in_specs += [pl.BlockSpec((N_SHARD - 1, p.shape[1] // 2, p.shape[2]), lambda s, b_ref: (0, s, 0)) for p in ps]
    out_specs = [pl.BlockSpec((p.shape[1] // 2, p.shape[2]), lambda s, b_ref: (s, 0)) for p in ps]
    return pl.pallas_call(
        kern, name=name, out_shape=[_sds(p.shape[1:], F32) for p in ps],
        grid_spec=pltpu.PrefetchScalarGridSpec(num_scalar_prefetch=1, grid=(2,), in_specs=in_specs, out_specs=out_specs),
        compiler_params=_params(1),
    )(shard_idx.reshape(1), *ps, *others)


def _swap_result(name, mines):
    n = len(mines)

    def body(*refs):
        ins, outs = refs[:n], refs[n:2 * n]
        send_sems, recv_sems = refs[2 * n:]
        x, y, c = _place()
        copies = [pltpu.make_async_remote_copy(src_ref=ins[a], dst_ref=outs[a], send_sem=send_sems.at[a], recv_sem=recv_sems.at[a],
                                               device_id=(x, y, 1 - c), device_id_type=MESH) for a in range(n)]
        for cp in copies:
            cp.start()
        for cp in copies:
            cp.wait()

    return pl.pallas_call(
        body, name=name, out_shape=[_sds(m.shape, m.dtype) for m in mines], in_specs=[HBM_SPEC] * n,
        out_specs=[HBM_SPEC] * n, scratch_shapes=[pltpu.SemaphoreType.DMA((n,)), pltpu.SemaphoreType.DMA((n,))],
    )(*mines)


def _allreduce_small(v):
    m_per, n = v.shape

    def body(x_ref, tot_ref, all_ref, send_sems, recv_sems, local_sem):
        x, y, c = _place()
        me, sibling = (x, y, c), (x, y, 1 - c)
        chips = _other_chips(x, y)

        def rows(px, py, pc):
            return all_ref.at[pl.ds(pl.multiple_of((4 * px + 2 * py + pc) * m_per, 8), m_per), :]

        def copy(k, block, to, src=None):
            return pltpu.make_async_remote_copy(
                src_ref=rows(*block) if src is None else src, dst_ref=rows(*block), send_sem=send_sems.at[k],
                recv_sem=recv_sems.at[k], device_id=to, device_id_type=MESH)

        mine = pltpu.make_async_copy(x_ref, rows(*me), local_sem)
        mine.start()
        first = [copy(0, me, sibling, src=x_ref)] + [copy(1 + j, me, (*chip, c), src=x_ref) for j, chip in enumerate(chips)]
        for cp in first:
            cp.start()
        passed = [copy(4 + j, (*chip, c), sibling) for j, chip in enumerate(chips)]
        for j, chip in enumerate(chips):
            copy(1 + j, (*chip, c), me).wait_recv()
            passed[j].start()
        copy(0, sibling, me).wait_recv()
        for j, chip in enumerate(chips):
            copy(4 + j, (*chip, 1 - c), me).wait_recv()
        for cp in first + passed:
            cp.wait_send()
        mine.wait()
        tot = all_ref[0:m_per, :]
        for dev in range(1, 8):
            tot = tot + all_ref[dev * m_per:(dev + 1) * m_per, :]
        tot_ref[...] = tot

    vmem = pl.BlockSpec(memory_space=pltpu.VMEM)
    return pl.pallas_call(
        body, name="allreduce_small", out_shape=_sds((m_per, n), F32), in_specs=[vmem], out_specs=vmem,
        scratch_shapes=[pltpu.VMEM((8 * m_per, n), F32), pltpu.SemaphoreType.DMA((7,)), pltpu.SemaphoreType.DMA((7,)),
                        pltpu.SemaphoreType.DMA],
    )(v)


def _adam_update(w, g, m, v):
    m_new = ADAM_B1 * m + (1.0 - ADAM_B1) * g
    v_new = ADAM_B2 * v + (1.0 - ADAM_B2) * (g * g)
    m_hat = m_new / (1.0 - ADAM_B1 ** ADAM_STEP)
    v_hat = v_new / (1.0 - ADAM_B2 ** ADAM_STEP)
    return -ADAM_LR * (m_hat / (jnp.sqrt(v_hat) + ADAM_EPS) + ADAM_WD * w), m_new, v_new


def _adamw(name, w, g, m, v):
    rows, width = w.shape
    tr = rows // 4 if rows % 32 == 0 else rows

    def kern(w_ref, g_ref, m_ref, v_ref, d_ref, mo_ref, vo_ref):
        d_ref[...], mo_ref[...], vo_ref[...] = _adam_update(w_ref[...], g_ref[...], m_ref[...], v_ref[...])

    spec = pl.BlockSpec((tr, width), lambda i: (i, 0))
    return pl.pallas_call(kern, name=name, grid=(rows // tr,), in_specs=[spec] * 4, out_specs=[spec] * 3,
                          out_shape=[_sds((rows, width), F32)] * 3, compiler_params=_params(1))(w, g, m, v)


def _adamw_halves(name, w, mine, theirs, m, v, c_idx):
    rows, width = w.shape
    tr = rows // 4

    def kern(c_ref, w_ref, mine_ref, theirs_ref, m_ref, v_ref, g_ref, d_ref, mo_ref, vo_ref):
        g = jnp.where(pl.program_id(0) == c_ref[0], mine_ref[...], theirs_ref[...])
        g_ref[...] = g
        d_ref[...], mo_ref[...], vo_ref[...] = _adam_update(w_ref[...], g, m_ref[...], v_ref[...])

    whole = pl.BlockSpec((tr, width), lambda h, j, c_ref: (2 * h + j, 0))
    part = pl.BlockSpec((tr, width), lambda h, j, c_ref: (j, 0))
    return pl.pallas_call(
        kern, name=name, out_shape=[_sds((rows, width), F32)] * 4,
        grid_spec=pltpu.PrefetchScalarGridSpec(num_scalar_prefetch=1, grid=(2, 2), in_specs=[whole, part, part, whole, whole],
                                               out_specs=[whole] * 4),
        compiler_params=_params(2),
    )(c_idx.reshape(1), w, mine, theirs, m, v)


SHARDED = (("w_in", D_MODEL, IN_WIDTH, 1), ("w_uq", Q_RANK, MLA_HEADS * MLA_QK, 1),
           ("w_ukv", KV_RANK, MLA_HEADS * (MLA_NOPE + MLA_V), 1), ("w_o", D_MODEL, D_MODEL, 0),
           ("w_gate", D_MODEL, D_FF, 1), ("w_up", D_MODEL, D_FF, 1), ("w_down", D_FF, D_MODEL, 0))
EARLY = ("w_in", "w_uq", "w_ukv")
LATE = ("w_o", "w_gate", "w_up", "w_down")
SMALL = (("norm_mix", D_MODEL), ("q_latent_norm", Q_RANK), ("kv_latent_norm", KV_RANK), ("out_norm_mla", MLA_WIDTH),
         ("out_norm_sb", SB_WIDTH), ("norm_ffn", D_MODEL), ("norm_final", D_MODEL))


def _full_weight(gathered, axis):
    n_sh, k, n = gathered.shape
    return gathered.transpose(1, 0, 2).reshape(k, n_sh * n) if axis == 1 else gathered.reshape(n_sh * k, n)


def _shard_major(g, axis):
    r, c = g.shape
    return g.reshape(r, N_SHARD, c // N_SHARD).transpose(1, 0, 2) if axis == 1 else g.reshape(N_SHARD, r // N_SHARD, c)


def _rot_cols(w):
    hh = MLA_ROPE // 2
    return jnp.concatenate([-w[..., hh:], w[..., :hh]], axis=-1)


def _rot_cols_t(g):
    hh = MLA_ROPE // 2
    return jnp.concatenate([g[..., hh:], -g[..., :hh]], axis=-1)


def _with_transposes(w):
    w.update({name + "_t": t.T for name, t in list(w.items())})
    return w


def _attention_weights(full, small):
    w_in = full["w_in"]
    s0, s1, s2 = Q_RANK, Q_RANK + KV_RANK, Q_RANK + KV_RANK + MLA_ROPE
    uq = full["w_uq"].reshape(Q_RANK, MLA_HEADS, MLA_QK)
    ukv = full["w_ukv"].reshape(KV_RANK, MLA_HEADS, MLA_NOPE + MLA_V)
    w_kr = w_in[:, s1:s2]
    per_tile = ROPE_TILE // MLA_ROPE
    w = _with_transposes({
        "w_cq": w_in[:, :s0], "w_ckv": w_in[:, s0:s1],
        "w_kr4": jnp.tile(w_kr, (1, per_tile)), "w_kr4r": jnp.tile(_rot_cols(w_kr), (1, per_tile)),
        "w_kr8": jnp.tile(w_kr, (1, MLA_HEADS)), "w_kr8r": jnp.tile(_rot_cols(w_kr), (1, MLA_HEADS)),
        "w_sbq": w_in[:, s2:s2 + SB_WIDTH], "w_sbk": w_in[:, s2 + SB_WIDTH:s2 + 2 * SB_WIDTH], "w_sbv": w_in[:, s2 + 2 * SB_WIDTH:],
        "w_qn": uq[..., :MLA_NOPE].reshape(Q_RANK, -1), "w_qr": uq[..., MLA_NOPE:].reshape(Q_RANK, -1),
        "w_qrr": _rot_cols(uq[..., MLA_NOPE:]).reshape(Q_RANK, -1),
        "w_kn": ukv[..., :MLA_NOPE].reshape(KV_RANK, -1), "w_v": ukv[..., MLA_NOPE:].reshape(KV_RANK, -1),
    })
    w.update(g_mix=small["norm_mix"], g_q=small["q_latent_norm"], g_kv=small["kv_latent_norm"], g_a=small["out_norm_mla"],
             g_b=small["out_norm_sb"], g_f=small["norm_ffn"], g_n=small["norm_final"])
    return w


def _ffn_weights(full):
    return _with_transposes({"w_oa": full["w_o"][:MLA_WIDTH], "w_ob": full["w_o"][MLA_WIDTH:], "w_gate": full["w_gate"],
                             "w_up": full["w_up"], "w_down": full["w_down"]})


def _rope_tables(positions):
    inv_freq = ROPE_THETA ** (-jnp.arange(0, MLA_ROPE, 2, dtype=F32) / MLA_ROPE)
    ang = positions.astype(F32)[:, None] * inv_freq[None, :]
    cos, sin = jnp.cos(ang), jnp.sin(ang)
    return {"cos": jnp.tile(jnp.concatenate([cos, cos], axis=1), (1, MLA_HEADS)),
            "sin": jnp.tile(jnp.concatenate([sin, sin], axis=1), (1, MLA_HEADS))}


def _by_head(g_wide, g_narrow, wide, narrow):
    r = g_wide.shape[0]
    return jnp.concatenate([g_wide.reshape(r, MLA_HEADS, wide), g_narrow.reshape(r, MLA_HEADS, narrow)], axis=-1).reshape(r, -1)


def kernel(x, positions, norm_mix, w_in, q_latent_norm, w_uq, kv_latent_norm, w_ukv, out_norm_mla, out_norm_sb, w_o, norm_ffn, w_gate, w_up, w_down, norm_final, loss_target, m_norm_mix, m_w_in, m_q_latent_norm, m_w_uq, m_kv_latent_norm, m_w_ukv, m_out_norm_mla, m_out_norm_sb, m_w_o, m_norm_ffn, m_w_gate, m_w_up, m_w_down, m_norm_final, v_norm_mix, v_w_in, v_q_latent_norm, v_w_uq, v_kv_latent_norm, v_w_ukv, v_out_norm_mla, v_out_norm_sb, v_w_o, v_norm_ffn, v_w_gate, v_w_up, v_w_down, v_norm_final):
    given = dict(norm_mix=norm_mix, w_in=w_in, q_latent_norm=q_latent_norm, w_uq=w_uq, kv_latent_norm=kv_latent_norm, w_ukv=w_ukv,
                 out_norm_mla=out_norm_mla, out_norm_sb=out_norm_sb, w_o=w_o, norm_ffn=norm_ffn, w_gate=w_gate, w_up=w_up,
                 w_down=w_down, norm_final=norm_final)
    mom_m = dict(norm_mix=m_norm_mix, w_in=m_w_in, q_latent_norm=m_q_latent_norm, w_uq=m_w_uq, kv_latent_norm=m_kv_latent_norm,
                 w_ukv=m_w_ukv, out_norm_mla=m_out_norm_mla, out_norm_sb=m_out_norm_sb, w_o=m_w_o, norm_ffn=m_norm_ffn,
                 w_gate=m_w_gate, w_up=m_w_up, w_down=m_w_down, norm_final=m_norm_final)
    mom_v = dict(norm_mix=v_norm_mix, w_in=v_w_in, q_latent_norm=v_q_latent_norm, w_uq=v_w_uq, kv_latent_norm=v_kv_latent_norm,
                 w_ukv=v_w_ukv, out_norm_mla=v_out_norm_mla, out_norm_sb=v_out_norm_sb, w_o=v_w_o, norm_ffn=v_norm_ffn,
                 w_gate=v_w_gate, w_up=v_w_up, w_down=v_w_down, norm_final=v_norm_final)
    xs = x[0]
    tgt = loss_target[0]
    s = xs.shape[0]
    c_idx = lax.axis_index("c")
    shard_idx = 2 * lax.axis_index("x") + lax.axis_index("y")

    shard2d = {name: given[name].reshape(given[name].shape[-2:]) for name, *_ in SHARDED}
    is_mine = (jnp.arange(N_SHARD) == shard_idx)[:, None, None]
    local = {name: shard2d[name].astype(BF16) for name, *_ in SHARDED}
    axis_of = {name: axis for name, _, _, axis in SHARDED}

    def whole(names, gathered):
        return {name: _full_weight(jnp.where(is_mine, local[name][None], t), axis_of[name]) for name, t in zip(names, gathered)}

    small = {name: given[name].reshape(1, n) for name, n in SMALL}
    w = _attention_weights(whole(EARLY, _allgather_list("allgather_w", [local[name] for name in EARLY])), small)
    tabs = _rope_tables(positions[0])
    msuf, mpre = _sb_masks(min(SB_TK, s))

    u, cq, ckv, cqn, ckvn, qn, qr, kn, vm, kr, sq, sk, sv = _fwd_a(xs, tabs, w)
    o_mla, lse, late = _mla_fwd(qn, qr, kn, kr, vm, [local[name] for name in LATE])
    w.update(_ffn_weights(whole(LATE, late)))
    o_sb, cmat = _sb_fwd(sq, sk, sv, msuf)
    merged, h1, f, gate, up, act = _fwd_b1(xs, o_mla, o_sb, w)
    dh2, loss_part, dg_n = _fwd_b2(h1, act, tgt, w)

    def chip_sums(tag, names, grads):
        gs = [_shard_major(grads[name], axis_of[name]) for name in names]
        return _add_sibling("add_sibling_" + tag, gs, _swap_halves("swap_halves_" + tag, gs), c_idx)

    def reduced(tag, chip_f32, others):
        mine = _add_chips("add_chips_" + tag, chip_f32, others, shard_idx)
        return tuple(mine), tuple(_swap_result("swap_result_" + tag, mine))

    dgate, dup = _bwd_b1(dh2, gate, up, w)
    dh1, do_mla, do_sb, dg_f, dg_a, dg_b = _bwd_b2(dgate, dup, h1, dh2, o_mla, o_sb, w)
    late_f32, late_bf16 = chip_sums("late", LATE, {
        "w_o": _tn_matmul("dw_o", merged, dh1), "w_gate": _tn_matmul("dw_gate", f, dgate),
        "w_up": _tn_matmul("dw_up", f, dup), "w_down": _tn_matmul("dw_down", act, dh2)})
    (dqn, dqr, dkn, dkr, dvm), late_others = _mla_bwd(qn, qr, kn, kr, vm, o_mla, do_mla, lse, late_bf16)
    mine_late, theirs_late = reduced("late", late_f32, late_others)
    dsq, dsk, dsv = _sb_bwd(sq, sk, sv, do_sb, cmat, msuf, mpre)
    dx, a1, a2, dcq, dckv, dkrc, dkrs, dg_q, dg_kv, dg_mix = _bwd_a(xs, dh1, cq, ckv, dqn, dqr, dkn, dvm, dkr, dsq, dsk, dsv, tabs, w)

    g_cq, g_ckv, g_krc, g_krs, g_sq, g_sk, g_sv = _tn_multi("dw_in", u, [dcq, dckv, dkrc, dkrs, dsq, dsk, dsv])
    g_qn, g_qr1, g_qr2 = _tn_multi("dw_uq", cqn, [dqn, a1, a2])
    g_kn, g_v = _tn_multi("dw_ukv", ckvn, [dkn, dvm])
    slots = lambda g: g.reshape(g.shape[0], MLA_HEADS, MLA_ROPE)
    g_kr = jnp.sum(slots(g_krc), axis=1) + _rot_cols_t(jnp.sum(slots(g_krs), axis=1))
    g_qr = (slots(g_qr1) + _rot_cols_t(slots(g_qr2))).reshape(Q_RANK, -1)
    early_f32, early_bf16 = chip_sums("early", EARLY, {
        "w_in": jnp.concatenate([g_cq, g_ckv, g_kr, g_sq, g_sk, g_sv], axis=1),
        "w_uq": _by_head(g_qn, g_qr, MLA_NOPE, MLA_ROPE),
        "w_ukv": _by_head(g_kn * MLA_DK_SCALE, g_v, MLA_NOPE, MLA_V)})
    mine_early, theirs_early = reduced("early", early_f32, _chip_scatter(early_bf16))
    halves = dict(zip(EARLY + LATE, zip(mine_early + mine_late, theirs_early + theirs_late)))

    small_parts = jnp.concatenate([dg_mix, dg_q, dg_kv, dg_a, dg_b, dg_f, dg_n], axis=1)
    small_g = _allreduce_small(jnp.broadcast_to(small_parts, (8, small_parts.shape[1])))[0:1]
    loss = lax.psum(loss_part[0, 0], ("x", "y", "c"))

    g_out, d_out, m_out, v_out = {}, {}, {}, {}
    for name, *_ in SHARDED:
        shape = given[name].shape
        outs = _adamw_halves("adamw_" + name, shard2d[name], *halves[name], mom_m[name].reshape(shard2d[name].shape),
                             mom_v[name].reshape(shard2d[name].shape), c_idx)
        g_out[name], d_out[name], m_out[name], v_out[name] = (t.reshape(shape) for t in outs)
    cat = lambda src: jnp.concatenate([src[name].reshape(1, n) for name, n in SMALL], axis=1)
    d, mn, vn = _adamw("adamw_small", cat(given), small_g, cat(mom_m), cat(mom_v))
    off = 0
    for name, n in SMALL:
        shape = given[name].shape
        g_out[name], d_out[name], m_out[name], v_out[name] = (t[:, off:off + n].reshape(shape) for t in (small_g, d, mn, vn))
        off += n

    order = ["norm_mix", "w_in", "q_latent_norm", "w_uq", "kv_latent_norm", "w_ukv", "out_norm_mla", "out_norm_sb", "w_o",
             "norm_ffn", "w_gate", "w_up", "w_down", "norm_final"]
    return (loss, dx[None], *[g_out[n] for n in order], *[d_out[n] for n in order], *[m_out[n] for n in order],
            *[v_out[n] for n in order])
```

```python
import functools
import math

import jax
import jax.numpy as jnp
from jax import lax
from jax.experimental import pallas as pl
from jax.experimental.pallas import tpu as pltpu

F32 = jnp.float32
BF16 = jnp.bfloat16
MESH = pl.DeviceIdType.MESH

D_MODEL = 1024
EPS = 1e-6
MLA_HEADS = 8
MLA_NOPE = 64
MLA_ROPE = 32
MLA_V = 64
MLA_QK = MLA_NOPE + MLA_ROPE
Q_RANK = 256
KV_RANK = 128
ROPE_THETA = 10000.0
SB_HEADS = 8
SB_DIM = 64
MLA_WIDTH = MLA_HEADS * MLA_V
SB_WIDTH = SB_HEADS * SB_DIM
D_FF = 2816
IN_WIDTH = Q_RANK + KV_RANK + MLA_ROPE + 3 * SB_WIDTH

ADAM_LR = 0.001
ADAM_B1 = 0.9
ADAM_B2 = 0.999
ADAM_EPS = 1e-08
ADAM_WD = 0.01
ADAM_STEP = 10

N_SHARD = 4
LANES = 128
ROPE_TILE = LANES
VMEM_LIMIT = 56 * 1024 * 1024
TN_ACC_BYTES = 6 * 1024 * 1024 + 512 * 1024
NEG = -1e30
MLA_SCALE = 1.0 / math.sqrt(MLA_QK)
MLA_DK_SCALE = math.log(2.0)
MLA_QSCALE = MLA_SCALE * math.log2(math.e)
SB_SKIP = 110.0

ROW_TILE = 512
ROW_TILE_ELEMENTWISE = 256
MLA_TQ = 1024
SB_TQ = 512
MLA_TK = 1024
MLA_BWD_TK = 512
MLA_DIAG_TK = 512
SB_TK = 256
TN_TS = 2048


def _dot(a, b):
    return jnp.dot(a, b, preferred_element_type=F32)


def _dot_nt(a, b):
    return lax.dot_general(a, b, (((1,), (1,)), ((), ())), preferred_element_type=F32)


def _dot_tn(a, b):
    return lax.dot_general(a, b, (((0,), (0,)), ((), ())), preferred_element_type=F32)


def _params(n_grid, vmem=VMEM_LIMIT):
    return pltpu.CompilerParams(dimension_semantics=("arbitrary",) * n_grid, vmem_limit_bytes=vmem)


def _rms(x):
    r = lax.rsqrt(jnp.mean(x * x, axis=-1, keepdims=True) + EPS)
    return x * r, r


def _rms_bwd(n, r, g, dy):
    dn = dy * g
    dx = r * (dn - n * jnp.mean(dn * n, axis=-1, keepdims=True))
    return dx, jnp.sum(dy * n, axis=0, keepdims=True)


def _accumulate(ref, val, step):
    @pl.when(step == 0)
    def _():
        ref[...] = val

    @pl.when(step != 0)
    def _():
        ref[...] += val


def _rowwise(name, body, rows, consts, row_out, acc_out, tm):
    n_rows = rows[0].shape[0]
    tm = min(tm, n_rows)
    nr, nc, no = len(rows), len(consts), len(row_out)

    def kern(*refs):
        body(refs[:nr], refs[nr:nr + nc], refs[nr + nc:nr + nc + no], refs[nr + nc + no:], pl.program_id(0))

    in_specs = [pl.BlockSpec((tm, a.shape[1]), lambda i: (i, 0)) for a in rows]
    in_specs += [pl.BlockSpec(a.shape, lambda i: (0, 0), pipeline_mode=pl.Buffered(1)) for a in consts]
    out_specs = [pl.BlockSpec((tm, s.shape[1]), lambda i: (i, 0)) for s in row_out]
    out_specs += [pl.BlockSpec(s.shape, lambda i: (0, 0)) for s in acc_out]
    return pl.pallas_call(
        kern, name=name, grid=(n_rows // tm,), in_specs=in_specs, out_specs=out_specs,
        out_shape=list(row_out) + list(acc_out), compiler_params=_params(1),
    )(*rows, *consts)


def _sds(shape, dtype):
    return jax.ShapeDtypeStruct(shape, dtype)


def _fwd_a(x, tabs, w):
    s = x.shape[0]

    def body(r, c, o, a, step):
        x_ref, cos_ref, sin_ref = r
        gmix, wcq, wckv, wkr, wkrr, wsq, wsk, wsv, gq, wqn, wqr, wqrr, gkv, wkn, wv = c
        u_o, cq_o, ckv_o, cqn_o, ckvn_o, qn_o, qr_o, kn_o, v_o, kr_o, sq_o, sk_o, sv_o = o
        cos, sin = cos_ref[...], sin_ref[...]
        n, _ = _rms(x_ref[...])
        u = (n * gmix[...]).astype(BF16)
        u_o[...] = u
        cq = _dot(u, wcq[...])
        ckv = _dot(u, wckv[...])
        kr_o[...] = (_dot(u, wkr[...]) * cos[:, :ROPE_TILE] + _dot(u, wkrr[...]) * sin[:, :ROPE_TILE]).astype(BF16)
        sq_o[...] = _dot(u, wsq[...]).astype(BF16)
        sk_o[...] = _dot(u, wsk[...]).astype(BF16)
        sv_o[...] = _dot(u, wsv[...]).astype(BF16)
        cq_o[...] = cq
        ckv_o[...] = ckv
        nq, _ = _rms(cq)
        cqn = (nq * gq[...]).astype(BF16)
        cqn_o[...] = cqn
        qn_o[...] = (_dot(cqn, wqn[...]) * MLA_QSCALE).astype(BF16)
        qr_o[...] = ((_dot(cqn, wqr[...]) * cos + _dot(cqn, wqrr[...]) * sin) * MLA_QSCALE).astype(BF16)
        nkv, _ = _rms(ckv)
        ckvn = (nkv * gkv[...]).astype(BF16)
        ckvn_o[...] = ckvn
        kn_o[...] = _dot(ckvn, wkn[...]).astype(BF16)
        v_o[...] = _dot(ckvn, wv[...]).astype(BF16)

    outs = [
        _sds((s, D_MODEL), BF16), _sds((s, Q_RANK), F32), _sds((s, KV_RANK), F32), _sds((s, Q_RANK), BF16),
        _sds((s, KV_RANK), BF16), _sds((s, MLA_HEADS * MLA_NOPE), BF16), _sds((s, MLA_HEADS * MLA_ROPE), BF16),
        _sds((s, MLA_HEADS * MLA_NOPE), BF16), _sds((s, MLA_WIDTH), BF16), _sds((s, ROPE_TILE), BF16),
        _sds((s, SB_WIDTH), BF16), _sds((s, SB_WIDTH), BF16), _sds((s, SB_WIDTH), BF16),
    ]
    consts = [w["g_mix"], w["w_cq"], w["w_ckv"], w["w_kr4"], w["w_kr4r"], w["w_sbq"], w["w_sbk"], w["w_sbv"], w["g_q"],
              w["w_qn"], w["w_qr"], w["w_qrr"], w["g_kv"], w["w_kn"], w["w_v"]]
    return _rowwise("fwd_a", body, [x, tabs["cos"], tabs["sin"]], consts, outs, [], ROW_TILE)


def _fwd_b1(x, o_mla, o_sb, w):
    s = x.shape[0]

    def body(r, c, o, a, step):
        x_ref, oa_ref, ob_ref = r
        ga, gb, woa, wob, gf, wg, wu = c
        mg_o, h1_o, f_o, gate_o, up_o, act_o = o
        na, _ = _rms(oa_ref[...])
        nb, _ = _rms(ob_ref[...])
        ma = (na * ga[...]).astype(BF16)
        mb = (nb * gb[...]).astype(BF16)
        mg_o[:, :MLA_WIDTH] = ma
        mg_o[:, MLA_WIDTH:] = mb
        h1 = x_ref[...] + _dot(ma, woa[...]) + _dot(mb, wob[...])
        h1_o[...] = h1
        nf, _ = _rms(h1)
        f = (nf * gf[...]).astype(BF16)
        f_o[...] = f
        gate = _dot(f, wg[...])
        up = _dot(f, wu[...])
        gate_o[...] = gate.astype(BF16)
        up_o[...] = up.astype(BF16)
        act_o[...] = (gate * (1.0 / (1.0 + jnp.exp(-gate))) * up).astype(BF16)

    outs = [_sds((s, D_MODEL), BF16), _sds((s, D_MODEL), F32), _sds((s, D_MODEL), BF16), _sds((s, D_FF), BF16),
            _sds((s, D_FF), BF16), _sds((s, D_FF), BF16)]
    consts = [w["g_a"], w["g_b"], w["w_oa"], w["w_ob"], w["g_f"], w["w_gate"], w["w_up"]]
    return _rowwise("fwd_b1", body, [x, o_mla, o_sb], consts, outs, [], ROW_TILE)


def _fwd_b2(h1, act, tgt, w):
    s = h1.shape[0]

    def body(r, c, o, a, step):
        h1_ref, act_ref, t_ref = r
        wd, gn = c
        (dh2_o,) = o
        loss_o, dgn_o = a
        h2 = h1_ref[...] + _dot(act_ref[...], wd[...])
        n2, r2 = _rms(h2)
        err = n2 * gn[...] - t_ref[...]
        part = jnp.sum(jnp.sum(err * err, axis=1, keepdims=True), axis=0, keepdims=True) * (0.5 / D_MODEL)
        _accumulate(loss_o, jnp.broadcast_to(part, (1, LANES)), step)
        dh2, dgn = _rms_bwd(n2, r2, gn[...], err * (1.0 / D_MODEL))
        dh2_o[...] = dh2
        _accumulate(dgn_o, dgn, step)

    return _rowwise("fwd_b2", body, [h1, act, tgt], [w["w_down"], w["g_n"]], [_sds((s, D_MODEL), F32)],
                    [_sds((1, LANES), F32), _sds((1, D_MODEL), F32)], ROW_TILE)


def _bwd_b1(dh2, gate, up, w):
    s = dh2.shape[0]

    def body(r, c, o, a, step):
        dh2_ref, gate_ref, up_ref = r
        (wdt,) = c
        dgate_o, dup_o = o
        dact = _dot(dh2_ref[...].astype(BF16), wdt[...])
        gate = gate_ref[...].astype(F32)
        sig = 1.0 / (1.0 + jnp.exp(-gate))
        dup_o[...] = (dact * (gate * sig)).astype(BF16)
        dgate_o[...] = (dact * up_ref[...].astype(F32) * (sig * (1.0 + gate * (1.0 - sig)))).astype(BF16)

    return _rowwise("bwd_b1", body, [dh2, gate, up], [w["w_down_t"]], [_sds((s, D_FF), BF16), _sds((s, D_FF), BF16)],
                    [], ROW_TILE_ELEMENTWISE)


def _bwd_b2(dgate, dup, h1, dh2, o_mla, o_sb, w):
    s = h1.shape[0]

    def body(r, c, o, a, step):
        dgate_ref, dup_ref, h1_ref, dh2_ref, oa_ref, ob_ref = r
        wgt, wut, gf, woat, wobt, ga, gb = c
        dh1_o, doa_o, dob_o = o
        dgf_o, dga_o, dgb_o = a
        df = _dot(dgate_ref[...], wgt[...]) + _dot(dup_ref[...], wut[...])
        nf, rf = _rms(h1_ref[...])
        dres, dgf = _rms_bwd(nf, rf, gf[...], df)
        dh1 = dh2_ref[...] + dres
        dh1_o[...] = dh1
        dh1b = dh1.astype(BF16)
        na, ra = _rms(oa_ref[...])
        doa, dga = _rms_bwd(na, ra, ga[...], _dot(dh1b, woat[...]))
        nb, rb = _rms(ob_ref[...])
        dob, dgb = _rms_bwd(nb, rb, gb[...], _dot(dh1b, wobt[...]))
        doa_o[...] = doa
        dob_o[...] = dob
        _accumulate(dgf_o, dgf, step)
        _accumulate(dga_o, dga, step)
        _accumulate(dgb_o, dgb, step)

    consts = [w["w_gate_t"], w["w_up_t"], w["g_f"], w["w_oa_t"], w["w_ob_t"], w["g_a"], w["g_b"]]
    outs = [_sds((s, D_MODEL), F32), _sds((s, MLA_WIDTH), F32), _sds((s, SB_WIDTH), F32)]
    accs = [_sds((1, D_MODEL), F32), _sds((1, MLA_WIDTH), F32), _sds((1, SB_WIDTH), F32)]
    return _rowwise("bwd_b2", body, [dgate, dup, h1, dh2, o_mla, o_sb], consts, outs, accs, ROW_TILE)


def _fold_pairs(t):
    return jnp.concatenate([t[:, :LANES] + t[:, LANES:2 * LANES], t[:, 2 * LANES:3 * LANES] + t[:, 3 * LANES:]], axis=1)


def _bwd_a(x, dh1, cq, ckv, dqn, dqr, dkn, dvm, dkr, dsq, dsk, dsv, tabs, w):
    s = x.shape[0]

    def body(r, c, o, a, step):
        x_ref, dh1_ref, cq_ref, ckv_ref, dqn_ref, dqr_ref, dkn_ref, dvm_ref, dkr_ref, dsq_ref, dsk_ref, dsv_ref, cos_ref, sin_ref = r
        wqnt, wqrt, wqrrt, gq, wknt, wvt, gkv, wcqt, wckvt, wkrt, wkrrt, wsqt, wskt, wsvt, gmix = c
        dx_o, a1_o, a2_o, dcq_o, dckv_o, dkrc_o, dkrs_o = o
        dgq_o, dgkv_o, dgmix_o = a
        cos, sin = cos_ref[...], sin_ref[...]
        dqr = _fold_pairs(dqr_ref[...])
        a1 = (dqr * cos).astype(BF16)
        a2 = (dqr * sin).astype(BF16)
        a1_o[...] = a1
        a2_o[...] = a2
        nq, rq = _rms(cq_ref[...])
        dcqn = _dot(dqn_ref[...].astype(BF16), wqnt[...]) + _dot(a1, wqrt[...]) + _dot(a2, wqrrt[...])
        dcq, dgq = _rms_bwd(nq, rq, gq[...], dcqn)
        nkv, rkv = _rms(ckv_ref[...])
        dckvn = _dot((dkn_ref[...] * MLA_DK_SCALE).astype(BF16), wknt[...]) + _dot(dvm_ref[...].astype(BF16), wvt[...])
        dckv, dgkv = _rms_bwd(nkv, rkv, gkv[...], dckvn)
        dkr = _fold_pairs(dkr_ref[...]) * MLA_DK_SCALE
        dcq_b = dcq.astype(BF16)
        dckv_b = dckv.astype(BF16)
        dkrc = (dkr * cos).astype(BF16)
        dkrs = (dkr * sin).astype(BF16)
        dcq_o[...] = dcq_b
        dckv_o[...] = dckv_b
        dkrc_o[...] = dkrc
        dkrs_o[...] = dkrs
        du = (_dot(dcq_b, wcqt[...]) + _dot(dckv_b, wckvt[...]) + _dot(dkrc, wkrt[...]) + _dot(dkrs, wkrrt[...])
              + _dot(dsq_ref[...].astype(BF16), wsqt[...]) + _dot(dsk_ref[...].astype(BF16), wskt[...])
              + _dot(dsv_ref[...].astype(BF16), wsvt[...]))
        nx, rx = _rms(x_ref[...])
        dres, dgmix = _rms_bwd(nx, rx, gmix[...], du)
        dx_o[...] = dh1_ref[...] + dres
        _accumulate(dgq_o, dgq, step)
        _accumulate(dgkv_o, dgkv, step)
        _accumulate(dgmix_o, dgmix, step)

    consts = [w["w_qn_t"], w["w_qr_t"], w["w_qrr_t"], w["g_q"], w["w_kn_t"], w["w_v_t"], w["g_kv"], w["w_cq_t"], w["w_ckv_t"],
              w["w_kr8_t"], w["w_kr8r_t"], w["w_sbq_t"], w["w_sbk_t"], w["w_sbv_t"], w["g_mix"]]
    rope_w = MLA_HEADS * MLA_ROPE
    outs = [_sds((s, D_MODEL), F32), _sds((s, rope_w), BF16), _sds((s, rope_w), BF16), _sds((s, Q_RANK), BF16),
            _sds((s, KV_RANK), BF16), _sds((s, rope_w), BF16), _sds((s, rope_w), BF16)]
    accs = [_sds((1, Q_RANK), F32), _sds((1, KV_RANK), F32), _sds((1, D_MODEL), F32)]
    rows = [x, dh1, cq, ckv, dqn, dqr, dkn, dvm, dkr, dsq, dsk, dsv, tabs["cos"], tabs["sin"]]
    return _rowwise("bwd_a", body, rows, consts, outs, accs, ROW_TILE)


def _tn_multi(name, x, ys):
    s, k = x.shape
    ts = min(TN_TS, s)
    n_y = len(ys)

    def kern(*refs):
        step = pl.program_id(0)
        xb = refs[0][...].astype(BF16)
        for j in range(n_y):
            _accumulate(refs[1 + n_y + j], _dot_tn(xb, refs[1 + j][...].astype(BF16)), step)

    return pl.pallas_call(
        kern, name=name, grid=(s // ts,),
        in_specs=[pl.BlockSpec((ts, k), lambda i: (i, 0))] + [pl.BlockSpec((ts, y.shape[1]), lambda i: (i, 0)) for y in ys],
        out_specs=[pl.BlockSpec((k, y.shape[1]), lambda i: (0, 0)) for y in ys],
        out_shape=[_sds((k, y.shape[1]), F32) for y in ys], compiler_params=_params(1),
    )(x, *ys)


def _tn_tile(k, n):
    if n % LANES or k * n * 4 <= TN_ACC_BYTES:
        return n
    units = n // LANES
    best = 1
    for d in range(1, units + 1):
        if units % d == 0 and k * d * LANES * 4 <= TN_ACC_BYTES:
            best = d
    return best * LANES


def _tn_matmul(name, x, y):
    s, k = x.shape
    n = y.shape[1]
    ts = min(TN_TS, s)
    tn = _tn_tile(k, n)

    def kern(x_ref, y_ref, o_ref):
        step = pl.program_id(1)
        _accumulate(o_ref, _dot_tn(x_ref[...].astype(BF16), y_ref[...].astype(BF16)), step)

    return pl.pallas_call(
        kern, name=name, grid=(n // tn, s // ts),
        in_specs=[pl.BlockSpec((ts, k), lambda j, i: (i, 0)), pl.BlockSpec((ts, tn), lambda j, i: (i, j))],
        out_specs=pl.BlockSpec((k, tn), lambda j, i: (0, j)), out_shape=_sds((k, n), F32), compiler_params=_params(2),
    )(x, y)


def _lanes(rows, lo, width):
    lane = lax.broadcasted_iota(jnp.int32, (rows, LANES), 1)
    return jnp.logical_and(lane >= lo, lane < lo + width)


def _keep(mask, t):
    return jnp.where(mask, t, jnp.zeros_like(t))


def _mla_qcat(qn_ref, qr_ref, rope_lo, half, rows):
    qn = _keep(_lanes(rows, MLA_NOPE * half, MLA_NOPE), qn_ref[...])
    qr = _keep(_lanes(rows, rope_lo, MLA_ROPE), qr_ref[...])
    return jnp.concatenate([qn, qr], axis=1)


def _diag_mask(rows, width, row0, col0):
    row = lax.broadcasted_iota(jnp.int32, (rows, width), 0)
    col = lax.broadcasted_iota(jnp.int32, (rows, width), 1)
    return col + (col0 - row0) <= row


def _mla_fwd(qn, qr, kn, kr, v, riders=(), tq=MLA_TQ, tk=MLA_TK, td=MLA_TQ):
    s = qn.shape[0]
    tq, tk, td = min(tq, s), min(tk, s), min(td, s)
    ratio = tq // tk

    n_ride = len(riders)
    n_pairs = MLA_HEADS // 2

    def kern(qn_ref, qr_ref, kn_ref, kr_ref, v_ref, *rest):
        o_ref, lse_ref = rest[n_ride:n_ride + 2]
        g = pl.program_id(0)
        i = pl.program_id(1)
        if n_ride:
            send, forward, finish = _gather_steps(rest[:n_ride], rest[n_ride + 2:2 * n_ride + 2], *rest[2 * n_ride + 2:])
            pl.when(jnp.logical_and(g == 0, i == 0))(send)
            pl.when(jnp.logical_and(g == 1, i == 0))(forward)
        for half in range(2):
            qcat = _mla_qcat(qn_ref, qr_ref, MLA_ROPE * (2 * (g % 2) + half), half, tq)

            def block(k0, width, carry, row0, masked, qcat=qcat):
                m, l, acc = (c[row0:] for c in carry)
                ks = pl.ds(pl.multiple_of(k0, width), width)
                kcat = jnp.concatenate([kn_ref[ks, :], kr_ref[ks, :]], axis=1)
                sc = _dot_nt(qcat[row0:], kcat)
                if masked:
                    sc = jnp.where(_diag_mask(tq - row0, width, row0, row0), sc, NEG)
                m_new = jnp.maximum(m, jnp.max(sc, axis=1, keepdims=True))
                p = jnp.exp2(sc - m_new)
                alpha = jnp.exp2(m - m_new)
                l = alpha * l + jnp.sum(p, axis=1, keepdims=True)
                acc = alpha * acc + _dot(p.astype(BF16), v_ref[ks, :])
                new = (m_new, l, acc)
                return new if row0 == 0 else tuple(jnp.concatenate([c[:row0], n], axis=0) for c, n in zip(carry, new))

            carry = (jnp.full((tq, 1), NEG, F32), jnp.zeros((tq, 1), F32), jnp.zeros((tq, LANES), F32))
            carry = lax.fori_loop(0, i * ratio, lambda kb, c, block=block: block(kb * tk, tk, c, 0, False), carry)
            for row0 in range(0, tq, td):
                carry = block(i * tq + row0, td, carry, row0, True)
            m, l, acc = carry
            out = _keep(_lanes(tq, MLA_V * half, MLA_V), acc / l)
            lse = _keep(_lanes(tq, MLA_ROPE * half, MLA_ROPE), jnp.broadcast_to(m + jnp.log2(l), (tq, LANES)))
            if half == 0:
                o_ref[...] = out
                lse_ref[...] = lse
            else:
                o_ref[...] += out
                lse_ref[...] += lse
        if n_ride:
            pl.when(jnp.logical_and(g == n_pairs - 1, i == s // tq - 1))(finish)

    qblk = pl.BlockSpec((tq, LANES), lambda g, i: (i, g))
    full = pl.BlockSpec((s, LANES), lambda g, i: (0, g))
    outs = pl.pallas_call(
        kern, name="mla_fwd", grid=(n_pairs, s // tq),
        in_specs=[qblk, pl.BlockSpec((tq, LANES), lambda g, i: (i, g // 2)), full, pl.BlockSpec((s, LANES), lambda g, i: (0, 0)), full]
        + [HBM_SPEC] * n_ride,
        out_specs=[qblk, qblk] + [HBM_SPEC] * n_ride,
        out_shape=[_sds((s, MLA_WIDTH), F32), _sds((s, n_pairs * LANES), F32)] + [_sds((N_SHARD,) + a.shape, a.dtype) for a in riders],
        scratch_shapes=_gather_sems(n_ride) if n_ride else [], compiler_params=_params(2),
    )(qn, qr, kn, kr, v, *riders)
    return outs[0], outs[1], outs[2:]


def _mla_bwd(qn, qr, kn, kr, v, o, do, lse, riders=(), tq=MLA_TQ, tk=MLA_BWD_TK, td=MLA_DIAG_TK):
    s = qn.shape[0]
    tq, tk, td = min(tq, s), min(tk, s), min(td, s)
    ratio = tq // tk

    n_ride = len(riders)
    n_pairs = MLA_HEADS // 2

    def kern(qn_ref, qr_ref, kn_ref, kr_ref, v_ref, o_ref, do_ref, lse_ref, *rest):
        dqn_ref, dqr_ref, dkn_ref, dkr_ref, dv_ref = rest[n_ride:n_ride + 5]
        g = pl.program_id(0)
        i = pl.program_id(1)
        if n_ride:
            start, finish = _scatter_steps(rest[:n_ride], rest[n_ride + 5:2 * n_ride + 5], *rest[2 * n_ride + 5:])
            pl.when(jnp.logical_and(g == 0, i == 0))(start)

        @pl.when(i == 0)
        def _():
            dkn_ref[...] = jnp.zeros_like(dkn_ref)
            dkr_ref[...] = jnp.zeros_like(dkr_ref)
            dv_ref[...] = jnp.zeros_like(dv_ref)

        for half in range(2):
            rope_lo = MLA_ROPE * (2 * (g % 2) + half)
            qcat = _mla_qcat(qn_ref, qr_ref, rope_lo, half, tq)
            mine = _lanes(tq, MLA_V * half, MLA_V)
            do_f = _keep(mine, do_ref[...])
            do_b = do_f.astype(BF16)
            delta = jnp.sum(do_f * o_ref[...], axis=1, keepdims=True)
            lse_v = lse_ref[:, MLA_ROPE * half:MLA_ROPE * half + 1]

            def block(k0, width, dq_acc, row0, masked, qcat=qcat, do_b=do_b, delta=delta, lse_v=lse_v):
                ks = pl.ds(pl.multiple_of(k0, width), width)
                kcat = jnp.concatenate([kn_ref[ks, :], kr_ref[ks, :]], axis=1)
                qc, dob = qcat[row0:], do_b[row0:]
                p = jnp.exp2(_dot_nt(qc, kcat) - lse_v[row0:])
                if masked:
                    p = jnp.where(_diag_mask(tq - row0, width, row0, row0), p, 0.0)
                ds = (p * (_dot_nt(dob, v_ref[ks, :]) - delta[row0:])).astype(BF16)
                dv_ref[ks, :] += _dot_tn(p.astype(BF16), dob)
                dkc = _dot_tn(ds, qc)
                dkn_ref[ks, :] += dkc[:, :LANES]
                dkr_ref[ks, :] += dkc[:, LANES:]
                new = dq_acc[row0:] + _dot(ds, kcat)
                return new if row0 == 0 else jnp.concatenate([dq_acc[:row0], new], axis=0)

            acc = lax.fori_loop(0, i * ratio, lambda kb, c, block=block: block(kb * tk, tk, c, 0, False),
                                jnp.zeros((tq, 2 * LANES), F32))
            for row0 in range(0, tq, td):
                acc = block(i * tq + row0, td, acc, row0, True)
            dqn = _keep(_lanes(tq, MLA_NOPE * half, MLA_NOPE), acc[:, :LANES] * MLA_SCALE)
            dqr = _keep(_lanes(tq, rope_lo, MLA_ROPE), acc[:, LANES:] * MLA_SCALE)
            if half == 0:
                dqn_ref[...] = dqn
                dqr_ref[...] = dqr
            else:
                dqn_ref[...] += dqn
                dqr_ref[...] += dqr
        if n_ride:
            pl.when(jnp.logical_and(g == n_pairs - 1, i == s // tq - 1))(finish)

    qblk = pl.BlockSpec((tq, LANES), lambda g, i: (i, g))
    full = pl.BlockSpec((s, LANES), lambda g, i: (0, g))
    once = lambda spec_map: pl.BlockSpec((s, LANES), spec_map, pipeline_mode=pl.Buffered(1))
    wide = _sds((s, n_pairs * LANES), F32)
    outs = pl.pallas_call(
        kern, name="mla_bwd", grid=(n_pairs, s // tq),
        in_specs=[qblk, pl.BlockSpec((tq, LANES), lambda g, i: (i, g // 2)), once(lambda g, i: (0, g)), once(lambda g, i: (0, 0)),
                  once(lambda g, i: (0, g)), qblk, qblk, qblk] + [HBM_SPEC] * n_ride,
        out_specs=[qblk, qblk, full, full, full] + [HBM_SPEC] * n_ride,
        out_shape=[wide] * 5 + [_sds((N_SHARD - 1,) + p.shape[1:], p.dtype) for p in riders],
        scratch_shapes=_scatter_sems(n_ride) if n_ride else [], compiler_params=_params(2),
    )(qn, qr, kn, kr, v, o, do, lse, *riders)
    return outs[:5], outs[5:]


def _sb_masks(tk):
    j = lax.broadcasted_iota(jnp.int32, (tk, tk), 0)
    c = lax.broadcasted_iota(jnp.int32, (tk, tk), 1)
    return (j > c).astype(BF16), (j < c).astype(BF16)


def _sb_scores(qs, kk, msuf, strict):
    z = _dot_nt(qs, kk)
    lom = -(jnp.maximum(z, 0.0) + jnp.log(1.0 + jnp.exp(-jnp.abs(z))))
    if strict is not None:
        lom = jnp.where(strict, lom, 0.0)
    hi = lom.astype(BF16)
    lo = (lom - hi.astype(F32)).astype(BF16)
    return z, lom, _dot(hi, msuf) + _dot(lo, msuf)


def _sb_strict(tq, tk, d):
    row = lax.broadcasted_iota(jnp.int32, (tq, tk), 0)
    col = lax.broadcasted_iota(jnp.int32, (tq, tk), 1)
    return col + d * tk < row


def _sb_fwd(q, k, v, msuf, tq=SB_TQ, tk=SB_TK):
    s = q.shape[0]
    tq, tk = min(tq, s), min(tk, s)
    ratio = tq // tk

    def kern(q_ref, k_ref, v_ref, m_ref, o_ref, c_ref):
        i = pl.program_id(1)
        msf = m_ref[...]
        lane = lax.broadcasted_iota(jnp.int32, (tq, LANES), 1)
        for half in range(2):
            mine = _lanes(tq, SB_DIM * half, SB_DIM)
            qs = _keep(mine, q_ref[...]) * 0.125

            def block(kb, carry, dd, qs=qs):
                c, acc, cm = carry
                ks = pl.ds(pl.multiple_of(kb * tk, tk), tk)
                strict = None if dd is None else _sb_strict(tq, tk, dd)
                z, lom, suf = _sb_scores(qs, k_ref[ks, :], msf, strict)
                a = jnp.exp(z + lom + (suf + c))
                if strict is not None:
                    a = jnp.where(strict, a, 0.0)
                acc = acc + _dot(a.astype(BF16), v_ref[ks, :])
                cm = jnp.where(lane == kb, c, cm)
                return c + jnp.sum(lom, axis=1, keepdims=True), acc, cm

            carry = (jnp.zeros((tq, 1), F32), jnp.zeros((tq, LANES), F32), jnp.full((tq, LANES), NEG, F32))
            for dd in range(ratio - 1, -1, -1):
                carry = block(i * ratio + dd, carry, dd)

            def live(st):
                return jnp.logical_and(st[0] >= 0, jnp.max(st[1]) > -SB_SKIP)

            def step(st, block=block):
                return (st[0] - 1, *block(st[0], st[1:], None))

            _, _, acc, cm = lax.while_loop(live, step, (i * ratio - 1, *carry))
            if half == 0:
                o_ref[...] = _keep(mine, acc)
            else:
                o_ref[...] += _keep(mine, acc)
            c_ref[:, LANES * half:LANES * (half + 1)] = cm

    qblk = lambda n: pl.BlockSpec((tq, n), lambda g, i: (i, g))
    full = pl.BlockSpec((s, LANES), lambda g, i: (0, g))
    return pl.pallas_call(
        kern, name="sb_fwd", grid=(SB_HEADS // 2, s // tq),
        in_specs=[qblk(LANES), full, full, pl.BlockSpec((tk, tk), lambda g, i: (0, 0))],
        out_specs=[qblk(LANES), qblk(2 * LANES)],
        out_shape=[_sds((s, SB_WIDTH), F32), _sds((s, SB_HEADS * LANES), F32)], compiler_params=_params(2),
    )(q, k, v, msuf)


def _sb_bwd(q, k, v, do, cmat, msuf, mpre, tq=SB_TQ, tk=SB_TK):
    s = q.shape[0]
    tq, tk = min(tq, s), min(tk, s)
    ratio = tq // tk

    def kern(q_ref, k_ref, v_ref, do_ref, c_ref, ms_ref, mp_ref, dq_ref, dk_ref, dv_ref):
        i = pl.program_id(1)

        @pl.when(i == 0)
        def _():
            dk_ref[...] = jnp.zeros_like(dk_ref)
            dv_ref[...] = jnp.zeros_like(dv_ref)

        msf = ms_ref[...]
        mpf = mp_ref[...]
        lane = lax.broadcasted_iota(jnp.int32, (tq, LANES), 1)
        lane1 = lax.broadcasted_iota(jnp.int32, (1, LANES), 1)
        for half in range(2):
            mine = _lanes(tq, SB_DIM * half, SB_DIM)
            qv = _keep(mine, q_ref[...])
            qs = qv * 0.125
            do_b = _keep(mine, do_ref[...]).astype(BF16)
            cm = c_ref[:, LANES * half:LANES * (half + 1)]

            def block(kb, carry, dd, qv=qv, qs=qs, do_b=do_b, cm=cm):
                dq_acc, pc = carry
                ks = pl.ds(pl.multiple_of(kb * tk, tk), tk)
                kk = k_ref[ks, :]
                strict = None if dd is None else _sb_strict(tq, tk, dd)
                z, lom, suf = _sb_scores(qs, kk, msf, strict)
                c = jnp.sum(jnp.where(lane == kb, cm, 0.0), axis=1, keepdims=True)
                a = jnp.exp(z + lom + (suf + c))
                if strict is not None:
                    a = jnp.where(strict, a, 0.0)
                g = _dot_nt(do_b, v_ref[ks, :]) * a
                p = pc + _dot(g.astype(BF16), mpf)
                omb = jnp.exp(lom)
                dz = (g * omb - (1.0 - omb) * p) * 0.125
                if strict is not None:
                    dz = jnp.where(strict, dz, 0.0)
                dz = dz.astype(BF16)
                dv_ref[ks, :] += _dot_tn(a.astype(BF16), do_b)
                dk_ref[ks, :] += _dot_tn(dz, qv)
                return dq_acc + _dot(dz, kk), pc + jnp.sum(g, axis=1, keepdims=True)

            seen = jnp.logical_and(jnp.max(cm, axis=0, keepdims=True) > -SB_SKIP, lane1 < i * ratio)
            first = i * ratio - jnp.sum(seen.astype(jnp.int32))
            carry = (jnp.zeros((tq, LANES), F32), jnp.zeros((tq, 1), F32))
            carry = lax.fori_loop(first, i * ratio, lambda kb, c, block=block: block(kb, c, None), carry)
            for dd in range(ratio):
                carry = block(i * ratio + dd, carry, dd)
            if half == 0:
                dq_ref[...] = _keep(mine, carry[0])
            else:
                dq_ref[...] += _keep(mine, carry[0])

    qblk = lambda n: pl.BlockSpec((tq, n), lambda g, i: (i, g))
    full = pl.BlockSpec((s, LANES), lambda g, i: (0, g))
    msk = pl.BlockSpec((tk, tk), lambda g, i: (0, 0))
    return pl.pallas_call(
        kern, name="sb_bwd", grid=(SB_HEADS // 2, s // tq),
        in_specs=[qblk(LANES), full, full, qblk(LANES), qblk(2 * LANES), msk, msk],
        out_specs=[qblk(LANES), full, full],
        out_shape=[_sds((s, SB_WIDTH), F32)] * 3, compiler_params=_params(2),
    )(q, k, v, do, cmat, msuf, mpre)


def _place():
    return lax.axis_index("x"), lax.axis_index("y"), lax.axis_index("c")


def _other_chips(x, y):
    return [(1 - x, y), (x, 1 - y), (1 - x, 1 - y)]


HBM_SPEC = pl.BlockSpec(memory_space=pl.ANY)


def _gather_steps(ins, outs, send_sems, recv_sems):
    n = len(ins)
    x, y, c = _place()
    sibling = (x, y, 1 - c)
    chips = _other_chips(x, y)

    def half_of(a, ref, pc):
        half = ins[a].shape[0] // 2
        return ref.at[pl.ds(pl.multiple_of(pc * half, 16), half), :]

    def copy(a, k, chip, pc, to, src=None):
        dst = half_of(a, outs[a].at[2 * chip[0] + chip[1]], pc)
        return pltpu.make_async_remote_copy(src_ref=dst if src is None else src, dst_ref=dst, send_sem=send_sems.at[6 * a + k],
                                            recv_sem=recv_sems.at[6 * a + k], device_id=to, device_id_type=MESH)

    def first():
        return [copy(a, j, (x, y), c, (*chip, c), src=half_of(a, ins[a], c)) for a in range(n) for j, chip in enumerate(chips)]

    def passed():
        return [copy(a, 3 + j, chip, c, sibling) for j, chip in enumerate(chips) for a in range(n)]

    def send():
        for cp in first():
            cp.start()

    def forward():
        for j, chip in enumerate(chips):
            for a in range(n):
                copy(a, j, chip, c, sibling).wait_recv()
        for cp in passed():
            cp.start()

    def finish():
        for j, chip in enumerate(chips):
            for a in range(n):
                copy(a, 3 + j, chip, 1 - c, sibling).wait_recv()
        for cp in first() + passed():
            cp.wait_send()

    return send, forward, finish


def _gather_sems(n):
    return [pltpu.SemaphoreType.DMA((6 * n,)), pltpu.SemaphoreType.DMA((6 * n,))]


def _allgather_list(name, shards):
    n = len(shards)

    def body(*refs):
        for stage in _gather_steps(refs[:n], refs[n:2 * n], *refs[2 * n:]):
            stage()

    return pl.pallas_call(
        body, name=name, out_shape=[_sds((N_SHARD,) + a.shape, a.dtype) for a in shards], in_specs=[HBM_SPEC] * n,
        out_specs=[HBM_SPEC] * n, scratch_shapes=_gather_sems(n),
    )(*shards)


def _swap_halves(name, gs):
    n = len(gs)

    def body(*refs):
        ins, outs = refs[:n], refs[n:2 * n]
        send_sems, recv_sems = refs[2 * n:]
        x, y, c = _place()
        copies = []
        for a in range(n):
            h = gs[a].shape[1] // 2
            src = ins[a].at[:, pl.ds(pl.multiple_of((1 - c) * h, 8), h), :]
            copies.append(pltpu.make_async_remote_copy(src_ref=src, dst_ref=outs[a], send_sem=send_sems.at[a], recv_sem=recv_sems.at[a],
                                                       device_id=(x, y, 1 - c), device_id_type=MESH))
        for cp in copies:
            cp.start()
        for cp in copies:
            cp.wait()

    return pl.pallas_call(
        body, name=name, out_shape=[_sds((N_SHARD, g.shape[1] // 2, g.shape[2]), g.dtype) for g in gs],
        in_specs=[HBM_SPEC] * n, out_specs=[HBM_SPEC] * n,
        scratch_shapes=[pltpu.SemaphoreType.DMA((n,)), pltpu.SemaphoreType.DMA((n,))],
    )(*gs)


def _add_sibling(name, gs, gots, c_idx):
    n = len(gs)

    def kern(c_ref, *refs):
        for a in range(n):
            tot = refs[a][...] + refs[n + a][...]
            refs[2 * n + a][...] = tot
            refs[3 * n + a][...] = tot.astype(BF16)

    quarter = lambda g: (None, g.shape[1] // 4, g.shape[2])
    in_specs = [pl.BlockSpec(quarter(g), lambda b, s, c_ref: (b, 2 * c_ref[0] + s, 0)) for g in gs]
    in_specs += [pl.BlockSpec(quarter(g), lambda b, s, c_ref: (b, s, 0)) for g in gs]
    out_specs = [pl.BlockSpec(quarter(g), lambda b, s, c_ref: (b, s, 0)) for g in gs] * 2
    out_shape = [_sds(t.shape, F32) for t in gots] + [_sds(t.shape, BF16) for t in gots]
    outs = pl.pallas_call(
        kern, name=name, out_shape=out_shape,
        grid_spec=pltpu.PrefetchScalarGridSpec(num_scalar_prefetch=1, grid=(N_SHARD, 2), in_specs=in_specs, out_specs=out_specs),
        compiler_params=_params(2),
    )(c_idx.reshape(1), *gs, *gots)
    return outs[:n], outs[n:]


def _scatter_steps(ins, outs, send_sems, recv_sems):
    x, y, c = _place()

    def copies():
        return [pltpu.make_async_remote_copy(
            src_ref=ins[a].at[2 * px + py], dst_ref=outs[a].at[j], send_sem=send_sems.at[3 * a + j], recv_sem=recv_sems.at[3 * a + j],
            device_id=(px, py, c), device_id_type=MESH) for a in range(len(ins)) for j, (px, py) in enumerate(_other_chips(x, y))]

    def start():
        for cp in copies():
            cp.start()

    def finish():
        for cp in copies():
            cp.wait()

    return start, finish


def _scatter_sems(n):
    return [pltpu.SemaphoreType.DMA((3 * n,)), pltpu.SemaphoreType.DMA((3 * n,))]


def _chip_scatter(ps):
    n = len(ps)

    def body(*refs):
        for stage in _scatter_steps(refs[:n], refs[n:2 * n], *refs[2 * n:]):
            stage()

    return pl.pallas_call(
        body, name="chip_scatter", out_shape=[_sds((N_SHARD - 1,) + p.shape[1:], p.dtype) for p in ps], in_specs=[HBM_SPEC] * n,
        out_specs=[HBM_SPEC] * n, scratch_shapes=_scatter_sems(n),
    )(*ps)


def _add_chips(name, ps, others, shard_idx):
    n = len(ps)

    def kern(b_ref, *refs):
        for a in range(n):
            tot = refs[a][...]
            for j in range(N_SHARD - 1):
                tot = tot + refs[n + a][j].astype(F32)
            refs[2 * n + a][...] = tot

    in_specs = [pl.BlockSpec((None, p.shape[1] // 2, p.shape[2]), lambda s, b_ref: (b_ref[0], s, 0)) for p in ps]
    in_specs += [pl.BlockSpec((N_SHARD - 1, p.shape[1] // 2, p.shape[2]), lambda s, b_ref: (0, s, 0)) for p in ps]
    out_specs = [pl.BlockSpec((p.shape[1] // 2, p.shape[2]), lambda s, b_ref: (s, 0)) for p in ps]
    return pl.pallas_call(
        kern, name=name, out_shape=[_sds(p.shape[1:], F32) for p in ps],
        grid_spec=pltpu.PrefetchScalarGridSpec(num_scalar_prefetch=1, grid=(2,), in_specs=in_specs, out_specs=out_specs),
        compiler_params=_params(1),
    )(shard_idx.reshape(1), *ps, *others)


def _swap_result(name, mines):
    n = len(mines)

    def body(*refs):
        ins, outs = refs[:n], refs[n:2 * n]
        send_sems, recv_sems = refs[2 * n:]
        x, y, c = _place()
        copies = [pltpu.make_async_remote_copy(src_ref=ins[a], dst_ref=outs[a], send_sem=send_sems.at[a], recv_sem=recv_sems.at[a],
                                               device_id=(x, y, 1 - c), device_id_type=MESH) for a in range(n)]
        for cp in copies:
            cp.start()
        for cp in copies:
            cp.wait()

    return pl.pallas_call(
        body, name=name, out_shape=[_sds(m.shape, m.dtype) for m in mines], in_specs=[HBM_SPEC] * n,
        out_specs=[HBM_SPEC] * n, scratch_shapes=[pltpu.SemaphoreType.DMA((n,)), pltpu.SemaphoreType.DMA((n,))],
    )(*mines)


def _allreduce_small(v):
    m_per, n = v.shape

    def body(x_ref, tot_ref, all_ref, send_sems, recv_sems, local_sem):
        x, y, c = _place()
        me, sibling = (x, y, c), (x, y, 1 - c)
        chips = _other_chips(x, y)

        def rows(px, py, pc):
            return all_ref.at[pl.ds(pl.multiple_of((4 * px + 2 * py + pc) * m_per, 8), m_per), :]

        def copy(k, block, to, src=None):
            return pltpu.make_async_remote_copy(
                src_ref=rows(*block) if src is None else src, dst_ref=rows(*block), send_sem=send_sems.at[k],
                recv_sem=recv_sems.at[k], device_id=to, device_id_type=MESH)

        mine = pltpu.make_async_copy(x_ref, rows(*me), local_sem)
        mine.start()
        first = [copy(0, me, sibling, src=x_ref)] + [copy(1 + j, me, (*chip, c), src=x_ref) for j, chip in enumerate(chips)]
        for cp in first:
            cp.start()
        passed = [copy(4 + j, (*chip, c), sibling) for j, chip in enumerate(chips)]
        for j, chip in enumerate(chips):
            copy(1 + j, (*chip, c), me).wait_recv()
            passed[j].start()
        copy(0, sibling, me).wait_recv()
        for j, chip in enumerate(chips):
            copy(4 + j, (*chip, 1 - c), me).wait_recv()
        for cp in first + passed:
            cp.wait_send()
        mine.wait()
        tot = all_ref[0:m_per, :]
        for dev in range(1, 8):
            tot = tot + all_ref[dev * m_per:(dev + 1) * m_per, :]
        tot_ref[...] = tot

    vmem = pl.BlockSpec(memory_space=pltpu.VMEM)
    return pl.pallas_call(
        body, name="allreduce_small", out_shape=_sds((m_per, n), F32), in_specs=[vmem], out_specs=vmem,
        scratch_shapes=[pltpu.VMEM((8 * m_per, n), F32), pltpu.SemaphoreType.DMA((7,)), pltpu.SemaphoreType.DMA((7,)),
                        pltpu.SemaphoreType.DMA],
    )(v)


def _adam_update(w, g, m, v):
    m_new = ADAM_B1 * m + (1.0 - ADAM_B1) * g
    v_new = ADAM_B2 * v + (1.0 - ADAM_B2) * (g * g)
    m_hat = m_new / (1.0 - ADAM_B1 ** ADAM_STEP)
    v_hat = v_new / (1.0 - ADAM_B2 ** ADAM_STEP)
    return -ADAM_LR * (m_hat / (jnp.sqrt(v_hat) + ADAM_EPS) + ADAM_WD * w), m_new, v_new


def _adamw(name, w, g, m, v):
    rows, width = w.shape
    tr = rows // 4 if rows % 32 == 0 else rows

    def kern(w_ref, g_ref, m_ref, v_ref, d_ref, mo_ref, vo_ref):
        d_ref[...], mo_ref[...], vo_ref[...] = _adam_update(w_ref[...], g_ref[...], m_ref[...], v_ref[...])

    spec = pl.BlockSpec((tr, width), lambda i: (i, 0))
    return pl.pallas_call(kern, name=name, grid=(rows // tr,), in_specs=[spec] * 4, out_specs=[spec] * 3,
                          out_shape=[_sds((rows, width), F32)] * 3, compiler_params=_params(1))(w, g, m, v)


def _adamw_halves(name, w, mine, theirs, m, v, c_idx):
    rows, width = w.shape
    tr = rows // 4

    def kern(c_ref, w_ref, mine_ref, theirs_ref, m_ref, v_ref, g_ref, d_ref, mo_ref, vo_ref):
        g = jnp.where(pl.program_id(0) == c_ref[0], mine_ref[...], theirs_ref[...])
        g_ref[...] = g
        d_ref[...], mo_ref[...], vo_ref[...] = _adam_update(w_ref[...], g, m_ref[...], v_ref[...])

    whole = pl.BlockSpec((tr, width), lambda h, j, c_ref: (2 * h + j, 0))
    part = pl.BlockSpec((tr, width), lambda h, j, c_ref: (j, 0))
    return pl.pallas_call(
        kern, name=name, out_shape=[_sds((rows, width), F32)] * 4,
        grid_spec=pltpu.PrefetchScalarGridSpec(num_scalar_prefetch=1, grid=(2, 2), in_specs=[whole, part, part, whole, whole],
                                               out_specs=[whole] * 4),
        compiler_params=_params(2),
    )(c_idx.reshape(1), w, mine, theirs, m, v)


SHARDED = (("w_in", D_MODEL, IN_WIDTH, 1), ("w_uq", Q_RANK, MLA_HEADS * MLA_QK, 1),
           ("w_ukv", KV_RANK, MLA_HEADS * (MLA_NOPE + MLA_V), 1), ("w_o", D_MODEL, D_MODEL, 0),
           ("w_gate", D_MODEL, D_FF, 1), ("w_up", D_MODEL, D_FF, 1), ("w_down", D_FF, D_MODEL, 0))
EARLY = ("w_in", "w_uq", "w_ukv")
LATE = ("w_o", "w_gate", "w_up", "w_down")
FLIPPED = ("w_gate", "w_up")
SMALL = (("norm_mix", D_MODEL), ("q_latent_norm", Q_RANK), ("kv_latent_norm", KV_RANK), ("out_norm_mla", MLA_WIDTH),
         ("out_norm_sb", SB_WIDTH), ("norm_ffn", D_MODEL), ("norm_final", D_MODEL))


def _full_weight(gathered, axis):
    n_sh, k, n = gathered.shape
    return gathered.transpose(1, 0, 2).reshape(k, n_sh * n) if axis == 1 else gathered.reshape(n_sh * k, n)


def _shard_major(g, axis):
    r, c = g.shape
    return g.reshape(r, N_SHARD, c // N_SHARD).transpose(1, 0, 2) if axis == 1 else g.reshape(N_SHARD, r // N_SHARD, c)


def _rot_cols(w):
    hh = MLA_ROPE // 2
    return jnp.concatenate([-w[..., hh:], w[..., :hh]], axis=-1)


def _rot_cols_t(g):
    hh = MLA_ROPE // 2
    return jnp.concatenate([g[..., hh:], -g[..., :hh]], axis=-1)


def _with_transposes(w):
    w.update({name + "_t": t.T for name, t in list(w.items())})
    return w


def _attention_weights(full, small):
    w_in = full["w_in"]
    s0, s1, s2 = Q_RANK, Q_RANK + KV_RANK, Q_RANK + KV_RANK + MLA_ROPE
    uq = full["w_uq"].reshape(Q_RANK, MLA_HEADS, MLA_QK)
    ukv = full["w_ukv"].reshape(KV_RANK, MLA_HEADS, MLA_NOPE + MLA_V)
    w_kr = w_in[:, s1:s2]
    per_tile = ROPE_TILE // MLA_ROPE
    w = _with_transposes({
        "w_cq": w_in[:, :s0], "w_ckv": w_in[:, s0:s1],
        "w_kr4": jnp.tile(w_kr, (1, per_tile)), "w_kr4r": jnp.tile(_rot_cols(w_kr), (1, per_tile)),
        "w_kr8": jnp.tile(w_kr, (1, MLA_HEADS)), "w_kr8r": jnp.tile(_rot_cols(w_kr), (1, MLA_HEADS)),
        "w_sbq": w_in[:, s2:s2 + SB_WIDTH], "w_sbk": w_in[:, s2 + SB_WIDTH:s2 + 2 * SB_WIDTH], "w_sbv": w_in[:, s2 + 2 * SB_WIDTH:],
        "w_qn": uq[..., :MLA_NOPE].reshape(Q_RANK, -1), "w_qr": uq[..., MLA_NOPE:].reshape(Q_RANK, -1),
        "w_qrr": _rot_cols(uq[..., MLA_NOPE:]).reshape(Q_RANK, -1),
        "w_kn": ukv[..., :MLA_NOPE].reshape(KV_RANK, -1), "w_v": ukv[..., MLA_NOPE:].reshape(KV_RANK, -1),
    })
    w.update(g_mix=small["norm_mix"], g_q=small["q_latent_norm"], g_kv=small["kv_latent_norm"], g_a=small["out_norm_mla"],
             g_b=small["out_norm_sb"], g_f=small["norm_ffn"], g_n=small["norm_final"])
    return w


def _ffn_weights(full):
    w = _with_transposes({"w_oa": full["w_o"][:MLA_WIDTH], "w_ob": full["w_o"][MLA_WIDTH:], "w_down": full["w_down"]})
    for name in FLIPPED:
        w[name + "_t"] = full[name]
        w[name] = full[name].T
    return w


def _rope_tables(positions):
    inv_freq = ROPE_THETA ** (-jnp.arange(0, MLA_ROPE, 2, dtype=F32) / MLA_ROPE)
    ang = positions.astype(F32)[:, None] * inv_freq[None, :]
    cos, sin = jnp.cos(ang), jnp.sin(ang)
    return {"cos": jnp.tile(jnp.concatenate([cos, cos], axis=1), (1, MLA_HEADS)),
            "sin": jnp.tile(jnp.concatenate([sin, sin], axis=1), (1, MLA_HEADS))}


def _by_head(g_wide, g_narrow, wide, narrow):
    r = g_wide.shape[0]
    return jnp.concatenate([g_wide.reshape(r, MLA_HEADS, wide), g_narrow.reshape(r, MLA_HEADS, narrow)], axis=-1).reshape(r, -1)


def kernel(x, positions, norm_mix, w_in, q_latent_norm, w_uq, kv_latent_norm, w_ukv, out_norm_mla, out_norm_sb, w_o, norm_ffn, w_gate, w_up, w_down, norm_final, loss_target, m_norm_mix, m_w_in, m_q_latent_norm, m_w_uq, m_kv_latent_norm, m_w_ukv, m_out_norm_mla, m_out_norm_sb, m_w_o, m_norm_ffn, m_w_gate, m_w_up, m_w_down, m_norm_final, v_norm_mix, v_w_in, v_q_latent_norm, v_w_uq, v_kv_latent_norm, v_w_ukv, v_out_norm_mla, v_out_norm_sb, v_w_o, v_norm_ffn, v_w_gate, v_w_up, v_w_down, v_norm_final):
    given = dict(norm_mix=norm_mix, w_in=w_in, q_latent_norm=q_latent_norm, w_uq=w_uq, kv_latent_norm=kv_latent_norm, w_ukv=w_ukv,
                 out_norm_mla=out_norm_mla, out_norm_sb=out_norm_sb, w_o=w_o, norm_ffn=norm_ffn, w_gate=w_gate, w_up=w_up,
                 w_down=w_down, norm_final=norm_final)
    mom_m = dict(norm_mix=m_norm_mix, w_in=m_w_in, q_latent_norm=m_q_latent_norm, w_uq=m_w_uq, kv_latent_norm=m_kv_latent_norm,
                 w_ukv=m_w_ukv, out_norm_mla=m_out_norm_mla, out_norm_sb=m_out_norm_sb, w_o=m_w_o, norm_ffn=m_norm_ffn,
                 w_gate=m_w_gate, w_up=m_w_up, w_down=m_w_down, norm_final=m_norm_final)
    mom_v = dict(norm_mix=v_norm_mix, w_in=v_w_in, q_latent_norm=v_q_latent_norm, w_uq=v_w_uq, kv_latent_norm=v_kv_latent_norm,
                 w_ukv=v_w_ukv, out_norm_mla=v_out_norm_mla, out_norm_sb=v_out_norm_sb, w_o=v_w_o, norm_ffn=v_norm_ffn,
                 w_gate=v_w_gate, w_up=v_w_up, w_down=v_w_down, norm_final=v_norm_final)
    xs = x[0]
    tgt = loss_target[0]
    s = xs.shape[0]
    c_idx = lax.axis_index("c")
    shard_idx = 2 * lax.axis_index("x") + lax.axis_index("y")

    def block2d(t, name):
        t = t.reshape(t.shape[-2:])
        return t.T if name in FLIPPED else t

    shard2d = {name: block2d(given[name], name) for name, *_ in SHARDED}
    is_mine = (jnp.arange(N_SHARD) == shard_idx)[:, None, None]
    local = {name: shard2d[name].astype(BF16) for name, *_ in SHARDED}
    axis_of = {name: 0 if name in FLIPPED else axis for name, _, _, axis in SHARDED}

    def whole(names, gathered):
        return {name: _full_weight(jnp.where(is_mine, local[name][None], t), axis_of[name]) for name, t in zip(names, gathered)}

    small = {name: given[name].reshape(1, n) for name, n in SMALL}
    w = _attention_weights(whole(EARLY, _allgather_list("allgather_w", [local[name] for name in EARLY])), small)
    tabs = _rope_tables(positions[0])
    msuf, mpre = _sb_masks(min(SB_TK, s))

    u, cq, ckv, cqn, ckvn, qn, qr, kn, vm, kr, sq, sk, sv = _fwd_a(xs, tabs, w)
    o_mla, lse, late = _mla_fwd(qn, qr, kn, kr, vm, [local[name] for name in LATE])
    w.update(_ffn_weights(whole(LATE, late)))
    o_sb, cmat = _sb_fwd(sq, sk, sv, msuf)
    merged, h1, f, gate, up, act = _fwd_b1(xs, o_mla, o_sb, w)
    dh2, loss_part, dg_n = _fwd_b2(h1, act, tgt, w)

    def chip_sums(tag, names, grads):
        gs = [_shard_major(grads[name], axis_of[name]) for name in names]
        return _add_sibling("add_sibling_" + tag, gs, _swap_halves("swap_halves_" + tag, gs), c_idx)

    def reduced(tag, chip_f32, others):
        mine = _add_chips("add_chips_" + tag, chip_f32, others, shard_idx)
        return tuple(mine), tuple(_swap_result("swap_result_" + tag, mine))

    dgate, dup = _bwd_b1(dh2, gate, up, w)
    dh1, do_mla, do_sb, dg_f, dg_a, dg_b = _bwd_b2(dgate, dup, h1, dh2, o_mla, o_sb, w)
    late_f32, late_bf16 = chip_sums("late", LATE, {
        "w_o": _tn_matmul("dw_o", merged, dh1), "w_gate": _tn_matmul("dw_gate", dgate, f),
        "w_up": _tn_matmul("dw_up", dup, f), "w_down": _tn_matmul("dw_down", act, dh2)})
    (dqn, dqr, dkn, dkr, dvm), late_others = _mla_bwd(qn, qr, kn, kr, vm, o_mla, do_mla, lse, late_bf16)
    mine_late, theirs_late = reduced("late", late_f32, late_others)
    dsq, dsk, dsv = _sb_bwd(sq, sk, sv, do_sb, cmat, msuf, mpre)
    dx, a1, a2, dcq, dckv, dkrc, dkrs, dg_q, dg_kv, dg_mix = _bwd_a(xs, dh1, cq, ckv, dqn, dqr, dkn, dvm, dkr, dsq, dsk, dsv, tabs, w)

    g_cq, g_ckv, g_krc, g_krs, g_sq, g_sk, g_sv = _tn_multi("dw_in", u, [dcq, dckv, dkrc, dkrs, dsq, dsk, dsv])
    g_qn, g_qr1, g_qr2 = _tn_multi("dw_uq", cqn, [dqn, a1, a2])
    g_kn, g_v = _tn_multi("dw_ukv", ckvn, [dkn, dvm])
    slots = lambda g: g.reshape(g.shape[0], MLA_HEADS, MLA_ROPE)
    g_kr = jnp.sum(slots(g_krc), axis=1) + _rot_cols_t(jnp.sum(slots(g_krs), axis=1))
    g_qr = (slots(g_qr1) + _rot_cols_t(slots(g_qr2))).reshape(Q_RANK, -1)
    early_f32, early_bf16 = chip_sums("early", EARLY, {
        "w_in": jnp.concatenate([g_cq, g_ckv, g_kr, g_sq, g_sk, g_sv], axis=1),
        "w_uq": _by_head(g_qn, g_qr, MLA_NOPE, MLA_ROPE),
        "w_ukv": _by_head(g_kn * MLA_DK_SCALE, g_v, MLA_NOPE, MLA_V)})
    mine_early, theirs_early = reduced("early", early_f32, _chip_scatter(early_bf16))
    halves = dict(zip(EARLY + LATE, zip(mine_early + mine_late, theirs_early + theirs_late)))

    small_parts = jnp.concatenate([dg_mix, dg_q, dg_kv, dg_a, dg_b, dg_f, dg_n], axis=1)
    small_g = _allreduce_small(jnp.broadcast_to(small_parts, (8, small_parts.shape[1])))[0:1]
    loss = lax.psum(loss_part[0, 0], ("x", "y", "c"))

    g_out, d_out, m_out, v_out = {}, {}, {}, {}
    for name, *_ in SHARDED:
        shape = given[name].shape
        outs = _adamw_halves("adamw_" + name, shard2d[name], *halves[name], block2d(mom_m[name], name), block2d(mom_v[name], name),
                             c_idx)
        g_out[name], d_out[name], m_out[name], v_out[name] = ((t.T if name in FLIPPED else t).reshape(shape) for t in outs)
    cat = lambda src: jnp.concatenate([src[name].reshape(1, n) for name, n in SMALL], axis=1)
    d, mn, vn = _adamw("adamw_small", cat(given), small_g, cat(mom_m), cat(mom_v))
    off = 0
    for name, n in SMALL:
        shape = given[name].shape
        g_out[name], d_out[name], m_out[name], v_out[name] = (t[:, off:off + n].reshape(shape) for t in (small_g, d, mn, vn))
        off += n

    order = ["norm_mix", "w_in", "q_latent_norm", "w_uq", "kv_latent_norm", "w_ukv", "out_norm_mla", "out_norm_sb", "w_o",
             "norm_ffn", "w_gate", "w_up", "w_down", "norm_final"]
    return (loss, dx[None], *[g_out[n] for n in order], *[d_out[n] for n in order], *[m_out[n] for n in order],
            *[v_out[n] for n in order])
```

```python
import functools
import math

import jax
import jax.numpy as jnp
from jax import lax
from jax.experimental import pallas as pl
from jax.experimental.pallas import tpu as pltpu

F32 = jnp.float32
BF16 = jnp.bfloat16
MESH = pl.DeviceIdType.MESH

D_MODEL = 1024
EPS = 1e-6
MLA_HEADS = 8
MLA_NOPE = 64
MLA_ROPE = 32
MLA_V = 64
MLA_QK = MLA_NOPE + MLA_ROPE
Q_RANK = 256
KV_RANK = 128
ROPE_THETA = 10000.0
SB_HEADS = 8
SB_DIM = 64
MLA_WIDTH = MLA_HEADS * MLA_V
SB_WIDTH = SB_HEADS * SB_DIM
D_FF = 2816
IN_WIDTH = Q_RANK + KV_RANK + MLA_ROPE + 3 * SB_WIDTH

ADAM_LR = 0.001
ADAM_B1 = 0.9
ADAM_B2 = 0.999
ADAM_EPS = 1e-08
ADAM_WD = 0.01
ADAM_STEP = 10

N_SHARD = 4
LANES = 128
ROPE_TILE = LANES
VMEM_LIMIT = 56 * 1024 * 1024
TN_ACC_BYTES = 6 * 1024 * 1024 + 512 * 1024
NEG = -1e30
MLA_SCALE = 1.0 / math.sqrt(MLA_QK)
MLA_DK_SCALE = math.log(2.0)
MLA_QSCALE = MLA_SCALE * math.log2(math.e)
SB_SKIP = 110.0

ROW_TILE = 512
ROW_TILE_ELEMENTWISE = 256
MLA_TQ = 1024
SB_TQ = 512
MLA_TK = 1024
MLA_BWD_TK = 512
MLA_DIAG_TK = 512
SB_TK = 256
TN_TS = 2048


def _dot(a, b):
    return jnp.dot(a, b, preferred_element_type=F32)


def _dot_nt(a, b):
    return lax.dot_general(a, b, (((1,), (1,)), ((), ())), preferred_element_type=F32)


def _dot_tn(a, b):
    return lax.dot_general(a, b, (((0,), (0,)), ((), ())), preferred_element_type=F32)


def _params(n_grid, vmem=VMEM_LIMIT):
    return pltpu.CompilerParams(dimension_semantics=("arbitrary",) * n_grid, vmem_limit_bytes=vmem)


def _rms(x):
    r = lax.rsqrt(jnp.mean(x * x, axis=-1, keepdims=True) + EPS)
    return x * r, r


def _rms_bwd(n, r, g, dy):
    dn = dy * g
    dx = r * (dn - n * jnp.mean(dn * n, axis=-1, keepdims=True))
    return dx, jnp.sum(dy * n, axis=0, keepdims=True)


def _accumulate(ref, val, step):
    @pl.when(step == 0)
    def _():
        ref[...] = val

    @pl.when(step != 0)
    def _():
        ref[...] += val


def _rowwise(name, body, rows, consts, row_out, acc_out, tm):
    n_rows = rows[0].shape[0]
    tm = min(tm, n_rows)
    nr, nc, no = len(rows), len(consts), len(row_out)

    def kern(*refs):
        body(refs[:nr], refs[nr:nr + nc], refs[nr + nc:nr + nc + no], refs[nr + nc + no:], pl.program_id(0))

    in_specs = [pl.BlockSpec((tm, a.shape[1]), lambda i: (i, 0)) for a in rows]
    in_specs += [pl.BlockSpec(a.shape, lambda i: (0, 0), pipeline_mode=pl.Buffered(1)) for a in consts]
    out_specs = [pl.BlockSpec((tm, s.shape[1]), lambda i: (i, 0)) for s in row_out]
    out_specs += [pl.BlockSpec(s.shape, lambda i: (0, 0)) for s in acc_out]
    return pl.pallas_call(
        kern, name=name, grid=(n_rows // tm,), in_specs=in_specs, out_specs=out_specs,
        out_shape=list(row_out) + list(acc_out), compiler_params=_params(1),
    )(*rows, *consts)


def _sds(shape, dtype):
    return jax.ShapeDtypeStruct(shape, dtype)


def _fwd_a(x, tabs, w):
    s = x.shape[0]

    def body(r, c, o, a, step):
        x_ref, cos_ref, sin_ref = r
        gmix, wcq, wckv, wkr, wkrr, wsq, wsk, wsv, gq, wqn, wqr, wqrr, gkv, wkn, wv = c
        u_o, cq_o, ckv_o, cqn_o, ckvn_o, qn_o, qr_o, kn_o, v_o, kr_o, sq_o, sk_o, sv_o = o
        cos, sin = cos_ref[...], sin_ref[...]
        n, _ = _rms(x_ref[...])
        u = (n * gmix[...]).astype(BF16)
        u_o[...] = u
        cq = _dot(u, wcq[...])
        ckv = _dot(u, wckv[...])
        kr_o[...] = (_dot(u, wkr[...]) * cos[:, :ROPE_TILE] + _dot(u, wkrr[...]) * sin[:, :ROPE_TILE]).astype(BF16)
        sq_o[...] = _dot(u, wsq[...]).astype(BF16)
        sk_o[...] = _dot(u, wsk[...]).astype(BF16)
        sv_o[...] = _dot(u, wsv[...]).astype(BF16)
        cq_o[...] = cq
        ckv_o[...] = ckv
        nq, _ = _rms(cq)
        cqn = (nq * gq[...]).astype(BF16)
        cqn_o[...] = cqn
        qn_o[...] = (_dot(cqn, wqn[...]) * MLA_QSCALE).astype(BF16)
        qr_o[...] = ((_dot(cqn, wqr[...]) * cos + _dot(cqn, wqrr[...]) * sin) * MLA_QSCALE).astype(BF16)
        nkv, _ = _rms(ckv)
        ckvn = (nkv * gkv[...]).astype(BF16)
        ckvn_o[...] = ckvn
        kn_o[...] = _dot(ckvn, wkn[...]).astype(BF16)
        v_o[...] = _dot(ckvn, wv[...]).astype(BF16)

    outs = [
        _sds((s, D_MODEL), BF16), _sds((s, Q_RANK), F32), _sds((s, KV_RANK), F32), _sds((s, Q_RANK), BF16),
        _sds((s, KV_RANK), BF16), _sds((s, MLA_HEADS * MLA_NOPE), BF16), _sds((s, MLA_HEADS * MLA_ROPE), BF16),
        _sds((s, MLA_HEADS * MLA_NOPE), BF16), _sds((s, MLA_WIDTH), BF16), _sds((s, ROPE_TILE), BF16),
        _sds((s, SB_WIDTH), BF16), _sds((s, SB_WIDTH), BF16), _sds((s, SB_WIDTH), BF16),
    ]
    consts = [w["g_mix"], w["w_cq"], w["w_ckv"], w["w_kr4"], w["w_kr4r"], w["w_sbq"], w["w_sbk"], w["w_sbv"], w["g_q"],
              w["w_qn"], w["w_qr"], w["w_qrr"], w["g_kv"], w["w_kn"], w["w_v"]]
    return _rowwise("fwd_a", body, [x, tabs["cos"], tabs["sin"]], consts, outs, [], ROW_TILE)


def _fwd_b1(x, o_mla, o_sb, w):
    s = x.shape[0]

    def body(r, c, o, a, step):
        x_ref, oa_ref, ob_ref = r
        ga, gb, woa, wob, gf, wg, wu = c
        mg_o, h1_o, f_o, gate_o, up_o, act_o = o
        na, _ = _rms(oa_ref[...])
        nb, _ = _rms(ob_ref[...])
        ma = (na * ga[...]).astype(BF16)
        mb = (nb * gb[...]).astype(BF16)
        mg_o[:, :MLA_WIDTH] = ma
        mg_o[:, MLA_WIDTH:] = mb
        h1 = x_ref[...] + _dot(ma, woa[...]) + _dot(mb, wob[...])
        h1_o[...] = h1
        nf, _ = _rms(h1)
        f = (nf * gf[...]).astype(BF16)
        f_o[...] = f
        gate = _dot(f, wg[...])
        up = _dot(f, wu[...])
        gate_o[...] = gate.astype(BF16)
        up_o[...] = up.astype(BF16)
        act_o[...] = (gate * (1.0 / (1.0 + jnp.exp(-gate))) * up).astype(BF16)

    outs = [_sds((s, D_MODEL), BF16), _sds((s, D_MODEL), F32), _sds((s, D_MODEL), BF16), _sds((s, D_FF), BF16),
            _sds((s, D_FF), BF16), _sds((s, D_FF), BF16)]
    consts = [w["g_a"], w["g_b"], w["w_oa"], w["w_ob"], w["g_f"], w["w_gate"], w["w_up"]]
    return _rowwise("fwd_b1", body, [x, o_mla, o_sb], consts, outs, [], ROW_TILE)


def _fwd_b2(h1, act, tgt, w):
    s = h1.shape[0]

    def body(r, c, o, a, step):
        h1_ref, act_ref, t_ref = r
        wd, gn = c
        (dh2_o,) = o
        loss_o, dgn_o = a
        h2 = h1_ref[...] + _dot(act_ref[...], wd[...])
        n2, r2 = _rms(h2)
        err = n2 * gn[...] - t_ref[...]
        part = jnp.sum(jnp.sum(err * err, axis=1, keepdims=True), axis=0, keepdims=True) * (0.5 / D_MODEL)
        _accumulate(loss_o, jnp.broadcast_to(part, (1, LANES)), step)
        dh2, dgn = _rms_bwd(n2, r2, gn[...], err * (1.0 / D_MODEL))
        dh2_o[...] = dh2
        _accumulate(dgn_o, dgn, step)

    return _rowwise("fwd_b2", body, [h1, act, tgt], [w["w_down"], w["g_n"]], [_sds((s, D_MODEL), F32)],
                    [_sds((1, LANES), F32), _sds((1, D_MODEL), F32)], ROW_TILE)


def _bwd_b1(dh2, gate, up, w):
    s = dh2.shape[0]

    def body(r, c, o, a, step):
        dh2_ref, gate_ref, up_ref = r
        (wdt,) = c
        dgate_o, dup_o = o
        dact = _dot(dh2_ref[...].astype(BF16), wdt[...])
        gate = gate_ref[...].astype(F32)
        sig = 1.0 / (1.0 + jnp.exp(-gate))
        dup_o[...] = (dact * (gate * sig)).astype(BF16)
        dgate_o[...] = (dact * up_ref[...].astype(F32) * (sig * (1.0 + gate * (1.0 - sig)))).astype(BF16)

    return _rowwise("bwd_b1", body, [dh2, gate, up], [w["w_down_t"]], [_sds((s, D_FF), BF16), _sds((s, D_FF), BF16)],
                    [], ROW_TILE_ELEMENTWISE)


def _bwd_b2(dgate, dup, h1, dh2, o_mla, o_sb, w):
    s = h1.shape[0]

    def body(r, c, o, a, step):
        dgate_ref, dup_ref, h1_ref, dh2_ref, oa_ref, ob_ref = r
        wgt, wut, gf, woat, wobt, ga, gb = c
        dh1_o, doa_o, dob_o = o
        dgf_o, dga_o, dgb_o = a
        df = _dot(dgate_ref[...], wgt[...]) + _dot(dup_ref[...], wut[...])
        nf, rf = _rms(h1_ref[...])
        dres, dgf = _rms_bwd(nf, rf, gf[...], df)
        dh1 = dh2_ref[...] + dres
        dh1_o[...] = dh1
        dh1b = dh1.astype(BF16)
        na, ra = _rms(oa_ref[...])
        doa, dga = _rms_bwd(na, ra, ga[...], _dot(dh1b, woat[...]))
        nb, rb = _rms(ob_ref[...])
        dob, dgb = _rms_bwd(nb, rb, gb[...], _dot(dh1b, wobt[...]))
        doa_o[...] = doa
        dob_o[...] = dob
        _accumulate(dgf_o, dgf, step)
        _accumulate(dga_o, dga, step)
        _accumulate(dgb_o, dgb, step)

    consts = [w["w_gate_t"], w["w_up_t"], w["g_f"], w["w_oa_t"], w["w_ob_t"], w["g_a"], w["g_b"]]
    outs = [_sds((s, D_MODEL), F32), _sds((s, MLA_WIDTH), F32), _sds((s, SB_WIDTH), F32)]
    accs = [_sds((1, D_MODEL), F32), _sds((1, MLA_WIDTH), F32), _sds((1, SB_WIDTH), F32)]
    return _rowwise("bwd_b2", body, [dgate, dup, h1, dh2, o_mla, o_sb], consts, outs, accs, ROW_TILE)


def _fold_pairs(t):
    return jnp.concatenate([t[:, :LANES] + t[:, LANES:2 * LANES], t[:, 2 * LANES:3 * LANES] + t[:, 3 * LANES:]], axis=1)


def _bwd_a(x, dh1, cq, ckv, dqn, dqr, dkn, dvm, dkr, dsq, dsk, dsv, tabs, w):
    s = x.shape[0]

    def body(r, c, o, a, step):
        x_ref, dh1_ref, cq_ref, ckv_ref, dqn_ref, dqr_ref, dkn_ref, dvm_ref, dkr_ref, dsq_ref, dsk_ref, dsv_ref, cos_ref, sin_ref = r
        wqnt, wqrt, wqrrt, gq, wknt, wvt, gkv, wcqt, wckvt, wkrt, wkrrt, wsqt, wskt, wsvt, gmix = c
        dx_o, a1_o, a2_o, dcq_o, dckv_o, dkrc_o, dkrs_o = o
        dgq_o, dgkv_o, dgmix_o = a
        cos, sin = cos_ref[...], sin_ref[...]
        dqr = _fold_pairs(dqr_ref[...])
        a1 = (dqr * cos).astype(BF16)
        a2 = (dqr * sin).astype(BF16)
        a1_o[...] = a1
        a2_o[...] = a2
        nq, rq = _rms(cq_ref[...])
        dcqn = _dot(dqn_ref[...].astype(BF16), wqnt[...]) + _dot(a1, wqrt[...]) + _dot(a2, wqrrt[...])
        dcq, dgq = _rms_bwd(nq, rq, gq[...], dcqn)
        nkv, rkv = _rms(ckv_ref[...])
        dckvn = _dot((dkn_ref[...] * MLA_DK_SCALE).astype(BF16), wknt[...]) + _dot(dvm_ref[...].astype(BF16), wvt[...])
        dckv, dgkv = _rms_bwd(nkv, rkv, gkv[...], dckvn)
        dkr = _fold_pairs(dkr_ref[...]) * MLA_DK_SCALE
        dcq_b = dcq.astype(BF16)
        dckv_b = dckv.astype(BF16)
        dkrc = (dkr * cos).astype(BF16)
        dkrs = (dkr * sin).astype(BF16)
        dcq_o[...] = dcq_b
        dckv_o[...] = dckv_b
        dkrc_o[...] = dkrc
        dkrs_o[...] = dkrs
        du = (_dot(dcq_b, wcqt[...]) + _dot(dckv_b, wckvt[...]) + _dot(dkrc, wkrt[...]) + _dot(dkrs, wkrrt[...])
              + _dot(dsq_ref[...].astype(BF16), wsqt[...]) + _dot(dsk_ref[...].astype(BF16), wskt[...])
              + _dot(dsv_ref[...].astype(BF16), wsvt[...]))
        nx, rx = _rms(x_ref[...])
        dres, dgmix = _rms_bwd(nx, rx, gmix[...], du)
        dx_o[...] = dh1_ref[...] + dres
        _accumulate(dgq_o, dgq, step)
        _accumulate(dgkv_o, dgkv, step)
        _accumulate(dgmix_o, dgmix, step)

    consts = [w["w_qn_t"], w["w_qr_t"], w["w_qrr_t"], w["g_q"], w["w_kn_t"], w["w_v_t"], w["g_kv"], w["w_cq_t"], w["w_ckv_t"],
              w["w_kr8_t"], w["w_kr8r_t"], w["w_sbq_t"], w["w_sbk_t"], w["w_sbv_t"], w["g_mix"]]
    rope_w = MLA_HEADS * MLA_ROPE
    outs = [_sds((s, D_MODEL), F32), _sds((s, rope_w), BF16), _sds((s, rope_w), BF16), _sds((s, Q_RANK), BF16),
            _sds((s, KV_RANK), BF16), _sds((s, rope_w), BF16), _sds((s, rope_w), BF16)]
    accs = [_sds((1, Q_RANK), F32), _sds((1, KV_RANK), F32), _sds((1, D_MODEL), F32)]
    rows = [x, dh1, cq, ckv, dqn, dqr, dkn, dvm, dkr, dsq, dsk, dsv, tabs["cos"], tabs["sin"]]
    return _rowwise("bwd_a", body, rows, consts, outs, accs, ROW_TILE)


def _tn_multi(name, x, ys):
    s, k = x.shape
    ts = min(TN_TS, s)
    n_y = len(ys)

    def kern(*refs):
        step = pl.program_id(0)
        xb = refs[0][...].astype(BF16)
        for j in range(n_y):
            _accumulate(refs[1 + n_y + j], _dot_tn(xb, refs[1 + j][...].astype(BF16)), step)

    return pl.pallas_call(
        kern, name=name, grid=(s // ts,),
        in_specs=[pl.BlockSpec((ts, k), lambda i: (i, 0))] + [pl.BlockSpec((ts, y.shape[1]), lambda i: (i, 0)) for y in ys],
        out_specs=[pl.BlockSpec((k, y.shape[1]), lambda i: (0, 0)) for y in ys],
        out_shape=[_sds((k, y.shape[1]), F32) for y in ys], compiler_params=_params(1),
    )(x, *ys)


def _tn_tile(k, n):
    if n % LANES or k * n * 4 <= TN_ACC_BYTES:
        return n
    units = n // LANES
    best = 1
    for d in range(1, units + 1):
        if units % d == 0 and k * d * LANES * 4 <= TN_ACC_BYTES:
            best = d
    return best * LANES


def _tn_matmul(name, x, y):
    s, k = x.shape
    n = y.shape[1]
    ts = min(TN_TS, s)
    tn = _tn_tile(k, n)

    def kern(x_ref, y_ref, o_ref):
        step = pl.program_id(1)
        _accumulate(o_ref, _dot_tn(x_ref[...].astype(BF16), y_ref[...].astype(BF16)), step)

    return pl.pallas_call(
        kern, name=name, grid=(n // tn, s // ts),
        in_specs=[pl.BlockSpec((ts, k), lambda j, i: (i, 0)), pl.BlockSpec((ts, tn), lambda j, i: (i, j))],
        out_specs=pl.BlockSpec((k, tn), lambda j, i: (0, j)), out_shape=_sds((k, n), F32), compiler_params=_params(2),
    )(x, y)


def _lanes(rows, lo, width):
    lane = lax.broadcasted_iota(jnp.int32, (rows, LANES), 1)
    return jnp.logical_and(lane >= lo, lane < lo + width)


def _keep(mask, t):
    return jnp.where(mask, t, jnp.zeros_like(t))


def _mla_qcat(qn_ref, qr_ref, rope_lo, half, rows):
    qn = _keep(_lanes(rows, MLA_NOPE * half, MLA_NOPE), qn_ref[...])
    qr = _keep(_lanes(rows, rope_lo, MLA_ROPE), qr_ref[...])
    return jnp.concatenate([qn, qr], axis=1)


def _diag_mask(rows, width, row0, col0):
    row = lax.broadcasted_iota(jnp.int32, (rows, width), 0)
    col = lax.broadcasted_iota(jnp.int32, (rows, width), 1)
    return col + (col0 - row0) <= row


def _mla_fwd(qn, qr, kn, kr, v, riders=(), tq=MLA_TQ, tk=MLA_TK, td=MLA_TQ):
    s = qn.shape[0]
    tq, tk, td = min(tq, s), min(tk, s), min(td, s)
    ratio = tq // tk

    n_ride = len(riders)
    n_pairs = MLA_HEADS // 2

    def kern(qn_ref, qr_ref, kn_ref, kr_ref, v_ref, *rest):
        o_ref, lse_ref = rest[n_ride:n_ride + 2]
        g = pl.program_id(0)
        i = pl.program_id(1)
        if n_ride:
            send, forward, finish = _gather_steps(rest[:n_ride], rest[n_ride + 2:2 * n_ride + 2], *rest[2 * n_ride + 2:])
            pl.when(jnp.logical_and(g == 0, i == 0))(send)
            pl.when(jnp.logical_and(g == 1, i == 0))(forward)
        for half in range(2):
            qcat = _mla_qcat(qn_ref, qr_ref, MLA_ROPE * (2 * (g % 2) + half), half, tq)

            def block(k0, width, carry, row0, masked, qcat=qcat):
                m, l, acc = (c[row0:] for c in carry)
                ks = pl.ds(pl.multiple_of(k0, width), width)
                kcat = jnp.concatenate([kn_ref[ks, :], kr_ref[ks, :]], axis=1)
                sc = _dot_nt(qcat[row0:], kcat)
                if masked:
                    sc = jnp.where(_diag_mask(tq - row0, width, row0, row0), sc, NEG)
                m_new = jnp.maximum(m, jnp.max(sc, axis=1, keepdims=True))
                p = jnp.exp2(sc - m_new)
                alpha = jnp.exp2(m - m_new)
                l = alpha * l + jnp.sum(p, axis=1, keepdims=True)
                acc = alpha * acc + _dot(p.astype(BF16), v_ref[ks, :])
                new = (m_new, l, acc)
                return new if row0 == 0 else tuple(jnp.concatenate([c[:row0], n], axis=0) for c, n in zip(carry, new))

            carry = (jnp.full((tq, 1), NEG, F32), jnp.zeros((tq, 1), F32), jnp.zeros((tq, LANES), F32))
            carry = lax.fori_loop(0, i * ratio, lambda kb, c, block=block: block(kb * tk, tk, c, 0, False), carry)
            for row0 in range(0, tq, td):
                carry = block(i * tq + row0, td, carry, row0, True)
            m, l, acc = carry
            out = _keep(_lanes(tq, MLA_V * half, MLA_V), acc / l)
            lse = _keep(_lanes(tq, MLA_ROPE * half, MLA_ROPE), jnp.broadcast_to(m + jnp.log2(l), (tq, LANES)))
            if half == 0:
                o_ref[...] = out
                lse_ref[...] = lse
            else:
                o_ref[...] += out
                lse_ref[...] += lse
        if n_ride:
            pl.when(jnp.logical_and(g == n_pairs - 1, i == s // tq - 1))(finish)

    qblk = pl.BlockSpec((tq, LANES), lambda g, i: (i, g))
    full = pl.BlockSpec((s, LANES), lambda g, i: (0, g))
    outs = pl.pallas_call(
        kern, name="mla_fwd", grid=(n_pairs, s // tq),
        in_specs=[qblk, pl.BlockSpec((tq, LANES), lambda g, i: (i, g // 2)), full, pl.BlockSpec((s, LANES), lambda g, i: (0, 0)), full]
        + [HBM_SPEC] * n_ride,
        out_specs=[qblk, qblk] + [HBM_SPEC] * n_ride,
        out_shape=[_sds((s, MLA_WIDTH), F32), _sds((s, n_pairs * LANES), F32)] + [_sds((N_SHARD,) + a.shape, a.dtype) for a in riders],
        scratch_shapes=_gather_sems(n_ride) if n_ride else [], compiler_params=_params(2),
    )(qn, qr, kn, kr, v, *riders)
    return outs[0], outs[1], outs[2:]


def _mla_bwd(qn, qr, kn, kr, v, o, do, lse, riders=(), tq=MLA_TQ, tk=MLA_BWD_TK, td=MLA_DIAG_TK):
    s = qn.shape[0]
    tq, tk, td = min(tq, s), min(tk, s), min(td, s)
    ratio = tq // tk

    n_ride = len(riders)
    n_pairs = MLA_HEADS // 2

    def kern(qn_ref, qr_ref, kn_ref, kr_ref, v_ref, o_ref, do_ref, lse_ref, *rest):
        dqn_ref, dqr_ref, dkn_ref, dkr_ref, dv_ref = rest[n_ride:n_ride + 5]
        g = pl.program_id(0)
        i = pl.program_id(1)
        if n_ride:
            start, finish = _scatter_steps(rest[:n_ride], rest[n_ride + 5:2 * n_ride + 5], *rest[2 * n_ride + 5:])
            pl.when(jnp.logical_and(g == 0, i == 0))(start)

        @pl.when(i == 0)
        def _():
            dkn_ref[...] = jnp.zeros_like(dkn_ref)
            dkr_ref[...] = jnp.zeros_like(dkr_ref)
            dv_ref[...] = jnp.zeros_like(dv_ref)

        for half in range(2):
            rope_lo = MLA_ROPE * (2 * (g % 2) + half)
            qcat = _mla_qcat(qn_ref, qr_ref, rope_lo, half, tq)
            mine = _lanes(tq, MLA_V * half, MLA_V)
            do_f = _keep(mine, do_ref[...])
            do_b = do_f.astype(BF16)
            delta = jnp.sum(do_f * o_ref[...], axis=1, keepdims=True)
            lse_v = lse_ref[:, MLA_ROPE * half:MLA_ROPE * half + 1]

            def block(k0, width, dq_acc, row0, masked, qcat=qcat, do_b=do_b, delta=delta, lse_v=lse_v):
                ks = pl.ds(pl.multiple_of(k0, width), width)
                kcat = jnp.concatenate([kn_ref[ks, :], kr_ref[ks, :]], axis=1)
                qc, dob = qcat[row0:], do_b[row0:]
                p = jnp.exp2(_dot_nt(qc, kcat) - lse_v[row0:])
                if masked:
                    p = jnp.where(_diag_mask(tq - row0, width, row0, row0), p, 0.0)
                ds = (p * (_dot_nt(dob, v_ref[ks, :]) - delta[row0:])).astype(BF16)
                dv_ref[ks, :] += _dot_tn(p.astype(BF16), dob)
                dkc = _dot_tn(ds, qc)
                dkn_ref[ks, :] += dkc[:, :LANES]
                dkr_ref[ks, :] += dkc[:, LANES:]
                new = dq_acc[row0:] + _dot(ds, kcat)
                return new if row0 == 0 else jnp.concatenate([dq_acc[:row0], new], axis=0)

            acc = lax.fori_loop(0, i * ratio, lambda kb, c, block=block: block(kb * tk, tk, c, 0, False),
                                jnp.zeros((tq, 2 * LANES), F32))
            for row0 in range(0, tq, td):
                acc = block(i * tq + row0, td, acc, row0, True)
            dqn = _keep(_lanes(tq, MLA_NOPE * half, MLA_NOPE), acc[:, :LANES] * MLA_SCALE)
            dqr = _keep(_lanes(tq, rope_lo, MLA_ROPE), acc[:, LANES:] * MLA_SCALE)
            if half == 0:
                dqn_ref[...] = dqn
                dqr_ref[...] = dqr
            else:
                dqn_ref[...] += dqn
                dqr_ref[...] += dqr
        if n_ride:
            pl.when(jnp.logical_and(g == n_pairs - 1, i == s // tq - 1))(finish)

    qblk = pl.BlockSpec((tq, LANES), lambda g, i: (i, g))
    full = pl.BlockSpec((s, LANES), lambda g, i: (0, g))
    once = lambda spec_map: pl.BlockSpec((s, LANES), spec_map, pipeline_mode=pl.Buffered(1))
    wide = _sds((s, n_pairs * LANES), F32)
    outs = pl.pallas_call(
        kern, name="mla_bwd", grid=(n_pairs, s // tq),
        in_specs=[qblk, pl.BlockSpec((tq, LANES), lambda g, i: (i, g // 2)), once(lambda g, i: (0, g)), once(lambda g, i: (0, 0)),
                  once(lambda g, i: (0, g)), qblk, qblk, qblk] + [HBM_SPEC] * n_ride,
        out_specs=[qblk, qblk, full, full, full] + [HBM_SPEC] * n_ride,
        out_shape=[wide] * 5 + [_sds((N_SHARD - 1,) + p.shape[1:], p.dtype) for p in riders],
        scratch_shapes=_scatter_sems(n_ride) if n_ride else [], compiler_params=_params(2),
    )(qn, qr, kn, kr, v, o, do, lse, *riders)
    return outs[:5], outs[5:]


def _sb_masks(tk):
    j = lax.broadcasted_iota(jnp.int32, (tk, tk), 0)
    c = lax.broadcasted_iota(jnp.int32, (tk, tk), 1)
    return (j > c).astype(BF16), (j < c).astype(BF16)


def _sb_scores(qs, kk, msuf, strict):
    z = _dot_nt(qs, kk)
    lom = -(jnp.maximum(z, 0.0) + jnp.log(1.0 + jnp.exp(-jnp.abs(z))))
    if strict is not None:
        lom = jnp.where(strict, lom, 0.0)
    return z, lom, _dot(lom.astype(BF16), msuf)


def _sb_strict(tq, tk, d):
    row = lax.broadcasted_iota(jnp.int32, (tq, tk), 0)
    col = lax.broadcasted_iota(jnp.int32, (tq, tk), 1)
    return col + d * tk < row


def _sb_fwd(q, k, v, msuf, tq=SB_TQ, tk=SB_TK):
    s = q.shape[0]
    tq, tk = min(tq, s), min(tk, s)
    ratio = tq // tk

    def kern(q_ref, k_ref, v_ref, m_ref, o_ref, c_ref):
        i = pl.program_id(1)
        msf = m_ref[...]
        lane = lax.broadcasted_iota(jnp.int32, (tq, LANES), 1)
        for half in range(2):
            mine = _lanes(tq, SB_DIM * half, SB_DIM)
            qs = _keep(mine, q_ref[...]) * 0.125

            def block(kb, carry, dd, qs=qs):
                c, acc, cm = carry
                ks = pl.ds(pl.multiple_of(kb * tk, tk), tk)
                strict = None if dd is None else _sb_strict(tq, tk, dd)
                z, lom, suf = _sb_scores(qs, k_ref[ks, :], msf, strict)
                a = jnp.exp(z + lom + (suf + c))
                if strict is not None:
                    a = jnp.where(strict, a, 0.0)
                acc = acc + _dot(a.astype(BF16), v_ref[ks, :])
                cm = jnp.where(lane == kb, c, cm)
                return c + jnp.sum(lom, axis=1, keepdims=True), acc, cm

            carry = (jnp.zeros((tq, 1), F32), jnp.zeros((tq, LANES), F32), jnp.full((tq, LANES), NEG, F32))
            for dd in range(ratio - 1, -1, -1):
                carry = block(i * ratio + dd, carry, dd)

            def live(st):
                return jnp.logical_and(st[0] >= 0, jnp.max(st[1]) > -SB_SKIP)

            def step(st, block=block):
                return (st[0] - 1, *block(st[0], st[1:], None))

            _, _, acc, cm = lax.while_loop(live, step, (i * ratio - 1, *carry))
            if half == 0:
                o_ref[...] = _keep(mine, acc)
            else:
                o_ref[...] += _keep(mine, acc)
            c_ref[:, LANES * half:LANES * (half + 1)] = cm

    qblk = lambda n: pl.BlockSpec((tq, n), lambda g, i: (i, g))
    full = pl.BlockSpec((s, LANES), lambda g, i: (0, g))
    return pl.pallas_call(
        kern, name="sb_fwd", grid=(SB_HEADS // 2, s // tq),
        in_specs=[qblk(LANES), full, full, pl.BlockSpec((tk, tk), lambda g, i: (0, 0))],
        out_specs=[qblk(LANES), qblk(2 * LANES)],
        out_shape=[_sds((s, SB_WIDTH), F32), _sds((s, SB_HEADS * LANES), F32)], compiler_params=_params(2),
    )(q, k, v, msuf)


def _sb_bwd(q, k, v, do, cmat, msuf, mpre, riders=(), tq=SB_TQ, tk=SB_TK):
    s = q.shape[0]
    tq, tk = min(tq, s), min(tk, s)
    ratio = tq // tk

    n_ride = len(riders)
    n_pairs = SB_HEADS // 2

    def kern(q_ref, k_ref, v_ref, do_ref, c_ref, ms_ref, mp_ref, *rest):
        dq_ref, dk_ref, dv_ref = rest[n_ride:n_ride + 3]
        i = pl.program_id(1)
        if n_ride:
            start, finish = _swap_steps(rest[:n_ride], rest[n_ride + 3:2 * n_ride + 3], *rest[2 * n_ride + 3:])
            pl.when(jnp.logical_and(pl.program_id(0) == 0, i == 0))(start)

        @pl.when(i == 0)
        def _():
            dk_ref[...] = jnp.zeros_like(dk_ref)
            dv_ref[...] = jnp.zeros_like(dv_ref)

        msf = ms_ref[...]
        mpf = mp_ref[...]
        lane = lax.broadcasted_iota(jnp.int32, (tq, LANES), 1)
        lane1 = lax.broadcasted_iota(jnp.int32, (1, LANES), 1)
        for half in range(2):
            mine = _lanes(tq, SB_DIM * half, SB_DIM)
            qv = _keep(mine, q_ref[...])
            qs = qv * 0.125
            do_b = _keep(mine, do_ref[...]).astype(BF16)
            cm = c_ref[:, LANES * half:LANES * (half + 1)]

            def block(kb, carry, dd, qv=qv, qs=qs, do_b=do_b, cm=cm):
                dq_acc, pc = carry
                ks = pl.ds(pl.multiple_of(kb * tk, tk), tk)
                kk = k_ref[ks, :]
                strict = None if dd is None else _sb_strict(tq, tk, dd)
                z, lom, suf = _sb_scores(qs, kk, msf, strict)
                c = jnp.sum(jnp.where(lane == kb, cm, 0.0), axis=1, keepdims=True)
                a = jnp.exp(z + lom + (suf + c))
                if strict is not None:
                    a = jnp.where(strict, a, 0.0)
                g = _dot_nt(do_b, v_ref[ks, :]) * a
                p = pc + _dot(g.astype(BF16), mpf)
                omb = jnp.exp(lom)
                dz = (g * omb - (1.0 - omb) * p) * 0.125
                if strict is not None:
                    dz = jnp.where(strict, dz, 0.0)
                dz = dz.astype(BF16)
                dv_ref[ks, :] += _dot_tn(a.astype(BF16), do_b)
                dk_ref[ks, :] += _dot_tn(dz, qv)
                return dq_acc + _dot(dz, kk), pc + jnp.sum(g, axis=1, keepdims=True)

            seen = jnp.logical_and(jnp.max(cm, axis=0, keepdims=True) > -SB_SKIP, lane1 < i * ratio)
            first = i * ratio - jnp.sum(seen.astype(jnp.int32))
            carry = (jnp.zeros((tq, LANES), F32), jnp.zeros((tq, 1), F32))
            carry = lax.fori_loop(first, i * ratio, lambda kb, c, block=block: block(kb, c, None), carry)
            for dd in range(ratio):
                carry = block(i * ratio + dd, carry, dd)
            if half == 0:
                dq_ref[...] = _keep(mine, carry[0])
            else:
                dq_ref[...] += _keep(mine, carry[0])
        if n_ride:
            pl.when(jnp.logical_and(pl.program_id(0) == n_pairs - 1, i == s // tq - 1))(finish)

    qblk = lambda n: pl.BlockSpec((tq, n), lambda g, i: (i, g))
    full = pl.BlockSpec((s, LANES), lambda g, i: (0, g))
    msk = pl.BlockSpec((tk, tk), lambda g, i: (0, 0))
    outs = pl.pallas_call(
        kern, name="sb_bwd", grid=(n_pairs, s // tq),
        in_specs=[qblk(LANES), full, full, qblk(LANES), qblk(2 * LANES), msk, msk] + [HBM_SPEC] * n_ride,
        out_specs=[qblk(LANES), full, full] + [HBM_SPEC] * n_ride,
        out_shape=[_sds((s, SB_WIDTH), F32)] * 3 + _halves_shapes(riders),
        scratch_shapes=_swap_sems(n_ride) if n_ride else [], compiler_params=_params(2),
    )(q, k, v, do, cmat, msuf, mpre, *riders)
    return outs[:3], outs[3:]


def _place():
    return lax.axis_index("x"), lax.axis_index("y"), lax.axis_index("c")


def _other_chips(x, y):
    return [(1 - x, y), (x, 1 - y), (1 - x, 1 - y)]


HBM_SPEC = pl.BlockSpec(memory_space=pl.ANY)


def _gather_steps(ins, outs, send_sems, recv_sems):
    n = len(ins)
    x, y, c = _place()
    sibling = (x, y, 1 - c)
    chips = _other_chips(x, y)

    def half_of(a, ref, pc):
        half = ins[a].shape[0] // 2
        return ref.at[pl.ds(pl.multiple_of(pc * half, 16), half), :]

    def copy(a, k, chip, pc, to, src=None):
        dst = half_of(a, outs[a].at[2 * chip[0] + chip[1]], pc)
        return pltpu.make_async_remote_copy(src_ref=dst if src is None else src, dst_ref=dst, send_sem=send_sems.at[6 * a + k],
                                            recv_sem=recv_sems.at[6 * a + k], device_id=to, device_id_type=MESH)

    def first():
        return [copy(a, j, (x, y), c, (*chip, c), src=half_of(a, ins[a], c)) for a in range(n) for j, chip in enumerate(chips)]

    def passed():
        return [copy(a, 3 + j, chip, c, sibling) for j, chip in enumerate(chips) for a in range(n)]

    def send():
        for cp in first():
            cp.start()

    def forward():
        for j, chip in enumerate(chips):
            for a in range(n):
                copy(a, j, chip, c, sibling).wait_recv()
        for cp in passed():
            cp.start()

    def finish():
        for j, chip in enumerate(chips):
            for a in range(n):
                copy(a, 3 + j, chip, 1 - c, sibling).wait_recv()
        for cp in first() + passed():
            cp.wait_send()

    return send, forward, finish


def _gather_sems(n):
    return [pltpu.SemaphoreType.DMA((6 * n,)), pltpu.SemaphoreType.DMA((6 * n,))]


def _allgather_list(name, shards):
    n = len(shards)

    def body(*refs):
        for stage in _gather_steps(refs[:n], refs[n:2 * n], *refs[2 * n:]):
            stage()

    return pl.pallas_call(
        body, name=name, out_shape=[_sds((N_SHARD,) + a.shape, a.dtype) for a in shards], in_specs=[HBM_SPEC] * n,
        out_specs=[HBM_SPEC] * n, scratch_shapes=_gather_sems(n),
    )(*shards)


def _swap_steps(ins, outs, send_sems, recv_sems):
    x, y, c = _place()

    def copies():
        out = []
        for a in range(len(ins)):
            h = ins[a].shape[1] // 2
            src = ins[a].at[:, pl.ds(pl.multiple_of((1 - c) * h, 8), h), :]
            out.append(pltpu.make_async_remote_copy(src_ref=src, dst_ref=outs[a], send_sem=send_sems.at[a], recv_sem=recv_sems.at[a],
                                                    device_id=(x, y, 1 - c), device_id_type=MESH))
        return out

    def start():
        for cp in copies():
            cp.start()

    def finish():
        for cp in copies():
            cp.wait()

    return start, finish


def _swap_sems(n):
    return [pltpu.SemaphoreType.DMA((n,)), pltpu.SemaphoreType.DMA((n,))]


def _halves_shapes(gs):
    return [_sds((N_SHARD, g.shape[1] // 2, g.shape[2]), g.dtype) for g in gs]


def _swap_halves(name, gs):
    n = len(gs)

    def body(*refs):
        for stage in _swap_steps(refs[:n], refs[n:2 * n], *refs[2 * n:]):
            stage()

    return pl.pallas_call(body, name=name, out_shape=_halves_shapes(gs), in_specs=[HBM_SPEC] * n, out_specs=[HBM_SPEC] * n,
                          scratch_shapes=_swap_sems(n))(*gs)


def _add_sibling(name, gs, gots, c_idx):
    n = len(gs)

    def kern(c_ref, *refs):
        for a in range(n):
            tot = refs[a][...] + refs[n + a][...]
            refs[2 * n + a][...] = tot
            refs[3 * n + a][...] = tot.astype(BF16)

    quarter = lambda g: (None, g.shape[1] // 4, g.shape[2])
    in_specs = [pl.BlockSpec(quarter(g), lambda b, s, c_ref: (b, 2 * c_ref[0] + s, 0)) for g in gs]
    in_specs += [pl.BlockSpec(quarter(g), lambda b, s, c_ref: (b, s, 0)) for g in gs]
    out_specs = [pl.BlockSpec(quarter(g), lambda b, s, c_ref: (b, s, 0)) for g in gs] * 2
    out_shape = [_sds(t.shape, F32) for t in gots] + [_sds(t.shape, BF16) for t in gots]
    outs = pl.pallas_call(
        kern, name=name, out_shape=out_shape,
        grid_spec=pltpu.PrefetchScalarGridSpec(num_scalar_prefetch=1, grid=(N_SHARD, 2), in_specs=in_specs, out_specs=out_specs),
        compiler_params=_params(2),
    )(c_idx.reshape(1), *gs, *gots)
    return outs[:n], outs[n:]


def _scatter_steps(ins, outs, send_sems, recv_sems):
    x, y, c = _place()

    def copies():
        return [pltpu.make_async_remote_copy(
            src_ref=ins[a].at[2 * px + py], dst_ref=outs[a].at[j], send_sem=send_sems.at[3 * a + j], recv_sem=recv_sems.at[3 * a + j],
            device_id=(px, py, c), device_id_type=MESH) for a in range(len(ins)) for j, (px, py) in enumerate(_other_chips(x, y))]

    def start():
        for cp in copies():
            cp.start()

    def finish():
        for cp in copies():
            cp.wait()

    return start, finish


def _scatter_sems(n):
    return [pltpu.SemaphoreType.DMA((3 * n,)), pltpu.SemaphoreType.DMA((3 * n,))]


def _chip_scatter(ps):
    n = len(ps)

    def body(*refs):
        for stage in _scatter_steps(refs[:n], refs[n:2 * n], *refs[2 * n:]):
            stage()

    return pl.pallas_call(
        body, name="chip_scatter", out_shape=[_sds((N_SHARD - 1,) + p.shape[1:], p.dtype) for p in ps], in_specs=[HBM_SPEC] * n,
        out_specs=[HBM_SPEC] * n, scratch_shapes=_scatter_sems(n),
    )(*ps)


def _add_chips(name, ps, others, shard_idx):
    n = len(ps)

    def kern(b_ref, *refs):
        for a in range(n):
            tot = refs[a][...]
            for j in range(N_SHARD - 1):
                tot = tot + refs[n + a][j].astype(F32)
            refs[2 * n + a][...] = tot

    in_specs = [pl.BlockSpec((None, p.shape[1] // 2, p.shape[2]), lambda s, b_ref: (b_ref[0], s, 0)) for p in ps]
    in_specs += [pl.BlockSpec((N_SHARD - 1, p.shape[1] // 2, p.shape[2]), lambda s, b_ref: (0, s, 0)) for p in ps]
    out_specs = [pl.BlockSpec((p.shape[1] // 2, p.shape[2]), lambda s, b_ref: (s, 0)) for p in ps]
    return pl.pallas_call(
        kern, name=name, out_shape=[_sds(p.shape[1:], F32) for p in ps],
        grid_spec=pltpu.PrefetchScalarGridSpec(num_scalar_prefetch=1, grid=(2,), in_specs=in_specs, out_specs=out_specs),
        compiler_params=_params(1),
    )(shard_idx.reshape(1), *ps, *others)


def _swap_result(name, mines):
    n = len(mines)

    def body(*refs):
        ins, outs = refs[:n], refs[n:2 * n]
        send_sems, recv_sems = refs[2 * n:]
        x, y, c = _place()
        copies = [pltpu.make_async_remote_copy(src_ref=ins[a], dst_ref=outs[a], send_sem=send_sems.at[a], recv_sem=recv_sems.at[a],
                                               device_id=(x, y, 1 - c), device_id_type=MESH) for a in range(n)]
        for cp in copies:
            cp.start()
        for cp in copies:
            cp.wait()

    return pl.pallas_call(
        body, name=name, out_shape=[_sds(m.shape, m.dtype) for m in mines], in_specs=[HBM_SPEC] * n,
        out_specs=[HBM_SPEC] * n, scratch_shapes=[pltpu.SemaphoreType.DMA((n,)), pltpu.SemaphoreType.DMA((n,))],
    )(*mines)


def _allreduce_small(v):
    m_per, n = v.shape

    def body(x_ref, tot_ref, all_ref, send_sems, recv_sems, local_sem):
        x, y, c = _place()
        me, sibling = (x, y, c), (x, y, 1 - c)
        chips = _other_chips(x, y)

        def rows(px, py, pc):
            return all_ref.at[pl.ds(pl.multiple_of((4 * px + 2 * py + pc) * m_per, 8), m_per), :]

        def copy(k, block, to, src=None):
            return pltpu.make_async_remote_copy(
                src_ref=rows(*block) if src is None else src, dst_ref=rows(*block), send_sem=send_sems.at[k],
                recv_sem=recv_sems.at[k], device_id=to, device_id_type=MESH)

        mine = pltpu.make_async_copy(x_ref, rows(*me), local_sem)
        mine.start()
        first = [copy(0, me, sibling, src=x_ref)] + [copy(1 + j, me, (*chip, c), src=x_ref) for j, chip in enumerate(chips)]
        for cp in first:
            cp.start()
        passed = [copy(4 + j, (*chip, c), sibling) for j, chip in enumerate(chips)]
        for j, chip in enumerate(chips):
            copy(1 + j, (*chip, c), me).wait_recv()
            passed[j].start()
        copy(0, sibling, me).wait_recv()
        for j, chip in enumerate(chips):
            copy(4 + j, (*chip, 1 - c), me).wait_recv()
        for cp in first + passed:
            cp.wait_send()
        mine.wait()
        tot = all_ref[0:m_per, :]
        for dev in range(1, 8):
            tot = tot + all_ref[dev * m_per:(dev + 1) * m_per, :]
        tot_ref[...] = tot

    vmem = pl.BlockSpec(memory_space=pltpu.VMEM)
    return pl.pallas_call(
        body, name="allreduce_small", out_shape=_sds((m_per, n), F32), in_specs=[vmem], out_specs=vmem,
        scratch_shapes=[pltpu.VMEM((8 * m_per, n), F32), pltpu.SemaphoreType.DMA((7,)), pltpu.SemaphoreType.DMA((7,)),
                        pltpu.SemaphoreType.DMA],
    )(v)


def _adam_update(w, g, m, v):
    m_new = ADAM_B1 * m + (1.0 - ADAM_B1) * g
    v_new = ADAM_B2 * v + (1.0 - ADAM_B2) * (g * g)
    m_hat = m_new / (1.0 - ADAM_B1 ** ADAM_STEP)
    v_hat = v_new / (1.0 - ADAM_B2 ** ADAM_STEP)
    return -ADAM_LR * (m_hat / (jnp.sqrt(v_hat) + ADAM_EPS) + ADAM_WD * w), m_new, v_new


def _adamw(name, w, g, m, v):
    rows, width = w.shape
    tr = rows // 4 if rows % 32 == 0 else rows

    def kern(w_ref, g_ref, m_ref, v_ref, d_ref, mo_ref, vo_ref):
        d_ref[...], mo_ref[...], vo_ref[...] = _adam_update(w_ref[...], g_ref[...], m_ref[...], v_ref[...])

    spec = pl.BlockSpec((tr, width), lambda i: (i, 0))
    return pl.pallas_call(kern, name=name, grid=(rows // tr,), in_specs=[spec] * 4, out_specs=[spec] * 3,
                          out_shape=[_sds((rows, width), F32)] * 3, compiler_params=_params(1))(w, g, m, v)


def _adamw_halves(name, w, mine, theirs, m, v, c_idx):
    rows, width = w.shape
    tr = rows // 4

    def kern(c_ref, w_ref, mine_ref, theirs_ref, m_ref, v_ref, g_ref, d_ref, mo_ref, vo_ref):
        g = jnp.where(pl.program_id(0) == c_ref[0], mine_ref[...], theirs_ref[...])
        g_ref[...] = g
        d_ref[...], mo_ref[...], vo_ref[...] = _adam_update(w_ref[...], g, m_ref[...], v_ref[...])

    whole = pl.BlockSpec((tr, width), lambda h, j, c_ref: (2 * h + j, 0))
    part = pl.BlockSpec((tr, width), lambda h, j, c_ref: (j, 0))
    return pl.pallas_call(
        kern, name=name, out_shape=[_sds((rows, width), F32)] * 4,
        grid_spec=pltpu.PrefetchScalarGridSpec(num_scalar_prefetch=1, grid=(2, 2), in_specs=[whole, part, part, whole, whole],
                                               out_specs=[whole] * 4),
        compiler_params=_params(2),
    )(c_idx.reshape(1), w, mine, theirs, m, v)


SHARDED = (("w_in", D_MODEL, IN_WIDTH, 1), ("w_uq", Q_RANK, MLA_HEADS * MLA_QK, 1),
           ("w_ukv", KV_RANK, MLA_HEADS * (MLA_NOPE + MLA_V), 1), ("w_o", D_MODEL, D_MODEL, 0),
           ("w_gate", D_MODEL, D_FF, 1), ("w_up", D_MODEL, D_FF, 1), ("w_down", D_FF, D_MODEL, 0))
EARLY = ("w_in", "w_uq", "w_ukv")
LATE = ("w_o", "w_gate", "w_up", "w_down")
FLIPPED = ("w_gate", "w_up")
SMALL = (("norm_mix", D_MODEL), ("q_latent_norm", Q_RANK), ("kv_latent_norm", KV_RANK), ("out_norm_mla", MLA_WIDTH),
         ("out_norm_sb", SB_WIDTH), ("norm_ffn", D_MODEL), ("norm_final", D_MODEL))


def _full_weight(gathered, axis):
    n_sh, k, n = gathered.shape
    return gathered.transpose(1, 0, 2).reshape(k, n_sh * n) if axis == 1 else gathered.reshape(n_sh * k, n)


def _shard_major(g, axis):
    r, c = g.shape
    return g.reshape(r, N_SHARD, c // N_SHARD).transpose(1, 0, 2) if axis == 1 else g.reshape(N_SHARD, r // N_SHARD, c)


def _rot_cols(w):
    hh = MLA_ROPE // 2
    return jnp.concatenate([-w[..., hh:], w[..., :hh]], axis=-1)


def _rot_cols_t(g):
    hh = MLA_ROPE // 2
    return jnp.concatenate([g[..., hh:], -g[..., :hh]], axis=-1)


def _with_transposes(w):
    w.update({name + "_t": t.T for name, t in list(w.items())})
    return w


def _attention_weights(full, small):
    w_in = full["w_in"]
    s0, s1, s2 = Q_RANK, Q_RANK + KV_RANK, Q_RANK + KV_RANK + MLA_ROPE
    uq = full["w_uq"].reshape(Q_RANK, MLA_HEADS, MLA_QK)
    ukv = full["w_ukv"].reshape(KV_RANK, MLA_HEADS, MLA_NOPE + MLA_V)
    w_kr = w_in[:, s1:s2]
    per_tile = ROPE_TILE // MLA_ROPE
    w = _with_transposes({
        "w_cq": w_in[:, :s0], "w_ckv": w_in[:, s0:s1],
        "w_kr4": jnp.tile(w_kr, (1, per_tile)), "w_kr4r": jnp.tile(_rot_cols(w_kr), (1, per_tile)),
        "w_kr8": jnp.tile(w_kr, (1, MLA_HEADS)), "w_kr8r": jnp.tile(_rot_cols(w_kr), (1, MLA_HEADS)),
        "w_sbq": w_in[:, s2:s2 + SB_WIDTH], "w_sbk": w_in[:, s2 + SB_WIDTH:s2 + 2 * SB_WIDTH], "w_sbv": w_in[:, s2 + 2 * SB_WIDTH:],
        "w_qn": uq[..., :MLA_NOPE].reshape(Q_RANK, -1), "w_qr": uq[..., MLA_NOPE:].reshape(Q_RANK, -1),
        "w_qrr": _rot_cols(uq[..., MLA_NOPE:]).reshape(Q_RANK, -1),
        "w_kn": ukv[..., :MLA_NOPE].reshape(KV_RANK, -1), "w_v": ukv[..., MLA_NOPE:].reshape(KV_RANK, -1),
    })
    w.update(g_mix=small["norm_mix"], g_q=small["q_latent_norm"], g_kv=small["kv_latent_norm"], g_a=small["out_norm_mla"],
             g_b=small["out_norm_sb"], g_f=small["norm_ffn"], g_n=small["norm_final"])
    return w


def _ffn_weights(full):
    w = _with_transposes({"w_oa": full["w_o"][:MLA_WIDTH], "w_ob": full["w_o"][MLA_WIDTH:], "w_down": full["w_down"]})
    for name in FLIPPED:
        w[name + "_t"] = full[name]
        w[name] = full[name].T
    return w


def _rope_tables(positions):
    inv_freq = ROPE_THETA ** (-jnp.arange(0, MLA_ROPE, 2, dtype=F32) / MLA_ROPE)
    ang = positions.astype(F32)[:, None] * inv_freq[None, :]
    cos, sin = jnp.cos(ang), jnp.sin(ang)
    return {"cos": jnp.tile(jnp.concatenate([cos, cos], axis=1), (1, MLA_HEADS)),
            "sin": jnp.tile(jnp.concatenate([sin, sin], axis=1), (1, MLA_HEADS))}


def _by_head(g_wide, g_narrow, wide, narrow):
    r = g_wide.shape[0]
    return jnp.concatenate([g_wide.reshape(r, MLA_HEADS, wide), g_narrow.reshape(r, MLA_HEADS, narrow)], axis=-1).reshape(r, -1)


def kernel(x, positions, norm_mix, w_in, q_latent_norm, w_uq, kv_latent_norm, w_ukv, out_norm_mla, out_norm_sb, w_o, norm_ffn, w_gate, w_up, w_down, norm_final, loss_target, m_norm_mix, m_w_in, m_q_latent_norm, m_w_uq, m_kv_latent_norm, m_w_ukv, m_out_norm_mla, m_out_norm_sb, m_w_o, m_norm_ffn, m_w_gate, m_w_up, m_w_down, m_norm_final, v_norm_mix, v_w_in, v_q_latent_norm, v_w_uq, v_kv_latent_norm, v_w_ukv, v_out_norm_mla, v_out_norm_sb, v_w_o, v_norm_ffn, v_w_gate, v_w_up, v_w_down, v_norm_final):
    given = dict(norm_mix=norm_mix, w_in=w_in, q_latent_norm=q_latent_norm, w_uq=w_uq, kv_latent_norm=kv_latent_norm, w_ukv=w_ukv,
                 out_norm_mla=out_norm_mla, out_norm_sb=out_norm_sb, w_o=w_o, norm_ffn=norm_ffn, w_gate=w_gate, w_up=w_up,
                 w_down=w_down, norm_final=norm_final)
    mom_m = dict(norm_mix=m_norm_mix, w_in=m_w_in, q_latent_norm=m_q_latent_norm, w_uq=m_w_uq, kv_latent_norm=m_kv_latent_norm,
                 w_ukv=m_w_ukv, out_norm_mla=m_out_norm_mla, out_norm_sb=m_out_norm_sb, w_o=m_w_o, norm_ffn=m_norm_ffn,
                 w_gate=m_w_gate, w_up=m_w_up, w_down=m_w_down, norm_final=m_norm_final)
    mom_v = dict(norm_mix=v_norm_mix, w_in=v_w_in, q_latent_norm=v_q_latent_norm, w_uq=v_w_uq, kv_latent_norm=v_kv_latent_norm,
                 w_ukv=v_w_ukv, out_norm_mla=v_out_norm_mla, out_norm_sb=v_out_norm_sb, w_o=v_w_o, norm_ffn=v_norm_ffn,
                 w_gate=v_w_gate, w_up=v_w_up, w_down=v_w_down, norm_final=v_norm_final)
    xs = x[0]
    tgt = loss_target[0]
    s = xs.shape[0]
    c_idx = lax.axis_index("c")
    shard_idx = 2 * lax.axis_index("x") + lax.axis_index("y")

    def block2d(t, name):
        t = t.reshape(t.shape[-2:])
        return t.T if name in FLIPPED else t

    shard2d = {name: block2d(given[name], name) for name, *_ in SHARDED}
    is_mine = (jnp.arange(N_SHARD) == shard_idx)[:, None, None]
    local = {name: shard2d[name].astype(BF16) for name, *_ in SHARDED}
    axis_of = {name: 0 if name in FLIPPED else axis for name, _, _, axis in SHARDED}

    def whole(names, gathered):
        return {name: _full_weight(jnp.where(is_mine, local[name][None], t), axis_of[name]) for name, t in zip(names, gathered)}

    small = {name: given[name].reshape(1, n) for name, n in SMALL}
    w = _attention_weights(whole(EARLY, _allgather_list("allgather_w", [local[name] for name in EARLY])), small)
    tabs = _rope_tables(positions[0])
    msuf, mpre = _sb_masks(min(SB_TK, s))

    u, cq, ckv, cqn, ckvn, qn, qr, kn, vm, kr, sq, sk, sv = _fwd_a(xs, tabs, w)
    o_mla, lse, late = _mla_fwd(qn, qr, kn, kr, vm, [local[name] for name in LATE])
    w.update(_ffn_weights(whole(LATE, late)))
    o_sb, cmat = _sb_fwd(sq, sk, sv, msuf)
    merged, h1, f, gate, up, act = _fwd_b1(xs, o_mla, o_sb, w)
    dh2, loss_part, dg_n = _fwd_b2(h1, act, tgt, w)

    def shards_of(names, grads):
        return [_shard_major(grads[name], axis_of[name]) for name in names]

    def reduced(tag, chip_f32, others):
        mine = _add_chips("add_chips_" + tag, chip_f32, others, shard_idx)
        return tuple(mine), tuple(_swap_result("swap_result_" + tag, mine))

    dgate, dup = _bwd_b1(dh2, gate, up, w)
    dh1, do_mla, do_sb, dg_f, dg_a, dg_b = _bwd_b2(dgate, dup, h1, dh2, o_mla, o_sb, w)
    late_gs = shards_of(LATE, {
        "w_o": _tn_matmul("dw_o", merged, dh1), "w_gate": _tn_matmul("dw_gate", dgate, f),
        "w_up": _tn_matmul("dw_up", dup, f), "w_down": _tn_matmul("dw_down", act, dh2)})
    (dsq, dsk, dsv), late_got = _sb_bwd(sq, sk, sv, do_sb, cmat, msuf, mpre, late_gs)
    late_f32, late_bf16 = _add_sibling("add_sibling_late", late_gs, late_got, c_idx)
    (dqn, dqr, dkn, dkr, dvm), late_others = _mla_bwd(qn, qr, kn, kr, vm, o_mla, do_mla, lse, late_bf16)
    mine_late, theirs_late = reduced("late", late_f32, late_others)
    dx, a1, a2, dcq, dckv, dkrc, dkrs, dg_q, dg_kv, dg_mix = _bwd_a(xs, dh1, cq, ckv, dqn, dqr, dkn, dvm, dkr, dsq, dsk, dsv, tabs, w)

    g_cq, g_ckv, g_krc, g_krs, g_sq, g_sk, g_sv = _tn_multi("dw_in", u, [dcq, dckv, dkrc, dkrs, dsq, dsk, dsv])
    g_qn, g_qr1, g_qr2 = _tn_multi("dw_uq", cqn, [dqn, a1, a2])
    g_kn, g_v = _tn_multi("dw_ukv", ckvn, [dkn, dvm])
    slots = lambda g: g.reshape(g.shape[0], MLA_HEADS, MLA_ROPE)
    g_kr = jnp.sum(slots(g_krc), axis=1) + _rot_cols_t(jnp.sum(slots(g_krs), axis=1))
    g_qr = (slots(g_qr1) + _rot_cols_t(slots(g_qr2))).reshape(Q_RANK, -1)
    early_gs = shards_of(EARLY, {
        "w_in": jnp.concatenate([g_cq, g_ckv, g_kr, g_sq, g_sk, g_sv], axis=1),
        "w_uq": _by_head(g_qn, g_qr, MLA_NOPE, MLA_ROPE),
        "w_ukv": _by_head(g_kn * MLA_DK_SCALE, g_v, MLA_NOPE, MLA_V)})
    early_f32, early_bf16 = _add_sibling("add_sibling_early", early_gs, _swap_halves("swap_halves_early", early_gs), c_idx)
    mine_early, theirs_early = reduced("early", early_f32, _chip_scatter(early_bf16))
    halves = dict(zip(EARLY + LATE, zip(mine_early + mine_late, theirs_early + theirs_late)))

    small_parts = jnp.concatenate([dg_mix, dg_q, dg_kv, dg_a, dg_b, dg_f, dg_n, loss_part], axis=1)
    small_sum = _allreduce_small(jnp.broadcast_to(small_parts, (8, small_parts.shape[1])))
    small_g, loss = small_sum[0:1, :-LANES], small_sum[0, -LANES]

    g_out, d_out, m_out, v_out = {}, {}, {}, {}
    for name, *_ in SHARDED:
        shape = given[name].shape
        outs = _adamw_halves("adamw_" + name, shard2d[name], *halves[name], block2d(mom_m[name], name), block2d(mom_v[name], name),
                             c_idx)
        g_out[name], d_out[name], m_out[name], v_out[name] = ((t.T if name in FLIPPED else t).reshape(shape) for t in outs)
    cat = lambda src: jnp.concatenate([src[name].reshape(1, n) for name, n in SMALL], axis=1)
    d, mn, vn = _adamw("adamw_small", cat(given), small_g, cat(mom_m), cat(mom_v))
    off = 0
    for name, n in SMALL:
        shape = given[name].shape
        g_out[name], d_out[name], m_out[name], v_out[name] = (t[:, off:off + n].reshape(shape) for t in (small_g, d, mn, vn))
        off += n

    order = ["norm_mix", "w_in", "q_latent_norm", "w_uq", "kv_latent_norm", "w_ukv", "out_norm_mla", "out_norm_sb", "w_o",
             "norm_ffn", "w_gate", "w_up", "w_down", "norm_final"]
    return (loss, dx[None], *[g_out[n] for n in order], *[d_out[n] for n in order], *[m_out[n] for n in order],
            *[v_out[n] for n in order])
```

```python
import functools
import math

import jax
import jax.numpy as jnp
from jax import lax
from jax.experimental import pallas as pl
from jax.experimental.pallas import tpu as pltpu

F32 = jnp.float32
BF16 = jnp.bfloat16
MESH = pl.DeviceIdType.MESH

D_MODEL = 1024
EPS = 1e-6
MLA_HEADS = 8
MLA_NOPE = 64
MLA_ROPE = 32
MLA_V = 64
MLA_QK = MLA_NOPE + MLA_ROPE
Q_RANK = 256
KV_RANK = 128
ROPE_THETA = 10000.0
SB_HEADS = 8
SB_DIM = 64
MLA_WIDTH = MLA_HEADS * MLA_V
SB_WIDTH = SB_HEADS * SB_DIM
D_FF = 2816
IN_WIDTH = Q_RANK + KV_RANK + MLA_ROPE + 3 * SB_WIDTH

ADAM_LR = 0.001
ADAM_B1 = 0.9
ADAM_B2 = 0.999
ADAM_EPS = 1e-08
ADAM_WD = 0.01
ADAM_STEP = 10

N_SHARD = 4
LANES = 128
ROPE_TILE = LANES
VMEM_LIMIT = 56 * 1024 * 1024
TN_ACC_BYTES = 6 * 1024 * 1024 + 512 * 1024
NEG = -1e30
MLA_SCALE = 1.0 / math.sqrt(MLA_QK)
MLA_DK_SCALE = math.log(2.0)
MLA_QSCALE = MLA_SCALE * math.log2(math.e)
SB_SKIP = 110.0

ROW_TILE = 512
ROW_TILE_ELEMENTWISE = 256
MLA_TQ = 1024
SB_TQ = 512
MLA_TK = 1024
MLA_BWD_TK = 1024
MLA_DIAG_TK = 512
SB_TK = 256
TN_TS = 2048


def _dot(a, b):
    return jnp.dot(a, b, preferred_element_type=F32)


def _dot_nt(a, b):
    return lax.dot_general(a, b, (((1,), (1,)), ((), ())), preferred_element_type=F32)


def _dot_tn(a, b):
    return lax.dot_general(a, b, (((0,), (0,)), ((), ())), preferred_element_type=F32)


def _params(n_grid, vmem=VMEM_LIMIT):
    return pltpu.CompilerParams(dimension_semantics=("arbitrary",) * n_grid, vmem_limit_bytes=vmem)


def _rms(x):
    r = lax.rsqrt(jnp.mean(x * x, axis=-1, keepdims=True) + EPS)
    return x * r, r


def _rms_bwd(n, r, g, dy):
    dn = dy * g
    dx = r * (dn - n * jnp.mean(dn * n, axis=-1, keepdims=True))
    return dx, jnp.sum(dy * n, axis=0, keepdims=True)


def _accumulate(ref, val, step):
    @pl.when(step == 0)
    def _():
        ref[...] = val

    @pl.when(step != 0)
    def _():
        ref[...] += val


def _rowwise(name, body, rows, consts, row_out, acc_out, tm):
    n_rows = rows[0].shape[0]
    tm = min(tm, n_rows)
    nr, nc, no = len(rows), len(consts), len(row_out)

    def kern(*refs):
        body(refs[:nr], refs[nr:nr + nc], refs[nr + nc:nr + nc + no], refs[nr + nc + no:], pl.program_id(0))

    in_specs = [pl.BlockSpec((tm, a.shape[1]), lambda i: (i, 0)) for a in rows]
    in_specs += [pl.BlockSpec(a.shape, lambda i: (0, 0), pipeline_mode=pl.Buffered(1)) for a in consts]
    out_specs = [pl.BlockSpec((tm, s.shape[1]), lambda i: (i, 0)) for s in row_out]
    out_specs += [pl.BlockSpec(s.shape, lambda i: (0, 0)) for s in acc_out]
    return pl.pallas_call(
        kern, name=name, grid=(n_rows // tm,), in_specs=in_specs, out_specs=out_specs,
        out_shape=list(row_out) + list(acc_out), compiler_params=_params(1),
    )(*rows, *consts)


def _sds(shape, dtype):
    return jax.ShapeDtypeStruct(shape, dtype)


def _fwd_a(x, tabs, w):
    s = x.shape[0]

    def body(r, c, o, a, step):
        x_ref, cos_ref, sin_ref = r
        gmix, wcq, wckv, wkr, wkrr, wsq, wsk, wsv, gq, wqn, wqr, wqrr, gkv, wkn, wv = c
        u_o, cq_o, ckv_o, cqn_o, ckvn_o, qn_o, qr_o, kn_o, v_o, kr_o, sq_o, sk_o, sv_o = o
        cos, sin = cos_ref[...], sin_ref[...]
        n, _ = _rms(x_ref[...])
        u = (n * gmix[...]).astype(BF16)
        u_o[...] = u
        cq = _dot(u, wcq[...])
        ckv = _dot(u, wckv[...])
        kr_o[...] = (_dot(u, wkr[...]) * cos[:, :ROPE_TILE] + _dot(u, wkrr[...]) * sin[:, :ROPE_TILE]).astype(BF16)
        sq_o[...] = _dot(u, wsq[...]).astype(BF16)
        sk_o[...] = _dot(u, wsk[...]).astype(BF16)
        sv_o[...] = _dot(u, wsv[...]).astype(BF16)
        cq_o[...] = cq
        ckv_o[...] = ckv
        nq, _ = _rms(cq)
        cqn = (nq * gq[...]).astype(BF16)
        cqn_o[...] = cqn
        qn_o[...] = (_dot(cqn, wqn[...]) * MLA_QSCALE).astype(BF16)
        qr_o[...] = ((_dot(cqn, wqr[...]) * cos + _dot(cqn, wqrr[...]) * sin) * MLA_QSCALE).astype(BF16)
        nkv, _ = _rms(ckv)
        ckvn = (nkv * gkv[...]).astype(BF16)
        ckvn_o[...] = ckvn
        kn_o[...] = _dot(ckvn, wkn[...]).astype(BF16)
        v_o[...] = _dot(ckvn, wv[...]).astype(BF16)

    outs = [
        _sds((s, D_MODEL), BF16), _sds((s, Q_RANK), F32), _sds((s, KV_RANK), F32), _sds((s, Q_RANK), BF16),
        _sds((s, KV_RANK), BF16), _sds((s, MLA_HEADS * MLA_NOPE), BF16), _sds((s, MLA_HEADS * MLA_ROPE), BF16),
        _sds((s, MLA_HEADS * MLA_NOPE), BF16), _sds((s, MLA_WIDTH), BF16), _sds((s, ROPE_TILE), BF16),
        _sds((s, SB_WIDTH), BF16), _sds((s, SB_WIDTH), BF16), _sds((s, SB_WIDTH), BF16),
    ]
    consts = [w["g_mix"], w["w_cq"], w["w_ckv"], w["w_kr4"], w["w_kr4r"], w["w_sbq"], w["w_sbk"], w["w_sbv"], w["g_q"],
              w["w_qn"], w["w_qr"], w["w_qrr"], w["g_kv"], w["w_kn"], w["w_v"]]
    return _rowwise("fwd_a", body, [x, tabs["cos"], tabs["sin"]], consts, outs, [], ROW_TILE)


def _fwd_b1(x, o_mla, o_sb, w):
    s = x.shape[0]

    def body(r, c, o, a, step):
        x_ref, oa_ref, ob_ref = r
        ga, gb, woa, wob, gf, wg, wu = c
        mg_o, h1_o, f_o, gate_o, up_o, act_o = o
        na, _ = _rms(oa_ref[...])
        nb, _ = _rms(ob_ref[...])
        ma = (na * ga[...]).astype(BF16)
        mb = (nb * gb[...]).astype(BF16)
        mg_o[:, :MLA_WIDTH] = ma
        mg_o[:, MLA_WIDTH:] = mb
        h1 = x_ref[...] + _dot(ma, woa[...]) + _dot(mb, wob[...])
        h1_o[...] = h1
        nf, _ = _rms(h1)
        f = (nf * gf[...]).astype(BF16)
        f_o[...] = f
        gate = _dot(f, wg[...])
        up = _dot(f, wu[...])
        gate_o[...] = gate.astype(BF16)
        up_o[...] = up.astype(BF16)
        act_o[...] = (gate * (1.0 / (1.0 + jnp.exp(-gate))) * up).astype(BF16)

    outs = [_sds((s, D_MODEL), BF16), _sds((s, D_MODEL), F32), _sds((s, D_MODEL), BF16), _sds((s, D_FF), BF16),
            _sds((s, D_FF), BF16), _sds((s, D_FF), BF16)]
    consts = [w["g_a"], w["g_b"], w["w_oa"], w["w_ob"], w["g_f"], w["w_gate"], w["w_up"]]
    return _rowwise("fwd_b1", body, [x, o_mla, o_sb], consts, outs, [], ROW_TILE)


def _fwd_b2(h1, act, tgt, w):
    s = h1.shape[0]

    def body(r, c, o, a, step):
        h1_ref, act_ref, t_ref = r
        wd, gn = c
        (dh2_o,) = o
        loss_o, dgn_o = a
        h2 = h1_ref[...] + _dot(act_ref[...], wd[...])
        n2, r2 = _rms(h2)
        err = n2 * gn[...] - t_ref[...]
        part = jnp.sum(jnp.sum(err * err, axis=1, keepdims=True), axis=0, keepdims=True) * (0.5 / D_MODEL)
        _accumulate(loss_o, jnp.broadcast_to(part, (1, LANES)), step)
        dh2, dgn = _rms_bwd(n2, r2, gn[...], err * (1.0 / D_MODEL))
        dh2_o[...] = dh2
        _accumulate(dgn_o, dgn, step)

    return _rowwise("fwd_b2", body, [h1, act, tgt], [w["w_down"], w["g_n"]], [_sds((s, D_MODEL), F32)],
                    [_sds((1, LANES), F32), _sds((1, D_MODEL), F32)], ROW_TILE)


def _bwd_b1(dh2, gate, up, w):
    s = dh2.shape[0]

    def body(r, c, o, a, step):
        dh2_ref, gate_ref, up_ref = r
        (wdt,) = c
        dgate_o, dup_o = o
        dact = _dot(dh2_ref[...].astype(BF16), wdt[...])
        gate = gate_ref[...].astype(F32)
        sig = 1.0 / (1.0 + jnp.exp(-gate))
        dup_o[...] = (dact * (gate * sig)).astype(BF16)
        dgate_o[...] = (dact * up_ref[...].astype(F32) * (sig * (1.0 + gate * (1.0 - sig)))).astype(BF16)

    return _rowwise("bwd_b1", body, [dh2, gate, up], [w["w_down_t"]], [_sds((s, D_FF), BF16), _sds((s, D_FF), BF16)],
                    [], ROW_TILE_ELEMENTWISE)


def _bwd_b2(dgate, dup, h1, dh2, o_mla, o_sb, w):
    s = h1.shape[0]

    def body(r, c, o, a, step):
        dgate_ref, dup_ref, h1_ref, dh2_ref, oa_ref, ob_ref = r
        wgt, wut, gf, woat, wobt, ga, gb = c
        dh1_o, doa_o, dob_o = o
        dgf_o, dga_o, dgb_o = a
        df = _dot(dgate_ref[...], wgt[...]) + _dot(dup_ref[...], wut[...])
        nf, rf = _rms(h1_ref[...])
        dres, dgf = _rms_bwd(nf, rf, gf[...], df)
        dh1 = dh2_ref[...] + dres
        dh1_o[...] = dh1
        dh1b = dh1.astype(BF16)
        na, ra = _rms(oa_ref[...])
        doa, dga = _rms_bwd(na, ra, ga[...], _dot(dh1b, woat[...]))
        nb, rb = _rms(ob_ref[...])
        dob, dgb = _rms_bwd(nb, rb, gb[...], _dot(dh1b, wobt[...]))
        doa_o[...] = doa
        dob_o[...] = dob
        _accumulate(dgf_o, dgf, step)
        _accumulate(dga_o, dga, step)
        _accumulate(dgb_o, dgb, step)

    consts = [w["w_gate_t"], w["w_up_t"], w["g_f"], w["w_oa_t"], w["w_ob_t"], w["g_a"], w["g_b"]]
    outs = [_sds((s, D_MODEL), F32), _sds((s, MLA_WIDTH), F32), _sds((s, SB_WIDTH), F32)]
    accs = [_sds((1, D_MODEL), F32), _sds((1, MLA_WIDTH), F32), _sds((1, SB_WIDTH), F32)]
    return _rowwise("bwd_b2", body, [dgate, dup, h1, dh2, o_mla, o_sb], consts, outs, accs, ROW_TILE)


def _fold_pairs(t):
    return jnp.concatenate([t[:, :LANES] + t[:, LANES:2 * LANES], t[:, 2 * LANES:3 * LANES] + t[:, 3 * LANES:]], axis=1)


def _bwd_a(x, dh1, cq, ckv, dqn, dqr, dkn, dvm, dkr, dsq, dsk, dsv, tabs, w):
    s = x.shape[0]

    def body(r, c, o, a, step):
        x_ref, dh1_ref, cq_ref, ckv_ref, dqn_ref, dqr_ref, dkn_ref, dvm_ref, dkr_ref, dsq_ref, dsk_ref, dsv_ref, cos_ref, sin_ref = r
        wqnt, wqrt, wqrrt, gq, wknt, wvt, gkv, wcqt, wckvt, wkrt, wkrrt, wsqt, wskt, wsvt, gmix = c
        dx_o, a1_o, a2_o, dcq_o, dckv_o, dkrc_o, dkrs_o = o
        dgq_o, dgkv_o, dgmix_o = a
        cos, sin = cos_ref[...], sin_ref[...]
        dqr = _fold_pairs(dqr_ref[...])
        a1 = (dqr * cos).astype(BF16)
        a2 = (dqr * sin).astype(BF16)
        a1_o[...] = a1
        a2_o[...] = a2
        nq, rq = _rms(cq_ref[...])
        dcqn = _dot(dqn_ref[...].astype(BF16), wqnt[...]) + _dot(a1, wqrt[...]) + _dot(a2, wqrrt[...])
        dcq, dgq = _rms_bwd(nq, rq, gq[...], dcqn)
        nkv, rkv = _rms(ckv_ref[...])
        dckvn = _dot((dkn_ref[...] * MLA_DK_SCALE).astype(BF16), wknt[...]) + _dot(dvm_ref[...].astype(BF16), wvt[...])
        dckv, dgkv = _rms_bwd(nkv, rkv, gkv[...], dckvn)
        dkr = _fold_pairs(dkr_ref[...]) * MLA_DK_SCALE
        dcq_b = dcq.astype(BF16)
        dckv_b = dckv.astype(BF16)
        dkrc = (dkr * cos).astype(BF16)
        dkrs = (dkr * sin).astype(BF16)
        dcq_o[...] = dcq_b
        dckv_o[...] = dckv_b
        dkrc_o[...] = dkrc
        dkrs_o[...] = dkrs
        du = (_dot(dcq_b, wcqt[...]) + _dot(dckv_b, wckvt[...]) + _dot(dkrc, wkrt[...]) + _dot(dkrs, wkrrt[...])
              + _dot(dsq_ref[...].astype(BF16), wsqt[...]) + _dot(dsk_ref[...].astype(BF16), wskt[...])
              + _dot(dsv_ref[...].astype(BF16), wsvt[...]))
        nx, rx = _rms(x_ref[...])
        dres, dgmix = _rms_bwd(nx, rx, gmix[...], du)
        dx_o[...] = dh1_ref[...] + dres
        _accumulate(dgq_o, dgq, step)
        _accumulate(dgkv_o, dgkv, step)
        _accumulate(dgmix_o, dgmix, step)

    consts = [w["w_qn_t"], w["w_qr_t"], w["w_qrr_t"], w["g_q"], w["w_kn_t"], w["w_v_t"], w["g_kv"], w["w_cq_t"], w["w_ckv_t"],
              w["w_kr8_t"], w["w_kr8r_t"], w["w_sbq_t"], w["w_sbk_t"], w["w_sbv_t"], w["g_mix"]]
    rope_w = MLA_HEADS * MLA_ROPE
    outs = [_sds((s, D_MODEL), F32), _sds((s, rope_w), BF16), _sds((s, rope_w), BF16), _sds((s, Q_RANK), BF16),
            _sds((s, KV_RANK), BF16), _sds((s, rope_w), BF16), _sds((s, rope_w), BF16)]
    accs = [_sds((1, Q_RANK), F32), _sds((1, KV_RANK), F32), _sds((1, D_MODEL), F32)]
    rows = [x, dh1, cq, ckv, dqn, dqr, dkn, dvm, dkr, dsq, dsk, dsv, tabs["cos"], tabs["sin"]]
    return _rowwise("bwd_a", body, rows, consts, outs, accs, ROW_TILE)


def _tn_multi(name, x, ys):
    s, k = x.shape
    ts = min(TN_TS, s)
    n_y = len(ys)

    def kern(*refs):
        step = pl.program_id(0)
        xb = refs[0][...].astype(BF16)
        for j in range(n_y):
            _accumulate(refs[1 + n_y + j], _dot_tn(xb, refs[1 + j][...].astype(BF16)), step)

    return pl.pallas_call(
        kern, name=name, grid=(s // ts,),
        in_specs=[pl.BlockSpec((ts, k), lambda i: (i, 0))] + [pl.BlockSpec((ts, y.shape[1]), lambda i: (i, 0)) for y in ys],
        out_specs=[pl.BlockSpec((k, y.shape[1]), lambda i: (0, 0)) for y in ys],
        out_shape=[_sds((k, y.shape[1]), F32) for y in ys], compiler_params=_params(1),
    )(x, *ys)


def _tn_tile(k, n):
    if n % LANES or k * n * 4 <= TN_ACC_BYTES:
        return n
    units = n // LANES
    best = 1
    for d in range(1, units + 1):
        if units % d == 0 and k * d * LANES * 4 <= TN_ACC_BYTES:
            best = d
    return best * LANES


def _tn_matmul(name, x, y):
    s, k = x.shape
    n = y.shape[1]
    ts = min(TN_TS, s)
    tn = _tn_tile(k, n)

    def kern(x_ref, y_ref, o_ref):
        step = pl.program_id(1)
        _accumulate(o_ref, _dot_tn(x_ref[...].astype(BF16), y_ref[...].astype(BF16)), step)

    return pl.pallas_call(
        kern, name=name, grid=(n // tn, s // ts),
        in_specs=[pl.BlockSpec((ts, k), lambda j, i: (i, 0)), pl.BlockSpec((ts, tn), lambda j, i: (i, j))],
        out_specs=pl.BlockSpec((k, tn), lambda j, i: (0, j)), out_shape=_sds((k, n), F32), compiler_params=_params(2),
    )(x, y)


def _lanes(rows, lo, width):
    lane = lax.broadcasted_iota(jnp.int32, (rows, LANES), 1)
    return jnp.logical_and(lane >= lo, lane < lo + width)


def _keep(mask, t):
    return jnp.where(mask, t, jnp.zeros_like(t))


def _mla_qcat(qn_ref, qr_ref, rope_lo, half, rows):
    qn = _keep(_lanes(rows, MLA_NOPE * half, MLA_NOPE), qn_ref[...])
    qr = _keep(_lanes(rows, rope_lo, MLA_ROPE), qr_ref[...])
    return jnp.concatenate([qn, qr], axis=1)


def _diag_mask(rows, width, row0, col0):
    row = lax.broadcasted_iota(jnp.int32, (rows, width), 0)
    col = lax.broadcasted_iota(jnp.int32, (rows, width), 1)
    return col + (col0 - row0) <= row


def _mla_fwd(qn, qr, kn, kr, v, riders=(), tq=MLA_TQ, tk=MLA_TK, td=MLA_TQ):
    s = qn.shape[0]
    tq, tk, td = min(tq, s), min(tk, s), min(td, s)
    ratio = tq // tk

    n_ride = len(riders)
    n_pairs = MLA_HEADS // 2

    def kern(qn_ref, qr_ref, kn_ref, kr_ref, v_ref, *rest):
        o_ref, lse_ref = rest[n_ride:n_ride + 2]
        g = pl.program_id(0)
        i = pl.program_id(1)
        if n_ride:
            send, forward, finish = _gather_steps(rest[:n_ride], rest[n_ride + 2:2 * n_ride + 2], *rest[2 * n_ride + 2:])
            pl.when(jnp.logical_and(g == 0, i == 0))(send)
            pl.when(jnp.logical_and(g == 1, i == 0))(forward)
        qcat = [_mla_qcat(qn_ref, qr_ref, MLA_ROPE * (2 * (g % 2) + half), half, tq) for half in range(2)]

        def block(k0, width, carry, row0, masked, half):
            m, l, acc = (c[row0:] for c in carry)
            ks = pl.ds(pl.multiple_of(k0, width), width)
            kcat = jnp.concatenate([kn_ref[ks, :], kr_ref[ks, :]], axis=1)
            sc = _dot_nt(qcat[half][row0:], kcat)
            if masked:
                sc = jnp.where(_diag_mask(tq - row0, width, row0, row0), sc, NEG)
            m_new = jnp.maximum(m, jnp.max(sc, axis=1, keepdims=True))
            p = jnp.exp2(sc - m_new)
            alpha = jnp.exp2(m - m_new)
            l = alpha * l + jnp.sum(p, axis=1, keepdims=True)
            acc = alpha * acc + _dot(p.astype(BF16), v_ref[ks, :])
            new = (m_new, l, acc)
            return new if row0 == 0 else tuple(jnp.concatenate([c[:row0], n], axis=0) for c, n in zip(carry, new))

        def both(k0, width, carries, row0, masked):
            return tuple(block(k0, width, carries[half], row0, masked, half) for half in range(2))

        init = (jnp.full((tq, 1), NEG, F32), jnp.zeros((tq, 1), F32), jnp.zeros((tq, LANES), F32))
        carries = lax.fori_loop(0, i * ratio, lambda kb, c: both(kb * tk, tk, c, 0, False), (init, init))
        for row0 in range(0, tq, td):
            carries = both(i * tq + row0, td, carries, row0, True)
        for half in range(2):
            m, l, acc = carries[half]
            out = _keep(_lanes(tq, MLA_V * half, MLA_V), acc / l)
            lse = _keep(_lanes(tq, MLA_ROPE * half, MLA_ROPE), jnp.broadcast_to(m + jnp.log2(l), (tq, LANES)))
            if half == 0:
                o_ref[...] = out
                lse_ref[...] = lse
            else:
                o_ref[...] += out
                lse_ref[...] += lse
        if n_ride:
            pl.when(jnp.logical_and(g == n_pairs - 1, i == s // tq - 1))(finish)

    qblk = pl.BlockSpec((tq, LANES), lambda g, i: (i, g))
    full = pl.BlockSpec((s, LANES), lambda g, i: (0, g))
    outs = pl.pallas_call(
        kern, name="mla_fwd", grid=(n_pairs, s // tq),
        in_specs=[qblk, pl.BlockSpec((tq, LANES), lambda g, i: (i, g // 2)), full, pl.BlockSpec((s, LANES), lambda g, i: (0, 0)), full]
        + [HBM_SPEC] * n_ride,
        out_specs=[qblk, qblk] + [HBM_SPEC] * n_ride,
        out_shape=[_sds((s, MLA_WIDTH), F32), _sds((s, n_pairs * LANES), F32)] + [_sds((N_SHARD,) + a.shape, a.dtype) for a in riders],
        scratch_shapes=_gather_sems(n_ride) if n_ride else [], compiler_params=_params(2),
    )(qn, qr, kn, kr, v, *riders)
    return outs[0], outs[1], outs[2:]


def _mla_bwd(qn, qr, kn, kr, v, o, do, lse, riders=(), tq=MLA_TQ, tk=MLA_BWD_TK, td=MLA_DIAG_TK):
    s = qn.shape[0]
    tq, tk, td = min(tq, s), min(tk, s), min(td, s)
    ratio = tq // tk

    n_ride = len(riders)
    n_pairs = MLA_HEADS // 2

    def kern(qn_ref, qr_ref, kn_ref, kr_ref, v_ref, o_ref, do_ref, lse_ref, *rest):
        dqn_ref, dqr_ref, dkn_ref, dkr_ref, dv_ref = rest[n_ride:n_ride + 5]
        g = pl.program_id(0)
        i = pl.program_id(1)
        if n_ride:
            start, finish = _scatter_steps(rest[:n_ride], rest[n_ride + 5:2 * n_ride + 5], *rest[2 * n_ride + 5:])
            pl.when(jnp.logical_and(g == 0, i == 0))(start)

        @pl.when(i == 0)
        def _():
            dkn_ref[...] = jnp.zeros_like(dkn_ref)
            dkr_ref[...] = jnp.zeros_like(dkr_ref)
            dv_ref[...] = jnp.zeros_like(dv_ref)

        for half in range(2):
            rope_lo = MLA_ROPE * (2 * (g % 2) + half)
            qcat = _mla_qcat(qn_ref, qr_ref, rope_lo, half, tq)
            mine = _lanes(tq, MLA_V * half, MLA_V)
            do_f = _keep(mine, do_ref[...])
            do_b = do_f.astype(BF16)
            delta = jnp.sum(do_f * o_ref[...], axis=1, keepdims=True)
            lse_v = lse_ref[:, MLA_ROPE * half:MLA_ROPE * half + 1]

            def block(k0, width, dq_acc, row0, masked, qcat=qcat, do_b=do_b, delta=delta, lse_v=lse_v):
                ks = pl.ds(pl.multiple_of(k0, width), width)
                kcat = jnp.concatenate([kn_ref[ks, :], kr_ref[ks, :]], axis=1)
                qc, dob = qcat[row0:], do_b[row0:]
                p = jnp.exp2(_dot_nt(qc, kcat) - lse_v[row0:])
                if masked:
                    p = jnp.where(_diag_mask(tq - row0, width, row0, row0), p, 0.0)
                ds = (p * (_dot_nt(dob, v_ref[ks, :]) - delta[row0:])).astype(BF16)
                dv_ref[ks, :] += _dot_tn(p.astype(BF16), dob)
                dkc = _dot_tn(ds, qc)
                dkn_ref[ks, :] += dkc[:, :LANES]
                dkr_ref[ks, :] += dkc[:, LANES:]
                new = dq_acc[row0:] + _dot(ds, kcat)
                return new if row0 == 0 else jnp.concatenate([dq_acc[:row0], new], axis=0)

            acc = lax.fori_loop(0, i * ratio, lambda kb, c, block=block: block(kb * tk, tk, c, 0, False),
                                jnp.zeros((tq, 2 * LANES), F32))
            for row0 in range(0, tq, td):
                acc = block(i * tq + row0, td, acc, row0, True)
            dqn = _keep(_lanes(tq, MLA_NOPE * half, MLA_NOPE), acc[:, :LANES] * MLA_SCALE)
            dqr = _keep(_lanes(tq, rope_lo, MLA_ROPE), acc[:, LANES:] * MLA_SCALE)
            if half == 0:
                dqn_ref[...] = dqn
                dqr_ref[...] = dqr
            else:
                dqn_ref[...] += dqn
                dqr_ref[...] += dqr
        if n_ride:
            pl.when(jnp.logical_and(g == n_pairs - 1, i == s // tq - 1))(finish)

    qblk = pl.BlockSpec((tq, LANES), lambda g, i: (i, g))
    full = pl.BlockSpec((s, LANES), lambda g, i: (0, g))
    once = lambda spec_map: pl.BlockSpec((s, LANES), spec_map, pipeline_mode=pl.Buffered(1))
    wide = _sds((s, n_pairs * LANES), F32)
    outs = pl.pallas_call(
        kern, name="mla_bwd", grid=(n_pairs, s // tq),
        in_specs=[qblk, pl.BlockSpec((tq, LANES), lambda g, i: (i, g // 2)), once(lambda g, i: (0, g)), once(lambda g, i: (0, 0)),
                  once(lambda g, i: (0, g)), qblk, qblk, qblk] + [HBM_SPEC] * n_ride,
        out_specs=[qblk, qblk, full, full, full] + [HBM_SPEC] * n_ride,
        out_shape=[wide] * 5 + [_sds((N_SHARD - 1,) + p.shape[1:], p.dtype) for p in riders],
        scratch_shapes=_scatter_sems(n_ride) if n_ride else [], compiler_params=_params(2),
    )(qn, qr, kn, kr, v, o, do, lse, *riders)
    return outs[:5], outs[5:]


def _sb_masks(tk):
    j = lax.broadcasted_iota(jnp.int32, (tk, tk), 0)
    c = lax.broadcasted_iota(jnp.int32, (tk, tk), 1)
    return (j > c).astype(BF16), (j < c).astype(BF16)


def _sb_scores(qs, kk, msuf, strict):
    z = _dot_nt(qs, kk)
    lom = -(jnp.maximum(z, 0.0) + jnp.log(1.0 + jnp.exp(-jnp.abs(z))))
    if strict is not None:
        lom = jnp.where(strict, lom, 0.0)
    return z, lom, _dot(lom.astype(BF16), msuf)


def _sb_strict(tq, tk, d):
    row = lax.broadcasted_iota(jnp.int32, (tq, tk), 0)
    col = lax.broadcasted_iota(jnp.int32, (tq, tk), 1)
    return col + d * tk < row


def _sb_fwd(q, k, v, msuf, tq=SB_TQ, tk=SB_TK):
    s = q.shape[0]
    tq, tk = min(tq, s), min(tk, s)
    ratio = tq // tk

    def kern(q_ref, k_ref, v_ref, m_ref, o_ref, c_ref):
        i = pl.program_id(1)
        msf = m_ref[...]
        lane = lax.broadcasted_iota(jnp.int32, (tq, LANES), 1)
        mine = [_lanes(tq, SB_DIM * half, SB_DIM) for half in range(2)]
        qs = [_keep(m, q_ref[...]) * 0.125 for m in mine]

        def block(kb, carry, dd, half):
            c, acc, cm = carry
            ks = pl.ds(pl.multiple_of(kb * tk, tk), tk)
            strict = None if dd is None else _sb_strict(tq, tk, dd)
            z, lom, suf = _sb_scores(qs[half], k_ref[ks, :], msf, strict)
            a = jnp.exp(z + lom + (suf + c))
            if strict is not None:
                a = jnp.where(strict, a, 0.0)
            acc = acc + _dot(a.astype(BF16), v_ref[ks, :])
            cm = jnp.where(lane == kb, c, cm)
            return c + jnp.sum(lom, axis=1, keepdims=True), acc, cm

        init = (jnp.zeros((tq, 1), F32), jnp.zeros((tq, LANES), F32), jnp.full((tq, LANES), NEG, F32))
        carries = [init, init]
        for dd in range(ratio - 1, -1, -1):
            carries = [block(i * ratio + dd, carries[half], dd, half) for half in range(2)]

        def live(st):
            return jnp.logical_and(st[0] >= 0, jnp.maximum(jnp.max(st[1][0]), jnp.max(st[2][0])) > -SB_SKIP)

        def step(st):
            return (st[0] - 1, block(st[0], st[1], None, 0), block(st[0], st[2], None, 1))

        _, done0, done1 = lax.while_loop(live, step, (i * ratio - 1, carries[0], carries[1]))
        o_ref[...] = _keep(mine[0], done0[1]) + _keep(mine[1], done1[1])
        c_ref[:, :LANES] = done0[2]
        c_ref[:, LANES:] = done1[2]

    qblk = lambda n: pl.BlockSpec((tq, n), lambda g, i: (i, g))
    full = pl.BlockSpec((s, LANES), lambda g, i: (0, g))
    return pl.pallas_call(
        kern, name="sb_fwd", grid=(SB_HEADS // 2, s // tq),
        in_specs=[qblk(LANES), full, full, pl.BlockSpec((tk, tk), lambda g, i: (0, 0))],
        out_specs=[qblk(LANES), qblk(2 * LANES)],
        out_shape=[_sds((s, SB_WIDTH), F32), _sds((s, SB_HEADS * LANES), F32)], compiler_params=_params(2),
    )(q, k, v, msuf)


def _sb_bwd(q, k, v, do, cmat, msuf, mpre, riders=(), tq=SB_TQ, tk=SB_TK):
    s = q.shape[0]
    tq, tk = min(tq, s), min(tk, s)
    ratio = tq // tk

    n_ride = len(riders)
    n_pairs = SB_HEADS // 2

    def kern(q_ref, k_ref, v_ref, do_ref, c_ref, ms_ref, mp_ref, *rest):
        dq_ref, dk_ref, dv_ref = rest[n_ride:n_ride + 3]
        i = pl.program_id(1)
        if n_ride:
            start, finish = _swap_steps(rest[:n_ride], rest[n_ride + 3:2 * n_ride + 3], *rest[2 * n_ride + 3:])
            pl.when(jnp.logical_and(pl.program_id(0) == 0, i == 0))(start)

        @pl.when(i == 0)
        def _():
            dk_ref[...] = jnp.zeros_like(dk_ref)
            dv_ref[...] = jnp.zeros_like(dv_ref)

        msf = ms_ref[...]
        mpf = mp_ref[...]
        lane = lax.broadcasted_iota(jnp.int32, (tq, LANES), 1)
        lane1 = lax.broadcasted_iota(jnp.int32, (1, LANES), 1)
        mine = [_lanes(tq, SB_DIM * half, SB_DIM) for half in range(2)]
        qv = [_keep(m, q_ref[...]) for m in mine]
        qs = [t * 0.125 for t in qv]
        do_b = [_keep(m, do_ref[...]).astype(BF16) for m in mine]
        cm = [c_ref[:, :LANES], c_ref[:, LANES:]]

        def block(kb, carry, dd, half):
            dq_acc, pc = carry
            ks = pl.ds(pl.multiple_of(kb * tk, tk), tk)
            kk = k_ref[ks, :]
            strict = None if dd is None else _sb_strict(tq, tk, dd)
            z, lom, suf = _sb_scores(qs[half], kk, msf, strict)
            c = jnp.sum(jnp.where(lane == kb, cm[half], 0.0), axis=1, keepdims=True)
            a = jnp.exp(z + lom + (suf + c))
            if strict is not None:
                a = jnp.where(strict, a, 0.0)
            g = _dot_nt(do_b[half], v_ref[ks, :]) * a
            p = pc + _dot(g.astype(BF16), mpf)
            omb = jnp.exp(lom)
            dz = (g * omb - (1.0 - omb) * p) * 0.125
            if strict is not None:
                dz = jnp.where(strict, dz, 0.0)
            dz = dz.astype(BF16)
            dv_ref[ks, :] += _dot_tn(a.astype(BF16), do_b[half])
            dk_ref[ks, :] += _dot_tn(dz, qv[half])
            return dq_acc + _dot(dz, kk), pc + jnp.sum(g, axis=1, keepdims=True)

        def needed(cm_h):
            seen = jnp.logical_and(jnp.max(cm_h, axis=0, keepdims=True) > -SB_SKIP, lane1 < i * ratio)
            return jnp.sum(seen.astype(jnp.int32))

        first = i * ratio - jnp.maximum(needed(cm[0]), needed(cm[1]))
        init = (jnp.zeros((tq, LANES), F32), jnp.zeros((tq, 1), F32))
        carries = lax.fori_loop(first, i * ratio, lambda kb, c: (block(kb, c[0], None, 0), block(kb, c[1], None, 1)), (init, init))
        for dd in range(ratio):
            carries = [block(i * ratio + dd, carries[half], dd, half) for half in range(2)]
        dq_ref[...] = _keep(mine[0], carries[0][0]) + _keep(mine[1], carries[1][0])
        if n_ride:
            pl.when(jnp.logical_and(pl.program_id(0) == n_pairs - 1, i == s // tq - 1))(finish)

    qblk = lambda n: pl.BlockSpec((tq, n), lambda g, i: (i, g))
    full = pl.BlockSpec((s, LANES), lambda g, i: (0, g))
    msk = pl.BlockSpec((tk, tk), lambda g, i: (0, 0))
    outs = pl.pallas_call(
        kern, name="sb_bwd", grid=(n_pairs, s // tq),
        in_specs=[qblk(LANES), full, full, qblk(LANES), qblk(2 * LANES), msk, msk] + [HBM_SPEC] * n_ride,
        out_specs=[qblk(LANES), full, full] + [HBM_SPEC] * n_ride,
        out_shape=[_sds((s, SB_WIDTH), F32)] * 3 + _halves_shapes(riders),
        scratch_shapes=_swap_sems(n_ride) if n_ride else [], compiler_params=_params(2),
    )(q, k, v, do, cmat, msuf, mpre, *riders)
    return outs[:3], outs[3:]


def _place():
    return lax.axis_index("x"), lax.axis_index("y"), lax.axis_index("c")


def _other_chips(x, y):
    return [(1 - x, y), (x, 1 - y), (1 - x, 1 - y)]


HBM_SPEC = pl.BlockSpec(memory_space=pl.ANY)


def _gather_steps(ins, outs, send_sems, recv_sems):
    n = len(ins)
    x, y, c = _place()
    sibling = (x, y, 1 - c)
    chips = _other_chips(x, y)

    def half_of(a, ref, pc):
        half = ins[a].shape[0] // 2
        return ref.at[pl.ds(pl.multiple_of(pc * half, 16), half), :]

    def copy(a, k, chip, pc, to, src=None):
        dst = half_of(a, outs[a].at[2 * chip[0] + chip[1]], pc)
        return pltpu.make_async_remote_copy(src_ref=dst if src is None else src, dst_ref=dst, send_sem=send_sems.at[6 * a + k],
                                            recv_sem=recv_sems.at[6 * a + k], device_id=to, device_id_type=MESH)

    def first():
        return [copy(a, j, (x, y), c, (*chip, c), src=half_of(a, ins[a], c)) for a in range(n) for j, chip in enumerate(chips)]

    def passed():
        return [copy(a, 3 + j, chip, c, sibling) for j, chip in enumerate(chips) for a in range(n)]

    def send():
        for cp in first():
            cp.start()

    def forward():
        for j, chip in enumerate(chips):
            for a in range(n):
                copy(a, j, chip, c, sibling).wait_recv()
        for cp in passed():
            cp.start()

    def finish():
        for j, chip in enumerate(chips):
            for a in range(n):
                copy(a, 3 + j, chip, 1 - c, sibling).wait_recv()
        for cp in first() + passed():
            cp.wait_send()

    return send, forward, finish


def _gather_sems(n):
    return [pltpu.SemaphoreType.DMA((6 * n,)), pltpu.SemaphoreType.DMA((6 * n,))]


def _allgather_list(name, shards):
    n = len(shards)

    def body(*refs):
        for stage in _gather_steps(refs[:n], refs[n:2 * n], *refs[2 * n:]):
            stage()

    return pl.pallas_call(
        body, name=name, out_shape=[_sds((N_SHARD,) + a.shape, a.dtype) for a in shards], in_specs=[HBM_SPEC] * n,
        out_specs=[HBM_SPEC] * n, scratch_shapes=_gather_sems(n),
    )(*shards)


def _swap_steps(ins, outs, send_sems, recv_sems):
    x, y, c = _place()

    def copies():
        out = []
        for a in range(len(ins)):
            h = ins[a].shape[1] // 2
            src = ins[a].at[:, pl.ds(pl.multiple_of((1 - c) * h, 8), h), :]
            out.append(pltpu.make_async_remote_copy(src_ref=src, dst_ref=outs[a], send_sem=send_sems.at[a], recv_sem=recv_sems.at[a],
                                                    device_id=(x, y, 1 - c), device_id_type=MESH))
        return out

    def start():
        for cp in copies():
            cp.start()

    def finish():
        for cp in copies():
            cp.wait()

    return start, finish


def _swap_sems(n):
    return [pltpu.SemaphoreType.DMA((n,)), pltpu.SemaphoreType.DMA((n,))]


def _halves_shapes(gs):
    return [_sds((N_SHARD, g.shape[1] // 2, g.shape[2]), g.dtype) for g in gs]


def _swap_halves(name, gs):
    n = len(gs)

    def body(*refs):
        for stage in _swap_steps(refs[:n], refs[n:2 * n], *refs[2 * n:]):
            stage()

    return pl.pallas_call(body, name=name, out_shape=_halves_shapes(gs), in_specs=[HBM_SPEC] * n, out_specs=[HBM_SPEC] * n,
                          scratch_shapes=_swap_sems(n))(*gs)


def _add_sibling(name, gs, gots, c_idx):
    n = len(gs)

    def kern(c_ref, *refs):
        for a in range(n):
            tot = refs[a][...] + refs[n + a][...]
            refs[2 * n + a][...] = tot
            refs[3 * n + a][...] = tot.astype(BF16)

    quarter = lambda g: (None, g.shape[1] // 4, g.shape[2])
    in_specs = [pl.BlockSpec(quarter(g), lambda b, s, c_ref: (b, 2 * c_ref[0] + s, 0)) for g in gs]
    in_specs += [pl.BlockSpec(quarter(g), lambda b, s, c_ref: (b, s, 0)) for g in gs]
    out_specs = [pl.BlockSpec(quarter(g), lambda b, s, c_ref: (b, s, 0)) for g in gs] * 2
    out_shape = [_sds(t.shape, F32) for t in gots] + [_sds(t.shape, BF16) for t in gots]
    outs = pl.pallas_call(
        kern, name=name, out_shape=out_shape,
        grid_spec=pltpu.PrefetchScalarGridSpec(num_scalar_prefetch=1, grid=(N_SHARD, 2), in_specs=in_specs, out_specs=out_specs),
        compiler_params=_params(2),
    )(c_idx.reshape(1), *gs, *gots)
    return outs[:n], outs[n:]


def _scatter_steps(ins, outs, send_sems, recv_sems):
    x, y, c = _place()

    def copies():
        return [pltpu.make_async_remote_copy(
            src_ref=ins[a].at[2 * px + py], dst_ref=outs[a].at[j], send_sem=send_sems.at[3 * a + j], recv_sem=recv_sems.at[3 * a + j],
            device_id=(px, py, c), device_id_type=MESH) for a in range(len(ins)) for j, (px, py) in enumerate(_other_chips(x, y))]

    def start():
        for cp in copies():
            cp.start()

    def finish():
        for cp in copies():
            cp.wait()

    return start, finish


def _scatter_sems(n):
    return [pltpu.SemaphoreType.DMA((3 * n,)), pltpu.SemaphoreType.DMA((3 * n,))]


def _chip_scatter(ps):
    n = len(ps)

    def body(*refs):
        for stage in _scatter_steps(refs[:n], refs[n:2 * n], *refs[2 * n:]):
            stage()

    return pl.pallas_call(
        body, name="chip_scatter", out_shape=[_sds((N_SHARD - 1,) + p.shape[1:], p.dtype) for p in ps], in_specs=[HBM_SPEC] * n,
        out_specs=[HBM_SPEC] * n, scratch_shapes=_scatter_sems(n),
    )(*ps)


def _add_chips(name, ps, others, shard_idx):
    n = len(ps)

    def kern(b_ref, *refs):
        for a in range(n):
            tot = refs[a][...]
            for j in range(N_SHARD - 1):
                tot = tot + refs[n + a][j].astype(F32)
            refs[2 * n + a][...] = tot

    in_specs = [pl.BlockSpec((None, p.shape[1] // 2, p.shape[2]), lambda s, b_ref: (b_ref[0], s, 0)) for p in ps]
    in_specs += [pl.BlockSpec((N_SHARD - 1, p.shape[1] // 2, p.shape[2]), lambda s, b_ref: (0, s, 0)) for p in ps]
    out_specs = [pl.BlockSpec((p.shape[1] // 2, p.shape[2]), lambda s, b_ref: (s, 0)) for p in ps]
    return pl.pallas_call(
        kern, name=name, out_shape=[_sds(p.shape[1:], F32) for p in ps],
        grid_spec=pltpu.PrefetchScalarGridSpec(num_scalar_prefetch=1, grid=(2,), in_specs=in_specs, out_specs=out_specs),
        compiler_params=_params(1),
    )(shard_idx.reshape(1), *ps, *others)


def _swap_result(name, mines):
    n = len(mines)

    def body(*refs):
        ins, outs = refs[:n], refs[n:2 * n]
        send_sems, recv_sems = refs[2 * n:]
        x, y, c = _place()
        copies = [pltpu.make_async_remote_copy(src_ref=ins[a], dst_ref=outs[a], send_sem=send_sems.at[a], recv_sem=recv_sems.at[a],
                                               device_id=(x, y, 1 - c), device_id_type=MESH) for a in range(n)]
        for cp in copies:
            cp.start()
        for cp in copies:
            cp.wait()

    return pl.pallas_call(
        body, name=name, out_shape=[_sds(m.shape, m.dtype) for m in mines], in_specs=[HBM_SPEC] * n,
        out_specs=[HBM_SPEC] * n, scratch_shapes=[pltpu.SemaphoreType.DMA((n,)), pltpu.SemaphoreType.DMA((n,))],
    )(*mines)


def _allreduce_small(v):
    m_per, n = v.shape

    def body(x_ref, tot_ref, all_ref, send_sems, recv_sems, local_sem):
        x, y, c = _place()
        me, sibling = (x, y, c), (x, y, 1 - c)
        chips = _other_chips(x, y)

        def rows(px, py, pc):
            return all_ref.at[pl.ds(pl.multiple_of((4 * px + 2 * py + pc) * m_per, 8), m_per), :]

        def copy(k, block, to, src=None):
            return pltpu.make_async_remote_copy(
                src_ref=rows(*block) if src is None else src, dst_ref=rows(*block), send_sem=send_sems.at[k],
                recv_sem=recv_sems.at[k], device_id=to, device_id_type=MESH)

        mine = pltpu.make_async_copy(x_ref, rows(*me), local_sem)
        mine.start()
        first = [copy(0, me, sibling, src=x_ref)] + [copy(1 + j, me, (*chip, c), src=x_ref) for j, chip in enumerate(chips)]
        for cp in first:
            cp.start()
        passed = [copy(4 + j, (*chip, c), sibling) for j, chip in enumerate(chips)]
        for j, chip in enumerate(chips):
            copy(1 + j, (*chip, c), me).wait_recv()
            passed[j].start()
        copy(0, sibling, me).wait_recv()
        for j, chip in enumerate(chips):
            copy(4 + j, (*chip, 1 - c), me).wait_recv()
        for cp in first + passed:
            cp.wait_send()
        mine.wait()
        tot = all_ref[0:m_per, :]
        for dev in range(1, 8):
            tot = tot + all_ref[dev * m_per:(dev + 1) * m_per, :]
        tot_ref[...] = tot

    vmem = pl.BlockSpec(memory_space=pltpu.VMEM)
    return pl.pallas_call(
        body, name="allreduce_small", out_shape=_sds((m_per, n), F32), in_specs=[vmem], out_specs=vmem,
        scratch_shapes=[pltpu.VMEM((8 * m_per, n), F32), pltpu.SemaphoreType.DMA((7,)), pltpu.SemaphoreType.DMA((7,)),
                        pltpu.SemaphoreType.DMA],
    )(v)


def _adam_update(w, g, m, v):
    m_new = ADAM_B1 * m + (1.0 - ADAM_B1) * g
    v_new = ADAM_B2 * v + (1.0 - ADAM_B2) * (g * g)
    m_hat = m_new / (1.0 - ADAM_B1 ** ADAM_STEP)
    v_hat = v_new / (1.0 - ADAM_B2 ** ADAM_STEP)
    return -ADAM_LR * (m_hat / (jnp.sqrt(v_hat) + ADAM_EPS) + ADAM_WD * w), m_new, v_new


def _adamw(name, w, g, m, v):
    rows, width = w.shape
    tr = rows // 4 if rows % 32 == 0 else rows

    def kern(w_ref, g_ref, m_ref, v_ref, d_ref, mo_ref, vo_ref):
        d_ref[...], mo_ref[...], vo_ref[...] = _adam_update(w_ref[...], g_ref[...], m_ref[...], v_ref[...])

    spec = pl.BlockSpec((tr, width), lambda i: (i, 0))
    return pl.pallas_call(kern, name=name, grid=(rows // tr,), in_specs=[spec] * 4, out_specs=[spec] * 3,
                          out_shape=[_sds((rows, width), F32)] * 3, compiler_params=_params(1))(w, g, m, v)


def _adamw_halves(name, w, mine, theirs, m, v, c_idx):
    rows, width = w.shape
    tr = rows // 4

    def kern(c_ref, w_ref, mine_ref, theirs_ref, m_ref, v_ref, g_ref, d_ref, mo_ref, vo_ref):
        g = jnp.where(pl.program_id(0) == c_ref[0], mine_ref[...], theirs_ref[...])
        g_ref[...] = g
        d_ref[...], mo_ref[...], vo_ref[...] = _adam_update(w_ref[...], g, m_ref[...], v_ref[...])

    whole = pl.BlockSpec((tr, width), lambda h, j, c_ref: (2 * h + j, 0))
    part = pl.BlockSpec((tr, width), lambda h, j, c_ref: (j, 0))
    return pl.pallas_call(
        kern, name=name, out_shape=[_sds((rows, width), F32)] * 4,
        grid_spec=pltpu.PrefetchScalarGridSpec(num_scalar_prefetch=1, grid=(2, 2), in_specs=[whole, part, part, whole, whole],
                                               out_specs=[whole] * 4),
        compiler_params=_params(2),
    )(c_idx.reshape(1), w, mine, theirs, m, v)


SHARDED = (("w_in", D_MODEL, IN_WIDTH, 1), ("w_uq", Q_RANK, MLA_HEADS * MLA_QK, 1),
           ("w_ukv", KV_RANK, MLA_HEADS * (MLA_NOPE + MLA_V), 1), ("w_o", D_MODEL, D_MODEL, 0),
           ("w_gate", D_MODEL, D_FF, 1), ("w_up", D_MODEL, D_FF, 1), ("w_down", D_FF, D_MODEL, 0))
EARLY = ("w_in", "w_uq", "w_ukv")
LATE = ("w_o", "w_gate", "w_up", "w_down")
FLIPPED = ("w_gate", "w_up")
SMALL = (("norm_mix", D_MODEL), ("q_latent_norm", Q_RANK), ("kv_latent_norm", KV_RANK), ("out_norm_mla", MLA_WIDTH),
         ("out_norm_sb", SB_WIDTH), ("norm_ffn", D_MODEL), ("norm_final", D_MODEL))


def _full_weight(gathered, axis):
    n_sh, k, n = gathered.shape
    return gathered.transpose(1, 0, 2).reshape(k, n_sh * n) if axis == 1 else gathered.reshape(n_sh * k, n)


def _shard_major(g, axis):
    r, c = g.shape
    return g.reshape(r, N_SHARD, c // N_SHARD).transpose(1, 0, 2) if axis == 1 else g.reshape(N_SHARD, r // N_SHARD, c)


def _rot_cols(w):
    hh = MLA_ROPE // 2
    return jnp.concatenate([-w[..., hh:], w[..., :hh]], axis=-1)


def _rot_cols_t(g):
    hh = MLA_ROPE // 2
    return jnp.concatenate([g[..., hh:], -g[..., :hh]], axis=-1)


def _with_transposes(w):
    w.update({name + "_t": t.T for name, t in list(w.items())})
    return w


def _attention_weights(full, small):
    w_in = full["w_in"]
    s0, s1, s2 = Q_RANK, Q_RANK + KV_RANK, Q_RANK + KV_RANK + MLA_ROPE
    uq = full["w_uq"].reshape(Q_RANK, MLA_HEADS, MLA_QK)
    ukv = full["w_ukv"].reshape(KV_RANK, MLA_HEADS, MLA_NOPE + MLA_V)
    w_kr = w_in[:, s1:s2]
    per_tile = ROPE_TILE // MLA_ROPE
    w = _with_transposes({
        "w_cq": w_in[:, :s0], "w_ckv": w_in[:, s0:s1],
        "w_kr4": jnp.tile(w_kr, (1, per_tile)), "w_kr4r": jnp.tile(_rot_cols(w_kr), (1, per_tile)),
        "w_kr8": jnp.tile(w_kr, (1, MLA_HEADS)), "w_kr8r": jnp.tile(_rot_cols(w_kr), (1, MLA_HEADS)),
        "w_sbq": w_in[:, s2:s2 + SB_WIDTH], "w_sbk": w_in[:, s2 + SB_WIDTH:s2 + 2 * SB_WIDTH], "w_sbv": w_in[:, s2 + 2 * SB_WIDTH:],
        "w_qn": uq[..., :MLA_NOPE].reshape(Q_RANK, -1), "w_qr": uq[..., MLA_NOPE:].reshape(Q_RANK, -1),
        "w_qrr": _rot_cols(uq[..., MLA_NOPE:]).reshape(Q_RANK, -1),
        "w_kn": ukv[..., :MLA_NOPE].reshape(KV_RANK, -1), "w_v": ukv[..., MLA_NOPE:].reshape(KV_RANK, -1),
    })
    w.update(g_mix=small["norm_mix"], g_q=small["q_latent_norm"], g_kv=small["kv_latent_norm"], g_a=small["out_norm_mla"],
             g_b=small["out_norm_sb"], g_f=small["norm_ffn"], g_n=small["norm_final"])
    return w


def _ffn_weights(full):
    w = _with_transposes({"w_oa": full["w_o"][:MLA_WIDTH], "w_ob": full["w_o"][MLA_WIDTH:], "w_down": full["w_down"]})
    for name in FLIPPED:
        w[name + "_t"] = full[name]
        w[name] = full[name].T
    return w


def _rope_tables(positions):
    inv_freq = ROPE_THETA ** (-jnp.arange(0, MLA_ROPE, 2, dtype=F32) / MLA_ROPE)
    ang = positions.astype(F32)[:, None] * inv_freq[None, :]
    cos, sin = jnp.cos(ang), jnp.sin(ang)
    return {"cos": jnp.tile(jnp.concatenate([cos, cos], axis=1), (1, MLA_HEADS)),
            "sin": jnp.tile(jnp.concatenate([sin, sin], axis=1), (1, MLA_HEADS))}


def _by_head(g_wide, g_narrow, wide, narrow):
    r = g_wide.shape[0]
    return jnp.concatenate([g_wide.reshape(r, MLA_HEADS, wide), g_narrow.reshape(r, MLA_HEADS, narrow)], axis=-1).reshape(r, -1)


def kernel(x, positions, norm_mix, w_in, q_latent_norm, w_uq, kv_latent_norm, w_ukv, out_norm_mla, out_norm_sb, w_o, norm_ffn, w_gate, w_up, w_down, norm_final, loss_target, m_norm_mix, m_w_in, m_q_latent_norm, m_w_uq, m_kv_latent_norm, m_w_ukv, m_out_norm_mla, m_out_norm_sb, m_w_o, m_norm_ffn, m_w_gate, m_w_up, m_w_down, m_norm_final, v_norm_mix, v_w_in, v_q_latent_norm, v_w_uq, v_kv_latent_norm, v_w_ukv, v_out_norm_mla, v_out_norm_sb, v_w_o, v_norm_ffn, v_w_gate, v_w_up, v_w_down, v_norm_final):
    given = dict(norm_mix=norm_mix, w_in=w_in, q_latent_norm=q_latent_norm, w_uq=w_uq, kv_latent_norm=kv_latent_norm, w_ukv=w_ukv,
                 out_norm_mla=out_norm_mla, out_norm_sb=out_norm_sb, w_o=w_o, norm_ffn=norm_ffn, w_gate=w_gate, w_up=w_up,
                 w_down=w_down, norm_final=norm_final)
    mom_m = dict(norm_mix=m_norm_mix, w_in=m_w_in, q_latent_norm=m_q_latent_norm, w_uq=m_w_uq, kv_latent_norm=m_kv_latent_norm,
                 w_ukv=m_w_ukv, out_norm_mla=m_out_norm_mla, out_norm_sb=m_out_norm_sb, w_o=m_w_o, norm_ffn=m_norm_ffn,
                 w_gate=m_w_gate, w_up=m_w_up, w_down=m_w_down, norm_final=m_norm_final)
    mom_v = dict(norm_mix=v_norm_mix, w_in=v_w_in, q_latent_norm=v_q_latent_norm, w_uq=v_w_uq, kv_latent_norm=v_kv_latent_norm,
                 w_ukv=v_w_ukv, out_norm_mla=v_out_norm_mla, out_norm_sb=v_out_norm_sb, w_o=v_w_o, norm_ffn=v_norm_ffn,
                 w_gate=v_w_gate, w_up=v_w_up, w_down=v_w_down, norm_final=v_norm_final)
    xs = x[0]
    tgt = loss_target[0]
    s = xs.shape[0]
    c_idx = lax.axis_index("c")
    shard_idx = 2 * lax.axis_index("x") + lax.axis_index("y")

    def block2d(t, name):
        t = t.reshape(t.shape[-2:])
        return t.T if name in FLIPPED else t

    shard2d = {name: block2d(given[name], name) for name, *_ in SHARDED}
    is_mine = (jnp.arange(N_SHARD) == shard_idx)[:, None, None]
    local = {name: shard2d[name].astype(BF16) for name, *_ in SHARDED}
    axis_of = {name: 0 if name in FLIPPED else axis for name, _, _, axis in SHARDED}

    def whole(names, gathered):
        return {name: _full_weight(jnp.where(is_mine, local[name][None], t), axis_of[name]) for name, t in zip(names, gathered)}

    small = {name: given[name].reshape(1, n) for name, n in SMALL}
    w = _attention_weights(whole(EARLY, _allgather_list("allgather_w", [local[name] for name in EARLY])), small)
    tabs = _rope_tables(positions[0])
    msuf, mpre = _sb_masks(min(SB_TK, s))

    u, cq, ckv, cqn, ckvn, qn, qr, kn, vm, kr, sq, sk, sv = _fwd_a(xs, tabs, w)
    o_mla, lse, late = _mla_fwd(qn, qr, kn, kr, vm, [local[name] for name in LATE])
    w.update(_ffn_weights(whole(LATE, late)))
    o_sb, cmat = _sb_fwd(sq, sk, sv, msuf)
    merged, h1, f, gate, up, act = _fwd_b1(xs, o_mla, o_sb, w)
    dh2, loss_part, dg_n = _fwd_b2(h1, act, tgt, w)

    def shards_of(names, grads):
        return [_shard_major(grads[name], axis_of[name]) for name in names]

    def reduced(tag, chip_f32, others):
        mine = _add_chips("add_chips_" + tag, chip_f32, others, shard_idx)
        return tuple(mine), tuple(_swap_result("swap_result_" + tag, mine))

    dgate, dup = _bwd_b1(dh2, gate, up, w)
    dh1, do_mla, do_sb, dg_f, dg_a, dg_b = _bwd_b2(dgate, dup, h1, dh2, o_mla, o_sb, w)
    late_gs = shards_of(LATE, {
        "w_o": _tn_matmul("dw_o", merged, dh1), "w_gate": _tn_matmul("dw_gate", dgate, f),
        "w_up": _tn_matmul("dw_up", dup, f), "w_down": _tn_matmul("dw_down", act, dh2)})
    (dsq, dsk, dsv), late_got = _sb_bwd(sq, sk, sv, do_sb, cmat, msuf, mpre, late_gs)
    late_f32, late_bf16 = _add_sibling("add_sibling_late", late_gs, late_got, c_idx)
    (dqn, dqr, dkn, dkr, dvm), late_others = _mla_bwd(qn, qr, kn, kr, vm, o_mla, do_mla, lse, late_bf16)
    mine_late, theirs_late = reduced("late", late_f32, late_others)
    dx, a1, a2, dcq, dckv, dkrc, dkrs, dg_q, dg_kv, dg_mix = _bwd_a(xs, dh1, cq, ckv, dqn, dqr, dkn, dvm, dkr, dsq, dsk, dsv, tabs, w)

    g_cq, g_ckv, g_krc, g_krs, g_sq, g_sk, g_sv = _tn_multi("dw_in", u, [dcq, dckv, dkrc, dkrs, dsq, dsk, dsv])
    g_qn, g_qr1, g_qr2 = _tn_multi("dw_uq", cqn, [dqn, a1, a2])
    g_kn, g_v = _tn_multi("dw_ukv", ckvn, [dkn, dvm])
    slots = lambda g: g.reshape(g.shape[0], MLA_HEADS, MLA_ROPE)
    g_kr = jnp.sum(slots(g_krc), axis=1) + _rot_cols_t(jnp.sum(slots(g_krs), axis=1))
    g_qr = (slots(g_qr1) + _rot_cols_t(slots(g_qr2))).reshape(Q_RANK, -1)
    early_gs = shards_of(EARLY, {
        "w_in": jnp.concatenate([g_cq, g_ckv, g_kr, g_sq, g_sk, g_sv], axis=1),
        "w_uq": _by_head(g_qn, g_qr, MLA_NOPE, MLA_ROPE),
        "w_ukv": _by_head(g_kn * MLA_DK_SCALE, g_v, MLA_NOPE, MLA_V)})
    early_f32, early_bf16 = _add_sibling("add_sibling_early", early_gs, _swap_halves("swap_halves_early", early_gs), c_idx)
    mine_early, theirs_early = reduced("early", early_f32, _chip_scatter(early_bf16))
    halves = dict(zip(EARLY + LATE, zip(mine_early + mine_late, theirs_early + theirs_late)))

    small_parts = jnp.concatenate([dg_mix, dg_q, dg_kv, dg_a, dg_b, dg_f, dg_n, loss_part], axis=1)
    small_sum = _allreduce_small(jnp.broadcast_to(small_parts, (8, small_parts.shape[1])))
    small_g, loss = small_sum[0:1, :-LANES], small_sum[0, -LANES]

    g_out, d_out, m_out, v_out = {}, {}, {}, {}
    for name, *_ in SHARDED:
        shape = given[name].shape
        outs = _adamw_halves("adamw_" + name, shard2d[name], *halves[name], block2d(mom_m[name], name), block2d(mom_v[name], name),
                             c_idx)
        g_out[name], d_out[name], m_out[name], v_out[name] = ((t.T if name in FLIPPED else t).reshape(shape) for t in outs)
    cat = lambda src: jnp.concatenate([src[name].reshape(1, n) for name, n in SMALL], axis=1)
    d, mn, vn = _adamw("adamw_small", cat(given), small_g, cat(mom_m), cat(mom_v))
    off = 0
    for name, n in SMALL:
        shape = given[name].shape
        g_out[name], d_out[name], m_out[name], v_out[name] = (t[:, off:off + n].reshape(shape) for t in (small_g, d, mn, vn))
        off += n

    order = ["norm_mix", "w_in", "q_latent_norm", "w_uq", "kv_latent_norm", "w_ukv", "out_norm_mla", "out_norm_sb", "w_o",
             "norm_ffn", "w_gate", "w_up", "w_down", "norm_final"]
    return (loss, dx[None], *[g_out[n] for n in order], *[d_out[n] for n in order], *[m_out[n] for n in order],
            *[v_out[n] for n in order])
```

```python
import functools
import math

import jax
import jax.numpy as jnp
from jax import lax
from jax.experimental import pallas as pl
from jax.experimental.pallas import tpu as pltpu

F32 = jnp.float32
BF16 = jnp.bfloat16
MESH = pl.DeviceIdType.MESH

D_MODEL = 1024
EPS = 1e-6
MLA_HEADS = 8
MLA_NOPE = 64
MLA_ROPE = 32
MLA_V = 64
MLA_QK = MLA_NOPE + MLA_ROPE
Q_RANK = 256
KV_RANK = 128
ROPE_THETA = 10000.0
SB_HEADS = 8
SB_DIM = 64
MLA_WIDTH = MLA_HEADS * MLA_V
SB_WIDTH = SB_HEADS * SB_DIM
D_FF = 2816
IN_WIDTH = Q_RANK + KV_RANK + MLA_ROPE + 3 * SB_WIDTH

ADAM_LR = 0.001
ADAM_B1 = 0.9
ADAM_B2 = 0.999
ADAM_EPS = 1e-08
ADAM_WD = 0.01
ADAM_STEP = 10

N_SHARD = 4
LANES = 128
ROPE_TILE = LANES
VMEM_LIMIT = 56 * 1024 * 1024
TN_ACC_BYTES = 6 * 1024 * 1024 + 512 * 1024
NEG = -1e30
MLA_SCALE = 1.0 / math.sqrt(MLA_QK)
MLA_DK_SCALE = math.log(2.0)
MLA_QSCALE = MLA_SCALE * math.log2(math.e)
SB_SKIP = 110.0

ROW_TILE = 512
ROW_TILE_ELEMENTWISE = 256
MLA_TQ = 1024
SB_TQ = 512
MLA_TK = 1024
MLA_BWD_TK = 1024
MLA_DIAG_TK = 512
SB_TK = 256
TN_TS = 2048


def _dot(a, b):
    return jnp.dot(a, b, preferred_element_type=F32)


def _dot_nt(a, b):
    return lax.dot_general(a, b, (((1,), (1,)), ((), ())), preferred_element_type=F32)


def _dot_tn(a, b):
    return lax.dot_general(a, b, (((0,), (0,)), ((), ())), preferred_element_type=F32)


def _params(n_grid, vmem=VMEM_LIMIT):
    return pltpu.CompilerParams(dimension_semantics=("arbitrary",) * n_grid, vmem_limit_bytes=vmem)


def _rms(x):
    r = lax.rsqrt(jnp.mean(x * x, axis=-1, keepdims=True) + EPS)
    return x * r, r


def _rms_bwd(n, r, g, dy):
    dn = dy * g
    dx = r * (dn - n * jnp.mean(dn * n, axis=-1, keepdims=True))
    return dx, jnp.sum(dy * n, axis=0, keepdims=True)


def _accumulate(ref, val, step):
    @pl.when(step == 0)
    def _():
        ref[...] = val

    @pl.when(step != 0)
    def _():
        ref[...] += val


def _rowwise(name, body, rows, consts, row_out, acc_out, tm):
    n_rows = rows[0].shape[0]
    tm = min(tm, n_rows)
    nr, nc, no = len(rows), len(consts), len(row_out)

    def kern(*refs):
        body(refs[:nr], refs[nr:nr + nc], refs[nr + nc:nr + nc + no], refs[nr + nc + no:], pl.program_id(0))

    in_specs = [pl.BlockSpec((tm, a.shape[1]), lambda i: (i, 0)) for a in rows]
    in_specs += [pl.BlockSpec(a.shape, lambda i: (0, 0), pipeline_mode=pl.Buffered(1)) for a in consts]
    out_specs = [pl.BlockSpec((tm, s.shape[1]), lambda i: (i, 0)) for s in row_out]
    out_specs += [pl.BlockSpec(s.shape, lambda i: (0, 0)) for s in acc_out]
    return pl.pallas_call(
        kern, name=name, grid=(n_rows // tm,), in_specs=in_specs, out_specs=out_specs,
        out_shape=list(row_out) + list(acc_out), compiler_params=_params(1),
    )(*rows, *consts)


def _sds(shape, dtype):
    return jax.ShapeDtypeStruct(shape, dtype)


def _fwd_a(x, tabs, w):
    s = x.shape[0]

    def body(r, c, o, a, step):
        x_ref, cos_ref, sin_ref = r
        gmix, wcq, wckv, wkr, wkrr, wsq, wsk, wsv, gq, wqn, wqr, wqrr, gkv, wkn, wv = c
        u_o, cq_o, ckv_o, cqn_o, ckvn_o, qn_o, qr_o, kn_o, v_o, kr_o, sq_o, sk_o, sv_o = o
        cos, sin = cos_ref[...], sin_ref[...]
        n, _ = _rms(x_ref[...])
        u = (n * gmix[...]).astype(BF16)
        u_o[...] = u
        cq = _dot(u, wcq[...])
        ckv = _dot(u, wckv[...])
        kr_o[...] = (_dot(u, wkr[...]) * cos[:, :ROPE_TILE] + _dot(u, wkrr[...]) * sin[:, :ROPE_TILE]).astype(BF16)
        sq_o[...] = _dot(u, wsq[...]).astype(BF16)
        sk_o[...] = _dot(u, wsk[...]).astype(BF16)
        sv_o[...] = _dot(u, wsv[...]).astype(BF16)
        cq_o[...] = cq
        ckv_o[...] = ckv
        nq, _ = _rms(cq)
        cqn = (nq * gq[...]).astype(BF16)
        cqn_o[...] = cqn
        qn_o[...] = (_dot(cqn, wqn[...]) * MLA_QSCALE).astype(BF16)
        qr_o[...] = ((_dot(cqn, wqr[...]) * cos + _dot(cqn, wqrr[...]) * sin) * MLA_QSCALE).astype(BF16)
        nkv, _ = _rms(ckv)
        ckvn = (nkv * gkv[...]).astype(BF16)
        ckvn_o[...] = ckvn
        kn_o[...] = _dot(ckvn, wkn[...]).astype(BF16)
        v_o[...] = _dot(ckvn, wv[...]).astype(BF16)

    outs = [
        _sds((s, D_MODEL), BF16), _sds((s, Q_RANK), F32), _sds((s, KV_RANK), F32), _sds((s, Q_RANK), BF16),
        _sds((s, KV_RANK), BF16), _sds((s, MLA_HEADS * MLA_NOPE), BF16), _sds((s, MLA_HEADS * MLA_ROPE), BF16),
        _sds((s, MLA_HEADS * MLA_NOPE), BF16), _sds((s, MLA_WIDTH), BF16), _sds((s, ROPE_TILE), BF16),
        _sds((s, SB_WIDTH), BF16), _sds((s, SB_WIDTH), BF16), _sds((s, SB_WIDTH), BF16),
    ]
    consts = [w["g_mix"], w["w_cq"], w["w_ckv"], w["w_kr4"], w["w_kr4r"], w["w_sbq"], w["w_sbk"], w["w_sbv"], w["g_q"],
              w["w_qn"], w["w_qr"], w["w_qrr"], w["g_kv"], w["w_kn"], w["w_v"]]
    return _rowwise("fwd_a", body, [x, tabs["cos"], tabs["sin"]], consts, outs, [], ROW_TILE)


def _fwd_b1(x, o_mla, o_sb, w):
    s = x.shape[0]

    def body(r, c, o, a, step):
        x_ref, oa_ref, ob_ref = r
        ga, gb, woa, wob, gf, wg, wu = c
        mg_o, h1_o, f_o, gate_o, up_o, act_o = o
        na, _ = _rms(oa_ref[...])
        nb, _ = _rms(ob_ref[...])
        ma = (na * ga[...]).astype(BF16)
        mb = (nb * gb[...]).astype(BF16)
        mg_o[:, :MLA_WIDTH] = ma
        mg_o[:, MLA_WIDTH:] = mb
        h1 = x_ref[...] + _dot(ma, woa[...]) + _dot(mb, wob[...])
        h1_o[...] = h1
        nf, _ = _rms(h1)
        f = (nf * gf[...]).astype(BF16)
        f_o[...] = f
        gate = _dot(f, wg[...])
        up = _dot(f, wu[...])
        gate_o[...] = gate.astype(BF16)
        up_o[...] = up.astype(BF16)
        act_o[...] = (gate * (1.0 / (1.0 + jnp.exp(-gate))) * up).astype(BF16)

    outs = [_sds((s, D_MODEL), BF16), _sds((s, D_MODEL), F32), _sds((s, D_MODEL), BF16), _sds((s, D_FF), BF16),
            _sds((s, D_FF), BF16), _sds((s, D_FF), BF16)]
    consts = [w["g_a"], w["g_b"], w["w_oa"], w["w_ob"], w["g_f"], w["w_gate"], w["w_up"]]
    return _rowwise("fwd_b1", body, [x, o_mla, o_sb], consts, outs, [], ROW_TILE)


def _fwd_b2(h1, act, tgt, w):
    s = h1.shape[0]

    def body(r, c, o, a, step):
        h1_ref, act_ref, t_ref = r
        wd, gn = c
        (dh2_o,) = o
        loss_o, dgn_o = a
        h2 = h1_ref[...] + _dot(act_ref[...], wd[...])
        n2, r2 = _rms(h2)
        err = n2 * gn[...] - t_ref[...]
        part = jnp.sum(jnp.sum(err * err, axis=1, keepdims=True), axis=0, keepdims=True) * (0.5 / D_MODEL)
        _accumulate(loss_o, jnp.broadcast_to(part, (1, LANES)), step)
        dh2, dgn = _rms_bwd(n2, r2, gn[...], err * (1.0 / D_MODEL))
        dh2_o[...] = dh2
        _accumulate(dgn_o, dgn, step)

    return _rowwise("fwd_b2", body, [h1, act, tgt], [w["w_down"], w["g_n"]], [_sds((s, D_MODEL), F32)],
                    [_sds((1, LANES), F32), _sds((1, D_MODEL), F32)], ROW_TILE)


def _bwd_b(dh2, gate, up, h1, o_mla, o_sb, w):
    s = h1.shape[0]

    def body(r, c, o, a, step):
        dh2_ref, gate_ref, up_ref, h1_ref, oa_ref, ob_ref = r
        wdt, wgt, wut, gf, woat, wobt, ga, gb = c
        dgate_o, dup_o, dh1_o, doa_o, dob_o = o
        dgf_o, dga_o, dgb_o = a
        dh2 = dh2_ref[...]
        dact = _dot(dh2.astype(BF16), wdt[...])
        gate = gate_ref[...].astype(F32)
        sig = 1.0 / (1.0 + jnp.exp(-gate))
        dup = (dact * (gate * sig)).astype(BF16)
        dgate = (dact * up_ref[...].astype(F32) * (sig * (1.0 + gate * (1.0 - sig)))).astype(BF16)
        dup_o[...] = dup
        dgate_o[...] = dgate
        df = _dot(dgate, wgt[...]) + _dot(dup, wut[...])
        nf, rf = _rms(h1_ref[...])
        dres, dgf = _rms_bwd(nf, rf, gf[...], df)
        dh1 = dh2 + dres
        dh1_o[...] = dh1
        dh1b = dh1.astype(BF16)
        na, ra = _rms(oa_ref[...])
        doa, dga = _rms_bwd(na, ra, ga[...], _dot(dh1b, woat[...]))
        nb, rb = _rms(ob_ref[...])
        dob, dgb = _rms_bwd(nb, rb, gb[...], _dot(dh1b, wobt[...]))
        doa_o[...] = doa
        dob_o[...] = dob
        _accumulate(dgf_o, dgf, step)
        _accumulate(dga_o, dga, step)
        _accumulate(dgb_o, dgb, step)

    consts = [w["w_down_t"], w["w_gate_t"], w["w_up_t"], w["g_f"], w["w_oa_t"], w["w_ob_t"], w["g_a"], w["g_b"]]
    outs = [_sds((s, D_FF), BF16), _sds((s, D_FF), BF16), _sds((s, D_MODEL), F32), _sds((s, MLA_WIDTH), F32), _sds((s, SB_WIDTH), F32)]
    accs = [_sds((1, D_MODEL), F32), _sds((1, MLA_WIDTH), F32), _sds((1, SB_WIDTH), F32)]
    return _rowwise("bwd_b", body, [dh2, gate, up, h1, o_mla, o_sb], consts, outs, accs, ROW_TILE_ELEMENTWISE)


def _fold_pairs(t):
    return jnp.concatenate([t[:, :LANES] + t[:, LANES:2 * LANES], t[:, 2 * LANES:3 * LANES] + t[:, 3 * LANES:]], axis=1)


def _bwd_a(x, dh1, cq, ckv, dqn, dqr, dkn, dvm, dkr, dsq, dsk, dsv, tabs, w):
    s = x.shape[0]

    def body(r, c, o, a, step):
        x_ref, dh1_ref, cq_ref, ckv_ref, dqn_ref, dqr_ref, dkn_ref, dvm_ref, dkr_ref, dsq_ref, dsk_ref, dsv_ref, cos_ref, sin_ref = r
        wqnt, wqrt, wqrrt, gq, wknt, wvt, gkv, wcqt, wckvt, wkrt, wkrrt, wsqt, wskt, wsvt, gmix = c
        dx_o, a1_o, a2_o, dcq_o, dckv_o, dkrc_o, dkrs_o = o
        dgq_o, dgkv_o, dgmix_o = a
        cos, sin = cos_ref[...], sin_ref[...]
        dqr = _fold_pairs(dqr_ref[...])
        a1 = (dqr * cos).astype(BF16)
        a2 = (dqr * sin).astype(BF16)
        a1_o[...] = a1
        a2_o[...] = a2
        nq, rq = _rms(cq_ref[...])
        dcqn = _dot(dqn_ref[...].astype(BF16), wqnt[...]) + _dot(a1, wqrt[...]) + _dot(a2, wqrrt[...])
        dcq, dgq = _rms_bwd(nq, rq, gq[...], dcqn)
        nkv, rkv = _rms(ckv_ref[...])
        dckvn = _dot((dkn_ref[...] * MLA_DK_SCALE).astype(BF16), wknt[...]) + _dot(dvm_ref[...].astype(BF16), wvt[...])
        dckv, dgkv = _rms_bwd(nkv, rkv, gkv[...], dckvn)
        dkr = _fold_pairs(dkr_ref[...]) * MLA_DK_SCALE
        dcq_b = dcq.astype(BF16)
        dckv_b = dckv.astype(BF16)
        dkrc = (dkr * cos).astype(BF16)
        dkrs = (dkr * sin).astype(BF16)
        dcq_o[...] = dcq_b
        dckv_o[...] = dckv_b
        dkrc_o[...] = dkrc
        dkrs_o[...] = dkrs
        du = (_dot(dcq_b, wcqt[...]) + _dot(dckv_b, wckvt[...]) + _dot(dkrc, wkrt[...]) + _dot(dkrs, wkrrt[...])
              + _dot(dsq_ref[...].astype(BF16), wsqt[...]) + _dot(dsk_ref[...].astype(BF16), wskt[...])
              + _dot(dsv_ref[...].astype(BF16), wsvt[...]))
        nx, rx = _rms(x_ref[...])
        dres, dgmix = _rms_bwd(nx, rx, gmix[...], du)
        dx_o[...] = dh1_ref[...] + dres
        _accumulate(dgq_o, dgq, step)
        _accumulate(dgkv_o, dgkv, step)
        _accumulate(dgmix_o, dgmix, step)

    consts = [w["w_qn_t"], w["w_qr_t"], w["w_qrr_t"], w["g_q"], w["w_kn_t"], w["w_v_t"], w["g_kv"], w["w_cq_t"], w["w_ckv_t"],
              w["w_kr8_t"], w["w_kr8r_t"], w["w_sbq_t"], w["w_sbk_t"], w["w_sbv_t"], w["g_mix"]]
    rope_w = MLA_HEADS * MLA_ROPE
    outs = [_sds((s, D_MODEL), F32), _sds((s, rope_w), BF16), _sds((s, rope_w), BF16), _sds((s, Q_RANK), BF16),
            _sds((s, KV_RANK), BF16), _sds((s, rope_w), BF16), _sds((s, rope_w), BF16)]
    accs = [_sds((1, Q_RANK), F32), _sds((1, KV_RANK), F32), _sds((1, D_MODEL), F32)]
    rows = [x, dh1, cq, ckv, dqn, dqr, dkn, dvm, dkr, dsq, dsk, dsv, tabs["cos"], tabs["sin"]]
    return _rowwise("bwd_a", body, rows, consts, outs, accs, ROW_TILE)


def _tn_multi(name, x, ys):
    s, k = x.shape
    ts = min(TN_TS, s)
    n_y = len(ys)

    def kern(*refs):
        step = pl.program_id(0)
        xb = refs[0][...].astype(BF16)
        for j in range(n_y):
            _accumulate(refs[1 + n_y + j], _dot_tn(xb, refs[1 + j][...].astype(BF16)), step)

    return pl.pallas_call(
        kern, name=name, grid=(s // ts,),
        in_specs=[pl.BlockSpec((ts, k), lambda i: (i, 0))] + [pl.BlockSpec((ts, y.shape[1]), lambda i: (i, 0)) for y in ys],
        out_specs=[pl.BlockSpec((k, y.shape[1]), lambda i: (0, 0)) for y in ys],
        out_shape=[_sds((k, y.shape[1]), F32) for y in ys], compiler_params=_params(1),
    )(x, *ys)


def _tn_tile(k, n):
    if n % LANES or k * n * 4 <= TN_ACC_BYTES:
        return n
    units = n // LANES
    best = 1
    for d in range(1, units + 1):
        if units % d == 0 and k * d * LANES * 4 <= TN_ACC_BYTES:
            best = d
    return best * LANES


def _tn_matmul(name, x, y):
    s, k = x.shape
    n = y.shape[1]
    ts = min(TN_TS, s)
    tn = _tn_tile(k, n)

    def kern(x_ref, y_ref, o_ref):
        step = pl.program_id(1)
        _accumulate(o_ref, _dot_tn(x_ref[...].astype(BF16), y_ref[...].astype(BF16)), step)

    return pl.pallas_call(
        kern, name=name, grid=(n // tn, s // ts),
        in_specs=[pl.BlockSpec((ts, k), lambda j, i: (i, 0)), pl.BlockSpec((ts, tn), lambda j, i: (i, j))],
        out_specs=pl.BlockSpec((k, tn), lambda j, i: (0, j)), out_shape=_sds((k, n), F32), compiler_params=_params(2),
    )(x, y)


def _lanes(rows, lo, width):
    lane = lax.broadcasted_iota(jnp.int32, (rows, LANES), 1)
    return jnp.logical_and(lane >= lo, lane < lo + width)


def _keep(mask, t):
    return jnp.where(mask, t, jnp.zeros_like(t))


def _mla_qcat(qn_ref, qr_ref, rope_lo, half, rows):
    qn = _keep(_lanes(rows, MLA_NOPE * half, MLA_NOPE), qn_ref[...])
    qr = _keep(_lanes(rows, rope_lo, MLA_ROPE), qr_ref[...])
    return jnp.concatenate([qn, qr], axis=1)


def _diag_mask(rows, width, row0, col0):
    row = lax.broadcasted_iota(jnp.int32, (rows, width), 0)
    col = lax.broadcasted_iota(jnp.int32, (rows, width), 1)
    return col + (col0 - row0) <= row


def _mla_fwd(qn, qr, kn, kr, v, riders=(), tq=MLA_TQ, tk=MLA_TK, td=MLA_TQ):
    s = qn.shape[0]
    tq, tk, td = min(tq, s), min(tk, s), min(td, s)
    ratio = tq // tk

    n_ride = len(riders)
    n_pairs = MLA_HEADS // 2

    def kern(qn_ref, qr_ref, kn_ref, kr_ref, v_ref, *rest):
        o_ref, lse_ref = rest[n_ride:n_ride + 2]
        g = pl.program_id(0)
        i = pl.program_id(1)
        if n_ride:
            send, forward, finish = _gather_steps(rest[:n_ride], rest[n_ride + 2:2 * n_ride + 2], *rest[2 * n_ride + 2:])
            pl.when(jnp.logical_and(g == 0, i == 0))(send)
            pl.when(jnp.logical_and(g == 1, i == 0))(forward)
        qcat = [_mla_qcat(qn_ref, qr_ref, MLA_ROPE * (2 * (g % 2) + half), half, tq) for half in range(2)]

        def block(k0, width, carry, row0, masked, half):
            m, l, acc = (c[row0:] for c in carry)
            ks = pl.ds(pl.multiple_of(k0, width), width)
            kcat = jnp.concatenate([kn_ref[ks, :], kr_ref[ks, :]], axis=1)
            sc = _dot_nt(qcat[half][row0:], kcat)
            if masked:
                sc = jnp.where(_diag_mask(tq - row0, width, row0, row0), sc, NEG)
            m_new = jnp.maximum(m, jnp.max(sc, axis=1, keepdims=True))
            p = jnp.exp2(sc - m_new)
            alpha = jnp.exp2(m - m_new)
            l = alpha * l + jnp.sum(p, axis=1, keepdims=True)
            acc = alpha * acc + _dot(p.astype(BF16), v_ref[ks, :])
            new = (m_new, l, acc)
            return new if row0 == 0 else tuple(jnp.concatenate([c[:row0], n], axis=0) for c, n in zip(carry, new))

        def both(k0, width, carries, row0, masked):
            return tuple(block(k0, width, carries[half], row0, masked, half) for half in range(2))

        init = (jnp.full((tq, 1), NEG, F32), jnp.zeros((tq, 1), F32), jnp.zeros((tq, LANES), F32))
        carries = lax.fori_loop(0, i * ratio, lambda kb, c: both(kb * tk, tk, c, 0, False), (init, init))
        for row0 in range(0, tq, td):
            carries = both(i * tq + row0, td, carries, row0, True)
        for half in range(2):
            m, l, acc = carries[half]
            out = _keep(_lanes(tq, MLA_V * half, MLA_V), acc / l)
            lse = _keep(_lanes(tq, MLA_ROPE * half, MLA_ROPE), jnp.broadcast_to(m + jnp.log2(l), (tq, LANES)))
            if half == 0:
                o_ref[...] = out
                lse_ref[...] = lse
            else:
                o_ref[...] += out
                lse_ref[...] += lse
        if n_ride:
            pl.when(jnp.logical_and(g == n_pairs - 1, i == s // tq - 1))(finish)

    qblk = pl.BlockSpec((tq, LANES), lambda g, i: (i, g))
    full = pl.BlockSpec((s, LANES), lambda g, i: (0, g))
    outs = pl.pallas_call(
        kern, name="mla_fwd", grid=(n_pairs, s // tq),
        in_specs=[qblk, pl.BlockSpec((tq, LANES), lambda g, i: (i, g // 2)), full, pl.BlockSpec((s, LANES), lambda g, i: (0, 0)), full]
        + [HBM_SPEC] * n_ride,
        out_specs=[qblk, qblk] + [HBM_SPEC] * n_ride,
        out_shape=[_sds((s, MLA_WIDTH), F32), _sds((s, n_pairs * LANES), F32)] + [_sds((N_SHARD,) + a.shape, a.dtype) for a in riders],
        scratch_shapes=_gather_sems(n_ride) if n_ride else [], compiler_params=_params(2),
    )(qn, qr, kn, kr, v, *riders)
    return outs[0], outs[1], outs[2:]


def _mla_bwd(qn, qr, kn, kr, v, o, do, lse, riders=(), tq=MLA_TQ, tk=MLA_BWD_TK, td=MLA_DIAG_TK):
    s = qn.shape[0]
    tq, tk, td = min(tq, s), min(tk, s), min(td, s)
    ratio = tq // tk

    n_ride = len(riders)
    n_pairs = MLA_HEADS // 2

    def kern(qn_ref, qr_ref, kn_ref, kr_ref, v_ref, o_ref, do_ref, lse_ref, *rest):
        dqn_ref, dqr_ref, dkn_ref, dkr_ref, dv_ref = rest[n_ride:n_ride + 5]
        g = pl.program_id(0)
        i = pl.program_id(1)
        if n_ride:
            start, finish = _scatter_steps(rest[:n_ride], rest[n_ride + 5:2 * n_ride + 5], *rest[2 * n_ride + 5:])
            pl.when(jnp.logical_and(g == 0, i == 0))(start)

        @pl.when(i == 0)
        def _():
            dkn_ref[...] = jnp.zeros_like(dkn_ref)
            dkr_ref[...] = jnp.zeros_like(dkr_ref)
            dv_ref[...] = jnp.zeros_like(dv_ref)

        for half in range(2):
            rope_lo = MLA_ROPE * (2 * (g % 2) + half)
            qcat = _mla_qcat(qn_ref, qr_ref, rope_lo, half, tq)
            mine = _lanes(tq, MLA_V * half, MLA_V)
            do_f = _keep(mine, do_ref[...])
            do_b = do_f.astype(BF16)
            delta = jnp.sum(do_f * o_ref[...], axis=1, keepdims=True)
            lse_v = lse_ref[:, MLA_ROPE * half:MLA_ROPE * half + 1]

            def block(k0, width, dq_acc, row0, masked, qcat=qcat, do_b=do_b, delta=delta, lse_v=lse_v):
                ks = pl.ds(pl.multiple_of(k0, width), width)
                kcat = jnp.concatenate([kn_ref[ks, :], kr_ref[ks, :]], axis=1)
                qc, dob = qcat[row0:], do_b[row0:]
                p = jnp.exp2(_dot_nt(qc, kcat) - lse_v[row0:])
                if masked:
                    p = jnp.where(_diag_mask(tq - row0, width, row0, row0), p, 0.0)
                ds = (p * (_dot_nt(dob, v_ref[ks, :]) - delta[row0:])).astype(BF16)
                dv_ref[ks, :] += _dot_tn(p.astype(BF16), dob)
                dkc = _dot_tn(ds, qc)
                dkn_ref[ks, :] += dkc[:, :LANES]
                dkr_ref[ks, :] += dkc[:, LANES:]
                new = dq_acc[row0:] + _dot(ds, kcat)
                return new if row0 == 0 else jnp.concatenate([dq_acc[:row0], new], axis=0)

            acc = lax.fori_loop(0, i * ratio, lambda kb, c, block=block: block(kb * tk, tk, c, 0, False),
                                jnp.zeros((tq, 2 * LANES), F32))
            for row0 in range(0, tq, td):
                acc = block(i * tq + row0, td, acc, row0, True)
            dqn = _keep(_lanes(tq, MLA_NOPE * half, MLA_NOPE), acc[:, :LANES] * MLA_SCALE)
            dqr = _keep(_lanes(tq, rope_lo, MLA_ROPE), acc[:, LANES:] * MLA_SCALE)
            if half == 0:
                dqn_ref[...] = dqn
                dqr_ref[...] = dqr
            else:
                dqn_ref[...] += dqn
                dqr_ref[...] += dqr
        if n_ride:
            pl.when(jnp.logical_and(g == n_pairs - 1, i == s // tq - 1))(finish)

    qblk = pl.BlockSpec((tq, LANES), lambda g, i: (i, g))
    full = pl.BlockSpec((s, LANES), lambda g, i: (0, g))
    once = lambda spec_map: pl.BlockSpec((s, LANES), spec_map, pipeline_mode=pl.Buffered(1))
    wide = _sds((s, n_pairs * LANES), F32)
    outs = pl.pallas_call(
        kern, name="mla_bwd", grid=(n_pairs, s // tq),
        in_specs=[qblk, pl.BlockSpec((tq, LANES), lambda g, i: (i, g // 2)), once(lambda g, i: (0, g)), once(lambda g, i: (0, 0)),
                  once(lambda g, i: (0, g)), qblk, qblk, qblk] + [HBM_SPEC] * n_ride,
        out_specs=[qblk, qblk, full, full, full] + [HBM_SPEC] * n_ride,
        out_shape=[wide] * 5 + [_sds((N_SHARD - 1,) + p.shape[1:], p.dtype) for p in riders],
        scratch_shapes=_scatter_sems(n_ride) if n_ride else [], compiler_params=_params(2),
    )(qn, qr, kn, kr, v, o, do, lse, *riders)
    return outs[:5], outs[5:]


def _sb_masks(tk):
    j = lax.broadcasted_iota(jnp.int32, (tk, tk), 0)
    c = lax.broadcasted_iota(jnp.int32, (tk, tk), 1)
    return (j > c).astype(BF16), (j < c).astype(BF16)


def _sb_scores(qs, kk, msuf, strict):
    z = _dot_nt(qs, kk)
    lom = -(jnp.maximum(z, 0.0) + jnp.log(1.0 + jnp.exp(-jnp.abs(z))))
    if strict is not None:
        lom = jnp.where(strict, lom, 0.0)
    return z, lom, _dot(lom.astype(BF16), msuf)


def _sb_strict(tq, tk, d):
    row = lax.broadcasted_iota(jnp.int32, (tq, tk), 0)
    col = lax.broadcasted_iota(jnp.int32, (tq, tk), 1)
    return col + d * tk < row


def _sb_fwd(q, k, v, msuf, tq=SB_TQ, tk=SB_TK):
    s = q.shape[0]
    tq, tk = min(tq, s), min(tk, s)
    ratio = tq // tk

    def kern(q_ref, k_ref, v_ref, m_ref, o_ref, c_ref):
        i = pl.program_id(1)
        msf = m_ref[...]
        lane = lax.broadcasted_iota(jnp.int32, (tq, LANES), 1)
        mine = [_lanes(tq, SB_DIM * half, SB_DIM) for half in range(2)]
        qs = [_keep(m, q_ref[...]) * 0.125 for m in mine]

        def block(kb, carry, dd, half):
            c, acc, cm = carry
            ks = pl.ds(pl.multiple_of(kb * tk, tk), tk)
            strict = None if dd is None else _sb_strict(tq, tk, dd)
            z, lom, suf = _sb_scores(qs[half], k_ref[ks, :], msf, strict)
            a = jnp.exp(z + lom + (suf + c))
            if strict is not None:
                a = jnp.where(strict, a, 0.0)
            acc = acc + _dot(a.astype(BF16), v_ref[ks, :])
            cm = jnp.where(lane == kb, c, cm)
            return c + jnp.sum(lom, axis=1, keepdims=True), acc, cm

        init = (jnp.zeros((tq, 1), F32), jnp.zeros((tq, LANES), F32), jnp.full((tq, LANES), NEG, F32))
        carries = [init, init]
        for dd in range(ratio - 1, -1, -1):
            carries = [block(i * ratio + dd, carries[half], dd, half) for half in range(2)]

        def live(st):
            return jnp.logical_and(st[0] >= 0, jnp.maximum(jnp.max(st[1][0]), jnp.max(st[2][0])) > -SB_SKIP)

        def step(st):
            return (st[0] - 1, block(st[0], st[1], None, 0), block(st[0], st[2], None, 1))

        _, done0, done1 = lax.while_loop(live, step, (i * ratio - 1, carries[0], carries[1]))
        o_ref[...] = _keep(mine[0], done0[1]) + _keep(mine[1], done1[1])
        c_ref[:, :LANES] = done0[2]
        c_ref[:, LANES:] = done1[2]

    qblk = lambda n: pl.BlockSpec((tq, n), lambda g, i: (i, g))
    full = pl.BlockSpec((s, LANES), lambda g, i: (0, g))
    return pl.pallas_call(
        kern, name="sb_fwd", grid=(SB_HEADS // 2, s // tq),
        in_specs=[qblk(LANES), full, full, pl.BlockSpec((tk, tk), lambda g, i: (0, 0))],
        out_specs=[qblk(LANES), qblk(2 * LANES)],
        out_shape=[_sds((s, SB_WIDTH), F32), _sds((s, SB_HEADS * LANES), F32)], compiler_params=_params(2),
    )(q, k, v, msuf)


def _sb_bwd(q, k, v, do, cmat, msuf, mpre, riders=(), tq=SB_TQ, tk=SB_TK):
    s = q.shape[0]
    tq, tk = min(tq, s), min(tk, s)
    ratio = tq // tk

    n_ride = len(riders)
    n_pairs = SB_HEADS // 2

    def kern(q_ref, k_ref, v_ref, do_ref, c_ref, ms_ref, mp_ref, *rest):
        dq_ref, dk_ref, dv_ref = rest[n_ride:n_ride + 3]
        i = pl.program_id(1)
        if n_ride:
            start, finish = _swap_steps(rest[:n_ride], rest[n_ride + 3:2 * n_ride + 3], *rest[2 * n_ride + 3:])
            pl.when(jnp.logical_and(pl.program_id(0) == 0, i == 0))(start)

        @pl.when(i == 0)
        def _():
            dk_ref[...] = jnp.zeros_like(dk_ref)
            dv_ref[...] = jnp.zeros_like(dv_ref)

        msf = ms_ref[...]
        mpf = mp_ref[...]
        lane = lax.broadcasted_iota(jnp.int32, (tq, LANES), 1)
        lane1 = lax.broadcasted_iota(jnp.int32, (1, LANES), 1)
        mine = [_lanes(tq, SB_DIM * half, SB_DIM) for half in range(2)]
        qv = [_keep(m, q_ref[...]) for m in mine]
        qs = [t * 0.125 for t in qv]
        do_b = [_keep(m, do_ref[...]).astype(BF16) for m in mine]
        cm = [c_ref[:, :LANES], c_ref[:, LANES:]]

        def block(kb, carry, dd, half):
            dq_acc, pc = carry
            ks = pl.ds(pl.multiple_of(kb * tk, tk), tk)
            kk = k_ref[ks, :]
            strict = None if dd is None else _sb_strict(tq, tk, dd)
            z, lom, suf = _sb_scores(qs[half], kk, msf, strict)
            c = jnp.sum(jnp.where(lane == kb, cm[half], 0.0), axis=1, keepdims=True)
            a = jnp.exp(z + lom + (suf + c))
            if strict is not None:
                a = jnp.where(strict, a, 0.0)
            g = _dot_nt(do_b[half], v_ref[ks, :]) * a
            p = pc + _dot(g.astype(BF16), mpf)
            omb = jnp.exp(lom)
            dz = (g * omb - (1.0 - omb) * p) * 0.125
            if strict is not None:
                dz = jnp.where(strict, dz, 0.0)
            dz = dz.astype(BF16)
            dv_ref[ks, :] += _dot_tn(a.astype(BF16), do_b[half])
            dk_ref[ks, :] += _dot_tn(dz, qv[half])
            return dq_acc + _dot(dz, kk), pc + jnp.sum(g, axis=1, keepdims=True)

        def needed(cm_h):
            seen = jnp.logical_and(jnp.max(cm_h, axis=0, keepdims=True) > -SB_SKIP, lane1 < i * ratio)
            return jnp.sum(seen.astype(jnp.int32))

        first = i * ratio - jnp.maximum(needed(cm[0]), needed(cm[1]))
        init = (jnp.zeros((tq, LANES), F32), jnp.zeros((tq, 1), F32))
        carries = lax.fori_loop(first, i * ratio, lambda kb, c: (block(kb, c[0], None, 0), block(kb, c[1], None, 1)), (init, init))
        for dd in range(ratio):
            carries = [block(i * ratio + dd, carries[half], dd, half) for half in range(2)]
        dq_ref[...] = _keep(mine[0], carries[0][0]) + _keep(mine[1], carries[1][0])
        if n_ride:
            pl.when(jnp.logical_and(pl.program_id(0) == n_pairs - 1, i == s // tq - 1))(finish)

    qblk = lambda n: pl.BlockSpec((tq, n), lambda g, i: (i, g))
    full = pl.BlockSpec((s, LANES), lambda g, i: (0, g))
    msk = pl.BlockSpec((tk, tk), lambda g, i: (0, 0))
    outs = pl.pallas_call(
        kern, name="sb_bwd", grid=(n_pairs, s // tq),
        in_specs=[qblk(LANES), full, full, qblk(LANES), qblk(2 * LANES), msk, msk] + [HBM_SPEC] * n_ride,
        out_specs=[qblk(LANES), full, full] + [HBM_SPEC] * n_ride,
        out_shape=[_sds((s, SB_WIDTH), F32)] * 3 + _halves_shapes(riders),
        scratch_shapes=_swap_sems(n_ride) if n_ride else [], compiler_params=_params(2),
    )(q, k, v, do, cmat, msuf, mpre, *riders)
    return outs[:3], outs[3:]


def _place():
    return lax.axis_index("x"), lax.axis_index("y"), lax.axis_index("c")


def _other_chips(x, y):
    return [(1 - x, y), (x, 1 - y), (1 - x, 1 - y)]


HBM_SPEC = pl.BlockSpec(memory_space=pl.ANY)


def _gather_steps(ins, outs, send_sems, recv_sems):
    n = len(ins)
    x, y, c = _place()
    sibling = (x, y, 1 - c)
    chips = _other_chips(x, y)

    def half_of(a, ref, pc):
        half = ins[a].shape[0] // 2
        return ref.at[pl.ds(pl.multiple_of(pc * half, 16), half), :]

    def copy(a, k, chip, pc, to, src=None):
        dst = half_of(a, outs[a].at[2 * chip[0] + chip[1]], pc)
        return pltpu.make_async_remote_copy(src_ref=dst if src is None else src, dst_ref=dst, send_sem=send_sems.at[7 * a + k],
                                            recv_sem=recv_sems.at[7 * a + k], device_id=to, device_id_type=MESH)

    def own(a):
        return pltpu.make_async_remote_copy(src_ref=ins[a], dst_ref=outs[a].at[2 * x + y], send_sem=send_sems.at[7 * a + 6],
                                            recv_sem=recv_sems.at[7 * a + 6], device_id=sibling, device_id_type=MESH)

    def first():
        far = [copy(a, j, (x, y), c, (*chip, c), src=half_of(a, ins[a], c)) for a in range(n) for j, chip in enumerate(chips)]
        return far + [own(a) for a in range(n)]

    def passed():
        return [copy(a, 3 + j, chip, c, sibling) for j, chip in enumerate(chips) for a in range(n)]

    def send():
        for cp in first():
            cp.start()

    def forward():
        for j, chip in enumerate(chips):
            for a in range(n):
                copy(a, j, chip, c, sibling).wait_recv()
        for cp in passed():
            cp.start()

    def finish():
        for j, chip in enumerate(chips):
            for a in range(n):
                copy(a, 3 + j, chip, 1 - c, sibling).wait_recv()
        for a in range(n):
            own(a).wait_recv()
        for cp in first() + passed():
            cp.wait_send()

    return send, forward, finish


def _gather_sems(n):
    return [pltpu.SemaphoreType.DMA((7 * n,)), pltpu.SemaphoreType.DMA((7 * n,))]


def _allgather_list(name, shards):
    n = len(shards)

    def body(*refs):
        for stage in _gather_steps(refs[:n], refs[n:2 * n], *refs[2 * n:]):
            stage()

    return pl.pallas_call(
        body, name=name, out_shape=[_sds((N_SHARD,) + a.shape, a.dtype) for a in shards], in_specs=[HBM_SPEC] * n,
        out_specs=[HBM_SPEC] * n, scratch_shapes=_gather_sems(n),
    )(*shards)


def _swap_steps(ins, outs, send_sems, recv_sems):
    x, y, c = _place()

    def copies():
        out = []
        for a in range(len(ins)):
            h = ins[a].shape[1] // 2
            src = ins[a].at[:, pl.ds(pl.multiple_of((1 - c) * h, 8), h), :]
            out.append(pltpu.make_async_remote_copy(src_ref=src, dst_ref=outs[a], send_sem=send_sems.at[a], recv_sem=recv_sems.at[a],
                                                    device_id=(x, y, 1 - c), device_id_type=MESH))
        return out

    def start():
        for cp in copies():
            cp.start()

    def finish():
        for cp in copies():
            cp.wait()

    return start, finish


def _swap_sems(n):
    return [pltpu.SemaphoreType.DMA((n,)), pltpu.SemaphoreType.DMA((n,))]


def _halves_shapes(gs):
    return [_sds((N_SHARD, g.shape[1] // 2, g.shape[2]), g.dtype) for g in gs]


def _swap_halves(name, gs):
    n = len(gs)

    def body(*refs):
        for stage in _swap_steps(refs[:n], refs[n:2 * n], *refs[2 * n:]):
            stage()

    return pl.pallas_call(body, name=name, out_shape=_halves_shapes(gs), in_specs=[HBM_SPEC] * n, out_specs=[HBM_SPEC] * n,
                          scratch_shapes=_swap_sems(n))(*gs)


def _add_sibling(name, gs, gots, c_idx):
    n = len(gs)

    def kern(c_ref, *refs):
        for a in range(n):
            tot = refs[a][...] + refs[n + a][...]
            refs[2 * n + a][...] = tot
            refs[3 * n + a][...] = tot.astype(BF16)

    quarter = lambda g: (None, g.shape[1] // 4, g.shape[2])
    in_specs = [pl.BlockSpec(quarter(g), lambda b, s, c_ref: (b, 2 * c_ref[0] + s, 0)) for g in gs]
    in_specs += [pl.BlockSpec(quarter(g), lambda b, s, c_ref: (b, s, 0)) for g in gs]
    out_specs = [pl.BlockSpec(quarter(g), lambda b, s, c_ref: (b, s, 0)) for g in gs] * 2
    out_shape = [_sds(t.shape, F32) for t in gots] + [_sds(t.shape, BF16) for t in gots]
    outs = pl.pallas_call(
        kern, name=name, out_shape=out_shape,
        grid_spec=pltpu.PrefetchScalarGridSpec(num_scalar_prefetch=1, grid=(N_SHARD, 2), in_specs=in_specs, out_specs=out_specs),
        compiler_params=_params(2),
    )(c_idx.reshape(1), *gs, *gots)
    return outs[:n], outs[n:]


def _scatter_steps(ins, outs, send_sems, recv_sems):
    x, y, c = _place()

    def copies():
        return [pltpu.make_async_remote_copy(
            src_ref=ins[a].at[2 * px + py], dst_ref=outs[a].at[j], send_sem=send_sems.at[3 * a + j], recv_sem=recv_sems.at[3 * a + j],
            device_id=(px, py, c), device_id_type=MESH) for a in range(len(ins)) for j, (px, py) in enumerate(_other_chips(x, y))]

    def start():
        for cp in copies():
            cp.start()

    def finish():
        for cp in copies():
            cp.wait()

    return start, finish


def _scatter_sems(n):
    return [pltpu.SemaphoreType.DMA((3 * n,)), pltpu.SemaphoreType.DMA((3 * n,))]


def _chip_scatter(ps):
    n = len(ps)

    def body(*refs):
        for stage in _scatter_steps(refs[:n], refs[n:2 * n], *refs[2 * n:]):
            stage()

    return pl.pallas_call(
        body, name="chip_scatter", out_shape=[_sds((N_SHARD - 1,) + p.shape[1:], p.dtype) for p in ps], in_specs=[HBM_SPEC] * n,
        out_specs=[HBM_SPEC] * n, scratch_shapes=_scatter_sems(n),
    )(*ps)


def _add_chips(name, ps, others, shard_idx):
    n = len(ps)

    def kern(b_ref, *refs):
        for a in range(n):
            tot = refs[a][...]
            for j in range(N_SHARD - 1):
                tot = tot + refs[n + a][j].astype(F32)
            refs[2 * n + a][...] = tot

    in_specs = [pl.BlockSpec((None, p.shape[1] // 2, p.shape[2]), lambda s, b_ref: (b_ref[0], s, 0)) for p in ps]
    in_specs += [pl.BlockSpec((N_SHARD - 1, p.shape[1] // 2, p.shape[2]), lambda s, b_ref: (0, s, 0)) for p in ps]
    out_specs = [pl.BlockSpec((p.shape[1] // 2, p.shape[2]), lambda s, b_ref: (s, 0)) for p in ps]
    return pl.pallas_call(
        kern, name=name, out_shape=[_sds(p.shape[1:], F32) for p in ps],
        grid_spec=pltpu.PrefetchScalarGridSpec(num_scalar_prefetch=1, grid=(2,), in_specs=in_specs, out_specs=out_specs),
        compiler_params=_params(1),
    )(shard_idx.reshape(1), *ps, *others)


def _swap_result(name, mines):
    n = len(mines)

    def body(*refs):
        ins, outs = refs[:n], refs[n:2 * n]
        send_sems, recv_sems = refs[2 * n:]
        x, y, c = _place()
        copies = [pltpu.make_async_remote_copy(src_ref=ins[a], dst_ref=outs[a], send_sem=send_sems.at[a], recv_sem=recv_sems.at[a],
                                               device_id=(x, y, 1 - c), device_id_type=MESH) for a in range(n)]
        for cp in copies:
            cp.start()
        for cp in copies:
            cp.wait()

    return pl.pallas_call(
        body, name=name, out_shape=[_sds(m.shape, m.dtype) for m in mines], in_specs=[HBM_SPEC] * n,
        out_specs=[HBM_SPEC] * n, scratch_shapes=[pltpu.SemaphoreType.DMA((n,)), pltpu.SemaphoreType.DMA((n,))],
    )(*mines)


def _allreduce_small(v):
    m_per, n = v.shape

    def body(x_ref, tot_ref, all_ref, send_sems, recv_sems, local_sem):
        x, y, c = _place()
        me, sibling = (x, y, c), (x, y, 1 - c)
        chips = _other_chips(x, y)

        def rows(px, py, pc):
            return all_ref.at[pl.ds(pl.multiple_of((4 * px + 2 * py + pc) * m_per, 8), m_per), :]

        def copy(k, block, to, src=None):
            return pltpu.make_async_remote_copy(
                src_ref=rows(*block) if src is None else src, dst_ref=rows(*block), send_sem=send_sems.at[k],
                recv_sem=recv_sems.at[k], device_id=to, device_id_type=MESH)

        mine = pltpu.make_async_copy(x_ref, rows(*me), local_sem)
        mine.start()
        first = [copy(0, me, sibling, src=x_ref)] + [copy(1 + j, me, (*chip, c), src=x_ref) for j, chip in enumerate(chips)]
        for cp in first:
            cp.start()
        passed = [copy(4 + j, (*chip, c), sibling) for j, chip in enumerate(chips)]
        for j, chip in enumerate(chips):
            copy(1 + j, (*chip, c), me).wait_recv()
            passed[j].start()
        copy(0, sibling, me).wait_recv()
        for j, chip in enumerate(chips):
            copy(4 + j, (*chip, 1 - c), me).wait_recv()
        for cp in first + passed:
            cp.wait_send()
        mine.wait()
        tot = all_ref[0:m_per, :]
        for dev in range(1, 8):
            tot = tot + all_ref[dev * m_per:(dev + 1) * m_per, :]
        tot_ref[...] = tot

    vmem = pl.BlockSpec(memory_space=pltpu.VMEM)
    return pl.pallas_call(
        body, name="allreduce_small", out_shape=_sds((m_per, n), F32), in_specs=[vmem], out_specs=vmem,
        scratch_shapes=[pltpu.VMEM((8 * m_per, n), F32), pltpu.SemaphoreType.DMA((7,)), pltpu.SemaphoreType.DMA((7,)),
                        pltpu.SemaphoreType.DMA],
    )(v)


def _adam_update(w, g, m, v):
    m_new = ADAM_B1 * m + (1.0 - ADAM_B1) * g
    v_new = ADAM_B2 * v + (1.0 - ADAM_B2) * (g * g)
    m_hat = m_new / (1.0 - ADAM_B1 ** ADAM_STEP)
    v_hat = v_new / (1.0 - ADAM_B2 ** ADAM_STEP)
    return -ADAM_LR * (m_hat / (jnp.sqrt(v_hat) + ADAM_EPS) + ADAM_WD * w), m_new, v_new


def _adamw(name, w, g, m, v):
    rows, width = w.shape
    tr = rows // 4 if rows % 32 == 0 else rows

    def kern(w_ref, g_ref, m_ref, v_ref, d_ref, mo_ref, vo_ref):
        d_ref[...], mo_ref[...], vo_ref[...] = _adam_update(w_ref[...], g_ref[...], m_ref[...], v_ref[...])

    spec = pl.BlockSpec((tr, width), lambda i: (i, 0))
    return pl.pallas_call(kern, name=name, grid=(rows // tr,), in_specs=[spec] * 4, out_specs=[spec] * 3,
                          out_shape=[_sds((rows, width), F32)] * 3, compiler_params=_params(1))(w, g, m, v)


def _adamw_halves(name, w, mine, theirs, m, v, c_idx):
    rows, width = w.shape
    tr = rows // 4

    def kern(c_ref, w_ref, mine_ref, theirs_ref, m_ref, v_ref, g_ref, d_ref, mo_ref, vo_ref):
        g = jnp.where(pl.program_id(0) == c_ref[0], mine_ref[...], theirs_ref[...])
        g_ref[...] = g
        d_ref[...], mo_ref[...], vo_ref[...] = _adam_update(w_ref[...], g, m_ref[...], v_ref[...])

    whole = pl.BlockSpec((tr, width), lambda h, j, c_ref: (2 * h + j, 0))
    part = pl.BlockSpec((tr, width), lambda h, j, c_ref: (j, 0))
    return pl.pallas_call(
        kern, name=name, out_shape=[_sds((rows, width), F32)] * 4,
        grid_spec=pltpu.PrefetchScalarGridSpec(num_scalar_prefetch=1, grid=(2, 2), in_specs=[whole, part, part, whole, whole],
                                               out_specs=[whole] * 4),
        compiler_params=_params(2),
    )(c_idx.reshape(1), w, mine, theirs, m, v)


SHARDED = (("w_in", D_MODEL, IN_WIDTH, 1), ("w_uq", Q_RANK, MLA_HEADS * MLA_QK, 1),
           ("w_ukv", KV_RANK, MLA_HEADS * (MLA_NOPE + MLA_V), 1), ("w_o", D_MODEL, D_MODEL, 0),
           ("w_gate", D_MODEL, D_FF, 1), ("w_up", D_MODEL, D_FF, 1), ("w_down", D_FF, D_MODEL, 0))
EARLY = ("w_in", "w_uq", "w_ukv")
LATE = ("w_o", "w_gate", "w_up", "w_down")
FLIPPED = ("w_gate", "w_up")
SMALL = (("norm_mix", D_MODEL), ("q_latent_norm", Q_RANK), ("kv_latent_norm", KV_RANK), ("out_norm_mla", MLA_WIDTH),
         ("out_norm_sb", SB_WIDTH), ("norm_ffn", D_MODEL), ("norm_final", D_MODEL))


def _full_weight(gathered, axis):
    n_sh, k, n = gathered.shape
    return gathered.transpose(1, 0, 2).reshape(k, n_sh * n) if axis == 1 else gathered.reshape(n_sh * k, n)


def _shard_major(g, axis):
    r, c = g.shape
    return g.reshape(r, N_SHARD, c // N_SHARD).transpose(1, 0, 2) if axis == 1 else g.reshape(N_SHARD, r // N_SHARD, c)


def _rot_cols(w):
    hh = MLA_ROPE // 2
    return jnp.concatenate([-w[..., hh:], w[..., :hh]], axis=-1)


def _rot_cols_t(g):
    hh = MLA_ROPE // 2
    return jnp.concatenate([g[..., hh:], -g[..., :hh]], axis=-1)


def _with_transposes(w):
    w.update({name + "_t": t.T for name, t in list(w.items())})
    return w


def _attention_weights(full, small):
    w_in = full["w_in"]
    s0, s1, s2 = Q_RANK, Q_RANK + KV_RANK, Q_RANK + KV_RANK + MLA_ROPE
    uq = full["w_uq"].reshape(Q_RANK, MLA_HEADS, MLA_QK)
    ukv = full["w_ukv"].reshape(KV_RANK, MLA_HEADS, MLA_NOPE + MLA_V)
    w_kr = w_in[:, s1:s2]
    per_tile = ROPE_TILE // MLA_ROPE
    w = _with_transposes({
        "w_cq": w_in[:, :s0], "w_ckv": w_in[:, s0:s1],
        "w_kr4": jnp.tile(w_kr, (1, per_tile)), "w_kr4r": jnp.tile(_rot_cols(w_kr), (1, per_tile)),
        "w_kr8": jnp.tile(w_kr, (1, MLA_HEADS)), "w_kr8r": jnp.tile(_rot_cols(w_kr), (1, MLA_HEADS)),
        "w_sbq": w_in[:, s2:s2 + SB_WIDTH], "w_sbk": w_in[:, s2 + SB_WIDTH:s2 + 2 * SB_WIDTH], "w_sbv": w_in[:, s2 + 2 * SB_WIDTH:],
        "w_qn": uq[..., :MLA_NOPE].reshape(Q_RANK, -1), "w_qr": uq[..., MLA_NOPE:].reshape(Q_RANK, -1),
        "w_qrr": _rot_cols(uq[..., MLA_NOPE:]).reshape(Q_RANK, -1),
        "w_kn": ukv[..., :MLA_NOPE].reshape(KV_RANK, -1), "w_v": ukv[..., MLA_NOPE:].reshape(KV_RANK, -1),
    })
    w.update(g_mix=small["norm_mix"], g_q=small["q_latent_norm"], g_kv=small["kv_latent_norm"], g_a=small["out_norm_mla"],
             g_b=small["out_norm_sb"], g_f=small["norm_ffn"], g_n=small["norm_final"])
    return w


def _ffn_weights(full):
    w = _with_transposes({"w_oa": full["w_o"][:MLA_WIDTH], "w_ob": full["w_o"][MLA_WIDTH:], "w_down": full["w_down"]})
    for name in FLIPPED:
        w[name + "_t"] = full[name]
        w[name] = full[name].T
    return w


def _rope_tables(positions):
    inv_freq = ROPE_THETA ** (-jnp.arange(0, MLA_ROPE, 2, dtype=F32) / MLA_ROPE)
    ang = positions.astype(F32)[:, None] * inv_freq[None, :]
    cos, sin = jnp.cos(ang), jnp.sin(ang)
    return {"cos": jnp.tile(jnp.concatenate([cos, cos], axis=1), (1, MLA_HEADS)),
            "sin": jnp.tile(jnp.concatenate([sin, sin], axis=1), (1, MLA_HEADS))}


def _by_head(g_wide, g_narrow, wide, narrow):
    r = g_wide.shape[0]
    return jnp.concatenate([g_wide.reshape(r, MLA_HEADS, wide), g_narrow.reshape(r, MLA_HEADS, narrow)], axis=-1).reshape(r, -1)


def kernel(x, positions, norm_mix, w_in, q_latent_norm, w_uq, kv_latent_norm, w_ukv, out_norm_mla, out_norm_sb, w_o, norm_ffn, w_gate, w_up, w_down, norm_final, loss_target, m_norm_mix, m_w_in, m_q_latent_norm, m_w_uq, m_kv_latent_norm, m_w_ukv, m_out_norm_mla, m_out_norm_sb, m_w_o, m_norm_ffn, m_w_gate, m_w_up, m_w_down, m_norm_final, v_norm_mix, v_w_in, v_q_latent_norm, v_w_uq, v_kv_latent_norm, v_w_ukv, v_out_norm_mla, v_out_norm_sb, v_w_o, v_norm_ffn, v_w_gate, v_w_up, v_w_down, v_norm_final):
    given = dict(norm_mix=norm_mix, w_in=w_in, q_latent_norm=q_latent_norm, w_uq=w_uq, kv_latent_norm=kv_latent_norm, w_ukv=w_ukv,
                 out_norm_mla=out_norm_mla, out_norm_sb=out_norm_sb, w_o=w_o, norm_ffn=norm_ffn, w_gate=w_gate, w_up=w_up,
                 w_down=w_down, norm_final=norm_final)
    mom_m = dict(norm_mix=m_norm_mix, w_in=m_w_in, q_latent_norm=m_q_latent_norm, w_uq=m_w_uq, kv_latent_norm=m_kv_latent_norm,
                 w_ukv=m_w_ukv, out_norm_mla=m_out_norm_mla, out_norm_sb=m_out_norm_sb, w_o=m_w_o, norm_ffn=m_norm_ffn,
                 w_gate=m_w_gate, w_up=m_w_up, w_down=m_w_down, norm_final=m_norm_final)
    mom_v = dict(norm_mix=v_norm_mix, w_in=v_w_in, q_latent_norm=v_q_latent_norm, w_uq=v_w_uq, kv_latent_norm=v_kv_latent_norm,
                 w_ukv=v_w_ukv, out_norm_mla=v_out_norm_mla, out_norm_sb=v_out_norm_sb, w_o=v_w_o, norm_ffn=v_norm_ffn,
                 w_gate=v_w_gate, w_up=v_w_up, w_down=v_w_down, norm_final=v_norm_final)
    xs = x[0]
    tgt = loss_target[0]
    s = xs.shape[0]
    c_idx = lax.axis_index("c")
    shard_idx = 2 * lax.axis_index("x") + lax.axis_index("y")

    def block2d(t, name):
        t = t.reshape(t.shape[-2:])
        return t.T if name in FLIPPED else t

    shard2d = {name: block2d(given[name], name) for name, *_ in SHARDED}
    local = {name: shard2d[name].astype(BF16) for name, *_ in SHARDED}
    axis_of = {name: 0 if name in FLIPPED else axis for name, _, _, axis in SHARDED}

    def whole(names, gathered):
        return {name: _full_weight(t, axis_of[name]) for name, t in zip(names, gathered)}

    small = {name: given[name].reshape(1, n) for name, n in SMALL}
    w = _attention_weights(whole(EARLY, _allgather_list("allgather_w", [local[name] for name in EARLY])), small)
    tabs = _rope_tables(positions[0])
    msuf, mpre = _sb_masks(min(SB_TK, s))

    u, cq, ckv, cqn, ckvn, qn, qr, kn, vm, kr, sq, sk, sv = _fwd_a(xs, tabs, w)
    o_mla, lse, late = _mla_fwd(qn, qr, kn, kr, vm, [local[name] for name in LATE])
    w.update(_ffn_weights(whole(LATE, late)))
    o_sb, cmat = _sb_fwd(sq, sk, sv, msuf)
    merged, h1, f, gate, up, act = _fwd_b1(xs, o_mla, o_sb, w)
    dh2, loss_part, dg_n = _fwd_b2(h1, act, tgt, w)

    def shards_of(names, grads):
        return [_shard_major(grads[name], axis_of[name]) for name in names]

    def reduced(tag, chip_f32, others):
        mine = _add_chips("add_chips_" + tag, chip_f32, others, shard_idx)
        return tuple(mine), tuple(_swap_result("swap_result_" + tag, mine))

    dgate, dup, dh1, do_mla, do_sb, dg_f, dg_a, dg_b = _bwd_b(dh2, gate, up, h1, o_mla, o_sb, w)
    late_gs = shards_of(LATE, {
        "w_o": _tn_matmul("dw_o", merged, dh1), "w_gate": _tn_matmul("dw_gate", dgate, f),
        "w_up": _tn_matmul("dw_up", dup, f), "w_down": _tn_matmul("dw_down", act, dh2)})
    (dsq, dsk, dsv), late_got = _sb_bwd(sq, sk, sv, do_sb, cmat, msuf, mpre, late_gs)
    late_f32, late_bf16 = _add_sibling("add_sibling_late", late_gs, late_got, c_idx)
    (dqn, dqr, dkn, dkr, dvm), late_others = _mla_bwd(qn, qr, kn, kr, vm, o_mla, do_mla, lse, late_bf16)
    mine_late, theirs_late = reduced("late", late_f32, late_others)
    dx, a1, a2, dcq, dckv, dkrc, dkrs, dg_q, dg_kv, dg_mix = _bwd_a(xs, dh1, cq, ckv, dqn, dqr, dkn, dvm, dkr, dsq, dsk, dsv, tabs, w)

    g_cq, g_ckv, g_krc, g_krs, g_sq, g_sk, g_sv = _tn_multi("dw_in", u, [dcq, dckv, dkrc, dkrs, dsq, dsk, dsv])
    g_qn, g_qr1, g_qr2 = _tn_multi("dw_uq", cqn, [dqn, a1, a2])
    g_kn, g_v = _tn_multi("dw_ukv", ckvn, [dkn, dvm])
    slots = lambda g: g.reshape(g.shape[0], MLA_HEADS, MLA_ROPE)
    g_kr = jnp.sum(slots(g_krc), axis=1) + _rot_cols_t(jnp.sum(slots(g_krs), axis=1))
    g_qr = (slots(g_qr1) + _rot_cols_t(slots(g_qr2))).reshape(Q_RANK, -1)
    early_gs = shards_of(EARLY, {
        "w_in": jnp.concatenate([g_cq, g_ckv, g_kr, g_sq, g_sk, g_sv], axis=1),
        "w_uq": _by_head(g_qn, g_qr, MLA_NOPE, MLA_ROPE),
        "w_ukv": _by_head(g_kn * MLA_DK_SCALE, g_v, MLA_NOPE, MLA_V)})
    early_f32, early_bf16 = _add_sibling("add_sibling_early", early_gs, _swap_halves("swap_halves_early", early_gs), c_idx)
    mine_early, theirs_early = reduced("early", early_f32, _chip_scatter(early_bf16))
    halves = dict(zip(EARLY + LATE, zip(mine_early + mine_late, theirs_early + theirs_late)))

    small_parts = jnp.concatenate([dg_mix, dg_q, dg_kv, dg_a, dg_b, dg_f, dg_n, loss_part], axis=1)
    small_sum = _allreduce_small(jnp.broadcast_to(small_parts, (8, small_parts.shape[1])))
    small_g, loss = small_sum[0:1, :-LANES], small_sum[0, -LANES]

    g_out, d_out, m_out, v_out = {}, {}, {}, {}
    for name, *_ in SHARDED:
        shape = given[name].shape
        outs = _adamw_halves("adamw_" + name, shard2d[name], *halves[name], block2d(mom_m[name], name), block2d(mom_v[name], name),
                             c_idx)
        g_out[name], d_out[name], m_out[name], v_out[name] = ((t.T if name in FLIPPED else t).reshape(shape) for t in outs)
    cat = lambda src: jnp.concatenate([src[name].reshape(1, n) for name, n in SMALL], axis=1)
    d, mn, vn = _adamw("adamw_small", cat(given), small_g, cat(mom_m), cat(mom_v))
    off = 0
    for name, n in SMALL:
        shape = given[name].shape
        g_out[name], d_out[name], m_out[name], v_out[name] = (t[:, off:off + n].reshape(shape) for t in (small_g, d, mn, vn))
        off += n

    order = ["norm_mix", "w_in", "q_latent_norm", "w_uq", "kv_latent_norm", "w_ukv", "out_norm_mla", "out_norm_sb", "w_o",
             "norm_ffn", "w_gate", "w_up", "w_down", "norm_final"]
    return (loss, dx[None], *[g_out[n] for n in order], *[d_out[n] for n in order], *[m_out[n] for n in order],
            *[v_out[n] for n in order])
```

```python
import functools
import math

import jax
import jax.numpy as jnp
from jax import lax
from jax.experimental import pallas as pl
from jax.experimental.pallas import tpu as pltpu

F32 = jnp.float32
BF16 = jnp.bfloat16
MESH = pl.DeviceIdType.MESH

D_MODEL = 1024
EPS = 1e-6
MLA_HEADS = 8
MLA_NOPE = 64
MLA_ROPE = 32
MLA_V = 64
MLA_QK = MLA_NOPE + MLA_ROPE
Q_RANK = 256
KV_RANK = 128
ROPE_THETA = 10000.0
SB_HEADS = 8
SB_DIM = 64
MLA_WIDTH = MLA_HEADS * MLA_V
SB_WIDTH = SB_HEADS * SB_DIM
D_FF = 2816
IN_WIDTH = Q_RANK + KV_RANK + MLA_ROPE + 3 * SB_WIDTH

ADAM_LR = 0.001
ADAM_B1 = 0.9
ADAM_B2 = 0.999
ADAM_EPS = 1e-08
ADAM_WD = 0.01
ADAM_STEP = 10

N_SHARD = 4
LANES = 128
ROPE_TILE = LANES
VMEM_LIMIT = 56 * 1024 * 1024
TN_ACC_BYTES = 6 * 1024 * 1024 + 512 * 1024
NEG = -1e30
MLA_SCALE = 1.0 / math.sqrt(MLA_QK)
MLA_DK_SCALE = math.log(2.0)
MLA_QSCALE = MLA_SCALE * math.log2(math.e)
SB_SKIP = 110.0

ROW_TILE = 512
ROW_TILE_ELEMENTWISE = 256
MLA_TQ = 1024
SB_TQ = 512
MLA_TK = 1024
MLA_BWD_TK = 1024
MLA_DIAG_TK = 256
SB_TK = 256
TN_TS = 2048
ADAM_STEPS = 4


def _dot(a, b):
    return jnp.dot(a, b, preferred_element_type=F32)


def _dot_nt(a, b):
    return lax.dot_general(a, b, (((1,), (1,)), ((), ())), preferred_element_type=F32)


def _dot_tn(a, b):
    return lax.dot_general(a, b, (((0,), (0,)), ((), ())), preferred_element_type=F32)


def _params(n_grid, vmem=VMEM_LIMIT):
    return pltpu.CompilerParams(dimension_semantics=("arbitrary",) * n_grid, vmem_limit_bytes=vmem)


def _rms(x):
    r = lax.rsqrt(jnp.mean(x * x, axis=-1, keepdims=True) + EPS)
    return x * r, r


def _rms_bwd(n, r, g, dy):
    dn = dy * g
    dx = r * (dn - n * jnp.mean(dn * n, axis=-1, keepdims=True))
    return dx, jnp.sum(dy * n, axis=0, keepdims=True)


def _accumulate(ref, val, step):
    @pl.when(step == 0)
    def _():
        ref[...] = val

    @pl.when(step != 0)
    def _():
        ref[...] += val


def _rowwise(name, body, rows, consts, row_out, acc_out, tm):
    n_rows = rows[0].shape[0]
    tm = min(tm, n_rows)
    nr, nc, no = len(rows), len(consts), len(row_out)

    def kern(*refs):
        body(refs[:nr], refs[nr:nr + nc], refs[nr + nc:nr + nc + no], refs[nr + nc + no:], pl.program_id(0))

    in_specs = [pl.BlockSpec((tm, a.shape[1]), lambda i: (i, 0)) for a in rows]
    in_specs += [pl.BlockSpec(a.shape, lambda i: (0, 0), pipeline_mode=pl.Buffered(1)) for a in consts]
    out_specs = [pl.BlockSpec((tm, s.shape[1]), lambda i: (i, 0)) for s in row_out]
    out_specs += [pl.BlockSpec(s.shape, lambda i: (0, 0)) for s in acc_out]
    return pl.pallas_call(
        kern, name=name, grid=(n_rows // tm,), in_specs=in_specs, out_specs=out_specs,
        out_shape=list(row_out) + list(acc_out), compiler_params=_params(1),
    )(*rows, *consts)


def _sds(shape, dtype):
    return jax.ShapeDtypeStruct(shape, dtype)


def _fwd_a(x, tabs, w):
    s = x.shape[0]

    def body(r, c, o, a, step):
        x_ref, cos_ref, sin_ref = r
        gmix, wcq, wckv, wkr, wkrr, wsq, wsk, wsv, gq, wqn, wqr, wqrr, gkv, wkn, wv = c
        u_o, cq_o, ckv_o, cqn_o, ckvn_o, qn_o, qr_o, kn_o, v_o, kr_o, sq_o, sk_o, sv_o = o
        cos, sin = cos_ref[...], sin_ref[...]
        n, _ = _rms(x_ref[...])
        u = (n * gmix[...]).astype(BF16)
        u_o[...] = u
        cq = _dot(u, wcq[...])
        ckv = _dot(u, wckv[...])
        kr_o[...] = (_dot(u, wkr[...]) * cos[:, :ROPE_TILE] + _dot(u, wkrr[...]) * sin[:, :ROPE_TILE]).astype(BF16)
        sq_o[...] = _dot(u, wsq[...]).astype(BF16)
        sk_o[...] = _dot(u, wsk[...]).astype(BF16)
        sv_o[...] = _dot(u, wsv[...]).astype(BF16)
        cq_o[...] = cq
        ckv_o[...] = ckv
        nq, _ = _rms(cq)
        cqn = (nq * gq[...]).astype(BF16)
        cqn_o[...] = cqn
        qn_o[...] = (_dot(cqn, wqn[...]) * MLA_QSCALE).astype(BF16)
        qr_o[...] = ((_dot(cqn, wqr[...]) * cos + _dot(cqn, wqrr[...]) * sin) * MLA_QSCALE).astype(BF16)
        nkv, _ = _rms(ckv)
        ckvn = (nkv * gkv[...]).astype(BF16)
        ckvn_o[...] = ckvn
        kn_o[...] = _dot(ckvn, wkn[...]).astype(BF16)
        v_o[...] = _dot(ckvn, wv[...]).astype(BF16)

    outs = [
        _sds((s, D_MODEL), BF16), _sds((s, Q_RANK), F32), _sds((s, KV_RANK), F32), _sds((s, Q_RANK), BF16),
        _sds((s, KV_RANK), BF16), _sds((s, MLA_HEADS * MLA_NOPE), BF16), _sds((s, MLA_HEADS * MLA_ROPE), BF16),
        _sds((s, MLA_HEADS * MLA_NOPE), BF16), _sds((s, MLA_WIDTH), BF16), _sds((s, ROPE_TILE), BF16),
        _sds((s, SB_WIDTH), BF16), _sds((s, SB_WIDTH), BF16), _sds((s, SB_WIDTH), BF16),
    ]
    consts = [w["g_mix"], w["w_cq"], w["w_ckv"], w["w_kr4"], w["w_kr4r"], w["w_sbq"], w["w_sbk"], w["w_sbv"], w["g_q"],
              w["w_qn"], w["w_qr"], w["w_qrr"], w["g_kv"], w["w_kn"], w["w_v"]]
    return _rowwise("fwd_a", body, [x, tabs["cos"], tabs["sin"]], consts, outs, [], ROW_TILE)


def _fwd_b1(x, o_mla, o_sb, w):
    s = x.shape[0]

    def body(r, c, o, a, step):
        x_ref, oa_ref, ob_ref = r
        ga, gb, woa, wob, gf, wg, wu = c
        mg_o, h1_o, f_o, gate_o, up_o, act_o = o
        na, _ = _rms(oa_ref[...])
        nb, _ = _rms(ob_ref[...])
        ma = (na * ga[...]).astype(BF16)
        mb = (nb * gb[...]).astype(BF16)
        mg_o[:, :MLA_WIDTH] = ma
        mg_o[:, MLA_WIDTH:] = mb
        h1 = x_ref[...] + _dot(ma, woa[...]) + _dot(mb, wob[...])
        h1_o[...] = h1
        nf, _ = _rms(h1)
        f = (nf * gf[...]).astype(BF16)
        f_o[...] = f
        gate = _dot(f, wg[...])
        up = _dot(f, wu[...])
        gate_o[...] = gate.astype(BF16)
        up_o[...] = up.astype(BF16)
        act_o[...] = (gate * (1.0 / (1.0 + jnp.exp(-gate))) * up).astype(BF16)

    outs = [_sds((s, D_MODEL), BF16), _sds((s, D_MODEL), F32), _sds((s, D_MODEL), BF16), _sds((s, D_FF), BF16),
            _sds((s, D_FF), BF16), _sds((s, D_FF), BF16)]
    consts = [w["g_a"], w["g_b"], w["w_oa"], w["w_ob"], w["g_f"], w["w_gate"], w["w_up"]]
    return _rowwise("fwd_b1", body, [x, o_mla, o_sb], consts, outs, [], ROW_TILE)


def _fwd_b2(h1, act, tgt, w):
    s = h1.shape[0]

    def body(r, c, o, a, step):
        h1_ref, act_ref, t_ref = r
        wd, gn = c
        (dh2_o,) = o
        loss_o, dgn_o = a
        h2 = h1_ref[...] + _dot(act_ref[...], wd[...])
        n2, r2 = _rms(h2)
        err = n2 * gn[...] - t_ref[...]
        part = jnp.sum(jnp.sum(err * err, axis=1, keepdims=True), axis=0, keepdims=True) * (0.5 / D_MODEL)
        _accumulate(loss_o, jnp.broadcast_to(part, (1, LANES)), step)
        dh2, dgn = _rms_bwd(n2, r2, gn[...], err * (1.0 / D_MODEL))
        dh2_o[...] = dh2
        _accumulate(dgn_o, dgn, step)

    return _rowwise("fwd_b2", body, [h1, act, tgt], [w["w_down"], w["g_n"]], [_sds((s, D_MODEL), F32)],
                    [_sds((1, LANES), F32), _sds((1, D_MODEL), F32)], ROW_TILE)


def _bwd_b(dh2, gate, up, h1, o_mla, o_sb, w):
    s = h1.shape[0]

    def body(r, c, o, a, step):
        dh2_ref, gate_ref, up_ref, h1_ref, oa_ref, ob_ref = r
        wdt, wgt, wut, gf, woat, wobt, ga, gb = c
        dgate_o, dup_o, dh1_o, doa_o, dob_o = o
        dgf_o, dga_o, dgb_o = a
        dh2 = dh2_ref[...]
        dact = _dot(dh2.astype(BF16), wdt[...])
        gate = gate_ref[...].astype(F32)
        sig = 1.0 / (1.0 + jnp.exp(-gate))
        dup = (dact * (gate * sig)).astype(BF16)
        dgate = (dact * up_ref[...].astype(F32) * (sig * (1.0 + gate * (1.0 - sig)))).astype(BF16)
        dup_o[...] = dup
        dgate_o[...] = dgate
        df = _dot(dgate, wgt[...]) + _dot(dup, wut[...])
        nf, rf = _rms(h1_ref[...])
        dres, dgf = _rms_bwd(nf, rf, gf[...], df)
        dh1 = dh2 + dres
        dh1_o[...] = dh1
        dh1b = dh1.astype(BF16)
        na, ra = _rms(oa_ref[...])
        doa, dga = _rms_bwd(na, ra, ga[...], _dot(dh1b, woat[...]))
        nb, rb = _rms(ob_ref[...])
        dob, dgb = _rms_bwd(nb, rb, gb[...], _dot(dh1b, wobt[...]))
        doa_o[...] = doa
        dob_o[...] = dob
        _accumulate(dgf_o, dgf, step)
        _accumulate(dga_o, dga, step)
        _accumulate(dgb_o, dgb, step)

    consts = [w["w_down_t"], w["w_gate_t"], w["w_up_t"], w["g_f"], w["w_oa_t"], w["w_ob_t"], w["g_a"], w["g_b"]]
    outs = [_sds((s, D_FF), BF16), _sds((s, D_FF), BF16), _sds((s, D_MODEL), F32), _sds((s, MLA_WIDTH), F32), _sds((s, SB_WIDTH), F32)]
    accs = [_sds((1, D_MODEL), F32), _sds((1, MLA_WIDTH), F32), _sds((1, SB_WIDTH), F32)]
    return _rowwise("bwd_b", body, [dh2, gate, up, h1, o_mla, o_sb], consts, outs, accs, ROW_TILE_ELEMENTWISE)


def _fold_pairs(t):
    return jnp.concatenate([t[:, :LANES] + t[:, LANES:2 * LANES], t[:, 2 * LANES:3 * LANES] + t[:, 3 * LANES:]], axis=1)


def _bwd_a(x, dh1, cq, ckv, dqn, dqr, dkn, dvm, dkr, dsq, dsk, dsv, tabs, w):
    s = x.shape[0]

    def body(r, c, o, a, step):
        x_ref, dh1_ref, cq_ref, ckv_ref, dqn_ref, dqr_ref, dkn_ref, dvm_ref, dkr_ref, dsq_ref, dsk_ref, dsv_ref, cos_ref, sin_ref = r
        wqnt, wqrt, wqrrt, gq, wknt, wvt, gkv, wcqt, wckvt, wkrt, wkrrt, wsqt, wskt, wsvt, gmix = c
        dx_o, a1_o, a2_o, dcq_o, dckv_o, dkrc_o, dkrs_o = o
        dgq_o, dgkv_o, dgmix_o = a
        cos, sin = cos_ref[...], sin_ref[...]
        dqr = _fold_pairs(dqr_ref[...])
        a1 = (dqr * cos).astype(BF16)
        a2 = (dqr * sin).astype(BF16)
        a1_o[...] = a1
        a2_o[...] = a2
        nq, rq = _rms(cq_ref[...])
        dcqn = _dot(dqn_ref[...].astype(BF16), wqnt[...]) + _dot(a1, wqrt[...]) + _dot(a2, wqrrt[...])
        dcq, dgq = _rms_bwd(nq, rq, gq[...], dcqn)
        nkv, rkv = _rms(ckv_ref[...])
        dckvn = _dot((dkn_ref[...] * MLA_DK_SCALE).astype(BF16), wknt[...]) + _dot(dvm_ref[...].astype(BF16), wvt[...])
        dckv, dgkv = _rms_bwd(nkv, rkv, gkv[...], dckvn)
        dkr = _fold_pairs(dkr_ref[...]) * MLA_DK_SCALE
        dcq_b = dcq.astype(BF16)
        dckv_b = dckv.astype(BF16)
        dkrc = (dkr * cos).astype(BF16)
        dkrs = (dkr * sin).astype(BF16)
        dcq_o[...] = dcq_b
        dckv_o[...] = dckv_b
        dkrc_o[...] = dkrc
        dkrs_o[...] = dkrs
        du = (_dot(dcq_b, wcqt[...]) + _dot(dckv_b, wckvt[...]) + _dot(dkrc, wkrt[...]) + _dot(dkrs, wkrrt[...])
              + _dot(dsq_ref[...].astype(BF16), wsqt[...]) + _dot(dsk_ref[...].astype(BF16), wskt[...])
              + _dot(dsv_ref[...].astype(BF16), wsvt[...]))
        nx, rx = _rms(x_ref[...])
        dres, dgmix = _rms_bwd(nx, rx, gmix[...], du)
        dx_o[...] = dh1_ref[...] + dres
        _accumulate(dgq_o, dgq, step)
        _accumulate(dgkv_o, dgkv, step)
        _accumulate(dgmix_o, dgmix, step)

    consts = [w["w_qn_t"], w["w_qr_t"], w["w_qrr_t"], w["g_q"], w["w_kn_t"], w["w_v_t"], w["g_kv"], w["w_cq_t"], w["w_ckv_t"],
              w["w_kr8_t"], w["w_kr8r_t"], w["w_sbq_t"], w["w_sbk_t"], w["w_sbv_t"], w["g_mix"]]
    rope_w = MLA_HEADS * MLA_ROPE
    outs = [_sds((s, D_MODEL), F32), _sds((s, rope_w), BF16), _sds((s, rope_w), BF16), _sds((s, Q_RANK), BF16),
            _sds((s, KV_RANK), BF16), _sds((s, rope_w), BF16), _sds((s, rope_w), BF16)]
    accs = [_sds((1, Q_RANK), F32), _sds((1, KV_RANK), F32), _sds((1, D_MODEL), F32)]
    rows = [x, dh1, cq, ckv, dqn, dqr, dkn, dvm, dkr, dsq, dsk, dsv, tabs["cos"], tabs["sin"]]
    return _rowwise("bwd_a", body, rows, consts, outs, accs, ROW_TILE)


def _tn_multi(name, x, ys):
    s, k = x.shape
    ts = min(TN_TS, s)
    n_y = len(ys)

    def kern(*refs):
        step = pl.program_id(0)
        xb = refs[0][...].astype(BF16)
        for j in range(n_y):
            _accumulate(refs[1 + n_y + j], _dot_tn(xb, refs[1 + j][...].astype(BF16)), step)

    return pl.pallas_call(
        kern, name=name, grid=(s // ts,),
        in_specs=[pl.BlockSpec((ts, k), lambda i: (i, 0))] + [pl.BlockSpec((ts, y.shape[1]), lambda i: (i, 0)) for y in ys],
        out_specs=[pl.BlockSpec((k, y.shape[1]), lambda i: (0, 0)) for y in ys],
        out_shape=[_sds((k, y.shape[1]), F32) for y in ys], compiler_params=_params(1),
    )(x, *ys)


def _tn_tile(k, n):
    if n % LANES or k * n * 4 <= TN_ACC_BYTES:
        return n
    units = n // LANES
    best = 1
    for d in range(1, units + 1):
        if units % d == 0 and k * d * LANES * 4 <= TN_ACC_BYTES:
            best = d
    return best * LANES


def _tn_matmul(name, x, y):
    s, k = x.shape
    n = y.shape[1]
    ts = min(TN_TS, s)
    tn = _tn_tile(k, n)

    def kern(x_ref, y_ref, o_ref):
        step = pl.program_id(1)
        _accumulate(o_ref, _dot_tn(x_ref[...].astype(BF16), y_ref[...].astype(BF16)), step)

    return pl.pallas_call(
        kern, name=name, grid=(n // tn, s // ts),
        in_specs=[pl.BlockSpec((ts, k), lambda j, i: (i, 0)), pl.BlockSpec((ts, tn), lambda j, i: (i, j))],
        out_specs=pl.BlockSpec((k, tn), lambda j, i: (0, j)), out_shape=_sds((k, n), F32), compiler_params=_params(2),
    )(x, y)


def _lanes(rows, lo, width):
    lane = lax.broadcasted_iota(jnp.int32, (rows, LANES), 1)
    return jnp.logical_and(lane >= lo, lane < lo + width)


def _keep(mask, t):
    return jnp.where(mask, t, jnp.zeros_like(t))


def _mla_qcat(qn_ref, qr_ref, rope_lo, half, rows):
    qn = _keep(_lanes(rows, MLA_NOPE * half, MLA_NOPE), qn_ref[...])
    qr = _keep(_lanes(rows, rope_lo, MLA_ROPE), qr_ref[...])
    return jnp.concatenate([qn, qr], axis=1)


def _diag_mask(rows, width, row0, col0):
    row = lax.broadcasted_iota(jnp.int32, (rows, width), 0)
    col = lax.broadcasted_iota(jnp.int32, (rows, width), 1)
    return col + (col0 - row0) <= row


def _mla_fwd(qn, qr, kn, kr, v, riders=(), tq=MLA_TQ, tk=MLA_TK, td=MLA_TQ):
    s = qn.shape[0]
    tq, tk, td = min(tq, s), min(tk, s), min(td, s)
    ratio = tq // tk

    n_ride = len(riders)
    n_pairs = MLA_HEADS // 2

    def kern(qn_ref, qr_ref, kn_ref, kr_ref, v_ref, *rest):
        o_ref, lse_ref = rest[n_ride:n_ride + 2]
        g = pl.program_id(0)
        i = pl.program_id(1)
        if n_ride:
            send, forward, finish = _gather_steps(rest[:n_ride], rest[n_ride + 2:2 * n_ride + 2], *rest[2 * n_ride + 2:])
            pl.when(jnp.logical_and(g == 0, i == 0))(send)
            pl.when(jnp.logical_and(g == 1, i == 0))(forward)
        qcat = [_mla_qcat(qn_ref, qr_ref, MLA_ROPE * (2 * (g % 2) + half), half, tq) for half in range(2)]

        def block(k0, width, carry, row0, masked, half):
            m, l, acc = (c[row0:] for c in carry)
            ks = pl.ds(pl.multiple_of(k0, width), width)
            kcat = jnp.concatenate([kn_ref[ks, :], kr_ref[ks, :]], axis=1)
            sc = _dot_nt(qcat[half][row0:], kcat)
            if masked:
                sc = jnp.where(_diag_mask(tq - row0, width, row0, row0), sc, NEG)
            m_new = jnp.maximum(m, jnp.max(sc, axis=1, keepdims=True))
            p = jnp.exp2(sc - m_new)
            alpha = jnp.exp2(m - m_new)
            l = alpha * l + jnp.sum(p, axis=1, keepdims=True)
            acc = alpha * acc + _dot(p.astype(BF16), v_ref[ks, :])
            new = (m_new, l, acc)
            return new if row0 == 0 else tuple(jnp.concatenate([c[:row0], n], axis=0) for c, n in zip(carry, new))

        def both(k0, width, carries, row0, masked):
            return tuple(block(k0, width, carries[half], row0, masked, half) for half in range(2))

        init = (jnp.full((tq, 1), NEG, F32), jnp.zeros((tq, 1), F32), jnp.zeros((tq, LANES), F32))
        carries = lax.fori_loop(0, i * ratio, lambda kb, c: both(kb * tk, tk, c, 0, False), (init, init))
        for row0 in range(0, tq, td):
            carries = both(i * tq + row0, td, carries, row0, True)
        for half in range(2):
            m, l, acc = carries[half]
            out = _keep(_lanes(tq, MLA_V * half, MLA_V), acc / l)
            lse = _keep(_lanes(tq, MLA_ROPE * half, MLA_ROPE), jnp.broadcast_to(m + jnp.log2(l), (tq, LANES)))
            if half == 0:
                o_ref[...] = out
                lse_ref[...] = lse
            else:
                o_ref[...] += out
                lse_ref[...] += lse
        if n_ride:
            pl.when(jnp.logical_and(g == n_pairs - 1, i == s // tq - 1))(finish)

    qblk = pl.BlockSpec((tq, LANES), lambda g, i: (i, g))
    full = pl.BlockSpec((s, LANES), lambda g, i: (0, g))
    outs = pl.pallas_call(
        kern, name="mla_fwd", grid=(n_pairs, s // tq),
        in_specs=[qblk, pl.BlockSpec((tq, LANES), lambda g, i: (i, g // 2)), full, pl.BlockSpec((s, LANES), lambda g, i: (0, 0)), full]
        + [HBM_SPEC] * n_ride,
        out_specs=[qblk, qblk] + [HBM_SPEC] * n_ride,
        out_shape=[_sds((s, MLA_WIDTH), F32), _sds((s, n_pairs * LANES), F32)] + [_sds((N_SHARD,) + a.shape, a.dtype) for a in riders],
        scratch_shapes=_gather_sems(n_ride) if n_ride else [], compiler_params=_params(2),
    )(qn, qr, kn, kr, v, *riders)
    return outs[0], outs[1], outs[2:]


def _mla_bwd(qn, qr, kn, kr, v, o, do, lse, riders=(), tq=MLA_TQ, tk=MLA_BWD_TK, td=MLA_DIAG_TK):
    s = qn.shape[0]
    tq, tk, td = min(tq, s), min(tk, s), min(td, s)
    ratio = tq // tk

    n_ride = len(riders)
    n_pairs = MLA_HEADS // 2

    def kern(qn_ref, qr_ref, kn_ref, kr_ref, v_ref, o_ref, do_ref, lse_ref, *rest):
        dqn_ref, dqr_ref, dkn_ref, dkr_ref, dv_ref = rest[n_ride:n_ride + 5]
        g = pl.program_id(0)
        i = pl.program_id(1)
        if n_ride:
            start, finish = _scatter_steps(rest[:n_ride], rest[n_ride + 5:2 * n_ride + 5], *rest[2 * n_ride + 5:])
            pl.when(jnp.logical_and(g == 0, i == 0))(start)

        @pl.when(i == 0)
        def _():
            dkn_ref[...] = jnp.zeros_like(dkn_ref)
            dkr_ref[...] = jnp.zeros_like(dkr_ref)
            dv_ref[...] = jnp.zeros_like(dv_ref)

        for half in range(2):
            rope_lo = MLA_ROPE * (2 * (g % 2) + half)
            qcat = _mla_qcat(qn_ref, qr_ref, rope_lo, half, tq)
            mine = _lanes(tq, MLA_V * half, MLA_V)
            do_f = _keep(mine, do_ref[...])
            do_b = do_f.astype(BF16)
            delta = jnp.sum(do_f * o_ref[...], axis=1, keepdims=True)
            lse_v = lse_ref[:, MLA_ROPE * half:MLA_ROPE * half + 1]

            def block(k0, width, dq_acc, row0, masked, qcat=qcat, do_b=do_b, delta=delta, lse_v=lse_v):
                ks = pl.ds(pl.multiple_of(k0, width), width)
                kcat = jnp.concatenate([kn_ref[ks, :], kr_ref[ks, :]], axis=1)
                qc, dob = qcat[row0:], do_b[row0:]
                p = jnp.exp2(_dot_nt(qc, kcat) - lse_v[row0:])
                if masked:
                    p = jnp.where(_diag_mask(tq - row0, width, row0, row0), p, 0.0)
                ds = (p * (_dot_nt(dob, v_ref[ks, :]) - delta[row0:])).astype(BF16)
                dv_ref[ks, :] += _dot_tn(p.astype(BF16), dob)
                dkc = _dot_tn(ds, qc)
                dkn_ref[ks, :] += dkc[:, :LANES]
                dkr_ref[ks, :] += dkc[:, LANES:]
                new = dq_acc[row0:] + _dot(ds, kcat)
                return new if row0 == 0 else jnp.concatenate([dq_acc[:row0], new], axis=0)

            acc = lax.fori_loop(0, i * ratio, lambda kb, c, block=block: block(kb * tk, tk, c, 0, False),
                                jnp.zeros((tq, 2 * LANES), F32))
            for row0 in range(0, tq, td):
                acc = block(i * tq + row0, td, acc, row0, True)
            dqn = _keep(_lanes(tq, MLA_NOPE * half, MLA_NOPE), acc[:, :LANES] * MLA_SCALE)
            dqr = _keep(_lanes(tq, rope_lo, MLA_ROPE), acc[:, LANES:] * MLA_SCALE)
            if half == 0:
                dqn_ref[...] = dqn
                dqr_ref[...] = dqr
            else:
                dqn_ref[...] += dqn
                dqr_ref[...] += dqr
        if n_ride:
            pl.when(jnp.logical_and(g == n_pairs - 1, i == s // tq - 1))(finish)

    qblk = pl.BlockSpec((tq, LANES), lambda g, i: (i, g))
    full = pl.BlockSpec((s, LANES), lambda g, i: (0, g))
    once = lambda spec_map: pl.BlockSpec((s, LANES), spec_map, pipeline_mode=pl.Buffered(1))
    wide = _sds((s, n_pairs * LANES), F32)
    outs = pl.pallas_call(
        kern, name="mla_bwd", grid=(n_pairs, s // tq),
        in_specs=[qblk, pl.BlockSpec((tq, LANES), lambda g, i: (i, g // 2)), once(lambda g, i: (0, g)), once(lambda g, i: (0, 0)),
                  once(lambda g, i: (0, g)), qblk, qblk, qblk] + [HBM_SPEC] * n_ride,
        out_specs=[qblk, qblk, full, full, full] + [HBM_SPEC] * n_ride,
        out_shape=[wide] * 5 + [_sds((N_SHARD - 1,) + p.shape[1:], p.dtype) for p in riders],
        scratch_shapes=_scatter_sems(n_ride) if n_ride else [], compiler_params=_params(2),
    )(qn, qr, kn, kr, v, o, do, lse, *riders)
    return outs[:5], outs[5:]


def _sb_masks(tk):
    j = lax.broadcasted_iota(jnp.int32, (tk, tk), 0)
    c = lax.broadcasted_iota(jnp.int32, (tk, tk), 1)
    return (j > c).astype(BF16), (j < c).astype(BF16)


def _sb_scores(qs, kk, msuf, strict):
    z = _dot_nt(qs, kk)
    lom = -(jnp.maximum(z, 0.0) + jnp.log(1.0 + jnp.exp(-jnp.abs(z))))
    if strict is not None:
        lom = jnp.where(strict, lom, 0.0)
    return z, lom, _dot(lom.astype(BF16), msuf)


def _sb_strict(tq, tk, d):
    row = lax.broadcasted_iota(jnp.int32, (tq, tk), 0)
    col = lax.broadcasted_iota(jnp.int32, (tq, tk), 1)
    return col + d * tk < row


def _sb_fwd(q, k, v, msuf, tq=SB_TQ, tk=SB_TK):
    s = q.shape[0]
    tq, tk = min(tq, s), min(tk, s)
    ratio = tq // tk

    def kern(q_ref, k_ref, v_ref, m_ref, o_ref, c_ref):
        i = pl.program_id(1)
        msf = m_ref[...]
        lane = lax.broadcasted_iota(jnp.int32, (tq, LANES), 1)
        mine = [_lanes(tq, SB_DIM * half, SB_DIM) for half in range(2)]
        qs = [_keep(m, q_ref[...]) * 0.125 for m in mine]

        def block(kb, carry, dd, half):
            c, acc, cm = carry
            ks = pl.ds(pl.multiple_of(kb * tk, tk), tk)
            strict = None if dd is None else _sb_strict(tq, tk, dd)
            z, lom, suf = _sb_scores(qs[half], k_ref[ks, :], msf, strict)
            a = jnp.exp(z + lom + (suf + c))
            if strict is not None:
                a = jnp.where(strict, a, 0.0)
            acc = acc + _dot(a.astype(BF16), v_ref[ks, :])
            cm = jnp.where(lane == kb, c, cm)
            return c + jnp.sum(lom, axis=1, keepdims=True), acc, cm

        init = (jnp.zeros((tq, 1), F32), jnp.zeros((tq, LANES), F32), jnp.full((tq, LANES), NEG, F32))
        carries = [init, init]
        for dd in range(ratio - 1, -1, -1):
            carries = [block(i * ratio + dd, carries[half], dd, half) for half in range(2)]

        def live(st):
            return jnp.logical_and(st[0] >= 0, jnp.maximum(jnp.max(st[1][0]), jnp.max(st[2][0])) > -SB_SKIP)

        def step(st):
            return (st[0] - 1, block(st[0], st[1], None, 0), block(st[0], st[2], None, 1))

        _, done0, done1 = lax.while_loop(live, step, (i * ratio - 1, carries[0], carries[1]))
        o_ref[...] = _keep(mine[0], done0[1]) + _keep(mine[1], done1[1])
        c_ref[:, :LANES] = done0[2]
        c_ref[:, LANES:] = done1[2]

    qblk = lambda n: pl.BlockSpec((tq, n), lambda g, i: (i, g))
    full = pl.BlockSpec((s, LANES), lambda g, i: (0, g))
    return pl.pallas_call(
        kern, name="sb_fwd", grid=(SB_HEADS // 2, s // tq),
        in_specs=[qblk(LANES), full, full, pl.BlockSpec((tk, tk), lambda g, i: (0, 0))],
        out_specs=[qblk(LANES), qblk(2 * LANES)],
        out_shape=[_sds((s, SB_WIDTH), F32), _sds((s, SB_HEADS * LANES), F32)], compiler_params=_params(2),
    )(q, k, v, msuf)


def _sb_bwd(q, k, v, do, cmat, msuf, mpre, riders=(), tq=SB_TQ, tk=SB_TK):
    s = q.shape[0]
    tq, tk = min(tq, s), min(tk, s)
    ratio = tq // tk

    n_ride = len(riders)
    n_pairs = SB_HEADS // 2

    def kern(q_ref, k_ref, v_ref, do_ref, c_ref, ms_ref, mp_ref, *rest):
        dq_ref, dk_ref, dv_ref = rest[n_ride:n_ride + 3]
        i = pl.program_id(1)
        if n_ride:
            start, finish = _swap_steps(rest[:n_ride], rest[n_ride + 3:2 * n_ride + 3], *rest[2 * n_ride + 3:])
            pl.when(jnp.logical_and(pl.program_id(0) == 0, i == 0))(start)

        @pl.when(i == 0)
        def _():
            dk_ref[...] = jnp.zeros_like(dk_ref)
            dv_ref[...] = jnp.zeros_like(dv_ref)

        msf = ms_ref[...]
        mpf = mp_ref[...]
        lane = lax.broadcasted_iota(jnp.int32, (tq, LANES), 1)
        lane1 = lax.broadcasted_iota(jnp.int32, (1, LANES), 1)
        mine = [_lanes(tq, SB_DIM * half, SB_DIM) for half in range(2)]
        qv = [_keep(m, q_ref[...]) for m in mine]
        qs = [t * 0.125 for t in qv]
        do_b = [_keep(m, do_ref[...]).astype(BF16) for m in mine]
        cm = [c_ref[:, :LANES], c_ref[:, LANES:]]

        def block(kb, carry, dd, half):
            dq_acc, pc = carry
            ks = pl.ds(pl.multiple_of(kb * tk, tk), tk)
            kk = k_ref[ks, :]
            strict = None if dd is None else _sb_strict(tq, tk, dd)
            z, lom, suf = _sb_scores(qs[half], kk, msf, strict)
            c = jnp.sum(jnp.where(lane == kb, cm[half], 0.0), axis=1, keepdims=True)
            a = jnp.exp(z + lom + (suf + c))
            if strict is not None:
                a = jnp.where(strict, a, 0.0)
            g = _dot_nt(do_b[half], v_ref[ks, :]) * a
            p = pc + _dot(g.astype(BF16), mpf)
            omb = jnp.exp(lom)
            dz = (g * omb - (1.0 - omb) * p) * 0.125
            if strict is not None:
                dz = jnp.where(strict, dz, 0.0)
            dz = dz.astype(BF16)
            dv_ref[ks, :] += _dot_tn(a.astype(BF16), do_b[half])
            dk_ref[ks, :] += _dot_tn(dz, qv[half])
            return dq_acc + _dot(dz, kk), pc + jnp.sum(g, axis=1, keepdims=True)

        def needed(cm_h):
            seen = jnp.logical_and(jnp.max(cm_h, axis=0, keepdims=True) > -SB_SKIP, lane1 < i * ratio)
            return jnp.sum(seen.astype(jnp.int32))

        first = i * ratio - jnp.maximum(needed(cm[0]), needed(cm[1]))
        init = (jnp.zeros((tq, LANES), F32), jnp.zeros((tq, 1), F32))
        carries = lax.fori_loop(first, i * ratio, lambda kb, c: (block(kb, c[0], None, 0), block(kb, c[1], None, 1)), (init, init))
        for dd in range(ratio):
            carries = [block(i * ratio + dd, carries[half], dd, half) for half in range(2)]
        dq_ref[...] = _keep(mine[0], carries[0][0]) + _keep(mine[1], carries[1][0])
        if n_ride:
            pl.when(jnp.logical_and(pl.program_id(0) == n_pairs - 1, i == s // tq - 1))(finish)

    qblk = lambda n: pl.BlockSpec((tq, n), lambda g, i: (i, g))
    full = pl.BlockSpec((s, LANES), lambda g, i: (0, g))
    msk = pl.BlockSpec((tk, tk), lambda g, i: (0, 0))
    outs = pl.pallas_call(
        kern, name="sb_bwd", grid=(n_pairs, s // tq),
        in_specs=[qblk(LANES), full, full, qblk(LANES), qblk(2 * LANES), msk, msk] + [HBM_SPEC] * n_ride,
        out_specs=[qblk(LANES), full, full] + [HBM_SPEC] * n_ride,
        out_shape=[_sds((s, SB_WIDTH), F32)] * 3 + _halves_shapes(riders),
        scratch_shapes=_swap_sems(n_ride) if n_ride else [], compiler_params=_params(2),
    )(q, k, v, do, cmat, msuf, mpre, *riders)
    return outs[:3], outs[3:]


def _place():
    return lax.axis_index("x"), lax.axis_index("y"), lax.axis_index("c")


def _other_chips(x, y):
    return [(1 - x, y), (x, 1 - y), (1 - x, 1 - y)]


HBM_SPEC = pl.BlockSpec(memory_space=pl.ANY)


def _gather_steps(ins, outs, send_sems, recv_sems):
    n = len(ins)
    x, y, c = _place()
    sibling = (x, y, 1 - c)
    chips = _other_chips(x, y)

    def half_of(a, ref, pc):
        half = ins[a].shape[0] // 2
        return ref.at[pl.ds(pl.multiple_of(pc * half, 16), half), :]

    def copy(a, k, chip, pc, to, src=None):
        dst = half_of(a, outs[a].at[2 * chip[0] + chip[1]], pc)
        return pltpu.make_async_remote_copy(src_ref=dst if src is None else src, dst_ref=dst, send_sem=send_sems.at[7 * a + k],
                                            recv_sem=recv_sems.at[7 * a + k], device_id=to, device_id_type=MESH)

    def own(a):
        return pltpu.make_async_remote_copy(src_ref=ins[a], dst_ref=outs[a].at[2 * x + y], send_sem=send_sems.at[7 * a + 6],
                                            recv_sem=recv_sems.at[7 * a + 6], device_id=sibling, device_id_type=MESH)

    def first():
        far = [copy(a, j, (x, y), c, (*chip, c), src=half_of(a, ins[a], c)) for a in range(n) for j, chip in enumerate(chips)]
        return far + [own(a) for a in range(n)]

    def passed():
        return [copy(a, 3 + j, chip, c, sibling) for j, chip in enumerate(chips) for a in range(n)]

    def send():
        for cp in first():
            cp.start()

    def forward():
        for j, chip in enumerate(chips):
            for a in range(n):
                copy(a, j, chip, c, sibling).wait_recv()
        for cp in passed():
            cp.start()

    def finish():
        for j, chip in enumerate(chips):
            for a in range(n):
                copy(a, 3 + j, chip, 1 - c, sibling).wait_recv()
        for a in range(n):
            own(a).wait_recv()
        for cp in first() + passed():
            cp.wait_send()

    return send, forward, finish


def _gather_sems(n):
    return [pltpu.SemaphoreType.DMA((7 * n,)), pltpu.SemaphoreType.DMA((7 * n,))]


def _allgather_list(name, shards):
    n = len(shards)

    def body(*refs):
        for stage in _gather_steps(refs[:n], refs[n:2 * n], *refs[2 * n:]):
            stage()

    return pl.pallas_call(
        body, name=name, out_shape=[_sds((N_SHARD,) + a.shape, a.dtype) for a in shards], in_specs=[HBM_SPEC] * n,
        out_specs=[HBM_SPEC] * n, scratch_shapes=_gather_sems(n),
    )(*shards)


def _swap_steps(ins, outs, send_sems, recv_sems):
    x, y, c = _place()

    def copies():
        out = []
        for a in range(len(ins)):
            h = ins[a].shape[1] // 2
            src = ins[a].at[:, pl.ds(pl.multiple_of((1 - c) * h, 8), h), :]
            out.append(pltpu.make_async_remote_copy(src_ref=src, dst_ref=outs[a], send_sem=send_sems.at[a], recv_sem=recv_sems.at[a],
                                                    device_id=(x, y, 1 - c), device_id_type=MESH))
        return out

    def start():
        for cp in copies():
            cp.start()

    def finish():
        for cp in copies():
            cp.wait()

    return start, finish


def _swap_sems(n):
    return [pltpu.SemaphoreType.DMA((n,)), pltpu.SemaphoreType.DMA((n,))]


def _halves_shapes(gs):
    return [_sds((N_SHARD, g.shape[1] // 2, g.shape[2]), g.dtype) for g in gs]


def _swap_halves(name, gs):
    n = len(gs)

    def body(*refs):
        for stage in _swap_steps(refs[:n], refs[n:2 * n], *refs[2 * n:]):
            stage()

    return pl.pallas_call(body, name=name, out_shape=_halves_shapes(gs), in_specs=[HBM_SPEC] * n, out_specs=[HBM_SPEC] * n,
                          scratch_shapes=_swap_sems(n))(*gs)


def _add_sibling(name, gs, gots, c_idx):
    n = len(gs)

    def kern(c_ref, *refs):
        for a in range(n):
            tot = refs[a][...] + refs[n + a][...]
            refs[2 * n + a][...] = tot
            refs[3 * n + a][...] = tot.astype(BF16)

    quarter = lambda g: (None, g.shape[1] // 4, g.shape[2])
    in_specs = [pl.BlockSpec(quarter(g), lambda b, s, c_ref: (b, 2 * c_ref[0] + s, 0)) for g in gs]
    in_specs += [pl.BlockSpec(quarter(g), lambda b, s, c_ref: (b, s, 0)) for g in gs]
    out_specs = [pl.BlockSpec(quarter(g), lambda b, s, c_ref: (b, s, 0)) for g in gs] * 2
    out_shape = [_sds(t.shape, F32) for t in gots] + [_sds(t.shape, BF16) for t in gots]
    outs = pl.pallas_call(
        kern, name=name, out_shape=out_shape,
        grid_spec=pltpu.PrefetchScalarGridSpec(num_scalar_prefetch=1, grid=(N_SHARD, 2), in_specs=in_specs, out_specs=out_specs),
        compiler_params=_params(2),
    )(c_idx.reshape(1), *gs, *gots)
    return outs[:n], outs[n:]


def _scatter_steps(ins, outs, send_sems, recv_sems):
    x, y, c = _place()

    def copies():
        return [pltpu.make_async_remote_copy(
            src_ref=ins[a].at[2 * px + py], dst_ref=outs[a].at[j], send_sem=send_sems.at[3 * a + j], recv_sem=recv_sems.at[3 * a + j],
            device_id=(px, py, c), device_id_type=MESH) for a in range(len(ins)) for j, (px, py) in enumerate(_other_chips(x, y))]

    def start():
        for cp in copies():
            cp.start()

    def finish():
        for cp in copies():
            cp.wait()

    return start, finish


def _scatter_sems(n):
    return [pltpu.SemaphoreType.DMA((3 * n,)), pltpu.SemaphoreType.DMA((3 * n,))]


def _chip_scatter(ps):
    n = len(ps)

    def body(*refs):
        for stage in _scatter_steps(refs[:n], refs[n:2 * n], *refs[2 * n:]):
            stage()

    return pl.pallas_call(
        body, name="chip_scatter", out_shape=[_sds((N_SHARD - 1,) + p.shape[1:], p.dtype) for p in ps], in_specs=[HBM_SPEC] * n,
        out_specs=[HBM_SPEC] * n, scratch_shapes=_scatter_sems(n),
    )(*ps)


def _add_chips(name, ps, others, shard_idx):
    n = len(ps)

    def kern(b_ref, *refs):
        for a in range(n):
            tot = refs[a][...]
            for j in range(N_SHARD - 1):
                tot = tot + refs[n + a][j].astype(F32)
            refs[2 * n + a][...] = tot

    in_specs = [pl.BlockSpec((None, p.shape[1] // 2, p.shape[2]), lambda s, b_ref: (b_ref[0], s, 0)) for p in ps]
    in_specs += [pl.BlockSpec((N_SHARD - 1, p.shape[1] // 2, p.shape[2]), lambda s, b_ref: (0, s, 0)) for p in ps]
    out_specs = [pl.BlockSpec((p.shape[1] // 2, p.shape[2]), lambda s, b_ref: (s, 0)) for p in ps]
    return pl.pallas_call(
        kern, name=name, out_shape=[_sds(p.shape[1:], F32) for p in ps],
        grid_spec=pltpu.PrefetchScalarGridSpec(num_scalar_prefetch=1, grid=(2,), in_specs=in_specs, out_specs=out_specs),
        compiler_params=_params(1),
    )(shard_idx.reshape(1), *ps, *others)


def _swap_result(name, mines):
    n = len(mines)

    def body(*refs):
        ins, outs = refs[:n], refs[n:2 * n]
        send_sems, recv_sems = refs[2 * n:]
        x, y, c = _place()
        copies = [pltpu.make_async_remote_copy(src_ref=ins[a], dst_ref=outs[a], send_sem=send_sems.at[a], recv_sem=recv_sems.at[a],
                                               device_id=(x, y, 1 - c), device_id_type=MESH) for a in range(n)]
        for cp in copies:
            cp.start()
        for cp in copies:
            cp.wait()

    return pl.pallas_call(
        body, name=name, out_shape=[_sds(m.shape, m.dtype) for m in mines], in_specs=[HBM_SPEC] * n,
        out_specs=[HBM_SPEC] * n, scratch_shapes=[pltpu.SemaphoreType.DMA((n,)), pltpu.SemaphoreType.DMA((n,))],
    )(*mines)


def _allreduce_small(v):
    m_per, n = v.shape

    def body(x_ref, tot_ref, all_ref, send_sems, recv_sems, local_sem):
        x, y, c = _place()
        me, sibling = (x, y, c), (x, y, 1 - c)
        chips = _other_chips(x, y)

        def rows(px, py, pc):
            return all_ref.at[pl.ds(pl.multiple_of((4 * px + 2 * py + pc) * m_per, 8), m_per), :]

        def copy(k, block, to, src=None):
            return pltpu.make_async_remote_copy(
                src_ref=rows(*block) if src is None else src, dst_ref=rows(*block), send_sem=send_sems.at[k],
                recv_sem=recv_sems.at[k], device_id=to, device_id_type=MESH)

        mine = pltpu.make_async_copy(x_ref, rows(*me), local_sem)
        mine.start()
        first = [copy(0, me, sibling, src=x_ref)] + [copy(1 + j, me, (*chip, c), src=x_ref) for j, chip in enumerate(chips)]
        for cp in first:
            cp.start()
        passed = [copy(4 + j, (*chip, c), sibling) for j, chip in enumerate(chips)]
        for j, chip in enumerate(chips):
            copy(1 + j, (*chip, c), me).wait_recv()
            passed[j].start()
        copy(0, sibling, me).wait_recv()
        for j, chip in enumerate(chips):
            copy(4 + j, (*chip, 1 - c), me).wait_recv()
        for cp in first + passed:
            cp.wait_send()
        mine.wait()
        tot = all_ref[0:m_per, :]
        for dev in range(1, 8):
            tot = tot + all_ref[dev * m_per:(dev + 1) * m_per, :]
        tot_ref[...] = tot

    vmem = pl.BlockSpec(memory_space=pltpu.VMEM)
    return pl.pallas_call(
        body, name="allreduce_small", out_shape=_sds((m_per, n), F32), in_specs=[vmem], out_specs=vmem,
        scratch_shapes=[pltpu.VMEM((8 * m_per, n), F32), pltpu.SemaphoreType.DMA((7,)), pltpu.SemaphoreType.DMA((7,)),
                        pltpu.SemaphoreType.DMA],
    )(v)


def _adam_update(w, g, m, v):
    m_new = ADAM_B1 * m + (1.0 - ADAM_B1) * g
    v_new = ADAM_B2 * v + (1.0 - ADAM_B2) * (g * g)
    m_hat = m_new / (1.0 - ADAM_B1 ** ADAM_STEP)
    v_hat = v_new / (1.0 - ADAM_B2 ** ADAM_STEP)
    return -ADAM_LR * (m_hat / (jnp.sqrt(v_hat) + ADAM_EPS) + ADAM_WD * w), m_new, v_new


def _adamw(name, w, g, m, v):
    rows, width = w.shape
    tr = rows // 4 if rows % 32 == 0 else rows

    def kern(w_ref, g_ref, m_ref, v_ref, d_ref, mo_ref, vo_ref):
        d_ref[...], mo_ref[...], vo_ref[...] = _adam_update(w_ref[...], g_ref[...], m_ref[...], v_ref[...])

    spec = pl.BlockSpec((tr, width), lambda i: (i, 0))
    return pl.pallas_call(kern, name=name, grid=(rows // tr,), in_specs=[spec] * 4, out_specs=[spec] * 3,
                          out_shape=[_sds((rows, width), F32)] * 3, compiler_params=_params(1))(w, g, m, v)


def _adamw_halves(ws, mines, theirs, ms, vs, c_idx):
    n = len(ws)

    def kern(c_ref, *refs):
        take_mine = pl.program_id(0) == c_ref[0]
        for a in range(n):
            w_ref, mine_ref, theirs_ref, m_ref, v_ref = refs[5 * a:5 * a + 5]
            g_ref, d_ref, mo_ref, vo_ref = refs[5 * n + 4 * a:5 * n + 4 * a + 4]
            g = jnp.where(take_mine, mine_ref[...], theirs_ref[...])
            g_ref[...] = g
            d_ref[...], mo_ref[...], vo_ref[...] = _adam_update(w_ref[...], g, m_ref[...], v_ref[...])

    in_specs, out_specs, out_shape = [], [], []
    for w in ws:
        rows, width = w.shape
        tr = rows // (2 * ADAM_STEPS)
        whole = pl.BlockSpec((tr, width), lambda h, j, c_ref: (ADAM_STEPS * h + j, 0))
        part = pl.BlockSpec((tr, width), lambda h, j, c_ref: (j, 0))
        in_specs += [whole, part, part, whole, whole]
        out_specs += [whole] * 4
        out_shape += [_sds((rows, width), F32)] * 4
    operands = [t for group in zip(ws, mines, theirs, ms, vs) for t in group]
    outs = pl.pallas_call(
        kern, name="adamw_shards", out_shape=out_shape,
        grid_spec=pltpu.PrefetchScalarGridSpec(num_scalar_prefetch=1, grid=(2, ADAM_STEPS), in_specs=in_specs, out_specs=out_specs),
        compiler_params=_params(2),
    )(c_idx.reshape(1), *operands)
    return [outs[4 * a:4 * a + 4] for a in range(n)]


SHARDED = (("w_in", D_MODEL, IN_WIDTH, 1), ("w_uq", Q_RANK, MLA_HEADS * MLA_QK, 1),
           ("w_ukv", KV_RANK, MLA_HEADS * (MLA_NOPE + MLA_V), 1), ("w_o", D_MODEL, D_MODEL, 0),
           ("w_gate", D_MODEL, D_FF, 1), ("w_up", D_MODEL, D_FF, 1), ("w_down", D_FF, D_MODEL, 0))
EARLY = ("w_in", "w_uq", "w_ukv")
LATE = ("w_o", "w_gate", "w_up", "w_down")
FLIPPED = ("w_gate", "w_up")
SMALL = (("norm_mix", D_MODEL), ("q_latent_norm", Q_RANK), ("kv_latent_norm", KV_RANK), ("out_norm_mla", MLA_WIDTH),
         ("out_norm_sb", SB_WIDTH), ("norm_ffn", D_MODEL), ("norm_final", D_MODEL))


def _full_weight(gathered, axis):
    n_sh, k, n = gathered.shape
    return gathered.transpose(1, 0, 2).reshape(k, n_sh * n) if axis == 1 else gathered.reshape(n_sh * k, n)


def _shard_major(g, axis):
    r, c = g.shape
    return g.reshape(r, N_SHARD, c // N_SHARD).transpose(1, 0, 2) if axis == 1 else g.reshape(N_SHARD, r // N_SHARD, c)


def _rot_cols(w):
    hh = MLA_ROPE // 2
    return jnp.concatenate([-w[..., hh:], w[..., :hh]], axis=-1)


def _rot_cols_t(g):
    hh = MLA_ROPE // 2
    return jnp.concatenate([g[..., hh:], -g[..., :hh]], axis=-1)


def _with_transposes(w):
    w.update({name + "_t": t.T for name, t in list(w.items())})
    return w


def _attention_weights(full, small):
    w_in = full["w_in"]
    s0, s1, s2 = Q_RANK, Q_RANK + KV_RANK, Q_RANK + KV_RANK + MLA_ROPE
    uq = full["w_uq"].reshape(Q_RANK, MLA_HEADS, MLA_QK)
    ukv = full["w_ukv"].reshape(KV_RANK, MLA_HEADS, MLA_NOPE + MLA_V)
    w_kr = w_in[:, s1:s2]
    per_tile = ROPE_TILE // MLA_ROPE
    w = _with_transposes({
        "w_cq": w_in[:, :s0], "w_ckv": w_in[:, s0:s1],
        "w_kr4": jnp.tile(w_kr, (1, per_tile)), "w_kr4r": jnp.tile(_rot_cols(w_kr), (1, per_tile)),
        "w_kr8": jnp.tile(w_kr, (1, MLA_HEADS)), "w_kr8r": jnp.tile(_rot_cols(w_kr), (1, MLA_HEADS)),
        "w_sbq": w_in[:, s2:s2 + SB_WIDTH], "w_sbk": w_in[:, s2 + SB_WIDTH:s2 + 2 * SB_WIDTH], "w_sbv": w_in[:, s2 + 2 * SB_WIDTH:],
        "w_qn": uq[..., :MLA_NOPE].reshape(Q_RANK, -1), "w_qr": uq[..., MLA_NOPE:].reshape(Q_RANK, -1),
        "w_qrr": _rot_cols(uq[..., MLA_NOPE:]).reshape(Q_RANK, -1),
        "w_kn": ukv[..., :MLA_NOPE].reshape(KV_RANK, -1), "w_v": ukv[..., MLA_NOPE:].reshape(KV_RANK, -1),
    })
    w.update(g_mix=small["norm_mix"], g_q=small["q_latent_norm"], g_kv=small["kv_latent_norm"], g_a=small["out_norm_mla"],
             g_b=small["out_norm_sb"], g_f=small["norm_ffn"], g_n=small["norm_final"])
    return w


def _ffn_weights(full):
    w = _with_transposes({"w_oa": full["w_o"][:MLA_WIDTH], "w_ob": full["w_o"][MLA_WIDTH:], "w_down": full["w_down"]})
    for name in FLIPPED:
        w[name + "_t"] = full[name]
        w[name] = full[name].T
    return w


def _rope_tables(positions):
    inv_freq = ROPE_THETA ** (-jnp.arange(0, MLA_ROPE, 2, dtype=F32) / MLA_ROPE)
    ang = positions.astype(F32)[:, None] * inv_freq[None, :]
    cos, sin = jnp.cos(ang), jnp.sin(ang)
    return {"cos": jnp.tile(jnp.concatenate([cos, cos], axis=1), (1, MLA_HEADS)),
            "sin": jnp.tile(jnp.concatenate([sin, sin], axis=1), (1, MLA_HEADS))}


def _by_head(g_wide, g_narrow, wide, narrow):
    r = g_wide.shape[0]
    return jnp.concatenate([g_wide.reshape(r, MLA_HEADS, wide), g_narrow.reshape(r, MLA_HEADS, narrow)], axis=-1).reshape(r, -1)


def kernel(x, positions, norm_mix, w_in, q_latent_norm, w_uq, kv_latent_norm, w_ukv, out_norm_mla, out_norm_sb, w_o, norm_ffn, w_gate, w_up, w_down, norm_final, loss_target, m_norm_mix, m_w_in, m_q_latent_norm, m_w_uq, m_kv_latent_norm, m_w_ukv, m_out_norm_mla, m_out_norm_sb, m_w_o, m_norm_ffn, m_w_gate, m_w_up, m_w_down, m_norm_final, v_norm_mix, v_w_in, v_q_latent_norm, v_w_uq, v_kv_latent_norm, v_w_ukv, v_out_norm_mla, v_out_norm_sb, v_w_o, v_norm_ffn, v_w_gate, v_w_up, v_w_down, v_norm_final):
    given = dict(norm_mix=norm_mix, w_in=w_in, q_latent_norm=q_latent_norm, w_uq=w_uq, kv_latent_norm=kv_latent_norm, w_ukv=w_ukv,
                 out_norm_mla=out_norm_mla, out_norm_sb=out_norm_sb, w_o=w_o, norm_ffn=norm_ffn, w_gate=w_gate, w_up=w_up,
                 w_down=w_down, norm_final=norm_final)
    mom_m = dict(norm_mix=m_norm_mix, w_in=m_w_in, q_latent_norm=m_q_latent_norm, w_uq=m_w_uq, kv_latent_norm=m_kv_latent_norm,
                 w_ukv=m_w_ukv, out_norm_mla=m_out_norm_mla, out_norm_sb=m_out_norm_sb, w_o=m_w_o, norm_ffn=m_norm_ffn,
                 w_gate=m_w_gate, w_up=m_w_up, w_down=m_w_down, norm_final=m_norm_final)
    mom_v = dict(norm_mix=v_norm_mix, w_in=v_w_in, q_latent_norm=v_q_latent_norm, w_uq=v_w_uq, kv_latent_norm=v_kv_latent_norm,
                 w_ukv=v_w_ukv, out_norm_mla=v_out_norm_mla, out_norm_sb=v_out_norm_sb, w_o=v_w_o, norm_ffn=v_norm_ffn,
                 w_gate=v_w_gate, w_up=v_w_up, w_down=v_w_down, norm_final=v_norm_final)
    xs = x[0]
    tgt = loss_target[0]
    s = xs.shape[0]
    c_idx = lax.axis_index("c")
    shard_idx = 2 * lax.axis_index("x") + lax.axis_index("y")

    def block2d(t, name):
        t = t.reshape(t.shape[-2:])
        return t.T if name in FLIPPED else t

    shard2d = {name: block2d(given[name], name) for name, *_ in SHARDED}
    local = {name: shard2d[name].astype(BF16) for name, *_ in SHARDED}
    axis_of = {name: 0 if name in FLIPPED else axis for name, _, _, axis in SHARDED}

    def whole(names, gathered):
        return {name: _full_weight(t, axis_of[name]) for name, t in zip(names, gathered)}

    small = {name: given[name].reshape(1, n) for name, n in SMALL}
    w = _attention_weights(whole(EARLY, _allgather_list("allgather_w", [local[name] for name in EARLY])), small)
    tabs = _rope_tables(positions[0])
    msuf, mpre = _sb_masks(min(SB_TK, s))

    u, cq, ckv, cqn, ckvn, qn, qr, kn, vm, kr, sq, sk, sv = _fwd_a(xs, tabs, w)
    o_mla, lse, late = _mla_fwd(qn, qr, kn, kr, vm, [local[name] for name in LATE])
    w.update(_ffn_weights(whole(LATE, late)))
    o_sb, cmat = _sb_fwd(sq, sk, sv, msuf)
    merged, h1, f, gate, up, act = _fwd_b1(xs, o_mla, o_sb, w)
    dh2, loss_part, dg_n = _fwd_b2(h1, act, tgt, w)

    def shards_of(names, grads):
        return [_shard_major(grads[name], axis_of[name]) for name in names]

    def reduced(tag, chip_f32, others):
        mine = _add_chips("add_chips_" + tag, chip_f32, others, shard_idx)
        return tuple(mine), tuple(_swap_result("swap_result_" + tag, mine))

    dgate, dup, dh1, do_mla, do_sb, dg_f, dg_a, dg_b = _bwd_b(dh2, gate, up, h1, o_mla, o_sb, w)
    late_gs = shards_of(LATE, {
        "w_o": _tn_matmul("dw_o", merged, dh1), "w_gate": _tn_matmul("dw_gate", dgate, f),
        "w_up": _tn_matmul("dw_up", dup, f), "w_down": _tn_matmul("dw_down", act, dh2)})
    (dsq, dsk, dsv), late_got = _sb_bwd(sq, sk, sv, do_sb, cmat, msuf, mpre, late_gs)
    late_f32, late_bf16 = _add_sibling("add_sibling_late", late_gs, late_got, c_idx)
    (dqn, dqr, dkn, dkr, dvm), late_others = _mla_bwd(qn, qr, kn, kr, vm, o_mla, do_mla, lse, late_bf16)
    mine_late, theirs_late = reduced("late", late_f32, late_others)
    dx, a1, a2, dcq, dckv, dkrc, dkrs, dg_q, dg_kv, dg_mix = _bwd_a(xs, dh1, cq, ckv, dqn, dqr, dkn, dvm, dkr, dsq, dsk, dsv, tabs, w)

    g_cq, g_ckv, g_krc, g_krs, g_sq, g_sk, g_sv = _tn_multi("dw_in", u, [dcq, dckv, dkrc, dkrs, dsq, dsk, dsv])
    g_qn, g_qr1, g_qr2 = _tn_multi("dw_uq", cqn, [dqn, a1, a2])
    g_kn, g_v = _tn_multi("dw_ukv", ckvn, [dkn, dvm])
    slots = lambda g: g.reshape(g.shape[0], MLA_HEADS, MLA_ROPE)
    g_kr = jnp.sum(slots(g_krc), axis=1) + _rot_cols_t(jnp.sum(slots(g_krs), axis=1))
    g_qr = (slots(g_qr1) + _rot_cols_t(slots(g_qr2))).reshape(Q_RANK, -1)
    early_gs = shards_of(EARLY, {
        "w_in": jnp.concatenate([g_cq, g_ckv, g_kr, g_sq, g_sk, g_sv], axis=1),
        "w_uq": _by_head(g_qn, g_qr, MLA_NOPE, MLA_ROPE),
        "w_ukv": _by_head(g_kn * MLA_DK_SCALE, g_v, MLA_NOPE, MLA_V)})
    early_f32, early_bf16 = _add_sibling("add_sibling_early", early_gs, _swap_halves("swap_halves_early", early_gs), c_idx)
    mine_early, theirs_early = reduced("early", early_f32, _chip_scatter(early_bf16))
    halves = dict(zip(EARLY + LATE, zip(mine_early + mine_late, theirs_early + theirs_late)))

    small_parts = jnp.concatenate([dg_mix, dg_q, dg_kv, dg_a, dg_b, dg_f, dg_n, loss_part], axis=1)
    small_sum = _allreduce_small(jnp.broadcast_to(small_parts, (8, small_parts.shape[1])))
    small_g, loss = small_sum[0:1, :-LANES], small_sum[0, -LANES]

    g_out, d_out, m_out, v_out = {}, {}, {}, {}
    names = [name for name, *_ in SHARDED]
    updated = _adamw_halves([shard2d[name] for name in names], [halves[name][0] for name in names], [halves[name][1] for name in names],
                            [block2d(mom_m[name], name) for name in names], [block2d(mom_v[name], name) for name in names], c_idx)
    for name, outs in zip(names, updated):
        shape = given[name].shape
        g_out[name], d_out[name], m_out[name], v_out[name] = ((t.T if name in FLIPPED else t).reshape(shape) for t in outs)
    cat = lambda src: jnp.concatenate([src[name].reshape(1, n) for name, n in SMALL], axis=1)
    d, mn, vn = _adamw("adamw_small", cat(given), small_g, cat(mom_m), cat(mom_v))
    off = 0
    for name, n in SMALL:
        shape = given[name].shape
        g_out[name], d_out[name], m_out[name], v_out[name] = (t[:, off:off + n].reshape(shape) for t in (small_g, d, mn, vn))
        off += n

    order = ["norm_mix", "w_in", "q_latent_norm", "w_uq", "kv_latent_norm", "w_ukv", "out_norm_mla", "out_norm_sb", "w_o",
             "norm_ffn", "w_gate", "w_up", "w_down", "norm_final"]
    return (loss, dx[None], *[g_out[n] for n in order], *[d_out[n] for n in order], *[m_out[n] for n in order],
            *[v_out[n] for n in order])
```

```python
import functools
import math

import jax
import jax.numpy as jnp
from jax import lax
from jax.experimental import pallas as pl
from jax.experimental.pallas import tpu as pltpu

F32 = jnp.float32
BF16 = jnp.bfloat16
MESH = pl.DeviceIdType.MESH

D_MODEL = 1024
EPS = 1e-6
MLA_HEADS = 8
MLA_NOPE = 64
MLA_ROPE = 32
MLA_V = 64
MLA_QK = MLA_NOPE + MLA_ROPE
Q_RANK = 256
KV_RANK = 128
ROPE_THETA = 10000.0
SB_HEADS = 8
SB_DIM = 64
MLA_WIDTH = MLA_HEADS * MLA_V
SB_WIDTH = SB_HEADS * SB_DIM
D_FF = 2816
IN_WIDTH = Q_RANK + KV_RANK + MLA_ROPE + 3 * SB_WIDTH

ADAM_LR = 0.001
ADAM_B1 = 0.9
ADAM_B2 = 0.999
ADAM_EPS = 1e-08
ADAM_WD = 0.01
ADAM_STEP = 10

N_SHARD = 4
LANES = 128
ROPE_TILE = LANES
VMEM_LIMIT = 56 * 1024 * 1024
TN_ACC_BYTES = 6 * 1024 * 1024 + 512 * 1024
NEG = -1e30
MLA_SCALE = 1.0 / math.sqrt(MLA_QK)
MLA_DK_SCALE = math.log(2.0)
MLA_QSCALE = MLA_SCALE * math.log2(math.e)
SB_SKIP = 110.0

ROW_TILE = 512
ROW_TILE_ELEMENTWISE = 256
MLA_TQ = 1024
SB_TQ = 512
MLA_TK = 1024
MLA_BWD_TK = 1024
MLA_DIAG_TK = 256
SB_TK = 256
TN_TS = 2048
ADAM_STEPS = 4


def _dot(a, b):
    return jnp.dot(a, b, preferred_element_type=F32)


def _dot_nt(a, b):
    return lax.dot_general(a, b, (((1,), (1,)), ((), ())), preferred_element_type=F32)


def _dot_tn(a, b):
    return lax.dot_general(a, b, (((0,), (0,)), ((), ())), preferred_element_type=F32)


def _params(n_grid, vmem=VMEM_LIMIT):
    return pltpu.CompilerParams(dimension_semantics=("arbitrary",) * n_grid, vmem_limit_bytes=vmem)


def _rms(x):
    r = lax.rsqrt(jnp.mean(x * x, axis=-1, keepdims=True) + EPS)
    return x * r, r


def _rms_bwd(n, r, g, dy):
    dn = dy * g
    dx = r * (dn - n * jnp.mean(dn * n, axis=-1, keepdims=True))
    return dx, jnp.sum(dy * n, axis=0, keepdims=True)


def _accumulate(ref, val, step):
    @pl.when(step == 0)
    def _():
        ref[...] = val

    @pl.when(step != 0)
    def _():
        ref[...] += val


def _rowwise(name, body, rows, consts, row_out, acc_out, tm):
    n_rows = rows[0].shape[0]
    tm = min(tm, n_rows)
    nr, nc, no = len(rows), len(consts), len(row_out)

    def kern(*refs):
        body(refs[:nr], refs[nr:nr + nc], refs[nr + nc:nr + nc + no], refs[nr + nc + no:], pl.program_id(0))

    in_specs = [pl.BlockSpec((tm, a.shape[1]), lambda i: (i, 0)) for a in rows]
    in_specs += [pl.BlockSpec(a.shape, lambda i: (0, 0), pipeline_mode=pl.Buffered(1)) for a in consts]
    out_specs = [pl.BlockSpec((tm, s.shape[1]), lambda i: (i, 0)) for s in row_out]
    out_specs += [pl.BlockSpec(s.shape, lambda i: (0, 0)) for s in acc_out]
    return pl.pallas_call(
        kern, name=name, grid=(n_rows // tm,), in_specs=in_specs, out_specs=out_specs,
        out_shape=list(row_out) + list(acc_out), compiler_params=_params(1),
    )(*rows, *consts)


def _sds(shape, dtype):
    return jax.ShapeDtypeStruct(shape, dtype)


def _fwd_a(x, tabs, w):
    s = x.shape[0]

    def body(r, c, o, a, step):
        x_ref, cos_ref, sin_ref = r
        gmix, wcq, wckv, wkr, wkrr, wsq, wsk, wsv, gq, wqn, wqr, wqrr, gkv, wkn, wv = c
        u_o, cq_o, ckv_o, cqn_o, ckvn_o, qn_o, qr_o, kn_o, v_o, kr_o, sq_o, sk_o, sv_o = o
        cos, sin = cos_ref[...], sin_ref[...]
        n, _ = _rms(x_ref[...])
        u = (n * gmix[...]).astype(BF16)
        u_o[...] = u
        cq = _dot(u, wcq[...])
        ckv = _dot(u, wckv[...])
        kr_o[...] = (_dot(u, wkr[...]) * cos[:, :ROPE_TILE] + _dot(u, wkrr[...]) * sin[:, :ROPE_TILE]).astype(BF16)
        sq_o[...] = _dot(u, wsq[...]).astype(BF16)
        sk_o[...] = _dot(u, wsk[...]).astype(BF16)
        sv_o[...] = _dot(u, wsv[...]).astype(BF16)
        cq_o[...] = cq
        ckv_o[...] = ckv
        nq, _ = _rms(cq)
        cqn = (nq * gq[...]).astype(BF16)
        cqn_o[...] = cqn
        qn_o[...] = (_dot(cqn, wqn[...]) * MLA_QSCALE).astype(BF16)
        qr_o[...] = ((_dot(cqn, wqr[...]) * cos + _dot(cqn, wqrr[...]) * sin) * MLA_QSCALE).astype(BF16)
        nkv, _ = _rms(ckv)
        ckvn = (nkv * gkv[...]).astype(BF16)
        ckvn_o[...] = ckvn
        kn_o[...] = _dot(ckvn, wkn[...]).astype(BF16)
        v_o[...] = _dot(ckvn, wv[...]).astype(BF16)

    outs = [
        _sds((s, D_MODEL), BF16), _sds((s, Q_RANK), F32), _sds((s, KV_RANK), F32), _sds((s, Q_RANK), BF16),
        _sds((s, KV_RANK), BF16), _sds((s, MLA_HEADS * MLA_NOPE), BF16), _sds((s, MLA_HEADS * MLA_ROPE), BF16),
        _sds((s, MLA_HEADS * MLA_NOPE), BF16), _sds((s, MLA_WIDTH), BF16), _sds((s, ROPE_TILE), BF16),
        _sds((s, SB_WIDTH), BF16), _sds((s, SB_WIDTH), BF16), _sds((s, SB_WIDTH), BF16),
    ]
    consts = [w["g_mix"], w["w_cq"], w["w_ckv"], w["w_kr4"], w["w_kr4r"], w["w_sbq"], w["w_sbk"], w["w_sbv"], w["g_q"],
              w["w_qn"], w["w_qr"], w["w_qrr"], w["g_kv"], w["w_kn"], w["w_v"]]
    return _rowwise("fwd_a", body, [x, tabs["cos"], tabs["sin"]], consts, outs, [], ROW_TILE)


def _fwd_b1(x, o_mla, o_sb, w):
    s = x.shape[0]

    def body(r, c, o, a, step):
        x_ref, oa_ref, ob_ref = r
        ga, gb, woa, wob, gf, wg, wu = c
        mg_o, h1_o, f_o, gate_o, up_o, act_o = o
        na, _ = _rms(oa_ref[...])
        nb, _ = _rms(ob_ref[...])
        ma = (na * ga[...]).astype(BF16)
        mb = (nb * gb[...]).astype(BF16)
        mg_o[:, :MLA_WIDTH] = ma
        mg_o[:, MLA_WIDTH:] = mb
        h1 = x_ref[...] + _dot(ma, woa[...]) + _dot(mb, wob[...])
        h1_o[...] = h1
        nf, _ = _rms(h1)
        f = (nf * gf[...]).astype(BF16)
        f_o[...] = f
        gate = _dot(f, wg[...])
        up = _dot(f, wu[...])
        gate_o[...] = gate.astype(BF16)
        up_o[...] = up.astype(BF16)
        act_o[...] = (gate * (1.0 / (1.0 + jnp.exp(-gate))) * up).astype(BF16)

    outs = [_sds((s, D_MODEL), BF16), _sds((s, D_MODEL), F32), _sds((s, D_MODEL), BF16), _sds((s, D_FF), BF16),
            _sds((s, D_FF), BF16), _sds((s, D_FF), BF16)]
    consts = [w["g_a"], w["g_b"], w["w_oa"], w["w_ob"], w["g_f"], w["w_gate"], w["w_up"]]
    return _rowwise("fwd_b1", body, [x, o_mla, o_sb], consts, outs, [], ROW_TILE)


def _fwd_b2(h1, act, tgt, w):
    s = h1.shape[0]

    def body(r, c, o, a, step):
        h1_ref, act_ref, t_ref = r
        wd, gn = c
        (dh2_o,) = o
        loss_o, dgn_o = a
        h2 = h1_ref[...] + _dot(act_ref[...], wd[...])
        n2, r2 = _rms(h2)
        err = n2 * gn[...] - t_ref[...]
        part = jnp.sum(jnp.sum(err * err, axis=1, keepdims=True), axis=0, keepdims=True) * (0.5 / D_MODEL)
        _accumulate(loss_o, jnp.broadcast_to(part, (1, LANES)), step)
        dh2, dgn = _rms_bwd(n2, r2, gn[...], err * (1.0 / D_MODEL))
        dh2_o[...] = dh2
        _accumulate(dgn_o, dgn, step)

    return _rowwise("fwd_b2", body, [h1, act, tgt], [w["w_down"], w["g_n"]], [_sds((s, D_MODEL), F32)],
                    [_sds((1, LANES), F32), _sds((1, D_MODEL), F32)], ROW_TILE)


def _bwd_b(dh2, gate, up, h1, o_mla, o_sb, w):
    s = h1.shape[0]

    def body(r, c, o, a, step):
        dh2_ref, gate_ref, up_ref, h1_ref, oa_ref, ob_ref = r
        wdt, wgt, wut, gf, woat, wobt, ga, gb = c
        dgate_o, dup_o, dh1_o, doa_o, dob_o = o
        dgf_o, dga_o, dgb_o = a
        dh2 = dh2_ref[...]
        dact = _dot(dh2.astype(BF16), wdt[...])
        gate = gate_ref[...].astype(F32)
        sig = 1.0 / (1.0 + jnp.exp(-gate))
        dup = (dact * (gate * sig)).astype(BF16)
        dgate = (dact * up_ref[...].astype(F32) * (sig * (1.0 + gate * (1.0 - sig)))).astype(BF16)
        dup_o[...] = dup
        dgate_o[...] = dgate
        df = _dot(dgate, wgt[...]) + _dot(dup, wut[...])
        nf, rf = _rms(h1_ref[...])
        dres, dgf = _rms_bwd(nf, rf, gf[...], df)
        dh1 = dh2 + dres
        dh1_o[...] = dh1
        dh1b = dh1.astype(BF16)
        na, ra = _rms(oa_ref[...])
        doa, dga = _rms_bwd(na, ra, ga[...], _dot(dh1b, woat[...]))
        nb, rb = _rms(ob_ref[...])
        dob, dgb = _rms_bwd(nb, rb, gb[...], _dot(dh1b, wobt[...]))
        doa_o[...] = doa
        dob_o[...] = dob
        _accumulate(dgf_o, dgf, step)
        _accumulate(dga_o, dga, step)
        _accumulate(dgb_o, dgb, step)

    consts = [w["w_down_t"], w["w_gate_t"], w["w_up_t"], w["g_f"], w["w_oa_t"], w["w_ob_t"], w["g_a"], w["g_b"]]
    outs = [_sds((s, D_FF), BF16), _sds((s, D_FF), BF16), _sds((s, D_MODEL), F32), _sds((s, MLA_WIDTH), F32), _sds((s, SB_WIDTH), F32)]
    accs = [_sds((1, D_MODEL), F32), _sds((1, MLA_WIDTH), F32), _sds((1, SB_WIDTH), F32)]
    return _rowwise("bwd_b", body, [dh2, gate, up, h1, o_mla, o_sb], consts, outs, accs, ROW_TILE_ELEMENTWISE)


def _fold_pairs(t):
    return jnp.concatenate([t[:, :LANES] + t[:, LANES:2 * LANES], t[:, 2 * LANES:3 * LANES] + t[:, 3 * LANES:]], axis=1)


def _bwd_a(x, dh1, cq, ckv, dqn, dqr, dkn, dvm, dkr, dsq, dsk, dsv, tabs, w):
    s = x.shape[0]

    def body(r, c, o, a, step):
        x_ref, dh1_ref, cq_ref, ckv_ref, dqn_ref, dqr_ref, dkn_ref, dvm_ref, dkr_ref, dsq_ref, dsk_ref, dsv_ref, cos_ref, sin_ref = r
        wqnt, wqrt, wqrrt, gq, wknt, wvt, gkv, wcqt, wckvt, wkrt, wkrrt, wsqt, wskt, wsvt, gmix = c
        dx_o, a1_o, a2_o, dcq_o, dckv_o, dkrc_o, dkrs_o = o
        dgq_o, dgkv_o, dgmix_o = a
        cos, sin = cos_ref[...], sin_ref[...]
        dqr = _fold_pairs(dqr_ref[...].astype(F32))
        a1 = (dqr * cos).astype(BF16)
        a2 = (dqr * sin).astype(BF16)
        a1_o[...] = a1
        a2_o[...] = a2
        nq, rq = _rms(cq_ref[...])
        dcqn = _dot(dqn_ref[...], wqnt[...]) + _dot(a1, wqrt[...]) + _dot(a2, wqrrt[...])
        dcq, dgq = _rms_bwd(nq, rq, gq[...], dcqn)
        nkv, rkv = _rms(ckv_ref[...])
        dckvn = _dot(dkn_ref[...], wknt[...]) + _dot(dvm_ref[...], wvt[...])
        dckv, dgkv = _rms_bwd(nkv, rkv, gkv[...], dckvn)
        dkr = _fold_pairs(dkr_ref[...].astype(F32))
        dcq_b = dcq.astype(BF16)
        dckv_b = dckv.astype(BF16)
        dkrc = (dkr * cos).astype(BF16)
        dkrs = (dkr * sin).astype(BF16)
        dcq_o[...] = dcq_b
        dckv_o[...] = dckv_b
        dkrc_o[...] = dkrc
        dkrs_o[...] = dkrs
        du = (_dot(dcq_b, wcqt[...]) + _dot(dckv_b, wckvt[...]) + _dot(dkrc, wkrt[...]) + _dot(dkrs, wkrrt[...])
              + _dot(dsq_ref[...], wsqt[...]) + _dot(dsk_ref[...], wskt[...]) + _dot(dsv_ref[...], wsvt[...]))
        nx, rx = _rms(x_ref[...])
        dres, dgmix = _rms_bwd(nx, rx, gmix[...], du)
        dx_o[...] = dh1_ref[...] + dres
        _accumulate(dgq_o, dgq, step)
        _accumulate(dgkv_o, dgkv, step)
        _accumulate(dgmix_o, dgmix, step)

    consts = [w["w_qn_t"], w["w_qr_t"], w["w_qrr_t"], w["g_q"], w["w_kn_t"], w["w_v_t"], w["g_kv"], w["w_cq_t"], w["w_ckv_t"],
              w["w_kr8_t"], w["w_kr8r_t"], w["w_sbq_t"], w["w_sbk_t"], w["w_sbv_t"], w["g_mix"]]
    rope_w = MLA_HEADS * MLA_ROPE
    outs = [_sds((s, D_MODEL), F32), _sds((s, rope_w), BF16), _sds((s, rope_w), BF16), _sds((s, Q_RANK), BF16),
            _sds((s, KV_RANK), BF16), _sds((s, rope_w), BF16), _sds((s, rope_w), BF16)]
    accs = [_sds((1, Q_RANK), F32), _sds((1, KV_RANK), F32), _sds((1, D_MODEL), F32)]
    rows = [x, dh1, cq, ckv, dqn, dqr, dkn, dvm, dkr, dsq, dsk, dsv, tabs["cos"], tabs["sin"]]
    return _rowwise("bwd_a", body, rows, consts, outs, accs, ROW_TILE)


def _tn_multi(name, x, ys):
    s, k = x.shape
    ts = min(TN_TS, s)
    n_y = len(ys)

    def kern(*refs):
        step = pl.program_id(0)
        xb = refs[0][...].astype(BF16)
        for j in range(n_y):
            _accumulate(refs[1 + n_y + j], _dot_tn(xb, refs[1 + j][...].astype(BF16)), step)

    return pl.pallas_call(
        kern, name=name, grid=(s // ts,),
        in_specs=[pl.BlockSpec((ts, k), lambda i: (i, 0))] + [pl.BlockSpec((ts, y.shape[1]), lambda i: (i, 0)) for y in ys],
        out_specs=[pl.BlockSpec((k, y.shape[1]), lambda i: (0, 0)) for y in ys],
        out_shape=[_sds((k, y.shape[1]), F32) for y in ys], compiler_params=_params(1),
    )(x, *ys)


def _tn_tile(k, n):
    if n % LANES or k * n * 4 <= TN_ACC_BYTES:
        return n
    units = n // LANES
    best = 1
    for d in range(1, units + 1):
        if units % d == 0 and k * d * LANES * 4 <= TN_ACC_BYTES:
            best = d
    return best * LANES


def _tn_matmul(name, x, y):
    s, k = x.shape
    n = y.shape[1]
    ts = min(TN_TS, s)
    tn = _tn_tile(k, n)

    def kern(x_ref, y_ref, o_ref):
        step = pl.program_id(1)
        _accumulate(o_ref, _dot_tn(x_ref[...].astype(BF16), y_ref[...].astype(BF16)), step)

    return pl.pallas_call(
        kern, name=name, grid=(n // tn, s // ts),
        in_specs=[pl.BlockSpec((ts, k), lambda j, i: (i, 0)), pl.BlockSpec((ts, tn), lambda j, i: (i, j))],
        out_specs=pl.BlockSpec((k, tn), lambda j, i: (0, j)), out_shape=_sds((k, n), F32), compiler_params=_params(2),
    )(x, y)


def _lanes(rows, lo, width):
    lane = lax.broadcasted_iota(jnp.int32, (rows, LANES), 1)
    return jnp.logical_and(lane >= lo, lane < lo + width)


def _keep(mask, t):
    return jnp.where(mask, t, jnp.zeros_like(t))


def _mla_qcat(qn_ref, qr_ref, rope_lo, half, rows):
    qn = _keep(_lanes(rows, MLA_NOPE * half, MLA_NOPE), qn_ref[...])
    qr = _keep(_lanes(rows, rope_lo, MLA_ROPE), qr_ref[...])
    return jnp.concatenate([qn, qr], axis=1)


def _diag_mask(rows, width, row0, col0):
    row = lax.broadcasted_iota(jnp.int32, (rows, width), 0)
    col = lax.broadcasted_iota(jnp.int32, (rows, width), 1)
    return col + (col0 - row0) <= row


def _mla_fwd(qn, qr, kn, kr, v, riders=(), tq=MLA_TQ, tk=MLA_TK, td=MLA_TQ):
    s = qn.shape[0]
    tq, tk, td = min(tq, s), min(tk, s), min(td, s)
    ratio = tq // tk

    n_ride = len(riders)
    n_pairs = MLA_HEADS // 2

    def kern(qn_ref, qr_ref, kn_ref, kr_ref, v_ref, *rest):
        o_ref, lse_ref = rest[n_ride:n_ride + 2]
        g = pl.program_id(0)
        i = pl.program_id(1)
        if n_ride:
            send, forward, finish = _gather_steps(rest[:n_ride], rest[n_ride + 2:2 * n_ride + 2], *rest[2 * n_ride + 2:])
            pl.when(jnp.logical_and(g == 0, i == 0))(send)
            pl.when(jnp.logical_and(g == 1, i == 0))(forward)
        qcat = [_mla_qcat(qn_ref, qr_ref, MLA_ROPE * (2 * (g % 2) + half), half, tq) for half in range(2)]

        def block(k0, width, carry, row0, masked, half):
            m, l, acc = (c[row0:] for c in carry)
            ks = pl.ds(pl.multiple_of(k0, width), width)
            kcat = jnp.concatenate([kn_ref[ks, :], kr_ref[ks, :]], axis=1)
            sc = _dot_nt(qcat[half][row0:], kcat)
            if masked:
                sc = jnp.where(_diag_mask(tq - row0, width, row0, row0), sc, NEG)
            m_new = jnp.maximum(m, jnp.max(sc, axis=1, keepdims=True))
            p = jnp.exp2(sc - m_new)
            alpha = jnp.exp2(m - m_new)
            l = alpha * l + jnp.sum(p, axis=1, keepdims=True)
            acc = alpha * acc + _dot(p.astype(BF16), v_ref[ks, :])
            new = (m_new, l, acc)
            return new if row0 == 0 else tuple(jnp.concatenate([c[:row0], n], axis=0) for c, n in zip(carry, new))

        def both(k0, width, carries, row0, masked):
            return tuple(block(k0, width, carries[half], row0, masked, half) for half in range(2))

        init = (jnp.full((tq, 1), NEG, F32), jnp.zeros((tq, 1), F32), jnp.zeros((tq, LANES), F32))
        carries = lax.fori_loop(0, i * ratio, lambda kb, c: both(kb * tk, tk, c, 0, False), (init, init))
        for row0 in range(0, tq, td):
            carries = both(i * tq + row0, td, carries, row0, True)
        for half in range(2):
            m, l, acc = carries[half]
            out = _keep(_lanes(tq, MLA_V * half, MLA_V), acc / l)
            lse = _keep(_lanes(tq, MLA_ROPE * half, MLA_ROPE), jnp.broadcast_to(m + jnp.log2(l), (tq, LANES)))
            if half == 0:
                o_ref[...] = out
                lse_ref[...] = lse
            else:
                o_ref[...] += out
                lse_ref[...] += lse
        if n_ride:
            pl.when(jnp.logical_and(g == n_pairs - 1, i == s // tq - 1))(finish)

    qblk = pl.BlockSpec((tq, LANES), lambda g, i: (i, g))
    full = pl.BlockSpec((s, LANES), lambda g, i: (0, g))
    outs = pl.pallas_call(
        kern, name="mla_fwd", grid=(n_pairs, s // tq),
        in_specs=[qblk, pl.BlockSpec((tq, LANES), lambda g, i: (i, g // 2)), full, pl.BlockSpec((s, LANES), lambda g, i: (0, 0)), full]
        + [HBM_SPEC] * n_ride,
        out_specs=[qblk, qblk] + [HBM_SPEC] * n_ride,
        out_shape=[_sds((s, MLA_WIDTH), F32), _sds((s, n_pairs * LANES), F32)] + [_sds((N_SHARD,) + a.shape, a.dtype) for a in riders],
        scratch_shapes=_gather_sems(n_ride) if n_ride else [], compiler_params=_params(2),
    )(qn, qr, kn, kr, v, *riders)
    return outs[0], outs[1], outs[2:]


def _mla_bwd(qn, qr, kn, kr, v, o, do, lse, riders=(), tq=MLA_TQ, tk=MLA_BWD_TK, td=MLA_DIAG_TK):
    s = qn.shape[0]
    tq, tk, td = min(tq, s), min(tk, s), min(td, s)
    ratio = tq // tk

    n_ride = len(riders)
    n_pairs = MLA_HEADS // 2

    def kern(qn_ref, qr_ref, kn_ref, kr_ref, v_ref, o_ref, do_ref, lse_ref, *rest):
        dqn_ref, dqr_ref, dkn_out, dkr_out, dv_out = rest[n_ride:n_ride + 5]
        dkn_ref, dkr_ref, dv_ref = rest[2 * n_ride + 5:2 * n_ride + 8]
        g = pl.program_id(0)
        i = pl.program_id(1)
        if n_ride:
            start, finish = _scatter_steps(rest[:n_ride], rest[n_ride + 5:2 * n_ride + 5], *rest[2 * n_ride + 8:])
            pl.when(jnp.logical_and(g == 0, i == 0))(start)

        @pl.when(i == 0)
        def _():
            dkn_ref[...] = jnp.zeros_like(dkn_ref)
            dkr_ref[...] = jnp.zeros_like(dkr_ref)
            dv_ref[...] = jnp.zeros_like(dv_ref)

        for half in range(2):
            rope_lo = MLA_ROPE * (2 * (g % 2) + half)
            qcat = _mla_qcat(qn_ref, qr_ref, rope_lo, half, tq)
            mine = _lanes(tq, MLA_V * half, MLA_V)
            do_f = _keep(mine, do_ref[...])
            do_b = do_f.astype(BF16)
            delta = jnp.sum(do_f * o_ref[...], axis=1, keepdims=True)
            lse_v = lse_ref[:, MLA_ROPE * half:MLA_ROPE * half + 1]

            def block(k0, width, dq_acc, row0, masked, qcat=qcat, do_b=do_b, delta=delta, lse_v=lse_v):
                ks = pl.ds(pl.multiple_of(k0, width), width)
                kcat = jnp.concatenate([kn_ref[ks, :], kr_ref[ks, :]], axis=1)
                qc, dob = qcat[row0:], do_b[row0:]
                p = jnp.exp2(_dot_nt(qc, kcat) - lse_v[row0:])
                if masked:
                    p = jnp.where(_diag_mask(tq - row0, width, row0, row0), p, 0.0)
                ds = (p * (_dot_nt(dob, v_ref[ks, :]) - delta[row0:])).astype(BF16)
                dv_ref[ks, :] += _dot_tn(p.astype(BF16), dob)
                dkc = _dot_tn(ds, qc)
                dkn_ref[ks, :] += dkc[:, :LANES]
                dkr_ref[ks, :] += dkc[:, LANES:]
                new = dq_acc[row0:] + _dot(ds, kcat)
                return new if row0 == 0 else jnp.concatenate([dq_acc[:row0], new], axis=0)

            acc = lax.fori_loop(0, i * ratio, lambda kb, c, block=block: block(kb * tk, tk, c, 0, False),
                                jnp.zeros((tq, 2 * LANES), F32))
            for row0 in range(0, tq, td):
                acc = block(i * tq + row0, td, acc, row0, True)
            dqn = _keep(_lanes(tq, MLA_NOPE * half, MLA_NOPE), acc[:, :LANES] * MLA_SCALE)
            dqr = _keep(_lanes(tq, rope_lo, MLA_ROPE), acc[:, LANES:] * MLA_SCALE)
            if half == 0:
                dqn_ref[...] = dqn.astype(BF16)
                dqr_ref[...] = dqr.astype(BF16)
            else:
                dqn_ref[...] += dqn.astype(BF16)
                dqr_ref[...] += dqr.astype(BF16)

        @pl.when(i == s // tq - 1)
        def _():
            dkn_out[...] = (dkn_ref[...] * MLA_DK_SCALE).astype(BF16)
            dkr_out[...] = (dkr_ref[...] * MLA_DK_SCALE).astype(BF16)
            dv_out[...] = dv_ref[...].astype(BF16)

        if n_ride:
            pl.when(jnp.logical_and(g == n_pairs - 1, i == s // tq - 1))(finish)

    qblk = pl.BlockSpec((tq, LANES), lambda g, i: (i, g))
    full = pl.BlockSpec((s, LANES), lambda g, i: (0, g))
    once = lambda spec_map: pl.BlockSpec((s, LANES), spec_map, pipeline_mode=pl.Buffered(1))
    wide = _sds((s, n_pairs * LANES), BF16)
    outs = pl.pallas_call(
        kern, name="mla_bwd", grid=(n_pairs, s // tq),
        in_specs=[qblk, pl.BlockSpec((tq, LANES), lambda g, i: (i, g // 2)), once(lambda g, i: (0, g)), once(lambda g, i: (0, 0)),
                  once(lambda g, i: (0, g)), qblk, qblk, qblk] + [HBM_SPEC] * n_ride,
        out_specs=[qblk, qblk, full, full, full] + [HBM_SPEC] * n_ride,
        out_shape=[wide] * 5 + [_sds((N_SHARD - 1,) + p.shape[1:], p.dtype) for p in riders],
        scratch_shapes=[pltpu.VMEM((s, LANES), F32)] * 3 + (_scatter_sems(n_ride) if n_ride else []), compiler_params=_params(2),
    )(qn, qr, kn, kr, v, o, do, lse, *riders)
    return outs[:5], outs[5:]


def _sb_masks(tk):
    j = lax.broadcasted_iota(jnp.int32, (tk, tk), 0)
    c = lax.broadcasted_iota(jnp.int32, (tk, tk), 1)
    return (j > c).astype(BF16), (j < c).astype(BF16)


def _sb_scores(qs, kk, msuf, strict):
    z = _dot_nt(qs, kk)
    lom = -(jnp.maximum(z, 0.0) + jnp.log(1.0 + jnp.exp(-jnp.abs(z))))
    if strict is not None:
        lom = jnp.where(strict, lom, 0.0)
    return z, lom, _dot(lom.astype(BF16), msuf)


def _sb_strict(tq, tk, d):
    row = lax.broadcasted_iota(jnp.int32, (tq, tk), 0)
    col = lax.broadcasted_iota(jnp.int32, (tq, tk), 1)
    return col + d * tk < row


def _sb_fwd(q, k, v, msuf, tq=SB_TQ, tk=SB_TK):
    s = q.shape[0]
    tq, tk = min(tq, s), min(tk, s)
    ratio = tq // tk

    def kern(q_ref, k_ref, v_ref, m_ref, o_ref, c_ref):
        i = pl.program_id(1)
        msf = m_ref[...]
        lane = lax.broadcasted_iota(jnp.int32, (tq, LANES), 1)
        mine = [_lanes(tq, SB_DIM * half, SB_DIM) for half in range(2)]
        qs = [_keep(m, q_ref[...]) * 0.125 for m in mine]

        def block(kb, carry, dd, half):
            c, acc, cm = carry
            ks = pl.ds(pl.multiple_of(kb * tk, tk), tk)
            strict = None if dd is None else _sb_strict(tq, tk, dd)
            z, lom, suf = _sb_scores(qs[half], k_ref[ks, :], msf, strict)
            a = jnp.exp(z + lom + (suf + c))
            if strict is not None:
                a = jnp.where(strict, a, 0.0)
            acc = acc + _dot(a.astype(BF16), v_ref[ks, :])
            cm = jnp.where(lane == kb, c, cm)
            return c + jnp.sum(lom, axis=1, keepdims=True), acc, cm

        init = (jnp.zeros((tq, 1), F32), jnp.zeros((tq, LANES), F32), jnp.full((tq, LANES), NEG, F32))
        carries = [init, init]
        for dd in range(ratio - 1, -1, -1):
            carries = [block(i * ratio + dd, carries[half], dd, half) for half in range(2)]

        def live(st):
            return jnp.logical_and(st[0] >= 0, jnp.maximum(jnp.max(st[1][0]), jnp.max(st[2][0])) > -SB_SKIP)

        def step(st):
            return (st[0] - 1, block(st[0], st[1], None, 0), block(st[0], st[2], None, 1))

        _, done0, done1 = lax.while_loop(live, step, (i * ratio - 1, carries[0], carries[1]))
        o_ref[...] = _keep(mine[0], done0[1]) + _keep(mine[1], done1[1])
        c_ref[:, :LANES] = done0[2]
        c_ref[:, LANES:] = done1[2]

    qblk = lambda n: pl.BlockSpec((tq, n), lambda g, i: (i, g))
    full = pl.BlockSpec((s, LANES), lambda g, i: (0, g))
    return pl.pallas_call(
        kern, name="sb_fwd", grid=(SB_HEADS // 2, s // tq),
        in_specs=[qblk(LANES), full, full, pl.BlockSpec((tk, tk), lambda g, i: (0, 0))],
        out_specs=[qblk(LANES), qblk(2 * LANES)],
        out_shape=[_sds((s, SB_WIDTH), F32), _sds((s, SB_HEADS * LANES), F32)], compiler_params=_params(2),
    )(q, k, v, msuf)


def _sb_bwd(q, k, v, do, cmat, msuf, mpre, riders=(), tq=SB_TQ, tk=SB_TK):
    s = q.shape[0]
    tq, tk = min(tq, s), min(tk, s)
    ratio = tq // tk

    n_ride = len(riders)
    n_pairs = SB_HEADS // 2

    def kern(q_ref, k_ref, v_ref, do_ref, c_ref, ms_ref, mp_ref, *rest):
        dq_ref, dk_out, dv_out = rest[n_ride:n_ride + 3]
        dk_ref, dv_ref = rest[2 * n_ride + 3:2 * n_ride + 5]
        i = pl.program_id(1)
        if n_ride:
            start, finish = _swap_steps(rest[:n_ride], rest[n_ride + 3:2 * n_ride + 3], *rest[2 * n_ride + 5:])
            pl.when(jnp.logical_and(pl.program_id(0) == 0, i == 0))(start)

        @pl.when(i == 0)
        def _():
            dk_ref[...] = jnp.zeros_like(dk_ref)
            dv_ref[...] = jnp.zeros_like(dv_ref)

        msf = ms_ref[...]
        mpf = mp_ref[...]
        lane = lax.broadcasted_iota(jnp.int32, (tq, LANES), 1)
        lane1 = lax.broadcasted_iota(jnp.int32, (1, LANES), 1)
        mine = [_lanes(tq, SB_DIM * half, SB_DIM) for half in range(2)]
        qv = [_keep(m, q_ref[...]) for m in mine]
        qs = [t * 0.125 for t in qv]
        do_b = [_keep(m, do_ref[...]).astype(BF16) for m in mine]
        cm = [c_ref[:, :LANES], c_ref[:, LANES:]]

        def block(kb, carry, dd, half):
            dq_acc, pc = carry
            ks = pl.ds(pl.multiple_of(kb * tk, tk), tk)
            kk = k_ref[ks, :]
            strict = None if dd is None else _sb_strict(tq, tk, dd)
            z, lom, suf = _sb_scores(qs[half], kk, msf, strict)
            c = jnp.sum(jnp.where(lane == kb, cm[half], 0.0), axis=1, keepdims=True)
            a = jnp.exp(z + lom + (suf + c))
            if strict is not None:
                a = jnp.where(strict, a, 0.0)
            g = _dot_nt(do_b[half], v_ref[ks, :]) * a
            p = pc + _dot(g.astype(BF16), mpf)
            omb = jnp.exp(lom)
            dz = (g * omb - (1.0 - omb) * p) * 0.125
            if strict is not None:
                dz = jnp.where(strict, dz, 0.0)
            dz = dz.astype(BF16)
            dv_ref[ks, :] += _dot_tn(a.astype(BF16), do_b[half])
            dk_ref[ks, :] += _dot_tn(dz, qv[half])
            return dq_acc + _dot(dz, kk), pc + jnp.sum(g, axis=1, keepdims=True)

        def needed(cm_h):
            seen = jnp.logical_and(jnp.max(cm_h, axis=0, keepdims=True) > -SB_SKIP, lane1 < i * ratio)
            return jnp.sum(seen.astype(jnp.int32))

        first = i * ratio - jnp.maximum(needed(cm[0]), needed(cm[1]))
        init = (jnp.zeros((tq, LANES), F32), jnp.zeros((tq, 1), F32))
        carries = lax.fori_loop(first, i * ratio, lambda kb, c: (block(kb, c[0], None, 0), block(kb, c[1], None, 1)), (init, init))
        for dd in range(ratio):
            carries = [block(i * ratio + dd, carries[half], dd, half) for half in range(2)]
        dq_ref[...] = (_keep(mine[0], carries[0][0]) + _keep(mine[1], carries[1][0])).astype(BF16)

        @pl.when(i == s // tq - 1)
        def _():
            dk_out[...] = dk_ref[...].astype(BF16)
            dv_out[...] = dv_ref[...].astype(BF16)

        if n_ride:
            pl.when(jnp.logical_and(pl.program_id(0) == n_pairs - 1, i == s // tq - 1))(finish)

    qblk = lambda n: pl.BlockSpec((tq, n), lambda g, i: (i, g))
    full = pl.BlockSpec((s, LANES), lambda g, i: (0, g))
    msk = pl.BlockSpec((tk, tk), lambda g, i: (0, 0))
    outs = pl.pallas_call(
        kern, name="sb_bwd", grid=(n_pairs, s // tq),
        in_specs=[qblk(LANES), full, full, qblk(LANES), qblk(2 * LANES), msk, msk] + [HBM_SPEC] * n_ride,
        out_specs=[qblk(LANES), full, full] + [HBM_SPEC] * n_ride,
        out_shape=[_sds((s, SB_WIDTH), BF16)] * 3 + _halves_shapes(riders),
        scratch_shapes=[pltpu.VMEM((s, LANES), F32)] * 2 + (_swap_sems(n_ride) if n_ride else []), compiler_params=_params(2),
    )(q, k, v, do, cmat, msuf, mpre, *riders)
    return outs[:3], outs[3:]


def _place():
    return lax.axis_index("x"), lax.axis_index("y"), lax.axis_index("c")


def _other_chips(x, y):
    return [(1 - x, y), (x, 1 - y), (1 - x, 1 - y)]


HBM_SPEC = pl.BlockSpec(memory_space=pl.ANY)


def _gather_steps(ins, outs, send_sems, recv_sems):
    n = len(ins)
    x, y, c = _place()
    sibling = (x, y, 1 - c)
    chips = _other_chips(x, y)

    def half_of(a, ref, pc):
        half = ins[a].shape[0] // 2
        return ref.at[pl.ds(pl.multiple_of(pc * half, 16), half), :]

    def copy(a, k, chip, pc, to, src=None):
        dst = half_of(a, outs[a].at[2 * chip[0] + chip[1]], pc)
        return pltpu.make_async_remote_copy(src_ref=dst if src is None else src, dst_ref=dst, send_sem=send_sems.at[7 * a + k],
                                            recv_sem=recv_sems.at[7 * a + k], device_id=to, device_id_type=MESH)

    def own(a):
        return pltpu.make_async_remote_copy(src_ref=ins[a], dst_ref=outs[a].at[2 * x + y], send_sem=send_sems.at[7 * a + 6],
                                            recv_sem=recv_sems.at[7 * a + 6], device_id=sibling, device_id_type=MESH)

    def first():
        far = [copy(a, j, (x, y), c, (*chip, c), src=half_of(a, ins[a], c)) for a in range(n) for j, chip in enumerate(chips)]
        return far + [own(a) for a in range(n)]

    def passed():
        return [copy(a, 3 + j, chip, c, sibling) for j, chip in enumerate(chips) for a in range(n)]

    def send():
        for cp in first():
            cp.start()

    def forward():
        for j, chip in enumerate(chips):
            for a in range(n):
                copy(a, j, chip, c, sibling).wait_recv()
        for cp in passed():
            cp.start()

    def finish():
        for j, chip in enumerate(chips):
            for a in range(n):
                copy(a, 3 + j, chip, 1 - c, sibling).wait_recv()
        for a in range(n):
            own(a).wait_recv()
        for cp in first() + passed():
            cp.wait_send()

    return send, forward, finish


def _gather_sems(n):
    return [pltpu.SemaphoreType.DMA((7 * n,)), pltpu.SemaphoreType.DMA((7 * n,))]


def _allgather_list(name, shards):
    n = len(shards)

    def body(*refs):
        for stage in _gather_steps(refs[:n], refs[n:2 * n], *refs[2 * n:]):
            stage()

    return pl.pallas_call(
        body, name=name, out_shape=[_sds((N_SHARD,) + a.shape, a.dtype) for a in shards], in_specs=[HBM_SPEC] * n,
        out_specs=[HBM_SPEC] * n, scratch_shapes=_gather_sems(n),
    )(*shards)


def _swap_steps(ins, outs, send_sems, recv_sems):
    x, y, c = _place()

    def copies():
        out = []
        for a in range(len(ins)):
            h = ins[a].shape[1] // 2
            src = ins[a].at[:, pl.ds(pl.multiple_of((1 - c) * h, 8), h), :]
            out.append(pltpu.make_async_remote_copy(src_ref=src, dst_ref=outs[a], send_sem=send_sems.at[a], recv_sem=recv_sems.at[a],
                                                    device_id=(x, y, 1 - c), device_id_type=MESH))
        return out

    def start():
        for cp in copies():
            cp.start()

    def finish():
        for cp in copies():
            cp.wait()

    return start, finish


def _swap_sems(n):
    return [pltpu.SemaphoreType.DMA((n,)), pltpu.SemaphoreType.DMA((n,))]


def _halves_shapes(gs):
    return [_sds((N_SHARD, g.shape[1] // 2, g.shape[2]), g.dtype) for g in gs]


def _swap_halves(name, gs):
    n = len(gs)

    def body(*refs):
        for stage in _swap_steps(refs[:n], refs[n:2 * n], *refs[2 * n:]):
            stage()

    return pl.pallas_call(body, name=name, out_shape=_halves_shapes(gs), in_specs=[HBM_SPEC] * n, out_specs=[HBM_SPEC] * n,
                          scratch_shapes=_swap_sems(n))(*gs)


def _add_sibling(name, gs, gots, c_idx):
    n = len(gs)

    def kern(c_ref, *refs):
        for a in range(n):
            tot = refs[a][...] + refs[n + a][...]
            refs[2 * n + a][...] = tot
            refs[3 * n + a][...] = tot.astype(BF16)

    quarter = lambda g: (None, g.shape[1] // 4, g.shape[2])
    in_specs = [pl.BlockSpec(quarter(g), lambda b, s, c_ref: (b, 2 * c_ref[0] + s, 0)) for g in gs]
    in_specs += [pl.BlockSpec(quarter(g), lambda b, s, c_ref: (b, s, 0)) for g in gs]
    out_specs = [pl.BlockSpec(quarter(g), lambda b, s, c_ref: (b, s, 0)) for g in gs] * 2
    out_shape = [_sds(t.shape, F32) for t in gots] + [_sds(t.shape, BF16) for t in gots]
    outs = pl.pallas_call(
        kern, name=name, out_shape=out_shape,
        grid_spec=pltpu.PrefetchScalarGridSpec(num_scalar_prefetch=1, grid=(N_SHARD, 2), in_specs=in_specs, out_specs=out_specs),
        compiler_params=_params(2),
    )(c_idx.reshape(1), *gs, *gots)
    return outs[:n], outs[n:]


def _scatter_steps(ins, outs, send_sems, recv_sems):
    x, y, c = _place()

    def copies():
        return [pltpu.make_async_remote_copy(
            src_ref=ins[a].at[2 * px + py], dst_ref=outs[a].at[j], send_sem=send_sems.at[3 * a + j], recv_sem=recv_sems.at[3 * a + j],
            device_id=(px, py, c), device_id_type=MESH) for a in range(len(ins)) for j, (px, py) in enumerate(_other_chips(x, y))]

    def start():
        for cp in copies():
            cp.start()

    def finish():
        for cp in copies():
            cp.wait()

    return start, finish


def _scatter_sems(n):
    return [pltpu.SemaphoreType.DMA((3 * n,)), pltpu.SemaphoreType.DMA((3 * n,))]


def _chip_scatter(ps):
    n = len(ps)

    def body(*refs):
        for stage in _scatter_steps(refs[:n], refs[n:2 * n], *refs[2 * n:]):
            stage()

    return pl.pallas_call(
        body, name="chip_scatter", out_shape=[_sds((N_SHARD - 1,) + p.shape[1:], p.dtype) for p in ps], in_specs=[HBM_SPEC] * n,
        out_specs=[HBM_SPEC] * n, scratch_shapes=_scatter_sems(n),
    )(*ps)


def _add_chips(name, ps, others, shard_idx):
    n = len(ps)

    def kern(b_ref, *refs):
        for a in range(n):
            tot = refs[a][...]
            for j in range(N_SHARD - 1):
                tot = tot + refs[n + a][j].astype(F32)
            refs[2 * n + a][...] = tot

    in_specs = [pl.BlockSpec((None, p.shape[1] // 2, p.shape[2]), lambda s, b_ref: (b_ref[0], s, 0)) for p in ps]
    in_specs += [pl.BlockSpec((N_SHARD - 1, p.shape[1] // 2, p.shape[2]), lambda s, b_ref: (0, s, 0)) for p in ps]
    out_specs = [pl.BlockSpec((p.shape[1] // 2, p.shape[2]), lambda s, b_ref: (s, 0)) for p in ps]
    return pl.pallas_call(
        kern, name=name, out_shape=[_sds(p.shape[1:], F32) for p in ps],
        grid_spec=pltpu.PrefetchScalarGridSpec(num_scalar_prefetch=1, grid=(2,), in_specs=in_specs, out_specs=out_specs),
        compiler_params=_params(1),
    )(shard_idx.reshape(1), *ps, *others)


def _swap_result(name, mines):
    n = len(mines)

    def body(*refs):
        ins, outs = refs[:n], refs[n:2 * n]
        send_sems, recv_sems = refs[2 * n:]
        x, y, c = _place()
        copies = [pltpu.make_async_remote_copy(src_ref=ins[a], dst_ref=outs[a], send_sem=send_sems.at[a], recv_sem=recv_sems.at[a],
                                               device_id=(x, y, 1 - c), device_id_type=MESH) for a in range(n)]
        for cp in copies:
            cp.start()
        for cp in copies:
            cp.wait()

    return pl.pallas_call(
        body, name=name, out_shape=[_sds(m.shape, m.dtype) for m in mines], in_specs=[HBM_SPEC] * n,
        out_specs=[HBM_SPEC] * n, scratch_shapes=[pltpu.SemaphoreType.DMA((n,)), pltpu.SemaphoreType.DMA((n,))],
    )(*mines)


def _allreduce_small(v):
    m_per, n = v.shape

    def body(x_ref, tot_ref, all_ref, send_sems, recv_sems, local_sem):
        x, y, c = _place()
        me, sibling = (x, y, c), (x, y, 1 - c)
        chips = _other_chips(x, y)

        def rows(px, py, pc):
            return all_ref.at[pl.ds(pl.multiple_of((4 * px + 2 * py + pc) * m_per, 8), m_per), :]

        def copy(k, block, to, src=None):
            return pltpu.make_async_remote_copy(
                src_ref=rows(*block) if src is None else src, dst_ref=rows(*block), send_sem=send_sems.at[k],
                recv_sem=recv_sems.at[k], device_id=to, device_id_type=MESH)

        mine = pltpu.make_async_copy(x_ref, rows(*me), local_sem)
        mine.start()
        first = [copy(0, me, sibling, src=x_ref)] + [copy(1 + j, me, (*chip, c), src=x_ref) for j, chip in enumerate(chips)]
        for cp in first:
            cp.start()
        passed = [copy(4 + j, (*chip, c), sibling) for j, chip in enumerate(chips)]
        for j, chip in enumerate(chips):
            copy(1 + j, (*chip, c), me).wait_recv()
            passed[j].start()
        copy(0, sibling, me).wait_recv()
        for j, chip in enumerate(chips):
            copy(4 + j, (*chip, 1 - c), me).wait_recv()
        for cp in first + passed:
            cp.wait_send()
        mine.wait()
        tot = all_ref[0:m_per, :]
        for dev in range(1, 8):
            tot = tot + all_ref[dev * m_per:(dev + 1) * m_per, :]
        tot_ref[...] = tot

    vmem = pl.BlockSpec(memory_space=pltpu.VMEM)
    return pl.pallas_call(
        body, name="allreduce_small", out_shape=_sds((m_per, n), F32), in_specs=[vmem], out_specs=vmem,
        scratch_shapes=[pltpu.VMEM((8 * m_per, n), F32), pltpu.SemaphoreType.DMA((7,)), pltpu.SemaphoreType.DMA((7,)),
                        pltpu.SemaphoreType.DMA],
    )(v)


def _adam_update(w, g, m, v):
    m_new = ADAM_B1 * m + (1.0 - ADAM_B1) * g
    v_new = ADAM_B2 * v + (1.0 - ADAM_B2) * (g * g)
    m_hat = m_new / (1.0 - ADAM_B1 ** ADAM_STEP)
    v_hat = v_new / (1.0 - ADAM_B2 ** ADAM_STEP)
    return -ADAM_LR * (m_hat / (jnp.sqrt(v_hat) + ADAM_EPS) + ADAM_WD * w), m_new, v_new


def _adamw(name, w, g, m, v):
    rows, width = w.shape
    tr = rows // 4 if rows % 32 == 0 else rows

    def kern(w_ref, g_ref, m_ref, v_ref, d_ref, mo_ref, vo_ref):
        d_ref[...], mo_ref[...], vo_ref[...] = _adam_update(w_ref[...], g_ref[...], m_ref[...], v_ref[...])

    spec = pl.BlockSpec((tr, width), lambda i: (i, 0))
    return pl.pallas_call(kern, name=name, grid=(rows // tr,), in_specs=[spec] * 4, out_specs=[spec] * 3,
                          out_shape=[_sds((rows, width), F32)] * 3, compiler_params=_params(1))(w, g, m, v)


def _adamw_halves(ws, mines, theirs, ms, vs, c_idx):
    n = len(ws)

    def kern(c_ref, *refs):
        take_mine = pl.program_id(0) == c_ref[0]
        for a in range(n):
            w_ref, mine_ref, theirs_ref, m_ref, v_ref = refs[5 * a:5 * a + 5]
            g_ref, d_ref, mo_ref, vo_ref = refs[5 * n + 4 * a:5 * n + 4 * a + 4]
            g = jnp.where(take_mine, mine_ref[...], theirs_ref[...])
            g_ref[...] = g
            d_ref[...], mo_ref[...], vo_ref[...] = _adam_update(w_ref[...], g, m_ref[...], v_ref[...])

    in_specs, out_specs, out_shape = [], [], []
    for w in ws:
        rows, width = w.shape
        tr = rows // (2 * ADAM_STEPS)
        whole = pl.BlockSpec((tr, width), lambda h, j, c_ref: (ADAM_STEPS * h + j, 0))
        part = pl.BlockSpec((tr, width), lambda h, j, c_ref: (j, 0))
        in_specs += [whole, part, part, whole, whole]
        out_specs += [whole] * 4
        out_shape += [_sds((rows, width), F32)] * 4
    operands = [t for group in zip(ws, mines, theirs, ms, vs) for t in group]
    outs = pl.pallas_call(
        kern, name="adamw_shards", out_shape=out_shape,
        grid_spec=pltpu.PrefetchScalarGridSpec(num_scalar_prefetch=1, grid=(2, ADAM_STEPS), in_specs=in_specs, out_specs=out_specs),
        compiler_params=_params(2),
    )(c_idx.reshape(1), *operands)
    return [outs[4 * a:4 * a + 4] for a in range(n)]


SHARDED = (("w_in", D_MODEL, IN_WIDTH, 1), ("w_uq", Q_RANK, MLA_HEADS * MLA_QK, 1),
           ("w_ukv", KV_RANK, MLA_HEADS * (MLA_NOPE + MLA_V), 1), ("w_o", D_MODEL, D_MODEL, 0),
           ("w_gate", D_MODEL, D_FF, 1), ("w_up", D_MODEL, D_FF, 1), ("w_down", D_FF, D_MODEL, 0))
EARLY = ("w_in", "w_uq", "w_ukv")
LATE = ("w_o", "w_gate", "w_up", "w_down")
FLIPPED = ("w_gate", "w_up")
SMALL = (("norm_mix", D_MODEL), ("q_latent_norm", Q_RANK), ("kv_latent_norm", KV_RANK), ("out_norm_mla", MLA_WIDTH),
         ("out_norm_sb", SB_WIDTH), ("norm_ffn", D_MODEL), ("norm_final", D_MODEL))


def _full_weight(gathered, axis):
    n_sh, k, n = gathered.shape
    return gathered.transpose(1, 0, 2).reshape(k, n_sh * n) if axis == 1 else gathered.reshape(n_sh * k, n)


def _shard_major(g, axis):
    r, c = g.shape
    return g.reshape(r, N_SHARD, c // N_SHARD).transpose(1, 0, 2) if axis == 1 else g.reshape(N_SHARD, r // N_SHARD, c)


def _rot_cols(w):
    hh = MLA_ROPE // 2
    return jnp.concatenate([-w[..., hh:], w[..., :hh]], axis=-1)


def _rot_cols_t(g):
    hh = MLA_ROPE // 2
    return jnp.concatenate([g[..., hh:], -g[..., :hh]], axis=-1)


def _with_transposes(w):
    w.update({name + "_t": t.T for name, t in list(w.items())})
    return w


def _attention_weights(full, small):
    w_in = full["w_in"]
    s0, s1, s2 = Q_RANK, Q_RANK + KV_RANK, Q_RANK + KV_RANK + MLA_ROPE
    uq = full["w_uq"].reshape(Q_RANK, MLA_HEADS, MLA_QK)
    ukv = full["w_ukv"].reshape(KV_RANK, MLA_HEADS, MLA_NOPE + MLA_V)
    w_kr = w_in[:, s1:s2]
    per_tile = ROPE_TILE // MLA_ROPE
    w = _with_transposes({
        "w_cq": w_in[:, :s0], "w_ckv": w_in[:, s0:s1],
        "w_kr4": jnp.tile(w_kr, (1, per_tile)), "w_kr4r": jnp.tile(_rot_cols(w_kr), (1, per_tile)),
        "w_kr8": jnp.tile(w_kr, (1, MLA_HEADS)), "w_kr8r": jnp.tile(_rot_cols(w_kr), (1, MLA_HEADS)),
        "w_sbq": w_in[:, s2:s2 + SB_WIDTH], "w_sbk": w_in[:, s2 + SB_WIDTH:s2 + 2 * SB_WIDTH], "w_sbv": w_in[:, s2 + 2 * SB_WIDTH:],
        "w_qn": uq[..., :MLA_NOPE].reshape(Q_RANK, -1), "w_qr": uq[..., MLA_NOPE:].reshape(Q_RANK, -1),
        "w_qrr": _rot_cols(uq[..., MLA_NOPE:]).reshape(Q_RANK, -1),
        "w_kn": ukv[..., :MLA_NOPE].reshape(KV_RANK, -1), "w_v": ukv[..., MLA_NOPE:].reshape(KV_RANK, -1),
    })
    w.update(g_mix=small["norm_mix"], g_q=small["q_latent_norm"], g_kv=small["kv_latent_norm"], g_a=small["out_norm_mla"],
             g_b=small["out_norm_sb"], g_f=small["norm_ffn"], g_n=small["norm_final"])
    return w


def _ffn_weights(full):
    w = _with_transposes({"w_oa": full["w_o"][:MLA_WIDTH], "w_ob": full["w_o"][MLA_WIDTH:], "w_down": full["w_down"]})
    for name in FLIPPED:
        w[name + "_t"] = full[name]
        w[name] = full[name].T
    return w


def _rope_tables(positions):
    inv_freq = ROPE_THETA ** (-jnp.arange(0, MLA_ROPE, 2, dtype=F32) / MLA_ROPE)
    ang = positions.astype(F32)[:, None] * inv_freq[None, :]
    cos, sin = jnp.cos(ang), jnp.sin(ang)
    return {"cos": jnp.tile(jnp.concatenate([cos, cos], axis=1), (1, MLA_HEADS)),
            "sin": jnp.tile(jnp.concatenate([sin, sin], axis=1), (1, MLA_HEADS))}


def _by_head(g_wide, g_narrow, wide, narrow):
    r = g_wide.shape[0]
    return jnp.concatenate([g_wide.reshape(r, MLA_HEADS, wide), g_narrow.reshape(r, MLA_HEADS, narrow)], axis=-1).reshape(r, -1)


def kernel(x, positions, norm_mix, w_in, q_latent_norm, w_uq, kv_latent_norm, w_ukv, out_norm_mla, out_norm_sb, w_o, norm_ffn, w_gate, w_up, w_down, norm_final, loss_target, m_norm_mix, m_w_in, m_q_latent_norm, m_w_uq, m_kv_latent_norm, m_w_ukv, m_out_norm_mla, m_out_norm_sb, m_w_o, m_norm_ffn, m_w_gate, m_w_up, m_w_down, m_norm_final, v_norm_mix, v_w_in, v_q_latent_norm, v_w_uq, v_kv_latent_norm, v_w_ukv, v_out_norm_mla, v_out_norm_sb, v_w_o, v_norm_ffn, v_w_gate, v_w_up, v_w_down, v_norm_final):
    given = dict(norm_mix=norm_mix, w_in=w_in, q_latent_norm=q_latent_norm, w_uq=w_uq, kv_latent_norm=kv_latent_norm, w_ukv=w_ukv,
                 out_norm_mla=out_norm_mla, out_norm_sb=out_norm_sb, w_o=w_o, norm_ffn=norm_ffn, w_gate=w_gate, w_up=w_up,
                 w_down=w_down, norm_final=norm_final)
    mom_m = dict(norm_mix=m_norm_mix, w_in=m_w_in, q_latent_norm=m_q_latent_norm, w_uq=m_w_uq, kv_latent_norm=m_kv_latent_norm,
                 w_ukv=m_w_ukv, out_norm_mla=m_out_norm_mla, out_norm_sb=m_out_norm_sb, w_o=m_w_o, norm_ffn=m_norm_ffn,
                 w_gate=m_w_gate, w_up=m_w_up, w_down=m_w_down, norm_final=m_norm_final)
    mom_v = dict(norm_mix=v_norm_mix, w_in=v_w_in, q_latent_norm=v_q_latent_norm, w_uq=v_w_uq, kv_latent_norm=v_kv_latent_norm,
                 w_ukv=v_w_ukv, out_norm_mla=v_out_norm_mla, out_norm_sb=v_out_norm_sb, w_o=v_w_o, norm_ffn=v_norm_ffn,
                 w_gate=v_w_gate, w_up=v_w_up, w_down=v_w_down, norm_final=v_norm_final)
    xs = x[0]
    tgt = loss_target[0]
    s = xs.shape[0]
    c_idx = lax.axis_index("c")
    shard_idx = 2 * lax.axis_index("x") + lax.axis_index("y")

    def block2d(t, name):
        t = t.reshape(t.shape[-2:])
        return t.T if name in FLIPPED else t

    shard2d = {name: block2d(given[name], name) for name, *_ in SHARDED}
    local = {name: shard2d[name].astype(BF16) for name, *_ in SHARDED}
    axis_of = {name: 0 if name in FLIPPED else axis for name, _, _, axis in SHARDED}

    def whole(names, gathered):
        return {name: _full_weight(t, axis_of[name]) for name, t in zip(names, gathered)}

    small = {name: given[name].reshape(1, n) for name, n in SMALL}
    w = _attention_weights(whole(EARLY, _allgather_list("allgather_w", [local[name] for name in EARLY])), small)
    tabs = _rope_tables(positions[0])
    msuf, mpre = _sb_masks(min(SB_TK, s))

    u, cq, ckv, cqn, ckvn, qn, qr, kn, vm, kr, sq, sk, sv = _fwd_a(xs, tabs, w)
    o_mla, lse, late = _mla_fwd(qn, qr, kn, kr, vm, [local[name] for name in LATE])
    w.update(_ffn_weights(whole(LATE, late)))
    o_sb, cmat = _sb_fwd(sq, sk, sv, msuf)
    merged, h1, f, gate, up, act = _fwd_b1(xs, o_mla, o_sb, w)
    dh2, loss_part, dg_n = _fwd_b2(h1, act, tgt, w)

    def shards_of(names, grads):
        return [_shard_major(grads[name], axis_of[name]) for name in names]

    def reduced(tag, chip_f32, others):
        mine = _add_chips("add_chips_" + tag, chip_f32, others, shard_idx)
        return tuple(mine), tuple(_swap_result("swap_result_" + tag, mine))

    dgate, dup, dh1, do_mla, do_sb, dg_f, dg_a, dg_b = _bwd_b(dh2, gate, up, h1, o_mla, o_sb, w)
    late_gs = shards_of(LATE, {
        "w_o": _tn_matmul("dw_o", merged, dh1), "w_gate": _tn_matmul("dw_gate", dgate, f),
        "w_up": _tn_matmul("dw_up", dup, f), "w_down": _tn_matmul("dw_down", act, dh2)})
    (dsq, dsk, dsv), late_got = _sb_bwd(sq, sk, sv, do_sb, cmat, msuf, mpre, late_gs)
    late_f32, late_bf16 = _add_sibling("add_sibling_late", late_gs, late_got, c_idx)
    (dqn, dqr, dkn, dkr, dvm), late_others = _mla_bwd(qn, qr, kn, kr, vm, o_mla, do_mla, lse, late_bf16)
    mine_late, theirs_late = reduced("late", late_f32, late_others)
    dx, a1, a2, dcq, dckv, dkrc, dkrs, dg_q, dg_kv, dg_mix = _bwd_a(xs, dh1, cq, ckv, dqn, dqr, dkn, dvm, dkr, dsq, dsk, dsv, tabs, w)

    g_cq, g_ckv, g_krc, g_krs, g_sq, g_sk, g_sv = _tn_multi("dw_in", u, [dcq, dckv, dkrc, dkrs, dsq, dsk, dsv])
    g_qn, g_qr1, g_qr2 = _tn_multi("dw_uq", cqn, [dqn, a1, a2])
    g_kn, g_v = _tn_multi("dw_ukv", ckvn, [dkn, dvm])
    slots = lambda g: g.reshape(g.shape[0], MLA_HEADS, MLA_ROPE)
    g_kr = jnp.sum(slots(g_krc), axis=1) + _rot_cols_t(jnp.sum(slots(g_krs), axis=1))
    g_qr = (slots(g_qr1) + _rot_cols_t(slots(g_qr2))).reshape(Q_RANK, -1)
    early_gs = shards_of(EARLY, {
        "w_in": jnp.concatenate([g_cq, g_ckv, g_kr, g_sq, g_sk, g_sv], axis=1),
        "w_uq": _by_head(g_qn, g_qr, MLA_NOPE, MLA_ROPE),
        "w_ukv": _by_head(g_kn, g_v, MLA_NOPE, MLA_V)})
    early_f32, early_bf16 = _add_sibling("add_sibling_early", early_gs, _swap_halves("swap_halves_early", early_gs), c_idx)
    mine_early, theirs_early = reduced("early", early_f32, _chip_scatter(early_bf16))
    halves = dict(zip(EARLY + LATE, zip(mine_early + mine_late, theirs_early + theirs_late)))

    small_parts = jnp.concatenate([dg_mix, dg_q, dg_kv, dg_a, dg_b, dg_f, dg_n, loss_part], axis=1)
    small_sum = _allreduce_small(jnp.broadcast_to(small_parts, (8, small_parts.shape[1])))
    small_g, loss = small_sum[0:1, :-LANES], small_sum[0, -LANES]

    g_out, d_out, m_out, v_out = {}, {}, {}, {}
    names = [name for name, *_ in SHARDED]
    updated = _adamw_halves([shard2d[name] for name in names], [halves[name][0] for name in names], [halves[name][1] for name in names],
                            [block2d(mom_m[name], name) for name in names], [block2d(mom_v[name], name) for name in names], c_idx)
    for name, outs in zip(names, updated):
        shape = given[name].shape
        g_out[name], d_out[name], m_out[name], v_out[name] = ((t.T if name in FLIPPED else t).reshape(shape) for t in outs)
    cat = lambda src: jnp.concatenate([src[name].reshape(1, n) for name, n in SMALL], axis=1)
    d, mn, vn = _adamw("adamw_small", cat(given), small_g, cat(mom_m), cat(mom_v))
    off = 0
    for name, n in SMALL:
        shape = given[name].shape
        g_out[name], d_out[name], m_out[name], v_out[name] = (t[:, off:off + n].reshape(shape) for t in (small_g, d, mn, vn))
        off += n

    order = ["norm_mix", "w_in", "q_latent_norm", "w_uq", "kv_latent_norm", "w_ukv", "out_norm_mla", "out_norm_sb", "w_o",
             "norm_ffn", "w_gate", "w_up", "w_down", "norm_final"]
    return (loss, dx[None], *[g_out[n] for n in order], *[d_out[n] for n in order], *[m_out[n] for n in order],
            *[v_out[n] for n in order])
```

```python
import math

import jax
import jax.numpy as jnp
from jax import lax
from jax.experimental import pallas as pl
from jax.experimental.pallas import tpu as pltpu

F32 = jnp.float32
BF16 = jnp.bfloat16
MESH = pl.DeviceIdType.MESH

D_MODEL = 1024
EPS = 1e-6
MLA_HEADS = 8
MLA_NOPE = 64
MLA_ROPE = 32
MLA_V = 64
MLA_QK = MLA_NOPE + MLA_ROPE
Q_RANK = 256
KV_RANK = 128
ROPE_THETA = 10000.0
SB_HEADS = 8
SB_DIM = 64
MLA_WIDTH = MLA_HEADS * MLA_V
SB_WIDTH = SB_HEADS * SB_DIM
D_FF = 2816
IN_WIDTH = Q_RANK + KV_RANK + MLA_ROPE + 3 * SB_WIDTH

ADAM_LR = 0.001
ADAM_B1 = 0.9
ADAM_B2 = 0.999
ADAM_EPS = 1e-08
ADAM_WD = 0.01
ADAM_STEP = 10

N_SHARD = 4
LANES = 128
ROPE_TILE = LANES
VMEM_LIMIT = 56 * 1024 * 1024
TN_ACC_BYTES = 6 * 1024 * 1024 + 512 * 1024
NEG = -1e30
MLA_SCALE = 1.0 / math.sqrt(MLA_QK)
MLA_DK_SCALE = math.log(2.0)
MLA_QSCALE = MLA_SCALE * math.log2(math.e)
SB_SKIP = 110.0

ROW_TILE = 512
ROW_TILE_ELEMENTWISE = 256
MLA_TQ = 1024
SB_TQ = 512
MLA_TK = 1024
MLA_BWD_TK = 1024
MLA_DIAG_TK = 256
SB_TK = 256
TN_TS = 2048
ADAM_STEPS = 4


def _dot(a, b):
    return jnp.dot(a, b, preferred_element_type=F32)


def _dot_nt(a, b):
    return lax.dot_general(a, b, (((1,), (1,)), ((), ())), preferred_element_type=F32)


def _dot_tn(a, b):
    return lax.dot_general(a, b, (((0,), (0,)), ((), ())), preferred_element_type=F32)


def _params(n_grid, vmem=VMEM_LIMIT):
    return pltpu.CompilerParams(dimension_semantics=("arbitrary",) * n_grid, vmem_limit_bytes=vmem)


def _rms(x):
    r = lax.rsqrt(jnp.mean(x * x, axis=-1, keepdims=True) + EPS)
    return x * r, r


def _rms_bwd(n, r, g, dy):
    dn = dy * g
    dx = r * (dn - n * jnp.mean(dn * n, axis=-1, keepdims=True))
    return dx, jnp.sum(dy * n, axis=0, keepdims=True)


def _accumulate(ref, val, step):
    @pl.when(step == 0)
    def _():
        ref[...] = val

    @pl.when(step != 0)
    def _():
        ref[...] += val


def _rowwise(name, body, rows, consts, row_out, acc_out, tm):
    n_rows = rows[0].shape[0]
    tm = min(tm, n_rows)
    nr, nc, no = len(rows), len(consts), len(row_out)

    def kern(*refs):
        body(refs[:nr], refs[nr:nr + nc], refs[nr + nc:nr + nc + no], refs[nr + nc + no:], pl.program_id(0))

    in_specs = [pl.BlockSpec((tm, a.shape[1]), lambda i: (i, 0)) for a in rows]
    in_specs += [pl.BlockSpec(a.shape, lambda i: (0, 0), pipeline_mode=pl.Buffered(1)) for a in consts]
    out_specs = [pl.BlockSpec((tm, s.shape[1]), lambda i: (i, 0)) for s in row_out]
    out_specs += [pl.BlockSpec(s.shape, lambda i: (0, 0)) for s in acc_out]
    return pl.pallas_call(
        kern, name=name, grid=(n_rows // tm,), in_specs=in_specs, out_specs=out_specs,
        out_shape=list(row_out) + list(acc_out), compiler_params=_params(1),
    )(*rows, *consts)


def _sds(shape, dtype):
    return jax.ShapeDtypeStruct(shape, dtype)


def _fwd_a(x, tabs, w):
    s = x.shape[0]

    def body(r, c, o, a, step):
        x_ref, cos_ref, sin_ref = r
        gmix, wcq, wckv, wkr, wkrr, wsq, wsk, wsv, gq, wqn, wqr, wqrr, gkv, wkn, wv = c
        u_o, cq_o, ckv_o, cqn_o, ckvn_o, qn_o, qr_o, kn_o, v_o, kr_o, sq_o, sk_o, sv_o = o
        cos, sin = cos_ref[...], sin_ref[...]
        n, _ = _rms(x_ref[...])
        u = (n * gmix[...]).astype(BF16)
        u_o[...] = u
        cq = _dot(u, wcq[...])
        ckv = _dot(u, wckv[...])
        kr_o[...] = (_dot(u, wkr[...]) * cos[:, :ROPE_TILE] + _dot(u, wkrr[...]) * sin[:, :ROPE_TILE]).astype(BF16)
        sq_o[...] = _dot(u, wsq[...]).astype(BF16)
        sk_o[...] = _dot(u, wsk[...]).astype(BF16)
        sv_o[...] = _dot(u, wsv[...]).astype(BF16)
        cq_o[...] = cq
        ckv_o[...] = ckv
        nq, _ = _rms(cq)
        cqn = (nq * gq[...]).astype(BF16)
        cqn_o[...] = cqn
        qn_o[...] = (_dot(cqn, wqn[...]) * MLA_QSCALE).astype(BF16)
        qr_o[...] = ((_dot(cqn, wqr[...]) * cos + _dot(cqn, wqrr[...]) * sin) * MLA_QSCALE).astype(BF16)
        nkv, _ = _rms(ckv)
        ckvn = (nkv * gkv[...]).astype(BF16)
        ckvn_o[...] = ckvn
        kn_o[...] = _dot(ckvn, wkn[...]).astype(BF16)
        v_o[...] = _dot(ckvn, wv[...]).astype(BF16)

    outs = [
        _sds((s, D_MODEL), BF16), _sds((s, Q_RANK), F32), _sds((s, KV_RANK), F32), _sds((s, Q_RANK), BF16),
        _sds((s, KV_RANK), BF16), _sds((s, MLA_HEADS * MLA_NOPE), BF16), _sds((s, MLA_HEADS * MLA_ROPE), BF16),
        _sds((s, MLA_HEADS * MLA_NOPE), BF16), _sds((s, MLA_WIDTH), BF16), _sds((s, ROPE_TILE), BF16),
        _sds((s, SB_WIDTH), BF16), _sds((s, SB_WIDTH), BF16), _sds((s, SB_WIDTH), BF16),
    ]
    consts = [w["g_mix"], w["w_cq"], w["w_ckv"], w["w_kr4"], w["w_kr4r"], w["w_sbq"], w["w_sbk"], w["w_sbv"], w["g_q"],
              w["w_qn"], w["w_qr"], w["w_qrr"], w["g_kv"], w["w_kn"], w["w_v"]]
    return _rowwise("fwd_a", body, [x, tabs["cos"], tabs["sin"]], consts, outs, [], ROW_TILE)


def _fwd_b1(x, o_mla, o_sb, w):
    s = x.shape[0]

    def body(r, c, o, a, step):
        x_ref, oa_ref, ob_ref = r
        ga, gb, woa, wob, gf, wg, wu = c
        mg_o, h1_o, f_o, gate_o, up_o, act_o = o
        na, _ = _rms(oa_ref[...])
        nb, _ = _rms(ob_ref[...])
        ma = (na * ga[...]).astype(BF16)
        mb = (nb * gb[...]).astype(BF16)
        mg_o[:, :MLA_WIDTH] = ma
        mg_o[:, MLA_WIDTH:] = mb
        h1 = x_ref[...] + _dot(ma, woa[...]) + _dot(mb, wob[...])
        h1_o[...] = h1
        nf, _ = _rms(h1)
        f = (nf * gf[...]).astype(BF16)
        f_o[...] = f
        gate = _dot(f, wg[...])
        up = _dot(f, wu[...])
        gate_o[...] = gate.astype(BF16)
        up_o[...] = up.astype(BF16)
        act_o[...] = (gate * (1.0 / (1.0 + jnp.exp(-gate))) * up).astype(BF16)

    outs = [_sds((s, D_MODEL), BF16), _sds((s, D_MODEL), F32), _sds((s, D_MODEL), BF16), _sds((s, D_FF), BF16),
            _sds((s, D_FF), BF16), _sds((s, D_FF), BF16)]
    consts = [w["g_a"], w["g_b"], w["w_oa"], w["w_ob"], w["g_f"], w["w_gate"], w["w_up"]]
    return _rowwise("fwd_b1", body, [x, o_mla, o_sb], consts, outs, [], ROW_TILE)


def _fwd_b2(h1, act, tgt, w):
    s = h1.shape[0]

    def body(r, c, o, a, step):
        h1_ref, act_ref, t_ref = r
        wd, gn = c
        (dh2_o,) = o
        loss_o, dgn_o = a
        h2 = h1_ref[...] + _dot(act_ref[...], wd[...])
        n2, r2 = _rms(h2)
        err = n2 * gn[...] - t_ref[...]
        part = jnp.sum(jnp.sum(err * err, axis=1, keepdims=True), axis=0, keepdims=True) * (0.5 / D_MODEL)
        _accumulate(loss_o, jnp.broadcast_to(part, (1, LANES)), step)
        dh2, dgn = _rms_bwd(n2, r2, gn[...], err * (1.0 / D_MODEL))
        dh2_o[...] = dh2
        _accumulate(dgn_o, dgn, step)

    return _rowwise("fwd_b2", body, [h1, act, tgt], [w["w_down"], w["g_n"]], [_sds((s, D_MODEL), F32)],
                    [_sds((1, LANES), F32), _sds((1, D_MODEL), F32)], ROW_TILE)


def _bwd_b(dh2, gate, up, h1, o_mla, o_sb, w):
    s = h1.shape[0]

    def body(r, c, o, a, step):
        dh2_ref, gate_ref, up_ref, h1_ref, oa_ref, ob_ref = r
        wdt, wgt, wut, gf, woat, wobt, ga, gb = c
        dgate_o, dup_o, dh1_o, doa_o, dob_o = o
        dgf_o, dga_o, dgb_o = a
        dh2 = dh2_ref[...]
        dact = _dot(dh2.astype(BF16), wdt[...])
        gate = gate_ref[...].astype(F32)
        sig = 1.0 / (1.0 + jnp.exp(-gate))
        dup = (dact * (gate * sig)).astype(BF16)
        dgate = (dact * up_ref[...].astype(F32) * (sig * (1.0 + gate * (1.0 - sig)))).astype(BF16)
        dup_o[...] = dup
        dgate_o[...] = dgate
        df = _dot(dgate, wgt[...]) + _dot(dup, wut[...])
        nf, rf = _rms(h1_ref[...])
        dres, dgf = _rms_bwd(nf, rf, gf[...], df)
        dh1 = dh2 + dres
        dh1_o[...] = dh1
        dh1b = dh1.astype(BF16)
        na, ra = _rms(oa_ref[...])
        doa, dga = _rms_bwd(na, ra, ga[...], _dot(dh1b, woat[...]))
        nb, rb = _rms(ob_ref[...])
        dob, dgb = _rms_bwd(nb, rb, gb[...], _dot(dh1b, wobt[...]))
        doa_o[...] = doa
        dob_o[...] = dob
        _accumulate(dgf_o, dgf, step)
        _accumulate(dga_o, dga, step)
        _accumulate(dgb_o, dgb, step)

    consts = [w["w_down_t"], w["w_gate_t"], w["w_up_t"], w["g_f"], w["w_oa_t"], w["w_ob_t"], w["g_a"], w["g_b"]]
    outs = [_sds((s, D_FF), BF16), _sds((s, D_FF), BF16), _sds((s, D_MODEL), F32), _sds((s, MLA_WIDTH), F32), _sds((s, SB_WIDTH), F32)]
    accs = [_sds((1, D_MODEL), F32), _sds((1, MLA_WIDTH), F32), _sds((1, SB_WIDTH), F32)]
    return _rowwise("bwd_b", body, [dh2, gate, up, h1, o_mla, o_sb], consts, outs, accs, ROW_TILE_ELEMENTWISE)


def _fold_pairs(t):
    return jnp.concatenate([t[:, :LANES] + t[:, LANES:2 * LANES], t[:, 2 * LANES:3 * LANES] + t[:, 3 * LANES:]], axis=1)


def _bwd_a(x, dh1, cq, ckv, dqn, dqr, dkn, dvm, dkr, dsq, dsk, dsv, tabs, w):
    s = x.shape[0]

    def body(r, c, o, a, step):
        x_ref, dh1_ref, cq_ref, ckv_ref, dqn_ref, dqr_ref, dkn_ref, dvm_ref, dkr_ref, dsq_ref, dsk_ref, dsv_ref, cos_ref, sin_ref = r
        wqnt, wqrt, wqrrt, gq, wknt, wvt, gkv, wcqt, wckvt, wkrt, wkrrt, wsqt, wskt, wsvt, gmix = c
        dx_o, a1_o, a2_o, dcq_o, dckv_o, dkrc_o, dkrs_o = o
        dgq_o, dgkv_o, dgmix_o = a
        cos, sin = cos_ref[...], sin_ref[...]
        dqr = _fold_pairs(dqr_ref[...].astype(F32))
        a1 = (dqr * cos).astype(BF16)
        a2 = (dqr * sin).astype(BF16)
        a1_o[...] = a1
        a2_o[...] = a2
        nq, rq = _rms(cq_ref[...])
        dcqn = _dot(dqn_ref[...], wqnt[...]) + _dot(a1, wqrt[...]) + _dot(a2, wqrrt[...])
        dcq, dgq = _rms_bwd(nq, rq, gq[...], dcqn)
        nkv, rkv = _rms(ckv_ref[...])
        dckvn = _dot(dkn_ref[...], wknt[...]) + _dot(dvm_ref[...], wvt[...])
        dckv, dgkv = _rms_bwd(nkv, rkv, gkv[...], dckvn)
        dkr = _fold_pairs(dkr_ref[...].astype(F32))
        dcq_b = dcq.astype(BF16)
        dckv_b = dckv.astype(BF16)
        dkrc = (dkr * cos).astype(BF16)
        dkrs = (dkr * sin).astype(BF16)
        dcq_o[...] = dcq_b
        dckv_o[...] = dckv_b
        dkrc_o[...] = dkrc
        dkrs_o[...] = dkrs
        du = (_dot(dcq_b, wcqt[...]) + _dot(dckv_b, wckvt[...]) + _dot(dkrc, wkrt[...]) + _dot(dkrs, wkrrt[...])
              + _dot(dsq_ref[...], wsqt[...]) + _dot(dsk_ref[...], wskt[...]) + _dot(dsv_ref[...], wsvt[...]))
        nx, rx = _rms(x_ref[...])
        dres, dgmix = _rms_bwd(nx, rx, gmix[...], du)
        dx_o[...] = dh1_ref[...] + dres
        _accumulate(dgq_o, dgq, step)
        _accumulate(dgkv_o, dgkv, step)
        _accumulate(dgmix_o, dgmix, step)

    consts = [w["w_qn_t"], w["w_qr_t"], w["w_qrr_t"], w["g_q"], w["w_kn_t"], w["w_v_t"], w["g_kv"], w["w_cq_t"], w["w_ckv_t"],
              w["w_kr8_t"], w["w_kr8r_t"], w["w_sbq_t"], w["w_sbk_t"], w["w_sbv_t"], w["g_mix"]]
    rope_w = MLA_HEADS * MLA_ROPE
    outs = [_sds((s, D_MODEL), F32), _sds((s, rope_w), BF16), _sds((s, rope_w), BF16), _sds((s, Q_RANK), BF16),
            _sds((s, KV_RANK), BF16), _sds((s, rope_w), BF16), _sds((s, rope_w), BF16)]
    accs = [_sds((1, Q_RANK), F32), _sds((1, KV_RANK), F32), _sds((1, D_MODEL), F32)]
    rows = [x, dh1, cq, ckv, dqn, dqr, dkn, dvm, dkr, dsq, dsk, dsv, tabs["cos"], tabs["sin"]]
    return _rowwise("bwd_a", body, rows, consts, outs, accs, ROW_TILE)


def _tn_multi(name, x, ys):
    s, k = x.shape
    ts = min(TN_TS, s)
    n_y = len(ys)

    def kern(*refs):
        step = pl.program_id(0)
        xb = refs[0][...].astype(BF16)
        for j in range(n_y):
            _accumulate(refs[1 + n_y + j], _dot_tn(xb, refs[1 + j][...].astype(BF16)), step)

    return pl.pallas_call(
        kern, name=name, grid=(s // ts,),
        in_specs=[pl.BlockSpec((ts, k), lambda i: (i, 0))] + [pl.BlockSpec((ts, y.shape[1]), lambda i: (i, 0)) for y in ys],
        out_specs=[pl.BlockSpec((k, y.shape[1]), lambda i: (0, 0)) for y in ys],
        out_shape=[_sds((k, y.shape[1]), F32) for y in ys], compiler_params=_params(1),
    )(x, *ys)


def _tn_tile(k, n):
    if n % LANES or k * n * 4 <= TN_ACC_BYTES:
        return n
    units = n // LANES
    best = 1
    for d in range(1, units + 1):
        if units % d == 0 and k * d * LANES * 4 <= TN_ACC_BYTES:
            best = d
    return best * LANES


def _tn_matmul(name, x, y):
    s, k = x.shape
    n = y.shape[1]
    ts = min(TN_TS, s)
    tn = _tn_tile(k, n)

    def kern(x_ref, y_ref, o_ref):
        step = pl.program_id(1)
        _accumulate(o_ref, _dot_tn(x_ref[...].astype(BF16), y_ref[...].astype(BF16)), step)

    return pl.pallas_call(
        kern, name=name, grid=(n // tn, s // ts),
        in_specs=[pl.BlockSpec((ts, k), lambda j, i: (i, 0)), pl.BlockSpec((ts, tn), lambda j, i: (i, j))],
        out_specs=pl.BlockSpec((k, tn), lambda j, i: (0, j)), out_shape=_sds((k, n), F32), compiler_params=_params(2),
    )(x, y)


def _lanes(rows, lo, width):
    lane = lax.broadcasted_iota(jnp.int32, (rows, LANES), 1)
    return jnp.logical_and(lane >= lo, lane < lo + width)


def _keep(mask, t):
    return jnp.where(mask, t, jnp.zeros_like(t))


def _mla_qcat(qn_ref, qr_ref, rope_lo, half, rows):
    qn = _keep(_lanes(rows, MLA_NOPE * half, MLA_NOPE), qn_ref[...])
    qr = _keep(_lanes(rows, rope_lo, MLA_ROPE), qr_ref[...])
    return jnp.concatenate([qn, qr], axis=1)


def _diag_mask(rows, width, row0, col0):
    row = lax.broadcasted_iota(jnp.int32, (rows, width), 0)
    col = lax.broadcasted_iota(jnp.int32, (rows, width), 1)
    return col + (col0 - row0) <= row


def _mla_fwd(qn, qr, kn, kr, v, riders=()):
    s = qn.shape[0]
    tq, tk = min(MLA_TQ, s), min(MLA_TK, s)
    td = tq
    ratio = tq // tk

    n_ride = len(riders)
    n_pairs = MLA_HEADS // 2

    def kern(qn_ref, qr_ref, kn_ref, kr_ref, v_ref, *rest):
        o_ref, lse_ref = rest[n_ride:n_ride + 2]
        g = pl.program_id(0)
        i = pl.program_id(1)
        if n_ride:
            send, forward, finish = _gather_steps(rest[:n_ride], rest[n_ride + 2:2 * n_ride + 2], *rest[2 * n_ride + 2:])
            pl.when(jnp.logical_and(g == 0, i == 0))(send)
            pl.when(jnp.logical_and(g == 1, i == 0))(forward)
        qcat = [_mla_qcat(qn_ref, qr_ref, MLA_ROPE * (2 * (g % 2) + half), half, tq) for half in range(2)]

        def block(k0, width, carry, row0, masked, half):
            m, l, acc = (c[row0:] for c in carry)
            ks = pl.ds(pl.multiple_of(k0, width), width)
            kcat = jnp.concatenate([kn_ref[ks, :], kr_ref[ks, :]], axis=1)
            sc = _dot_nt(qcat[half][row0:], kcat)
            if masked:
                sc = jnp.where(_diag_mask(tq - row0, width, row0, row0), sc, NEG)
            m_new = jnp.maximum(m, jnp.max(sc, axis=1, keepdims=True))
            p = jnp.exp2(sc - m_new)
            alpha = jnp.exp2(m - m_new)
            l = alpha * l + jnp.sum(p, axis=1, keepdims=True)
            acc = alpha * acc + _dot(p.astype(BF16), v_ref[ks, :])
            new = (m_new, l, acc)
            return new if row0 == 0 else tuple(jnp.concatenate([c[:row0], n], axis=0) for c, n in zip(carry, new))

        def both(k0, width, carries, row0, masked):
            return tuple(block(k0, width, carries[half], row0, masked, half) for half in range(2))

        init = (jnp.full((tq, 1), NEG, F32), jnp.zeros((tq, 1), F32), jnp.zeros((tq, LANES), F32))
        carries = lax.fori_loop(0, i * ratio, lambda kb, c: both(kb * tk, tk, c, 0, False), (init, init))
        for row0 in range(0, tq, td):
            carries = both(i * tq + row0, td, carries, row0, True)
        for half in range(2):
            m, l, acc = carries[half]
            out = _keep(_lanes(tq, MLA_V * half, MLA_V), acc / l)
            lse = _keep(_lanes(tq, MLA_ROPE * half, MLA_ROPE), jnp.broadcast_to(m + jnp.log2(l), (tq, LANES)))
            if half == 0:
                o_ref[...] = out
                lse_ref[...] = lse
            else:
                o_ref[...] += out
                lse_ref[...] += lse
        if n_ride:
            pl.when(jnp.logical_and(g == n_pairs - 1, i == s // tq - 1))(finish)

    qblk = pl.BlockSpec((tq, LANES), lambda g, i: (i, g))
    full = pl.BlockSpec((s, LANES), lambda g, i: (0, g))
    outs = pl.pallas_call(
        kern, name="mla_fwd", grid=(n_pairs, s // tq),
        in_specs=[qblk, pl.BlockSpec((tq, LANES), lambda g, i: (i, g // 2)), full, pl.BlockSpec((s, LANES), lambda g, i: (0, 0)), full]
        + [HBM_SPEC] * n_ride,
        out_specs=[qblk, qblk] + [HBM_SPEC] * n_ride,
        out_shape=[_sds((s, MLA_WIDTH), F32), _sds((s, n_pairs * LANES), F32)] + [_sds((N_SHARD,) + a.shape, a.dtype) for a in riders],
        scratch_shapes=_gather_sems(n_ride) if n_ride else [], compiler_params=_params(2),
    )(qn, qr, kn, kr, v, *riders)
    return outs[0], outs[1], outs[2:]


def _mla_bwd(qn, qr, kn, kr, v, o, do, lse, riders=()):
    s = qn.shape[0]
    tq, tk, td = min(MLA_TQ, s), min(MLA_BWD_TK, s), min(MLA_DIAG_TK, s)
    ratio = tq // tk

    n_ride = len(riders)
    n_pairs = MLA_HEADS // 2

    def kern(qn_ref, qr_ref, kn_ref, kr_ref, v_ref, o_ref, do_ref, lse_ref, *rest):
        dqn_ref, dqr_ref, dkn_out, dkr_out, dv_out = rest[n_ride:n_ride + 5]
        dkn_ref, dkr_ref, dv_ref = rest[2 * n_ride + 5:2 * n_ride + 8]
        g = pl.program_id(0)
        i = pl.program_id(1)
        if n_ride:
            start, finish = _scatter_steps(rest[:n_ride], rest[n_ride + 5:2 * n_ride + 5], *rest[2 * n_ride + 8:])
            pl.when(jnp.logical_and(g == 0, i == 0))(start)

        @pl.when(i == 0)
        def _():
            dkn_ref[...] = jnp.zeros_like(dkn_ref)
            dkr_ref[...] = jnp.zeros_like(dkr_ref)
            dv_ref[...] = jnp.zeros_like(dv_ref)

        for half in range(2):
            rope_lo = MLA_ROPE * (2 * (g % 2) + half)
            qcat = _mla_qcat(qn_ref, qr_ref, rope_lo, half, tq)
            mine = _lanes(tq, MLA_V * half, MLA_V)
            do_f = _keep(mine, do_ref[...])
            do_b = do_f.astype(BF16)
            delta = jnp.sum(do_f * o_ref[...], axis=1, keepdims=True)
            lse_v = lse_ref[:, MLA_ROPE * half:MLA_ROPE * half + 1]

            def block(k0, width, dq_acc, row0, masked, qcat=qcat, do_b=do_b, delta=delta, lse_v=lse_v):
                ks = pl.ds(pl.multiple_of(k0, width), width)
                kcat = jnp.concatenate([kn_ref[ks, :], kr_ref[ks, :]], axis=1)
                qc, dob = qcat[row0:], do_b[row0:]
                p = jnp.exp2(_dot_nt(qc, kcat) - lse_v[row0:])
                if masked:
                    p = jnp.where(_diag_mask(tq - row0, width, row0, row0), p, 0.0)
                ds = (p * (_dot_nt(dob, v_ref[ks, :]) - delta[row0:])).astype(BF16)
                dv_ref[ks, :] += _dot_tn(p.astype(BF16), dob)
                dkc = _dot_tn(ds, qc)
                dkn_ref[ks, :] += dkc[:, :LANES]
                dkr_ref[ks, :] += dkc[:, LANES:]
                new = dq_acc[row0:] + _dot(ds, kcat)
                return new if row0 == 0 else jnp.concatenate([dq_acc[:row0], new], axis=0)

            acc = lax.fori_loop(0, i * ratio, lambda kb, c, block=block: block(kb * tk, tk, c, 0, False),
                                jnp.zeros((tq, 2 * LANES), F32))
            for row0 in range(0, tq, td):
                acc = block(i * tq + row0, td, acc, row0, True)
            dqn = _keep(_lanes(tq, MLA_NOPE * half, MLA_NOPE), acc[:, :LANES] * MLA_SCALE)
            dqr = _keep(_lanes(tq, rope_lo, MLA_ROPE), acc[:, LANES:] * MLA_SCALE)
            if half == 0:
                dqn_ref[...] = dqn.astype(BF16)
                dqr_ref[...] = dqr.astype(BF16)
            else:
                dqn_ref[...] += dqn.astype(BF16)
                dqr_ref[...] += dqr.astype(BF16)

        @pl.when(i == s // tq - 1)
        def _():
            dkn_out[...] = (dkn_ref[...] * MLA_DK_SCALE).astype(BF16)
            dkr_out[...] = (dkr_ref[...] * MLA_DK_SCALE).astype(BF16)
            dv_out[...] = dv_ref[...].astype(BF16)

        if n_ride:
            pl.when(jnp.logical_and(g == n_pairs - 1, i == s // tq - 1))(finish)

    qblk = pl.BlockSpec((tq, LANES), lambda g, i: (i, g))
    full = pl.BlockSpec((s, LANES), lambda g, i: (0, g))
    once = lambda spec_map: pl.BlockSpec((s, LANES), spec_map, pipeline_mode=pl.Buffered(1))
    wide = _sds((s, n_pairs * LANES), BF16)
    outs = pl.pallas_call(
        kern, name="mla_bwd", grid=(n_pairs, s // tq),
        in_specs=[qblk, pl.BlockSpec((tq, LANES), lambda g, i: (i, g // 2)), once(lambda g, i: (0, g)), once(lambda g, i: (0, 0)),
                  once(lambda g, i: (0, g)), qblk, qblk, qblk] + [HBM_SPEC] * n_ride,
        out_specs=[qblk, qblk, full, full, full] + [HBM_SPEC] * n_ride,
        out_shape=[wide] * 5 + [_sds((N_SHARD - 1,) + p.shape[1:], p.dtype) for p in riders],
        scratch_shapes=[pltpu.VMEM((s, LANES), F32)] * 3 + (_scatter_sems(n_ride) if n_ride else []), compiler_params=_params(2),
    )(qn, qr, kn, kr, v, o, do, lse, *riders)
    return outs[:5], outs[5:]


def _sb_masks(tk):
    j = lax.broadcasted_iota(jnp.int32, (tk, tk), 0)
    c = lax.broadcasted_iota(jnp.int32, (tk, tk), 1)
    return (j > c).astype(BF16), (j < c).astype(BF16)


def _sb_scores(qs, kk, msuf, strict):
    z = _dot_nt(qs, kk)
    lom = -(jnp.maximum(z, 0.0) + jnp.log(1.0 + jnp.exp(-jnp.abs(z))))
    if strict is not None:
        lom = jnp.where(strict, lom, 0.0)
    return z, lom, _dot(lom.astype(BF16), msuf)


def _sb_strict(tq, tk, d):
    row = lax.broadcasted_iota(jnp.int32, (tq, tk), 0)
    col = lax.broadcasted_iota(jnp.int32, (tq, tk), 1)
    return col + d * tk < row


def _sb_fwd(q, k, v, msuf):
    s = q.shape[0]
    tq, tk = min(SB_TQ, s), min(SB_TK, s)
    ratio = tq // tk

    def kern(q_ref, k_ref, v_ref, m_ref, o_ref, c_ref):
        i = pl.program_id(1)
        msf = m_ref[...]
        lane = lax.broadcasted_iota(jnp.int32, (tq, LANES), 1)
        mine = [_lanes(tq, SB_DIM * half, SB_DIM) for half in range(2)]
        qs = [_keep(m, q_ref[...]) * 0.125 for m in mine]

        def block(kb, carry, dd, half):
            c, acc, cm = carry
            ks = pl.ds(pl.multiple_of(kb * tk, tk), tk)
            strict = None if dd is None else _sb_strict(tq, tk, dd)
            z, lom, suf = _sb_scores(qs[half], k_ref[ks, :], msf, strict)
            a = jnp.exp(z + lom + (suf + c))
            if strict is not None:
                a = jnp.where(strict, a, 0.0)
            acc = acc + _dot(a.astype(BF16), v_ref[ks, :])
            cm = jnp.where(lane == kb, c, cm)
            return c + jnp.sum(lom, axis=1, keepdims=True), acc, cm

        init = (jnp.zeros((tq, 1), F32), jnp.zeros((tq, LANES), F32), jnp.full((tq, LANES), NEG, F32))
        carries = [init, init]
        for dd in range(ratio - 1, -1, -1):
            carries = [block(i * ratio + dd, carries[half], dd, half) for half in range(2)]

        def live(st):
            return jnp.logical_and(st[0] >= 0, jnp.maximum(jnp.max(st[1][0]), jnp.max(st[2][0])) > -SB_SKIP)

        def step(st):
            return (st[0] - 1, block(st[0], st[1], None, 0), block(st[0], st[2], None, 1))

        _, done0, done1 = lax.while_loop(live, step, (i * ratio - 1, carries[0], carries[1]))
        o_ref[...] = _keep(mine[0], done0[1]) + _keep(mine[1], done1[1])
        c_ref[:, :LANES] = done0[2]
        c_ref[:, LANES:] = done1[2]

    qblk = lambda n: pl.BlockSpec((tq, n), lambda g, i: (i, g))
    full = pl.BlockSpec((s, LANES), lambda g, i: (0, g))
    return pl.pallas_call(
        kern, name="sb_fwd", grid=(SB_HEADS // 2, s // tq),
        in_specs=[qblk(LANES), full, full, pl.BlockSpec((tk, tk), lambda g, i: (0, 0))],
        out_specs=[qblk(LANES), qblk(2 * LANES)],
        out_shape=[_sds((s, SB_WIDTH), F32), _sds((s, SB_HEADS * LANES), F32)], compiler_params=_params(2),
    )(q, k, v, msuf)


def _sb_bwd(q, k, v, do, cmat, msuf, mpre, riders=()):
    s = q.shape[0]
    tq, tk = min(SB_TQ, s), min(SB_TK, s)
    ratio = tq // tk

    n_ride = len(riders)
    n_pairs = SB_HEADS // 2

    def kern(q_ref, k_ref, v_ref, do_ref, c_ref, ms_ref, mp_ref, *rest):
        dq_ref, dk_out, dv_out = rest[n_ride:n_ride + 3]
        dk_ref, dv_ref = rest[2 * n_ride + 3:2 * n_ride + 5]
        i = pl.program_id(1)
        if n_ride:
            start, finish = _swap_steps(rest[:n_ride], rest[n_ride + 3:2 * n_ride + 3], *rest[2 * n_ride + 5:])
            pl.when(jnp.logical_and(pl.program_id(0) == 0, i == 0))(start)

        @pl.when(i == 0)
        def _():
            dk_ref[...] = jnp.zeros_like(dk_ref)
            dv_ref[...] = jnp.zeros_like(dv_ref)

        msf = ms_ref[...]
        mpf = mp_ref[...]
        lane = lax.broadcasted_iota(jnp.int32, (tq, LANES), 1)
        lane1 = lax.broadcasted_iota(jnp.int32, (1, LANES), 1)
        mine = [_lanes(tq, SB_DIM * half, SB_DIM) for half in range(2)]
        qv = [_keep(m, q_ref[...]) for m in mine]
        qs = [t * 0.125 for t in qv]
        do_b = [_keep(m, do_ref[...]).astype(BF16) for m in mine]
        cm = [c_ref[:, :LANES], c_ref[:, LANES:]]

        def block(kb, carry, dd, half):
            dq_acc, pc = carry
            ks = pl.ds(pl.multiple_of(kb * tk, tk), tk)
            kk = k_ref[ks, :]
            strict = None if dd is None else _sb_strict(tq, tk, dd)
            z, lom, suf = _sb_scores(qs[half], kk, msf, strict)
            c = jnp.sum(jnp.where(lane == kb, cm[half], 0.0), axis=1, keepdims=True)
            a = jnp.exp(z + lom + (suf + c))
            if strict is not None:
                a = jnp.where(strict, a, 0.0)
            g = _dot_nt(do_b[half], v_ref[ks, :]) * a
            p = pc + _dot(g.astype(BF16), mpf)
            omb = jnp.exp(lom)
            dz = (g * omb - (1.0 - omb) * p) * 0.125
            if strict is not None:
                dz = jnp.where(strict, dz, 0.0)
            dz = dz.astype(BF16)
            dv_ref[ks, :] += _dot_tn(a.astype(BF16), do_b[half])
            dk_ref[ks, :] += _dot_tn(dz, qv[half])
            return dq_acc + _dot(dz, kk), pc + jnp.sum(g, axis=1, keepdims=True)

        def needed(cm_h):
            seen = jnp.logical_and(jnp.max(cm_h, axis=0, keepdims=True) > -SB_SKIP, lane1 < i * ratio)
            return jnp.sum(seen.astype(jnp.int32))

        first = i * ratio - jnp.maximum(needed(cm[0]), needed(cm[1]))
        init = (jnp.zeros((tq, LANES), F32), jnp.zeros((tq, 1), F32))
        carries = lax.fori_loop(first, i * ratio, lambda kb, c: (block(kb, c[0], None, 0), block(kb, c[1], None, 1)), (init, init))
        for dd in range(ratio):
            carries = [block(i * ratio + dd, carries[half], dd, half) for half in range(2)]
        dq_ref[...] = (_keep(mine[0], carries[0][0]) + _keep(mine[1], carries[1][0])).astype(BF16)

        @pl.when(i == s // tq - 1)
        def _():
            dk_out[...] = dk_ref[...].astype(BF16)
            dv_out[...] = dv_ref[...].astype(BF16)

        if n_ride:
            pl.when(jnp.logical_and(pl.program_id(0) == n_pairs - 1, i == s // tq - 1))(finish)

    qblk = lambda n: pl.BlockSpec((tq, n), lambda g, i: (i, g))
    full = pl.BlockSpec((s, LANES), lambda g, i: (0, g))
    msk = pl.BlockSpec((tk, tk), lambda g, i: (0, 0))
    outs = pl.pallas_call(
        kern, name="sb_bwd", grid=(n_pairs, s // tq),
        in_specs=[qblk(LANES), full, full, qblk(LANES), qblk(2 * LANES), msk, msk] + [HBM_SPEC] * n_ride,
        out_specs=[qblk(LANES), full, full] + [HBM_SPEC] * n_ride,
        out_shape=[_sds((s, SB_WIDTH), BF16)] * 3 + _halves_shapes(riders),
        scratch_shapes=[pltpu.VMEM((s, LANES), F32)] * 2 + (_swap_sems(n_ride) if n_ride else []), compiler_params=_params(2),
    )(q, k, v, do, cmat, msuf, mpre, *riders)
    return outs[:3], outs[3:]


def _place():
    return lax.axis_index("x"), lax.axis_index("y"), lax.axis_index("c")


def _other_chips(x, y):
    return [(1 - x, y), (x, 1 - y), (1 - x, 1 - y)]


HBM_SPEC = pl.BlockSpec(memory_space=pl.ANY)


def _gather_steps(ins, outs, send_sems, recv_sems):
    n = len(ins)
    x, y, c = _place()
    sibling = (x, y, 1 - c)
    chips = _other_chips(x, y)

    def half_of(a, ref, pc):
        half = ins[a].shape[0] // 2
        return ref.at[pl.ds(pl.multiple_of(pc * half, 16), half), :]

    def copy(a, k, chip, pc, to, src=None):
        dst = half_of(a, outs[a].at[2 * chip[0] + chip[1]], pc)
        return pltpu.make_async_remote_copy(src_ref=dst if src is None else src, dst_ref=dst, send_sem=send_sems.at[7 * a + k],
                                            recv_sem=recv_sems.at[7 * a + k], device_id=to, device_id_type=MESH)

    def own(a):
        return pltpu.make_async_remote_copy(src_ref=ins[a], dst_ref=outs[a].at[2 * x + y], send_sem=send_sems.at[7 * a + 6],
                                            recv_sem=recv_sems.at[7 * a + 6], device_id=sibling, device_id_type=MESH)

    def first():
        far = [copy(a, j, (x, y), c, (*chip, c), src=half_of(a, ins[a], c)) for a in range(n) for j, chip in enumerate(chips)]
        return far + [own(a) for a in range(n)]

    def passed():
        return [copy(a, 3 + j, chip, c, sibling) for j, chip in enumerate(chips) for a in range(n)]

    def send():
        for cp in first():
            cp.start()

    def forward():
        for j, chip in enumerate(chips):
            for a in range(n):
                copy(a, j, chip, c, sibling).wait_recv()
        for cp in passed():
            cp.start()

    def finish():
        for j, chip in enumerate(chips):
            for a in range(n):
                copy(a, 3 + j, chip, 1 - c, sibling).wait_recv()
        for a in range(n):
            own(a).wait_recv()
        for cp in first() + passed():
            cp.wait_send()

    return send, forward, finish


def _gather_sems(n):
    return [pltpu.SemaphoreType.DMA((7 * n,)), pltpu.SemaphoreType.DMA((7 * n,))]


def _allgather_list(name, shards):
    n = len(shards)

    def body(*refs):
        for stage in _gather_steps(refs[:n], refs[n:2 * n], *refs[2 * n:]):
            stage()

    return pl.pallas_call(
        body, name=name, out_shape=[_sds((N_SHARD,) + a.shape, a.dtype) for a in shards], in_specs=[HBM_SPEC] * n,
        out_specs=[HBM_SPEC] * n, scratch_shapes=_gather_sems(n),
    )(*shards)


def _swap_steps(ins, outs, send_sems, recv_sems):
    x, y, c = _place()

    def copies():
        out = []
        for a in range(len(ins)):
            h = ins[a].shape[1] // 2
            src = ins[a].at[:, pl.ds(pl.multiple_of((1 - c) * h, 8), h), :]
            out.append(pltpu.make_async_remote_copy(src_ref=src, dst_ref=outs[a], send_sem=send_sems.at[a], recv_sem=recv_sems.at[a],
                                                    device_id=(x, y, 1 - c), device_id_type=MESH))
        return out

    def start():
        for cp in copies():
            cp.start()

    def finish():
        for cp in copies():
            cp.wait()

    return start, finish


def _swap_sems(n):
    return [pltpu.SemaphoreType.DMA((n,)), pltpu.SemaphoreType.DMA((n,))]


def _halves_shapes(gs):
    return [_sds((N_SHARD, g.shape[1] // 2, g.shape[2]), g.dtype) for g in gs]


def _swap_halves(name, gs):
    n = len(gs)

    def body(*refs):
        for stage in _swap_steps(refs[:n], refs[n:2 * n], *refs[2 * n:]):
            stage()

    return pl.pallas_call(body, name=name, out_shape=_halves_shapes(gs), in_specs=[HBM_SPEC] * n, out_specs=[HBM_SPEC] * n,
                          scratch_shapes=_swap_sems(n))(*gs)


def _add_sibling(name, gs, gots, c_idx):
    n = len(gs)

    def kern(c_ref, *refs):
        for a in range(n):
            tot = refs[a][...] + refs[n + a][...]
            refs[2 * n + a][...] = tot
            refs[3 * n + a][...] = tot.astype(BF16)

    quarter = lambda g: (None, g.shape[1] // 4, g.shape[2])
    in_specs = [pl.BlockSpec(quarter(g), lambda b, s, c_ref: (b, 2 * c_ref[0] + s, 0)) for g in gs]
    in_specs += [pl.BlockSpec(quarter(g), lambda b, s, c_ref: (b, s, 0)) for g in gs]
    out_specs = [pl.BlockSpec(quarter(g), lambda b, s, c_ref: (b, s, 0)) for g in gs] * 2
    out_shape = [_sds(t.shape, F32) for t in gots] + [_sds(t.shape, BF16) for t in gots]
    outs = pl.pallas_call(
        kern, name=name, out_shape=out_shape,
        grid_spec=pltpu.PrefetchScalarGridSpec(num_scalar_prefetch=1, grid=(N_SHARD, 2), in_specs=in_specs, out_specs=out_specs),
        compiler_params=_params(2),
    )(c_idx.reshape(1), *gs, *gots)
    return outs[:n], outs[n:]


def _scatter_steps(ins, outs, send_sems, recv_sems):
    x, y, c = _place()

    def copies():
        return [pltpu.make_async_remote_copy(
            src_ref=ins[a].at[2 * px + py], dst_ref=outs[a].at[j], send_sem=send_sems.at[3 * a + j], recv_sem=recv_sems.at[3 * a + j],
            device_id=(px, py, c), device_id_type=MESH) for a in range(len(ins)) for j, (px, py) in enumerate(_other_chips(x, y))]

    def start():
        for cp in copies():
            cp.start()

    def finish():
        for cp in copies():
            cp.wait()

    return start, finish


def _scatter_sems(n):
    return [pltpu.SemaphoreType.DMA((3 * n,)), pltpu.SemaphoreType.DMA((3 * n,))]


def _add_chips(name, ps, others, shard_idx):
    n = len(ps)

    def kern(b_ref, *refs):
        for a in range(n):
            tot = refs[a][...]
            for j in range(N_SHARD - 1):
                tot = tot + refs[n + a][j].astype(F32)
            refs[2 * n + a][...] = tot

    in_specs = [pl.BlockSpec((None, p.shape[1] // 2, p.shape[2]), lambda s, b_ref: (b_ref[0], s, 0)) for p in ps]
    in_specs += [pl.BlockSpec((N_SHARD - 1, p.shape[1] // 2, p.shape[2]), lambda s, b_ref: (0, s, 0)) for p in ps]
    out_specs = [pl.BlockSpec((p.shape[1] // 2, p.shape[2]), lambda s, b_ref: (s, 0)) for p in ps]
    return pl.pallas_call(
        kern, name=name, out_shape=[_sds(p.shape[1:], F32) for p in ps],
        grid_spec=pltpu.PrefetchScalarGridSpec(num_scalar_prefetch=1, grid=(2,), in_specs=in_specs, out_specs=out_specs),
        compiler_params=_params(1),
    )(shard_idx.reshape(1), *ps, *others)


def _swap_result(name, mines):
    n = len(mines)

    def body(*refs):
        ins, outs = refs[:n], refs[n:2 * n]
        send_sems, recv_sems = refs[2 * n:]
        x, y, c = _place()
        copies = [pltpu.make_async_remote_copy(src_ref=ins[a], dst_ref=outs[a], send_sem=send_sems.at[a], recv_sem=recv_sems.at[a],
                                               device_id=(x, y, 1 - c), device_id_type=MESH) for a in range(n)]
        for cp in copies:
            cp.start()
        for cp in copies:
            cp.wait()

    return pl.pallas_call(
        body, name=name, out_shape=[_sds(m.shape, m.dtype) for m in mines], in_specs=[HBM_SPEC] * n,
        out_specs=[HBM_SPEC] * n, scratch_shapes=[pltpu.SemaphoreType.DMA((n,)), pltpu.SemaphoreType.DMA((n,))],
    )(*mines)


def _allreduce_small(v):
    m_per, n = v.shape

    def body(x_ref, tot_ref, all_ref, send_sems, recv_sems, local_sem):
        x, y, c = _place()
        me, sibling = (x, y, c), (x, y, 1 - c)
        chips = _other_chips(x, y)

        def rows(px, py, pc):
            return all_ref.at[pl.ds(pl.multiple_of((4 * px + 2 * py + pc) * m_per, 8), m_per), :]

        def copy(k, block, to, src=None):
            return pltpu.make_async_remote_copy(
                src_ref=rows(*block) if src is None else src, dst_ref=rows(*block), send_sem=send_sems.at[k],
                recv_sem=recv_sems.at[k], device_id=to, device_id_type=MESH)

        mine = pltpu.make_async_copy(x_ref, rows(*me), local_sem)
        mine.start()
        first = [copy(0, me, sibling, src=x_ref)] + [copy(1 + j, me, (*chip, c), src=x_ref) for j, chip in enumerate(chips)]
        for cp in first:
            cp.start()
        passed = [copy(4 + j, (*chip, c), sibling) for j, chip in enumerate(chips)]
        for j, chip in enumerate(chips):
            copy(1 + j, (*chip, c), me).wait_recv()
            passed[j].start()
        copy(0, sibling, me).wait_recv()
        for j, chip in enumerate(chips):
            copy(4 + j, (*chip, 1 - c), me).wait_recv()
        for cp in first + passed:
            cp.wait_send()
        mine.wait()
        tot = all_ref[0:m_per, :]
        for dev in range(1, 8):
            tot = tot + all_ref[dev * m_per:(dev + 1) * m_per, :]
        tot_ref[...] = tot

    vmem = pl.BlockSpec(memory_space=pltpu.VMEM)
    return pl.pallas_call(
        body, name="allreduce_small", out_shape=_sds((m_per, n), F32), in_specs=[vmem], out_specs=vmem,
        scratch_shapes=[pltpu.VMEM((8 * m_per, n), F32), pltpu.SemaphoreType.DMA((7,)), pltpu.SemaphoreType.DMA((7,)),
                        pltpu.SemaphoreType.DMA],
    )(v)


def _adam_update(w, g, m, v):
    m_new = ADAM_B1 * m + (1.0 - ADAM_B1) * g
    v_new = ADAM_B2 * v + (1.0 - ADAM_B2) * (g * g)
    m_hat = m_new / (1.0 - ADAM_B1 ** ADAM_STEP)
    v_hat = v_new / (1.0 - ADAM_B2 ** ADAM_STEP)
    return -ADAM_LR * (m_hat / (jnp.sqrt(v_hat) + ADAM_EPS) + ADAM_WD * w), m_new, v_new


def _adamw(name, w, g, m, v):
    rows, width = w.shape
    tr = rows // 4 if rows % 32 == 0 else rows

    def kern(w_ref, g_ref, m_ref, v_ref, d_ref, mo_ref, vo_ref):
        d_ref[...], mo_ref[...], vo_ref[...] = _adam_update(w_ref[...], g_ref[...], m_ref[...], v_ref[...])

    spec = pl.BlockSpec((tr, width), lambda i: (i, 0))
    return pl.pallas_call(kern, name=name, grid=(rows // tr,), in_specs=[spec] * 4, out_specs=[spec] * 3,
                          out_shape=[_sds((rows, width), F32)] * 3, compiler_params=_params(1))(w, g, m, v)


def _adamw_halves(name, ws, mines, theirs, ms, vs, c_idx, riders=()):
    n, n_ride = len(ws), len(riders)

    def kern(c_ref, *refs):
        ins, rest = refs[:5 * n], refs[5 * n:]
        outs = rest[n_ride:n_ride + 4 * n]
        first = jnp.logical_and(pl.program_id(0) == 0, pl.program_id(1) == 0)
        last = jnp.logical_and(pl.program_id(0) == 1, pl.program_id(1) == ADAM_STEPS - 1)
        if n_ride:
            start, finish = _scatter_steps(rest[:n_ride], rest[n_ride + 4 * n:2 * n_ride + 4 * n], *rest[2 * n_ride + 4 * n:])
            pl.when(first)(start)
        take_mine = pl.program_id(0) == c_ref[0]
        for a in range(n):
            w_ref, mine_ref, theirs_ref, m_ref, v_ref = ins[5 * a:5 * a + 5]
            g_ref, d_ref, mo_ref, vo_ref = outs[4 * a:4 * a + 4]
            g = jnp.where(take_mine, mine_ref[...], theirs_ref[...])
            g_ref[...] = g
            d_ref[...], mo_ref[...], vo_ref[...] = _adam_update(w_ref[...], g, m_ref[...], v_ref[...])
        if n_ride:
            pl.when(last)(finish)

    in_specs, out_specs, out_shape = [], [], []
    for w in ws:
        rows, width = w.shape
        tr = rows // (2 * ADAM_STEPS)
        whole = pl.BlockSpec((tr, width), lambda h, j, c_ref: (ADAM_STEPS * h + j, 0))
        part = pl.BlockSpec((tr, width), lambda h, j, c_ref: (j, 0))
        in_specs += [whole, part, part, whole, whole]
        out_specs += [whole] * 4
        out_shape += [_sds((rows, width), F32)] * 4
    operands = [t for group in zip(ws, mines, theirs, ms, vs) for t in group]
    outs = pl.pallas_call(
        kern, name=name, out_shape=out_shape + [_sds((N_SHARD - 1,) + p.shape[1:], p.dtype) for p in riders],
        grid_spec=pltpu.PrefetchScalarGridSpec(
            num_scalar_prefetch=1, grid=(2, ADAM_STEPS), in_specs=in_specs + [HBM_SPEC] * n_ride,
            out_specs=out_specs + [HBM_SPEC] * n_ride, scratch_shapes=_scatter_sems(n_ride) if n_ride else []),
        compiler_params=_params(2),
    )(c_idx.reshape(1), *operands, *riders)
    return [outs[4 * a:4 * a + 4] for a in range(n)], outs[4 * n:]


SHARDED = (("w_in", D_MODEL, IN_WIDTH, 1), ("w_uq", Q_RANK, MLA_HEADS * MLA_QK, 1),
           ("w_ukv", KV_RANK, MLA_HEADS * (MLA_NOPE + MLA_V), 1), ("w_o", D_MODEL, D_MODEL, 0),
           ("w_gate", D_MODEL, D_FF, 1), ("w_up", D_MODEL, D_FF, 1), ("w_down", D_FF, D_MODEL, 0))
EARLY = ("w_in", "w_uq", "w_ukv")
LATE = ("w_o", "w_gate", "w_up", "w_down")
FLIPPED = ("w_gate", "w_up")
SMALL = (("norm_mix", D_MODEL), ("q_latent_norm", Q_RANK), ("kv_latent_norm", KV_RANK), ("out_norm_mla", MLA_WIDTH),
         ("out_norm_sb", SB_WIDTH), ("norm_ffn", D_MODEL), ("norm_final", D_MODEL))


def _full_weight(gathered, axis):
    n_sh, k, n = gathered.shape
    return gathered.transpose(1, 0, 2).reshape(k, n_sh * n) if axis == 1 else gathered.reshape(n_sh * k, n)


def _shard_major(g, axis):
    r, c = g.shape
    return g.reshape(r, N_SHARD, c // N_SHARD).transpose(1, 0, 2) if axis == 1 else g.reshape(N_SHARD, r // N_SHARD, c)


def _rot_cols(w):
    hh = MLA_ROPE // 2
    return jnp.concatenate([-w[..., hh:], w[..., :hh]], axis=-1)


def _rot_cols_t(g):
    hh = MLA_ROPE // 2
    return jnp.concatenate([g[..., hh:], -g[..., :hh]], axis=-1)


def _with_transposes(w):
    w.update({name + "_t": t.T for name, t in list(w.items())})
    return w


def _attention_weights(full, small):
    w_in = full["w_in"]
    s0, s1, s2 = Q_RANK, Q_RANK + KV_RANK, Q_RANK + KV_RANK + MLA_ROPE
    uq = full["w_uq"].reshape(Q_RANK, MLA_HEADS, MLA_QK)
    ukv = full["w_ukv"].reshape(KV_RANK, MLA_HEADS, MLA_NOPE + MLA_V)
    w_kr = w_in[:, s1:s2]
    per_tile = ROPE_TILE // MLA_ROPE
    w = _with_transposes({
        "w_cq": w_in[:, :s0], "w_ckv": w_in[:, s0:s1],
        "w_kr4": jnp.tile(w_kr, (1, per_tile)), "w_kr4r": jnp.tile(_rot_cols(w_kr), (1, per_tile)),
        "w_kr8": jnp.tile(w_kr, (1, MLA_HEADS)), "w_kr8r": jnp.tile(_rot_cols(w_kr), (1, MLA_HEADS)),
        "w_sbq": w_in[:, s2:s2 + SB_WIDTH], "w_sbk": w_in[:, s2 + SB_WIDTH:s2 + 2 * SB_WIDTH], "w_sbv": w_in[:, s2 + 2 * SB_WIDTH:],
        "w_qn": uq[..., :MLA_NOPE].reshape(Q_RANK, -1), "w_qr": uq[..., MLA_NOPE:].reshape(Q_RANK, -1),
        "w_qrr": _rot_cols(uq[..., MLA_NOPE:]).reshape(Q_RANK, -1),
        "w_kn": ukv[..., :MLA_NOPE].reshape(KV_RANK, -1), "w_v": ukv[..., MLA_NOPE:].reshape(KV_RANK, -1),
    })
    w.update(g_mix=small["norm_mix"], g_q=small["q_latent_norm"], g_kv=small["kv_latent_norm"], g_a=small["out_norm_mla"],
             g_b=small["out_norm_sb"], g_f=small["norm_ffn"], g_n=small["norm_final"])
    return w


def _ffn_weights(full):
    w = _with_transposes({"w_oa": full["w_o"][:MLA_WIDTH], "w_ob": full["w_o"][MLA_WIDTH:], "w_down": full["w_down"]})
    for name in FLIPPED:
        w[name + "_t"] = full[name]
        w[name] = full[name].T
    return w


def _rope_tables(positions):
    inv_freq = ROPE_THETA ** (-jnp.arange(0, MLA_ROPE, 2, dtype=F32) / MLA_ROPE)
    ang = positions.astype(F32)[:, None] * inv_freq[None, :]
    cos, sin = jnp.cos(ang), jnp.sin(ang)
    return {"cos": jnp.tile(jnp.concatenate([cos, cos], axis=1), (1, MLA_HEADS)),
            "sin": jnp.tile(jnp.concatenate([sin, sin], axis=1), (1, MLA_HEADS))}


def _by_head(g_wide, g_narrow, wide, narrow):
    r = g_wide.shape[0]
    return jnp.concatenate([g_wide.reshape(r, MLA_HEADS, wide), g_narrow.reshape(r, MLA_HEADS, narrow)], axis=-1).reshape(r, -1)


def kernel(x, positions, norm_mix, w_in, q_latent_norm, w_uq, kv_latent_norm, w_ukv, out_norm_mla, out_norm_sb, w_o, norm_ffn, w_gate, w_up, w_down, norm_final, loss_target, m_norm_mix, m_w_in, m_q_latent_norm, m_w_uq, m_kv_latent_norm, m_w_ukv, m_out_norm_mla, m_out_norm_sb, m_w_o, m_norm_ffn, m_w_gate, m_w_up, m_w_down, m_norm_final, v_norm_mix, v_w_in, v_q_latent_norm, v_w_uq, v_kv_latent_norm, v_w_ukv, v_out_norm_mla, v_out_norm_sb, v_w_o, v_norm_ffn, v_w_gate, v_w_up, v_w_down, v_norm_final):
    given = dict(norm_mix=norm_mix, w_in=w_in, q_latent_norm=q_latent_norm, w_uq=w_uq, kv_latent_norm=kv_latent_norm, w_ukv=w_ukv,
                 out_norm_mla=out_norm_mla, out_norm_sb=out_norm_sb, w_o=w_o, norm_ffn=norm_ffn, w_gate=w_gate, w_up=w_up,
                 w_down=w_down, norm_final=norm_final)
    mom_m = dict(norm_mix=m_norm_mix, w_in=m_w_in, q_latent_norm=m_q_latent_norm, w_uq=m_w_uq, kv_latent_norm=m_kv_latent_norm,
                 w_ukv=m_w_ukv, out_norm_mla=m_out_norm_mla, out_norm_sb=m_out_norm_sb, w_o=m_w_o, norm_ffn=m_norm_ffn,
                 w_gate=m_w_gate, w_up=m_w_up, w_down=m_w_down, norm_final=m_norm_final)
    mom_v = dict(norm_mix=v_norm_mix, w_in=v_w_in, q_latent_norm=v_q_latent_norm, w_uq=v_w_uq, kv_latent_norm=v_kv_latent_norm,
                 w_ukv=v_w_ukv, out_norm_mla=v_out_norm_mla, out_norm_sb=v_out_norm_sb, w_o=v_w_o, norm_ffn=v_norm_ffn,
                 w_gate=v_w_gate, w_up=v_w_up, w_down=v_w_down, norm_final=v_norm_final)
    xs = x[0]
    tgt = loss_target[0]
    s = xs.shape[0]
    c_idx = lax.axis_index("c")
    shard_idx = 2 * lax.axis_index("x") + lax.axis_index("y")

    def block2d(t, name):
        t = t.reshape(t.shape[-2:])
        return t.T if name in FLIPPED else t

    shard2d = {name: block2d(given[name], name) for name, *_ in SHARDED}
    local = {name: shard2d[name].astype(BF16) for name, *_ in SHARDED}
    axis_of = {name: 0 if name in FLIPPED else axis for name, _, _, axis in SHARDED}

    def whole(names, gathered):
        return {name: _full_weight(t, axis_of[name]) for name, t in zip(names, gathered)}

    small = {name: given[name].reshape(1, n) for name, n in SMALL}
    w = _attention_weights(whole(EARLY, _allgather_list("allgather_w", [local[name] for name in EARLY])), small)
    tabs = _rope_tables(positions[0])
    msuf, mpre = _sb_masks(min(SB_TK, s))

    u, cq, ckv, cqn, ckvn, qn, qr, kn, vm, kr, sq, sk, sv = _fwd_a(xs, tabs, w)
    o_mla, lse, late = _mla_fwd(qn, qr, kn, kr, vm, [local[name] for name in LATE])
    w.update(_ffn_weights(whole(LATE, late)))
    o_sb, cmat = _sb_fwd(sq, sk, sv, msuf)
    merged, h1, f, gate, up, act = _fwd_b1(xs, o_mla, o_sb, w)
    dh2, loss_part, dg_n = _fwd_b2(h1, act, tgt, w)

    def shards_of(names, grads):
        return [_shard_major(grads[name], axis_of[name]) for name in names]

    def reduced(tag, chip_f32, others):
        mine = _add_chips("add_chips_" + tag, chip_f32, others, shard_idx)
        return tuple(mine), tuple(_swap_result("swap_result_" + tag, mine))

    dgate, dup, dh1, do_mla, do_sb, dg_f, dg_a, dg_b = _bwd_b(dh2, gate, up, h1, o_mla, o_sb, w)
    late_gs = shards_of(LATE, {
        "w_o": _tn_matmul("dw_o", merged, dh1), "w_gate": _tn_matmul("dw_gate", dgate, f),
        "w_up": _tn_matmul("dw_up", dup, f), "w_down": _tn_matmul("dw_down", act, dh2)})
    (dsq, dsk, dsv), late_got = _sb_bwd(sq, sk, sv, do_sb, cmat, msuf, mpre, late_gs)
    late_f32, late_bf16 = _add_sibling("add_sibling_late", late_gs, late_got, c_idx)
    (dqn, dqr, dkn, dkr, dvm), late_others = _mla_bwd(qn, qr, kn, kr, vm, o_mla, do_mla, lse, late_bf16)
    mine_late, theirs_late = reduced("late", late_f32, late_others)
    dx, a1, a2, dcq, dckv, dkrc, dkrs, dg_q, dg_kv, dg_mix = _bwd_a(xs, dh1, cq, ckv, dqn, dqr, dkn, dvm, dkr, dsq, dsk, dsv, tabs, w)

    g_cq, g_ckv, g_krc, g_krs, g_sq, g_sk, g_sv = _tn_multi("dw_in", u, [dcq, dckv, dkrc, dkrs, dsq, dsk, dsv])
    g_qn, g_qr1, g_qr2 = _tn_multi("dw_uq", cqn, [dqn, a1, a2])
    g_kn, g_v = _tn_multi("dw_ukv", ckvn, [dkn, dvm])
    slots = lambda g: g.reshape(g.shape[0], MLA_HEADS, MLA_ROPE)
    g_kr = jnp.sum(slots(g_krc), axis=1) + _rot_cols_t(jnp.sum(slots(g_krs), axis=1))
    g_qr = (slots(g_qr1) + _rot_cols_t(slots(g_qr2))).reshape(Q_RANK, -1)
    early_gs = shards_of(EARLY, {
        "w_in": jnp.concatenate([g_cq, g_ckv, g_kr, g_sq, g_sk, g_sv], axis=1),
        "w_uq": _by_head(g_qn, g_qr, MLA_NOPE, MLA_ROPE),
        "w_ukv": _by_head(g_kn, g_v, MLA_NOPE, MLA_V)})
    early_f32, early_bf16 = _add_sibling("add_sibling_early", early_gs, _swap_halves("swap_halves_early", early_gs), c_idx)

    def update(name, names, halves_of, riders=()):
        tiles = lambda src: [block2d(src[k], k) for k in names]
        return _adamw_halves(name, [shard2d[k] for k in names], [halves_of[k][0] for k in names], [halves_of[k][1] for k in names],
                             tiles(mom_m), tiles(mom_v), c_idx, riders)

    late_new, early_others = update("adamw_late", LATE, dict(zip(LATE, zip(mine_late, theirs_late))), early_bf16)
    mine_early, theirs_early = reduced("early", early_f32, early_others)
    early_new, _ = update("adamw_early", EARLY, dict(zip(EARLY, zip(mine_early, theirs_early))))
    updated = dict(zip(EARLY + LATE, list(early_new) + list(late_new)))

    small_parts = jnp.concatenate([dg_mix, dg_q, dg_kv, dg_a, dg_b, dg_f, dg_n, loss_part], axis=1)
    small_sum = _allreduce_small(jnp.broadcast_to(small_parts, (8, small_parts.shape[1])))
    small_g, loss = small_sum[0:1, :-LANES], small_sum[0, -LANES]

    g_out, d_out, m_out, v_out = {}, {}, {}, {}
    for name, *_ in SHARDED:
        shape = given[name].shape
        g_out[name], d_out[name], m_out[name], v_out[name] = ((t.T if name in FLIPPED else t).reshape(shape) for t in updated[name])
    cat = lambda src: jnp.concatenate([src[name].reshape(1, n) for name, n in SMALL], axis=1)
    d, mn, vn = _adamw("adamw_small", cat(given), small_g, cat(mom_m), cat(mom_v))
    off = 0
    for name, n in SMALL:
        shape = given[name].shape
        g_out[name], d_out[name], m_out[name], v_out[name] = (t[:, off:off + n].reshape(shape) for t in (small_g, d, mn, vn))
        off += n

    order = ["norm_mix", "w_in", "q_latent_norm", "w_uq", "kv_latent_norm", "w_ukv", "out_norm_mla", "out_norm_sb", "w_o",
             "norm_ffn", "w_gate", "w_up", "w_down", "norm_final"]
    return (loss, dx[None], *[g_out[n] for n in order], *[d_out[n] for n in order], *[m_out[n] for n in order],
            *[v_out[n] for n in order])
```

```python
import math

import jax
import jax.numpy as jnp
from jax import lax
from jax.experimental import pallas as pl
from jax.experimental.pallas import tpu as pltpu

F32 = jnp.float32
BF16 = jnp.bfloat16
MESH = pl.DeviceIdType.MESH

D_MODEL = 1024
EPS = 1e-6
MLA_HEADS = 8
MLA_NOPE = 64
MLA_ROPE = 32
MLA_V = 64
MLA_QK = MLA_NOPE + MLA_ROPE
Q_RANK = 256
KV_RANK = 128
ROPE_THETA = 10000.0
SB_HEADS = 8
SB_DIM = 64
MLA_WIDTH = MLA_HEADS * MLA_V
SB_WIDTH = SB_HEADS * SB_DIM
D_FF = 2816
IN_WIDTH = Q_RANK + KV_RANK + MLA_ROPE + 3 * SB_WIDTH

ADAM_LR = 0.001
ADAM_B1 = 0.9
ADAM_B2 = 0.999
ADAM_EPS = 1e-08
ADAM_WD = 0.01
ADAM_STEP = 10

N_SHARD = 4
LANES = 128
ROPE_TILE = LANES
VMEM_LIMIT = 56 * 1024 * 1024
TN_ACC_BYTES = 6 * 1024 * 1024 + 512 * 1024
NEG = -1e30
MLA_SCALE = 1.0 / math.sqrt(MLA_QK)
MLA_DK_SCALE = math.log(2.0)
MLA_QSCALE = MLA_SCALE * math.log2(math.e)
SB_SKIP = 110.0

ROW_TILE = 512
ROW_TILE_ELEMENTWISE = 256
MLA_TQ = 1024
SB_TQ = 512
MLA_TK = 1024
MLA_BWD_TK = 1024
MLA_DIAG_TK = 256
SB_TK = 256
TN_TS = 2048
ADAM_STEPS = 4


def _dot(a, b):
    return jnp.dot(a, b, preferred_element_type=F32)


def _dot_nt(a, b):
    return lax.dot_general(a, b, (((1,), (1,)), ((), ())), preferred_element_type=F32)


def _dot_tn(a, b):
    return lax.dot_general(a, b, (((0,), (0,)), ((), ())), preferred_element_type=F32)


def _params(n_grid, vmem=VMEM_LIMIT):
    return pltpu.CompilerParams(dimension_semantics=("arbitrary",) * n_grid, vmem_limit_bytes=vmem)


def _rms(x):
    r = lax.rsqrt(jnp.mean(x * x, axis=-1, keepdims=True) + EPS)
    return x * r, r


def _rms_bwd(n, r, g, dy):
    dn = dy * g
    dx = r * (dn - n * jnp.mean(dn * n, axis=-1, keepdims=True))
    return dx, jnp.sum(dy * n, axis=0, keepdims=True)


def _accumulate(ref, val, step):
    @pl.when(step == 0)
    def _():
        ref[...] = val

    @pl.when(step != 0)
    def _():
        ref[...] += val


def _rowwise(name, body, rows, consts, row_out, acc_out, tm):
    n_rows = rows[0].shape[0]
    tm = min(tm, n_rows)
    nr, nc, no = len(rows), len(consts), len(row_out)

    def kern(*refs):
        body(refs[:nr], refs[nr:nr + nc], refs[nr + nc:nr + nc + no], refs[nr + nc + no:], pl.program_id(0))

    in_specs = [pl.BlockSpec((tm, a.shape[1]), lambda i: (i, 0)) for a in rows]
    in_specs += [pl.BlockSpec(a.shape, lambda i: (0, 0), pipeline_mode=pl.Buffered(1)) for a in consts]
    out_specs = [pl.BlockSpec((tm, s.shape[1]), lambda i: (i, 0)) for s in row_out]
    out_specs += [pl.BlockSpec(s.shape, lambda i: (0, 0)) for s in acc_out]
    return pl.pallas_call(
        kern, name=name, grid=(n_rows // tm,), in_specs=in_specs, out_specs=out_specs,
        out_shape=list(row_out) + list(acc_out), compiler_params=_params(1),
    )(*rows, *consts)


def _sds(shape, dtype):
    return jax.ShapeDtypeStruct(shape, dtype)


def _fwd_a(x, tabs, w):
    s = x.shape[0]

    def body(r, c, o, a, step):
        x_ref, cos_ref, sin_ref = r
        gmix, wcq, wckv, wkr, wkrr, wsq, wsk, wsv, gq, wqn, wqr, wqrr, gkv, wkn, wv = c
        u_o, cq_o, ckv_o, cqn_o, ckvn_o, qn_o, qr_o, kn_o, v_o, kr_o, sq_o, sk_o, sv_o = o
        cos, sin = cos_ref[...], sin_ref[...]
        n, _ = _rms(x_ref[...])
        u = (n * gmix[...]).astype(BF16)
        u_o[...] = u
        cq = _dot(u, wcq[...])
        ckv = _dot(u, wckv[...])
        kr_o[...] = (_dot(u, wkr[...]) * cos[:, :ROPE_TILE] + _dot(u, wkrr[...]) * sin[:, :ROPE_TILE]).astype(BF16)
        sq_o[...] = _dot(u, wsq[...]).astype(BF16)
        sk_o[...] = _dot(u, wsk[...]).astype(BF16)
        sv_o[...] = _dot(u, wsv[...]).astype(BF16)
        cq_o[...] = cq
        ckv_o[...] = ckv
        nq, _ = _rms(cq)
        cqn = (nq * gq[...]).astype(BF16)
        cqn_o[...] = cqn
        qn_o[...] = (_dot(cqn, wqn[...]) * MLA_QSCALE).astype(BF16)
        qr_o[...] = ((_dot(cqn, wqr[...]) * cos + _dot(cqn, wqrr[...]) * sin) * MLA_QSCALE).astype(BF16)
        nkv, _ = _rms(ckv)
        ckvn = (nkv * gkv[...]).astype(BF16)
        ckvn_o[...] = ckvn
        kn_o[...] = _dot(ckvn, wkn[...]).astype(BF16)
        v_o[...] = _dot(ckvn, wv[...]).astype(BF16)

    outs = [
        _sds((s, D_MODEL), BF16), _sds((s, Q_RANK), F32), _sds((s, KV_RANK), F32), _sds((s, Q_RANK), BF16),
        _sds((s, KV_RANK), BF16), _sds((s, MLA_HEADS * MLA_NOPE), BF16), _sds((s, MLA_HEADS * MLA_ROPE), BF16),
        _sds((s, MLA_HEADS * MLA_NOPE), BF16), _sds((s, MLA_WIDTH), BF16), _sds((s, ROPE_TILE), BF16),
        _sds((s, SB_WIDTH), BF16), _sds((s, SB_WIDTH), BF16), _sds((s, SB_WIDTH), BF16),
    ]
    consts = [w["g_mix"], w["w_cq"], w["w_ckv"], w["w_kr4"], w["w_kr4r"], w["w_sbq"], w["w_sbk"], w["w_sbv"], w["g_q"],
              w["w_qn"], w["w_qr"], w["w_qrr"], w["g_kv"], w["w_kn"], w["w_v"]]
    return _rowwise("fwd_a", body, [x, tabs["cos"], tabs["sin"]], consts, outs, [], ROW_TILE)


def _fwd_b1(x, o_mla, o_sb, w):
    s = x.shape[0]

    def body(r, c, o, a, step):
        x_ref, oa_ref, ob_ref = r
        ga, gb, woa, wob, gf, wg, wu = c
        mg_o, h1_o, f_o, gate_o, up_o, act_o = o
        na, _ = _rms(oa_ref[...])
        nb, _ = _rms(ob_ref[...])
        ma = (na * ga[...]).astype(BF16)
        mb = (nb * gb[...]).astype(BF16)
        mg_o[:, :MLA_WIDTH] = ma
        mg_o[:, MLA_WIDTH:] = mb
        h1 = x_ref[...] + _dot(ma, woa[...]) + _dot(mb, wob[...])
        h1_o[...] = h1
        nf, _ = _rms(h1)
        f = (nf * gf[...]).astype(BF16)
        f_o[...] = f
        gate = _dot_nt(f, wg[...])
        up = _dot_nt(f, wu[...])
        gate_o[...] = gate.astype(BF16)
        up_o[...] = up.astype(BF16)
        act_o[...] = (gate * (1.0 / (1.0 + jnp.exp(-gate))) * up).astype(BF16)

    outs = [_sds((s, D_MODEL), BF16), _sds((s, D_MODEL), F32), _sds((s, D_MODEL), BF16), _sds((s, D_FF), BF16),
            _sds((s, D_FF), BF16), _sds((s, D_FF), BF16)]
    consts = [w["g_a"], w["g_b"], w["w_oa"], w["w_ob"], w["g_f"], w["w_gate_t"], w["w_up_t"]]
    return _rowwise("fwd_b1", body, [x, o_mla, o_sb], consts, outs, [], ROW_TILE)


def _fwd_b2(h1, act, tgt, w):
    s = h1.shape[0]

    def body(r, c, o, a, step):
        h1_ref, act_ref, t_ref = r
        wd, gn = c
        (dh2_o,) = o
        loss_o, dgn_o = a
        h2 = h1_ref[...] + _dot(act_ref[...], wd[...])
        n2, r2 = _rms(h2)
        err = n2 * gn[...] - t_ref[...]
        part = jnp.sum(jnp.sum(err * err, axis=1, keepdims=True), axis=0, keepdims=True) * (0.5 / D_MODEL)
        _accumulate(loss_o, jnp.broadcast_to(part, (1, LANES)), step)
        dh2, dgn = _rms_bwd(n2, r2, gn[...], err * (1.0 / D_MODEL))
        dh2_o[...] = dh2
        _accumulate(dgn_o, dgn, step)

    return _rowwise("fwd_b2", body, [h1, act, tgt], [w["w_down"], w["g_n"]], [_sds((s, D_MODEL), F32)],
                    [_sds((1, LANES), F32), _sds((1, D_MODEL), F32)], ROW_TILE)


def _bwd_b(dh2, gate, up, h1, o_mla, o_sb, w):
    s = h1.shape[0]

    def body(r, c, o, a, step):
        dh2_ref, gate_ref, up_ref, h1_ref, oa_ref, ob_ref = r
        wd, wgt, wut, gf, woa, wob, ga, gb = c
        dgate_o, dup_o, dh1_o, doa_o, dob_o = o
        dgf_o, dga_o, dgb_o = a
        dh2 = dh2_ref[...]
        dact = _dot_nt(dh2.astype(BF16), wd[...])
        gate = gate_ref[...].astype(F32)
        sig = 1.0 / (1.0 + jnp.exp(-gate))
        dup = (dact * (gate * sig)).astype(BF16)
        dgate = (dact * up_ref[...].astype(F32) * (sig * (1.0 + gate * (1.0 - sig)))).astype(BF16)
        dup_o[...] = dup
        dgate_o[...] = dgate
        df = _dot(dgate, wgt[...]) + _dot(dup, wut[...])
        nf, rf = _rms(h1_ref[...])
        dres, dgf = _rms_bwd(nf, rf, gf[...], df)
        dh1 = dh2 + dres
        dh1_o[...] = dh1
        dh1b = dh1.astype(BF16)
        na, ra = _rms(oa_ref[...])
        doa, dga = _rms_bwd(na, ra, ga[...], _dot_nt(dh1b, woa[...]))
        nb, rb = _rms(ob_ref[...])
        dob, dgb = _rms_bwd(nb, rb, gb[...], _dot_nt(dh1b, wob[...]))
        doa_o[...] = doa
        dob_o[...] = dob
        _accumulate(dgf_o, dgf, step)
        _accumulate(dga_o, dga, step)
        _accumulate(dgb_o, dgb, step)

    consts = [w["w_down"], w["w_gate_t"], w["w_up_t"], w["g_f"], w["w_oa"], w["w_ob"], w["g_a"], w["g_b"]]
    outs = [_sds((s, D_FF), BF16), _sds((s, D_FF), BF16), _sds((s, D_MODEL), F32), _sds((s, MLA_WIDTH), F32), _sds((s, SB_WIDTH), F32)]
    accs = [_sds((1, D_MODEL), F32), _sds((1, MLA_WIDTH), F32), _sds((1, SB_WIDTH), F32)]
    return _rowwise("bwd_b", body, [dh2, gate, up, h1, o_mla, o_sb], consts, outs, accs, ROW_TILE_ELEMENTWISE)


def _fold_pairs(t):
    return jnp.concatenate([t[:, :LANES] + t[:, LANES:2 * LANES], t[:, 2 * LANES:3 * LANES] + t[:, 3 * LANES:]], axis=1)


def _bwd_a(x, dh1, cq, ckv, dqn, dqr, dkn, dvm, dkr, dsq, dsk, dsv, tabs, w):
    s = x.shape[0]

    def body(r, c, o, a, step):
        x_ref, dh1_ref, cq_ref, ckv_ref, dqn_ref, dqr_ref, dkn_ref, dvm_ref, dkr_ref, dsq_ref, dsk_ref, dsv_ref, cos_ref, sin_ref = r
        wqn, wqr, wqrr, gq, wkn, wv, gkv, wcq, wckv, wkr, wkrr, wsq, wsk, wsv, gmix = c
        dx_o, a1_o, a2_o, dcq_o, dckv_o, dkrc_o, dkrs_o = o
        dgq_o, dgkv_o, dgmix_o = a
        cos, sin = cos_ref[...], sin_ref[...]
        dqr = _fold_pairs(dqr_ref[...].astype(F32))
        a1 = (dqr * cos).astype(BF16)
        a2 = (dqr * sin).astype(BF16)
        a1_o[...] = a1
        a2_o[...] = a2
        nq, rq = _rms(cq_ref[...])
        dcqn = _dot_nt(dqn_ref[...], wqn[...]) + _dot_nt(a1, wqr[...]) + _dot_nt(a2, wqrr[...])
        dcq, dgq = _rms_bwd(nq, rq, gq[...], dcqn)
        nkv, rkv = _rms(ckv_ref[...])
        dckvn = _dot_nt(dkn_ref[...], wkn[...]) + _dot_nt(dvm_ref[...], wv[...])
        dckv, dgkv = _rms_bwd(nkv, rkv, gkv[...], dckvn)
        dkr = _fold_pairs(dkr_ref[...].astype(F32))
        dcq_b = dcq.astype(BF16)
        dckv_b = dckv.astype(BF16)
        dkrc = (dkr * cos).astype(BF16)
        dkrs = (dkr * sin).astype(BF16)
        dcq_o[...] = dcq_b
        dckv_o[...] = dckv_b
        dkrc_o[...] = dkrc
        dkrs_o[...] = dkrs
        du = (_dot_nt(dcq_b, wcq[...]) + _dot_nt(dckv_b, wckv[...]) + _dot_nt(dkrc, wkr[...]) + _dot_nt(dkrs, wkrr[...])
              + _dot_nt(dsq_ref[...], wsq[...]) + _dot_nt(dsk_ref[...], wsk[...]) + _dot_nt(dsv_ref[...], wsv[...]))
        nx, rx = _rms(x_ref[...])
        dres, dgmix = _rms_bwd(nx, rx, gmix[...], du)
        dx_o[...] = dh1_ref[...] + dres
        _accumulate(dgq_o, dgq, step)
        _accumulate(dgkv_o, dgkv, step)
        _accumulate(dgmix_o, dgmix, step)

    consts = [w["w_qn"], w["w_qr"], w["w_qrr"], w["g_q"], w["w_kn"], w["w_v"], w["g_kv"], w["w_cq"], w["w_ckv"],
              w["w_kr8"], w["w_kr8r"], w["w_sbq"], w["w_sbk"], w["w_sbv"], w["g_mix"]]
    rope_w = MLA_HEADS * MLA_ROPE
    outs = [_sds((s, D_MODEL), F32), _sds((s, rope_w), BF16), _sds((s, rope_w), BF16), _sds((s, Q_RANK), BF16),
            _sds((s, KV_RANK), BF16), _sds((s, rope_w), BF16), _sds((s, rope_w), BF16)]
    accs = [_sds((1, Q_RANK), F32), _sds((1, KV_RANK), F32), _sds((1, D_MODEL), F32)]
    rows = [x, dh1, cq, ckv, dqn, dqr, dkn, dvm, dkr, dsq, dsk, dsv, tabs["cos"], tabs["sin"]]
    return _rowwise("bwd_a", body, rows, consts, outs, accs, ROW_TILE)


def _tn_multi(name, x, ys):
    s, k = x.shape
    ts = min(TN_TS, s)
    n_y = len(ys)

    def kern(*refs):
        step = pl.program_id(0)
        xb = refs[0][...].astype(BF16)
        for j in range(n_y):
            _accumulate(refs[1 + n_y + j], _dot_tn(xb, refs[1 + j][...].astype(BF16)), step)

    return pl.pallas_call(
        kern, name=name, grid=(s // ts,),
        in_specs=[pl.BlockSpec((ts, k), lambda i: (i, 0))] + [pl.BlockSpec((ts, y.shape[1]), lambda i: (i, 0)) for y in ys],
        out_specs=[pl.BlockSpec((k, y.shape[1]), lambda i: (0, 0)) for y in ys],
        out_shape=[_sds((k, y.shape[1]), F32) for y in ys], compiler_params=_params(1),
    )(x, *ys)


def _tn_tile(k, n):
    if n % LANES or k * n * 4 <= TN_ACC_BYTES:
        return n
    units = n // LANES
    best = 1
    for d in range(1, units + 1):
        if units % d == 0 and k * d * LANES * 4 <= TN_ACC_BYTES:
            best = d
    return best * LANES


def _tn_matmul(name, x, y):
    s, k = x.shape
    n = y.shape[1]
    ts = min(TN_TS, s)
    tn = _tn_tile(k, n)

    def kern(x_ref, y_ref, o_ref):
        step = pl.program_id(1)
        _accumulate(o_ref, _dot_tn(x_ref[...].astype(BF16), y_ref[...].astype(BF16)), step)

    return pl.pallas_call(
        kern, name=name, grid=(n // tn, s // ts),
        in_specs=[pl.BlockSpec((ts, k), lambda j, i: (i, 0)), pl.BlockSpec((ts, tn), lambda j, i: (i, j))],
        out_specs=pl.BlockSpec((k, tn), lambda j, i: (0, j)), out_shape=_sds((k, n), F32), compiler_params=_params(2),
    )(x, y)


def _lanes(rows, lo, width):
    lane = lax.broadcasted_iota(jnp.int32, (rows, LANES), 1)
    return jnp.logical_and(lane >= lo, lane < lo + width)


def _keep(mask, t):
    return jnp.where(mask, t, jnp.zeros_like(t))


def _mla_qcat(qn_ref, qr_ref, rope_lo, half, rows):
    qn = _keep(_lanes(rows, MLA_NOPE * half, MLA_NOPE), qn_ref[...])
    qr = _keep(_lanes(rows, rope_lo, MLA_ROPE), qr_ref[...])
    return jnp.concatenate([qn, qr], axis=1)


def _diag_mask(rows, width, row0, col0):
    row = lax.broadcasted_iota(jnp.int32, (rows, width), 0)
    col = lax.broadcasted_iota(jnp.int32, (rows, width), 1)
    return col + (col0 - row0) <= row


def _mla_fwd(qn, qr, kn, kr, v, riders=()):
    s = qn.shape[0]
    tq, tk = min(MLA_TQ, s), min(MLA_TK, s)
    td = tq
    ratio = tq // tk

    n_ride = len(riders)
    n_pairs = MLA_HEADS // 2

    def kern(qn_ref, qr_ref, kn_ref, kr_ref, v_ref, *rest):
        o_ref, lse_ref = rest[n_ride:n_ride + 2]
        g = pl.program_id(0)
        i = pl.program_id(1)
        if n_ride:
            send, forward, finish = _gather_steps(rest[:n_ride], rest[n_ride + 2:2 * n_ride + 2], *rest[2 * n_ride + 2:])
            pl.when(jnp.logical_and(g == 0, i == 0))(send)
            pl.when(jnp.logical_and(g == 1, i == 0))(forward)
        qcat = [_mla_qcat(qn_ref, qr_ref, MLA_ROPE * (2 * (g % 2) + half), half, tq) for half in range(2)]

        def block(k0, width, carry, row0, masked, half):
            m, l, acc = (c[row0:] for c in carry)
            ks = pl.ds(pl.multiple_of(k0, width), width)
            kcat = jnp.concatenate([kn_ref[ks, :], kr_ref[ks, :]], axis=1)
            sc = _dot_nt(qcat[half][row0:], kcat)
            if masked:
                sc = jnp.where(_diag_mask(tq - row0, width, row0, row0), sc, NEG)
            m_new = jnp.maximum(m, jnp.max(sc, axis=1, keepdims=True))
            p = jnp.exp2(sc - m_new)
            alpha = jnp.exp2(m - m_new)
            l = alpha * l + jnp.sum(p, axis=1, keepdims=True)
            acc = alpha * acc + _dot(p.astype(BF16), v_ref[ks, :])
            new = (m_new, l, acc)
            return new if row0 == 0 else tuple(jnp.concatenate([c[:row0], n], axis=0) for c, n in zip(carry, new))

        def both(k0, width, carries, row0, masked):
            return tuple(block(k0, width, carries[half], row0, masked, half) for half in range(2))

        init = (jnp.full((tq, 1), NEG, F32), jnp.zeros((tq, 1), F32), jnp.zeros((tq, LANES), F32))
        carries = lax.fori_loop(0, i * ratio, lambda kb, c: both(kb * tk, tk, c, 0, False), (init, init))
        for row0 in range(0, tq, td):
            carries = both(i * tq + row0, td, carries, row0, True)
        for half in range(2):
            m, l, acc = carries[half]
            out = _keep(_lanes(tq, MLA_V * half, MLA_V), acc / l)
            lse = _keep(_lanes(tq, MLA_ROPE * half, MLA_ROPE), jnp.broadcast_to(m + jnp.log2(l), (tq, LANES)))
            if half == 0:
                o_ref[...] = out
                lse_ref[...] = lse
            else:
                o_ref[...] += out
                lse_ref[...] += lse
        if n_ride:
            pl.when(jnp.logical_and(g == n_pairs - 1, i == s // tq - 1))(finish)

    qblk = pl.BlockSpec((tq, LANES), lambda g, i: (i, g))
    full = pl.BlockSpec((s, LANES), lambda g, i: (0, g))
    outs = pl.pallas_call(
        kern, name="mla_fwd", grid=(n_pairs, s // tq),
        in_specs=[qblk, pl.BlockSpec((tq, LANES), lambda g, i: (i, g // 2)), full, pl.BlockSpec((s, LANES), lambda g, i: (0, 0)), full]
        + [HBM_SPEC] * n_ride,
        out_specs=[qblk, qblk] + [HBM_SPEC] * n_ride,
        out_shape=[_sds((s, MLA_WIDTH), F32), _sds((s, n_pairs * LANES), F32)] + [_sds((N_SHARD,) + a.shape, a.dtype) for a in riders],
        scratch_shapes=_gather_sems(n_ride) if n_ride else [], compiler_params=_params(2),
    )(qn, qr, kn, kr, v, *riders)
    return outs[0], outs[1], outs[2:]


def _mla_bwd(qn, qr, kn, kr, v, o, do, lse, riders=()):
    s = qn.shape[0]
    tq, tk, td = min(MLA_TQ, s), min(MLA_BWD_TK, s), min(MLA_DIAG_TK, s)
    ratio = tq // tk

    n_ride = len(riders)
    n_pairs = MLA_HEADS // 2

    def kern(qn_ref, qr_ref, kn_ref, kr_ref, v_ref, o_ref, do_ref, lse_ref, *rest):
        dqn_ref, dqr_ref, dkn_out, dkr_out, dv_out = rest[n_ride:n_ride + 5]
        dkn_ref, dkr_ref, dv_ref = rest[2 * n_ride + 5:2 * n_ride + 8]
        g = pl.program_id(0)
        i = pl.program_id(1)
        if n_ride:
            start, finish = _scatter_steps(rest[:n_ride], rest[n_ride + 5:2 * n_ride + 5], *rest[2 * n_ride + 8:])
            pl.when(jnp.logical_and(g == 0, i == 0))(start)

        @pl.when(i == 0)
        def _():
            dkn_ref[...] = jnp.zeros_like(dkn_ref)
            dkr_ref[...] = jnp.zeros_like(dkr_ref)
            dv_ref[...] = jnp.zeros_like(dv_ref)

        for half in range(2):
            rope_lo = MLA_ROPE * (2 * (g % 2) + half)
            qcat = _mla_qcat(qn_ref, qr_ref, rope_lo, half, tq)
            mine = _lanes(tq, MLA_V * half, MLA_V)
            do_f = _keep(mine, do_ref[...])
            do_b = do_f.astype(BF16)
            delta = jnp.sum(do_f * o_ref[...], axis=1, keepdims=True)
            lse_v = lse_ref[:, MLA_ROPE * half:MLA_ROPE * half + 1]

            def block(k0, width, dq_acc, row0, masked, qcat=qcat, do_b=do_b, delta=delta, lse_v=lse_v):
                ks = pl.ds(pl.multiple_of(k0, width), width)
                kcat = jnp.concatenate([kn_ref[ks, :], kr_ref[ks, :]], axis=1)
                qc, dob = qcat[row0:], do_b[row0:]
                p = jnp.exp2(_dot_nt(qc, kcat) - lse_v[row0:])
                if masked:
                    p = jnp.where(_diag_mask(tq - row0, width, row0, row0), p, 0.0)
                ds = (p * (_dot_nt(dob, v_ref[ks, :]) - delta[row0:])).astype(BF16)
                dv_ref[ks, :] += _dot_tn(p.astype(BF16), dob)
                dkc = _dot_tn(ds, qc)
                dkn_ref[ks, :] += dkc[:, :LANES]
                dkr_ref[ks, :] += dkc[:, LANES:]
                new = dq_acc[row0:] + _dot(ds, kcat)
                return new if row0 == 0 else jnp.concatenate([dq_acc[:row0], new], axis=0)

            acc = lax.fori_loop(0, i * ratio, lambda kb, c, block=block: block(kb * tk, tk, c, 0, False),
                                jnp.zeros((tq, 2 * LANES), F32))
            for row0 in range(0, tq, td):
                acc = block(i * tq + row0, td, acc, row0, True)
            dqn = _keep(_lanes(tq, MLA_NOPE * half, MLA_NOPE), acc[:, :LANES] * MLA_SCALE)
            dqr = _keep(_lanes(tq, rope_lo, MLA_ROPE), acc[:, LANES:] * MLA_SCALE)
            if half == 0:
                dqn_ref[...] = dqn.astype(BF16)
                dqr_ref[...] = dqr.astype(BF16)
            else:
                dqn_ref[...] += dqn.astype(BF16)
                dqr_ref[...] += dqr.astype(BF16)

        @pl.when(i == s // tq - 1)
        def _():
            dkn_out[...] = (dkn_ref[...] * MLA_DK_SCALE).astype(BF16)
            dkr_out[...] = (dkr_ref[...] * MLA_DK_SCALE).astype(BF16)
            dv_out[...] = dv_ref[...].astype(BF16)

        if n_ride:
            pl.when(jnp.logical_and(g == n_pairs - 1, i == s // tq - 1))(finish)

    qblk = pl.BlockSpec((tq, LANES), lambda g, i: (i, g))
    full = pl.BlockSpec((s, LANES), lambda g, i: (0, g))
    once = lambda spec_map: pl.BlockSpec((s, LANES), spec_map, pipeline_mode=pl.Buffered(1))
    wide = _sds((s, n_pairs * LANES), BF16)
    outs = pl.pallas_call(
        kern, name="mla_bwd", grid=(n_pairs, s // tq),
        in_specs=[qblk, pl.BlockSpec((tq, LANES), lambda g, i: (i, g // 2)), once(lambda g, i: (0, g)), once(lambda g, i: (0, 0)),
                  once(lambda g, i: (0, g)), qblk, qblk, qblk] + [HBM_SPEC] * n_ride,
        out_specs=[qblk, qblk, full, full, full] + [HBM_SPEC] * n_ride,
        out_shape=[wide] * 5 + [_sds((N_SHARD - 1,) + p.shape[1:], p.dtype) for p in riders],
        scratch_shapes=[pltpu.VMEM((s, LANES), F32)] * 3 + (_scatter_sems(n_ride) if n_ride else []), compiler_params=_params(2),
    )(qn, qr, kn, kr, v, o, do, lse, *riders)
    return outs[:5], outs[5:]


def _sb_masks(tk):
    j = lax.broadcasted_iota(jnp.int32, (tk, tk), 0)
    c = lax.broadcasted_iota(jnp.int32, (tk, tk), 1)
    return (j > c).astype(BF16), (j < c).astype(BF16)


def _sb_scores(qs, kk, msuf, strict):
    z = _dot_nt(qs, kk)
    lom = -(jnp.maximum(z, 0.0) + jnp.log(1.0 + jnp.exp(-jnp.abs(z))))
    if strict is not None:
        lom = jnp.where(strict, lom, 0.0)
    return z, lom, _dot(lom.astype(BF16), msuf)


def _sb_strict(tq, tk, d):
    row = lax.broadcasted_iota(jnp.int32, (tq, tk), 0)
    col = lax.broadcasted_iota(jnp.int32, (tq, tk), 1)
    return col + d * tk < row


def _sb_fwd(q, k, v, msuf):
    s = q.shape[0]
    tq, tk = min(SB_TQ, s), min(SB_TK, s)
    ratio = tq // tk

    def kern(q_ref, k_ref, v_ref, m_ref, o_ref, c_ref):
        i = pl.program_id(1)
        msf = m_ref[...]
        lane = lax.broadcasted_iota(jnp.int32, (tq, LANES), 1)
        mine = [_lanes(tq, SB_DIM * half, SB_DIM) for half in range(2)]
        qs = [_keep(m, q_ref[...]) * 0.125 for m in mine]

        def block(kb, carry, dd, half):
            c, acc, cm = carry
            ks = pl.ds(pl.multiple_of(kb * tk, tk), tk)
            strict = None if dd is None else _sb_strict(tq, tk, dd)
            z, lom, suf = _sb_scores(qs[half], k_ref[ks, :], msf, strict)
            a = jnp.exp(z + lom + (suf + c))
            if strict is not None:
                a = jnp.where(strict, a, 0.0)
            acc = acc + _dot(a.astype(BF16), v_ref[ks, :])
            cm = jnp.where(lane == kb, c, cm)
            return c + jnp.sum(lom, axis=1, keepdims=True), acc, cm

        init = (jnp.zeros((tq, 1), F32), jnp.zeros((tq, LANES), F32), jnp.full((tq, LANES), NEG, F32))
        carries = [init, init]
        for dd in range(ratio - 1, -1, -1):
            carries = [block(i * ratio + dd, carries[half], dd, half) for half in range(2)]

        def live(st):
            return jnp.logical_and(st[0] >= 0, jnp.maximum(jnp.max(st[1][0]), jnp.max(st[2][0])) > -SB_SKIP)

        def step(st):
            return (st[0] - 1, block(st[0], st[1], None, 0), block(st[0], st[2], None, 1))

        _, done0, done1 = lax.while_loop(live, step, (i * ratio - 1, carries[0], carries[1]))
        o_ref[...] = _keep(mine[0], done0[1]) + _keep(mine[1], done1[1])
        c_ref[:, :LANES] = done0[2]
        c_ref[:, LANES:] = done1[2]

    qblk = lambda n: pl.BlockSpec((tq, n), lambda g, i: (i, g))
    full = pl.BlockSpec((s, LANES), lambda g, i: (0, g))
    return pl.pallas_call(
        kern, name="sb_fwd", grid=(SB_HEADS // 2, s // tq),
        in_specs=[qblk(LANES), full, full, pl.BlockSpec((tk, tk), lambda g, i: (0, 0))],
        out_specs=[qblk(LANES), qblk(2 * LANES)],
        out_shape=[_sds((s, SB_WIDTH), F32), _sds((s, SB_HEADS * LANES), F32)], compiler_params=_params(2),
    )(q, k, v, msuf)


def _sb_bwd(q, k, v, do, cmat, msuf, mpre, riders=()):
    s = q.shape[0]
    tq, tk = min(SB_TQ, s), min(SB_TK, s)
    ratio = tq // tk

    n_ride = len(riders)
    n_pairs = SB_HEADS // 2

    def kern(q_ref, k_ref, v_ref, do_ref, c_ref, ms_ref, mp_ref, *rest):
        dq_ref, dk_out, dv_out = rest[n_ride:n_ride + 3]
        dk_ref, dv_ref = rest[2 * n_ride + 3:2 * n_ride + 5]
        i = pl.program_id(1)
        if n_ride:
            start, finish = _swap_steps(rest[:n_ride], rest[n_ride + 3:2 * n_ride + 3], *rest[2 * n_ride + 5:])
            pl.when(jnp.logical_and(pl.program_id(0) == 0, i == 0))(start)

        @pl.when(i == 0)
        def _():
            dk_ref[...] = jnp.zeros_like(dk_ref)
            dv_ref[...] = jnp.zeros_like(dv_ref)

        msf = ms_ref[...]
        mpf = mp_ref[...]
        lane = lax.broadcasted_iota(jnp.int32, (tq, LANES), 1)
        lane1 = lax.broadcasted_iota(jnp.int32, (1, LANES), 1)
        mine = [_lanes(tq, SB_DIM * half, SB_DIM) for half in range(2)]
        qv = [_keep(m, q_ref[...]) for m in mine]
        qs = [t * 0.125 for t in qv]
        do_b = [_keep(m, do_ref[...]).astype(BF16) for m in mine]
        cm = [c_ref[:, :LANES], c_ref[:, LANES:]]

        def block(kb, carry, dd, half):
            dq_acc, pc = carry
            ks = pl.ds(pl.multiple_of(kb * tk, tk), tk)
            kk = k_ref[ks, :]
            strict = None if dd is None else _sb_strict(tq, tk, dd)
            z, lom, suf = _sb_scores(qs[half], kk, msf, strict)
            c = jnp.sum(jnp.where(lane == kb, cm[half], 0.0), axis=1, keepdims=True)
            a = jnp.exp(z + lom + (suf + c))
            if strict is not None:
                a = jnp.where(strict, a, 0.0)
            g = _dot_nt(do_b[half], v_ref[ks, :]) * a
            p = pc + _dot(g.astype(BF16), mpf)
            omb = jnp.exp(lom)
            dz = (g * omb - (1.0 - omb) * p) * 0.125
            if strict is not None:
                dz = jnp.where(strict, dz, 0.0)
            dz = dz.astype(BF16)
            dv_ref[ks, :] += _dot_tn(a.astype(BF16), do_b[half])
            dk_ref[ks, :] += _dot_tn(dz, qv[half])
            return dq_acc + _dot(dz, kk), pc + jnp.sum(g, axis=1, keepdims=True)

        def needed(cm_h):
            seen = jnp.logical_and(jnp.max(cm_h, axis=0, keepdims=True) > -SB_SKIP, lane1 < i * ratio)
            return jnp.sum(seen.astype(jnp.int32))

        first = i * ratio - jnp.maximum(needed(cm[0]), needed(cm[1]))
        init = (jnp.zeros((tq, LANES), F32), jnp.zeros((tq, 1), F32))
        carries = lax.fori_loop(first, i * ratio, lambda kb, c: (block(kb, c[0], None, 0), block(kb, c[1], None, 1)), (init, init))
        for dd in range(ratio):
            carries = [block(i * ratio + dd, carries[half], dd, half) for half in range(2)]
        dq_ref[...] = (_keep(mine[0], carries[0][0]) + _keep(mine[1], carries[1][0])).astype(BF16)

        @pl.when(i == s // tq - 1)
        def _():
            dk_out[...] = dk_ref[...].astype(BF16)
            dv_out[...] = dv_ref[...].astype(BF16)

        if n_ride:
            pl.when(jnp.logical_and(pl.program_id(0) == n_pairs - 1, i == s // tq - 1))(finish)

    qblk = lambda n: pl.BlockSpec((tq, n), lambda g, i: (i, g))
    full = pl.BlockSpec((s, LANES), lambda g, i: (0, g))
    msk = pl.BlockSpec((tk, tk), lambda g, i: (0, 0))
    outs = pl.pallas_call(
        kern, name="sb_bwd", grid=(n_pairs, s // tq),
        in_specs=[qblk(LANES), full, full, qblk(LANES), qblk(2 * LANES), msk, msk] + [HBM_SPEC] * n_ride,
        out_specs=[qblk(LANES), full, full] + [HBM_SPEC] * n_ride,
        out_shape=[_sds((s, SB_WIDTH), BF16)] * 3 + _halves_shapes(riders),
        scratch_shapes=[pltpu.VMEM((s, LANES), F32)] * 2 + (_swap_sems(n_ride) if n_ride else []), compiler_params=_params(2),
    )(q, k, v, do, cmat, msuf, mpre, *riders)
    return outs[:3], outs[3:]


def _place():
    return lax.axis_index("x"), lax.axis_index("y"), lax.axis_index("c")


def _other_chips(x, y):
    return [(1 - x, y), (x, 1 - y), (1 - x, 1 - y)]


HBM_SPEC = pl.BlockSpec(memory_space=pl.ANY)


def _gather_steps(ins, outs, send_sems, recv_sems):
    n = len(ins)
    x, y, c = _place()
    sibling = (x, y, 1 - c)
    chips = _other_chips(x, y)

    def half_of(a, ref, pc):
        half = ins[a].shape[0] // 2
        return ref.at[pl.ds(pl.multiple_of(pc * half, 16), half), :]

    def copy(a, k, chip, pc, to, src=None):
        dst = half_of(a, outs[a].at[2 * chip[0] + chip[1]], pc)
        return pltpu.make_async_remote_copy(src_ref=dst if src is None else src, dst_ref=dst, send_sem=send_sems.at[7 * a + k],
                                            recv_sem=recv_sems.at[7 * a + k], device_id=to, device_id_type=MESH)

    def own(a):
        return pltpu.make_async_remote_copy(src_ref=ins[a], dst_ref=outs[a].at[2 * x + y], send_sem=send_sems.at[7 * a + 6],
                                            recv_sem=recv_sems.at[7 * a + 6], device_id=sibling, device_id_type=MESH)

    def first():
        far = [copy(a, j, (x, y), c, (*chip, c), src=half_of(a, ins[a], c)) for a in range(n) for j, chip in enumerate(chips)]
        return far + [own(a) for a in range(n)]

    def passed():
        return [copy(a, 3 + j, chip, c, sibling) for j, chip in enumerate(chips) for a in range(n)]

    def send():
        for cp in first():
            cp.start()

    def forward():
        for j, chip in enumerate(chips):
            for a in range(n):
                copy(a, j, chip, c, sibling).wait_recv()
        for cp in passed():
            cp.start()

    def finish():
        for j, chip in enumerate(chips):
            for a in range(n):
                copy(a, 3 + j, chip, 1 - c, sibling).wait_recv()
        for a in range(n):
            own(a).wait_recv()
        for cp in first() + passed():
            cp.wait_send()

    return send, forward, finish


def _gather_sems(n):
    return [pltpu.SemaphoreType.DMA((7 * n,)), pltpu.SemaphoreType.DMA((7 * n,))]


def _allgather_list(name, shards):
    n = len(shards)

    def body(*refs):
        for stage in _gather_steps(refs[:n], refs[n:2 * n], *refs[2 * n:]):
            stage()

    return pl.pallas_call(
        body, name=name, out_shape=[_sds((N_SHARD,) + a.shape, a.dtype) for a in shards], in_specs=[HBM_SPEC] * n,
        out_specs=[HBM_SPEC] * n, scratch_shapes=_gather_sems(n),
    )(*shards)


def _swap_steps(ins, outs, send_sems, recv_sems):
    x, y, c = _place()

    def copies():
        out = []
        for a in range(len(ins)):
            h = ins[a].shape[1] // 2
            src = ins[a].at[:, pl.ds(pl.multiple_of((1 - c) * h, 8), h), :]
            out.append(pltpu.make_async_remote_copy(src_ref=src, dst_ref=outs[a], send_sem=send_sems.at[a], recv_sem=recv_sems.at[a],
                                                    device_id=(x, y, 1 - c), device_id_type=MESH))
        return out

    def start():
        for cp in copies():
            cp.start()

    def finish():
        for cp in copies():
            cp.wait()

    return start, finish


def _swap_sems(n):
    return [pltpu.SemaphoreType.DMA((n,)), pltpu.SemaphoreType.DMA((n,))]


def _halves_shapes(gs):
    return [_sds((N_SHARD, g.shape[1] // 2, g.shape[2]), g.dtype) for g in gs]


def _swap_halves(name, gs):
    n = len(gs)

    def body(*refs):
        for stage in _swap_steps(refs[:n], refs[n:2 * n], *refs[2 * n:]):
            stage()

    return pl.pallas_call(body, name=name, out_shape=_halves_shapes(gs), in_specs=[HBM_SPEC] * n, out_specs=[HBM_SPEC] * n,
                          scratch_shapes=_swap_sems(n))(*gs)


def _add_sibling(name, gs, gots, c_idx):
    n = len(gs)

    def kern(c_ref, *refs):
        for a in range(n):
            tot = refs[a][...] + refs[n + a][...]
            refs[2 * n + a][...] = tot
            refs[3 * n + a][...] = tot.astype(BF16)

    quarter = lambda g: (None, g.shape[1] // 4, g.shape[2])
    in_specs = [pl.BlockSpec(quarter(g), lambda b, s, c_ref: (b, 2 * c_ref[0] + s, 0)) for g in gs]
    in_specs += [pl.BlockSpec(quarter(g), lambda b, s, c_ref: (b, s, 0)) for g in gs]
    out_specs = [pl.BlockSpec(quarter(g), lambda b, s, c_ref: (b, s, 0)) for g in gs] * 2
    out_shape = [_sds(t.shape, F32) for t in gots] + [_sds(t.shape, BF16) for t in gots]
    outs = pl.pallas_call(
        kern, name=name, out_shape=out_shape,
        grid_spec=pltpu.PrefetchScalarGridSpec(num_scalar_prefetch=1, grid=(N_SHARD, 2), in_specs=in_specs, out_specs=out_specs),
        compiler_params=_params(2),
    )(c_idx.reshape(1), *gs, *gots)
    return outs[:n], outs[n:]


def _scatter_steps(ins, outs, send_sems, recv_sems):
    x, y, c = _place()

    def copies():
        return [pltpu.make_async_remote_copy(
            src_ref=ins[a].at[2 * px + py], dst_ref=outs[a].at[j], send_sem=send_sems.at[3 * a + j], recv_sem=recv_sems.at[3 * a + j],
            device_id=(px, py, c), device_id_type=MESH) for a in range(len(ins)) for j, (px, py) in enumerate(_other_chips(x, y))]

    def start():
        for cp in copies():
            cp.start()

    def finish():
        for cp in copies():
            cp.wait()

    return start, finish


def _scatter_sems(n):
    return [pltpu.SemaphoreType.DMA((3 * n,)), pltpu.SemaphoreType.DMA((3 * n,))]


def _chip_scatter(ps):
    n = len(ps)

    def body(*refs):
        for stage in _scatter_steps(refs[:n], refs[n:2 * n], *refs[2 * n:]):
            stage()

    return pl.pallas_call(
        body, name="chip_scatter", out_shape=[_sds((N_SHARD - 1,) + p.shape[1:], p.dtype) for p in ps], in_specs=[HBM_SPEC] * n,
        out_specs=[HBM_SPEC] * n, scratch_shapes=_scatter_sems(n),
    )(*ps)


def _add_chips(name, ps, others, shard_idx):
    n = len(ps)

    def kern(b_ref, *refs):
        for a in range(n):
            tot = refs[a][...]
            for j in range(N_SHARD - 1):
                tot = tot + refs[n + a][j].astype(F32)
            refs[2 * n + a][...] = tot

    in_specs = [pl.BlockSpec((None, p.shape[1] // 2, p.shape[2]), lambda s, b_ref: (b_ref[0], s, 0)) for p in ps]
    in_specs += [pl.BlockSpec((N_SHARD - 1, p.shape[1] // 2, p.shape[2]), lambda s, b_ref: (0, s, 0)) for p in ps]
    out_specs = [pl.BlockSpec((p.shape[1] // 2, p.shape[2]), lambda s, b_ref: (s, 0)) for p in ps]
    return pl.pallas_call(
        kern, name=name, out_shape=[_sds(p.shape[1:], F32) for p in ps],
        grid_spec=pltpu.PrefetchScalarGridSpec(num_scalar_prefetch=1, grid=(2,), in_specs=in_specs, out_specs=out_specs),
        compiler_params=_params(1),
    )(shard_idx.reshape(1), *ps, *others)


def _swap_result(name, mines):
    n = len(mines)

    def body(*refs):
        ins, outs = refs[:n], refs[n:2 * n]
        send_sems, recv_sems = refs[2 * n:]
        x, y, c = _place()
        copies = [pltpu.make_async_remote_copy(src_ref=ins[a], dst_ref=outs[a], send_sem=send_sems.at[a], recv_sem=recv_sems.at[a],
                                               device_id=(x, y, 1 - c), device_id_type=MESH) for a in range(n)]
        for cp in copies:
            cp.start()
        for cp in copies:
            cp.wait()

    return pl.pallas_call(
        body, name=name, out_shape=[_sds(m.shape, m.dtype) for m in mines], in_specs=[HBM_SPEC] * n,
        out_specs=[HBM_SPEC] * n, scratch_shapes=[pltpu.SemaphoreType.DMA((n,)), pltpu.SemaphoreType.DMA((n,))],
    )(*mines)


def _allreduce_small(v):
    m_per, n = v.shape

    def body(x_ref, tot_ref, all_ref, send_sems, recv_sems, local_sem):
        x, y, c = _place()
        me, sibling = (x, y, c), (x, y, 1 - c)
        chips = _other_chips(x, y)

        def rows(px, py, pc):
            return all_ref.at[pl.ds(pl.multiple_of((4 * px + 2 * py + pc) * m_per, 8), m_per), :]

        def copy(k, block, to, src=None):
            return pltpu.make_async_remote_copy(
                src_ref=rows(*block) if src is None else src, dst_ref=rows(*block), send_sem=send_sems.at[k],
                recv_sem=recv_sems.at[k], device_id=to, device_id_type=MESH)

        mine = pltpu.make_async_copy(x_ref, rows(*me), local_sem)
        mine.start()
        first = [copy(0, me, sibling, src=x_ref)] + [copy(1 + j, me, (*chip, c), src=x_ref) for j, chip in enumerate(chips)]
        for cp in first:
            cp.start()
        passed = [copy(4 + j, (*chip, c), sibling) for j, chip in enumerate(chips)]
        for j, chip in enumerate(chips):
            copy(1 + j, (*chip, c), me).wait_recv()
            passed[j].start()
        copy(0, sibling, me).wait_recv()
        for j, chip in enumerate(chips):
            copy(4 + j, (*chip, 1 - c), me).wait_recv()
        for cp in first + passed:
            cp.wait_send()
        mine.wait()
        tot = all_ref[0:m_per, :]
        for dev in range(1, 8):
            tot = tot + all_ref[dev * m_per:(dev + 1) * m_per, :]
        tot_ref[...] = tot

    vmem = pl.BlockSpec(memory_space=pltpu.VMEM)
    return pl.pallas_call(
        body, name="allreduce_small", out_shape=_sds((m_per, n), F32), in_specs=[vmem], out_specs=vmem,
        scratch_shapes=[pltpu.VMEM((8 * m_per, n), F32), pltpu.SemaphoreType.DMA((7,)), pltpu.SemaphoreType.DMA((7,)),
                        pltpu.SemaphoreType.DMA],
    )(v)


def _adam_update(w, g, m, v):
    m_new = ADAM_B1 * m + (1.0 - ADAM_B1) * g
    v_new = ADAM_B2 * v + (1.0 - ADAM_B2) * (g * g)
    m_hat = m_new / (1.0 - ADAM_B1 ** ADAM_STEP)
    v_hat = v_new / (1.0 - ADAM_B2 ** ADAM_STEP)
    return -ADAM_LR * (m_hat / (jnp.sqrt(v_hat) + ADAM_EPS) + ADAM_WD * w), m_new, v_new


def _adamw(name, w, g, m, v):
    rows, width = w.shape
    tr = rows // 4 if rows % 32 == 0 else rows

    def kern(w_ref, g_ref, m_ref, v_ref, d_ref, mo_ref, vo_ref):
        d_ref[...], mo_ref[...], vo_ref[...] = _adam_update(w_ref[...], g_ref[...], m_ref[...], v_ref[...])

    spec = pl.BlockSpec((tr, width), lambda i: (i, 0))
    return pl.pallas_call(kern, name=name, grid=(rows // tr,), in_specs=[spec] * 4, out_specs=[spec] * 3,
                          out_shape=[_sds((rows, width), F32)] * 3, compiler_params=_params(1))(w, g, m, v)


def _adamw_halves(ws, mines, theirs, ms, vs, c_idx):
    n = len(ws)

    def kern(c_ref, *refs):
        take_mine = pl.program_id(0) == c_ref[0]
        for a in range(n):
            w_ref, mine_ref, theirs_ref, m_ref, v_ref = refs[5 * a:5 * a + 5]
            g_ref, d_ref, mo_ref, vo_ref = refs[5 * n + 4 * a:5 * n + 4 * a + 4]
            g = jnp.where(take_mine, mine_ref[...], theirs_ref[...])
            g_ref[...] = g
            d_ref[...], mo_ref[...], vo_ref[...] = _adam_update(w_ref[...], g, m_ref[...], v_ref[...])

    in_specs, out_specs, out_shape = [], [], []
    for w in ws:
        rows, width = w.shape
        tr = rows // (2 * ADAM_STEPS)
        whole = pl.BlockSpec((tr, width), lambda h, j, c_ref: (ADAM_STEPS * h + j, 0))
        part = pl.BlockSpec((tr, width), lambda h, j, c_ref: (j, 0))
        in_specs += [whole, part, part, whole, whole]
        out_specs += [whole] * 4
        out_shape += [_sds((rows, width), F32)] * 4
    operands = [t for group in zip(ws, mines, theirs, ms, vs) for t in group]
    outs = pl.pallas_call(
        kern, name="adamw_shards", out_shape=out_shape,
        grid_spec=pltpu.PrefetchScalarGridSpec(num_scalar_prefetch=1, grid=(2, ADAM_STEPS), in_specs=in_specs, out_specs=out_specs),
        compiler_params=_params(2),
    )(c_idx.reshape(1), *operands)
    return [outs[4 * a:4 * a + 4] for a in range(n)]


SHARDED = (("w_in", D_MODEL, IN_WIDTH, 1), ("w_uq", Q_RANK, MLA_HEADS * MLA_QK, 1),
           ("w_ukv", KV_RANK, MLA_HEADS * (MLA_NOPE + MLA_V), 1), ("w_o", D_MODEL, D_MODEL, 0),
           ("w_gate", D_MODEL, D_FF, 1), ("w_up", D_MODEL, D_FF, 1), ("w_down", D_FF, D_MODEL, 0))
EARLY = ("w_in", "w_uq", "w_ukv")
LATE = ("w_o", "w_gate", "w_up", "w_down")
FLIPPED = ("w_gate", "w_up")
SMALL = (("norm_mix", D_MODEL), ("q_latent_norm", Q_RANK), ("kv_latent_norm", KV_RANK), ("out_norm_mla", MLA_WIDTH),
         ("out_norm_sb", SB_WIDTH), ("norm_ffn", D_MODEL), ("norm_final", D_MODEL))


def _full_weight(gathered, axis):
    n_sh, k, n = gathered.shape
    return gathered.transpose(1, 0, 2).reshape(k, n_sh * n) if axis == 1 else gathered.reshape(n_sh * k, n)


def _shard_major(g, axis):
    r, c = g.shape
    return g.reshape(r, N_SHARD, c // N_SHARD).transpose(1, 0, 2) if axis == 1 else g.reshape(N_SHARD, r // N_SHARD, c)


def _rot_cols(w):
    hh = MLA_ROPE // 2
    return jnp.concatenate([-w[..., hh:], w[..., :hh]], axis=-1)


def _rot_cols_t(g):
    hh = MLA_ROPE // 2
    return jnp.concatenate([g[..., hh:], -g[..., :hh]], axis=-1)


def _attention_weights(full, small):
    w_in = full["w_in"]
    s0, s1, s2 = Q_RANK, Q_RANK + KV_RANK, Q_RANK + KV_RANK + MLA_ROPE
    uq = full["w_uq"].reshape(Q_RANK, MLA_HEADS, MLA_QK)
    ukv = full["w_ukv"].reshape(KV_RANK, MLA_HEADS, MLA_NOPE + MLA_V)
    w_kr = w_in[:, s1:s2]
    per_tile = ROPE_TILE // MLA_ROPE
    w = {
        "w_cq": w_in[:, :s0], "w_ckv": w_in[:, s0:s1],
        "w_kr4": jnp.tile(w_kr, (1, per_tile)), "w_kr4r": jnp.tile(_rot_cols(w_kr), (1, per_tile)),
        "w_kr8": jnp.tile(w_kr, (1, MLA_HEADS)), "w_kr8r": jnp.tile(_rot_cols(w_kr), (1, MLA_HEADS)),
        "w_sbq": w_in[:, s2:s2 + SB_WIDTH], "w_sbk": w_in[:, s2 + SB_WIDTH:s2 + 2 * SB_WIDTH], "w_sbv": w_in[:, s2 + 2 * SB_WIDTH:],
        "w_qn": uq[..., :MLA_NOPE].reshape(Q_RANK, -1), "w_qr": uq[..., MLA_NOPE:].reshape(Q_RANK, -1),
        "w_qrr": _rot_cols(uq[..., MLA_NOPE:]).reshape(Q_RANK, -1),
        "w_kn": ukv[..., :MLA_NOPE].reshape(KV_RANK, -1), "w_v": ukv[..., MLA_NOPE:].reshape(KV_RANK, -1),
    }
    w.update(g_mix=small["norm_mix"], g_q=small["q_latent_norm"], g_kv=small["kv_latent_norm"], g_a=small["out_norm_mla"],
             g_b=small["out_norm_sb"], g_f=small["norm_ffn"], g_n=small["norm_final"])
    return w


def _ffn_weights(full):
    w = {"w_oa": full["w_o"][:MLA_WIDTH], "w_ob": full["w_o"][MLA_WIDTH:], "w_down": full["w_down"]}
    w.update({name + "_t": full[name] for name in FLIPPED})
    return w


def _rope_tables(positions):
    inv_freq = ROPE_THETA ** (-jnp.arange(0, MLA_ROPE, 2, dtype=F32) / MLA_ROPE)
    ang = positions.astype(F32)[:, None] * inv_freq[None, :]
    cos, sin = jnp.cos(ang), jnp.sin(ang)
    return {"cos": jnp.tile(jnp.concatenate([cos, cos], axis=1), (1, MLA_HEADS)),
            "sin": jnp.tile(jnp.concatenate([sin, sin], axis=1), (1, MLA_HEADS))}


def _by_head(g_wide, g_narrow, wide, narrow):
    r = g_wide.shape[0]
    return jnp.concatenate([g_wide.reshape(r, MLA_HEADS, wide), g_narrow.reshape(r, MLA_HEADS, narrow)], axis=-1).reshape(r, -1)


def kernel(x, positions, norm_mix, w_in, q_latent_norm, w_uq, kv_latent_norm, w_ukv, out_norm_mla, out_norm_sb, w_o, norm_ffn, w_gate, w_up, w_down, norm_final, loss_target, m_norm_mix, m_w_in, m_q_latent_norm, m_w_uq, m_kv_latent_norm, m_w_ukv, m_out_norm_mla, m_out_norm_sb, m_w_o, m_norm_ffn, m_w_gate, m_w_up, m_w_down, m_norm_final, v_norm_mix, v_w_in, v_q_latent_norm, v_w_uq, v_kv_latent_norm, v_w_ukv, v_out_norm_mla, v_out_norm_sb, v_w_o, v_norm_ffn, v_w_gate, v_w_up, v_w_down, v_norm_final):
    given = dict(norm_mix=norm_mix, w_in=w_in, q_latent_norm=q_latent_norm, w_uq=w_uq, kv_latent_norm=kv_latent_norm, w_ukv=w_ukv,
                 out_norm_mla=out_norm_mla, out_norm_sb=out_norm_sb, w_o=w_o, norm_ffn=norm_ffn, w_gate=w_gate, w_up=w_up,
                 w_down=w_down, norm_final=norm_final)
    mom_m = dict(norm_mix=m_norm_mix, w_in=m_w_in, q_latent_norm=m_q_latent_norm, w_uq=m_w_uq, kv_latent_norm=m_kv_latent_norm,
                 w_ukv=m_w_ukv, out_norm_mla=m_out_norm_mla, out_norm_sb=m_out_norm_sb, w_o=m_w_o, norm_ffn=m_norm_ffn,
                 w_gate=m_w_gate, w_up=m_w_up, w_down=m_w_down, norm_final=m_norm_final)
    mom_v = dict(norm_mix=v_norm_mix, w_in=v_w_in, q_latent_norm=v_q_latent_norm, w_uq=v_w_uq, kv_latent_norm=v_kv_latent_norm,
                 w_ukv=v_w_ukv, out_norm_mla=v_out_norm_mla, out_norm_sb=v_out_norm_sb, w_o=v_w_o, norm_ffn=v_norm_ffn,
                 w_gate=v_w_gate, w_up=v_w_up, w_down=v_w_down, norm_final=v_norm_final)
    xs = x[0]
    tgt = loss_target[0]
    s = xs.shape[0]
    c_idx = lax.axis_index("c")
    shard_idx = 2 * lax.axis_index("x") + lax.axis_index("y")

    def block2d(t, name):
        t = t.reshape(t.shape[-2:])
        return t.T if name in FLIPPED else t

    shard2d = {name: block2d(given[name], name) for name, *_ in SHARDED}
    local = {name: shard2d[name].astype(BF16) for name, *_ in SHARDED}
    axis_of = {name: 0 if name in FLIPPED else axis for name, _, _, axis in SHARDED}

    def whole(names, gathered):
        return {name: _full_weight(t, axis_of[name]) for name, t in zip(names, gathered)}

    small = {name: given[name].reshape(1, n) for name, n in SMALL}
    w = _attention_weights(whole(EARLY, _allgather_list("allgather_w", [local[name] for name in EARLY])), small)
    tabs = _rope_tables(positions[0])
    msuf, mpre = _sb_masks(min(SB_TK, s))

    u, cq, ckv, cqn, ckvn, qn, qr, kn, vm, kr, sq, sk, sv = _fwd_a(xs, tabs, w)
    o_mla, lse, late = _mla_fwd(qn, qr, kn, kr, vm, [local[name] for name in LATE])
    w.update(_ffn_weights(whole(LATE, late)))
    o_sb, cmat = _sb_fwd(sq, sk, sv, msuf)
    merged, h1, f, gate, up, act = _fwd_b1(xs, o_mla, o_sb, w)
    dh2, loss_part, dg_n = _fwd_b2(h1, act, tgt, w)

    def shards_of(names, grads):
        return [_shard_major(grads[name], axis_of[name]) for name in names]

    def reduced(tag, chip_f32, others):
        mine = _add_chips("add_chips_" + tag, chip_f32, others, shard_idx)
        return tuple(mine), tuple(_swap_result("swap_result_" + tag, mine))

    dgate, dup, dh1, do_mla, do_sb, dg_f, dg_a, dg_b = _bwd_b(dh2, gate, up, h1, o_mla, o_sb, w)
    late_gs = shards_of(LATE, {
        "w_o": _tn_matmul("dw_o", merged, dh1), "w_gate": _tn_matmul("dw_gate", dgate, f),
        "w_up": _tn_matmul("dw_up", dup, f), "w_down": _tn_matmul("dw_down", act, dh2)})
    (dsq, dsk, dsv), late_got = _sb_bwd(sq, sk, sv, do_sb, cmat, msuf, mpre, late_gs)
    late_f32, late_bf16 = _add_sibling("add_sibling_late", late_gs, late_got, c_idx)
    (dqn, dqr, dkn, dkr, dvm), late_others = _mla_bwd(qn, qr, kn, kr, vm, o_mla, do_mla, lse, late_bf16)
    mine_late, theirs_late = reduced("late", late_f32, late_others)
    dx, a1, a2, dcq, dckv, dkrc, dkrs, dg_q, dg_kv, dg_mix = _bwd_a(xs, dh1, cq, ckv, dqn, dqr, dkn, dvm, dkr, dsq, dsk, dsv, tabs, w)

    g_cq, g_ckv, g_krc, g_krs, g_sq, g_sk, g_sv = _tn_multi("dw_in", u, [dcq, dckv, dkrc, dkrs, dsq, dsk, dsv])
    g_qn, g_qr1, g_qr2 = _tn_multi("dw_uq", cqn, [dqn, a1, a2])
    g_kn, g_v = _tn_multi("dw_ukv", ckvn, [dkn, dvm])
    slots = lambda g: g.reshape(g.shape[0], MLA_HEADS, MLA_ROPE)
    g_kr = jnp.sum(slots(g_krc), axis=1) + _rot_cols_t(jnp.sum(slots(g_krs), axis=1))
    g_qr = (slots(g_qr1) + _rot_cols_t(slots(g_qr2))).reshape(Q_RANK, -1)
    early_gs = shards_of(EARLY, {
        "w_in": jnp.concatenate([g_cq, g_ckv, g_kr, g_sq, g_sk, g_sv], axis=1),
        "w_uq": _by_head(g_qn, g_qr, MLA_NOPE, MLA_ROPE),
        "w_ukv": _by_head(g_kn, g_v, MLA_NOPE, MLA_V)})
    early_f32, early_bf16 = _add_sibling("add_sibling_early", early_gs, _swap_halves("swap_halves_early", early_gs), c_idx)
    mine_early, theirs_early = reduced("early", early_f32, _chip_scatter(early_bf16))
    halves = dict(zip(EARLY + LATE, zip(mine_early + mine_late, theirs_early + theirs_late)))

    small_parts = jnp.concatenate([dg_mix, dg_q, dg_kv, dg_a, dg_b, dg_f, dg_n, loss_part], axis=1)
    small_sum = _allreduce_small(jnp.broadcast_to(small_parts, (8, small_parts.shape[1])))
    small_g, loss = small_sum[0:1, :-LANES], small_sum[0, -LANES]

    g_out, d_out, m_out, v_out = {}, {}, {}, {}
    names = [name for name, *_ in SHARDED]
    updated = _adamw_halves([shard2d[name] for name in names], [halves[name][0] for name in names], [halves[name][1] for name in names],
                            [block2d(mom_m[name], name) for name in names], [block2d(mom_v[name], name) for name in names], c_idx)
    for name, outs in zip(names, updated):
        shape = given[name].shape
        g_out[name], d_out[name], m_out[name], v_out[name] = ((t.T if name in FLIPPED else t).reshape(shape) for t in outs)
    cat = lambda src: jnp.concatenate([src[name].reshape(1, n) for name, n in SMALL], axis=1)
    d, mn, vn = _adamw("adamw_small", cat(given), small_g, cat(mom_m), cat(mom_v))
    off = 0
    for name, n in SMALL:
        shape = given[name].shape
        g_out[name], d_out[name], m_out[name], v_out[name] = (t[:, off:off + n].reshape(shape) for t in (small_g, d, mn, vn))
        off += n

    order = ["norm_mix", "w_in", "q_latent_norm", "w_uq", "kv_latent_norm", "w_ukv", "out_norm_mla", "out_norm_sb", "w_o",
             "norm_ffn", "w_gate", "w_up", "w_down", "norm_final"]
    return (loss, dx[None], *[g_out[n] for n in order], *[d_out[n] for n in order], *[m_out[n] for n in order],
            *[v_out[n] for n in order])
```

```python
import functools
import math

import jax
import jax.numpy as jnp
from jax import lax
from jax.experimental import pallas as pl
from jax.experimental.pallas import tpu as pltpu

F32 = jnp.float32
BF16 = jnp.bfloat16
MESH = pl.DeviceIdType.MESH

D_MODEL = 1024
EPS = 1e-6
MLA_HEADS = 8
MLA_NOPE = 64
MLA_ROPE = 32
MLA_V = 64
MLA_QK = MLA_NOPE + MLA_ROPE
Q_RANK = 256
KV_RANK = 128
ROPE_THETA = 10000.0
SB_HEADS = 8
SB_DIM = 64
MLA_WIDTH = MLA_HEADS * MLA_V
SB_WIDTH = SB_HEADS * SB_DIM
D_FF = 2816
IN_WIDTH = Q_RANK + KV_RANK + MLA_ROPE + 3 * SB_WIDTH

ADAM_LR = 0.001
ADAM_B1 = 0.9
ADAM_B2 = 0.999
ADAM_EPS = 1e-08
ADAM_WD = 0.01
ADAM_STEP = 10

N_SHARD = 4
LANES = 128
ROPE_TILE = LANES
VMEM_LIMIT = 56 * 1024 * 1024
TN_ACC_BYTES = 6 * 1024 * 1024 + 512 * 1024
NEG = -1e30
MLA_SCALE = 1.0 / math.sqrt(MLA_QK)
MLA_DK_SCALE = math.log(2.0)
MLA_QSCALE = MLA_SCALE * math.log2(math.e)
SB_SKIP = 110.0

ROW_TILE = 512
ROW_TILE_ELEMENTWISE = 256
MLA_TQ = 1024
SB_TQ = 512
MLA_TK = 1024
MLA_BWD_TK = 1024
MLA_DIAG_TK = 256
SB_TK = 256
TN_TS = 2048
ADAM_STEPS = 4


def _dot(a, b):
    return jnp.dot(a, b, preferred_element_type=F32)


def _dot_nt(a, b):
    return lax.dot_general(a, b, (((1,), (1,)), ((), ())), preferred_element_type=F32)


def _dot_tn(a, b):
    return lax.dot_general(a, b, (((0,), (0,)), ((), ())), preferred_element_type=F32)


def _params(n_grid, vmem=VMEM_LIMIT):
    return pltpu.CompilerParams(dimension_semantics=("arbitrary",) * n_grid, vmem_limit_bytes=vmem)


def _rms(x):
    r = lax.rsqrt(jnp.mean(x * x, axis=-1, keepdims=True) + EPS)
    return x * r, r


def _rms_bwd(n, r, g, dy):
    dn = dy * g
    dx = r * (dn - n * jnp.mean(dn * n, axis=-1, keepdims=True))
    return dx, jnp.sum(dy * n, axis=0, keepdims=True)


def _accumulate(ref, val, step):
    @pl.when(step == 0)
    def _():
        ref[...] = val

    @pl.when(step != 0)
    def _():
        ref[...] += val


def _rowwise(name, body, rows, consts, row_out, acc_out, tm):
    n_rows = rows[0].shape[0]
    tm = min(tm, n_rows)
    nr, nc, no = len(rows), len(consts), len(row_out)

    def kern(*refs):
        body(refs[:nr], refs[nr:nr + nc], refs[nr + nc:nr + nc + no], refs[nr + nc + no:], pl.program_id(0))

    in_specs = [pl.BlockSpec((tm, a.shape[1]), lambda i: (i, 0)) for a in rows]
    in_specs += [pl.BlockSpec(a.shape, lambda i: (0, 0), pipeline_mode=pl.Buffered(1)) for a in consts]
    out_specs = [pl.BlockSpec((tm, s.shape[1]), lambda i: (i, 0)) for s in row_out]
    out_specs += [pl.BlockSpec(s.shape, lambda i: (0, 0)) for s in acc_out]
    return pl.pallas_call(
        kern, name=name, grid=(n_rows // tm,), in_specs=in_specs, out_specs=out_specs,
        out_shape=list(row_out) + list(acc_out), compiler_params=_params(1),
    )(*rows, *consts)


def _sds(shape, dtype):
    return jax.ShapeDtypeStruct(shape, dtype)


def _fwd_a(x, tabs, w):
    s = x.shape[0]

    def body(r, c, o, a, step):
        x_ref, cos_ref, sin_ref = r
        gmix, wcq, wckv, wkr, wkrr, wsq, wsk, wsv, gq, wqn, wqr, wqrr, gkv, wkn, wv = c
        u_o, cq_o, ckv_o, cqn_o, ckvn_o, qn_o, qr_o, kn_o, v_o, kr_o, sq_o, sk_o, sv_o = o
        cos, sin = cos_ref[...], sin_ref[...]
        n, _ = _rms(x_ref[...])
        u = (n * gmix[...]).astype(BF16)
        u_o[...] = u
        cq = _dot(u, wcq[...])
        ckv = _dot(u, wckv[...])
        kr_o[...] = (_dot(u, wkr[...]) * cos[:, :ROPE_TILE] + _dot(u, wkrr[...]) * sin[:, :ROPE_TILE]).astype(BF16)
        sq_o[...] = _dot(u, wsq[...]).astype(BF16)
        sk_o[...] = _dot(u, wsk[...]).astype(BF16)
        sv_o[...] = _dot(u, wsv[...]).astype(BF16)
        cq_o[...] = cq
        ckv_o[...] = ckv
        nq, _ = _rms(cq)
        cqn = (nq * gq[...]).astype(BF16)
        cqn_o[...] = cqn
        qn_o[...] = (_dot(cqn, wqn[...]) * MLA_QSCALE).astype(BF16)
        qr_o[...] = ((_dot(cqn, wqr[...]) * cos + _dot(cqn, wqrr[...]) * sin) * MLA_QSCALE).astype(BF16)
        nkv, _ = _rms(ckv)
        ckvn = (nkv * gkv[...]).astype(BF16)
        ckvn_o[...] = ckvn
        kn_o[...] = _dot(ckvn, wkn[...]).astype(BF16)
        v_o[...] = _dot(ckvn, wv[...]).astype(BF16)

    outs = [
        _sds((s, D_MODEL), BF16), _sds((s, Q_RANK), F32), _sds((s, KV_RANK), F32), _sds((s, Q_RANK), BF16),
        _sds((s, KV_RANK), BF16), _sds((s, MLA_HEADS * MLA_NOPE), BF16), _sds((s, MLA_HEADS * MLA_ROPE), BF16),
        _sds((s, MLA_HEADS * MLA_NOPE), BF16), _sds((s, MLA_WIDTH), BF16), _sds((s, ROPE_TILE), BF16),
        _sds((s, SB_WIDTH), BF16), _sds((s, SB_WIDTH), BF16), _sds((s, SB_WIDTH), BF16),
    ]
    consts = [w["g_mix"], w["w_cq"], w["w_ckv"], w["w_kr4"], w["w_kr4r"], w["w_sbq"], w["w_sbk"], w["w_sbv"], w["g_q"],
              w["w_qn"], w["w_qr"], w["w_qrr"], w["g_kv"], w["w_kn"], w["w_v"]]
    return _rowwise("fwd_a", body, [x, tabs["cos"], tabs["sin"]], consts, outs, [], ROW_TILE)


def _fwd_b1(x, o_mla, o_sb, w):
    s = x.shape[0]

    def body(r, c, o, a, step):
        x_ref, oa_ref, ob_ref = r
        ga, gb, woa, wob, gf, wg, wu = c
        mg_o, h1_o, f_o, gate_o, up_o, act_o = o
        na, _ = _rms(oa_ref[...])
        nb, _ = _rms(ob_ref[...])
        ma = (na * ga[...]).astype(BF16)
        mb = (nb * gb[...]).astype(BF16)
        mg_o[:, :MLA_WIDTH] = ma
        mg_o[:, MLA_WIDTH:] = mb
        h1 = x_ref[...] + _dot(ma, woa[...]) + _dot(mb, wob[...])
        h1_o[...] = h1
        nf, _ = _rms(h1)
        f = (nf * gf[...]).astype(BF16)
        f_o[...] = f
        gate = _dot_nt(f, wg[...])
        up = _dot_nt(f, wu[...])
        gate_o[...] = gate.astype(BF16)
        up_o[...] = up.astype(BF16)
        act_o[...] = (gate * (1.0 / (1.0 + jnp.exp(-gate))) * up).astype(BF16)

    outs = [_sds((s, D_MODEL), BF16), _sds((s, D_MODEL), F32), _sds((s, D_MODEL), BF16), _sds((s, D_FF), BF16),
            _sds((s, D_FF), BF16), _sds((s, D_FF), BF16)]
    consts = [w["g_a"], w["g_b"], w["w_oa"], w["w_ob"], w["g_f"], w["w_gate_t"], w["w_up_t"]]
    return _rowwise("fwd_b1", body, [x, o_mla, o_sb], consts, outs, [], ROW_TILE)


def _fwd_b2(h1, act, tgt, w):
    s = h1.shape[0]

    def body(r, c, o, a, step):
        h1_ref, act_ref, t_ref = r
        wd, gn = c
        (dh2_o,) = o
        loss_o, dgn_o = a
        h2 = h1_ref[...] + _dot(act_ref[...], wd[...])
        n2, r2 = _rms(h2)
        err = n2 * gn[...] - t_ref[...]
        part = jnp.sum(jnp.sum(err * err, axis=1, keepdims=True), axis=0, keepdims=True) * (0.5 / D_MODEL)
        _accumulate(loss_o, jnp.broadcast_to(part, (1, LANES)), step)
        dh2, dgn = _rms_bwd(n2, r2, gn[...], err * (1.0 / D_MODEL))
        dh2_o[...] = dh2
        _accumulate(dgn_o, dgn, step)

    return _rowwise("fwd_b2", body, [h1, act, tgt], [w["w_down"], w["g_n"]], [_sds((s, D_MODEL), F32)],
                    [_sds((1, LANES), F32), _sds((1, D_MODEL), F32)], ROW_TILE)


def _bwd_b(dh2, gate, up, h1, o_mla, o_sb, w):
    s = h1.shape[0]

    def body(r, c, o, a, step):
        dh2_ref, gate_ref, up_ref, h1_ref, oa_ref, ob_ref = r
        wd, wgt, wut, gf, woa, wob, ga, gb = c
        dgate_o, dup_o, dh1_o, doa_o, dob_o = o
        dgf_o, dga_o, dgb_o = a
        dh2 = dh2_ref[...]
        dact = _dot_nt(dh2.astype(BF16), wd[...])
        gate = gate_ref[...].astype(F32)
        sig = 1.0 / (1.0 + jnp.exp(-gate))
        dup = (dact * (gate * sig)).astype(BF16)
        dgate = (dact * up_ref[...].astype(F32) * (sig * (1.0 + gate * (1.0 - sig)))).astype(BF16)
        dup_o[...] = dup
        dgate_o[...] = dgate
        df = _dot(dgate, wgt[...]) + _dot(dup, wut[...])
        nf, rf = _rms(h1_ref[...])
        dres, dgf = _rms_bwd(nf, rf, gf[...], df)
        dh1 = dh2 + dres
        dh1_o[...] = dh1
        dh1b = dh1.astype(BF16)
        na, ra = _rms(oa_ref[...])
        doa, dga = _rms_bwd(na, ra, ga[...], _dot_nt(dh1b, woa[...]))
        nb, rb = _rms(ob_ref[...])
        dob, dgb = _rms_bwd(nb, rb, gb[...], _dot_nt(dh1b, wob[...]))
        doa_o[...] = doa
        dob_o[...] = dob
        _accumulate(dgf_o, dgf, step)
        _accumulate(dga_o, dga, step)
        _accumulate(dgb_o, dgb, step)

    consts = [w["w_down"], w["w_gate_t"], w["w_up_t"], w["g_f"], w["w_oa"], w["w_ob"], w["g_a"], w["g_b"]]
    outs = [_sds((s, D_FF), BF16), _sds((s, D_FF), BF16), _sds((s, D_MODEL), F32), _sds((s, MLA_WIDTH), F32), _sds((s, SB_WIDTH), F32)]
    accs = [_sds((1, D_MODEL), F32), _sds((1, MLA_WIDTH), F32), _sds((1, SB_WIDTH), F32)]
    return _rowwise("bwd_b", body, [dh2, gate, up, h1, o_mla, o_sb], consts, outs, accs, ROW_TILE_ELEMENTWISE)


def _fold_pairs(t):
    return jnp.concatenate([t[:, :LANES] + t[:, LANES:2 * LANES], t[:, 2 * LANES:3 * LANES] + t[:, 3 * LANES:]], axis=1)


def _bwd_a(x, dh1, cq, ckv, dqn, dqr, dkn, dvm, dkr, dsq, dsk, dsv, tabs, w):
    s = x.shape[0]

    def body(r, c, o, a, step):
        x_ref, dh1_ref, cq_ref, ckv_ref, dqn_ref, dqr_ref, dkn_ref, dvm_ref, dkr_ref, dsq_ref, dsk_ref, dsv_ref, cos_ref, sin_ref = r
        wqn, wqr, wqrr, gq, wkn, wv, gkv, wcq, wckv, wkr, wkrr, wsq, wsk, wsv, gmix = c
        dx_o, a1_o, a2_o, dcq_o, dckv_o, dkrc_o, dkrs_o = o
        dgq_o, dgkv_o, dgmix_o = a
        cos, sin = cos_ref[...], sin_ref[...]
        dqr = _fold_pairs(dqr_ref[...].astype(F32))
        a1 = (dqr * cos).astype(BF16)
        a2 = (dqr * sin).astype(BF16)
        a1_o[...] = a1
        a2_o[...] = a2
        nq, rq = _rms(cq_ref[...])
        dcqn = _dot_nt(dqn_ref[...], wqn[...]) + _dot_nt(a1, wqr[...]) + _dot_nt(a2, wqrr[...])
        dcq, dgq = _rms_bwd(nq, rq, gq[...], dcqn)
        nkv, rkv = _rms(ckv_ref[...])
        dckvn = _dot_nt(dkn_ref[...], wkn[...]) + _dot_nt(dvm_ref[...], wv[...])
        dckv, dgkv = _rms_bwd(nkv, rkv, gkv[...], dckvn)
        dkr = _fold_pairs(dkr_ref[...].astype(F32))
        dcq_b = dcq.astype(BF16)
        dckv_b = dckv.astype(BF16)
        dkrc = (dkr * cos).astype(BF16)
        dkrs = (dkr * sin).astype(BF16)
        dcq_o[...] = dcq_b
        dckv_o[...] = dckv_b
        dkrc_o[...] = dkrc
        dkrs_o[...] = dkrs
        du = (_dot_nt(dcq_b, wcq[...]) + _dot_nt(dckv_b, wckv[...]) + _dot_nt(dkrc, wkr[...]) + _dot_nt(dkrs, wkrr[...])
              + _dot_nt(dsq_ref[...], wsq[...]) + _dot_nt(dsk_ref[...], wsk[...]) + _dot_nt(dsv_ref[...], wsv[...]))
        nx, rx = _rms(x_ref[...])
        dres, dgmix = _rms_bwd(nx, rx, gmix[...], du)
        dx_o[...] = dh1_ref[...] + dres
        _accumulate(dgq_o, dgq, step)
        _accumulate(dgkv_o, dgkv, step)
        _accumulate(dgmix_o, dgmix, step)

    consts = [w["w_qn"], w["w_qr"], w["w_qrr"], w["g_q"], w["w_kn"], w["w_v"], w["g_kv"], w["w_cq"], w["w_ckv"],
              w["w_kr8"], w["w_kr8r"], w["w_sbq"], w["w_sbk"], w["w_sbv"], w["g_mix"]]
    rope_w = MLA_HEADS * MLA_ROPE
    outs = [_sds((s, D_MODEL), F32), _sds((s, rope_w), BF16), _sds((s, rope_w), BF16), _sds((s, Q_RANK), BF16),
            _sds((s, KV_RANK), BF16), _sds((s, rope_w), BF16), _sds((s, rope_w), BF16)]
    accs = [_sds((1, Q_RANK), F32), _sds((1, KV_RANK), F32), _sds((1, D_MODEL), F32)]
    rows = [x, dh1, cq, ckv, dqn, dqr, dkn, dvm, dkr, dsq, dsk, dsv, tabs["cos"], tabs["sin"]]
    return _rowwise("bwd_a", body, rows, consts, outs, accs, ROW_TILE)


def _tn_multi(name, x, ys):
    s, k = x.shape
    ts = min(TN_TS, s)
    n_y = len(ys)

    def kern(*refs):
        step = pl.program_id(0)
        xb = refs[0][...].astype(BF16)
        for j in range(n_y):
            _accumulate(refs[1 + n_y + j], _dot_tn(xb, refs[1 + j][...].astype(BF16)), step)

    return pl.pallas_call(
        kern, name=name, grid=(s // ts,),
        in_specs=[pl.BlockSpec((ts, k), lambda i: (i, 0))] + [pl.BlockSpec((ts, y.shape[1]), lambda i: (i, 0)) for y in ys],
        out_specs=[pl.BlockSpec((k, y.shape[1]), lambda i: (0, 0)) for y in ys],
        out_shape=[_sds((k, y.shape[1]), F32) for y in ys], compiler_params=_params(1),
    )(x, *ys)


def _tn_tile(k, n):
    if n % LANES or k * n * 4 <= TN_ACC_BYTES:
        return n
    units = n // LANES
    best = 1
    for d in range(1, units + 1):
        if units % d == 0 and k * d * LANES * 4 <= TN_ACC_BYTES:
            best = d
    return best * LANES


def _tn_matmul(name, x, y):
    s, k = x.shape
    n = y.shape[1]
    ts = min(TN_TS, s)
    tn = _tn_tile(k, n)

    def kern(x_ref, y_ref, o_ref):
        step = pl.program_id(1)
        _accumulate(o_ref, _dot_tn(x_ref[...].astype(BF16), y_ref[...].astype(BF16)), step)

    return pl.pallas_call(
        kern, name=name, grid=(n // tn, s // ts),
        in_specs=[pl.BlockSpec((ts, k), lambda j, i: (i, 0)), pl.BlockSpec((ts, tn), lambda j, i: (i, j))],
        out_specs=pl.BlockSpec((k, tn), lambda j, i: (0, j)), out_shape=_sds((k, n), F32), compiler_params=_params(2),
    )(x, y)


def _lanes(rows, lo, width):
    lane = lax.broadcasted_iota(jnp.int32, (rows, LANES), 1)
    return jnp.logical_and(lane >= lo, lane < lo + width)


def _keep(mask, t):
    return jnp.where(mask, t, jnp.zeros_like(t))


def _mla_qcat(qn_ref, qr_ref, rope_lo, half, rows):
    qn = _keep(_lanes(rows, MLA_NOPE * half, MLA_NOPE), qn_ref[...])
    qr = _keep(_lanes(rows, rope_lo, MLA_ROPE), qr_ref[...])
    return jnp.concatenate([qn, qr], axis=1)


def _diag_mask(rows, width, row0, col0):
    row = lax.broadcasted_iota(jnp.int32, (rows, width), 0)
    col = lax.broadcasted_iota(jnp.int32, (rows, width), 1)
    return col + (col0 - row0) <= row


def _mla_fwd(qn, qr, kn, kr, v, riders=()):
    s = qn.shape[0]
    tq, tk = min(MLA_TQ, s), min(MLA_TK, s)
    td = tq
    ratio = tq // tk

    n_ride = len(riders)
    n_pairs = MLA_HEADS // 2

    def kern(qn_ref, qr_ref, kn_ref, kr_ref, v_ref, *rest):
        o_ref, lse_ref = rest[n_ride:n_ride + 2]
        g = pl.program_id(0)
        i = pl.program_id(1)
        if n_ride:
            send, forward, finish = _gather_steps(rest[:n_ride], rest[n_ride + 2:2 * n_ride + 2], *rest[2 * n_ride + 2:])
            pl.when(jnp.logical_and(g == 0, i == 0))(send)
            pl.when(jnp.logical_and(g == 1, i == 0))(forward)
        qcat = [_mla_qcat(qn_ref, qr_ref, MLA_ROPE * (2 * (g % 2) + half), half, tq) for half in range(2)]

        def block(k0, width, carry, row0, masked, half):
            m, l, acc = (c[row0:] for c in carry)
            ks = pl.ds(pl.multiple_of(k0, width), width)
            kcat = jnp.concatenate([kn_ref[ks, :], kr_ref[ks, :]], axis=1)
            sc = _dot_nt(qcat[half][row0:], kcat)
            if masked:
                sc = jnp.where(_diag_mask(tq - row0, width, row0, row0), sc, NEG)
            m_new = jnp.maximum(m, jnp.max(sc, axis=1, keepdims=True))
            p = jnp.exp2(sc - m_new)
            alpha = jnp.exp2(m - m_new)
            l = alpha * l + jnp.sum(p, axis=1, keepdims=True)
            acc = alpha * acc + _dot(p.astype(BF16), v_ref[ks, :])
            new = (m_new, l, acc)
            return new if row0 == 0 else tuple(jnp.concatenate([c[:row0], n], axis=0) for c, n in zip(carry, new))

        def both(k0, width, carries, row0, masked):
            return tuple(block(k0, width, carries[half], row0, masked, half) for half in range(2))

        init = (jnp.full((tq, 1), NEG, F32), jnp.zeros((tq, 1), F32), jnp.zeros((tq, LANES), F32))
        carries = lax.fori_loop(0, i * ratio, lambda kb, c: both(kb * tk, tk, c, 0, False), (init, init))
        for row0 in range(0, tq, td):
            carries = both(i * tq + row0, td, carries, row0, True)
        for half in range(2):
            m, l, acc = carries[half]
            out = _keep(_lanes(tq, MLA_V * half, MLA_V), acc / l)
            lse = _keep(_lanes(tq, MLA_ROPE * half, MLA_ROPE), jnp.broadcast_to(m + jnp.log2(l), (tq, LANES)))
            if half == 0:
                o_ref[...] = out
                lse_ref[...] = lse
            else:
                o_ref[...] += out
                lse_ref[...] += lse
        if n_ride:
            pl.when(jnp.logical_and(g == n_pairs - 1, i == s // tq - 1))(finish)

    qblk = pl.BlockSpec((tq, LANES), lambda g, i: (i, g))
    full = pl.BlockSpec((s, LANES), lambda g, i: (0, g))
    outs = pl.pallas_call(
        kern, name="mla_fwd", grid=(n_pairs, s // tq),
        in_specs=[qblk, pl.BlockSpec((tq, LANES), lambda g, i: (i, g // 2)), full, pl.BlockSpec((s, LANES), lambda g, i: (0, 0)), full]
        + [HBM_SPEC] * n_ride,
        out_specs=[qblk, qblk] + [HBM_SPEC] * n_ride,
        out_shape=[_sds((s, MLA_WIDTH), F32), _sds((s, n_pairs * LANES), F32)] + [_sds((N_SHARD,) + a.shape, a.dtype) for a in riders],
        scratch_shapes=_gather_sems(n_ride) if n_ride else [], compiler_params=_params(2),
    )(qn, qr, kn, kr, v, *riders)
    return outs[0], outs[1], outs[2:]


def _mla_bwd(qn, qr, kn, kr, v, o, do, lse, riders=()):
    s = qn.shape[0]
    tq, tk, td = min(MLA_TQ, s), min(MLA_BWD_TK, s), min(MLA_DIAG_TK, s)
    ratio = tq // tk

    n_ride = len(riders)
    n_pairs = MLA_HEADS // 2

    def kern(qn_ref, qr_ref, kn_ref, kr_ref, v_ref, o_ref, do_ref, lse_ref, *rest):
        dqn_ref, dqr_ref, dkn_out, dkr_out, dv_out = rest[n_ride:n_ride + 5]
        dkn_ref, dkr_ref, dv_ref = rest[2 * n_ride + 5:2 * n_ride + 8]
        g = pl.program_id(0)
        i = pl.program_id(1)
        if n_ride:
            start, finish = _scatter_steps(rest[:n_ride], rest[n_ride + 5:2 * n_ride + 5], *rest[2 * n_ride + 8:])
            pl.when(jnp.logical_and(g == 0, i == 0))(start)

        @pl.when(i == 0)
        def _():
            dkn_ref[...] = jnp.zeros_like(dkn_ref)
            dkr_ref[...] = jnp.zeros_like(dkr_ref)
            dv_ref[...] = jnp.zeros_like(dv_ref)

        for half in range(2):
            rope_lo = MLA_ROPE * (2 * (g % 2) + half)
            qcat = _mla_qcat(qn_ref, qr_ref, rope_lo, half, tq)
            mine = _lanes(tq, MLA_V * half, MLA_V)
            do_f = _keep(mine, do_ref[...])
            do_b = do_f.astype(BF16)
            delta = jnp.sum(do_f * o_ref[...], axis=1, keepdims=True)
            lse_v = lse_ref[:, MLA_ROPE * half:MLA_ROPE * half + 1]

            def block(k0, width, dq_acc, row0, masked, qcat=qcat, do_b=do_b, delta=delta, lse_v=lse_v):
                ks = pl.ds(pl.multiple_of(k0, width), width)
                kcat = jnp.concatenate([kn_ref[ks, :], kr_ref[ks, :]], axis=1)
                qc, dob = qcat[row0:], do_b[row0:]
                p = jnp.exp2(_dot_nt(qc, kcat) - lse_v[row0:])
                if masked:
                    p = jnp.where(_diag_mask(tq - row0, width, row0, row0), p, 0.0)
                ds = (p * (_dot_nt(dob, v_ref[ks, :]) - delta[row0:])).astype(BF16)
                dv_ref[ks, :] += _dot_tn(p.astype(BF16), dob)
                dkc = _dot_tn(ds, qc)
                dkn_ref[ks, :] += dkc[:, :LANES]
                dkr_ref[ks, :] += dkc[:, LANES:]
                new = dq_acc[row0:] + _dot(ds, kcat)
                return new if row0 == 0 else jnp.concatenate([dq_acc[:row0], new], axis=0)

            acc = lax.fori_loop(0, i * ratio, lambda kb, c, block=block: block(kb * tk, tk, c, 0, False),
                                jnp.zeros((tq, 2 * LANES), F32))
            for row0 in range(0, tq, td):
                acc = block(i * tq + row0, td, acc, row0, True)
            dqn = _keep(_lanes(tq, MLA_NOPE * half, MLA_NOPE), acc[:, :LANES] * MLA_SCALE)
            dqr = _keep(_lanes(tq, rope_lo, MLA_ROPE), acc[:, LANES:] * MLA_SCALE)
            if half == 0:
                dqn_ref[...] = dqn.astype(BF16)
                dqr_ref[...] = dqr.astype(BF16)
            else:
                dqn_ref[...] += dqn.astype(BF16)
                dqr_ref[...] += dqr.astype(BF16)

        @pl.when(i == s // tq - 1)
        def _():
            dkn_out[...] = (dkn_ref[...] * MLA_DK_SCALE).astype(BF16)
            dkr_out[...] = (dkr_ref[...] * MLA_DK_SCALE).astype(BF16)
            dv_out[...] = dv_ref[...].astype(BF16)

        if n_ride:
            pl.when(jnp.logical_and(g == n_pairs - 1, i == s // tq - 1))(finish)

    qblk = pl.BlockSpec((tq, LANES), lambda g, i: (i, g))
    full = pl.BlockSpec((s, LANES), lambda g, i: (0, g))
    once = lambda spec_map: pl.BlockSpec((s, LANES), spec_map, pipeline_mode=pl.Buffered(1))
    wide = _sds((s, n_pairs * LANES), BF16)
    outs = pl.pallas_call(
        kern, name="mla_bwd", grid=(n_pairs, s // tq),
        in_specs=[qblk, pl.BlockSpec((tq, LANES), lambda g, i: (i, g // 2)), once(lambda g, i: (0, g)), once(lambda g, i: (0, 0)),
                  once(lambda g, i: (0, g)), qblk, qblk, qblk] + [HBM_SPEC] * n_ride,
        out_specs=[qblk, qblk, full, full, full] + [HBM_SPEC] * n_ride,
        out_shape=[wide] * 5 + [_sds((N_SHARD - 1,) + p.shape[1:], p.dtype) for p in riders],
        scratch_shapes=[pltpu.VMEM((s, LANES), F32)] * 3 + (_scatter_sems(n_ride) if n_ride else []), compiler_params=_params(2),
    )(qn, qr, kn, kr, v, o, do, lse, *riders)
    return outs[:5], outs[5:]


def _sb_masks(tk):
    j = lax.broadcasted_iota(jnp.int32, (tk, tk), 0)
    c = lax.broadcasted_iota(jnp.int32, (tk, tk), 1)
    return (j > c).astype(BF16), (j < c).astype(BF16)


def _sb_scores(qs, kk, msuf, strict):
    z = _dot_nt(qs, kk)
    lom = -(jnp.maximum(z, 0.0) + jnp.log(1.0 + jnp.exp(-jnp.abs(z))))
    if strict is not None:
        lom = jnp.where(strict, lom, 0.0)
    return z, lom, _dot(lom.astype(BF16), msuf)


def _sb_strict(tq, tk, d):
    row = lax.broadcasted_iota(jnp.int32, (tq, tk), 0)
    col = lax.broadcasted_iota(jnp.int32, (tq, tk), 1)
    return col + d * tk < row


def _sb_fwd(q, k, v, msuf):
    s = q.shape[0]
    tq, tk = min(SB_TQ, s), min(SB_TK, s)
    ratio = tq // tk
    subs = 2 if s % (2 * tq) == 0 else 1
    chains = [(sub, half) for sub in range(subs) for half in range(2)]

    def kern(q_ref, k_ref, v_ref, m_ref, o_ref, c_ref):
        i = pl.program_id(1)
        msf = m_ref[...]
        lane = lax.broadcasted_iota(jnp.int32, (tq, LANES), 1)
        mine = [_lanes(tq, SB_DIM * half, SB_DIM) for half in range(2)]
        rows = [slice(tq * sub, tq * (sub + 1)) for sub in range(subs)]
        qs = {(sub, half): _keep(mine[half], q_ref[rows[sub], :]) * 0.125 for sub, half in chains}
        tile = [subs * i + sub for sub in range(subs)]

        def block(kb, carry, dd, chain, valid=None):
            c, acc, cm = carry
            ks = pl.ds(pl.multiple_of(jnp.maximum(kb, 0) * tk, tk), tk)
            strict = None if dd is None else _sb_strict(tq, tk, dd)
            z, lom, suf = _sb_scores(qs[chain], k_ref[ks, :], msf, strict)
            c_in = c if valid is None else jnp.where(valid, c, NEG)
            a = jnp.exp(z + lom + (suf + c_in))
            if strict is not None:
                a = jnp.where(strict, a, 0.0)
            acc = acc + _dot(a.astype(BF16), v_ref[ks, :])
            cm = jnp.where(lane == kb, c, cm)
            return c + jnp.sum(lom, axis=1, keepdims=True), acc, cm

        init = (jnp.zeros((tq, 1), F32), jnp.zeros((tq, LANES), F32), jnp.full((tq, LANES), NEG, F32))
        carries = {chain: init for chain in chains}
        for dd in range(ratio - 1, -1, -1):
            carries = {chain: block(tile[chain[0]] * ratio + dd, carries[chain], dd, chain) for chain in chains}

        def left(sub, t):
            return tile[sub] * ratio - 1 - t

        def live(st):
            t, cs = st
            alive = [jnp.logical_and(left(sub, t) >= 0, jnp.max(cs[chains.index((sub, half))][0]) > -SB_SKIP) for sub, half in chains]
            return functools.reduce(jnp.logical_or, alive)

        def step(st):
            t, cs = st
            return t + 1, tuple(block(left(sub, t), cs[n], None, (sub, half), left(sub, t) >= 0) for n, (sub, half) in enumerate(chains))

        _, done = lax.while_loop(live, step, (0, tuple(carries[chain] for chain in chains)))
        for sub in range(subs):
            d0, d1 = done[chains.index((sub, 0))], done[chains.index((sub, 1))]
            o_ref[rows[sub], :] = _keep(mine[0], d0[1]) + _keep(mine[1], d1[1])
            c_ref[rows[sub], :LANES] = d0[2]
            c_ref[rows[sub], LANES:] = d1[2]

    qblk = lambda n: pl.BlockSpec((subs * tq, n), lambda g, i: (i, g))
    full = pl.BlockSpec((s, LANES), lambda g, i: (0, g))
    return pl.pallas_call(
        kern, name="sb_fwd", grid=(SB_HEADS // 2, s // (subs * tq)),
        in_specs=[qblk(LANES), full, full, pl.BlockSpec((tk, tk), lambda g, i: (0, 0))],
        out_specs=[qblk(LANES), qblk(2 * LANES)],
        out_shape=[_sds((s, SB_WIDTH), F32), _sds((s, SB_HEADS * LANES), F32)], compiler_params=_params(2),
    )(q, k, v, msuf)


def _sb_bwd(q, k, v, do, cmat, msuf, mpre, riders=()):
    s = q.shape[0]
    tq, tk = min(SB_TQ, s), min(SB_TK, s)
    ratio = tq // tk

    n_ride = len(riders)
    n_pairs = SB_HEADS // 2

    def kern(q_ref, k_ref, v_ref, do_ref, c_ref, ms_ref, mp_ref, *rest):
        dq_ref, dk_out, dv_out = rest[n_ride:n_ride + 3]
        dk_ref, dv_ref = rest[2 * n_ride + 3:2 * n_ride + 5]
        i = pl.program_id(1)
        if n_ride:
            start, finish = _swap_steps(rest[:n_ride], rest[n_ride + 3:2 * n_ride + 3], *rest[2 * n_ride + 5:])
            pl.when(jnp.logical_and(pl.program_id(0) == 0, i == 0))(start)

        @pl.when(i == 0)
        def _():
            dk_ref[...] = jnp.zeros_like(dk_ref)
            dv_ref[...] = jnp.zeros_like(dv_ref)

        msf = ms_ref[...]
        mpf = mp_ref[...]
        lane = lax.broadcasted_iota(jnp.int32, (tq, LANES), 1)
        lane1 = lax.broadcasted_iota(jnp.int32, (1, LANES), 1)
        mine = [_lanes(tq, SB_DIM * half, SB_DIM) for half in range(2)]
        qv = [_keep(m, q_ref[...]) for m in mine]
        qs = [t * 0.125 for t in qv]
        do_b = [_keep(m, do_ref[...]).astype(BF16) for m in mine]
        cm = [c_ref[:, :LANES], c_ref[:, LANES:]]

        def block(kb, carry, dd, half):
            dq_acc, pc = carry
            ks = pl.ds(pl.multiple_of(kb * tk, tk), tk)
            kk = k_ref[ks, :]
            strict = None if dd is None else _sb_strict(tq, tk, dd)
            z, lom, suf = _sb_scores(qs[half], kk, msf, strict)
            c = jnp.sum(jnp.where(lane == kb, cm[half], 0.0), axis=1, keepdims=True)
            a = jnp.exp(z + lom + (suf + c))
            if strict is not None:
                a = jnp.where(strict, a, 0.0)
            g = _dot_nt(do_b[half], v_ref[ks, :]) * a
            p = pc + _dot(g.astype(BF16), mpf)
            omb = jnp.exp(lom)
            dz = (g * omb - (1.0 - omb) * p) * 0.125
            if strict is not None:
                dz = jnp.where(strict, dz, 0.0)
            dz = dz.astype(BF16)
            dv_ref[ks, :] += _dot_tn(a.astype(BF16), do_b[half])
            dk_ref[ks, :] += _dot_tn(dz, qv[half])
            return dq_acc + _dot(dz, kk), pc + jnp.sum(g, axis=1, keepdims=True)

        def needed(cm_h):
            seen = jnp.logical_and(jnp.max(cm_h, axis=0, keepdims=True) > -SB_SKIP, lane1 < i * ratio)
            return jnp.sum(seen.astype(jnp.int32))

        first = i * ratio - jnp.maximum(needed(cm[0]), needed(cm[1]))
        init = (jnp.zeros((tq, LANES), F32), jnp.zeros((tq, 1), F32))
        carries = lax.fori_loop(first, i * ratio, lambda kb, c: (block(kb, c[0], None, 0), block(kb, c[1], None, 1)), (init, init))
        for dd in range(ratio):
            carries = [block(i * ratio + dd, carries[half], dd, half) for half in range(2)]
        dq_ref[...] = (_keep(mine[0], carries[0][0]) + _keep(mine[1], carries[1][0])).astype(BF16)

        @pl.when(i == s // tq - 1)
        def _():
            dk_out[...] = dk_ref[...].astype(BF16)
            dv_out[...] = dv_ref[...].astype(BF16)

        if n_ride:
            pl.when(jnp.logical_and(pl.program_id(0) == n_pairs - 1, i == s // tq - 1))(finish)

    qblk = lambda n: pl.BlockSpec((tq, n), lambda g, i: (i, g))
    full = pl.BlockSpec((s, LANES), lambda g, i: (0, g))
    msk = pl.BlockSpec((tk, tk), lambda g, i: (0, 0))
    outs = pl.pallas_call(
        kern, name="sb_bwd", grid=(n_pairs, s // tq),
        in_specs=[qblk(LANES), full, full, qblk(LANES), qblk(2 * LANES), msk, msk] + [HBM_SPEC] * n_ride,
        out_specs=[qblk(LANES), full, full] + [HBM_SPEC] * n_ride,
        out_shape=[_sds((s, SB_WIDTH), BF16)] * 3 + _halves_shapes(riders),
        scratch_shapes=[pltpu.VMEM((s, LANES), F32)] * 2 + (_swap_sems(n_ride) if n_ride else []), compiler_params=_params(2),
    )(q, k, v, do, cmat, msuf, mpre, *riders)
    return outs[:3], outs[3:]


def _place():
    return lax.axis_index("x"), lax.axis_index("y"), lax.axis_index("c")


def _other_chips(x, y):
    return [(1 - x, y), (x, 1 - y), (1 - x, 1 - y)]


HBM_SPEC = pl.BlockSpec(memory_space=pl.ANY)


def _gather_steps(ins, outs, send_sems, recv_sems):
    n = len(ins)
    x, y, c = _place()
    sibling = (x, y, 1 - c)
    chips = _other_chips(x, y)

    def half_of(a, ref, pc):
        half = ins[a].shape[0] // 2
        return ref.at[pl.ds(pl.multiple_of(pc * half, 16), half), :]

    def copy(a, k, chip, pc, to, src=None):
        dst = half_of(a, outs[a].at[2 * chip[0] + chip[1]], pc)
        return pltpu.make_async_remote_copy(src_ref=dst if src is None else src, dst_ref=dst, send_sem=send_sems.at[7 * a + k],
                                            recv_sem=recv_sems.at[7 * a + k], device_id=to, device_id_type=MESH)

    def own(a):
        return pltpu.make_async_remote_copy(src_ref=ins[a], dst_ref=outs[a].at[2 * x + y], send_sem=send_sems.at[7 * a + 6],
                                            recv_sem=recv_sems.at[7 * a + 6], device_id=sibling, device_id_type=MESH)

    def first():
        far = [copy(a, j, (x, y), c, (*chip, c), src=half_of(a, ins[a], c)) for a in range(n) for j, chip in enumerate(chips)]
        return far + [own(a) for a in range(n)]

    def passed():
        return [copy(a, 3 + j, chip, c, sibling) for j, chip in enumerate(chips) for a in range(n)]

    def send():
        for cp in first():
            cp.start()

    def forward():
        for j, chip in enumerate(chips):
            for a in range(n):
                copy(a, j, chip, c, sibling).wait_recv()
        for cp in passed():
            cp.start()

    def finish():
        for j, chip in enumerate(chips):
            for a in range(n):
                copy(a, 3 + j, chip, 1 - c, sibling).wait_recv()
        for a in range(n):
            own(a).wait_recv()
        for cp in first() + passed():
            cp.wait_send()

    return send, forward, finish


def _gather_sems(n):
    return [pltpu.SemaphoreType.DMA((7 * n,)), pltpu.SemaphoreType.DMA((7 * n,))]


def _allgather_list(name, shards):
    n = len(shards)

    def body(*refs):
        for stage in _gather_steps(refs[:n], refs[n:2 * n], *refs[2 * n:]):
            stage()

    return pl.pallas_call(
        body, name=name, out_shape=[_sds((N_SHARD,) + a.shape, a.dtype) for a in shards], in_specs=[HBM_SPEC] * n,
        out_specs=[HBM_SPEC] * n, scratch_shapes=_gather_sems(n),
    )(*shards)


def _swap_steps(ins, outs, send_sems, recv_sems):
    x, y, c = _place()

    def copies():
        out = []
        for a in range(len(ins)):
            h = ins[a].shape[1] // 2
            src = ins[a].at[:, pl.ds(pl.multiple_of((1 - c) * h, 8), h), :]
            out.append(pltpu.make_async_remote_copy(src_ref=src, dst_ref=outs[a], send_sem=send_sems.at[a], recv_sem=recv_sems.at[a],
                                                    device_id=(x, y, 1 - c), device_id_type=MESH))
        return out

    def start():
        for cp in copies():
            cp.start()

    def finish():
        for cp in copies():
            cp.wait()

    return start, finish


def _swap_sems(n):
    return [pltpu.SemaphoreType.DMA((n,)), pltpu.SemaphoreType.DMA((n,))]


def _halves_shapes(gs):
    return [_sds((N_SHARD, g.shape[1] // 2, g.shape[2]), g.dtype) for g in gs]


def _swap_halves(name, gs):
    n = len(gs)

    def body(*refs):
        for stage in _swap_steps(refs[:n], refs[n:2 * n], *refs[2 * n:]):
            stage()

    return pl.pallas_call(body, name=name, out_shape=_halves_shapes(gs), in_specs=[HBM_SPEC] * n, out_specs=[HBM_SPEC] * n,
                          scratch_shapes=_swap_sems(n))(*gs)


def _add_sibling(name, gs, gots, c_idx):
    n = len(gs)

    def kern(c_ref, *refs):
        for a in range(n):
            tot = refs[a][...] + refs[n + a][...]
            refs[2 * n + a][...] = tot
            refs[3 * n + a][...] = tot.astype(BF16)

    quarter = lambda g: (None, g.shape[1] // 4, g.shape[2])
    in_specs = [pl.BlockSpec(quarter(g), lambda b, s, c_ref: (b, 2 * c_ref[0] + s, 0)) for g in gs]
    in_specs += [pl.BlockSpec(quarter(g), lambda b, s, c_ref: (b, s, 0)) for g in gs]
    out_specs = [pl.BlockSpec(quarter(g), lambda b, s, c_ref: (b, s, 0)) for g in gs] * 2
    out_shape = [_sds(t.shape, F32) for t in gots] + [_sds(t.shape, BF16) for t in gots]
    outs = pl.pallas_call(
        kern, name=name, out_shape=out_shape,
        grid_spec=pltpu.PrefetchScalarGridSpec(num_scalar_prefetch=1, grid=(N_SHARD, 2), in_specs=in_specs, out_specs=out_specs),
        compiler_params=_params(2),
    )(c_idx.reshape(1), *gs, *gots)
    return outs[:n], outs[n:]


def _scatter_steps(ins, outs, send_sems, recv_sems):
    x, y, c = _place()

    def copies():
        return [pltpu.make_async_remote_copy(
            src_ref=ins[a].at[2 * px + py], dst_ref=outs[a].at[j], send_sem=send_sems.at[3 * a + j], recv_sem=recv_sems.at[3 * a + j],
            device_id=(px, py, c), device_id_type=MESH) for a in range(len(ins)) for j, (px, py) in enumerate(_other_chips(x, y))]

    def start():
        for cp in copies():
            cp.start()

    def finish():
        for cp in copies():
            cp.wait()

    return start, finish


def _scatter_sems(n):
    return [pltpu.SemaphoreType.DMA((3 * n,)), pltpu.SemaphoreType.DMA((3 * n,))]


def _chip_scatter(ps):
    n = len(ps)

    def body(*refs):
        for stage in _scatter_steps(refs[:n], refs[n:2 * n], *refs[2 * n:]):
            stage()

    return pl.pallas_call(
        body, name="chip_scatter", out_shape=[_sds((N_SHARD - 1,) + p.shape[1:], p.dtype) for p in ps], in_specs=[HBM_SPEC] * n,
        out_specs=[HBM_SPEC] * n, scratch_shapes=_scatter_sems(n),
    )(*ps)


def _add_chips(name, ps, others, shard_idx):
    n = len(ps)

    def kern(b_ref, *refs):
        for a in range(n):
            tot = refs[a][...]
            for j in range(N_SHARD - 1):
                tot = tot + refs[n + a][j].astype(F32)
            refs[2 * n + a][...] = tot

    in_specs = [pl.BlockSpec((None, p.shape[1] // 2, p.shape[2]), lambda s, b_ref: (b_ref[0], s, 0)) for p in ps]
    in_specs += [pl.BlockSpec((N_SHARD - 1, p.shape[1] // 2, p.shape[2]), lambda s, b_ref: (0, s, 0)) for p in ps]
    out_specs = [pl.BlockSpec((p.shape[1] // 2, p.shape[2]), lambda s, b_ref: (s, 0)) for p in ps]
    return pl.pallas_call(
        kern, name=name, out_shape=[_sds(p.shape[1:], F32) for p in ps],
        grid_spec=pltpu.PrefetchScalarGridSpec(num_scalar_prefetch=1, grid=(2,), in_specs=in_specs, out_specs=out_specs),
        compiler_params=_params(1),
    )(shard_idx.reshape(1), *ps, *others)


def _swap_result(name, mines):
    n = len(mines)

    def body(*refs):
        ins, outs = refs[:n], refs[n:2 * n]
        send_sems, recv_sems = refs[2 * n:]
        x, y, c = _place()
        copies = [pltpu.make_async_remote_copy(src_ref=ins[a], dst_ref=outs[a], send_sem=send_sems.at[a], recv_sem=recv_sems.at[a],
                                               device_id=(x, y, 1 - c), device_id_type=MESH) for a in range(n)]
        for cp in copies:
            cp.start()
        for cp in copies:
            cp.wait()

    return pl.pallas_call(
        body, name=name, out_shape=[_sds(m.shape, m.dtype) for m in mines], in_specs=[HBM_SPEC] * n,
        out_specs=[HBM_SPEC] * n, scratch_shapes=[pltpu.SemaphoreType.DMA((n,)), pltpu.SemaphoreType.DMA((n,))],
    )(*mines)


def _allreduce_small(v):
    m_per, n = v.shape

    def body(x_ref, tot_ref, all_ref, send_sems, recv_sems, local_sem):
        x, y, c = _place()
        me, sibling = (x, y, c), (x, y, 1 - c)
        chips = _other_chips(x, y)

        def rows(px, py, pc):
            return all_ref.at[pl.ds(pl.multiple_of((4 * px + 2 * py + pc) * m_per, 8), m_per), :]

        def copy(k, block, to, src=None):
            return pltpu.make_async_remote_copy(
                src_ref=rows(*block) if src is None else src, dst_ref=rows(*block), send_sem=send_sems.at[k],
                recv_sem=recv_sems.at[k], device_id=to, device_id_type=MESH)

        mine = pltpu.make_async_copy(x_ref, rows(*me), local_sem)
        mine.start()
        first = [copy(0, me, sibling, src=x_ref)] + [copy(1 + j, me, (*chip, c), src=x_ref) for j, chip in enumerate(chips)]
        for cp in first:
            cp.start()
        passed = [copy(4 + j, (*chip, c), sibling) for j, chip in enumerate(chips)]
        for j, chip in enumerate(chips):
            copy(1 + j, (*chip, c), me).wait_recv()
            passed[j].start()
        copy(0, sibling, me).wait_recv()
        for j, chip in enumerate(chips):
            copy(4 + j, (*chip, 1 - c), me).wait_recv()
        for cp in first + passed:
            cp.wait_send()
        mine.wait()
        tot = all_ref[0:m_per, :]
        for dev in range(1, 8):
            tot = tot + all_ref[dev * m_per:(dev + 1) * m_per, :]
        tot_ref[...] = tot

    vmem = pl.BlockSpec(memory_space=pltpu.VMEM)
    return pl.pallas_call(
        body, name="allreduce_small", out_shape=_sds((m_per, n), F32), in_specs=[vmem], out_specs=vmem,
        scratch_shapes=[pltpu.VMEM((8 * m_per, n), F32), pltpu.SemaphoreType.DMA((7,)), pltpu.SemaphoreType.DMA((7,)),
                        pltpu.SemaphoreType.DMA],
    )(v)


def _adam_update(w, g, m, v):
    m_new = ADAM_B1 * m + (1.0 - ADAM_B1) * g
    v_new = ADAM_B2 * v + (1.0 - ADAM_B2) * (g * g)
    m_hat = m_new / (1.0 - ADAM_B1 ** ADAM_STEP)
    v_hat = v_new / (1.0 - ADAM_B2 ** ADAM_STEP)
    return -ADAM_LR * (m_hat / (jnp.sqrt(v_hat) + ADAM_EPS) + ADAM_WD * w), m_new, v_new


def _adamw(name, w, g, m, v):
    rows, width = w.shape
    tr = rows // 4 if rows % 32 == 0 else rows

    def kern(w_ref, g_ref, m_ref, v_ref, d_ref, mo_ref, vo_ref):
        d_ref[...], mo_ref[...], vo_ref[...] = _adam_update(w_ref[...], g_ref[...], m_ref[...], v_ref[...])

    spec = pl.BlockSpec((tr, width), lambda i: (i, 0))
    return pl.pallas_call(kern, name=name, grid=(rows // tr,), in_specs=[spec] * 4, out_specs=[spec] * 3,
                          out_shape=[_sds((rows, width), F32)] * 3, compiler_params=_params(1))(w, g, m, v)


def _adamw_halves(ws, mines, theirs, ms, vs, c_idx):
    n = len(ws)

    def kern(c_ref, *refs):
        take_mine = pl.program_id(0) == c_ref[0]
        for a in range(n):
            w_ref, mine_ref, theirs_ref, m_ref, v_ref = refs[5 * a:5 * a + 5]
            g_ref, d_ref, mo_ref, vo_ref = refs[5 * n + 4 * a:5 * n + 4 * a + 4]
            g = jnp.where(take_mine, mine_ref[...], theirs_ref[...])
            g_ref[...] = g
            d_ref[...], mo_ref[...], vo_ref[...] = _adam_update(w_ref[...], g, m_ref[...], v_ref[...])

    in_specs, out_specs, out_shape = [], [], []
    for w in ws:
        rows, width = w.shape
        tr = rows // (2 * ADAM_STEPS)
        whole = pl.BlockSpec((tr, width), lambda h, j, c_ref: (ADAM_STEPS * h + j, 0))
        part = pl.BlockSpec((tr, width), lambda h, j, c_ref: (j, 0))
        in_specs += [whole, part, part, whole, whole]
        out_specs += [whole] * 4
        out_shape += [_sds((rows, width), F32)] * 4
    operands = [t for group in zip(ws, mines, theirs, ms, vs) for t in group]
    outs = pl.pallas_call(
        kern, name="adamw_shards", out_shape=out_shape,
        grid_spec=pltpu.PrefetchScalarGridSpec(num_scalar_prefetch=1, grid=(2, ADAM_STEPS), in_specs=in_specs, out_specs=out_specs),
        compiler_params=_params(2),
    )(c_idx.reshape(1), *operands)
    return [outs[4 * a:4 * a + 4] for a in range(n)]


SHARDED = (("w_in", D_MODEL, IN_WIDTH, 1), ("w_uq", Q_RANK, MLA_HEADS * MLA_QK, 1),
           ("w_ukv", KV_RANK, MLA_HEADS * (MLA_NOPE + MLA_V), 1), ("w_o", D_MODEL, D_MODEL, 0),
           ("w_gate", D_MODEL, D_FF, 1), ("w_up", D_MODEL, D_FF, 1), ("w_down", D_FF, D_MODEL, 0))
EARLY = ("w_in", "w_uq", "w_ukv")
LATE = ("w_o", "w_gate", "w_up", "w_down")
FLIPPED = ("w_gate", "w_up")
SMALL = (("norm_mix", D_MODEL), ("q_latent_norm", Q_RANK), ("kv_latent_norm", KV_RANK), ("out_norm_mla", MLA_WIDTH),
         ("out_norm_sb", SB_WIDTH), ("norm_ffn", D_MODEL), ("norm_final", D_MODEL))


def _full_weight(gathered, axis):
    n_sh, k, n = gathered.shape
    return gathered.transpose(1, 0, 2).reshape(k, n_sh * n) if axis == 1 else gathered.reshape(n_sh * k, n)


def _shard_major(g, axis):
    r, c = g.shape
    return g.reshape(r, N_SHARD, c // N_SHARD).transpose(1, 0, 2) if axis == 1 else g.reshape(N_SHARD, r // N_SHARD, c)


def _rot_cols(w):
    hh = MLA_ROPE // 2
    return jnp.concatenate([-w[..., hh:], w[..., :hh]], axis=-1)


def _rot_cols_t(g):
    hh = MLA_ROPE // 2
    return jnp.concatenate([g[..., hh:], -g[..., :hh]], axis=-1)


def _attention_weights(full, small):
    w_in = full["w_in"]
    s0, s1, s2 = Q_RANK, Q_RANK + KV_RANK, Q_RANK + KV_RANK + MLA_ROPE
    uq = full["w_uq"].reshape(Q_RANK, MLA_HEADS, MLA_QK)
    ukv = full["w_ukv"].reshape(KV_RANK, MLA_HEADS, MLA_NOPE + MLA_V)
    w_kr = w_in[:, s1:s2]
    per_tile = ROPE_TILE // MLA_ROPE
    w = {
        "w_cq": w_in[:, :s0], "w_ckv": w_in[:, s0:s1],
        "w_kr4": jnp.tile(w_kr, (1, per_tile)), "w_kr4r": jnp.tile(_rot_cols(w_kr), (1, per_tile)),
        "w_kr8": jnp.tile(w_kr, (1, MLA_HEADS)), "w_kr8r": jnp.tile(_rot_cols(w_kr), (1, MLA_HEADS)),
        "w_sbq": w_in[:, s2:s2 + SB_WIDTH], "w_sbk": w_in[:, s2 + SB_WIDTH:s2 + 2 * SB_WIDTH], "w_sbv": w_in[:, s2 + 2 * SB_WIDTH:],
        "w_qn": uq[..., :MLA_NOPE].reshape(Q_RANK, -1), "w_qr": uq[..., MLA_NOPE:].reshape(Q_RANK, -1),
        "w_qrr": _rot_cols(uq[..., MLA_NOPE:]).reshape(Q_RANK, -1),
        "w_kn": ukv[..., :MLA_NOPE].reshape(KV_RANK, -1), "w_v": ukv[..., MLA_NOPE:].reshape(KV_RANK, -1),
    }
    w.update(g_mix=small["norm_mix"], g_q=small["q_latent_norm"], g_kv=small["kv_latent_norm"], g_a=small["out_norm_mla"],
             g_b=small["out_norm_sb"], g_f=small["norm_ffn"], g_n=small["norm_final"])
    return w


def _ffn_weights(full):
    w = {"w_oa": full["w_o"][:MLA_WIDTH], "w_ob": full["w_o"][MLA_WIDTH:], "w_down": full["w_down"]}
    w.update({name + "_t": full[name] for name in FLIPPED})
    return w


def _rope_tables(positions):
    inv_freq = ROPE_THETA ** (-jnp.arange(0, MLA_ROPE, 2, dtype=F32) / MLA_ROPE)
    ang = positions.astype(F32)[:, None] * inv_freq[None, :]
    cos, sin = jnp.cos(ang), jnp.sin(ang)
    return {"cos": jnp.tile(jnp.concatenate([cos, cos], axis=1), (1, MLA_HEADS)),
            "sin": jnp.tile(jnp.concatenate([sin, sin], axis=1), (1, MLA_HEADS))}


def _by_head(g_wide, g_narrow, wide, narrow):
    r = g_wide.shape[0]
    return jnp.concatenate([g_wide.reshape(r, MLA_HEADS, wide), g_narrow.reshape(r, MLA_HEADS, narrow)], axis=-1).reshape(r, -1)


def kernel(x, positions, norm_mix, w_in, q_latent_norm, w_uq, kv_latent_norm, w_ukv, out_norm_mla, out_norm_sb, w_o, norm_ffn, w_gate, w_up, w_down, norm_final, loss_target, m_norm_mix, m_w_in, m_q_latent_norm, m_w_uq, m_kv_latent_norm, m_w_ukv, m_out_norm_mla, m_out_norm_sb, m_w_o, m_norm_ffn, m_w_gate, m_w_up, m_w_down, m_norm_final, v_norm_mix, v_w_in, v_q_latent_norm, v_w_uq, v_kv_latent_norm, v_w_ukv, v_out_norm_mla, v_out_norm_sb, v_w_o, v_norm_ffn, v_w_gate, v_w_up, v_w_down, v_norm_final):
    given = dict(norm_mix=norm_mix, w_in=w_in, q_latent_norm=q_latent_norm, w_uq=w_uq, kv_latent_norm=kv_latent_norm, w_ukv=w_ukv,
                 out_norm_mla=out_norm_mla, out_norm_sb=out_norm_sb, w_o=w_o, norm_ffn=norm_ffn, w_gate=w_gate, w_up=w_up,
                 w_down=w_down, norm_final=norm_final)
    mom_m = dict(norm_mix=m_norm_mix, w_in=m_w_in, q_latent_norm=m_q_latent_norm, w_uq=m_w_uq, kv_latent_norm=m_kv_latent_norm,
                 w_ukv=m_w_ukv, out_norm_mla=m_out_norm_mla, out_norm_sb=m_out_norm_sb, w_o=m_w_o, norm_ffn=m_norm_ffn,
                 w_gate=m_w_gate, w_up=m_w_up, w_down=m_w_down, norm_final=m_norm_final)
    mom_v = dict(norm_mix=v_norm_mix, w_in=v_w_in, q_latent_norm=v_q_latent_norm, w_uq=v_w_uq, kv_latent_norm=v_kv_latent_norm,
                 w_ukv=v_w_ukv, out_norm_mla=v_out_norm_mla, out_norm_sb=v_out_norm_sb, w_o=v_w_o, norm_ffn=v_norm_ffn,
                 w_gate=v_w_gate, w_up=v_w_up, w_down=v_w_down, norm_final=v_norm_final)
    xs = x[0]
    tgt = loss_target[0]
    s = xs.shape[0]
    c_idx = lax.axis_index("c")
    shard_idx = 2 * lax.axis_index("x") + lax.axis_index("y")

    def block2d(t, name):
        t = t.reshape(t.shape[-2:])
        return t.T if name in FLIPPED else t

    shard2d = {name: block2d(given[name], name) for name, *_ in SHARDED}
    local = {name: shard2d[name].astype(BF16) for name, *_ in SHARDED}
    axis_of = {name: 0 if name in FLIPPED else axis for name, _, _, axis in SHARDED}

    def whole(names, gathered):
        return {name: _full_weight(t, axis_of[name]) for name, t in zip(names, gathered)}

    small = {name: given[name].reshape(1, n) for name, n in SMALL}
    w = _attention_weights(whole(EARLY, _allgather_list("allgather_w", [local[name] for name in EARLY])), small)
    tabs = _rope_tables(positions[0])
    msuf, mpre = _sb_masks(min(SB_TK, s))

    u, cq, ckv, cqn, ckvn, qn, qr, kn, vm, kr, sq, sk, sv = _fwd_a(xs, tabs, w)
    o_mla, lse, late = _mla_fwd(qn, qr, kn, kr, vm, [local[name] for name in LATE])
    w.update(_ffn_weights(whole(LATE, late)))
    o_sb, cmat = _sb_fwd(sq, sk, sv, msuf)
    merged, h1, f, gate, up, act = _fwd_b1(xs, o_mla, o_sb, w)
    dh2, loss_part, dg_n = _fwd_b2(h1, act, tgt, w)

    def shards_of(names, grads):
        return [_shard_major(grads[name], axis_of[name]) for name in names]

    def reduced(tag, chip_f32, others):
        mine = _add_chips("add_chips_" + tag, chip_f32, others, shard_idx)
        return tuple(mine), tuple(_swap_result("swap_result_" + tag, mine))

    dgate, dup, dh1, do_mla, do_sb, dg_f, dg_a, dg_b = _bwd_b(dh2, gate, up, h1, o_mla, o_sb, w)
    late_gs = shards_of(LATE, {
        "w_o": _tn_matmul("dw_o", merged, dh1), "w_gate": _tn_matmul("dw_gate", dgate, f),
        "w_up": _tn_matmul("dw_up", dup, f), "w_down": _tn_matmul("dw_down", act, dh2)})
    (dsq, dsk, dsv), late_got = _sb_bwd(sq, sk, sv, do_sb, cmat, msuf, mpre, late_gs)
    late_f32, late_bf16 = _add_sibling("add_sibling_late", late_gs, late_got, c_idx)
    (dqn, dqr, dkn, dkr, dvm), late_others = _mla_bwd(qn, qr, kn, kr, vm, o_mla, do_mla, lse, late_bf16)
    mine_late, theirs_late = reduced("late", late_f32, late_others)
    dx, a1, a2, dcq, dckv, dkrc, dkrs, dg_q, dg_kv, dg_mix = _bwd_a(xs, dh1, cq, ckv, dqn, dqr, dkn, dvm, dkr, dsq, dsk, dsv, tabs, w)

    g_cq, g_ckv, g_krc, g_krs, g_sq, g_sk, g_sv = _tn_multi("dw_in", u, [dcq, dckv, dkrc, dkrs, dsq, dsk, dsv])
    g_qn, g_qr1, g_qr2 = _tn_multi("dw_uq", cqn, [dqn, a1, a2])
    g_kn, g_v = _tn_multi("dw_ukv", ckvn, [dkn, dvm])
    slots = lambda g: g.reshape(g.shape[0], MLA_HEADS, MLA_ROPE)
    g_kr = jnp.sum(slots(g_krc), axis=1) + _rot_cols_t(jnp.sum(slots(g_krs), axis=1))
    g_qr = (slots(g_qr1) + _rot_cols_t(slots(g_qr2))).reshape(Q_RANK, -1)
    early_gs = shards_of(EARLY, {
        "w_in": jnp.concatenate([g_cq, g_ckv, g_kr, g_sq, g_sk, g_sv], axis=1),
        "w_uq": _by_head(g_qn, g_qr, MLA_NOPE, MLA_ROPE),
        "w_ukv": _by_head(g_kn, g_v, MLA_NOPE, MLA_V)})
    early_f32, early_bf16 = _add_sibling("add_sibling_early", early_gs, _swap_halves("swap_halves_early", early_gs), c_idx)
    mine_early, theirs_early = reduced("early", early_f32, _chip_scatter(early_bf16))
    halves = dict(zip(EARLY + LATE, zip(mine_early + mine_late, theirs_early + theirs_late)))

    small_parts = jnp.concatenate([dg_mix, dg_q, dg_kv, dg_a, dg_b, dg_f, dg_n, loss_part], axis=1)
    small_sum = _allreduce_small(jnp.broadcast_to(small_parts, (8, small_parts.shape[1])))
    small_g, loss = small_sum[0:1, :-LANES], small_sum[0, -LANES]

    g_out, d_out, m_out, v_out = {}, {}, {}, {}
    names = [name for name, *_ in SHARDED]
    updated = _adamw_halves([shard2d[name] for name in names], [halves[name][0] for name in names], [halves[name][1] for name in names],
                            [block2d(mom_m[name], name) for name in names], [block2d(mom_v[name], name) for name in names], c_idx)
    for name, outs in zip(names, updated):
        shape = given[name].shape
        g_out[name], d_out[name], m_out[name], v_out[name] = ((t.T if name in FLIPPED else t).reshape(shape) for t in outs)
    cat = lambda src: jnp.concatenate([src[name].reshape(1, n) for name, n in SMALL], axis=1)
    d, mn, vn = _adamw("adamw_small", cat(given), small_g, cat(mom_m), cat(mom_v))
    off = 0
    for name, n in SMALL:
        shape = given[name].shape
        g_out[name], d_out[name], m_out[name], v_out[name] = (t[:, off:off + n].reshape(shape) for t in (small_g, d, mn, vn))
        off += n

    order = ["norm_mix", "w_in", "q_latent_norm", "w_uq", "kv_latent_norm", "w_ukv", "out_norm_mla", "out_norm_sb", "w_o",
             "norm_ffn", "w_gate", "w_up", "w_down", "norm_final"]
    return (loss, dx[None], *[g_out[n] for n in order], *[d_out[n] for n in order], *[m_out[n] for n in order],
            *[v_out[n] for n in order])
```

```python
import functools
import math

import jax
import jax.numpy as jnp
from jax import lax
from jax.experimental import pallas as pl
from jax.experimental.pallas import tpu as pltpu

F32 = jnp.float32
BF16 = jnp.bfloat16
MESH = pl.DeviceIdType.MESH

D_MODEL = 1024
EPS = 1e-6
MLA_HEADS = 8
MLA_NOPE = 64
MLA_ROPE = 32
MLA_V = 64
MLA_QK = MLA_NOPE + MLA_ROPE
Q_RANK = 256
KV_RANK = 128
ROPE_THETA = 10000.0
SB_HEADS = 8
SB_DIM = 64
MLA_WIDTH = MLA_HEADS * MLA_V
SB_WIDTH = SB_HEADS * SB_DIM
D_FF = 2816
IN_WIDTH = Q_RANK + KV_RANK + MLA_ROPE + 3 * SB_WIDTH

ADAM_LR = 0.001
ADAM_B1 = 0.9
ADAM_B2 = 0.999
ADAM_EPS = 1e-08
ADAM_WD = 0.01
ADAM_STEP = 10

N_SHARD = 4
LANES = 128
ROPE_TILE = LANES
VMEM_LIMIT = 56 * 1024 * 1024
TN_ACC_BYTES = 6 * 1024 * 1024 + 512 * 1024
NEG = -1e30
MLA_SCALE = 1.0 / math.sqrt(MLA_QK)
MLA_DK_SCALE = math.log(2.0)
MLA_QSCALE = MLA_SCALE * math.log2(math.e)
SB_SKIP = 110.0

ROW_TILE = 512
ROW_TILE_ELEMENTWISE = 256
MLA_TQ = 1024
SB_TQ = 512
MLA_TK = 1024
MLA_BWD_TK = 1024
MLA_DIAG_TK = 256
SB_TK = 256
TN_TS = 2048
ADAM_STEPS = 4


def _dot(a, b):
    return jnp.dot(a, b, preferred_element_type=F32)


def _dot_nt(a, b):
    return lax.dot_general(a, b, (((1,), (1,)), ((), ())), preferred_element_type=F32)


def _dot_tn(a, b):
    return lax.dot_general(a, b, (((0,), (0,)), ((), ())), preferred_element_type=F32)


def _params(n_grid, vmem=VMEM_LIMIT):
    return pltpu.CompilerParams(dimension_semantics=("arbitrary",) * n_grid, vmem_limit_bytes=vmem)


def _rms(x):
    r = lax.rsqrt(jnp.mean(x * x, axis=-1, keepdims=True) + EPS)
    return x * r, r


def _rms_bwd(n, r, g, dy):
    dn = dy * g
    dx = r * (dn - n * jnp.mean(dn * n, axis=-1, keepdims=True))
    return dx, jnp.sum(dy * n, axis=0, keepdims=True)


def _accumulate(ref, val, step):
    @pl.when(step == 0)
    def _():
        ref[...] = val

    @pl.when(step != 0)
    def _():
        ref[...] += val


def _rowwise(name, body, rows, consts, row_out, acc_out, tm):
    n_rows = rows[0].shape[0]
    tm = min(tm, n_rows)
    nr, nc, no = len(rows), len(consts), len(row_out)

    def kern(*refs):
        body(refs[:nr], refs[nr:nr + nc], refs[nr + nc:nr + nc + no], refs[nr + nc + no:], pl.program_id(0))

    in_specs = [pl.BlockSpec((tm, a.shape[1]), lambda i: (i, 0)) for a in rows]
    in_specs += [pl.BlockSpec(a.shape, lambda i: (0, 0), pipeline_mode=pl.Buffered(1)) for a in consts]
    out_specs = [pl.BlockSpec((tm, s.shape[1]), lambda i: (i, 0)) for s in row_out]
    out_specs += [pl.BlockSpec(s.shape, lambda i: (0, 0)) for s in acc_out]
    return pl.pallas_call(
        kern, name=name, grid=(n_rows // tm,), in_specs=in_specs, out_specs=out_specs,
        out_shape=list(row_out) + list(acc_out), compiler_params=_params(1),
    )(*rows, *consts)


def _sds(shape, dtype):
    return jax.ShapeDtypeStruct(shape, dtype)


def _fwd_a(x, tabs, w):
    s = x.shape[0]

    def body(r, c, o, a, step):
        x_ref, cos_ref, sin_ref = r
        gmix, wcq, wckv, wkr, wkrr, wsq, wsk, wsv, gq, wqn, wqr, wqrr, gkv, wkn, wv = c
        u_o, cq_o, ckv_o, cqn_o, ckvn_o, qn_o, qr_o, kn_o, v_o, kr_o, sq_o, sk_o, sv_o = o
        cos, sin = cos_ref[...], sin_ref[...]
        n, _ = _rms(x_ref[...])
        u = (n * gmix[...]).astype(BF16)
        u_o[...] = u
        cq = _dot(u, wcq[...])
        ckv = _dot(u, wckv[...])
        kr_o[...] = (_dot(u, wkr[...]) * cos[:, :ROPE_TILE] + _dot(u, wkrr[...]) * sin[:, :ROPE_TILE]).astype(BF16)
        sq_o[...] = _dot(u, wsq[...]).astype(BF16)
        sk_o[...] = _dot(u, wsk[...]).astype(BF16)
        sv_o[...] = _dot(u, wsv[...]).astype(BF16)
        cq_o[...] = cq
        ckv_o[...] = ckv
        nq, _ = _rms(cq)
        cqn = (nq * gq[...]).astype(BF16)
        cqn_o[...] = cqn
        qn_o[...] = (_dot(cqn, wqn[...]) * MLA_QSCALE).astype(BF16)
        qr_o[...] = ((_dot(cqn, wqr[...]) * cos + _dot(cqn, wqrr[...]) * sin) * MLA_QSCALE).astype(BF16)
        nkv, _ = _rms(ckv)
        ckvn = (nkv * gkv[...]).astype(BF16)
        ckvn_o[...] = ckvn
        kn_o[...] = _dot(ckvn, wkn[...]).astype(BF16)
        v_o[...] = _dot(ckvn, wv[...]).astype(BF16)

    outs = [
        _sds((s, D_MODEL), BF16), _sds((s, Q_RANK), F32), _sds((s, KV_RANK), F32), _sds((s, Q_RANK), BF16),
        _sds((s, KV_RANK), BF16), _sds((s, MLA_HEADS * MLA_NOPE), BF16), _sds((s, MLA_HEADS * MLA_ROPE), BF16),
        _sds((s, MLA_HEADS * MLA_NOPE), BF16), _sds((s, MLA_WIDTH), BF16), _sds((s, ROPE_TILE), BF16),
        _sds((s, SB_WIDTH), BF16), _sds((s, SB_WIDTH), BF16), _sds((s, SB_WIDTH), BF16),
    ]
    consts = [w["g_mix"], w["w_cq"], w["w_ckv"], w["w_kr4"], w["w_kr4r"], w["w_sbq"], w["w_sbk"], w["w_sbv"], w["g_q"],
              w["w_qn"], w["w_qr"], w["w_qrr"], w["g_kv"], w["w_kn"], w["w_v"]]
    return _rowwise("fwd_a", body, [x, tabs["cos"], tabs["sin"]], consts, outs, [], ROW_TILE)


def _fwd_b1(x, o_mla, o_sb, w):
    s = x.shape[0]

    def body(r, c, o, a, step):
        x_ref, oa_ref, ob_ref = r
        ga, gb, woa, wob, gf, wg, wu = c
        mg_o, h1_o, f_o, gate_o, up_o, act_o = o
        na, _ = _rms(oa_ref[...])
        nb, _ = _rms(ob_ref[...])
        ma = (na * ga[...]).astype(BF16)
        mb = (nb * gb[...]).astype(BF16)
        mg_o[:, :MLA_WIDTH] = ma
        mg_o[:, MLA_WIDTH:] = mb
        h1 = x_ref[...] + _dot(ma, woa[...]) + _dot(mb, wob[...])
        h1_o[...] = h1
        nf, _ = _rms(h1)
        f = (nf * gf[...]).astype(BF16)
        f_o[...] = f
        gate = _dot_nt(f, wg[...])
        up = _dot_nt(f, wu[...])
        gate_o[...] = gate.astype(BF16)
        up_o[...] = up.astype(BF16)
        act_o[...] = (gate * (1.0 / (1.0 + jnp.exp(-gate))) * up).astype(BF16)

    outs = [_sds((s, D_MODEL), BF16), _sds((s, D_MODEL), F32), _sds((s, D_MODEL), BF16), _sds((s, D_FF), BF16),
            _sds((s, D_FF), BF16), _sds((s, D_FF), BF16)]
    consts = [w["g_a"], w["g_b"], w["w_oa"], w["w_ob"], w["g_f"], w["w_gate_t"], w["w_up_t"]]
    return _rowwise("fwd_b1", body, [x, o_mla, o_sb], consts, outs, [], ROW_TILE)


def _fwd_b2(h1, act, tgt, w):
    s = h1.shape[0]

    def body(r, c, o, a, step):
        h1_ref, act_ref, t_ref = r
        wd, gn = c
        (dh2_o,) = o
        loss_o, dgn_o = a
        h2 = h1_ref[...] + _dot(act_ref[...], wd[...])
        n2, r2 = _rms(h2)
        err = n2 * gn[...] - t_ref[...]
        part = jnp.sum(jnp.sum(err * err, axis=1, keepdims=True), axis=0, keepdims=True) * (0.5 / D_MODEL)
        _accumulate(loss_o, jnp.broadcast_to(part, (1, LANES)), step)
        dh2, dgn = _rms_bwd(n2, r2, gn[...], err * (1.0 / D_MODEL))
        dh2_o[...] = dh2
        _accumulate(dgn_o, dgn, step)

    return _rowwise("fwd_b2", body, [h1, act, tgt], [w["w_down"], w["g_n"]], [_sds((s, D_MODEL), F32)],
                    [_sds((1, LANES), F32), _sds((1, D_MODEL), F32)], ROW_TILE)


def _bwd_b(dh2, gate, up, h1, o_mla, o_sb, w):
    s = h1.shape[0]

    def body(r, c, o, a, step):
        dh2_ref, gate_ref, up_ref, h1_ref, oa_ref, ob_ref = r
        wd, wgt, wut, gf, woa, wob, ga, gb = c
        dgate_o, dup_o, dh1_o, doa_o, dob_o = o
        dgf_o, dga_o, dgb_o = a
        dh2 = dh2_ref[...]
        dact = _dot_nt(dh2.astype(BF16), wd[...])
        gate = gate_ref[...].astype(F32)
        sig = 1.0 / (1.0 + jnp.exp(-gate))
        dup = (dact * (gate * sig)).astype(BF16)
        dgate = (dact * up_ref[...].astype(F32) * (sig * (1.0 + gate * (1.0 - sig)))).astype(BF16)
        dup_o[...] = dup
        dgate_o[...] = dgate
        df = _dot(dgate, wgt[...]) + _dot(dup, wut[...])
        nf, rf = _rms(h1_ref[...])
        dres, dgf = _rms_bwd(nf, rf, gf[...], df)
        dh1 = dh2 + dres
        dh1_o[...] = dh1
        dh1b = dh1.astype(BF16)
        na, ra = _rms(oa_ref[...])
        doa, dga = _rms_bwd(na, ra, ga[...], _dot_nt(dh1b, woa[...]))
        nb, rb = _rms(ob_ref[...])
        dob, dgb = _rms_bwd(nb, rb, gb[...], _dot_nt(dh1b, wob[...]))
        doa_o[...] = doa
        dob_o[...] = dob
        _accumulate(dgf_o, dgf, step)
        _accumulate(dga_o, dga, step)
        _accumulate(dgb_o, dgb, step)

    consts = [w["w_down"], w["w_gate_t"], w["w_up_t"], w["g_f"], w["w_oa"], w["w_ob"], w["g_a"], w["g_b"]]
    outs = [_sds((s, D_FF), BF16), _sds((s, D_FF), BF16), _sds((s, D_MODEL), F32), _sds((s, MLA_WIDTH), F32), _sds((s, SB_WIDTH), F32)]
    accs = [_sds((1, D_MODEL), F32), _sds((1, MLA_WIDTH), F32), _sds((1, SB_WIDTH), F32)]
    return _rowwise("bwd_b", body, [dh2, gate, up, h1, o_mla, o_sb], consts, outs, accs, ROW_TILE_ELEMENTWISE)


def _fold_pairs(t):
    return jnp.concatenate([t[:, :LANES] + t[:, LANES:2 * LANES], t[:, 2 * LANES:3 * LANES] + t[:, 3 * LANES:]], axis=1)


def _bwd_a(x, dh1, cq, ckv, dqn, dqr, dkn, dvm, dkr, dsq, dsk, dsv, tabs, w):
    s = x.shape[0]

    def body(r, c, o, a, step):
        x_ref, dh1_ref, cq_ref, ckv_ref, dqn_ref, dqr_ref, dkn_ref, dvm_ref, dkr_ref, dsq_ref, dsk_ref, dsv_ref, cos_ref, sin_ref = r
        wqn, wqr, wqrr, gq, wkn, wv, gkv, wcq, wckv, wkr, wkrr, wsq, wsk, wsv, gmix = c
        dx_o, a1_o, a2_o, dcq_o, dckv_o, dkrc_o, dkrs_o = o
        dgq_o, dgkv_o, dgmix_o = a
        cos, sin = cos_ref[...], sin_ref[...]
        dqr = _fold_pairs(dqr_ref[...].astype(F32))
        a1 = (dqr * cos).astype(BF16)
        a2 = (dqr * sin).astype(BF16)
        a1_o[...] = a1
        a2_o[...] = a2
        nq, rq = _rms(cq_ref[...])
        dcqn = _dot_nt(dqn_ref[...], wqn[...]) + _dot_nt(a1, wqr[...]) + _dot_nt(a2, wqrr[...])
        dcq, dgq = _rms_bwd(nq, rq, gq[...], dcqn)
        nkv, rkv = _rms(ckv_ref[...])
        dckvn = _dot_nt(dkn_ref[...], wkn[...]) + _dot_nt(dvm_ref[...], wv[...])
        dckv, dgkv = _rms_bwd(nkv, rkv, gkv[...], dckvn)
        dkr = _fold_pairs(dkr_ref[...].astype(F32))
        dcq_b = dcq.astype(BF16)
        dckv_b = dckv.astype(BF16)
        dkrc = (dkr * cos).astype(BF16)
        dkrs = (dkr * sin).astype(BF16)
        dcq_o[...] = dcq_b
        dckv_o[...] = dckv_b
        dkrc_o[...] = dkrc
        dkrs_o[...] = dkrs
        du = (_dot_nt(dcq_b, wcq[...]) + _dot_nt(dckv_b, wckv[...]) + _dot_nt(dkrc, wkr[...]) + _dot_nt(dkrs, wkrr[...])
              + _dot_nt(dsq_ref[...], wsq[...]) + _dot_nt(dsk_ref[...], wsk[...]) + _dot_nt(dsv_ref[...], wsv[...]))
        nx, rx = _rms(x_ref[...])
        dres, dgmix = _rms_bwd(nx, rx, gmix[...], du)
        dx_o[...] = dh1_ref[...] + dres
        _accumulate(dgq_o, dgq, step)
        _accumulate(dgkv_o, dgkv, step)
        _accumulate(dgmix_o, dgmix, step)

    consts = [w["w_qn"], w["w_qr"], w["w_qrr"], w["g_q"], w["w_kn"], w["w_v"], w["g_kv"], w["w_cq"], w["w_ckv"],
              w["w_kr8"], w["w_kr8r"], w["w_sbq"], w["w_sbk"], w["w_sbv"], w["g_mix"]]
    rope_w = MLA_HEADS * MLA_ROPE
    outs = [_sds((s, D_MODEL), F32), _sds((s, rope_w), BF16), _sds((s, rope_w), BF16), _sds((s, Q_RANK), BF16),
            _sds((s, KV_RANK), BF16), _sds((s, rope_w), BF16), _sds((s, rope_w), BF16)]
    accs = [_sds((1, Q_RANK), F32), _sds((1, KV_RANK), F32), _sds((1, D_MODEL), F32)]
    rows = [x, dh1, cq, ckv, dqn, dqr, dkn, dvm, dkr, dsq, dsk, dsv, tabs["cos"], tabs["sin"]]
    return _rowwise("bwd_a", body, rows, consts, outs, accs, ROW_TILE)


def _tn_multi(name, x, ys):
    s, k = x.shape
    ts = min(TN_TS, s)
    n_y = len(ys)

    def kern(*refs):
        step = pl.program_id(0)
        xb = refs[0][...].astype(BF16)
        for j in range(n_y):
            _accumulate(refs[1 + n_y + j], _dot_tn(xb, refs[1 + j][...].astype(BF16)), step)

    return pl.pallas_call(
        kern, name=name, grid=(s // ts,),
        in_specs=[pl.BlockSpec((ts, k), lambda i: (i, 0))] + [pl.BlockSpec((ts, y.shape[1]), lambda i: (i, 0)) for y in ys],
        out_specs=[pl.BlockSpec((k, y.shape[1]), lambda i: (0, 0)) for y in ys],
        out_shape=[_sds((k, y.shape[1]), F32) for y in ys], compiler_params=_params(1),
    )(x, *ys)


def _tn_tile(k, n):
    if n % LANES or k * n * 4 <= TN_ACC_BYTES:
        return n
    units = n // LANES
    best = 1
    for d in range(1, units + 1):
        if units % d == 0 and k * d * LANES * 4 <= TN_ACC_BYTES:
            best = d
    return best * LANES


def _tn_matmul(name, x, y):
    s, k = x.shape
    n = y.shape[1]
    ts = min(TN_TS, s)
    tn = _tn_tile(k, n)

    def kern(x_ref, y_ref, o_ref):
        step = pl.program_id(1)
        _accumulate(o_ref, _dot_tn(x_ref[...].astype(BF16), y_ref[...].astype(BF16)), step)

    return pl.pallas_call(
        kern, name=name, grid=(n // tn, s // ts),
        in_specs=[pl.BlockSpec((ts, k), lambda j, i: (i, 0)), pl.BlockSpec((ts, tn), lambda j, i: (i, j))],
        out_specs=pl.BlockSpec((k, tn), lambda j, i: (0, j)), out_shape=_sds((k, n), F32), compiler_params=_params(2),
    )(x, y)


def _lanes(rows, lo, width):
    lane = lax.broadcasted_iota(jnp.int32, (rows, LANES), 1)
    return jnp.logical_and(lane >= lo, lane < lo + width)


def _keep(mask, t):
    return jnp.where(mask, t, jnp.zeros_like(t))


def _mla_qcat(qn_ref, qr_ref, rope_lo, half, rows):
    qn = _keep(_lanes(rows, MLA_NOPE * half, MLA_NOPE), qn_ref[...])
    qr = _keep(_lanes(rows, rope_lo, MLA_ROPE), qr_ref[...])
    return jnp.concatenate([qn, qr], axis=1)


def _diag_mask(rows, width, row0, col0):
    row = lax.broadcasted_iota(jnp.int32, (rows, width), 0)
    col = lax.broadcasted_iota(jnp.int32, (rows, width), 1)
    return col + (col0 - row0) <= row


def _mla_fwd(qn, qr, kn, kr, v, riders=()):
    s = qn.shape[0]
    tq, tk = min(MLA_TQ, s), min(MLA_TK, s)
    td = tq
    ratio = tq // tk

    n_ride = len(riders)
    n_pairs = MLA_HEADS // 2

    def kern(qn_ref, qr_ref, kn_ref, kr_ref, v_ref, *rest):
        o_ref, lse_ref = rest[n_ride:n_ride + 2]
        g = pl.program_id(0)
        i = pl.program_id(1)
        if n_ride:
            send, forward, finish = _gather_steps(rest[:n_ride], rest[n_ride + 2:2 * n_ride + 2], *rest[2 * n_ride + 2:])
            pl.when(jnp.logical_and(g == 0, i == 0))(send)
            pl.when(jnp.logical_and(g == 1, i == 0))(forward)
        qcat = [_mla_qcat(qn_ref, qr_ref, MLA_ROPE * (2 * (g % 2) + half), half, tq) for half in range(2)]

        def block(k0, width, carry, row0, masked, half):
            m, l, acc = (c[row0:] for c in carry)
            ks = pl.ds(pl.multiple_of(k0, width), width)
            kcat = jnp.concatenate([kn_ref[ks, :], kr_ref[ks, :]], axis=1)
            sc = _dot_nt(qcat[half][row0:], kcat)
            if masked:
                sc = jnp.where(_diag_mask(tq - row0, width, row0, row0), sc, NEG)
            m_new = jnp.maximum(m, jnp.max(sc, axis=1, keepdims=True))
            p = jnp.exp2(sc - m_new)
            alpha = jnp.exp2(m - m_new)
            l = alpha * l + jnp.sum(p, axis=1, keepdims=True)
            acc = alpha * acc + _dot(p.astype(BF16), v_ref[ks, :])
            new = (m_new, l, acc)
            return new if row0 == 0 else tuple(jnp.concatenate([c[:row0], n], axis=0) for c, n in zip(carry, new))

        def both(k0, width, carries, row0, masked):
            return tuple(block(k0, width, carries[half], row0, masked, half) for half in range(2))

        init = (jnp.full((tq, 1), NEG, F32), jnp.zeros((tq, 1), F32), jnp.zeros((tq, LANES), F32))
        carries = lax.fori_loop(0, i * ratio, lambda kb, c: both(kb * tk, tk, c, 0, False), (init, init))
        for row0 in range(0, tq, td):
            carries = both(i * tq + row0, td, carries, row0, True)
        for half in range(2):
            m, l, acc = carries[half]
            out = _keep(_lanes(tq, MLA_V * half, MLA_V), acc / l)
            lse = _keep(_lanes(tq, MLA_ROPE * half, MLA_ROPE), jnp.broadcast_to(m + jnp.log2(l), (tq, LANES)))
            if half == 0:
                o_ref[...] = out
                lse_ref[...] = lse
            else:
                o_ref[...] += out
                lse_ref[...] += lse
        if n_ride:
            pl.when(jnp.logical_and(g == n_pairs - 1, i == s // tq - 1))(finish)

    qblk = pl.BlockSpec((tq, LANES), lambda g, i: (i, g))
    full = pl.BlockSpec((s, LANES), lambda g, i: (0, g))
    outs = pl.pallas_call(
        kern, name="mla_fwd", grid=(n_pairs, s // tq),
        in_specs=[qblk, pl.BlockSpec((tq, LANES), lambda g, i: (i, g // 2)), full, pl.BlockSpec((s, LANES), lambda g, i: (0, 0)), full]
        + [HBM_SPEC] * n_ride,
        out_specs=[qblk, qblk] + [HBM_SPEC] * n_ride,
        out_shape=[_sds((s, MLA_WIDTH), F32), _sds((s, n_pairs * LANES), F32)] + [_sds((N_SHARD,) + a.shape, a.dtype) for a in riders],
        scratch_shapes=_gather_sems(n_ride) if n_ride else [], compiler_params=_params(2),
    )(qn, qr, kn, kr, v, *riders)
    return outs[0], outs[1], outs[2:]


def _mla_bwd(qn, qr, kn, kr, v, o, do, lse, riders=()):
    s = qn.shape[0]
    tq, tk, td = min(MLA_TQ, s), min(MLA_BWD_TK, s), min(MLA_DIAG_TK, s)
    ratio = tq // tk

    n_ride = len(riders)
    n_pairs = MLA_HEADS // 2

    def kern(qn_ref, qr_ref, kn_ref, kr_ref, v_ref, o_ref, do_ref, lse_ref, *rest):
        dqn_ref, dqr_ref, dkn_out, dkr_out, dv_out = rest[n_ride:n_ride + 5]
        dkn_ref, dkr_ref, dv_ref = rest[2 * n_ride + 5:2 * n_ride + 8]
        g = pl.program_id(0)
        i = pl.program_id(1)
        if n_ride:
            start, finish = _scatter_steps(rest[:n_ride], rest[n_ride + 5:2 * n_ride + 5], *rest[2 * n_ride + 8:])
            pl.when(jnp.logical_and(g == 0, i == 0))(start)

        @pl.when(i == 0)
        def _():
            dkn_ref[...] = jnp.zeros_like(dkn_ref)
            dkr_ref[...] = jnp.zeros_like(dkr_ref)
            dv_ref[...] = jnp.zeros_like(dv_ref)

        for half in range(2):
            rope_lo = MLA_ROPE * (2 * (g % 2) + half)
            qcat = _mla_qcat(qn_ref, qr_ref, rope_lo, half, tq)
            mine = _lanes(tq, MLA_V * half, MLA_V)
            do_f = _keep(mine, do_ref[...])
            do_b = do_f.astype(BF16)
            delta = jnp.sum(do_f * o_ref[...], axis=1, keepdims=True)
            lse_v = lse_ref[:, MLA_ROPE * half:MLA_ROPE * half + 1]

            def block(k0, width, dq_acc, row0, masked, qcat=qcat, do_b=do_b, delta=delta, lse_v=lse_v):
                ks = pl.ds(pl.multiple_of(k0, width), width)
                kcat = jnp.concatenate([kn_ref[ks, :], kr_ref[ks, :]], axis=1)
                qc, dob = qcat[row0:], do_b[row0:]
                p = jnp.exp2(_dot_nt(qc, kcat) - lse_v[row0:])
                if masked:
                    p = jnp.where(_diag_mask(tq - row0, width, row0, row0), p, 0.0)
                ds = (p * (_dot_nt(dob, v_ref[ks, :]) - delta[row0:])).astype(BF16)
                dv_ref[ks, :] += _dot_tn(p.astype(BF16), dob)
                dkc = _dot_tn(ds, qc)
                dkn_ref[ks, :] += dkc[:, :LANES]
                dkr_ref[ks, :] += dkc[:, LANES:]
                new = dq_acc[row0:] + _dot(ds, kcat)
                return new if row0 == 0 else jnp.concatenate([dq_acc[:row0], new], axis=0)

            acc = lax.fori_loop(0, i * ratio, lambda kb, c, block=block: block(kb * tk, tk, c, 0, False),
                                jnp.zeros((tq, 2 * LANES), F32))
            for row0 in range(0, tq, td):
                acc = block(i * tq + row0, td, acc, row0, True)
            dqn = _keep(_lanes(tq, MLA_NOPE * half, MLA_NOPE), acc[:, :LANES] * MLA_SCALE)
            dqr = _keep(_lanes(tq, rope_lo, MLA_ROPE), acc[:, LANES:] * MLA_SCALE)
            if half == 0:
                dqn_ref[...] = dqn.astype(BF16)
                dqr_ref[...] = dqr.astype(BF16)
            else:
                dqn_ref[...] += dqn.astype(BF16)
                dqr_ref[...] += dqr.astype(BF16)

        @pl.when(i == s // tq - 1)
        def _():
            dkn_out[...] = (dkn_ref[...] * MLA_DK_SCALE).astype(BF16)
            dkr_out[...] = (dkr_ref[...] * MLA_DK_SCALE).astype(BF16)
            dv_out[...] = dv_ref[...].astype(BF16)

        if n_ride:
            pl.when(jnp.logical_and(g == n_pairs - 1, i == s // tq - 1))(finish)

    qblk = pl.BlockSpec((tq, LANES), lambda g, i: (i, g))
    full = pl.BlockSpec((s, LANES), lambda g, i: (0, g))
    once = lambda spec_map: pl.BlockSpec((s, LANES), spec_map, pipeline_mode=pl.Buffered(1))
    wide = _sds((s, n_pairs * LANES), BF16)
    outs = pl.pallas_call(
        kern, name="mla_bwd", grid=(n_pairs, s // tq),
        in_specs=[qblk, pl.BlockSpec((tq, LANES), lambda g, i: (i, g // 2)), once(lambda g, i: (0, g)), once(lambda g, i: (0, 0)),
                  once(lambda g, i: (0, g)), qblk, qblk, qblk] + [HBM_SPEC] * n_ride,
        out_specs=[qblk, qblk, full, full, full] + [HBM_SPEC] * n_ride,
        out_shape=[wide] * 5 + [_sds((N_SHARD - 1,) + p.shape[1:], p.dtype) for p in riders],
        scratch_shapes=[pltpu.VMEM((s, LANES), F32)] * 3 + (_scatter_sems(n_ride) if n_ride else []), compiler_params=_params(2),
    )(qn, qr, kn, kr, v, o, do, lse, *riders)
    return outs[:5], outs[5:]


def _sb_masks(tk):
    j = lax.broadcasted_iota(jnp.int32, (tk, tk), 0)
    c = lax.broadcasted_iota(jnp.int32, (tk, tk), 1)
    return (j > c).astype(BF16), (j < c).astype(BF16)


def _sb_scores(qs, kk, msuf, strict):
    z = _dot_nt(qs, kk)
    lom = -(jnp.maximum(z, 0.0) + jnp.log(1.0 + jnp.exp(-jnp.abs(z))))
    if strict is not None:
        lom = jnp.where(strict, lom, 0.0)
    return z, lom, _dot(lom.astype(BF16), msuf)


def _sb_strict(tq, tk, d):
    row = lax.broadcasted_iota(jnp.int32, (tq, tk), 0)
    col = lax.broadcasted_iota(jnp.int32, (tq, tk), 1)
    return col + d * tk < row


def _sb_fwd(q, k, v, msuf):
    s = q.shape[0]
    tq, tk = min(SB_TQ, s), min(SB_TK, s)
    ratio = tq // tk
    subs = 2 if s % (2 * tq) == 0 else 1
    chains = [(sub, half) for sub in range(subs) for half in range(2)]

    def kern(q_ref, k_ref, v_ref, m_ref, o_ref, c_ref):
        i = pl.program_id(1)
        msf = m_ref[...]
        lane = lax.broadcasted_iota(jnp.int32, (tq, LANES), 1)
        mine = [_lanes(tq, SB_DIM * half, SB_DIM) for half in range(2)]
        rows = [slice(tq * sub, tq * (sub + 1)) for sub in range(subs)]
        qs = {(sub, half): _keep(mine[half], q_ref[rows[sub], :]) * 0.125 for sub, half in chains}
        tile = [subs * i + sub for sub in range(subs)]

        def block(kb, carry, dd, chain, valid=None):
            c, acc, cm = carry
            ks = pl.ds(pl.multiple_of(jnp.maximum(kb, 0) * tk, tk), tk)
            strict = None if dd is None else _sb_strict(tq, tk, dd)
            z, lom, suf = _sb_scores(qs[chain], k_ref[ks, :], msf, strict)
            c_in = c if valid is None else jnp.where(valid, c, NEG)
            a = jnp.exp(z + lom + (suf + c_in))
            if strict is not None:
                a = jnp.where(strict, a, 0.0)
            acc = acc + _dot(a.astype(BF16), v_ref[ks, :])
            cm = jnp.where(lane == kb, c, cm)
            return c + jnp.sum(lom, axis=1, keepdims=True), acc, cm

        init = (jnp.zeros((tq, 1), F32), jnp.zeros((tq, LANES), F32), jnp.full((tq, LANES), NEG, F32))
        carries = {chain: init for chain in chains}
        for dd in range(ratio - 1, -1, -1):
            carries = {chain: block(tile[chain[0]] * ratio + dd, carries[chain], dd, chain) for chain in chains}

        def left(sub, t):
            return tile[sub] * ratio - 1 - t

        def live(st):
            t, cs = st
            alive = [jnp.logical_and(left(sub, t) >= 0, jnp.max(cs[chains.index((sub, half))][0]) > -SB_SKIP) for sub, half in chains]
            return functools.reduce(jnp.logical_or, alive)

        def step(st):
            t, cs = st
            return t + 1, tuple(block(left(sub, t), cs[n], None, (sub, half), left(sub, t) >= 0) for n, (sub, half) in enumerate(chains))

        _, done = lax.while_loop(live, step, (0, tuple(carries[chain] for chain in chains)))
        for sub in range(subs):
            d0, d1 = done[chains.index((sub, 0))], done[chains.index((sub, 1))]
            o_ref[rows[sub], :] = _keep(mine[0], d0[1]) + _keep(mine[1], d1[1])
            c_ref[rows[sub], :LANES] = d0[2]
            c_ref[rows[sub], LANES:] = d1[2]

    qblk = lambda n: pl.BlockSpec((subs * tq, n), lambda g, i: (i, g))
    full = pl.BlockSpec((s, LANES), lambda g, i: (0, g))
    return pl.pallas_call(
        kern, name="sb_fwd", grid=(SB_HEADS // 2, s // (subs * tq)),
        in_specs=[qblk(LANES), full, full, pl.BlockSpec((tk, tk), lambda g, i: (0, 0))],
        out_specs=[qblk(LANES), qblk(2 * LANES)],
        out_shape=[_sds((s, SB_WIDTH), F32), _sds((s, SB_HEADS * LANES), F32)], compiler_params=_params(2),
    )(q, k, v, msuf)


def _sb_bwd(q, k, v, do, cmat, msuf, mpre, riders=()):
    s = q.shape[0]
    tq, tk = min(SB_TQ, s), min(SB_TK, s)
    ratio = tq // tk
    subs = 2 if s % (2 * tq) == 0 else 1
    chains = [(sub, half) for sub in range(subs) for half in range(2)]
    n_steps = s // (subs * tq)
    n_ride = len(riders)
    n_pairs = SB_HEADS // 2

    def kern(q_ref, k_ref, v_ref, do_ref, c_ref, ms_ref, mp_ref, *rest):
        dq_ref, dk_out, dv_out = rest[n_ride:n_ride + 3]
        dk_ref, dv_ref = rest[2 * n_ride + 3:2 * n_ride + 5]
        i = pl.program_id(1)
        if n_ride:
            start, finish = _swap_steps(rest[:n_ride], rest[n_ride + 3:2 * n_ride + 3], *rest[2 * n_ride + 5:])
            pl.when(jnp.logical_and(pl.program_id(0) == 0, i == 0))(start)

        @pl.when(i == 0)
        def _():
            dk_ref[...] = jnp.zeros_like(dk_ref)
            dv_ref[...] = jnp.zeros_like(dv_ref)

        msf = ms_ref[...]
        mpf = mp_ref[...]
        lane = lax.broadcasted_iota(jnp.int32, (tq, LANES), 1)
        lane1 = lax.broadcasted_iota(jnp.int32, (1, LANES), 1)
        mine = [_lanes(tq, SB_DIM * half, SB_DIM) for half in range(2)]
        rows = [slice(tq * sub, tq * (sub + 1)) for sub in range(subs)]
        qv = {(sub, half): _keep(mine[half], q_ref[rows[sub], :]) for sub, half in chains}
        qs = {chain: t * 0.125 for chain, t in qv.items()}
        do_b = {(sub, half): _keep(mine[half], do_ref[rows[sub], :]).astype(BF16) for sub, half in chains}
        cm = {(sub, half): c_ref[rows[sub], LANES * half:LANES * (half + 1)] for sub, half in chains}
        n_left = [(subs * i + sub) * ratio for sub in range(subs)]

        def block(kb, carry, dd, chain, valid=None):
            dq_acc, pc = carry
            ks = pl.ds(pl.multiple_of(jnp.maximum(kb, 0) * tk, tk), tk)
            kk = k_ref[ks, :]
            strict = None if dd is None else _sb_strict(tq, tk, dd)
            z, lom, suf = _sb_scores(qs[chain], kk, msf, strict)
            c = jnp.sum(jnp.where(lane == kb, cm[chain], 0.0), axis=1, keepdims=True)
            if valid is not None:
                c = jnp.where(valid, c, NEG)
            a = jnp.exp(z + lom + (suf + c))
            if strict is not None:
                a = jnp.where(strict, a, 0.0)
            g = _dot_nt(do_b[chain], v_ref[ks, :]) * a
            p = pc + _dot(g.astype(BF16), mpf)
            omb = jnp.exp(lom)
            dz = (g * omb - (1.0 - omb) * p) * 0.125
            if strict is not None:
                dz = jnp.where(strict, dz, 0.0)
            dz = dz.astype(BF16)
            dv_ref[ks, :] += _dot_tn(a.astype(BF16), do_b[chain])
            dk_ref[ks, :] += _dot_tn(dz, qv[chain])
            return dq_acc + _dot(dz, kk), pc + jnp.sum(g, axis=1, keepdims=True)

        def needed(chain):
            seen = jnp.logical_and(jnp.max(cm[chain], axis=0, keepdims=True) > -SB_SKIP, lane1 < n_left[chain[0]])
            return jnp.sum(seen.astype(jnp.int32))

        depth = functools.reduce(jnp.maximum, [needed(chain) for chain in chains])

        def step(t, cs):
            back = depth - t
            return tuple(block(n_left[sub] - back, cs[n], None, (sub, half), n_left[sub] - back >= 0)
                         for n, (sub, half) in enumerate(chains))

        init = (jnp.zeros((tq, LANES), F32), jnp.zeros((tq, 1), F32))
        carries = list(lax.fori_loop(0, depth, step, tuple(init for _ in chains)))
        for dd in range(ratio):
            carries = [block(n_left[sub] + dd, carries[n], dd, (sub, half)) for n, (sub, half) in enumerate(chains)]
        for sub in range(subs):
            d0, d1 = carries[chains.index((sub, 0))], carries[chains.index((sub, 1))]
            dq_ref[rows[sub], :] = (_keep(mine[0], d0[0]) + _keep(mine[1], d1[0])).astype(BF16)

        @pl.when(i == n_steps - 1)
        def _():
            dk_out[...] = dk_ref[...].astype(BF16)
            dv_out[...] = dv_ref[...].astype(BF16)

        if n_ride:
            pl.when(jnp.logical_and(pl.program_id(0) == n_pairs - 1, i == n_steps - 1))(finish)

    qblk = lambda n: pl.BlockSpec((subs * tq, n), lambda g, i: (i, g))
    full = pl.BlockSpec((s, LANES), lambda g, i: (0, g))
    msk = pl.BlockSpec((tk, tk), lambda g, i: (0, 0))
    outs = pl.pallas_call(
        kern, name="sb_bwd", grid=(n_pairs, n_steps),
        in_specs=[qblk(LANES), full, full, qblk(LANES), qblk(2 * LANES), msk, msk] + [HBM_SPEC] * n_ride,
        out_specs=[qblk(LANES), full, full] + [HBM_SPEC] * n_ride,
        out_shape=[_sds((s, SB_WIDTH), BF16)] * 3 + _halves_shapes(riders),
        scratch_shapes=[pltpu.VMEM((s, LANES), F32)] * 2 + (_swap_sems(n_ride) if n_ride else []), compiler_params=_params(2),
    )(q, k, v, do, cmat, msuf, mpre, *riders)
    return outs[:3], outs[3:]


def _place():
    return lax.axis_index("x"), lax.axis_index("y"), lax.axis_index("c")


def _other_chips(x, y):
    return [(1 - x, y), (x, 1 - y), (1 - x, 1 - y)]


HBM_SPEC = pl.BlockSpec(memory_space=pl.ANY)


def _gather_steps(ins, outs, send_sems, recv_sems):
    n = len(ins)
    x, y, c = _place()
    sibling = (x, y, 1 - c)
    chips = _other_chips(x, y)

    def half_of(a, ref, pc):
        half = ins[a].shape[0] // 2
        return ref.at[pl.ds(pl.multiple_of(pc * half, 16), half), :]

    def copy(a, k, chip, pc, to, src=None):
        dst = half_of(a, outs[a].at[2 * chip[0] + chip[1]], pc)
        return pltpu.make_async_remote_copy(src_ref=dst if src is None else src, dst_ref=dst, send_sem=send_sems.at[7 * a + k],
                                            recv_sem=recv_sems.at[7 * a + k], device_id=to, device_id_type=MESH)

    def own(a):
        return pltpu.make_async_remote_copy(src_ref=ins[a], dst_ref=outs[a].at[2 * x + y], send_sem=send_sems.at[7 * a + 6],
                                            recv_sem=recv_sems.at[7 * a + 6], device_id=sibling, device_id_type=MESH)

    def first():
        far = [copy(a, j, (x, y), c, (*chip, c), src=half_of(a, ins[a], c)) for a in range(n) for j, chip in enumerate(chips)]
        return far + [own(a) for a in range(n)]

    def passed():
        return [copy(a, 3 + j, chip, c, sibling) for j, chip in enumerate(chips) for a in range(n)]

    def send():
        for cp in first():
            cp.start()

    def forward():
        for j, chip in enumerate(chips):
            for a in range(n):
                copy(a, j, chip, c, sibling).wait_recv()
        for cp in passed():
            cp.start()

    def finish():
        for j, chip in enumerate(chips):
            for a in range(n):
                copy(a, 3 + j, chip, 1 - c, sibling).wait_recv()
        for a in range(n):
            own(a).wait_recv()
        for cp in first() + passed():
            cp.wait_send()

    return send, forward, finish


def _gather_sems(n):
    return [pltpu.SemaphoreType.DMA((7 * n,)), pltpu.SemaphoreType.DMA((7 * n,))]


def _allgather_list(name, shards):
    n = len(shards)

    def body(*refs):
        for stage in _gather_steps(refs[:n], refs[n:2 * n], *refs[2 * n:]):
            stage()

    return pl.pallas_call(
        body, name=name, out_shape=[_sds((N_SHARD,) + a.shape, a.dtype) for a in shards], in_specs=[HBM_SPEC] * n,
        out_specs=[HBM_SPEC] * n, scratch_shapes=_gather_sems(n),
    )(*shards)


def _swap_steps(ins, outs, send_sems, recv_sems):
    x, y, c = _place()

    def copies():
        out = []
        for a in range(len(ins)):
            h = ins[a].shape[1] // 2
            src = ins[a].at[:, pl.ds(pl.multiple_of((1 - c) * h, 8), h), :]
            out.append(pltpu.make_async_remote_copy(src_ref=src, dst_ref=outs[a], send_sem=send_sems.at[a], recv_sem=recv_sems.at[a],
                                                    device_id=(x, y, 1 - c), device_id_type=MESH))
        return out

    def start():
        for cp in copies():
            cp.start()

    def finish():
        for cp in copies():
            cp.wait()

    return start, finish


def _swap_sems(n):
    return [pltpu.SemaphoreType.DMA((n,)), pltpu.SemaphoreType.DMA((n,))]


def _halves_shapes(gs):
    return [_sds((N_SHARD, g.shape[1] // 2, g.shape[2]), g.dtype) for g in gs]


def _swap_halves(name, gs):
    n = len(gs)

    def body(*refs):
        for stage in _swap_steps(refs[:n], refs[n:2 * n], *refs[2 * n:]):
            stage()

    return pl.pallas_call(body, name=name, out_shape=_halves_shapes(gs), in_specs=[HBM_SPEC] * n, out_specs=[HBM_SPEC] * n,
                          scratch_shapes=_swap_sems(n))(*gs)


def _add_sibling(name, gs, gots, c_idx):
    n = len(gs)

    def kern(c_ref, *refs):
        for a in range(n):
            tot = refs[a][...] + refs[n + a][...]
            refs[2 * n + a][...] = tot
            refs[3 * n + a][...] = tot.astype(BF16)

    quarter = lambda g: (None, g.shape[1] // 4, g.shape[2])
    in_specs = [pl.BlockSpec(quarter(g), lambda b, s, c_ref: (b, 2 * c_ref[0] + s, 0)) for g in gs]
    in_specs += [pl.BlockSpec(quarter(g), lambda b, s, c_ref: (b, s, 0)) for g in gs]
    out_specs = [pl.BlockSpec(quarter(g), lambda b, s, c_ref: (b, s, 0)) for g in gs] * 2
    out_shape = [_sds(t.shape, F32) for t in gots] + [_sds(t.shape, BF16) for t in gots]
    outs = pl.pallas_call(
        kern, name=name, out_shape=out_shape,
        grid_spec=pltpu.PrefetchScalarGridSpec(num_scalar_prefetch=1, grid=(N_SHARD, 2), in_specs=in_specs, out_specs=out_specs),
        compiler_params=_params(2),
    )(c_idx.reshape(1), *gs, *gots)
    return outs[:n], outs[n:]


def _scatter_steps(ins, outs, send_sems, recv_sems):
    x, y, c = _place()

    def copies():
        return [pltpu.make_async_remote_copy(
            src_ref=ins[a].at[2 * px + py], dst_ref=outs[a].at[j], send_sem=send_sems.at[3 * a + j], recv_sem=recv_sems.at[3 * a + j],
            device_id=(px, py, c), device_id_type=MESH) for a in range(len(ins)) for j, (px, py) in enumerate(_other_chips(x, y))]

    def start():
        for cp in copies():
            cp.start()

    def finish():
        for cp in copies():
            cp.wait()

    return start, finish


def _scatter_sems(n):
    return [pltpu.SemaphoreType.DMA((3 * n,)), pltpu.SemaphoreType.DMA((3 * n,))]


def _chip_scatter(ps):
    n = len(ps)

    def body(*refs):
        for stage in _scatter_steps(refs[:n], refs[n:2 * n], *refs[2 * n:]):
            stage()

    return pl.pallas_call(
        body, name="chip_scatter", out_shape=[_sds((N_SHARD - 1,) + p.shape[1:], p.dtype) for p in ps], in_specs=[HBM_SPEC] * n,
        out_specs=[HBM_SPEC] * n, scratch_shapes=_scatter_sems(n),
    )(*ps)


def _add_chips(name, ps, others, shard_idx):
    n = len(ps)

    def kern(b_ref, *refs):
        for a in range(n):
            tot = refs[a][...]
            for j in range(N_SHARD - 1):
                tot = tot + refs[n + a][j].astype(F32)
            refs[2 * n + a][...] = tot

    in_specs = [pl.BlockSpec((None, p.shape[1] // 2, p.shape[2]), lambda s, b_ref: (b_ref[0], s, 0)) for p in ps]
    in_specs += [pl.BlockSpec((N_SHARD - 1, p.shape[1] // 2, p.shape[2]), lambda s, b_ref: (0, s, 0)) for p in ps]
    out_specs = [pl.BlockSpec((p.shape[1] // 2, p.shape[2]), lambda s, b_ref: (s, 0)) for p in ps]
    return pl.pallas_call(
        kern, name=name, out_shape=[_sds(p.shape[1:], F32) for p in ps],
        grid_spec=pltpu.PrefetchScalarGridSpec(num_scalar_prefetch=1, grid=(2,), in_specs=in_specs, out_specs=out_specs),
        compiler_params=_params(1),
    )(shard_idx.reshape(1), *ps, *others)


def _swap_result(name, mines):
    n = len(mines)

    def body(*refs):
        ins, outs = refs[:n], refs[n:2 * n]
        send_sems, recv_sems = refs[2 * n:]
        x, y, c = _place()
        copies = [pltpu.make_async_remote_copy(src_ref=ins[a], dst_ref=outs[a], send_sem=send_sems.at[a], recv_sem=recv_sems.at[a],
                                               device_id=(x, y, 1 - c), device_id_type=MESH) for a in range(n)]
        for cp in copies:
            cp.start()
        for cp in copies:
            cp.wait()

    return pl.pallas_call(
        body, name=name, out_shape=[_sds(m.shape, m.dtype) for m in mines], in_specs=[HBM_SPEC] * n,
        out_specs=[HBM_SPEC] * n, scratch_shapes=[pltpu.SemaphoreType.DMA((n,)), pltpu.SemaphoreType.DMA((n,))],
    )(*mines)


def _allreduce_small(v):
    m_per, n = v.shape

    def body(x_ref, tot_ref, all_ref, send_sems, recv_sems, local_sem):
        x, y, c = _place()
        me, sibling = (x, y, c), (x, y, 1 - c)
        chips = _other_chips(x, y)

        def rows(px, py, pc):
            return all_ref.at[pl.ds(pl.multiple_of((4 * px + 2 * py + pc) * m_per, 8), m_per), :]

        def copy(k, block, to, src=None):
            return pltpu.make_async_remote_copy(
                src_ref=rows(*block) if src is None else src, dst_ref=rows(*block), send_sem=send_sems.at[k],
                recv_sem=recv_sems.at[k], device_id=to, device_id_type=MESH)

        mine = pltpu.make_async_copy(x_ref, rows(*me), local_sem)
        mine.start()
        first = [copy(0, me, sibling, src=x_ref)] + [copy(1 + j, me, (*chip, c), src=x_ref) for j, chip in enumerate(chips)]
        for cp in first:
            cp.start()
        passed = [copy(4 + j, (*chip, c), sibling) for j, chip in enumerate(chips)]
        for j, chip in enumerate(chips):
            copy(1 + j, (*chip, c), me).wait_recv()
            passed[j].start()
        copy(0, sibling, me).wait_recv()
        for j, chip in enumerate(chips):
            copy(4 + j, (*chip, 1 - c), me).wait_recv()
        for cp in first + passed:
            cp.wait_send()
        mine.wait()
        tot = all_ref[0:m_per, :]
        for dev in range(1, 8):
            tot = tot + all_ref[dev * m_per:(dev + 1) * m_per, :]
        tot_ref[...] = tot

    vmem = pl.BlockSpec(memory_space=pltpu.VMEM)
    return pl.pallas_call(
        body, name="allreduce_small", out_shape=_sds((m_per, n), F32), in_specs=[vmem], out_specs=vmem,
        scratch_shapes=[pltpu.VMEM((8 * m_per, n), F32), pltpu.SemaphoreType.DMA((7,)), pltpu.SemaphoreType.DMA((7,)),
                        pltpu.SemaphoreType.DMA],
    )(v)


def _adam_update(w, g, m, v):
    m_new = ADAM_B1 * m + (1.0 - ADAM_B1) * g
    v_new = ADAM_B2 * v + (1.0 - ADAM_B2) * (g * g)
    m_hat = m_new / (1.0 - ADAM_B1 ** ADAM_STEP)
    v_hat = v_new / (1.0 - ADAM_B2 ** ADAM_STEP)
    return -ADAM_LR * (m_hat / (jnp.sqrt(v_hat) + ADAM_EPS) + ADAM_WD * w), m_new, v_new


def _adamw(name, w, g, m, v):
    rows, width = w.shape
    tr = rows // 4 if rows % 32 == 0 else rows

    def kern(w_ref, g_ref, m_ref, v_ref, d_ref, mo_ref, vo_ref):
        d_ref[...], mo_ref[...], vo_ref[...] = _adam_update(w_ref[...], g_ref[...], m_ref[...], v_ref[...])

    spec = pl.BlockSpec((tr, width), lambda i: (i, 0))
    return pl.pallas_call(kern, name=name, grid=(rows // tr,), in_specs=[spec] * 4, out_specs=[spec] * 3,
                          out_shape=[_sds((rows, width), F32)] * 3, compiler_params=_params(1))(w, g, m, v)


def _adamw_halves(ws, mines, theirs, ms, vs, c_idx):
    n = len(ws)

    def kern(c_ref, *refs):
        take_mine = pl.program_id(0) == c_ref[0]
        for a in range(n):
            w_ref, mine_ref, theirs_ref, m_ref, v_ref = refs[5 * a:5 * a + 5]
            g_ref, d_ref, mo_ref, vo_ref = refs[5 * n + 4 * a:5 * n + 4 * a + 4]
            g = jnp.where(take_mine, mine_ref[...], theirs_ref[...])
            g_ref[...] = g
            d_ref[...], mo_ref[...], vo_ref[...] = _adam_update(w_ref[...], g, m_ref[...], v_ref[...])

    in_specs, out_specs, out_shape = [], [], []
    for w in ws:
        rows, width = w.shape
        tr = rows // (2 * ADAM_STEPS)
        whole = pl.BlockSpec((tr, width), lambda h, j, c_ref: (ADAM_STEPS * h + j, 0))
        part = pl.BlockSpec((tr, width), lambda h, j, c_ref: (j, 0))
        in_specs += [whole, part, part, whole, whole]
        out_specs += [whole] * 4
        out_shape += [_sds((rows, width), F32)] * 4
    operands = [t for group in zip(ws, mines, theirs, ms, vs) for t in group]
    outs = pl.pallas_call(
        kern, name="adamw_shards", out_shape=out_shape,
        grid_spec=pltpu.PrefetchScalarGridSpec(num_scalar_prefetch=1, grid=(2, ADAM_STEPS), in_specs=in_specs, out_specs=out_specs),
        compiler_params=_params(2),
    )(c_idx.reshape(1), *operands)
    return [outs[4 * a:4 * a + 4] for a in range(n)]


SHARDED = (("w_in", D_MODEL, IN_WIDTH, 1), ("w_uq", Q_RANK, MLA_HEADS * MLA_QK, 1),
           ("w_ukv", KV_RANK, MLA_HEADS * (MLA_NOPE + MLA_V), 1), ("w_o", D_MODEL, D_MODEL, 0),
           ("w_gate", D_MODEL, D_FF, 1), ("w_up", D_MODEL, D_FF, 1), ("w_down", D_FF, D_MODEL, 0))
EARLY = ("w_in", "w_uq", "w_ukv")
LATE = ("w_o", "w_gate", "w_up", "w_down")
FLIPPED = ("w_gate", "w_up")
SMALL = (("norm_mix", D_MODEL), ("q_latent_norm", Q_RANK), ("kv_latent_norm", KV_RANK), ("out_norm_mla", MLA_WIDTH),
         ("out_norm_sb", SB_WIDTH), ("norm_ffn", D_MODEL), ("norm_final", D_MODEL))


def _full_weight(gathered, axis):
    n_sh, k, n = gathered.shape
    return gathered.transpose(1, 0, 2).reshape(k, n_sh * n) if axis == 1 else gathered.reshape(n_sh * k, n)


def _shard_major(g, axis):
    r, c = g.shape
    return g.reshape(r, N_SHARD, c // N_SHARD).transpose(1, 0, 2) if axis == 1 else g.reshape(N_SHARD, r // N_SHARD, c)


def _rot_cols(w):
    hh = MLA_ROPE // 2
    return jnp.concatenate([-w[..., hh:], w[..., :hh]], axis=-1)


def _rot_cols_t(g):
    hh = MLA_ROPE // 2
    return jnp.concatenate([g[..., hh:], -g[..., :hh]], axis=-1)


def _attention_weights(full, small):
    w_in = full["w_in"]
    s0, s1, s2 = Q_RANK, Q_RANK + KV_RANK, Q_RANK + KV_RANK + MLA_ROPE
    uq = full["w_uq"].reshape(Q_RANK, MLA_HEADS, MLA_QK)
    ukv = full["w_ukv"].reshape(KV_RANK, MLA_HEADS, MLA_NOPE + MLA_V)
    w_kr = w_in[:, s1:s2]
    per_tile = ROPE_TILE // MLA_ROPE
    w = {
        "w_cq": w_in[:, :s0], "w_ckv": w_in[:, s0:s1],
        "w_kr4": jnp.tile(w_kr, (1, per_tile)), "w_kr4r": jnp.tile(_rot_cols(w_kr), (1, per_tile)),
        "w_kr8": jnp.tile(w_kr, (1, MLA_HEADS)), "w_kr8r": jnp.tile(_rot_cols(w_kr), (1, MLA_HEADS)),
        "w_sbq": w_in[:, s2:s2 + SB_WIDTH], "w_sbk": w_in[:, s2 + SB_WIDTH:s2 + 2 * SB_WIDTH], "w_sbv": w_in[:, s2 + 2 * SB_WIDTH:],
        "w_qn": uq[..., :MLA_NOPE].reshape(Q_RANK, -1), "w_qr": uq[..., MLA_NOPE:].reshape(Q_RANK, -1),
        "w_qrr": _rot_cols(uq[..., MLA_NOPE:]).reshape(Q_RANK, -1),
        "w_kn": ukv[..., :MLA_NOPE].reshape(KV_RANK, -1), "w_v": ukv[..., MLA_NOPE:].reshape(KV_RANK, -1),
    }
    w.update(g_mix=small["norm_mix"], g_q=small["q_latent_norm"], g_kv=small["kv_latent_norm"], g_a=small["out_norm_mla"],
             g_b=small["out_norm_sb"], g_f=small["norm_ffn"], g_n=small["norm_final"])
    return w


def _ffn_weights(full):
    w = {"w_oa": full["w_o"][:MLA_WIDTH], "w_ob": full["w_o"][MLA_WIDTH:], "w_down": full["w_down"]}
    w.update({name + "_t": full[name] for name in FLIPPED})
    return w


def _rope_tables(positions):
    inv_freq = ROPE_THETA ** (-jnp.arange(0, MLA_ROPE, 2, dtype=F32) / MLA_ROPE)
    ang = positions.astype(F32)[:, None] * inv_freq[None, :]
    cos, sin = jnp.cos(ang), jnp.sin(ang)
    return {"cos": jnp.tile(jnp.concatenate([cos, cos], axis=1), (1, MLA_HEADS)),
            "sin": jnp.tile(jnp.concatenate([sin, sin], axis=1), (1, MLA_HEADS))}


def _by_head(g_wide, g_narrow, wide, narrow):
    r = g_wide.shape[0]
    return jnp.concatenate([g_wide.reshape(r, MLA_HEADS, wide), g_narrow.reshape(r, MLA_HEADS, narrow)], axis=-1).reshape(r, -1)


def kernel(x, positions, norm_mix, w_in, q_latent_norm, w_uq, kv_latent_norm, w_ukv, out_norm_mla, out_norm_sb, w_o, norm_ffn, w_gate, w_up, w_down, norm_final, loss_target, m_norm_mix, m_w_in, m_q_latent_norm, m_w_uq, m_kv_latent_norm, m_w_ukv, m_out_norm_mla, m_out_norm_sb, m_w_o, m_norm_ffn, m_w_gate, m_w_up, m_w_down, m_norm_final, v_norm_mix, v_w_in, v_q_latent_norm, v_w_uq, v_kv_latent_norm, v_w_ukv, v_out_norm_mla, v_out_norm_sb, v_w_o, v_norm_ffn, v_w_gate, v_w_up, v_w_down, v_norm_final):
    given = dict(norm_mix=norm_mix, w_in=w_in, q_latent_norm=q_latent_norm, w_uq=w_uq, kv_latent_norm=kv_latent_norm, w_ukv=w_ukv,
                 out_norm_mla=out_norm_mla, out_norm_sb=out_norm_sb, w_o=w_o, norm_ffn=norm_ffn, w_gate=w_gate, w_up=w_up,
                 w_down=w_down, norm_final=norm_final)
    mom_m = dict(norm_mix=m_norm_mix, w_in=m_w_in, q_latent_norm=m_q_latent_norm, w_uq=m_w_uq, kv_latent_norm=m_kv_latent_norm,
                 w_ukv=m_w_ukv, out_norm_mla=m_out_norm_mla, out_norm_sb=m_out_norm_sb, w_o=m_w_o, norm_ffn=m_norm_ffn,
                 w_gate=m_w_gate, w_up=m_w_up, w_down=m_w_down, norm_final=m_norm_final)
    mom_v = dict(norm_mix=v_norm_mix, w_in=v_w_in, q_latent_norm=v_q_latent_norm, w_uq=v_w_uq, kv_latent_norm=v_kv_latent_norm,
                 w_ukv=v_w_ukv, out_norm_mla=v_out_norm_mla, out_norm_sb=v_out_norm_sb, w_o=v_w_o, norm_ffn=v_norm_ffn,
                 w_gate=v_w_gate, w_up=v_w_up, w_down=v_w_down, norm_final=v_norm_final)
    xs = x[0]
    tgt = loss_target[0]
    s = xs.shape[0]
    c_idx = lax.axis_index("c")
    shard_idx = 2 * lax.axis_index("x") + lax.axis_index("y")

    def block2d(t, name):
        t = t.reshape(t.shape[-2:])
        return t.T if name in FLIPPED else t

    shard2d = {name: block2d(given[name], name) for name, *_ in SHARDED}
    local = {name: shard2d[name].astype(BF16) for name, *_ in SHARDED}
    axis_of = {name: 0 if name in FLIPPED else axis for name, _, _, axis in SHARDED}

    def whole(names, gathered):
        return {name: _full_weight(t, axis_of[name]) for name, t in zip(names, gathered)}

    small = {name: given[name].reshape(1, n) for name, n in SMALL}
    w = _attention_weights(whole(EARLY, _allgather_list("allgather_w", [local[name] for name in EARLY])), small)
    tabs = _rope_tables(positions[0])
    msuf, mpre = _sb_masks(min(SB_TK, s))

    u, cq, ckv, cqn, ckvn, qn, qr, kn, vm, kr, sq, sk, sv = _fwd_a(xs, tabs, w)
    o_mla, lse, late = _mla_fwd(qn, qr, kn, kr, vm, [local[name] for name in LATE])
    w.update(_ffn_weights(whole(LATE, late)))
    o_sb, cmat = _sb_fwd(sq, sk, sv, msuf)
    merged, h1, f, gate, up, act = _fwd_b1(xs, o_mla, o_sb, w)
    dh2, loss_part, dg_n = _fwd_b2(h1, act, tgt, w)

    def shards_of(names, grads):
        return [_shard_major(grads[name], axis_of[name]) for name in names]

    def reduced(tag, chip_f32, others):
        mine = _add_chips("add_chips_" + tag, chip_f32, others, shard_idx)
        return tuple(mine), tuple(_swap_result("swap_result_" + tag, mine))

    dgate, dup, dh1, do_mla, do_sb, dg_f, dg_a, dg_b = _bwd_b(dh2, gate, up, h1, o_mla, o_sb, w)
    late_gs = shards_of(LATE, {
        "w_o": _tn_matmul("dw_o", merged, dh1), "w_gate": _tn_matmul("dw_gate", dgate, f),
        "w_up": _tn_matmul("dw_up", dup, f), "w_down": _tn_matmul("dw_down", act, dh2)})
    (dsq, dsk, dsv), late_got = _sb_bwd(sq, sk, sv, do_sb, cmat, msuf, mpre, late_gs)
    late_f32, late_bf16 = _add_sibling("add_sibling_late", late_gs, late_got, c_idx)
    (dqn, dqr, dkn, dkr, dvm), late_others = _mla_bwd(qn, qr, kn, kr, vm, o_mla, do_mla, lse, late_bf16)
    mine_late, theirs_late = reduced("late", late_f32, late_others)
    dx, a1, a2, dcq, dckv, dkrc, dkrs, dg_q, dg_kv, dg_mix = _bwd_a(xs, dh1, cq, ckv, dqn, dqr, dkn, dvm, dkr, dsq, dsk, dsv, tabs, w)

    g_cq, g_ckv, g_krc, g_krs, g_sq, g_sk, g_sv = _tn_multi("dw_in", u, [dcq, dckv, dkrc, dkrs, dsq, dsk, dsv])
    g_qn, g_qr1, g_qr2 = _tn_multi("dw_uq", cqn, [dqn, a1, a2])
    g_kn, g_v = _tn_multi("dw_ukv", ckvn, [dkn, dvm])
    slots = lambda g: g.reshape(g.shape[0], MLA_HEADS, MLA_ROPE)
    g_kr = jnp.sum(slots(g_krc), axis=1) + _rot_cols_t(jnp.sum(slots(g_krs), axis=1))
    g_qr = (slots(g_qr1) + _rot_cols_t(slots(g_qr2))).reshape(Q_RANK, -1)
    early_gs = shards_of(EARLY, {
        "w_in": jnp.concatenate([g_cq, g_ckv, g_kr, g_sq, g_sk, g_sv], axis=1),
        "w_uq": _by_head(g_qn, g_qr, MLA_NOPE, MLA_ROPE),
        "w_ukv": _by_head(g_kn, g_v, MLA_NOPE, MLA_V)})
    early_f32, early_bf16 = _add_sibling("add_sibling_early", early_gs, _swap_halves("swap_halves_early", early_gs), c_idx)
    mine_early, theirs_early = reduced("early", early_f32, _chip_scatter(early_bf16))
    halves = dict(zip(EARLY + LATE, zip(mine_early + mine_late, theirs_early + theirs_late)))

    small_parts = jnp.concatenate([dg_mix, dg_q, dg_kv, dg_a, dg_b, dg_f, dg_n, loss_part], axis=1)
    small_sum = _allreduce_small(jnp.broadcast_to(small_parts, (8, small_parts.shape[1])))
    small_g, loss = small_sum[0:1, :-LANES], small_sum[0, -LANES]

    g_out, d_out, m_out, v_out = {}, {}, {}, {}
    names = [name for name, *_ in SHARDED]
    updated = _adamw_halves([shard2d[name] for name in names], [halves[name][0] for name in names], [halves[name][1] for name in names],
                            [block2d(mom_m[name], name) for name in names], [block2d(mom_v[name], name) for name in names], c_idx)
    for name, outs in zip(names, updated):
        shape = given[name].shape
        g_out[name], d_out[name], m_out[name], v_out[name] = ((t.T if name in FLIPPED else t).reshape(shape) for t in outs)
    cat = lambda src: jnp.concatenate([src[name].reshape(1, n) for name, n in SMALL], axis=1)
    d, mn, vn = _adamw("adamw_small", cat(given), small_g, cat(mom_m), cat(mom_v))
    off = 0
    for name, n in SMALL:
        shape = given[name].shape
        g_out[name], d_out[name], m_out[name], v_out[name] = (t[:, off:off + n].reshape(shape) for t in (small_g, d, mn, vn))
        off += n

    order = ["norm_mix", "w_in", "q_latent_norm", "w_uq", "kv_latent_norm", "w_ukv", "out_norm_mla", "out_norm_sb", "w_o",
             "norm_ffn", "w_gate", "w_up", "w_down", "norm_final"]
    return (loss, dx[None], *[g_out[n] for n in order], *[d_out[n] for n in order], *[m_out[n] for n in order],
            *[v_out[n] for n in order])
```

```python
import functools
import math

import jax
import jax.numpy as jnp
from jax import lax
from jax.experimental import pallas as pl
from jax.experimental.pallas import tpu as pltpu

F32 = jnp.float32
BF16 = jnp.bfloat16
MESH = pl.DeviceIdType.MESH

D_MODEL = 1024
EPS = 1e-6
MLA_HEADS = 8
MLA_NOPE = 64
MLA_ROPE = 32
MLA_V = 64
MLA_QK = MLA_NOPE + MLA_ROPE
Q_RANK = 256
KV_RANK = 128
ROPE_THETA = 10000.0
SB_HEADS = 8
SB_DIM = 64
MLA_WIDTH = MLA_HEADS * MLA_V
SB_WIDTH = SB_HEADS * SB_DIM
D_FF = 2816
IN_WIDTH = Q_RANK + KV_RANK + MLA_ROPE + 3 * SB_WIDTH

ADAM_LR = 0.001
ADAM_B1 = 0.9
ADAM_B2 = 0.999
ADAM_EPS = 1e-08
ADAM_WD = 0.01
ADAM_STEP = 10

N_SHARD = 4
LANES = 128
ROPE_TILE = LANES
VMEM_LIMIT = 56 * 1024 * 1024
TN_ACC_BYTES = 6 * 1024 * 1024 + 512 * 1024
NEG = -1e30
MLA_SCALE = 1.0 / math.sqrt(MLA_QK)
MLA_DK_SCALE = math.log(2.0)
MLA_QSCALE = MLA_SCALE * math.log2(math.e)
SB_SKIP = 110.0

ROW_TILE = 512
ROW_TILE_ELEMENTWISE = 256
MLA_TQ = 1024
SB_TQ = 512
MLA_TK = 1024
MLA_BWD_TK = 1024
MLA_DIAG_TK = 256
SB_TK = 256
TN_TS = 2048
ADAM_STEPS = 4


def _dot(a, b):
    return jnp.dot(a, b, preferred_element_type=F32)


def _dot_nt(a, b):
    return lax.dot_general(a, b, (((1,), (1,)), ((), ())), preferred_element_type=F32)


def _dot_tn(a, b):
    return lax.dot_general(a, b, (((0,), (0,)), ((), ())), preferred_element_type=F32)


def _params(n_grid, vmem=VMEM_LIMIT):
    return pltpu.CompilerParams(dimension_semantics=("arbitrary",) * n_grid, vmem_limit_bytes=vmem)


def _rms(x):
    r = lax.rsqrt(jnp.mean(x * x, axis=-1, keepdims=True) + EPS)
    return x * r, r


def _rms_bwd(n, r, g, dy):
    dn = dy * g
    dx = r * (dn - n * jnp.mean(dn * n, axis=-1, keepdims=True))
    return dx, jnp.sum(dy * n, axis=0, keepdims=True)


def _accumulate(ref, val, step):
    @pl.when(step == 0)
    def _():
        ref[...] = val

    @pl.when(step != 0)
    def _():
        ref[...] += val


def _rowwise(name, body, rows, consts, row_out, acc_out, tm):
    n_rows = rows[0].shape[0]
    tm = min(tm, n_rows)
    nr, nc, no = len(rows), len(consts), len(row_out)

    def kern(*refs):
        body(refs[:nr], refs[nr:nr + nc], refs[nr + nc:nr + nc + no], refs[nr + nc + no:], pl.program_id(0))

    in_specs = [pl.BlockSpec((tm, a.shape[1]), lambda i: (i, 0)) for a in rows]
    in_specs += [pl.BlockSpec(a.shape, lambda i: (0, 0), pipeline_mode=pl.Buffered(1)) for a in consts]
    out_specs = [pl.BlockSpec((tm, s.shape[1]), lambda i: (i, 0)) for s in row_out]
    out_specs += [pl.BlockSpec(s.shape, lambda i: (0, 0)) for s in acc_out]
    return pl.pallas_call(
        kern, name=name, grid=(n_rows // tm,), in_specs=in_specs, out_specs=out_specs,
        out_shape=list(row_out) + list(acc_out), compiler_params=_params(1),
    )(*rows, *consts)


def _sds(shape, dtype):
    return jax.ShapeDtypeStruct(shape, dtype)


def _fwd_a(x, tabs, w):
    s = x.shape[0]

    def body(r, c, o, a, step):
        x_ref, cos_ref, sin_ref = r
        gmix, wcq, wckv, wkr, wkrr, wsq, wsk, wsv, gq, wqn, wqr, wqrr, gkv, wkn, wv = c
        u_o, cq_o, ckv_o, cqn_o, ckvn_o, qn_o, qr_o, kn_o, v_o, kr_o, sq_o, sk_o, sv_o = o
        cos, sin = cos_ref[...], sin_ref[...]
        n, _ = _rms(x_ref[...])
        u = (n * gmix[...]).astype(BF16)
        u_o[...] = u
        cq = _dot(u, wcq[...])
        ckv = _dot(u, wckv[...])
        kr_o[...] = (_dot(u, wkr[...]) * cos[:, :ROPE_TILE] + _dot(u, wkrr[...]) * sin[:, :ROPE_TILE]).astype(BF16)
        sq_o[...] = _dot(u, wsq[...]).astype(BF16)
        sk_o[...] = _dot(u, wsk[...]).astype(BF16)
        sv_o[...] = _dot(u, wsv[...]).astype(BF16)
        cq_o[...] = cq
        ckv_o[...] = ckv
        nq, _ = _rms(cq)
        cqn = (nq * gq[...]).astype(BF16)
        cqn_o[...] = cqn
        qn_o[...] = (_dot(cqn, wqn[...]) * MLA_QSCALE).astype(BF16)
        qr_o[...] = ((_dot(cqn, wqr[...]) * cos + _dot(cqn, wqrr[...]) * sin) * MLA_QSCALE).astype(BF16)
        nkv, _ = _rms(ckv)
        ckvn = (nkv * gkv[...]).astype(BF16)
        ckvn_o[...] = ckvn
        kn_o[...] = _dot(ckvn, wkn[...]).astype(BF16)
        v_o[...] = _dot(ckvn, wv[...]).astype(BF16)

    outs = [
        _sds((s, D_MODEL), BF16), _sds((s, Q_RANK), F32), _sds((s, KV_RANK), F32), _sds((s, Q_RANK), BF16),
        _sds((s, KV_RANK), BF16), _sds((s, MLA_HEADS * MLA_NOPE), BF16), _sds((s, MLA_HEADS * MLA_ROPE), BF16),
        _sds((s, MLA_HEADS * MLA_NOPE), BF16), _sds((s, MLA_WIDTH), BF16), _sds((s, ROPE_TILE), BF16),
        _sds((s, SB_WIDTH), BF16), _sds((s, SB_WIDTH), BF16), _sds((s, SB_WIDTH), BF16),
    ]
    consts = [w["g_mix"], w["w_cq"], w["w_ckv"], w["w_kr4"], w["w_kr4r"], w["w_sbq"], w["w_sbk"], w["w_sbv"], w["g_q"],
              w["w_qn"], w["w_qr"], w["w_qrr"], w["g_kv"], w["w_kn"], w["w_v"]]
    return _rowwise("fwd_a", body, [x, tabs["cos"], tabs["sin"]], consts, outs, [], ROW_TILE)


def _fwd_b1(x, o_mla, o_sb, w):
    s = x.shape[0]

    def body(r, c, o, a, step):
        x_ref, oa_ref, ob_ref = r
        ga, gb, woa, wob, gf, wg, wu = c
        mg_o, h1_o, f_o, gate_o, up_o, act_o = o
        na, _ = _rms(oa_ref[...])
        nb, _ = _rms(ob_ref[...])
        ma = (na * ga[...]).astype(BF16)
        mb = (nb * gb[...]).astype(BF16)
        mg_o[:, :MLA_WIDTH] = ma
        mg_o[:, MLA_WIDTH:] = mb
        h1 = x_ref[...] + _dot(ma, woa[...]) + _dot(mb, wob[...])
        h1_o[...] = h1
        nf, _ = _rms(h1)
        f = (nf * gf[...]).astype(BF16)
        f_o[...] = f
        gate = _dot_nt(f, wg[...])
        up = _dot_nt(f, wu[...])
        gate_o[...] = gate.astype(BF16)
        up_o[...] = up.astype(BF16)
        act_o[...] = (gate * (1.0 / (1.0 + jnp.exp(-gate))) * up).astype(BF16)

    outs = [_sds((s, D_MODEL), BF16), _sds((s, D_MODEL), F32), _sds((s, D_MODEL), BF16), _sds((s, D_FF), BF16),
            _sds((s, D_FF), BF16), _sds((s, D_FF), BF16)]
    consts = [w["g_a"], w["g_b"], w["w_oa"], w["w_ob"], w["g_f"], w["w_gate_t"], w["w_up_t"]]
    return _rowwise("fwd_b1", body, [x, o_mla, o_sb], consts, outs, [], ROW_TILE)


def _fwd_b2(h1, act, tgt, w):
    s = h1.shape[0]

    def body(r, c, o, a, step):
        h1_ref, act_ref, t_ref = r
        wd, gn = c
        (dh2_o,) = o
        loss_o, dgn_o = a
        h2 = h1_ref[...] + _dot(act_ref[...], wd[...])
        n2, r2 = _rms(h2)
        err = n2 * gn[...] - t_ref[...]
        part = jnp.sum(jnp.sum(err * err, axis=1, keepdims=True), axis=0, keepdims=True) * (0.5 / D_MODEL)
        _accumulate(loss_o, jnp.broadcast_to(part, (1, LANES)), step)
        dh2, dgn = _rms_bwd(n2, r2, gn[...], err * (1.0 / D_MODEL))
        dh2_o[...] = dh2
        _accumulate(dgn_o, dgn, step)

    return _rowwise("fwd_b2", body, [h1, act, tgt], [w["w_down"], w["g_n"]], [_sds((s, D_MODEL), F32)],
                    [_sds((1, LANES), F32), _sds((1, D_MODEL), F32)], ROW_TILE)


def _bwd_b(dh2, gate, up, h1, o_mla, o_sb, w):
    s = h1.shape[0]

    def body(r, c, o, a, step):
        dh2_ref, gate_ref, up_ref, h1_ref, oa_ref, ob_ref = r
        wd, wgt, wut, gf, woa, wob, ga, gb = c
        dgate_o, dup_o, dh1_o, doa_o, dob_o = o
        dgf_o, dga_o, dgb_o = a
        dh2 = dh2_ref[...]
        dact = _dot_nt(dh2.astype(BF16), wd[...])
        gate = gate_ref[...].astype(F32)
        sig = 1.0 / (1.0 + jnp.exp(-gate))
        dup = (dact * (gate * sig)).astype(BF16)
        dgate = (dact * up_ref[...].astype(F32) * (sig * (1.0 + gate * (1.0 - sig)))).astype(BF16)
        dup_o[...] = dup
        dgate_o[...] = dgate
        df = _dot(dgate, wgt[...]) + _dot(dup, wut[...])
        nf, rf = _rms(h1_ref[...])
        dres, dgf = _rms_bwd(nf, rf, gf[...], df)
        dh1 = dh2 + dres
        dh1_o[...] = dh1
        dh1b = dh1.astype(BF16)
        na, ra = _rms(oa_ref[...])
        doa, dga = _rms_bwd(na, ra, ga[...], _dot_nt(dh1b, woa[...]))
        nb, rb = _rms(ob_ref[...])
        dob, dgb = _rms_bwd(nb, rb, gb[...], _dot_nt(dh1b, wob[...]))
        doa_o[...] = doa
        dob_o[...] = dob
        _accumulate(dgf_o, dgf, step)
        _accumulate(dga_o, dga, step)
        _accumulate(dgb_o, dgb, step)

    consts = [w["w_down"], w["w_gate_t"], w["w_up_t"], w["g_f"], w["w_oa"], w["w_ob"], w["g_a"], w["g_b"]]
    outs = [_sds((s, D_FF), BF16), _sds((s, D_FF), BF16), _sds((s, D_MODEL), F32), _sds((s, MLA_WIDTH), F32), _sds((s, SB_WIDTH), F32)]
    accs = [_sds((1, D_MODEL), F32), _sds((1, MLA_WIDTH), F32), _sds((1, SB_WIDTH), F32)]
    return _rowwise("bwd_b", body, [dh2, gate, up, h1, o_mla, o_sb], consts, outs, accs, ROW_TILE_ELEMENTWISE)


def _fold_pairs(t):
    return jnp.concatenate([t[:, :LANES] + t[:, LANES:2 * LANES], t[:, 2 * LANES:3 * LANES] + t[:, 3 * LANES:]], axis=1)


def _bwd_a(x, dh1, cq, ckv, dqn, dqr, dkn, dvm, dkr, dsq, dsk, dsv, tabs, w):
    s = x.shape[0]

    def body(r, c, o, a, step):
        x_ref, dh1_ref, cq_ref, ckv_ref, dqn_ref, dqr_ref, dkn_ref, dvm_ref, dkr_ref, dsq_ref, dsk_ref, dsv_ref, cos_ref, sin_ref = r
        wqn, wqr, wqrr, gq, wkn, wv, gkv, wcq, wckv, wkr, wkrr, wsq, wsk, wsv, gmix = c
        dx_o, a1_o, a2_o, dcq_o, dckv_o, dkrc_o, dkrs_o = o
        dgq_o, dgkv_o, dgmix_o = a
        cos, sin = cos_ref[...], sin_ref[...]
        dqr = _fold_pairs(dqr_ref[...].astype(F32))
        a1 = (dqr * cos).astype(BF16)
        a2 = (dqr * sin).astype(BF16)
        a1_o[...] = a1
        a2_o[...] = a2
        nq, rq = _rms(cq_ref[...])
        dcqn = _dot_nt(dqn_ref[...], wqn[...]) + _dot_nt(a1, wqr[...]) + _dot_nt(a2, wqrr[...])
        dcq, dgq = _rms_bwd(nq, rq, gq[...], dcqn)
        nkv, rkv = _rms(ckv_ref[...])
        dckvn = _dot_nt(dkn_ref[...], wkn[...]) + _dot_nt(dvm_ref[...], wv[...])
        dckv, dgkv = _rms_bwd(nkv, rkv, gkv[...], dckvn)
        dkr = _fold_pairs(dkr_ref[...].astype(F32))
        dcq_b = dcq.astype(BF16)
        dckv_b = dckv.astype(BF16)
        dkrc = (dkr * cos).astype(BF16)
        dkrs = (dkr * sin).astype(BF16)
        dcq_o[...] = dcq_b
        dckv_o[...] = dckv_b
        dkrc_o[...] = dkrc
        dkrs_o[...] = dkrs
        du = (_dot_nt(dcq_b, wcq[...]) + _dot_nt(dckv_b, wckv[...]) + _dot_nt(dkrc, wkr[...]) + _dot_nt(dkrs, wkrr[...])
              + _dot_nt(dsq_ref[...], wsq[...]) + _dot_nt(dsk_ref[...], wsk[...]) + _dot_nt(dsv_ref[...], wsv[...]))
        nx, rx = _rms(x_ref[...])
        dres, dgmix = _rms_bwd(nx, rx, gmix[...], du)
        dx_o[...] = dh1_ref[...] + dres
        _accumulate(dgq_o, dgq, step)
        _accumulate(dgkv_o, dgkv, step)
        _accumulate(dgmix_o, dgmix, step)

    consts = [w["w_qn"], w["w_qr"], w["w_qrr"], w["g_q"], w["w_kn"], w["w_v"], w["g_kv"], w["w_cq"], w["w_ckv"],
              w["w_kr8"], w["w_kr8r"], w["w_sbq"], w["w_sbk"], w["w_sbv"], w["g_mix"]]
    rope_w = MLA_HEADS * MLA_ROPE
    outs = [_sds((s, D_MODEL), F32), _sds((s, rope_w), BF16), _sds((s, rope_w), BF16), _sds((s, Q_RANK), BF16),
            _sds((s, KV_RANK), BF16), _sds((s, rope_w), BF16), _sds((s, rope_w), BF16)]
    accs = [_sds((1, Q_RANK), F32), _sds((1, KV_RANK), F32), _sds((1, D_MODEL), F32)]
    rows = [x, dh1, cq, ckv, dqn, dqr, dkn, dvm, dkr, dsq, dsk, dsv, tabs["cos"], tabs["sin"]]
    return _rowwise("bwd_a", body, rows, consts, outs, accs, ROW_TILE)


def _tn_multi(name, x, ys, riders=()):
    s, k = x.shape
    ts = min(TN_TS, s)
    n_y, n_ride = len(ys), len(riders)

    def kern(*refs):
        step = pl.program_id(0)
        rest = refs[1 + n_y:]
        if n_ride:
            start, finish = _result_steps(rest[:n_ride], rest[n_ride + n_y:2 * n_ride + n_y], *rest[2 * n_ride + n_y:])
            pl.when(step == 0)(start)
        xb = refs[0][...].astype(BF16)
        for j in range(n_y):
            _accumulate(rest[n_ride + j], _dot_tn(xb, refs[1 + j][...].astype(BF16)), step)
        if n_ride:
            pl.when(step == s // ts - 1)(finish)

    outs = pl.pallas_call(
        kern, name=name, grid=(s // ts,),
        in_specs=[pl.BlockSpec((ts, k), lambda i: (i, 0))] + [pl.BlockSpec((ts, y.shape[1]), lambda i: (i, 0)) for y in ys]
        + [HBM_SPEC] * n_ride,
        out_specs=[pl.BlockSpec((k, y.shape[1]), lambda i: (0, 0)) for y in ys] + [HBM_SPEC] * n_ride,
        out_shape=[_sds((k, y.shape[1]), F32) for y in ys] + [_sds(m.shape, m.dtype) for m in riders],
        scratch_shapes=_swap_sems(n_ride) if n_ride else [], compiler_params=_params(1),
    )(x, *ys, *riders)
    return (outs[:n_y], outs[n_y:]) if n_ride else outs


def _tn_tile(k, n):
    if n % LANES or k * n * 4 <= TN_ACC_BYTES:
        return n
    units = n // LANES
    best = 1
    for d in range(1, units + 1):
        if units % d == 0 and k * d * LANES * 4 <= TN_ACC_BYTES:
            best = d
    return best * LANES


def _tn_matmul(name, x, y):
    s, k = x.shape
    n = y.shape[1]
    ts = min(TN_TS, s)
    tn = _tn_tile(k, n)

    def kern(x_ref, y_ref, o_ref):
        step = pl.program_id(1)
        _accumulate(o_ref, _dot_tn(x_ref[...].astype(BF16), y_ref[...].astype(BF16)), step)

    return pl.pallas_call(
        kern, name=name, grid=(n // tn, s // ts),
        in_specs=[pl.BlockSpec((ts, k), lambda j, i: (i, 0)), pl.BlockSpec((ts, tn), lambda j, i: (i, j))],
        out_specs=pl.BlockSpec((k, tn), lambda j, i: (0, j)), out_shape=_sds((k, n), F32), compiler_params=_params(2),
    )(x, y)


def _lanes(rows, lo, width):
    lane = lax.broadcasted_iota(jnp.int32, (rows, LANES), 1)
    return jnp.logical_and(lane >= lo, lane < lo + width)


def _keep(mask, t):
    return jnp.where(mask, t, jnp.zeros_like(t))


def _mla_qcat(qn_ref, qr_ref, rope_lo, half, rows):
    qn = _keep(_lanes(rows, MLA_NOPE * half, MLA_NOPE), qn_ref[...])
    qr = _keep(_lanes(rows, rope_lo, MLA_ROPE), qr_ref[...])
    return jnp.concatenate([qn, qr], axis=1)


def _diag_mask(rows, width, row0, col0):
    row = lax.broadcasted_iota(jnp.int32, (rows, width), 0)
    col = lax.broadcasted_iota(jnp.int32, (rows, width), 1)
    return col + (col0 - row0) <= row


def _mla_fwd(qn, qr, kn, kr, v, riders=()):
    s = qn.shape[0]
    tq, tk = min(MLA_TQ, s), min(MLA_TK, s)
    td = tq
    ratio = tq // tk

    n_ride = len(riders)
    n_pairs = MLA_HEADS // 2

    def kern(qn_ref, qr_ref, kn_ref, kr_ref, v_ref, *rest):
        o_ref, lse_ref = rest[n_ride:n_ride + 2]
        g = pl.program_id(0)
        i = pl.program_id(1)
        if n_ride:
            send, forward, finish = _gather_steps(rest[:n_ride], rest[n_ride + 2:2 * n_ride + 2], *rest[2 * n_ride + 2:])
            pl.when(jnp.logical_and(g == 0, i == 0))(send)
            pl.when(jnp.logical_and(g == 1, i == 0))(forward)
        qcat = [_mla_qcat(qn_ref, qr_ref, MLA_ROPE * (2 * (g % 2) + half), half, tq) for half in range(2)]

        def block(k0, width, carry, row0, masked, half):
            m, l, acc = (c[row0:] for c in carry)
            ks = pl.ds(pl.multiple_of(k0, width), width)
            kcat = jnp.concatenate([kn_ref[ks, :], kr_ref[ks, :]], axis=1)
            sc = _dot_nt(qcat[half][row0:], kcat)
            if masked:
                sc = jnp.where(_diag_mask(tq - row0, width, row0, row0), sc, NEG)
            m_new = jnp.maximum(m, jnp.max(sc, axis=1, keepdims=True))
            p = jnp.exp2(sc - m_new)
            alpha = jnp.exp2(m - m_new)
            l = alpha * l + jnp.sum(p, axis=1, keepdims=True)
            acc = alpha * acc + _dot(p.astype(BF16), v_ref[ks, :])
            new = (m_new, l, acc)
            return new if row0 == 0 else tuple(jnp.concatenate([c[:row0], n], axis=0) for c, n in zip(carry, new))

        def both(k0, width, carries, row0, masked):
            return tuple(block(k0, width, carries[half], row0, masked, half) for half in range(2))

        init = (jnp.full((tq, 1), NEG, F32), jnp.zeros((tq, 1), F32), jnp.zeros((tq, LANES), F32))
        carries = lax.fori_loop(0, i * ratio, lambda kb, c: both(kb * tk, tk, c, 0, False), (init, init))
        for row0 in range(0, tq, td):
            carries = both(i * tq + row0, td, carries, row0, True)
        for half in range(2):
            m, l, acc = carries[half]
            out = _keep(_lanes(tq, MLA_V * half, MLA_V), acc / l)
            lse = _keep(_lanes(tq, MLA_ROPE * half, MLA_ROPE), jnp.broadcast_to(m + jnp.log2(l), (tq, LANES)))
            if half == 0:
                o_ref[...] = out
                lse_ref[...] = lse
            else:
                o_ref[...] += out
                lse_ref[...] += lse
        if n_ride:
            pl.when(jnp.logical_and(g == n_pairs - 1, i == s // tq - 1))(finish)

    qblk = pl.BlockSpec((tq, LANES), lambda g, i: (i, g))
    full = pl.BlockSpec((s, LANES), lambda g, i: (0, g))
    outs = pl.pallas_call(
        kern, name="mla_fwd", grid=(n_pairs, s // tq),
        in_specs=[qblk, pl.BlockSpec((tq, LANES), lambda g, i: (i, g // 2)), full, pl.BlockSpec((s, LANES), lambda g, i: (0, 0)), full]
        + [HBM_SPEC] * n_ride,
        out_specs=[qblk, qblk] + [HBM_SPEC] * n_ride,
        out_shape=[_sds((s, MLA_WIDTH), F32), _sds((s, n_pairs * LANES), F32)] + [_sds((N_SHARD,) + a.shape, a.dtype) for a in riders],
        scratch_shapes=_gather_sems(n_ride) if n_ride else [], compiler_params=_params(2),
    )(qn, qr, kn, kr, v, *riders)
    return outs[0], outs[1], outs[2:]


def _mla_bwd(qn, qr, kn, kr, v, o, do, lse, riders=()):
    s = qn.shape[0]
    tq, tk, td = min(MLA_TQ, s), min(MLA_BWD_TK, s), min(MLA_DIAG_TK, s)
    ratio = tq // tk

    n_ride = len(riders)
    n_pairs = MLA_HEADS // 2

    def kern(qn_ref, qr_ref, kn_ref, kr_ref, v_ref, o_ref, do_ref, lse_ref, *rest):
        dqn_ref, dqr_ref, dkn_out, dkr_out, dv_out = rest[n_ride:n_ride + 5]
        dkn_ref, dkr_ref, dv_ref = rest[2 * n_ride + 5:2 * n_ride + 8]
        g = pl.program_id(0)
        i = pl.program_id(1)
        if n_ride:
            start, finish = _scatter_steps(rest[:n_ride], rest[n_ride + 5:2 * n_ride + 5], *rest[2 * n_ride + 8:])
            pl.when(jnp.logical_and(g == 0, i == 0))(start)

        @pl.when(i == 0)
        def _():
            dkn_ref[...] = jnp.zeros_like(dkn_ref)
            dkr_ref[...] = jnp.zeros_like(dkr_ref)
            dv_ref[...] = jnp.zeros_like(dv_ref)

        for half in range(2):
            rope_lo = MLA_ROPE * (2 * (g % 2) + half)
            qcat = _mla_qcat(qn_ref, qr_ref, rope_lo, half, tq)
            mine = _lanes(tq, MLA_V * half, MLA_V)
            do_f = _keep(mine, do_ref[...])
            do_b = do_f.astype(BF16)
            delta = jnp.sum(do_f * o_ref[...], axis=1, keepdims=True)
            lse_v = lse_ref[:, MLA_ROPE * half:MLA_ROPE * half + 1]

            def block(k0, width, dq_acc, row0, masked, qcat=qcat, do_b=do_b, delta=delta, lse_v=lse_v):
                ks = pl.ds(pl.multiple_of(k0, width), width)
                kcat = jnp.concatenate([kn_ref[ks, :], kr_ref[ks, :]], axis=1)
                qc, dob = qcat[row0:], do_b[row0:]
                p = jnp.exp2(_dot_nt(qc, kcat) - lse_v[row0:])
                if masked:
                    p = jnp.where(_diag_mask(tq - row0, width, row0, row0), p, 0.0)
                ds = (p * (_dot_nt(dob, v_ref[ks, :]) - delta[row0:])).astype(BF16)
                dv_ref[ks, :] += _dot_tn(p.astype(BF16), dob)
                dkc = _dot_tn(ds, qc)
                dkn_ref[ks, :] += dkc[:, :LANES]
                dkr_ref[ks, :] += dkc[:, LANES:]
                new = dq_acc[row0:] + _dot(ds, kcat)
                return new if row0 == 0 else jnp.concatenate([dq_acc[:row0], new], axis=0)

            acc = lax.fori_loop(0, i * ratio, lambda kb, c, block=block: block(kb * tk, tk, c, 0, False),
                                jnp.zeros((tq, 2 * LANES), F32))
            for row0 in range(0, tq, td):
                acc = block(i * tq + row0, td, acc, row0, True)
            dqn = _keep(_lanes(tq, MLA_NOPE * half, MLA_NOPE), acc[:, :LANES] * MLA_SCALE)
            dqr = _keep(_lanes(tq, rope_lo, MLA_ROPE), acc[:, LANES:] * MLA_SCALE)
            if half == 0:
                dqn_ref[...] = dqn.astype(BF16)
                dqr_ref[...] = dqr.astype(BF16)
            else:
                dqn_ref[...] += dqn.astype(BF16)
                dqr_ref[...] += dqr.astype(BF16)

        @pl.when(i == s // tq - 1)
        def _():
            dkn_out[...] = (dkn_ref[...] * MLA_DK_SCALE).astype(BF16)
            dkr_out[...] = (dkr_ref[...] * MLA_DK_SCALE).astype(BF16)
            dv_out[...] = dv_ref[...].astype(BF16)

        if n_ride:
            pl.when(jnp.logical_and(g == n_pairs - 1, i == s // tq - 1))(finish)

    qblk = pl.BlockSpec((tq, LANES), lambda g, i: (i, g))
    full = pl.BlockSpec((s, LANES), lambda g, i: (0, g))
    once = lambda spec_map: pl.BlockSpec((s, LANES), spec_map, pipeline_mode=pl.Buffered(1))
    wide = _sds((s, n_pairs * LANES), BF16)
    outs = pl.pallas_call(
        kern, name="mla_bwd", grid=(n_pairs, s // tq),
        in_specs=[qblk, pl.BlockSpec((tq, LANES), lambda g, i: (i, g // 2)), once(lambda g, i: (0, g)), once(lambda g, i: (0, 0)),
                  once(lambda g, i: (0, g)), qblk, qblk, qblk] + [HBM_SPEC] * n_ride,
        out_specs=[qblk, qblk, full, full, full] + [HBM_SPEC] * n_ride,
        out_shape=[wide] * 5 + [_sds((N_SHARD - 1,) + p.shape[1:], p.dtype) for p in riders],
        scratch_shapes=[pltpu.VMEM((s, LANES), F32)] * 3 + (_scatter_sems(n_ride) if n_ride else []), compiler_params=_params(2),
    )(qn, qr, kn, kr, v, o, do, lse, *riders)
    return outs[:5], outs[5:]


def _sb_masks(tk):
    j = lax.broadcasted_iota(jnp.int32, (tk, tk), 0)
    c = lax.broadcasted_iota(jnp.int32, (tk, tk), 1)
    return (j > c).astype(BF16), (j < c).astype(BF16)


def _sb_scores(qs, kk, msuf, strict):
    z = _dot_nt(qs, kk)
    lom = -(jnp.maximum(z, 0.0) + jnp.log(1.0 + jnp.exp(-jnp.abs(z))))
    if strict is not None:
        lom = jnp.where(strict, lom, 0.0)
    return z, lom, _dot(lom.astype(BF16), msuf)


def _sb_strict(tq, tk, d):
    row = lax.broadcasted_iota(jnp.int32, (tq, tk), 0)
    col = lax.broadcasted_iota(jnp.int32, (tq, tk), 1)
    return col + d * tk < row


def _sb_fwd(q, k, v, msuf):
    s = q.shape[0]
    tq, tk = min(SB_TQ, s), min(SB_TK, s)
    ratio = tq // tk
    subs = 2 if s % (2 * tq) == 0 else 1
    chains = [(sub, half) for sub in range(subs) for half in range(2)]

    def kern(q_ref, k_ref, v_ref, m_ref, o_ref, c_ref):
        i = pl.program_id(1)
        msf = m_ref[...]
        lane = lax.broadcasted_iota(jnp.int32, (tq, LANES), 1)
        mine = [_lanes(tq, SB_DIM * half, SB_DIM) for half in range(2)]
        rows = [slice(tq * sub, tq * (sub + 1)) for sub in range(subs)]
        qs = {(sub, half): _keep(mine[half], q_ref[rows[sub], :]) * 0.125 for sub, half in chains}
        tile = [subs * i + sub for sub in range(subs)]

        def block(kb, carry, dd, chain, valid=None):
            c, acc, cm = carry
            ks = pl.ds(pl.multiple_of(jnp.maximum(kb, 0) * tk, tk), tk)
            strict = None if dd is None else _sb_strict(tq, tk, dd)
            z, lom, suf = _sb_scores(qs[chain], k_ref[ks, :], msf, strict)
            c_in = c if valid is None else jnp.where(valid, c, NEG)
            a = jnp.exp(z + lom + (suf + c_in))
            if strict is not None:
                a = jnp.where(strict, a, 0.0)
            acc = acc + _dot(a.astype(BF16), v_ref[ks, :])
            cm = jnp.where(lane == kb, c, cm)
            return c + jnp.sum(lom, axis=1, keepdims=True), acc, cm

        init = (jnp.zeros((tq, 1), F32), jnp.zeros((tq, LANES), F32), jnp.full((tq, LANES), NEG, F32))
        carries = {chain: init for chain in chains}
        for dd in range(ratio - 1, -1, -1):
            carries = {chain: block(tile[chain[0]] * ratio + dd, carries[chain], dd, chain) for chain in chains}

        def left(sub, t):
            return tile[sub] * ratio - 1 - t

        def live(st):
            t, cs = st
            alive = [jnp.logical_and(left(sub, t) >= 0, jnp.max(cs[chains.index((sub, half))][0]) > -SB_SKIP) for sub, half in chains]
            return functools.reduce(jnp.logical_or, alive)

        def step(st):
            t, cs = st
            return t + 1, tuple(block(left(sub, t), cs[n], None, (sub, half), left(sub, t) >= 0) for n, (sub, half) in enumerate(chains))

        _, done = lax.while_loop(live, step, (0, tuple(carries[chain] for chain in chains)))
        for sub in range(subs):
            d0, d1 = done[chains.index((sub, 0))], done[chains.index((sub, 1))]
            o_ref[rows[sub], :] = _keep(mine[0], d0[1]) + _keep(mine[1], d1[1])
            c_ref[rows[sub], :LANES] = d0[2]
            c_ref[rows[sub], LANES:] = d1[2]

    qblk = lambda n: pl.BlockSpec((subs * tq, n), lambda g, i: (i, g))
    full = pl.BlockSpec((s, LANES), lambda g, i: (0, g))
    return pl.pallas_call(
        kern, name="sb_fwd", grid=(SB_HEADS // 2, s // (subs * tq)),
        in_specs=[qblk(LANES), full, full, pl.BlockSpec((tk, tk), lambda g, i: (0, 0))],
        out_specs=[qblk(LANES), qblk(2 * LANES)],
        out_shape=[_sds((s, SB_WIDTH), F32), _sds((s, SB_HEADS * LANES), F32)], compiler_params=_params(2),
    )(q, k, v, msuf)


def _sb_bwd(q, k, v, do, cmat, msuf, mpre, riders=()):
    s = q.shape[0]
    tq, tk = min(SB_TQ, s), min(SB_TK, s)
    ratio = tq // tk
    subs = 2 if s % (2 * tq) == 0 else 1
    chains = [(sub, half) for sub in range(subs) for half in range(2)]
    n_steps = s // (subs * tq)
    n_ride = len(riders)
    n_pairs = SB_HEADS // 2

    def kern(q_ref, k_ref, v_ref, do_ref, c_ref, ms_ref, mp_ref, *rest):
        dq_ref, dk_out, dv_out = rest[n_ride:n_ride + 3]
        dk_ref, dv_ref = rest[2 * n_ride + 3:2 * n_ride + 5]
        i = pl.program_id(1)
        if n_ride:
            start, finish = _swap_steps(rest[:n_ride], rest[n_ride + 3:2 * n_ride + 3], *rest[2 * n_ride + 5:])
            pl.when(jnp.logical_and(pl.program_id(0) == 0, i == 0))(start)

        @pl.when(i == 0)
        def _():
            dk_ref[...] = jnp.zeros_like(dk_ref)
            dv_ref[...] = jnp.zeros_like(dv_ref)

        msf = ms_ref[...]
        mpf = mp_ref[...]
        lane = lax.broadcasted_iota(jnp.int32, (tq, LANES), 1)
        lane1 = lax.broadcasted_iota(jnp.int32, (1, LANES), 1)
        mine = [_lanes(tq, SB_DIM * half, SB_DIM) for half in range(2)]
        rows = [slice(tq * sub, tq * (sub + 1)) for sub in range(subs)]
        qv = {(sub, half): _keep(mine[half], q_ref[rows[sub], :]) for sub, half in chains}
        qs = {chain: t * 0.125 for chain, t in qv.items()}
        do_b = {(sub, half): _keep(mine[half], do_ref[rows[sub], :]).astype(BF16) for sub, half in chains}
        cm = {(sub, half): c_ref[rows[sub], LANES * half:LANES * (half + 1)] for sub, half in chains}
        n_left = [(subs * i + sub) * ratio for sub in range(subs)]

        def block(kb, carry, dd, chain, valid=None):
            dq_acc, pc = carry
            ks = pl.ds(pl.multiple_of(jnp.maximum(kb, 0) * tk, tk), tk)
            kk = k_ref[ks, :]
            strict = None if dd is None else _sb_strict(tq, tk, dd)
            z, lom, suf = _sb_scores(qs[chain], kk, msf, strict)
            c = jnp.sum(jnp.where(lane == kb, cm[chain], 0.0), axis=1, keepdims=True)
            if valid is not None:
                c = jnp.where(valid, c, NEG)
            a = jnp.exp(z + lom + (suf + c))
            if strict is not None:
                a = jnp.where(strict, a, 0.0)
            g = _dot_nt(do_b[chain], v_ref[ks, :]) * a
            p = pc + _dot(g.astype(BF16), mpf)
            omb = jnp.exp(lom)
            dz = (g * omb - (1.0 - omb) * p) * 0.125
            if strict is not None:
                dz = jnp.where(strict, dz, 0.0)
            dz = dz.astype(BF16)
            dv_ref[ks, :] += _dot_tn(a.astype(BF16), do_b[chain])
            dk_ref[ks, :] += _dot_tn(dz, qv[chain])
            return dq_acc + _dot(dz, kk), pc + jnp.sum(g, axis=1, keepdims=True)

        def needed(chain):
            seen = jnp.logical_and(jnp.max(cm[chain], axis=0, keepdims=True) > -SB_SKIP, lane1 < n_left[chain[0]])
            return jnp.sum(seen.astype(jnp.int32))

        depth = functools.reduce(jnp.maximum, [needed(chain) for chain in chains])

        def step(t, cs):
            back = depth - t
            return tuple(block(n_left[sub] - back, cs[n], None, (sub, half), n_left[sub] - back >= 0)
                         for n, (sub, half) in enumerate(chains))

        init = (jnp.zeros((tq, LANES), F32), jnp.zeros((tq, 1), F32))
        carries = list(lax.fori_loop(0, depth, step, tuple(init for _ in chains)))
        for dd in range(ratio):
            carries = [block(n_left[sub] + dd, carries[n], dd, (sub, half)) for n, (sub, half) in enumerate(chains)]
        for sub in range(subs):
            d0, d1 = carries[chains.index((sub, 0))], carries[chains.index((sub, 1))]
            dq_ref[rows[sub], :] = (_keep(mine[0], d0[0]) + _keep(mine[1], d1[0])).astype(BF16)

        @pl.when(i == n_steps - 1)
        def _():
            dk_out[...] = dk_ref[...].astype(BF16)
            dv_out[...] = dv_ref[...].astype(BF16)

        if n_ride:
            pl.when(jnp.logical_and(pl.program_id(0) == n_pairs - 1, i == n_steps - 1))(finish)

    qblk = lambda n: pl.BlockSpec((subs * tq, n), lambda g, i: (i, g))
    full = pl.BlockSpec((s, LANES), lambda g, i: (0, g))
    msk = pl.BlockSpec((tk, tk), lambda g, i: (0, 0))
    outs = pl.pallas_call(
        kern, name="sb_bwd", grid=(n_pairs, n_steps),
        in_specs=[qblk(LANES), full, full, qblk(LANES), qblk(2 * LANES), msk, msk] + [HBM_SPEC] * n_ride,
        out_specs=[qblk(LANES), full, full] + [HBM_SPEC] * n_ride,
        out_shape=[_sds((s, SB_WIDTH), BF16)] * 3 + _halves_shapes(riders),
        scratch_shapes=[pltpu.VMEM((s, LANES), F32)] * 2 + (_swap_sems(n_ride) if n_ride else []), compiler_params=_params(2),
    )(q, k, v, do, cmat, msuf, mpre, *riders)
    return outs[:3], outs[3:]


def _place():
    return lax.axis_index("x"), lax.axis_index("y"), lax.axis_index("c")


def _other_chips(x, y):
    return [(1 - x, y), (x, 1 - y), (1 - x, 1 - y)]


HBM_SPEC = pl.BlockSpec(memory_space=pl.ANY)


def _gather_steps(ins, outs, send_sems, recv_sems):
    n = len(ins)
    x, y, c = _place()
    sibling = (x, y, 1 - c)
    chips = _other_chips(x, y)

    def half_of(a, ref, pc):
        half = ins[a].shape[0] // 2
        return ref.at[pl.ds(pl.multiple_of(pc * half, 16), half), :]

    def copy(a, k, chip, pc, to, src=None):
        dst = half_of(a, outs[a].at[2 * chip[0] + chip[1]], pc)
        return pltpu.make_async_remote_copy(src_ref=dst if src is None else src, dst_ref=dst, send_sem=send_sems.at[7 * a + k],
                                            recv_sem=recv_sems.at[7 * a + k], device_id=to, device_id_type=MESH)

    def own(a):
        return pltpu.make_async_remote_copy(src_ref=ins[a], dst_ref=outs[a].at[2 * x + y], send_sem=send_sems.at[7 * a + 6],
                                            recv_sem=recv_sems.at[7 * a + 6], device_id=sibling, device_id_type=MESH)

    def first():
        far = [copy(a, j, (x, y), c, (*chip, c), src=half_of(a, ins[a], c)) for a in range(n) for j, chip in enumerate(chips)]
        return far + [own(a) for a in range(n)]

    def passed():
        return [copy(a, 3 + j, chip, c, sibling) for j, chip in enumerate(chips) for a in range(n)]

    def send():
        for cp in first():
            cp.start()

    def forward():
        for j, chip in enumerate(chips):
            for a in range(n):
                copy(a, j, chip, c, sibling).wait_recv()
        for cp in passed():
            cp.start()

    def finish():
        for j, chip in enumerate(chips):
            for a in range(n):
                copy(a, 3 + j, chip, 1 - c, sibling).wait_recv()
        for a in range(n):
            own(a).wait_recv()
        for cp in first() + passed():
            cp.wait_send()

    return send, forward, finish


def _gather_sems(n):
    return [pltpu.SemaphoreType.DMA((7 * n,)), pltpu.SemaphoreType.DMA((7 * n,))]


def _allgather_list(name, shards):
    n = len(shards)

    def body(*refs):
        for stage in _gather_steps(refs[:n], refs[n:2 * n], *refs[2 * n:]):
            stage()

    return pl.pallas_call(
        body, name=name, out_shape=[_sds((N_SHARD,) + a.shape, a.dtype) for a in shards], in_specs=[HBM_SPEC] * n,
        out_specs=[HBM_SPEC] * n, scratch_shapes=_gather_sems(n),
    )(*shards)


def _swap_steps(ins, outs, send_sems, recv_sems):
    x, y, c = _place()

    def copies():
        out = []
        for a in range(len(ins)):
            h = ins[a].shape[1] // 2
            src = ins[a].at[:, pl.ds(pl.multiple_of((1 - c) * h, 8), h), :]
            out.append(pltpu.make_async_remote_copy(src_ref=src, dst_ref=outs[a], send_sem=send_sems.at[a], recv_sem=recv_sems.at[a],
                                                    device_id=(x, y, 1 - c), device_id_type=MESH))
        return out

    def start():
        for cp in copies():
            cp.start()

    def finish():
        for cp in copies():
            cp.wait()

    return start, finish


def _swap_sems(n):
    return [pltpu.SemaphoreType.DMA((n,)), pltpu.SemaphoreType.DMA((n,))]


def _halves_shapes(gs):
    return [_sds((N_SHARD, g.shape[1] // 2, g.shape[2]), g.dtype) for g in gs]


def _swap_halves(name, gs):
    n = len(gs)

    def body(*refs):
        for stage in _swap_steps(refs[:n], refs[n:2 * n], *refs[2 * n:]):
            stage()

    return pl.pallas_call(body, name=name, out_shape=_halves_shapes(gs), in_specs=[HBM_SPEC] * n, out_specs=[HBM_SPEC] * n,
                          scratch_shapes=_swap_sems(n))(*gs)


def _add_sibling(name, gs, gots, c_idx):
    n = len(gs)

    def kern(c_ref, *refs):
        for a in range(n):
            tot = refs[a][...] + refs[n + a][...]
            refs[2 * n + a][...] = tot
            refs[3 * n + a][...] = tot.astype(BF16)

    quarter = lambda g: (None, g.shape[1] // 4, g.shape[2])
    in_specs = [pl.BlockSpec(quarter(g), lambda b, s, c_ref: (b, 2 * c_ref[0] + s, 0)) for g in gs]
    in_specs += [pl.BlockSpec(quarter(g), lambda b, s, c_ref: (b, s, 0)) for g in gs]
    out_specs = [pl.BlockSpec(quarter(g), lambda b, s, c_ref: (b, s, 0)) for g in gs] * 2
    out_shape = [_sds(t.shape, F32) for t in gots] + [_sds(t.shape, BF16) for t in gots]
    outs = pl.pallas_call(
        kern, name=name, out_shape=out_shape,
        grid_spec=pltpu.PrefetchScalarGridSpec(num_scalar_prefetch=1, grid=(N_SHARD, 2), in_specs=in_specs, out_specs=out_specs),
        compiler_params=_params(2),
    )(c_idx.reshape(1), *gs, *gots)
    return outs[:n], outs[n:]


def _scatter_steps(ins, outs, send_sems, recv_sems):
    x, y, c = _place()

    def copies():
        return [pltpu.make_async_remote_copy(
            src_ref=ins[a].at[2 * px + py], dst_ref=outs[a].at[j], send_sem=send_sems.at[3 * a + j], recv_sem=recv_sems.at[3 * a + j],
            device_id=(px, py, c), device_id_type=MESH) for a in range(len(ins)) for j, (px, py) in enumerate(_other_chips(x, y))]

    def start():
        for cp in copies():
            cp.start()

    def finish():
        for cp in copies():
            cp.wait()

    return start, finish


def _scatter_sems(n):
    return [pltpu.SemaphoreType.DMA((3 * n,)), pltpu.SemaphoreType.DMA((3 * n,))]


def _chip_scatter(ps):
    n = len(ps)

    def body(*refs):
        for stage in _scatter_steps(refs[:n], refs[n:2 * n], *refs[2 * n:]):
            stage()

    return pl.pallas_call(
        body, name="chip_scatter", out_shape=[_sds((N_SHARD - 1,) + p.shape[1:], p.dtype) for p in ps], in_specs=[HBM_SPEC] * n,
        out_specs=[HBM_SPEC] * n, scratch_shapes=_scatter_sems(n),
    )(*ps)


def _add_chips(name, ps, others, shard_idx):
    n = len(ps)

    def kern(b_ref, *refs):
        for a in range(n):
            tot = refs[a][...]
            for j in range(N_SHARD - 1):
                tot = tot + refs[n + a][j].astype(F32)
            refs[2 * n + a][...] = tot

    in_specs = [pl.BlockSpec((None, p.shape[1] // 2, p.shape[2]), lambda s, b_ref: (b_ref[0], s, 0)) for p in ps]
    in_specs += [pl.BlockSpec((N_SHARD - 1, p.shape[1] // 2, p.shape[2]), lambda s, b_ref: (0, s, 0)) for p in ps]
    out_specs = [pl.BlockSpec((p.shape[1] // 2, p.shape[2]), lambda s, b_ref: (s, 0)) for p in ps]
    return pl.pallas_call(
        kern, name=name, out_shape=[_sds(p.shape[1:], F32) for p in ps],
        grid_spec=pltpu.PrefetchScalarGridSpec(num_scalar_prefetch=1, grid=(2,), in_specs=in_specs, out_specs=out_specs),
        compiler_params=_params(1),
    )(shard_idx.reshape(1), *ps, *others)


def _result_steps(ins, outs, send_sems, recv_sems):
    x, y, c = _place()

    def copies():
        return [pltpu.make_async_remote_copy(src_ref=ins[a], dst_ref=outs[a], send_sem=send_sems.at[a], recv_sem=recv_sems.at[a],
                                             device_id=(x, y, 1 - c), device_id_type=MESH) for a in range(len(ins))]

    def start():
        for cp in copies():
            cp.start()

    def finish():
        for cp in copies():
            cp.wait()

    return start, finish


def _swap_result(name, mines):
    n = len(mines)

    def body(*refs):
        for stage in _result_steps(refs[:n], refs[n:2 * n], *refs[2 * n:]):
            stage()

    return pl.pallas_call(
        body, name=name, out_shape=[_sds(m.shape, m.dtype) for m in mines], in_specs=[HBM_SPEC] * n,
        out_specs=[HBM_SPEC] * n, scratch_shapes=_swap_sems(n),
    )(*mines)


def _allreduce_small(v):
    m_per, n = v.shape

    def body(x_ref, tot_ref, all_ref, send_sems, recv_sems, local_sem):
        x, y, c = _place()
        me, sibling = (x, y, c), (x, y, 1 - c)
        chips = _other_chips(x, y)

        def rows(px, py, pc):
            return all_ref.at[pl.ds(pl.multiple_of((4 * px + 2 * py + pc) * m_per, 8), m_per), :]

        def copy(k, block, to, src=None):
            return pltpu.make_async_remote_copy(
                src_ref=rows(*block) if src is None else src, dst_ref=rows(*block), send_sem=send_sems.at[k],
                recv_sem=recv_sems.at[k], device_id=to, device_id_type=MESH)

        mine = pltpu.make_async_copy(x_ref, rows(*me), local_sem)
        mine.start()
        first = [copy(0, me, sibling, src=x_ref)] + [copy(1 + j, me, (*chip, c), src=x_ref) for j, chip in enumerate(chips)]
        for cp in first:
            cp.start()
        passed = [copy(4 + j, (*chip, c), sibling) for j, chip in enumerate(chips)]
        for j, chip in enumerate(chips):
            copy(1 + j, (*chip, c), me).wait_recv()
            passed[j].start()
        copy(0, sibling, me).wait_recv()
        for j, chip in enumerate(chips):
            copy(4 + j, (*chip, 1 - c), me).wait_recv()
        for cp in first + passed:
            cp.wait_send()
        mine.wait()
        tot = all_ref[0:m_per, :]
        for dev in range(1, 8):
            tot = tot + all_ref[dev * m_per:(dev + 1) * m_per, :]
        tot_ref[...] = tot

    vmem = pl.BlockSpec(memory_space=pltpu.VMEM)
    return pl.pallas_call(
        body, name="allreduce_small", out_shape=_sds((m_per, n), F32), in_specs=[vmem], out_specs=vmem,
        scratch_shapes=[pltpu.VMEM((8 * m_per, n), F32), pltpu.SemaphoreType.DMA((7,)), pltpu.SemaphoreType.DMA((7,)),
                        pltpu.SemaphoreType.DMA],
    )(v)


def _adam_update(w, g, m, v):
    m_new = ADAM_B1 * m + (1.0 - ADAM_B1) * g
    v_new = ADAM_B2 * v + (1.0 - ADAM_B2) * (g * g)
    m_hat = m_new / (1.0 - ADAM_B1 ** ADAM_STEP)
    v_hat = v_new / (1.0 - ADAM_B2 ** ADAM_STEP)
    return -ADAM_LR * (m_hat / (jnp.sqrt(v_hat) + ADAM_EPS) + ADAM_WD * w), m_new, v_new


def _adamw(name, w, g, m, v):
    rows, width = w.shape
    tr = rows // 4 if rows % 32 == 0 else rows

    def kern(w_ref, g_ref, m_ref, v_ref, d_ref, mo_ref, vo_ref):
        d_ref[...], mo_ref[...], vo_ref[...] = _adam_update(w_ref[...], g_ref[...], m_ref[...], v_ref[...])

    spec = pl.BlockSpec((tr, width), lambda i: (i, 0))
    return pl.pallas_call(kern, name=name, grid=(rows // tr,), in_specs=[spec] * 4, out_specs=[spec] * 3,
                          out_shape=[_sds((rows, width), F32)] * 3, compiler_params=_params(1))(w, g, m, v)


def _adamw_halves(ws, mines, theirs, ms, vs, c_idx):
    n = len(ws)

    def kern(c_ref, *refs):
        take_mine = pl.program_id(0) == c_ref[0]
        for a in range(n):
            w_ref, mine_ref, theirs_ref, m_ref, v_ref = refs[5 * a:5 * a + 5]
            g_ref, d_ref, mo_ref, vo_ref = refs[5 * n + 4 * a:5 * n + 4 * a + 4]
            g = jnp.where(take_mine, mine_ref[...], theirs_ref[...])
            g_ref[...] = g
            d_ref[...], mo_ref[...], vo_ref[...] = _adam_update(w_ref[...], g, m_ref[...], v_ref[...])

    in_specs, out_specs, out_shape = [], [], []
    for w in ws:
        rows, width = w.shape
        tr = rows // (2 * ADAM_STEPS)
        whole = pl.BlockSpec((tr, width), lambda h, j, c_ref: (ADAM_STEPS * h + j, 0))
        part = pl.BlockSpec((tr, width), lambda h, j, c_ref: (j, 0))
        in_specs += [whole, part, part, whole, whole]
        out_specs += [whole] * 4
        out_shape += [_sds((rows, width), F32)] * 4
    operands = [t for group in zip(ws, mines, theirs, ms, vs) for t in group]
    outs = pl.pallas_call(
        kern, name="adamw_shards", out_shape=out_shape,
        grid_spec=pltpu.PrefetchScalarGridSpec(num_scalar_prefetch=1, grid=(2, ADAM_STEPS), in_specs=in_specs, out_specs=out_specs),
        compiler_params=_params(2),
    )(c_idx.reshape(1), *operands)
    return [outs[4 * a:4 * a + 4] for a in range(n)]


SHARDED = (("w_in", D_MODEL, IN_WIDTH, 1), ("w_uq", Q_RANK, MLA_HEADS * MLA_QK, 1),
           ("w_ukv", KV_RANK, MLA_HEADS * (MLA_NOPE + MLA_V), 1), ("w_o", D_MODEL, D_MODEL, 0),
           ("w_gate", D_MODEL, D_FF, 1), ("w_up", D_MODEL, D_FF, 1), ("w_down", D_FF, D_MODEL, 0))
EARLY = ("w_in", "w_uq", "w_ukv")
LATE = ("w_o", "w_gate", "w_up", "w_down")
FLIPPED = ("w_gate", "w_up")
SMALL = (("norm_mix", D_MODEL), ("q_latent_norm", Q_RANK), ("kv_latent_norm", KV_RANK), ("out_norm_mla", MLA_WIDTH),
         ("out_norm_sb", SB_WIDTH), ("norm_ffn", D_MODEL), ("norm_final", D_MODEL))


def _full_weight(gathered, axis):
    n_sh, k, n = gathered.shape
    return gathered.transpose(1, 0, 2).reshape(k, n_sh * n) if axis == 1 else gathered.reshape(n_sh * k, n)


def _shard_major(g, axis):
    r, c = g.shape
    return g.reshape(r, N_SHARD, c // N_SHARD).transpose(1, 0, 2) if axis == 1 else g.reshape(N_SHARD, r // N_SHARD, c)


def _rot_cols(w):
    hh = MLA_ROPE // 2
    return jnp.concatenate([-w[..., hh:], w[..., :hh]], axis=-1)


def _rot_cols_t(g):
    hh = MLA_ROPE // 2
    return jnp.concatenate([g[..., hh:], -g[..., :hh]], axis=-1)


def _attention_weights(full, small):
    w_in = full["w_in"]
    s0, s1, s2 = Q_RANK, Q_RANK + KV_RANK, Q_RANK + KV_RANK + MLA_ROPE
    uq = full["w_uq"].reshape(Q_RANK, MLA_HEADS, MLA_QK)
    ukv = full["w_ukv"].reshape(KV_RANK, MLA_HEADS, MLA_NOPE + MLA_V)
    w_kr = w_in[:, s1:s2]
    per_tile = ROPE_TILE // MLA_ROPE
    w = {
        "w_cq": w_in[:, :s0], "w_ckv": w_in[:, s0:s1],
        "w_kr4": jnp.tile(w_kr, (1, per_tile)), "w_kr4r": jnp.tile(_rot_cols(w_kr), (1, per_tile)),
        "w_kr8": jnp.tile(w_kr, (1, MLA_HEADS)), "w_kr8r": jnp.tile(_rot_cols(w_kr), (1, MLA_HEADS)),
        "w_sbq": w_in[:, s2:s2 + SB_WIDTH], "w_sbk": w_in[:, s2 + SB_WIDTH:s2 + 2 * SB_WIDTH], "w_sbv": w_in[:, s2 + 2 * SB_WIDTH:],
        "w_qn": uq[..., :MLA_NOPE].reshape(Q_RANK, -1), "w_qr": uq[..., MLA_NOPE:].reshape(Q_RANK, -1),
        "w_qrr": _rot_cols(uq[..., MLA_NOPE:]).reshape(Q_RANK, -1),
        "w_kn": ukv[..., :MLA_NOPE].reshape(KV_RANK, -1), "w_v": ukv[..., MLA_NOPE:].reshape(KV_RANK, -1),
    }
    w.update(g_mix=small["norm_mix"], g_q=small["q_latent_norm"], g_kv=small["kv_latent_norm"], g_a=small["out_norm_mla"],
             g_b=small["out_norm_sb"], g_f=small["norm_ffn"], g_n=small["norm_final"])
    return w


def _ffn_weights(full):
    w = {"w_oa": full["w_o"][:MLA_WIDTH], "w_ob": full["w_o"][MLA_WIDTH:], "w_down": full["w_down"]}
    w.update({name + "_t": full[name] for name in FLIPPED})
    return w


def _rope_tables(positions):
    inv_freq = ROPE_THETA ** (-jnp.arange(0, MLA_ROPE, 2, dtype=F32) / MLA_ROPE)
    ang = positions.astype(F32)[:, None] * inv_freq[None, :]
    cos, sin = jnp.cos(ang), jnp.sin(ang)
    return {"cos": jnp.tile(jnp.concatenate([cos, cos], axis=1), (1, MLA_HEADS)),
            "sin": jnp.tile(jnp.concatenate([sin, sin], axis=1), (1, MLA_HEADS))}


def _by_head(g_wide, g_narrow, wide, narrow):
    r = g_wide.shape[0]
    return jnp.concatenate([g_wide.reshape(r, MLA_HEADS, wide), g_narrow.reshape(r, MLA_HEADS, narrow)], axis=-1).reshape(r, -1)


def kernel(x, positions, norm_mix, w_in, q_latent_norm, w_uq, kv_latent_norm, w_ukv, out_norm_mla, out_norm_sb, w_o, norm_ffn, w_gate, w_up, w_down, norm_final, loss_target, m_norm_mix, m_w_in, m_q_latent_norm, m_w_uq, m_kv_latent_norm, m_w_ukv, m_out_norm_mla, m_out_norm_sb, m_w_o, m_norm_ffn, m_w_gate, m_w_up, m_w_down, m_norm_final, v_norm_mix, v_w_in, v_q_latent_norm, v_w_uq, v_kv_latent_norm, v_w_ukv, v_out_norm_mla, v_out_norm_sb, v_w_o, v_norm_ffn, v_w_gate, v_w_up, v_w_down, v_norm_final):
    given = dict(norm_mix=norm_mix, w_in=w_in, q_latent_norm=q_latent_norm, w_uq=w_uq, kv_latent_norm=kv_latent_norm, w_ukv=w_ukv,
                 out_norm_mla=out_norm_mla, out_norm_sb=out_norm_sb, w_o=w_o, norm_ffn=norm_ffn, w_gate=w_gate, w_up=w_up,
                 w_down=w_down, norm_final=norm_final)
    mom_m = dict(norm_mix=m_norm_mix, w_in=m_w_in, q_latent_norm=m_q_latent_norm, w_uq=m_w_uq, kv_latent_norm=m_kv_latent_norm,
                 w_ukv=m_w_ukv, out_norm_mla=m_out_norm_mla, out_norm_sb=m_out_norm_sb, w_o=m_w_o, norm_ffn=m_norm_ffn,
                 w_gate=m_w_gate, w_up=m_w_up, w_down=m_w_down, norm_final=m_norm_final)
    mom_v = dict(norm_mix=v_norm_mix, w_in=v_w_in, q_latent_norm=v_q_latent_norm, w_uq=v_w_uq, kv_latent_norm=v_kv_latent_norm,
                 w_ukv=v_w_ukv, out_norm_mla=v_out_norm_mla, out_norm_sb=v_out_norm_sb, w_o=v_w_o, norm_ffn=v_norm_ffn,
                 w_gate=v_w_gate, w_up=v_w_up, w_down=v_w_down, norm_final=v_norm_final)
    xs = x[0]
    tgt = loss_target[0]
    s = xs.shape[0]
    c_idx = lax.axis_index("c")
    shard_idx = 2 * lax.axis_index("x") + lax.axis_index("y")

    def block2d(t, name):
        t = t.reshape(t.shape[-2:])
        return t.T if name in FLIPPED else t

    shard2d = {name: block2d(given[name], name) for name, *_ in SHARDED}
    local = {name: shard2d[name].astype(BF16) for name, *_ in SHARDED}
    axis_of = {name: 0 if name in FLIPPED else axis for name, _, _, axis in SHARDED}

    def whole(names, gathered):
        return {name: _full_weight(t, axis_of[name]) for name, t in zip(names, gathered)}

    small = {name: given[name].reshape(1, n) for name, n in SMALL}
    w = _attention_weights(whole(EARLY, _allgather_list("allgather_w", [local[name] for name in EARLY])), small)
    tabs = _rope_tables(positions[0])
    msuf, mpre = _sb_masks(min(SB_TK, s))

    u, cq, ckv, cqn, ckvn, qn, qr, kn, vm, kr, sq, sk, sv = _fwd_a(xs, tabs, w)
    o_mla, lse, late = _mla_fwd(qn, qr, kn, kr, vm, [local[name] for name in LATE])
    w.update(_ffn_weights(whole(LATE, late)))
    o_sb, cmat = _sb_fwd(sq, sk, sv, msuf)
    merged, h1, f, gate, up, act = _fwd_b1(xs, o_mla, o_sb, w)
    dh2, loss_part, dg_n = _fwd_b2(h1, act, tgt, w)

    def shards_of(names, grads):
        return [_shard_major(grads[name], axis_of[name]) for name in names]

    def reduced(tag, chip_f32, others):
        mine = _add_chips("add_chips_" + tag, chip_f32, others, shard_idx)
        return tuple(mine), tuple(_swap_result("swap_result_" + tag, mine))

    dgate, dup, dh1, do_mla, do_sb, dg_f, dg_a, dg_b = _bwd_b(dh2, gate, up, h1, o_mla, o_sb, w)
    late_gs = shards_of(LATE, {
        "w_o": _tn_matmul("dw_o", merged, dh1), "w_gate": _tn_matmul("dw_gate", dgate, f),
        "w_up": _tn_matmul("dw_up", dup, f), "w_down": _tn_matmul("dw_down", act, dh2)})
    (dsq, dsk, dsv), late_got = _sb_bwd(sq, sk, sv, do_sb, cmat, msuf, mpre, late_gs)
    late_f32, late_bf16 = _add_sibling("add_sibling_late", late_gs, late_got, c_idx)
    (dqn, dqr, dkn, dkr, dvm), late_others = _mla_bwd(qn, qr, kn, kr, vm, o_mla, do_mla, lse, late_bf16)
    mine_late = tuple(_add_chips("add_chips_late", late_f32, late_others, shard_idx))
    dx, a1, a2, dcq, dckv, dkrc, dkrs, dg_q, dg_kv, dg_mix = _bwd_a(xs, dh1, cq, ckv, dqn, dqr, dkn, dvm, dkr, dsq, dsk, dsv, tabs, w)

    (g_cq, g_ckv, g_krc, g_krs, g_sq, g_sk, g_sv), theirs_late = _tn_multi("dw_in", u, [dcq, dckv, dkrc, dkrs, dsq, dsk, dsv], mine_late)
    theirs_late = tuple(theirs_late)
    g_qn, g_qr1, g_qr2 = _tn_multi("dw_uq", cqn, [dqn, a1, a2])
    g_kn, g_v = _tn_multi("dw_ukv", ckvn, [dkn, dvm])
    slots = lambda g: g.reshape(g.shape[0], MLA_HEADS, MLA_ROPE)
    g_kr = jnp.sum(slots(g_krc), axis=1) + _rot_cols_t(jnp.sum(slots(g_krs), axis=1))
    g_qr = (slots(g_qr1) + _rot_cols_t(slots(g_qr2))).reshape(Q_RANK, -1)
    early_gs = shards_of(EARLY, {
        "w_in": jnp.concatenate([g_cq, g_ckv, g_kr, g_sq, g_sk, g_sv], axis=1),
        "w_uq": _by_head(g_qn, g_qr, MLA_NOPE, MLA_ROPE),
        "w_ukv": _by_head(g_kn, g_v, MLA_NOPE, MLA_V)})
    early_f32, early_bf16 = _add_sibling("add_sibling_early", early_gs, _swap_halves("swap_halves_early", early_gs), c_idx)
    mine_early, theirs_early = reduced("early", early_f32, _chip_scatter(early_bf16))
    halves = dict(zip(EARLY + LATE, zip(mine_early + mine_late, theirs_early + theirs_late)))

    small_parts = jnp.concatenate([dg_mix, dg_q, dg_kv, dg_a, dg_b, dg_f, dg_n, loss_part], axis=1)
    small_sum = _allreduce_small(jnp.broadcast_to(small_parts, (8, small_parts.shape[1])))
    small_g, loss = small_sum[0:1, :-LANES], small_sum[0, -LANES]

    g_out, d_out, m_out, v_out = {}, {}, {}, {}
    names = [name for name, *_ in SHARDED]
    updated = _adamw_halves([shard2d[name] for name in names], [halves[name][0] for name in names], [halves[name][1] for name in names],
                            [block2d(mom_m[name], name) for name in names], [block2d(mom_v[name], name) for name in names], c_idx)
    for name, outs in zip(names, updated):
        shape = given[name].shape
        g_out[name], d_out[name], m_out[name], v_out[name] = ((t.T if name in FLIPPED else t).reshape(shape) for t in outs)
    cat = lambda src: jnp.concatenate([src[name].reshape(1, n) for name, n in SMALL], axis=1)
    d, mn, vn = _adamw("adamw_small", cat(given), small_g, cat(mom_m), cat(mom_v))
    off = 0
    for name, n in SMALL:
        shape = given[name].shape
        g_out[name], d_out[name], m_out[name], v_out[name] = (t[:, off:off + n].reshape(shape) for t in (small_g, d, mn, vn))
        off += n

    order = ["norm_mix", "w_in", "q_latent_norm", "w_uq", "kv_latent_norm", "w_ukv", "out_norm_mla", "out_norm_sb", "w_o",
             "norm_ffn", "w_gate", "w_up", "w_down", "norm_final"]
    return (loss, dx[None], *[g_out[n] for n in order], *[d_out[n] for n in order], *[m_out[n] for n in order],
            *[v_out[n] for n in order])
```
